```python
import math
import jax, jax.numpy as jnp
from jax import lax
import numpy as np

D_MODEL = 1024
BATCH = 32
SEQ = 2048
DEPTH = 4

N_MIXERS = 2
CONV_WIDTH = 3
N_HEADS = 16
N_KV_HEADS = 4
HEAD_DIM = D_MODEL // N_HEADS
GROUP = N_HEADS // N_KV_HEADS
WINDOW = 128
BLOCK = WINDOW
ROPE_THETA = 10000.0
D_FF = 2816
EPS = 1e-5
QKV_WIDTH = (N_HEADS + 2 * N_KV_HEADS) * HEAD_DIM
N_CONV_LAYERS = (DEPTH + 1) // 2
N_ATTN_LAYERS = DEPTH // 2

kernel_name = "hybrid_shortconv_swa_sink_convffn"


def rms_norm(x, g):
    xf = x.astype(jnp.float32)
    y = xf * lax.rsqrt(jnp.mean(xf * xf, axis=-1, keepdims=True) + EPS)
    return (y * g.astype(jnp.float32)).astype(x.dtype)


def causal_dwconv(x, w):
    c = x.shape[-1]
    return lax.conv_general_dilated(
        x, w.astype(x.dtype)[:, None, :], window_strides=(1,),
        padding=[(CONV_WIDTH - 1, 0)], dimension_numbers=("NWC", "WIO", "NWC"),
        feature_group_count=c)


def short_conv_mixer(h, w_in, w_conv, w_out):
    bcv = h @ w_in
    b_gate, c_gate, v = jnp.split(bcv, 3, axis=-1)
    y = b_gate * causal_dwconv(c_gate * v, w_conv)
    return y @ w_out


def rope(x, cos, sin):
    x1, x2 = jnp.split(x, 2, axis=-1)
    c = cos[None, :, None, :]
    s = sin[None, :, None, :]
    return jnp.concatenate([x1 * c - x2 * s, x2 * c + x1 * s], axis=-1)


def swa_sink_attention(h, w_qkv, b_qkv, sinks, w_o, b_o, cos, sin):
    bsz, seq, _ = h.shape
    nb = seq // BLOCK
    qkv = h @ w_qkv + b_qkv
    q_w, kv_w = N_HEADS * HEAD_DIM, N_KV_HEADS * HEAD_DIM
    q = qkv[..., :q_w].reshape(bsz, seq, N_HEADS, HEAD_DIM)
    k = qkv[..., q_w:q_w + kv_w].reshape(bsz, seq, N_KV_HEADS, HEAD_DIM)
    v = qkv[..., q_w + kv_w:].reshape(bsz, seq, N_KV_HEADS, HEAD_DIM)
    q = rope(q, cos, sin)
    k = rope(k, cos, sin)

    q = q.reshape(bsz, nb, BLOCK, N_KV_HEADS, GROUP, HEAD_DIM)
    pad = ((0, 0), (1, 0), (0, 0), (0, 0), (0, 0))
    kp = jnp.pad(k.reshape(bsz, nb, BLOCK, N_KV_HEADS, HEAD_DIM), pad)
    vp = jnp.pad(v.reshape(bsz, nb, BLOCK, N_KV_HEADS, HEAD_DIM), pad)
    k_band = jnp.concatenate([kp[:, :-1], kp[:, 1:]], axis=2)
    v_band = jnp.concatenate([vp[:, :-1], vp[:, 1:]], axis=2)

    scores = jnp.einsum("bnqkgd,bnskd->bnkgqs", q, k_band).astype(jnp.float32)
    scores = scores * (HEAD_DIM ** -0.5)

    blk = jnp.arange(nb)[:, None, None]
    qi = jnp.arange(BLOCK)[None, :, None]
    kj = jnp.arange(2 * BLOCK)[None, None, :]
    q_pos = blk * BLOCK + qi
    k_pos = (blk - 1) * BLOCK + kj
    valid = (k_pos <= q_pos) & (q_pos - k_pos < WINDOW) & (k_pos >= 0)
    scores = jnp.where(valid[None, :, None, None], scores, jnp.finfo(jnp.float32).min)

    sink = sinks.astype(jnp.float32).reshape(N_KV_HEADS, GROUP)[None, None, :, :, None, None]
    m = jnp.maximum(jnp.max(scores, axis=-1, keepdims=True), sink)
    p = jnp.exp(scores - m)
    denom = jnp.sum(p, axis=-1, keepdims=True) + jnp.exp(sink - m)
    probs = (p / denom).astype(v_band.dtype)

    o = jnp.einsum("bnkgqs,bnskd->bnqkgd", probs, v_band)
    o = o.reshape(bsz, seq, N_HEADS * HEAD_DIM)
    return o @ w_o + b_o


def conv_ffn(h, w_in, w_conv, w_down):
    gu = h @ w_in
    g, u = jnp.split(gu, 2, axis=-1)
    g = causal_dwconv(g, w_conv)
    return (jax.nn.silu(g) * u) @ w_down


def _fwd_setup_inputs(seed: int = 0) -> dict:
    key = jax.random.key(seed)
    ks = jax.random.split(key, 16)
    f32 = jnp.float32

    def w(k, shape, fan_in):
        return jax.random.normal(k, shape, f32) * (fan_in ** -0.5)

    return {
        "x": jax.random.normal(ks[0], (BATCH, SEQ, D_MODEL), f32),
        "norm_mix": 1.0 + 0.02 * jax.random.normal(ks[1], (DEPTH, D_MODEL), f32),
        "norm_ffn": 1.0 + 0.02 * jax.random.normal(ks[2], (DEPTH, D_MODEL), f32),
        "norm_final": 1.0 + 0.02 * jax.random.normal(ks[3], (D_MODEL,), f32),
        "conv_w_in": w(ks[4], (N_CONV_LAYERS, D_MODEL, 3 * D_MODEL), D_MODEL),
        "conv_w_conv": w(ks[5], (N_CONV_LAYERS, CONV_WIDTH, D_MODEL), CONV_WIDTH),
        "conv_w_out": w(ks[6], (N_CONV_LAYERS, D_MODEL, D_MODEL), D_MODEL),
        "attn_w_qkv": w(ks[7], (N_ATTN_LAYERS, D_MODEL, QKV_WIDTH), D_MODEL),
        "attn_b_qkv": 0.02 * jax.random.normal(ks[8], (N_ATTN_LAYERS, QKV_WIDTH), f32),
        "attn_sinks": 0.5 * jax.random.normal(ks[9], (N_ATTN_LAYERS, N_HEADS), f32),
        "attn_w_o": w(ks[10], (N_ATTN_LAYERS, N_HEADS * HEAD_DIM, D_MODEL), N_HEADS * HEAD_DIM),
        "attn_b_o": 0.02 * jax.random.normal(ks[11], (N_ATTN_LAYERS, D_MODEL), f32),
        "ffn_w_in": w(ks[12], (DEPTH, D_MODEL, 2 * D_FF), D_MODEL),
        "ffn_w_conv": w(ks[13], (DEPTH, CONV_WIDTH, D_FF), CONV_WIDTH),
        "ffn_w_down": w(ks[14], (DEPTH, D_FF, D_MODEL), D_FF),
    }


def _fwd_reference(x, norm_mix, norm_ffn, norm_final, conv_w_in, conv_w_conv, conv_w_out,
              attn_w_qkv, attn_b_qkv, attn_sinks, attn_w_o, attn_b_o,
              ffn_w_in, ffn_w_conv, ffn_w_down):
    seq = x.shape[1]
    pos = jnp.arange(seq, dtype=jnp.float32)
    inv_freq = 1.0 / (ROPE_THETA ** (jnp.arange(0, HEAD_DIM, 2, dtype=jnp.float32) / HEAD_DIM))
    ang = pos[:, None] * inv_freq[None, :]
    cos = jnp.cos(ang).astype(x.dtype)
    sin = jnp.sin(ang).astype(x.dtype)

    for i in range(DEPTH):
        h = rms_norm(x, norm_mix[i])
        j = i // N_MIXERS
        if i % N_MIXERS == 0:
            mix = short_conv_mixer(h, conv_w_in[j], conv_w_conv[j], conv_w_out[j])
        else:
            mix = swa_sink_attention(h, attn_w_qkv[j], attn_b_qkv[j], attn_sinks[j],
                                     attn_w_o[j], attn_b_o[j], cos, sin)
        x = x + mix
        x = x + conv_ffn(rms_norm(x, norm_ffn[i]), ffn_w_in[i], ffn_w_conv[i], ffn_w_down[i])
    return rms_norm(x, norm_final)


import jax as _jax
import jax.numpy as _jnp

TWIN_FORMAT = 'train_step'
FWD_PARAMS = ['x', 'norm_mix', 'norm_ffn', 'norm_final', 'conv_w_in', 'conv_w_conv', 'conv_w_out', 'attn_w_qkv', 'attn_b_qkv', 'attn_sinks', 'attn_w_o', 'attn_b_o', 'ffn_w_in', 'ffn_w_conv', 'ffn_w_down']
TWIN_WEIGHTS = ['norm_mix', 'norm_ffn', 'norm_final', 'conv_w_in', 'conv_w_conv', 'conv_w_out', 'attn_w_qkv', 'attn_b_qkv', 'attn_sinks', 'attn_w_o', 'attn_b_o', 'ffn_w_in', 'ffn_w_conv', 'ffn_w_down']
TWIN_DIFF_INPUT = 'x'
TWIN_INPUTS = ['x', 'norm_mix', 'norm_ffn', 'norm_final', 'conv_w_in', 'conv_w_conv', 'conv_w_out', 'attn_w_qkv', 'attn_b_qkv', 'attn_sinks', 'attn_w_o', 'attn_b_o', 'ffn_w_in', 'ffn_w_conv', 'ffn_w_down', 'loss_target', 'm_norm_mix', 'm_norm_ffn', 'm_norm_final', 'm_conv_w_in', 'm_conv_w_conv', 'm_conv_w_out', 'm_attn_w_qkv', 'm_attn_b_qkv', 'm_attn_sinks', 'm_attn_w_o', 'm_attn_b_o', 'm_ffn_w_in', 'm_ffn_w_conv', 'm_ffn_w_down', 'v_norm_mix', 'v_norm_ffn', 'v_norm_final', 'v_conv_w_in', 'v_conv_w_conv', 'v_conv_w_out', 'v_attn_w_qkv', 'v_attn_b_qkv', 'v_attn_sinks', 'v_attn_w_o', 'v_attn_b_o', 'v_ffn_w_in', 'v_ffn_w_conv', 'v_ffn_w_down']
TWIN_OUTPUTS = ['loss', 'grad_x', 'grad_norm_mix', 'grad_norm_ffn', 'grad_norm_final', 'grad_conv_w_in', 'grad_conv_w_conv', 'grad_conv_w_out', 'grad_attn_w_qkv', 'grad_attn_b_qkv', 'grad_attn_sinks', 'grad_attn_w_o', 'grad_attn_b_o', 'grad_ffn_w_in', 'grad_ffn_w_conv', 'grad_ffn_w_down', 'delta_norm_mix', 'delta_norm_ffn', 'delta_norm_final', 'delta_conv_w_in', 'delta_conv_w_conv', 'delta_conv_w_out', 'delta_attn_w_qkv', 'delta_attn_b_qkv', 'delta_attn_sinks', 'delta_attn_w_o', 'delta_attn_b_o', 'delta_ffn_w_in', 'delta_ffn_w_conv', 'delta_ffn_w_down', 'new_m_norm_mix', 'new_m_norm_ffn', 'new_m_norm_final', 'new_m_conv_w_in', 'new_m_conv_w_conv', 'new_m_conv_w_out', 'new_m_attn_w_qkv', 'new_m_attn_b_qkv', 'new_m_attn_sinks', 'new_m_attn_w_o', 'new_m_attn_b_o', 'new_m_ffn_w_in', 'new_m_ffn_w_conv', 'new_m_ffn_w_down', 'new_v_norm_mix', 'new_v_norm_ffn', 'new_v_norm_final', 'new_v_conv_w_in', 'new_v_conv_w_conv', 'new_v_conv_w_out', 'new_v_attn_w_qkv', 'new_v_attn_b_qkv', 'new_v_attn_sinks', 'new_v_attn_w_o', 'new_v_attn_b_o', 'new_v_ffn_w_in', 'new_v_ffn_w_conv', 'new_v_ffn_w_down']
TWIN_LEAF_KINDS = {'loss': 'loss', 'grad_x': 'grad_x', 'grad_norm_mix': 'grad_w', 'grad_norm_ffn': 'grad_w', 'grad_norm_final': 'grad_w', 'grad_conv_w_in': 'grad_w', 'grad_conv_w_conv': 'grad_w', 'grad_conv_w_out': 'grad_w', 'grad_attn_w_qkv': 'grad_w', 'grad_attn_b_qkv': 'grad_w', 'grad_attn_sinks': 'grad_w', 'grad_attn_w_o': 'grad_w', 'grad_attn_b_o': 'grad_w', 'grad_ffn_w_in': 'grad_w', 'grad_ffn_w_conv': 'grad_w', 'grad_ffn_w_down': 'grad_w', 'delta_norm_mix': 'delta_w', 'delta_norm_ffn': 'delta_w', 'delta_norm_final': 'delta_w', 'delta_conv_w_in': 'delta_w', 'delta_conv_w_conv': 'delta_w', 'delta_conv_w_out': 'delta_w', 'delta_attn_w_qkv': 'delta_w', 'delta_attn_b_qkv': 'delta_w', 'delta_attn_sinks': 'delta_w', 'delta_attn_w_o': 'delta_w', 'delta_attn_b_o': 'delta_w', 'delta_ffn_w_in': 'delta_w', 'delta_ffn_w_conv': 'delta_w', 'delta_ffn_w_down': 'delta_w', 'new_m_norm_mix': 'new_m', 'new_m_norm_ffn': 'new_m', 'new_m_norm_final': 'new_m', 'new_m_conv_w_in': 'new_m', 'new_m_conv_w_conv': 'new_m', 'new_m_conv_w_out': 'new_m', 'new_m_attn_w_qkv': 'new_m', 'new_m_attn_b_qkv': 'new_m', 'new_m_attn_sinks': 'new_m', 'new_m_attn_w_o': 'new_m', 'new_m_attn_b_o': 'new_m', 'new_m_ffn_w_in': 'new_m', 'new_m_ffn_w_conv': 'new_m', 'new_m_ffn_w_down': 'new_m', 'new_v_norm_mix': 'new_v', 'new_v_norm_ffn': 'new_v', 'new_v_norm_final': 'new_v', 'new_v_conv_w_in': 'new_v', 'new_v_conv_w_conv': 'new_v', 'new_v_conv_w_out': 'new_v', 'new_v_attn_w_qkv': 'new_v', 'new_v_attn_b_qkv': 'new_v', 'new_v_attn_sinks': 'new_v', 'new_v_attn_w_o': 'new_v', 'new_v_attn_b_o': 'new_v', 'new_v_ffn_w_in': 'new_v', 'new_v_ffn_w_conv': 'new_v', 'new_v_ffn_w_down': 'new_v'}


def _forward(args):
    return _fwd_reference(*[args[k] for k in FWD_PARAMS])


def _output_shape():
    out = _jax.eval_shape(lambda: _forward(_fwd_setup_inputs(0)))
    return out.shape, out.dtype

N_MICROBATCH = 1
ADAM_LR = 0.001
ADAM_B1 = 0.9
ADAM_B2 = 0.999
ADAM_EPS = 1e-08
ADAM_WD = 0.01
ADAM_STEP = 10
PER_EXAMPLE_BATCH_AXIS = {'x': 0, 'loss_target': 0}
SHARED_INPUTS = []
_WEIGHT_DTYPES = {'norm_mix': _jnp.float32, 'norm_ffn': _jnp.float32, 'norm_final': _jnp.float32, 'conv_w_in': _jnp.float32, 'conv_w_conv': _jnp.float32, 'conv_w_out': _jnp.float32, 'attn_w_qkv': _jnp.float32, 'attn_b_qkv': _jnp.float32, 'attn_sinks': _jnp.float32, 'attn_w_o': _jnp.float32, 'attn_b_o': _jnp.float32, 'ffn_w_in': _jnp.float32, 'ffn_w_conv': _jnp.float32, 'ffn_w_down': _jnp.float32}
MOMENT_SCALE = {'norm_mix': 2.764766e-01, 'norm_ffn': 1.703680e-01, 'norm_final': 6.403491e+01, 'conv_w_in': 2.237570e-01, 'conv_w_conv': 2.372324e-01, 'conv_w_out': 2.225517e-01, 'attn_w_qkv': 5.136522e-02, 'attn_b_qkv': 1.813299e-01, 'attn_sinks': 3.086283e-02, 'attn_w_o': 4.058175e-02, 'attn_b_o': 2.053953e-01, 'ffn_w_in': 6.831488e-02, 'ffn_w_conv': 6.868801e-02, 'ffn_w_down': 1.118803e-01}


def _to_microbatches(a, axis):
    t = _jnp.moveaxis(a, axis, 0)
    t = t.reshape((N_MICROBATCH, t.shape[0] // N_MICROBATCH) + t.shape[1:])
    return _jnp.moveaxis(t, 1, axis + 1)


def setup_inputs(seed: int = 0) -> dict:
    inp = _fwd_setup_inputs(seed)
    key = _jax.random.fold_in(_jax.random.key(seed), 7919)
    shape, _ = _output_shape()
    out = dict(inp)
    out["loss_target"] = _jax.random.normal(_jax.random.fold_in(key, 0), shape, _jnp.float32)
    for i, name in enumerate(TWIN_WEIGHTS):
        w = inp[name].astype(_jnp.float32)
        if MOMENT_SCALE is None:
            s = _jnp.sqrt(_jnp.mean(_jnp.square(w)) + 1e-30)
        else:
            s = MOMENT_SCALE[name]
        km, kv = _jax.random.split(_jax.random.fold_in(key, i + 1))
        out[name] = w
        out["m_" + name] = s * _jax.random.normal(km, w.shape, _jnp.float32)
        out["v_" + name] = (s * s) * _jax.random.uniform(kv, w.shape, _jnp.float32, 0.5, 1.5)
    if N_MICROBATCH > 1:
        for name, axis in PER_EXAMPLE_BATCH_AXIS.items():
            out[name] = _to_microbatches(out[name], axis)
    return {'x': out['x'], 'norm_mix': out['norm_mix'], 'norm_ffn': out['norm_ffn'], 'norm_final': out['norm_final'], 'conv_w_in': out['conv_w_in'], 'conv_w_conv': out['conv_w_conv'], 'conv_w_out': out['conv_w_out'], 'attn_w_qkv': out['attn_w_qkv'], 'attn_b_qkv': out['attn_b_qkv'], 'attn_sinks': out['attn_sinks'], 'attn_w_o': out['attn_w_o'], 'attn_b_o': out['attn_b_o'], 'ffn_w_in': out['ffn_w_in'], 'ffn_w_conv': out['ffn_w_conv'], 'ffn_w_down': out['ffn_w_down'], 'loss_target': out['loss_target'], 'm_norm_mix': out['m_norm_mix'], 'm_norm_ffn': out['m_norm_ffn'], 'm_norm_final': out['m_norm_final'], 'm_conv_w_in': out['m_conv_w_in'], 'm_conv_w_conv': out['m_conv_w_conv'], 'm_conv_w_out': out['m_conv_w_out'], 'm_attn_w_qkv': out['m_attn_w_qkv'], 'm_attn_b_qkv': out['m_attn_b_qkv'], 'm_attn_sinks': out['m_attn_sinks'], 'm_attn_w_o': out['m_attn_w_o'], 'm_attn_b_o': out['m_attn_b_o'], 'm_ffn_w_in': out['m_ffn_w_in'], 'm_ffn_w_conv': out['m_ffn_w_conv'], 'm_ffn_w_down': out['m_ffn_w_down'], 'v_norm_mix': out['v_norm_mix'], 'v_norm_ffn': out['v_norm_ffn'], 'v_norm_final': out['v_norm_final'], 'v_conv_w_in': out['v_conv_w_in'], 'v_conv_w_conv': out['v_conv_w_conv'], 'v_conv_w_out': out['v_conv_w_out'], 'v_attn_w_qkv': out['v_attn_w_qkv'], 'v_attn_b_qkv': out['v_attn_b_qkv'], 'v_attn_sinks': out['v_attn_sinks'], 'v_attn_w_o': out['v_attn_w_o'], 'v_attn_b_o': out['v_attn_b_o'], 'v_ffn_w_in': out['v_ffn_w_in'], 'v_ffn_w_conv': out['v_ffn_w_conv'], 'v_ffn_w_down': out['v_ffn_w_down']}


def _loss(weights, diff, rest, loss_target):
    with _jax.named_scope("forward"):
        args = {**rest, TWIN_DIFF_INPUT: diff, **{k: w.astype(_WEIGHT_DTYPES[k]) for k, w in weights.items()}}
        y = _forward(args)
    with _jax.named_scope("loss_head"):
        err = _jnp.square(y.astype(_jnp.float32) - loss_target)
        return 0.5 * _jnp.sum(_jnp.mean(err, axis=-1)) if err.ndim else 0.5 * err


def _adamw(w, g, m, v):
    m = ADAM_B1 * m + (1.0 - ADAM_B1) * g
    v = ADAM_B2 * v + (1.0 - ADAM_B2) * _jnp.square(g)
    m_hat = m / (1.0 - ADAM_B1 ** ADAM_STEP)
    v_hat = v / (1.0 - ADAM_B2 ** ADAM_STEP)
    delta = -ADAM_LR * (m_hat / (_jnp.sqrt(v_hat) + ADAM_EPS) + ADAM_WD * w)
    return delta, m, v


def reference(x, norm_mix, norm_ffn, norm_final, conv_w_in, conv_w_conv, conv_w_out, attn_w_qkv, attn_b_qkv, attn_sinks, attn_w_o, attn_b_o, ffn_w_in, ffn_w_conv, ffn_w_down, loss_target, m_norm_mix, m_norm_ffn, m_norm_final, m_conv_w_in, m_conv_w_conv, m_conv_w_out, m_attn_w_qkv, m_attn_b_qkv, m_attn_sinks, m_attn_w_o, m_attn_b_o, m_ffn_w_in, m_ffn_w_conv, m_ffn_w_down, v_norm_mix, v_norm_ffn, v_norm_final, v_conv_w_in, v_conv_w_conv, v_conv_w_out, v_attn_w_qkv, v_attn_b_qkv, v_attn_sinks, v_attn_w_o, v_attn_b_o, v_ffn_w_in, v_ffn_w_conv, v_ffn_w_down):
    given = dict(x=x, norm_mix=norm_mix, norm_ffn=norm_ffn, norm_final=norm_final, conv_w_in=conv_w_in, conv_w_conv=conv_w_conv, conv_w_out=conv_w_out, attn_w_qkv=attn_w_qkv, attn_b_qkv=attn_b_qkv, attn_sinks=attn_sinks, attn_w_o=attn_w_o, attn_b_o=attn_b_o, ffn_w_in=ffn_w_in, ffn_w_conv=ffn_w_conv, ffn_w_down=ffn_w_down, loss_target=loss_target, m_norm_mix=m_norm_mix, m_norm_ffn=m_norm_ffn, m_norm_final=m_norm_final, m_conv_w_in=m_conv_w_in, m_conv_w_conv=m_conv_w_conv, m_conv_w_out=m_conv_w_out, m_attn_w_qkv=m_attn_w_qkv, m_attn_b_qkv=m_attn_b_qkv, m_attn_sinks=m_attn_sinks, m_attn_w_o=m_attn_w_o, m_attn_b_o=m_attn_b_o, m_ffn_w_in=m_ffn_w_in, m_ffn_w_conv=m_ffn_w_conv, m_ffn_w_down=m_ffn_w_down, v_norm_mix=v_norm_mix, v_norm_ffn=v_norm_ffn, v_norm_final=v_norm_final, v_conv_w_in=v_conv_w_in, v_conv_w_conv=v_conv_w_conv, v_conv_w_out=v_conv_w_out, v_attn_w_qkv=v_attn_w_qkv, v_attn_b_qkv=v_attn_b_qkv, v_attn_sinks=v_attn_sinks, v_attn_w_o=v_attn_w_o, v_attn_b_o=v_attn_b_o, v_ffn_w_in=v_ffn_w_in, v_ffn_w_conv=v_ffn_w_conv, v_ffn_w_down=v_ffn_w_down)
    weights = {n: given[n] for n in TWIN_WEIGHTS}
    shared = {n: given[n] for n in SHARED_INPUTS}
    per_example = {n: given[n] for n in ['x']}
    grad_fn = _jax.value_and_grad(_loss, argnums=(0, 1))

    def one_microbatch(ex, loss_target):
        ex = dict(ex)
        diff = ex.pop(TWIN_DIFF_INPUT)
        return grad_fn(weights, diff, {**shared, **ex}, loss_target)

    if N_MICROBATCH == 1:
        loss, (grad_w, grad_x) = one_microbatch(per_example, given["loss_target"])
    else:
        def body(carry, xs):
            loss_sum, grad_sum = carry
            l_k, (gw_k, gx_k) = one_microbatch(xs[0], xs[1])
            with _jax.named_scope("update"):
                return (loss_sum + l_k, _jax.tree.map(_jnp.add, grad_sum, gw_k)), gx_k

        init = (_jnp.zeros((), _jnp.float32), _jax.tree.map(_jnp.zeros_like, weights))
        (loss, grad_w), grad_x = _jax.lax.scan(body, init, (per_example, given["loss_target"]))
    with _jax.named_scope("update"):
        delta_w, new_m, new_v = {}, {}, {}
        for n in TWIN_WEIGHTS:
            delta_w[n], new_m[n], new_v[n] = _adamw(weights[n], grad_w[n], given["m_" + n], given["v_" + n])
    return (loss, grad_x, *[grad_w[n] for n in TWIN_WEIGHTS], *[delta_w[n] for n in TWIN_WEIGHTS],
            *[new_m[n] for n in TWIN_WEIGHTS], *[new_v[n] for n in TWIN_WEIGHTS])
```

```python
import math

import jax
import jax.numpy as jnp
from jax import lax
from jax.experimental import pallas as pl
from jax.experimental.pallas import tpu as pltpu

F32 = jnp.float32
BF16 = jnp.bfloat16

HEAD_DIM = 64
GROUP = 4
WINDOW = 128
EPS = 1e-5
ROPE_THETA = 10000.0
ADAM_LR, ADAM_B1, ADAM_B2, ADAM_EPS, ADAM_WD, ADAM_STEP = 0.001, 0.9, 0.999, 1e-08, 0.01, 10

N_CHIPS = 4
N_CORES = 2
N_DEV = 8
HALO = 8
VMEM_LIMIT_BYTES = 56 * 1024 * 1024
MESH = pl.DeviceIdType.MESH
ANY = pl.BlockSpec(memory_space=pl.ANY)
SMEM = pl.BlockSpec(memory_space=pltpu.SMEM)
NEG = float(jnp.finfo(jnp.float32).min)
ROW_TILES = (512, 256, 128, 64, 32, 16, 8)


def _pick(dim, cands):
    for c in cands:
        if dim % c == 0:
            return c
    return dim


def _params(sem):
    return pltpu.CompilerParams(dimension_semantics=sem, vmem_limit_bytes=VMEM_LIMIT_BYTES)


_DIMS = {"nn": (((1,), (0,)), ((), ())), "nt": (((1,), (1,)), ((), ())), "tn": (((0,), (0,)), ((), ()))}


def _mm(a, b, mode, out_dtype, *, layer=None, bias=None, residual=None, tm, tn, tk, name):
    b2 = b.shape[1:] if layer is not None else b.shape
    if mode == "nn":
        (m, k), n = a.shape, b2[1]
    elif mode == "nt":
        (m, k), n = a.shape, b2[0]
    else:
        (k, m), n = a.shape, b2[1]
    tm, tn, tk = min(tm, m), min(tn, n), min(tk, k)
    assert m % tm == 0 and n % tn == 0 and k % tk == 0, (name, a.shape, b.shape, tm, tn, tk)
    nk = k // tk
    a_spec = pl.BlockSpec((tk, tm), lambda i, j, l: (l, i)) if mode == "tn" else pl.BlockSpec((tm, tk), lambda i, j, l: (i, l))
    if layer is None:
        b_spec = pl.BlockSpec((tn, tk), lambda i, j, l: (j, l)) if mode == "nt" else pl.BlockSpec((tk, tn), lambda i, j, l: (l, j))
    elif mode == "nt":
        b_spec = pl.BlockSpec((None, tn, tk), lambda i, j, l: (layer, j, l))
    else:
        b_spec = pl.BlockSpec((None, tk, tn), lambda i, j, l: (layer, l, j))
    in_specs, args = [a_spec, b_spec], [a, b]
    if bias is not None:
        in_specs.append(pl.BlockSpec((1, tn), lambda i, j, l: (0, j)))
        args.append(bias)
    if residual is not None:
        in_specs.append(pl.BlockSpec((tm, tn), lambda i, j, l: (i, j)))
        args.append(residual)
    has_bias, has_res = bias is not None, residual is not None

    def body(*refs):
        a_ref, b_ref = refs[0], refs[1]
        pos = 2
        bias_ref = res_ref = None
        if has_bias:
            bias_ref, pos = refs[pos], pos + 1
        if has_res:
            res_ref, pos = refs[pos], pos + 1
        o_ref = refs[pos]
        acc_ref = refs[pos + 1] if nk > 1 else None

        def finish(acc):
            if has_bias:
                acc = acc + bias_ref[...]
            if has_res:
                acc = acc + res_ref[...]
            o_ref[...] = acc.astype(o_ref.dtype)

        part = lax.dot_general(a_ref[...], b_ref[...], _DIMS[mode], preferred_element_type=F32)
        if nk == 1:
            finish(part)
        else:
            l = pl.program_id(2)

            @pl.when(l == 0)
            def _():
                acc_ref[...] = part

            @pl.when(l > 0)
            def _():
                acc_ref[...] += part

            @pl.when(l == nk - 1)
            def _():
                finish(acc_ref[...])

    return pl.pallas_call(
        body,
        name=name,
        grid=(m // tm, n // tn, nk),
        in_specs=in_specs,
        out_specs=pl.BlockSpec((tm, tn), lambda i, j, l: (i, j)),
        out_shape=jax.ShapeDtypeStruct((m, n), out_dtype),
        scratch_shapes=[pltpu.VMEM((tm, tn), F32)] if nk > 1 else [],
        compiler_params=_params(("parallel", "parallel", "arbitrary")),
    )(*args)


def _fold8(v):
    r, d = v.shape
    return jnp.sum(v.reshape(r // 8, 8, d), axis=0)


def _rms_fwd(x, g, name):
    t, d = x.shape
    tm = _pick(t, ROW_TILES)

    def body(x_ref, g_ref, h_ref):
        xv = x_ref[...]
        r = lax.rsqrt(jnp.mean(xv * xv, axis=-1, keepdims=True) + EPS)
        h_ref[...] = (xv * r * g_ref[...]).astype(BF16)

    return pl.pallas_call(
        body,
        name=name,
        grid=(t // tm,),
        in_specs=[pl.BlockSpec((tm, d), lambda i: (i, 0)), pl.BlockSpec((1, d), lambda i: (0, 0))],
        out_specs=pl.BlockSpec((tm, d), lambda i: (i, 0)),
        out_shape=jax.ShapeDtypeStruct((t, d), BF16),
        compiler_params=_params(("parallel",)),
    )(x, g)


def _rms_bwd(x, dh, g, dx_in, name):
    t, d = x.shape
    tm = _pick(t, ROW_TILES)

    def body(x_ref, dh_ref, g_ref, dxi_ref, dx_ref, dxb_ref, dg_ref, cs_ref):
        i = pl.program_id(0)
        xv = x_ref[...]
        r = lax.rsqrt(jnp.mean(xv * xv, axis=-1, keepdims=True) + EPS)
        xhat = xv * r
        dy = dh_ref[...]
        gdy = dy * g_ref[...]
        dx = dxi_ref[...] + r * (gdy - xhat * jnp.mean(gdy * xhat, axis=-1, keepdims=True))
        dx_ref[...] = dx
        dxb_ref[...] = dx.astype(BF16)

        @pl.when(i == 0)
        def _():
            dg_ref[...] = jnp.zeros_like(dg_ref)
            cs_ref[...] = jnp.zeros_like(cs_ref)

        dg_ref[...] += _fold8(dy * xhat)
        cs_ref[...] += _fold8(dx)

    row = pl.BlockSpec((tm, d), lambda i: (i, 0))
    acc = pl.BlockSpec((8, d), lambda i: (0, 0))
    return pl.pallas_call(
        body,
        name=name,
        grid=(t // tm,),
        in_specs=[row, row, pl.BlockSpec((1, d), lambda i: (0, 0)), row],
        out_specs=[row, row, acc, acc],
        out_shape=[jax.ShapeDtypeStruct((t, d), F32), jax.ShapeDtypeStruct((t, d), BF16),
                   jax.ShapeDtypeStruct((8, d), F32), jax.ShapeDtypeStruct((8, d), F32)],
        compiler_params=_params(("arbitrary",)),
    )(x, dh, g, dx_in)


def _loss_head(x, target, g, name):
    t, d = x.shape
    tm = _pick(t, ROW_TILES)
    inv_d = 1.0 / d

    def body(x_ref, t_ref, g_ref, dx_ref, dxb_ref, sq_ref, dg_ref):
        i = pl.program_id(0)
        xv = x_ref[...]
        gv = g_ref[...]
        r = lax.rsqrt(jnp.mean(xv * xv, axis=-1, keepdims=True) + EPS)
        xhat = xv * r
        err = xhat * gv - t_ref[...]
        dy = err * inv_d
        gdy = dy * gv
        dx = r * (gdy - xhat * jnp.mean(gdy * xhat, axis=-1, keepdims=True))
        dx_ref[...] = dx
        dxb_ref[...] = dx.astype(BF16)

        @pl.when(i == 0)
        def _():
            sq_ref[...] = jnp.zeros_like(sq_ref)
            dg_ref[...] = jnp.zeros_like(dg_ref)

        sq_ref[...] += _fold8(err * err)
        dg_ref[...] += _fold8(dy * xhat)

    row = pl.BlockSpec((tm, d), lambda i: (i, 0))
    acc = pl.BlockSpec((8, d), lambda i: (0, 0))
    return pl.pallas_call(
        body,
        name=name,
        grid=(t // tm,),
        in_specs=[row, row, pl.BlockSpec((1, d), lambda i: (0, 0))],
        out_specs=[row, row, acc, acc],
        out_shape=[jax.ShapeDtypeStruct((t, d), F32), jax.ShapeDtypeStruct((t, d), BF16),
                   jax.ShapeDtypeStruct((8, d), F32), jax.ShapeDtypeStruct((8, d), F32)],
        compiler_params=_params(("arbitrary",)),
    )(x, target, g)


def _rows(tm):
    return lax.broadcasted_iota(jnp.int32, (tm, 1), 0)


def _shift_down(u, before2, rows):
    s1 = jnp.where(rows == 0, before2[1:2], pltpu.roll(u, 1, 0))
    s2 = jnp.where(rows == 0, before2[0:1], jnp.where(rows == 1, before2[1:2], pltpu.roll(u, 2, 0)))
    return s1, s2


def _shift_up(u, after2, rows):
    tm = u.shape[0]
    s1 = jnp.where(rows == tm - 1, after2[0:1], pltpu.roll(u, tm - 1, 0))
    s2 = jnp.where(rows == tm - 2, after2[0:1], jnp.where(rows == tm - 1, after2[1:2], pltpu.roll(u, tm - 2, 0)))
    return s1, s2


def _conv_tile(seq):
    return _pick(seq, (256, 128, 64, 32, 16, 8))


def _halo_specs(tm, width, n_tiles):
    per = tm // HALO
    before = pl.BlockSpec((HALO, width), lambda i: (jnp.maximum(i * per - 1, 0), 0))
    after = pl.BlockSpec((HALO, width), lambda i: (jnp.minimum((i + 1) * per, n_tiles * per - 1), 0))
    return before, after


def _convgate_fwd(bcv, w, seq, name):
    t, d3 = bcv.shape
    d = d3 // 3
    tm = _conv_tile(seq)
    tps = seq // tm
    before, _ = _halo_specs(tm, d3, t // tm)

    def body(x_ref, xb_ref, w_ref, y_ref):
        i = pl.program_id(0)
        inner = (i % tps != 0).astype(F32)
        rows = _rows(tm)
        u = x_ref[:, d:2 * d] * x_ref[:, 2 * d:]
        ub = xb_ref[HALO - 2:, d:2 * d] * xb_ref[HALO - 2:, 2 * d:] * inner
        s1, s2 = _shift_down(u, ub, rows)
        z = w_ref[2:3] * u + w_ref[1:2] * s1 + w_ref[0:1] * s2
        y_ref[...] = (x_ref[:, :d] * z).astype(BF16)

    return pl.pallas_call(
        body,
        name=name,
        grid=(t // tm,),
        in_specs=[pl.BlockSpec((tm, d3), lambda i: (i, 0)), before, pl.BlockSpec((3, d), lambda i: (0, 0))],
        out_specs=pl.BlockSpec((tm, d), lambda i: (i, 0)),
        out_shape=jax.ShapeDtypeStruct((t, d), BF16),
        compiler_params=_params(("parallel",)),
    )(bcv, bcv, w)


def _convgate_bwd(bcv, dy, w, seq, name):
    t, d3 = bcv.shape
    d = d3 // 3
    tm = _conv_tile(seq)
    tps = seq // tm
    before, after = _halo_specs(tm, d3, t // tm)
    _, after_dy = _halo_specs(tm, d, t // tm)

    def body(x_ref, xb_ref, xa_ref, dy_ref, dya_ref, w_ref, dx_ref, dw_ref):
        i = pl.program_id(0)
        inner_lo = (i % tps != 0).astype(F32)
        inner_hi = (i % tps != tps - 1).astype(F32)
        rows = _rows(tm)
        w0, w1, w2 = w_ref[0:1], w_ref[1:2], w_ref[2:3]
        b, c, v = x_ref[:, :d], x_ref[:, d:2 * d], x_ref[:, 2 * d:]
        u = c * v
        ub = xb_ref[HALO - 2:, d:2 * d] * xb_ref[HALO - 2:, 2 * d:] * inner_lo
        s1, s2 = _shift_down(u, ub, rows)
        z = w2 * u + w1 * s1 + w0 * s2
        dyv = dy_ref[...]
        dz = dyv * b
        dza = dya_ref[0:2] * xa_ref[0:2, :d] * inner_hi
        n1, n2 = _shift_up(dz, dza, rows)
        du = w2 * dz + w1 * n1 + w0 * n2
        dx_ref[:, :d] = (dyv * z).astype(BF16)
        dx_ref[:, d:2 * d] = (du * v).astype(BF16)
        dx_ref[:, 2 * d:] = (du * c).astype(BF16)

        @pl.when(i == 0)
        def _():
            dw_ref[...] = jnp.zeros_like(dw_ref)

        dw_ref[0:1] += jnp.sum(dz * s2, axis=0, keepdims=True)
        dw_ref[1:2] += jnp.sum(dz * s1, axis=0, keepdims=True)
        dw_ref[2:3] += jnp.sum(dz * u, axis=0, keepdims=True)

    return pl.pallas_call(
        body,
        name=name,
        grid=(t // tm,),
        in_specs=[pl.BlockSpec((tm, d3), lambda i: (i, 0)), before, after,
                  pl.BlockSpec((tm, d), lambda i: (i, 0)), after_dy, pl.BlockSpec((3, d), lambda i: (0, 0))],
        out_specs=[pl.BlockSpec((tm, d3), lambda i: (i, 0)), pl.BlockSpec((8, d), lambda i: (0, 0))],
        out_shape=[jax.ShapeDtypeStruct((t, d3), BF16), jax.ShapeDtypeStruct((8, d), F32)],
        compiler_params=_params(("arbitrary",)),
    )(bcv, bcv, bcv, dy, dy, w)


def _sigmoid(x):
    return 1.0 / (1.0 + jnp.exp(-x))


def _ffngate_fwd(gu, w, seq, name):
    t, f2 = gu.shape
    f = f2 // 2
    tm = _conv_tile(seq)
    tps = seq // tm
    before, _ = _halo_specs(tm, f2, t // tm)

    def body(x_ref, xb_ref, w_ref, a_ref):
        i = pl.program_id(0)
        inner = (i % tps != 0).astype(F32)
        rows = _rows(tm)
        g = x_ref[:, :f]
        gb = xb_ref[HALO - 2:, :f] * inner
        s1, s2 = _shift_down(g, gb, rows)
        gc = w_ref[2:3] * g + w_ref[1:2] * s1 + w_ref[0:1] * s2
        a_ref[...] = (gc * _sigmoid(gc) * x_ref[:, f:]).astype(BF16)

    return pl.pallas_call(
        body,
        name=name,
        grid=(t // tm,),
        in_specs=[pl.BlockSpec((tm, f2), lambda i: (i, 0)), before, pl.BlockSpec((3, f), lambda i: (0, 0))],
        out_specs=pl.BlockSpec((tm, f), lambda i: (i, 0)),
        out_shape=jax.ShapeDtypeStruct((t, f), BF16),
        compiler_params=_params(("parallel",)),
    )(gu, gu, w)


def _ffngate_bwd(gu, da, w, seq, name):
    t, f2 = gu.shape
    f = f2 // 2
    tm = _conv_tile(seq)
    tps = seq // tm
    before, after = _halo_specs(tm, f2, t // tm)
    _, after_da = _halo_specs(tm, f, t // tm)

    def body(x_ref, xb_ref, xa_ref, da_ref, daa_ref, w_ref, dx_ref, dw_ref):
        i = pl.program_id(0)
        inner_lo = (i % tps != 0).astype(F32)
        inner_hi = (i % tps != tps - 1).astype(F32)
        rows = _rows(tm)
        w0, w1, w2 = w_ref[0:1], w_ref[1:2], w_ref[2:3]

        def dgate(gc, uv, dav):
            sg = _sigmoid(gc)
            return dav * uv * (sg * (1.0 + gc * (1.0 - sg))), dav * (gc * sg)

        g, u = x_ref[:, :f], x_ref[:, f:]
        gb = xb_ref[HALO - 2:, :f] * inner_lo
        s1, s2 = _shift_down(g, gb, rows)
        gc = w2 * g + w1 * s1 + w0 * s2
        dgc, du = dgate(gc, u, da_ref[...])
        ga = xa_ref[:, :f]
        a1, a2 = _shift_down(ga, x_ref[tm - 2:, :f], _rows(HALO))
        gca = w2 * ga + w1 * a1 + w0 * a2
        dgca, _ = dgate(gca, xa_ref[:, f:], daa_ref[...])
        n1, n2 = _shift_up(dgc, dgca[0:2] * inner_hi, rows)
        dx_ref[:, :f] = (w2 * dgc + w1 * n1 + w0 * n2).astype(BF16)
        dx_ref[:, f:] = du.astype(BF16)

        @pl.when(i == 0)
        def _():
            dw_ref[...] = jnp.zeros_like(dw_ref)

        dw_ref[0:1] += jnp.sum(dgc * s2, axis=0, keepdims=True)
        dw_ref[1:2] += jnp.sum(dgc * s1, axis=0, keepdims=True)
        dw_ref[2:3] += jnp.sum(dgc * g, axis=0, keepdims=True)

    return pl.pallas_call(
        body,
        name=name,
        grid=(t // tm,),
        in_specs=[pl.BlockSpec((tm, f2), lambda i: (i, 0)), before, after,
                  pl.BlockSpec((tm, f), lambda i: (i, 0)), after_da, pl.BlockSpec((3, f), lambda i: (0, 0))],
        out_specs=[pl.BlockSpec((tm, f2), lambda i: (i, 0)), pl.BlockSpec((8, f), lambda i: (0, 0))],
        out_shape=[jax.ShapeDtypeStruct((t, f2), BF16), jax.ShapeDtypeStruct((8, f), F32)],
        compiler_params=_params(("arbitrary",)),
    )(gu, gu, gu, da, da, w)


def _swap_halves(xt):
    half = HEAD_DIM // 2
    return jnp.concatenate([xt[half:], xt[:half]], axis=0)


def _rope(xt, cos, sin):
    return xt * cos + _swap_halves(xt) * sin


def _unrope(dxt, cos, sin):
    return dxt * cos - _swap_halves(dxt) * sin


def _band_masks(n):
    kj = lax.broadcasted_iota(jnp.int32, (WINDOW, WINDOW), 0)
    qi = lax.broadcasted_iota(jnp.int32, (WINDOW, WINDOW), 1)
    return kj <= qi, jnp.logical_and(kj > qi, n > 0)


def _tn(a, b):
    return lax.dot_general(a, b, _DIMS["tn"], preferred_element_type=F32)


def _nt(a, b):
    return lax.dot_general(a, b, _DIMS["nt"], preferred_element_type=F32)


def _nn(a, b):
    return jnp.dot(a, b, preferred_element_type=F32)


def _attn_fwd(qkv, sinks, cos_t, sin_t, bsz, seq, name):
    t, qw = qkv.shape
    d = qw * 2 // 3
    kvw = d // GROUP
    n_heads, n_kv = d // HEAD_DIM, kvw // HEAD_DIM
    nb = seq // WINDOW
    scale = HEAD_DIM ** -0.5

    def body(sink_ref, xc_ref, xp_ref, cc_ref, sc_ref, cp_ref, sp_ref, o_ref, lse_ref, xt_ref, pt_ref, ot_ref):
        n = pl.program_id(1)
        xt_ref[...] = xc_ref[...].T
        pt_ref[...] = xp_ref[:, d:].T
        cos_c, sin_c, cos_p, sin_p = cc_ref[...], sc_ref[...], cp_ref[...], sp_ref[...]
        valid_c, valid_p = _band_masks(n)
        for j in range(n_kv):
            ko = j * HEAD_DIM
            kc = _rope(xt_ref[d + ko:d + ko + HEAD_DIM, :], cos_c, sin_c).astype(BF16)
            kp = _rope(pt_ref[ko:ko + HEAD_DIM, :], cos_p, sin_p).astype(BF16)
            vc = xt_ref[d + kvw + ko:d + kvw + ko + HEAD_DIM, :].astype(BF16)
            vp = pt_ref[kvw + ko:kvw + ko + HEAD_DIM, :].astype(BF16)
            for g in range(GROUP):
                h = j * GROUP + g
                qo = h * HEAD_DIM
                q = _rope(xt_ref[qo:qo + HEAD_DIM, :], cos_c, sin_c).astype(BF16)
                s_c = jnp.where(valid_c, _tn(kc, q) * scale, NEG)
                s_p = jnp.where(valid_p, _tn(kp, q) * scale, NEG)
                sink = sink_ref[h]
                m = jnp.maximum(jnp.maximum(jnp.max(s_c, axis=0, keepdims=True),
                                            jnp.max(s_p, axis=0, keepdims=True)), sink)
                p_c = jnp.exp(s_c - m)
                p_p = jnp.exp(s_p - m)
                den = jnp.sum(p_c, axis=0, keepdims=True) + jnp.sum(p_p, axis=0, keepdims=True) + jnp.exp(sink - m)
                inv = 1.0 / den
                ot_ref[qo:qo + HEAD_DIM, :] = (_nn(vc, (p_c * inv).astype(BF16)) + _nn(vp, (p_p * inv).astype(BF16)))
                lse_ref[h:h + 1, :] = m + jnp.log(den)
        o_ref[...] = ot_ref[...].T.astype(BF16)

    cur = lambda b, n: (b * nb + n, 0)
    prev = lambda b, n: (b * nb + jnp.maximum(n - 1, 0), 0)
    tab_c = pl.BlockSpec((HEAD_DIM, WINDOW), lambda b, n: (0, n))
    tab_p = pl.BlockSpec((HEAD_DIM, WINDOW), lambda b, n: (0, jnp.maximum(n - 1, 0)))
    return pl.pallas_call(
        body,
        name=name,
        grid=(bsz, nb),
        in_specs=[SMEM, pl.BlockSpec((WINDOW, qw), cur), pl.BlockSpec((WINDOW, qw), prev), tab_c, tab_c, tab_p, tab_p],
        out_specs=[pl.BlockSpec((WINDOW, d), cur), pl.BlockSpec((n_heads, WINDOW), lambda b, n: (0, b * nb + n))],
        out_shape=[jax.ShapeDtypeStruct((t, d), BF16), jax.ShapeDtypeStruct((n_heads, t), F32)],
        scratch_shapes=[pltpu.VMEM((qw, WINDOW), F32), pltpu.VMEM((2 * kvw, WINDOW), F32), pltpu.VMEM((d, WINDOW), F32)],
        compiler_params=_params(("parallel", "arbitrary")),
    )(sinks, qkv, qkv, cos_t, sin_t, cos_t, sin_t)


def _attn_bwd(qkv, o, lse, do, sinks, cos_t, sin_t, bsz, seq, name):
    t, qw = qkv.shape
    d = qw * 2 // 3
    kvw = d // GROUP
    n_heads, n_kv = d // HEAD_DIM, kvw // HEAD_DIM
    nb = seq // WINDOW
    scale = HEAD_DIM ** -0.5

    def body(sink_ref, xc_ref, xp_ref, xn_ref, oc_ref, on_ref, doc_ref, don_ref, lc_ref, ln_ref,
             cc_ref, sc_ref, cp_ref, sp_ref, cn_ref, sn_ref,
             dx_ref, db_ref, dsk_ref, xt_ref, pt_ref, qn_ref, otc_ref, otn_ref, dtc_ref, dtn_ref, gt_ref):
        b, n = pl.program_id(0), pl.program_id(1)
        xt_ref[...] = xc_ref[...].T
        pt_ref[...] = xp_ref[:, d:].T
        qn_ref[...] = xn_ref[:, :d].T
        otc_ref[...] = oc_ref[...].astype(F32).T
        otn_ref[...] = on_ref[...].astype(F32).T
        dtc_ref[...] = doc_ref[...].T
        dtn_ref[...] = don_ref[...].T
        cos_c, sin_c, cos_p, sin_p, cos_n, sin_n = (cc_ref[...], sc_ref[...], cp_ref[...], sp_ref[...],
                                                    cn_ref[...], sn_ref[...])
        valid_c, valid_p = _band_masks(n)
        kj = lax.broadcasted_iota(jnp.int32, (WINDOW, WINDOW), 0)
        qi = lax.broadcasted_iota(jnp.int32, (WINDOW, WINDOW), 1)
        valid_n = jnp.logical_and(kj > qi, n < nb - 1)

        @pl.when(jnp.logical_and(b == 0, n == 0))
        def _():
            db_ref[...] = jnp.zeros_like(db_ref)
            dsk_ref[...] = jnp.zeros_like(dsk_ref)

        for j in range(n_kv):
            ko = j * HEAD_DIM
            kc = _rope(xt_ref[d + ko:d + ko + HEAD_DIM, :], cos_c, sin_c).astype(BF16)
            kp = _rope(pt_ref[ko:ko + HEAD_DIM, :], cos_p, sin_p).astype(BF16)
            vc = xt_ref[d + kvw + ko:d + kvw + ko + HEAD_DIM, :].astype(BF16)
            vp = pt_ref[kvw + ko:kvw + ko + HEAD_DIM, :].astype(BF16)
            dk = jnp.zeros((HEAD_DIM, WINDOW), F32)
            dv = jnp.zeros((HEAD_DIM, WINDOW), F32)
            for g in range(GROUP):
                h = j * GROUP + g
                qo = h * HEAD_DIM
                sink = sink_ref[h]
                q = _rope(xt_ref[qo:qo + HEAD_DIM, :], cos_c, sin_c).astype(BF16)
                do_h = dtc_ref[qo:qo + HEAD_DIM, :]
                do_b = do_h.astype(BF16)
                lse_h = lc_ref[h:h + 1, :]
                delta = jnp.sum(otc_ref[qo:qo + HEAD_DIM, :] * do_h, axis=0, keepdims=True)
                p_c = jnp.exp(jnp.where(valid_c, _tn(kc, q) * scale, NEG) - lse_h)
                p_p = jnp.exp(jnp.where(valid_p, _tn(kp, q) * scale, NEG) - lse_h)
                ds_c = (p_c * (_tn(vc, do_b) - delta)).astype(BF16)
                ds_p = (p_p * (_tn(vp, do_b) - delta)).astype(BF16)
                dq = (_nn(kc, ds_c) + _nn(kp, ds_p)) * scale
                gt_ref[qo:qo + HEAD_DIM, :] = _unrope(dq, cos_c, sin_c)
                dsk_ref[h:h + 1, :] += -jnp.exp(sink - lse_h) * delta
                dv += _nt(do_b, p_c.astype(BF16))
                dk += _nt(q, ds_c)
                q2 = _rope(qn_ref[qo:qo + HEAD_DIM, :], cos_n, sin_n).astype(BF16)
                do2 = dtn_ref[qo:qo + HEAD_DIM, :]
                do2_b = do2.astype(BF16)
                delta2 = jnp.sum(otn_ref[qo:qo + HEAD_DIM, :] * do2, axis=0, keepdims=True)
                p_n = jnp.exp(jnp.where(valid_n, _tn(kc, q2) * scale, NEG) - ln_ref[h:h + 1, :])
                ds_n = (p_n * (_tn(vc, do2_b) - delta2)).astype(BF16)
                dv += _nt(do2_b, p_n.astype(BF16))
                dk += _nt(q2, ds_n)
            gt_ref[d + ko:d + ko + HEAD_DIM, :] = _unrope(dk * scale, cos_c, sin_c)
            gt_ref[d + kvw + ko:d + kvw + ko + HEAD_DIM, :] = dv
        dx = gt_ref[...].T
        dx_ref[...] = dx.astype(BF16)
        db_ref[...] += _fold8(dx)

    cur = lambda b, n: (b * nb + n, 0)
    prev = lambda b, n: (b * nb + jnp.maximum(n - 1, 0), 0)
    nxt = lambda b, n: (b * nb + jnp.minimum(n + 1, nb - 1), 0)
    stat_c = pl.BlockSpec((n_heads, WINDOW), lambda b, n: (0, b * nb + n))
    stat_n = pl.BlockSpec((n_heads, WINDOW), lambda b, n: (0, b * nb + jnp.minimum(n + 1, nb - 1)))
    tab_c = pl.BlockSpec((HEAD_DIM, WINDOW), lambda b, n: (0, n))
    tab_p = pl.BlockSpec((HEAD_DIM, WINDOW), lambda b, n: (0, jnp.maximum(n - 1, 0)))
    tab_n = pl.BlockSpec((HEAD_DIM, WINDOW), lambda b, n: (0, jnp.minimum(n + 1, nb - 1)))
    return pl.pallas_call(
        body,
        name=name,
        grid=(bsz, nb),
        in_specs=[SMEM, pl.BlockSpec((WINDOW, qw), cur), pl.BlockSpec((WINDOW, qw), prev), pl.BlockSpec((WINDOW, qw), nxt),
                  pl.BlockSpec((WINDOW, d), cur), pl.BlockSpec((WINDOW, d), nxt),
                  pl.BlockSpec((WINDOW, d), cur), pl.BlockSpec((WINDOW, d), nxt),
                  stat_c, stat_n, tab_c, tab_c, tab_p, tab_p, tab_n, tab_n],
        out_specs=[pl.BlockSpec((WINDOW, qw), cur), pl.BlockSpec((8, qw), lambda b, n: (0, 0)),
                   pl.BlockSpec((n_heads, WINDOW), lambda b, n: (0, 0))],
        out_shape=[jax.ShapeDtypeStruct((t, qw), BF16), jax.ShapeDtypeStruct((8, qw), F32),
                   jax.ShapeDtypeStruct((n_heads, WINDOW), F32)],
        scratch_shapes=[pltpu.VMEM((qw, WINDOW), F32), pltpu.VMEM((2 * kvw, WINDOW), F32), pltpu.VMEM((d, WINDOW), F32),
                        pltpu.VMEM((d, WINDOW), F32), pltpu.VMEM((d, WINDOW), F32), pltpu.VMEM((d, WINDOW), F32),
                        pltpu.VMEM((d, WINDOW), F32), pltpu.VMEM((qw, WINDOW), F32)],
        compiler_params=_params(("arbitrary", "arbitrary")),
    )(sinks, qkv, qkv, qkv, o, o, do, do, lse, lse, cos_t, sin_t, cos_t, sin_t, cos_t, sin_t)


def _place():
    return lax.axis_index("x"), lax.axis_index("y"), lax.axis_index("c")


def _other_chips(x, y):
    return [(1 - x, y), (x, 1 - y), (1 - x, 1 - y)]


def _gather_weights(shards, axes, name):
    n = len(shards)
    outs = []
    for s, ax in zip(shards, axes):
        shape = list(s.shape)
        shape[ax] *= N_CHIPS
        outs.append(jax.ShapeDtypeStruct(tuple(shape), s.dtype))

    def body(*refs):
        src, dst = refs[:n], refs[n:2 * n]
        send_sems, recv_sems, local_sems = refs[2 * n:]
        x, y, c = _place()
        chips = _other_chips(x, y)

        def block(i, px, py):
            width = src[i].shape[axes[i]]
            start = pl.multiple_of((2 * px + py) * width, width)
            return dst[i].at[:, pl.ds(start, width), :] if axes[i] == 1 else dst[i].at[:, :, pl.ds(start, width)]

        def remote(i, k, chip, origin):
            return pltpu.make_async_remote_copy(
                src_ref=src[i], dst_ref=block(i, *origin), send_sem=send_sems.at[i * 3 + k],
                recv_sem=recv_sems.at[i * 3 + k], device_id=(*chip, c), device_id_type=MESH)

        local = [pltpu.make_async_copy(src[i], block(i, x, y), local_sems.at[i]) for i in range(n)]
        sends = [remote(i, k, chip, (x, y)) for i in range(n) for k, chip in enumerate(chips)]
        for cp in local + sends:
            cp.start()
        for i in range(n):
            for k, chip in enumerate(chips):
                remote(i, k, chip, chip).wait_recv()
        for cp in sends:
            cp.wait_send()
        for cp in local:
            cp.wait()

    return pl.pallas_call(
        body,
        name=name,
        in_specs=[ANY] * n,
        out_specs=[ANY] * n,
        out_shape=outs,
        scratch_shapes=[pltpu.SemaphoreType.DMA((3 * n,)), pltpu.SemaphoreType.DMA((3 * n,)), pltpu.SemaphoreType.DMA((n,))],
    )(*shards)


def _half_shape(kind, shape):
    if kind == "col":
        return (shape[0] // 2, shape[1])
    return (N_CHIPS, shape[1] // 2, shape[2])


def _half_of(kind, ref, h):
    if kind == "col":
        r = ref.shape[0] // 2
        return ref.at[pl.ds(pl.multiple_of(h * r, 8), r), :]
    r = ref.shape[1] // 2
    return ref.at[:, pl.ds(pl.multiple_of(h * r, 8), r), :]


def _slice_of_half(kind, ref, s):
    if kind == "col":
        w = ref.shape[1] // N_CHIPS
        return ref.at[:, pl.ds(pl.multiple_of(s * w, 128), w)]
    return ref.at[s]


def _pair_exchange(grads, kinds, name):
    n = len(grads)
    outs = [jax.ShapeDtypeStruct(_half_shape(kd, g.shape), g.dtype) for g, kd in zip(grads, kinds)]

    def body(*refs):
        src, dst = refs[:n], refs[n:2 * n]
        send_sems, recv_sems = refs[2 * n:]
        x, y, c = _place()
        copies = [pltpu.make_async_remote_copy(
            src_ref=_half_of(kinds[i], src[i], 1 - c), dst_ref=dst[i],
            send_sem=send_sems.at[i], recv_sem=recv_sems.at[i], device_id=(x, y, 1 - c), device_id_type=MESH)
            for i in range(n)]
        for cp in copies:
            cp.start()
        for cp in copies:
            cp.wait_recv()
        for cp in copies:
            cp.wait_send()

    return pl.pallas_call(
        body,
        name=name,
        in_specs=[ANY] * n,
        out_specs=[ANY] * n,
        out_shape=outs,
        scratch_shapes=[pltpu.SemaphoreType.DMA((n,)), pltpu.SemaphoreType.DMA((n,))],
    )(*grads)


def _pair_sum(grad, recv, kind, c, name):
    if kind == "col":
        k, n = grad.shape
        rows = k // 2
        tr = _pick(rows, (256, 128, 64, 32, 16, 8))
        steps = rows // tr
        grid = (steps,)
        g_spec = pl.BlockSpec((tr, n), lambda i, c_ref: (c_ref[0] * steps + i, 0))
        r_spec = pl.BlockSpec((tr, n), lambda i, c_ref: (i, 0))
        g_in = grad
    else:
        _, k4, n = grad.shape
        r8 = k4 // 2
        grid = (N_CHIPS,)
        g_spec = pl.BlockSpec((None, None, r8, n), lambda s, c_ref: (s, c_ref[0], 0, 0))
        r_spec = pl.BlockSpec((None, r8, n), lambda s, c_ref: (s, 0, 0))
        g_in = grad.reshape(N_CHIPS, 2, r8, n)

    def body(c_ref, g_ref, r_ref, o_ref):
        del c_ref
        o_ref[...] = (g_ref[...].astype(F32) + r_ref[...].astype(F32)).astype(o_ref.dtype)

    return pl.pallas_call(
        body,
        name=name,
        grid_spec=pltpu.PrefetchScalarGridSpec(num_scalar_prefetch=1, grid=grid, in_specs=[g_spec, r_spec], out_specs=r_spec),
        out_shape=jax.ShapeDtypeStruct(recv.shape, recv.dtype),
        compiler_params=_params(("parallel",)),
    )(c, g_in, recv)


def _chip_scatter(sums, kinds, name):
    n = len(sums)
    outs = []
    for s, kd in zip(sums, kinds):
        shp = (s.shape[0], s.shape[1] // N_CHIPS) if kd == "col" else s.shape[1:]
        outs.append(jax.ShapeDtypeStruct((N_CHIPS,) + tuple(shp), s.dtype))

    def body(*refs):
        src, dst = refs[:n], refs[n:2 * n]
        send_sems, recv_sems, local_sems = refs[2 * n:]
        x, y, c = _place()
        q = 2 * x + y
        chips = _other_chips(x, y)

        def remote(i, k, chip, to_slice, slot):
            return pltpu.make_async_remote_copy(
                src_ref=_slice_of_half(kinds[i], src[i], to_slice), dst_ref=dst[i].at[slot],
                send_sem=send_sems.at[i * 3 + k], recv_sem=recv_sems.at[i * 3 + k],
                device_id=(*chip, c), device_id_type=MESH)

        local = [pltpu.make_async_copy(_slice_of_half(kinds[i], src[i], q), dst[i].at[q], local_sems.at[i]) for i in range(n)]
        sends = [remote(i, k, chip, 2 * chip[0] + chip[1], q) for i in range(n) for k, chip in enumerate(chips)]
        for cp in local + sends:
            cp.start()
        for i in range(n):
            for k, chip in enumerate(chips):
                remote(i, k, chip, q, 2 * chip[0] + chip[1]).wait_recv()
        for cp in sends:
            cp.wait_send()
        for cp in local:
            cp.wait()

    return pl.pallas_call(
        body,
        name=name,
        in_specs=[ANY] * n,
        out_specs=[ANY] * n,
        out_shape=outs,
        scratch_shapes=[pltpu.SemaphoreType.DMA((3 * n,)), pltpu.SemaphoreType.DMA((3 * n,)), pltpu.SemaphoreType.DMA((n,))],
    )(*sums)


def _sum_slots(slots, name):
    _, r, c = slots.shape
    tr = _pick(r, (256, 128, 64, 32, 16, 8))

    def body(s_ref, o_ref):
        acc = s_ref[0].astype(F32)
        for s in range(1, N_CHIPS):
            acc = acc + s_ref[s].astype(F32)
        o_ref[...] = acc

    return pl.pallas_call(
        body,
        name=name,
        grid=(r // tr,),
        in_specs=[pl.BlockSpec((N_CHIPS, tr, c), lambda i: (0, i, 0))],
        out_specs=pl.BlockSpec((tr, c), lambda i: (i, 0)),
        out_shape=jax.ShapeDtypeStruct((r, c), F32),
        compiler_params=_params(("parallel",)),
    )(slots)


def _pair_assemble(halves, name):
    flat = [h for hs in halves for h in hs]
    n, n_out = len(flat), len(halves)
    outs = [jax.ShapeDtypeStruct((len(hs), 2 * hs[0].shape[0], hs[0].shape[1]), F32) for hs in halves]
    owner = [(i, l) for i, hs in enumerate(halves) for l in range(len(hs))]

    def body(*refs):
        src, dst = refs[:n], refs[n:n + n_out]
        send_sems, recv_sems, local_sems = refs[n + n_out:]
        x, y, c = _place()

        def rows(j, h):
            i, l = owner[j]
            r = src[j].shape[0]
            return dst[i].at[l, pl.ds(pl.multiple_of(h * r, 8), r), :]

        def remote(j, h):
            return pltpu.make_async_remote_copy(
                src_ref=src[j], dst_ref=rows(j, h), send_sem=send_sems.at[j], recv_sem=recv_sems.at[j],
                device_id=(x, y, 1 - c), device_id_type=MESH)

        local = [pltpu.make_async_copy(src[j], rows(j, c), local_sems.at[j]) for j in range(n)]
        sends = [remote(j, c) for j in range(n)]
        for cp in local + sends:
            cp.start()
        for j in range(n):
            remote(j, 1 - c).wait_recv()
        for cp in sends:
            cp.wait_send()
        for cp in local:
            cp.wait()

    return pl.pallas_call(
        body,
        name=name,
        in_specs=[ANY] * n,
        out_specs=[ANY] * n_out,
        out_shape=outs,
        scratch_shapes=[pltpu.SemaphoreType.DMA((n,)), pltpu.SemaphoreType.DMA((n,)), pltpu.SemaphoreType.DMA((n,))],
    )(*flat)


def _allreduce_small(v, name):
    r, w = v.shape

    def body(v_ref, o_ref, buf_ref, send_sems, recv_sems):
        x, y, c = _place()
        me = 4 * x + 2 * y + c

        def peer(k):
            return x ^ (k >> 2), y ^ ((k >> 1) & 1), c ^ (k & 1)

        def remote(k, slot):
            return pltpu.make_async_remote_copy(
                src_ref=v_ref, dst_ref=buf_ref.at[slot], send_sem=send_sems.at[k - 1], recv_sem=recv_sems.at[k - 1],
                device_id=peer(k), device_id_type=MESH)

        sends = [remote(k, me) for k in range(1, N_DEV)]
        for cp in sends:
            cp.start()
        buf_ref[me] = v_ref[...]
        for k in range(1, N_DEV):
            px, py, pc = peer(k)
            remote(k, 4 * px + 2 * py + pc).wait_recv()
        for cp in sends:
            cp.wait_send()
        acc = buf_ref[0]
        for dev in range(1, N_DEV):
            acc = acc + buf_ref[dev]
        o_ref[...] = acc

    vm = pl.BlockSpec(memory_space=pltpu.VMEM)
    return pl.pallas_call(
        body,
        name=name,
        in_specs=[vm],
        out_specs=vm,
        out_shape=jax.ShapeDtypeStruct((r, w), F32),
        scratch_shapes=[pltpu.VMEM((N_DEV, r, w), F32), pltpu.SemaphoreType.DMA((N_DEV - 1,)), pltpu.SemaphoreType.DMA((N_DEV - 1,))],
        compiler_params=pltpu.CompilerParams(vmem_limit_bytes=VMEM_LIMIT_BYTES),
    )(v)


def _adamw(w, g, m, v, name):
    ly, r, c = w.shape
    tr = _pick(r, (256, 128, 64, 32, 16, 8))
    c1 = 1.0 - ADAM_B1 ** ADAM_STEP
    c2 = 1.0 - ADAM_B2 ** ADAM_STEP

    def body(w_ref, g_ref, m_ref, v_ref, d_ref, nm_ref, nv_ref):
        gv = g_ref[...]
        nm = ADAM_B1 * m_ref[...] + (1.0 - ADAM_B1) * gv
        nv = ADAM_B2 * v_ref[...] + (1.0 - ADAM_B2) * (gv * gv)
        d_ref[...] = -ADAM_LR * ((nm / c1) / (jnp.sqrt(nv / c2) + ADAM_EPS) + ADAM_WD * w_ref[...])
        nm_ref[...] = nm
        nv_ref[...] = nv

    spec = pl.BlockSpec((None, tr, c), lambda l, i: (l, i, 0))
    shp = jax.ShapeDtypeStruct((ly, r, c), F32)
    return pl.pallas_call(
        body,
        name=name,
        grid=(ly, r // tr),
        in_specs=[spec] * 4,
        out_specs=[spec] * 3,
        out_shape=[shp] * 3,
        compiler_params=_params(("parallel", "parallel")),
    )(w, g, m, v)


def _rope_tables(seq):
    pos = jnp.arange(seq, dtype=F32)
    inv_freq = 1.0 / (ROPE_THETA ** (jnp.arange(0, HEAD_DIM, 2, dtype=F32) / HEAD_DIM))
    ang = (pos[:, None] * inv_freq[None, :]).T
    cos, sin = jnp.cos(ang), jnp.sin(ang)
    return jnp.concatenate([cos, cos], axis=0), jnp.concatenate([-sin, sin], axis=0)


def _pack(vs, fill=0.0):
    p = jnp.concatenate([v.reshape(-1) for v in vs])
    size = -(-p.shape[0] // 8192) * 8192
    return jnp.pad(p, (0, size - p.shape[0]), constant_values=fill).reshape(-1, 1024)


def _unpack(p, like):
    p = p.reshape(-1)
    out, o = [], 0
    for v in like:
        n = int(math.prod(v.shape))
        out.append(p[o:o + n].reshape(v.shape))
        o += n
    return out


def kernel(x, norm_mix, norm_ffn, norm_final, conv_w_in, conv_w_conv, conv_w_out, attn_w_qkv, attn_b_qkv, attn_sinks, attn_w_o, attn_b_o, ffn_w_in, ffn_w_conv, ffn_w_down, loss_target, m_norm_mix, m_norm_ffn, m_norm_final, m_conv_w_in, m_conv_w_conv, m_conv_w_out, m_attn_w_qkv, m_attn_b_qkv, m_attn_sinks, m_attn_w_o, m_attn_b_o, m_ffn_w_in, m_ffn_w_conv, m_ffn_w_down, v_norm_mix, v_norm_ffn, v_norm_final, v_conv_w_in, v_conv_w_conv, v_conv_w_out, v_attn_w_qkv, v_attn_b_qkv, v_attn_sinks, v_attn_w_o, v_attn_b_o, v_ffn_w_in, v_ffn_w_conv, v_ffn_w_down):
    bsz, seq, d = x.shape
    t = bsz * seq
    depth = norm_mix.shape[0]
    n_conv, n_attn = conv_w_in.shape[0], attn_w_qkv.shape[0]
    xq, yq, cq = _place()
    q = 2 * xq + yq

    big = [conv_w_in, conv_w_out, attn_w_qkv, attn_w_o, ffn_w_in, ffn_w_down]
    w_cin, w_cout, w_qkv, w_o, w_fin, w_fdown = _gather_weights(
        [w.astype(BF16) for w in big], [2, 1, 2, 1, 2, 1], "gather_weights")

    small_cols = [conv_w_conv, attn_b_qkv, attn_b_o, ffn_w_conv]

    def placed(v):
        width = v.shape[-1]
        full = jnp.zeros(v.shape[:-1] + (N_CHIPS * width,), F32)
        return lax.dynamic_update_slice_in_dim(full, v * (1.0 / N_CORES), q * width, axis=v.ndim - 1)

    full_cols = [placed(v) for v in small_cols]
    wc_conv, b_qkv, b_o, wf_conv = _unpack(_allreduce_small(_pack(full_cols), "gather_small"), full_cols)
    cos_t, sin_t = _rope_tables(seq)

    xs = x.reshape(t, d)
    saved = []
    for i in range(depth):
        j = i // 2
        h = _rms_fwd(xs, norm_mix[i:i + 1], f"norm_mix_fwd{i}")
        if i % 2 == 0:
            pre = _mm(h, w_cin, "nn", F32, layer=j, tm=1024, tn=768, tk=4096, name=f"conv_in_fwd{i}")
            mixed = _convgate_fwd(pre, wc_conv[j], seq, f"conv_gate_fwd{i}")
            x_mid = _mm(mixed, w_cout, "nn", F32, layer=j, residual=xs, tm=512, tn=1024, tk=4096, name=f"conv_out_fwd{i}")
            lse = None
        else:
            pre = _mm(h, w_qkv, "nn", F32, layer=j, bias=b_qkv[j:j + 1], tm=1024, tn=768, tk=4096, name=f"qkv_fwd{i}")
            mixed, lse = _attn_fwd(pre, attn_sinks[j], cos_t, sin_t, bsz, seq, f"attn_fwd{i}")
            x_mid = _mm(mixed, w_o, "nn", F32, layer=j, bias=b_o[j:j + 1], residual=xs, tm=512, tn=1024, tk=4096,
                        name=f"attn_out_fwd{i}")
        h2 = _rms_fwd(x_mid, norm_ffn[i:i + 1], f"norm_ffn_fwd{i}")
        gu = _mm(h2, w_fin, "nn", F32, layer=i, tm=512, tn=1408, tk=4096, name=f"ffn_in_fwd{i}")
        act = _ffngate_fwd(gu, wf_conv[i], seq, f"ffn_gate_fwd{i}")
        x_next = _mm(act, w_fdown, "nn", F32, layer=i, residual=x_mid, tm=512, tn=1024, tk=4096, name=f"ffn_down_fwd{i}")
        saved.append((xs, h, pre, mixed, lse, x_mid, h2, gu, act))
        xs = x_next

    dx, dxb, sq, dg_final = _loss_head(xs, loss_target.reshape(t, d), norm_final.reshape(1, d), "loss_head")
    loss = lax.psum(0.5 * jnp.sum(sq) / d, ("x", "y", "c"))

    g_norm_mix, g_norm_ffn = [None] * depth, [None] * depth
    g_cin, g_cconv, g_cout = [None] * n_conv, [None] * n_conv, [None] * n_conv
    g_qkv, g_bqkv, g_sinks, g_o, g_bo = ([None] * n_attn for _ in range(5))
    g_fin, g_fconv, g_fdown = [None] * depth, [None] * depth, [None] * depth
    for i in reversed(range(depth)):
        j = i // 2
        x_in, h, pre, mixed, lse, x_mid, h2, gu, act = saved[i]
        da = _mm(dxb, w_fdown, "nt", F32, layer=i, tm=512, tn=1408, tk=4096, name=f"ffn_down_dx{i}")
        g_fdown[i] = _mm(act, dxb, "tn", BF16, tm=1408, tn=1024, tk=512, name=f"ffn_down_dw{i}")
        dgu, dwc = _ffngate_bwd(gu, da, wf_conv[i], seq, f"ffn_gate_bwd{i}")
        g_fconv[i] = dwc[:3]
        g_fin[i] = _mm(h2, dgu, "tn", BF16, tm=1024, tn=1408, tk=512, name=f"ffn_in_dw{i}")
        dh2 = _mm(dgu, w_fin, "nt", F32, layer=i, tm=512, tn=1024, tk=1408, name=f"ffn_in_dx{i}")
        dx, dxb, dg, colsum = _rms_bwd(x_mid, dh2, norm_ffn[i:i + 1], dx, f"norm_ffn_bwd{i}")
        g_norm_ffn[i] = jnp.sum(dg, axis=0)
        if i % 2 == 0:
            dmix = _mm(dxb, w_cout, "nt", F32, layer=j, tm=512, tn=1024, tk=4096, name=f"conv_out_dx{i}")
            g_cout[j] = _mm(mixed, dxb, "tn", BF16, tm=1024, tn=1024, tk=512, name=f"conv_out_dw{i}")
            dpre, dwc = _convgate_bwd(pre, dmix, wc_conv[j], seq, f"conv_gate_bwd{i}")
            g_cconv[j] = dwc[:3]
            g_cin[j] = _mm(h, dpre, "tn", BF16, tm=1024, tn=768, tk=512, name=f"conv_in_dw{i}")
            dh = _mm(dpre, w_cin, "nt", F32, layer=j, tm=512, tn=1024, tk=1536, name=f"conv_in_dx{i}")
        else:
            g_bo[j] = jnp.sum(colsum, axis=0)
            dmix = _mm(dxb, w_o, "nt", F32, layer=j, tm=512, tn=1024, tk=4096, name=f"attn_out_dx{i}")
            g_o[j] = _mm(mixed, dxb, "tn", BF16, tm=1024, tn=1024, tk=512, name=f"attn_out_dw{i}")
            dpre, dbias, dsk = _attn_bwd(pre, mixed, lse, dmix, attn_sinks[j], cos_t, sin_t, bsz, seq, f"attn_bwd{i}")
            g_bqkv[j] = jnp.sum(dbias, axis=0)
            g_sinks[j] = jnp.sum(dsk, axis=1)
            g_qkv[j] = _mm(h, dpre, "tn", BF16, tm=1024, tn=768, tk=512, name=f"qkv_dw{i}")
            dh = _mm(dpre, w_qkv, "nt", F32, layer=j, tm=512, tn=1024, tk=1536, name=f"qkv_dx{i}")
        dx, dxb, dg, _ = _rms_bwd(x_in, dh, norm_mix[i:i + 1], dx, f"norm_mix_bwd{i}")
        g_norm_mix[i] = jnp.sum(dg, axis=0)
    grad_x = dx.reshape(bsz, seq, d)

    tensors = [(g_cin, "col"), (g_cout, "row"), (g_qkv, "col"), (g_o, "row"), (g_fin, "col"), (g_fdown, "row")]
    flat, kinds = [], []
    for gs, kd in tensors:
        for g in gs:
            flat.append(g if kd == "col" else g.reshape(N_CHIPS, g.shape[0] // N_CHIPS, g.shape[1]))
            kinds.append(kd)
    c_arr = cq.astype(jnp.int32).reshape(1)
    recv = _pair_exchange(flat, kinds, "grad_pair_exchange")
    sums = [_pair_sum(g, r, kd, c_arr, f"grad_pair_sum{n}") for n, (g, r, kd) in enumerate(zip(flat, recv, kinds))]
    slots = _chip_scatter(sums, kinds, "grad_chip_scatter")
    halves, pos = [], 0
    for gs, _ in tensors:
        halves.append([_sum_slots(slots[pos + l], f"grad_chip_sum{pos + l}") for l in range(len(gs))])
        pos += len(gs)
    gw_cin, gw_cout, gw_qkv, gw_o, gw_fin, gw_fdown = _pair_assemble(halves, "grad_pair_assemble")

    small = [jnp.stack(g_norm_mix), jnp.stack(g_norm_ffn), jnp.sum(dg_final, axis=0), jnp.stack(g_cconv),
             jnp.stack(g_bqkv), jnp.stack(g_sinks), jnp.stack(g_bo), jnp.stack(g_fconv)]
    sg = _unpack(_allreduce_small(_pack(small), "grad_small_allreduce"), small)

    def my_cols(v, like):
        width = like.shape[-1]
        return lax.dynamic_slice_in_dim(v, q * width, width, axis=v.ndim - 1)

    small_w = [norm_mix, norm_ffn, norm_final, conv_w_conv, attn_b_qkv, attn_sinks, attn_b_o, ffn_w_conv]
    small_m = [m_norm_mix, m_norm_ffn, m_norm_final, m_conv_w_conv, m_attn_b_qkv, m_attn_sinks, m_attn_b_o, m_ffn_w_conv]
    small_v = [v_norm_mix, v_norm_ffn, v_norm_final, v_conv_w_conv, v_attn_b_qkv, v_attn_sinks, v_attn_b_o, v_ffn_w_conv]
    small_g = [sg[0], sg[1], sg[2], my_cols(sg[3], conv_w_conv), my_cols(sg[4], attn_b_qkv), sg[5],
               my_cols(sg[6], attn_b_o), my_cols(sg[7], ffn_w_conv)]

    upd = {
        "conv_w_in": _adamw(conv_w_in, gw_cin, m_conv_w_in, v_conv_w_in, "adamw_conv_in"),
        "conv_w_out": _adamw(conv_w_out, gw_cout, m_conv_w_out, v_conv_w_out, "adamw_conv_out"),
        "attn_w_qkv": _adamw(attn_w_qkv, gw_qkv, m_attn_w_qkv, v_attn_w_qkv, "adamw_qkv"),
        "attn_w_o": _adamw(attn_w_o, gw_o, m_attn_w_o, v_attn_w_o, "adamw_attn_out"),
        "ffn_w_in": _adamw(ffn_w_in, gw_fin, m_ffn_w_in, v_ffn_w_in, "adamw_ffn_in"),
        "ffn_w_down": _adamw(ffn_w_down, gw_fdown, m_ffn_w_down, v_ffn_w_down, "adamw_ffn_down"),
    }
    sd, sm, sv = _adamw(_pack(small_w)[None], _pack(small_g)[None], _pack(small_m)[None], _pack(small_v, 1.0)[None],
                        "adamw_small")
    sd, sm, sv = _unpack(sd, small_w), _unpack(sm, small_w), _unpack(sv, small_w)
    names = ["norm_mix", "norm_ffn", "norm_final", "conv_w_in", "conv_w_conv", "conv_w_out", "attn_w_qkv", "attn_b_qkv",
             "attn_sinks", "attn_w_o", "attn_b_o", "ffn_w_in", "ffn_w_conv", "ffn_w_down"]
    small_names = ["norm_mix", "norm_ffn", "norm_final", "conv_w_conv", "attn_b_qkv", "attn_sinks", "attn_b_o", "ffn_w_conv"]
    grads = dict(zip(small_names, small_g))
    grads.update(conv_w_in=gw_cin, conv_w_out=gw_cout, attn_w_qkv=gw_qkv, attn_w_o=gw_o, ffn_w_in=gw_fin, ffn_w_down=gw_fdown)
    for n, nm in enumerate(small_names):
        upd[nm] = (sd[n], sm[n], sv[n])
    return (loss, grad_x, *[grads[nm] for nm in names], *[upd[nm][0] for nm in names],
            *[upd[nm][1] for nm in names], *[upd[nm][2] for nm in names])
```

```python
import math

import jax
import jax.numpy as jnp
from jax import lax
from jax.experimental import pallas as pl
from jax.experimental.pallas import tpu as pltpu

F32 = jnp.float32
BF16 = jnp.bfloat16

HEAD_DIM = 64
GROUP = 4
WINDOW = 128
EPS = 1e-5
ROPE_THETA = 10000.0
ADAM_LR, ADAM_B1, ADAM_B2, ADAM_EPS, ADAM_WD, ADAM_STEP = 0.001, 0.9, 0.999, 1e-08, 0.01, 10

N_CHIPS = 4
N_CORES = 2
N_DEV = 8
HALO = 8
VMEM_LIMIT_BYTES = 56 * 1024 * 1024
MESH = pl.DeviceIdType.MESH
ANY = pl.BlockSpec(memory_space=pl.ANY)
SMEM = pl.BlockSpec(memory_space=pltpu.SMEM)
NEG = float(jnp.finfo(jnp.float32).min)
ROW_TILES = (512, 256, 128, 64, 32, 16, 8)


def _pick(dim, cands):
    for c in cands:
        if dim % c == 0:
            return c
    return dim


def _params(sem):
    return pltpu.CompilerParams(dimension_semantics=sem, vmem_limit_bytes=VMEM_LIMIT_BYTES)


_DIMS = {"nn": (((1,), (0,)), ((), ())), "nt": (((1,), (1,)), ((), ())), "tn": (((0,), (0,)), ((), ()))}


def _mm(a, b, mode, out_dtype, *, layer=None, bias=None, residual=None, n_outer=False, tm, tn, tk, name):
    b2 = b.shape[1:] if layer is not None else b.shape
    if mode == "nn":
        (m, k), n = a.shape, b2[1]
    elif mode == "nt":
        (m, k), n = a.shape, b2[0]
    else:
        (k, m), n = a.shape, b2[1]
    tm, tn, tk = min(tm, m), min(tn, n), min(tk, k)
    assert m % tm == 0 and n % tn == 0 and k % tk == 0, (name, a.shape, b.shape, tm, tn, tk)
    nk = k // tk

    def at(f):
        return (lambda p0, p1, p2: f(p1, p0, p2)) if n_outer else f

    a_spec = pl.BlockSpec((tk, tm), at(lambda i, j, l: (l, i))) if mode == "tn" else pl.BlockSpec((tm, tk), at(lambda i, j, l: (i, l)))
    if layer is None:
        b_spec = (pl.BlockSpec((tn, tk), at(lambda i, j, l: (j, l))) if mode == "nt"
                  else pl.BlockSpec((tk, tn), at(lambda i, j, l: (l, j))))
    elif mode == "nt":
        b_spec = pl.BlockSpec((None, tn, tk), at(lambda i, j, l: (layer, j, l)))
    else:
        b_spec = pl.BlockSpec((None, tk, tn), at(lambda i, j, l: (layer, l, j)))
    in_specs, args = [a_spec, b_spec], [a, b]
    if bias is not None:
        in_specs.append(pl.BlockSpec((1, tn), at(lambda i, j, l: (0, j))))
        args.append(bias)
    if residual is not None:
        in_specs.append(pl.BlockSpec((tm, tn), at(lambda i, j, l: (i, j))))
        args.append(residual)
    has_bias, has_res = bias is not None, residual is not None

    def body(*refs):
        a_ref, b_ref = refs[0], refs[1]
        pos = 2
        bias_ref = res_ref = None
        if has_bias:
            bias_ref, pos = refs[pos], pos + 1
        if has_res:
            res_ref, pos = refs[pos], pos + 1
        o_ref = refs[pos]
        acc_ref = refs[pos + 1] if nk > 1 else None

        def finish(acc):
            if has_bias:
                acc = acc + bias_ref[...]
            if has_res:
                acc = acc + res_ref[...]
            o_ref[...] = acc.astype(o_ref.dtype)

        part = lax.dot_general(a_ref[...], b_ref[...], _DIMS[mode], preferred_element_type=F32)
        if nk == 1:
            finish(part)
        else:
            l = pl.program_id(2)

            @pl.when(l == 0)
            def _():
                acc_ref[...] = part

            @pl.when(l > 0)
            def _():
                acc_ref[...] += part

            @pl.when(l == nk - 1)
            def _():
                finish(acc_ref[...])

    return pl.pallas_call(
        body,
        name=name,
        grid=(n // tn, m // tm, nk) if n_outer else (m // tm, n // tn, nk),
        in_specs=in_specs,
        out_specs=pl.BlockSpec((tm, tn), at(lambda i, j, l: (i, j))),
        out_shape=jax.ShapeDtypeStruct((m, n), out_dtype),
        scratch_shapes=[pltpu.VMEM((tm, tn), F32)] if nk > 1 else [],
        compiler_params=_params(("parallel", "parallel", "arbitrary")),
    )(*args)


def _fold8(v):
    r, d = v.shape
    return jnp.sum(v.reshape(r // 8, 8, d), axis=0)


def _rms_fwd(x, g, name):
    t, d = x.shape
    tm = _pick(t, ROW_TILES)

    def body(x_ref, g_ref, h_ref):
        xv = x_ref[...]
        r = lax.rsqrt(jnp.mean(xv * xv, axis=-1, keepdims=True) + EPS)
        h_ref[...] = (xv * r * g_ref[...]).astype(BF16)

    return pl.pallas_call(
        body,
        name=name,
        grid=(t // tm,),
        in_specs=[pl.BlockSpec((tm, d), lambda i: (i, 0)), pl.BlockSpec((1, d), lambda i: (0, 0))],
        out_specs=pl.BlockSpec((tm, d), lambda i: (i, 0)),
        out_shape=jax.ShapeDtypeStruct((t, d), BF16),
        compiler_params=_params(("parallel",)),
    )(x, g)


def _rms_bwd(x, dh, g, dx_in, name):
    t, d = x.shape
    tm = _pick(t, ROW_TILES)

    def body(x_ref, dh_ref, g_ref, dxi_ref, dx_ref, dxb_ref, dg_ref, cs_ref):
        i = pl.program_id(0)
        xv = x_ref[...]
        r = lax.rsqrt(jnp.mean(xv * xv, axis=-1, keepdims=True) + EPS)
        xhat = xv * r
        dy = dh_ref[...]
        gdy = dy * g_ref[...]
        dx = dxi_ref[...] + r * (gdy - xhat * jnp.mean(gdy * xhat, axis=-1, keepdims=True))
        dx_ref[...] = dx
        dxb_ref[...] = dx.astype(BF16)

        @pl.when(i == 0)
        def _():
            dg_ref[...] = jnp.zeros_like(dg_ref)
            cs_ref[...] = jnp.zeros_like(cs_ref)

        dg_ref[...] += _fold8(dy * xhat)
        cs_ref[...] += _fold8(dx)

    row = pl.BlockSpec((tm, d), lambda i: (i, 0))
    acc = pl.BlockSpec((8, d), lambda i: (0, 0))
    return pl.pallas_call(
        body,
        name=name,
        grid=(t // tm,),
        in_specs=[row, row, pl.BlockSpec((1, d), lambda i: (0, 0)), row],
        out_specs=[row, row, acc, acc],
        out_shape=[jax.ShapeDtypeStruct((t, d), F32), jax.ShapeDtypeStruct((t, d), BF16),
                   jax.ShapeDtypeStruct((8, d), F32), jax.ShapeDtypeStruct((8, d), F32)],
        compiler_params=_params(("arbitrary",)),
    )(x, dh, g, dx_in)


def _loss_head(x, target, g, name):
    t, d = x.shape
    tm = _pick(t, ROW_TILES)
    inv_d = 1.0 / d

    def body(x_ref, t_ref, g_ref, dx_ref, dxb_ref, sq_ref, dg_ref):
        i = pl.program_id(0)
        xv = x_ref[...]
        gv = g_ref[...]
        r = lax.rsqrt(jnp.mean(xv * xv, axis=-1, keepdims=True) + EPS)
        xhat = xv * r
        err = xhat * gv - t_ref[...]
        dy = err * inv_d
        gdy = dy * gv
        dx = r * (gdy - xhat * jnp.mean(gdy * xhat, axis=-1, keepdims=True))
        dx_ref[...] = dx
        dxb_ref[...] = dx.astype(BF16)

        @pl.when(i == 0)
        def _():
            sq_ref[...] = jnp.zeros_like(sq_ref)
            dg_ref[...] = jnp.zeros_like(dg_ref)

        sq_ref[...] += _fold8(err * err)
        dg_ref[...] += _fold8(dy * xhat)

    row = pl.BlockSpec((tm, d), lambda i: (i, 0))
    acc = pl.BlockSpec((8, d), lambda i: (0, 0))
    return pl.pallas_call(
        body,
        name=name,
        grid=(t // tm,),
        in_specs=[row, row, pl.BlockSpec((1, d), lambda i: (0, 0))],
        out_specs=[row, row, acc, acc],
        out_shape=[jax.ShapeDtypeStruct((t, d), F32), jax.ShapeDtypeStruct((t, d), BF16),
                   jax.ShapeDtypeStruct((8, d), F32), jax.ShapeDtypeStruct((8, d), F32)],
        compiler_params=_params(("arbitrary",)),
    )(x, target, g)


def _rows(tm):
    return lax.broadcasted_iota(jnp.int32, (tm, 1), 0)


def _shift_down(u, before2, rows):
    s1 = jnp.where(rows == 0, before2[1:2], pltpu.roll(u, 1, 0))
    s2 = jnp.where(rows == 0, before2[0:1], jnp.where(rows == 1, before2[1:2], pltpu.roll(u, 2, 0)))
    return s1, s2


def _shift_up(u, after2, rows):
    tm = u.shape[0]
    s1 = jnp.where(rows == tm - 1, after2[0:1], pltpu.roll(u, tm - 1, 0))
    s2 = jnp.where(rows == tm - 2, after2[0:1], jnp.where(rows == tm - 1, after2[1:2], pltpu.roll(u, tm - 2, 0)))
    return s1, s2


def _conv_tile(seq):
    return _pick(seq, (256, 128, 64, 32, 16, 8))


def _halo_specs(tm, width, n_tiles):
    per = tm // HALO
    before = pl.BlockSpec((HALO, width), lambda i: (jnp.maximum(i * per - 1, 0), 0))
    after = pl.BlockSpec((HALO, width), lambda i: (jnp.minimum((i + 1) * per, n_tiles * per - 1), 0))
    return before, after


def _convgate_fwd(bcv, w, seq, name):
    t, d3 = bcv.shape
    d = d3 // 3
    tm = _conv_tile(seq)
    tps = seq // tm
    before, _ = _halo_specs(tm, d3, t // tm)

    def body(x_ref, xb_ref, w_ref, y_ref):
        i = pl.program_id(0)
        inner = (i % tps != 0).astype(F32)
        rows = _rows(tm)
        u = x_ref[:, d:2 * d] * x_ref[:, 2 * d:]
        ub = xb_ref[HALO - 2:, d:2 * d] * xb_ref[HALO - 2:, 2 * d:] * inner
        s1, s2 = _shift_down(u, ub, rows)
        z = w_ref[2:3] * u + w_ref[1:2] * s1 + w_ref[0:1] * s2
        y_ref[...] = (x_ref[:, :d] * z).astype(BF16)

    return pl.pallas_call(
        body,
        name=name,
        grid=(t // tm,),
        in_specs=[pl.BlockSpec((tm, d3), lambda i: (i, 0)), before, pl.BlockSpec((3, d), lambda i: (0, 0))],
        out_specs=pl.BlockSpec((tm, d), lambda i: (i, 0)),
        out_shape=jax.ShapeDtypeStruct((t, d), BF16),
        compiler_params=_params(("parallel",)),
    )(bcv, bcv, w)


def _convgate_bwd(bcv, dy, w, seq, name):
    t, d3 = bcv.shape
    d = d3 // 3
    tm = _conv_tile(seq)
    tps = seq // tm
    before, after = _halo_specs(tm, d3, t // tm)
    _, after_dy = _halo_specs(tm, d, t // tm)

    def body(x_ref, xb_ref, xa_ref, dy_ref, dya_ref, w_ref, dx_ref, dw_ref):
        i = pl.program_id(0)
        inner_lo = (i % tps != 0).astype(F32)
        inner_hi = (i % tps != tps - 1).astype(F32)
        rows = _rows(tm)
        w0, w1, w2 = w_ref[0:1], w_ref[1:2], w_ref[2:3]
        b, c, v = x_ref[:, :d], x_ref[:, d:2 * d], x_ref[:, 2 * d:]
        u = c * v
        ub = xb_ref[HALO - 2:, d:2 * d] * xb_ref[HALO - 2:, 2 * d:] * inner_lo
        s1, s2 = _shift_down(u, ub, rows)
        z = w2 * u + w1 * s1 + w0 * s2
        dyv = dy_ref[...]
        dz = dyv * b
        dza = dya_ref[0:2] * xa_ref[0:2, :d] * inner_hi
        n1, n2 = _shift_up(dz, dza, rows)
        du = w2 * dz + w1 * n1 + w0 * n2
        dx_ref[:, :d] = (dyv * z).astype(BF16)
        dx_ref[:, d:2 * d] = (du * v).astype(BF16)
        dx_ref[:, 2 * d:] = (du * c).astype(BF16)

        @pl.when(i == 0)
        def _():
            dw_ref[...] = jnp.zeros_like(dw_ref)

        dw_ref[0:1] += jnp.sum(dz * s2, axis=0, keepdims=True)
        dw_ref[1:2] += jnp.sum(dz * s1, axis=0, keepdims=True)
        dw_ref[2:3] += jnp.sum(dz * u, axis=0, keepdims=True)

    return pl.pallas_call(
        body,
        name=name,
        grid=(t // tm,),
        in_specs=[pl.BlockSpec((tm, d3), lambda i: (i, 0)), before, after,
                  pl.BlockSpec((tm, d), lambda i: (i, 0)), after_dy, pl.BlockSpec((3, d), lambda i: (0, 0))],
        out_specs=[pl.BlockSpec((tm, d3), lambda i: (i, 0)), pl.BlockSpec((8, d), lambda i: (0, 0))],
        out_shape=[jax.ShapeDtypeStruct((t, d3), BF16), jax.ShapeDtypeStruct((8, d), F32)],
        compiler_params=_params(("arbitrary",)),
    )(bcv, bcv, bcv, dy, dy, w)


def _sigmoid(x):
    return 1.0 / (1.0 + jnp.exp(-x))


def _ffngate_fwd(gu, w, seq, name):
    t, f2 = gu.shape
    f = f2 // 2
    tm = _conv_tile(seq)
    tps = seq // tm
    before, _ = _halo_specs(tm, f2, t // tm)

    def body(x_ref, xb_ref, w_ref, a_ref):
        i = pl.program_id(0)
        inner = (i % tps != 0).astype(F32)
        rows = _rows(tm)
        g = x_ref[:, :f]
        gb = xb_ref[HALO - 2:, :f] * inner
        s1, s2 = _shift_down(g, gb, rows)
        gc = w_ref[2:3] * g + w_ref[1:2] * s1 + w_ref[0:1] * s2
        a_ref[...] = (gc * _sigmoid(gc) * x_ref[:, f:]).astype(BF16)

    return pl.pallas_call(
        body,
        name=name,
        grid=(t // tm,),
        in_specs=[pl.BlockSpec((tm, f2), lambda i: (i, 0)), before, pl.BlockSpec((3, f), lambda i: (0, 0))],
        out_specs=pl.BlockSpec((tm, f), lambda i: (i, 0)),
        out_shape=jax.ShapeDtypeStruct((t, f), BF16),
        compiler_params=_params(("parallel",)),
    )(gu, gu, w)


def _ffngate_bwd(gu, da, w, seq, name):
    t, f2 = gu.shape
    f = f2 // 2
    tm = _conv_tile(seq)
    tps = seq // tm
    before, after = _halo_specs(tm, f2, t // tm)
    _, after_da = _halo_specs(tm, f, t // tm)

    def body(x_ref, xb_ref, xa_ref, da_ref, daa_ref, w_ref, dx_ref, dw_ref):
        i = pl.program_id(0)
        inner_lo = (i % tps != 0).astype(F32)
        inner_hi = (i % tps != tps - 1).astype(F32)
        rows = _rows(tm)
        w0, w1, w2 = w_ref[0:1], w_ref[1:2], w_ref[2:3]

        def dgate(gc, uv, dav):
            sg = _sigmoid(gc)
            return dav * uv * (sg * (1.0 + gc * (1.0 - sg))), dav * (gc * sg)

        g, u = x_ref[:, :f], x_ref[:, f:]
        gb = xb_ref[HALO - 2:, :f] * inner_lo
        s1, s2 = _shift_down(g, gb, rows)
        gc = w2 * g + w1 * s1 + w0 * s2
        dgc, du = dgate(gc, u, da_ref[...])
        ga = xa_ref[:, :f]
        a1, a2 = _shift_down(ga, x_ref[tm - 2:, :f], _rows(HALO))
        gca = w2 * ga + w1 * a1 + w0 * a2
        dgca, _ = dgate(gca, xa_ref[:, f:], daa_ref[...])
        n1, n2 = _shift_up(dgc, dgca[0:2] * inner_hi, rows)
        dx_ref[:, :f] = (w2 * dgc + w1 * n1 + w0 * n2).astype(BF16)
        dx_ref[:, f:] = du.astype(BF16)

        @pl.when(i == 0)
        def _():
            dw_ref[...] = jnp.zeros_like(dw_ref)

        dw_ref[0:1] += jnp.sum(dgc * s2, axis=0, keepdims=True)
        dw_ref[1:2] += jnp.sum(dgc * s1, axis=0, keepdims=True)
        dw_ref[2:3] += jnp.sum(dgc * g, axis=0, keepdims=True)

    return pl.pallas_call(
        body,
        name=name,
        grid=(t // tm,),
        in_specs=[pl.BlockSpec((tm, f2), lambda i: (i, 0)), before, after,
                  pl.BlockSpec((tm, f), lambda i: (i, 0)), after_da, pl.BlockSpec((3, f), lambda i: (0, 0))],
        out_specs=[pl.BlockSpec((tm, f2), lambda i: (i, 0)), pl.BlockSpec((8, f), lambda i: (0, 0))],
        out_shape=[jax.ShapeDtypeStruct((t, f2), BF16), jax.ShapeDtypeStruct((8, f), F32)],
        compiler_params=_params(("arbitrary",)),
    )(gu, gu, gu, da, da, w)


def _swap_halves(xt):
    half = HEAD_DIM // 2
    return jnp.concatenate([xt[half:], xt[:half]], axis=0)


def _rope(xt, cos, sin):
    return xt * cos + _swap_halves(xt) * sin


def _unrope(dxt, cos, sin):
    return dxt * cos - _swap_halves(dxt) * sin


def _band_masks(n):
    kj = lax.broadcasted_iota(jnp.int32, (WINDOW, WINDOW), 0)
    qi = lax.broadcasted_iota(jnp.int32, (WINDOW, WINDOW), 1)
    return kj <= qi, jnp.logical_and(kj > qi, n > 0)


def _tn(a, b):
    return lax.dot_general(a, b, _DIMS["tn"], preferred_element_type=F32)


def _nt(a, b):
    return lax.dot_general(a, b, _DIMS["nt"], preferred_element_type=F32)


def _nn(a, b):
    return jnp.dot(a, b, preferred_element_type=F32)


def _attn_fwd(qkv, sinks, cos_t, sin_t, bsz, seq, name):
    t, qw = qkv.shape
    d = qw * 2 // 3
    kvw = d // GROUP
    n_heads, n_kv = d // HEAD_DIM, kvw // HEAD_DIM
    nb = seq // WINDOW
    scale = HEAD_DIM ** -0.5

    def body(sink_ref, xc_ref, xp_ref, cc_ref, sc_ref, cp_ref, sp_ref, o_ref, lse_ref, xt_ref, pt_ref, ot_ref):
        n = pl.program_id(1)
        xt_ref[...] = xc_ref[...].T
        pt_ref[...] = xp_ref[:, d:].T
        cos_c, sin_c, cos_p, sin_p = cc_ref[...], sc_ref[...], cp_ref[...], sp_ref[...]
        valid_c, valid_p = _band_masks(n)
        for j in range(n_kv):
            ko = j * HEAD_DIM
            kc = _rope(xt_ref[d + ko:d + ko + HEAD_DIM, :], cos_c, sin_c).astype(BF16)
            kp = _rope(pt_ref[ko:ko + HEAD_DIM, :], cos_p, sin_p).astype(BF16)
            vc = xt_ref[d + kvw + ko:d + kvw + ko + HEAD_DIM, :].astype(BF16)
            vp = pt_ref[kvw + ko:kvw + ko + HEAD_DIM, :].astype(BF16)
            for g in range(GROUP):
                h = j * GROUP + g
                qo = h * HEAD_DIM
                q = _rope(xt_ref[qo:qo + HEAD_DIM, :], cos_c, sin_c).astype(BF16)
                s_c = jnp.where(valid_c, _tn(kc, q) * scale, NEG)
                s_p = jnp.where(valid_p, _tn(kp, q) * scale, NEG)
                sink = sink_ref[h]
                m = jnp.maximum(jnp.maximum(jnp.max(s_c, axis=0, keepdims=True),
                                            jnp.max(s_p, axis=0, keepdims=True)), sink)
                p_c = jnp.exp(s_c - m)
                p_p = jnp.exp(s_p - m)
                den = jnp.sum(p_c, axis=0, keepdims=True) + jnp.sum(p_p, axis=0, keepdims=True) + jnp.exp(sink - m)
                inv = 1.0 / den
                ot_ref[qo:qo + HEAD_DIM, :] = (_nn(vc, (p_c * inv).astype(BF16)) + _nn(vp, (p_p * inv).astype(BF16)))
                lse_ref[h:h + 1, :] = m + jnp.log(den)
        o_ref[...] = ot_ref[...].T.astype(BF16)

    cur = lambda b, n: (b * nb + n, 0)
    prev = lambda b, n: (b * nb + jnp.maximum(n - 1, 0), 0)
    tab_c = pl.BlockSpec((HEAD_DIM, WINDOW), lambda b, n: (0, n))
    tab_p = pl.BlockSpec((HEAD_DIM, WINDOW), lambda b, n: (0, jnp.maximum(n - 1, 0)))
    return pl.pallas_call(
        body,
        name=name,
        grid=(bsz, nb),
        in_specs=[SMEM, pl.BlockSpec((WINDOW, qw), cur), pl.BlockSpec((WINDOW, qw), prev), tab_c, tab_c, tab_p, tab_p],
        out_specs=[pl.BlockSpec((WINDOW, d), cur), pl.BlockSpec((n_heads, WINDOW), lambda b, n: (0, b * nb + n))],
        out_shape=[jax.ShapeDtypeStruct((t, d), BF16), jax.ShapeDtypeStruct((n_heads, t), F32)],
        scratch_shapes=[pltpu.VMEM((qw, WINDOW), F32), pltpu.VMEM((2 * kvw, WINDOW), F32), pltpu.VMEM((d, WINDOW), F32)],
        compiler_params=_params(("parallel", "arbitrary")),
    )(sinks, qkv, qkv, cos_t, sin_t, cos_t, sin_t)


def _attn_bwd(qkv, o, lse, do, sinks, cos_t, sin_t, bsz, seq, name):
    t, qw = qkv.shape
    d = qw * 2 // 3
    kvw = d // GROUP
    n_heads, n_kv = d // HEAD_DIM, kvw // HEAD_DIM
    nb = seq // WINDOW
    scale = HEAD_DIM ** -0.5

    def body(sink_ref, xc_ref, xp_ref, xn_ref, oc_ref, on_ref, doc_ref, don_ref, lc_ref, ln_ref,
             cc_ref, sc_ref, cp_ref, sp_ref, cn_ref, sn_ref,
             dx_ref, db_ref, dsk_ref, xt_ref, pt_ref, qn_ref, otc_ref, otn_ref, dtc_ref, dtn_ref, gt_ref):
        b, n = pl.program_id(0), pl.program_id(1)
        xt_ref[...] = xc_ref[...].T
        pt_ref[...] = xp_ref[:, d:].T
        qn_ref[...] = xn_ref[:, :d].T
        otc_ref[...] = oc_ref[...].astype(F32).T
        otn_ref[...] = on_ref[...].astype(F32).T
        dtc_ref[...] = doc_ref[...].T
        dtn_ref[...] = don_ref[...].T
        cos_c, sin_c, cos_p, sin_p, cos_n, sin_n = (cc_ref[...], sc_ref[...], cp_ref[...], sp_ref[...],
                                                    cn_ref[...], sn_ref[...])
        valid_c, valid_p = _band_masks(n)
        kj = lax.broadcasted_iota(jnp.int32, (WINDOW, WINDOW), 0)
        qi = lax.broadcasted_iota(jnp.int32, (WINDOW, WINDOW), 1)
        valid_n = jnp.logical_and(kj > qi, n < nb - 1)

        @pl.when(jnp.logical_and(b == 0, n == 0))
        def _():
            db_ref[...] = jnp.zeros_like(db_ref)
            dsk_ref[...] = jnp.zeros_like(dsk_ref)

        for j in range(n_kv):
            ko = j * HEAD_DIM
            kc = _rope(xt_ref[d + ko:d + ko + HEAD_DIM, :], cos_c, sin_c).astype(BF16)
            kp = _rope(pt_ref[ko:ko + HEAD_DIM, :], cos_p, sin_p).astype(BF16)
            vc = xt_ref[d + kvw + ko:d + kvw + ko + HEAD_DIM, :].astype(BF16)
            vp = pt_ref[kvw + ko:kvw + ko + HEAD_DIM, :].astype(BF16)
            dk = jnp.zeros((HEAD_DIM, WINDOW), F32)
            dv = jnp.zeros((HEAD_DIM, WINDOW), F32)
            for g in range(GROUP):
                h = j * GROUP + g
                qo = h * HEAD_DIM
                sink = sink_ref[h]
                q = _rope(xt_ref[qo:qo + HEAD_DIM, :], cos_c, sin_c).astype(BF16)
                do_h = dtc_ref[qo:qo + HEAD_DIM, :]
                do_b = do_h.astype(BF16)
                lse_h = lc_ref[h:h + 1, :]
                delta = jnp.sum(otc_ref[qo:qo + HEAD_DIM, :] * do_h, axis=0, keepdims=True)
                p_c = jnp.exp(jnp.where(valid_c, _tn(kc, q) * scale, NEG) - lse_h)
                p_p = jnp.exp(jnp.where(valid_p, _tn(kp, q) * scale, NEG) - lse_h)
                ds_c = (p_c * (_tn(vc, do_b) - delta)).astype(BF16)
                ds_p = (p_p * (_tn(vp, do_b) - delta)).astype(BF16)
                dq = (_nn(kc, ds_c) + _nn(kp, ds_p)) * scale
                gt_ref[qo:qo + HEAD_DIM, :] = _unrope(dq, cos_c, sin_c)
                dsk_ref[h:h + 1, :] += -jnp.exp(sink - lse_h) * delta
                dv += _nt(do_b, p_c.astype(BF16))
                dk += _nt(q, ds_c)
                q2 = _rope(qn_ref[qo:qo + HEAD_DIM, :], cos_n, sin_n).astype(BF16)
                do2 = dtn_ref[qo:qo + HEAD_DIM, :]
                do2_b = do2.astype(BF16)
                delta2 = jnp.sum(otn_ref[qo:qo + HEAD_DIM, :] * do2, axis=0, keepdims=True)
                p_n = jnp.exp(jnp.where(valid_n, _tn(kc, q2) * scale, NEG) - ln_ref[h:h + 1, :])
                ds_n = (p_n * (_tn(vc, do2_b) - delta2)).astype(BF16)
                dv += _nt(do2_b, p_n.astype(BF16))
                dk += _nt(q2, ds_n)
            gt_ref[d + ko:d + ko + HEAD_DIM, :] = _unrope(dk * scale, cos_c, sin_c)
            gt_ref[d + kvw + ko:d + kvw + ko + HEAD_DIM, :] = dv
        dx = gt_ref[...].T
        dx_ref[...] = dx.astype(BF16)
        db_ref[...] += _fold8(dx)

    cur = lambda b, n: (b * nb + n, 0)
    prev = lambda b, n: (b * nb + jnp.maximum(n - 1, 0), 0)
    nxt = lambda b, n: (b * nb + jnp.minimum(n + 1, nb - 1), 0)
    stat_c = pl.BlockSpec((n_heads, WINDOW), lambda b, n: (0, b * nb + n))
    stat_n = pl.BlockSpec((n_heads, WINDOW), lambda b, n: (0, b * nb + jnp.minimum(n + 1, nb - 1)))
    tab_c = pl.BlockSpec((HEAD_DIM, WINDOW), lambda b, n: (0, n))
    tab_p = pl.BlockSpec((HEAD_DIM, WINDOW), lambda b, n: (0, jnp.maximum(n - 1, 0)))
    tab_n = pl.BlockSpec((HEAD_DIM, WINDOW), lambda b, n: (0, jnp.minimum(n + 1, nb - 1)))
    return pl.pallas_call(
        body,
        name=name,
        grid=(bsz, nb),
        in_specs=[SMEM, pl.BlockSpec((WINDOW, qw), cur), pl.BlockSpec((WINDOW, qw), prev), pl.BlockSpec((WINDOW, qw), nxt),
                  pl.BlockSpec((WINDOW, d), cur), pl.BlockSpec((WINDOW, d), nxt),
                  pl.BlockSpec((WINDOW, d), cur), pl.BlockSpec((WINDOW, d), nxt),
                  stat_c, stat_n, tab_c, tab_c, tab_p, tab_p, tab_n, tab_n],
        out_specs=[pl.BlockSpec((WINDOW, qw), cur), pl.BlockSpec((8, qw), lambda b, n: (0, 0)),
                   pl.BlockSpec((n_heads, WINDOW), lambda b, n: (0, 0))],
        out_shape=[jax.ShapeDtypeStruct((t, qw), BF16), jax.ShapeDtypeStruct((8, qw), F32),
                   jax.ShapeDtypeStruct((n_heads, WINDOW), F32)],
        scratch_shapes=[pltpu.VMEM((qw, WINDOW), F32), pltpu.VMEM((2 * kvw, WINDOW), F32), pltpu.VMEM((d, WINDOW), F32),
                        pltpu.VMEM((d, WINDOW), F32), pltpu.VMEM((d, WINDOW), F32), pltpu.VMEM((d, WINDOW), F32),
                        pltpu.VMEM((d, WINDOW), F32), pltpu.VMEM((qw, WINDOW), F32)],
        compiler_params=_params(("arbitrary", "arbitrary")),
    )(sinks, qkv, qkv, qkv, o, o, do, do, lse, lse, cos_t, sin_t, cos_t, sin_t, cos_t, sin_t)


def _place():
    return lax.axis_index("x"), lax.axis_index("y"), lax.axis_index("c")


def _other_chips(x, y):
    return [(1 - x, y), (x, 1 - y), (1 - x, 1 - y)]


def _place_shard(w, axis, q, name):
    ly, k, n = w.shape
    tr = _pick(k, (256, 128, 64, 32, 16, 8))
    steps = k // tr
    shape = (ly, k * N_CHIPS, n) if axis == 1 else (ly, k, n * N_CHIPS)
    if axis == 1:
        out_spec = pl.BlockSpec((None, tr, n), lambda l, i, q_ref: (l, q_ref[0] * steps + i, 0))
    else:
        out_spec = pl.BlockSpec((None, tr, n), lambda l, i, q_ref: (l, i, q_ref[0]))

    def body(q_ref, w_ref, o_ref):
        del q_ref
        o_ref[...] = w_ref[...].astype(BF16)

    return pl.pallas_call(
        body,
        name=name,
        grid_spec=pltpu.PrefetchScalarGridSpec(
            num_scalar_prefetch=1, grid=(ly, steps),
            in_specs=[pl.BlockSpec((None, tr, n), lambda l, i, q_ref: (l, i, 0))], out_specs=out_spec),
        out_shape=jax.ShapeDtypeStruct(shape, BF16),
        compiler_params=_params(("parallel", "parallel")),
    )(q, w)


def _gather_weights(fulls, axes, name):
    n = len(fulls)

    def body(*refs):
        dst = refs[n:2 * n]
        send_sems, recv_sems = refs[2 * n:]
        x, y, c = _place()
        chips = _other_chips(x, y)

        def half(i, px, py, pc):
            ref, blk = dst[i], 2 * px + py
            if axes[i] == 1:
                rows = ref.shape[1] // (2 * N_CHIPS)
                return ref.at[:, pl.ds(pl.multiple_of((2 * blk + pc) * rows, 8), rows), :]
            rows, width = ref.shape[1] // 2, ref.shape[2] // N_CHIPS
            return ref.at[:, pl.ds(pl.multiple_of(pc * rows, 8), rows), pl.ds(pl.multiple_of(blk * width, 128), width)]

        def copy(i, m, piece, to):
            return pltpu.make_async_remote_copy(
                src_ref=half(i, *piece), dst_ref=half(i, *piece), send_sem=send_sems.at[i * 6 + m],
                recv_sem=recv_sems.at[i * 6 + m], device_id=to, device_id_type=MESH)

        sends = [copy(i, k, (x, y, c), (*chip, c)) for i in range(n) for k, chip in enumerate(chips)]
        for cp in sends:
            cp.start()
        passed = []
        for i in range(n):
            for k, chip in enumerate(chips):
                copy(i, k, (*chip, c), (*chip, c)).wait_recv()
                passed.append(copy(i, 3 + k, (*chip, c), (x, y, 1 - c)))
                passed[-1].start()
        for i in range(n):
            for k, chip in enumerate(chips):
                copy(i, 3 + k, (*chip, 1 - c), (x, y, 1 - c)).wait_recv()
        for cp in sends + passed:
            cp.wait_send()

    return pl.pallas_call(
        body,
        name=name,
        in_specs=[ANY] * n,
        out_specs=[ANY] * n,
        out_shape=[jax.ShapeDtypeStruct(f.shape, f.dtype) for f in fulls],
        input_output_aliases={i: i for i in range(n)},
        scratch_shapes=[pltpu.SemaphoreType.DMA((6 * n,)), pltpu.SemaphoreType.DMA((6 * n,))],
    )(*fulls)


def _half_shape(kind, shape):
    if kind == "col":
        return (shape[0] // 2, shape[1])
    return (N_CHIPS, shape[1] // 2, shape[2])


def _half_of(kind, ref, h):
    if kind == "col":
        r = ref.shape[0] // 2
        return ref.at[pl.ds(pl.multiple_of(h * r, 8), r), :]
    r = ref.shape[1] // 2
    return ref.at[:, pl.ds(pl.multiple_of(h * r, 8), r), :]


def _slice_of_half(kind, ref, s):
    if kind == "col":
        w = ref.shape[1] // N_CHIPS
        return ref.at[:, pl.ds(pl.multiple_of(s * w, 128), w)]
    return ref.at[s]


def _pair_exchange(grads, kinds, name):
    n = len(grads)
    outs = [jax.ShapeDtypeStruct(_half_shape(kd, g.shape), g.dtype) for g, kd in zip(grads, kinds)]

    def body(*refs):
        src, dst = refs[:n], refs[n:2 * n]
        send_sems, recv_sems = refs[2 * n:]
        x, y, c = _place()
        copies = [pltpu.make_async_remote_copy(
            src_ref=_half_of(kinds[i], src[i], 1 - c), dst_ref=dst[i],
            send_sem=send_sems.at[i], recv_sem=recv_sems.at[i], device_id=(x, y, 1 - c), device_id_type=MESH)
            for i in range(n)]
        for cp in copies:
            cp.start()
        for cp in copies:
            cp.wait_recv()
        for cp in copies:
            cp.wait_send()

    return pl.pallas_call(
        body,
        name=name,
        in_specs=[ANY] * n,
        out_specs=[ANY] * n,
        out_shape=outs,
        scratch_shapes=[pltpu.SemaphoreType.DMA((n,)), pltpu.SemaphoreType.DMA((n,))],
    )(*grads)


def _pair_sum(grad, recv, kind, c, name):
    if kind == "col":
        k, n = grad.shape
        rows = k // 2
        tr = _pick(rows, (256, 128, 64, 32, 16, 8))
        steps = rows // tr
        grid = (steps,)
        g_spec = pl.BlockSpec((tr, n), lambda i, c_ref: (c_ref[0] * steps + i, 0))
        r_spec = pl.BlockSpec((tr, n), lambda i, c_ref: (i, 0))
        g_in = grad
    else:
        _, k4, n = grad.shape
        r8 = k4 // 2
        grid = (N_CHIPS,)
        g_spec = pl.BlockSpec((None, None, r8, n), lambda s, c_ref: (s, c_ref[0], 0, 0))
        r_spec = pl.BlockSpec((None, r8, n), lambda s, c_ref: (s, 0, 0))
        g_in = grad.reshape(N_CHIPS, 2, r8, n)

    def body(c_ref, g_ref, r_ref, o_ref):
        del c_ref
        o_ref[...] = (g_ref[...].astype(F32) + r_ref[...].astype(F32)).astype(o_ref.dtype)

    return pl.pallas_call(
        body,
        name=name,
        grid_spec=pltpu.PrefetchScalarGridSpec(num_scalar_prefetch=1, grid=grid, in_specs=[g_spec, r_spec], out_specs=r_spec),
        out_shape=jax.ShapeDtypeStruct(recv.shape, recv.dtype),
        compiler_params=_params(("parallel",)),
    )(c, g_in, recv)


def _chip_scatter(sums, kinds, name):
    flat = [(i, l, s) for i, ss in enumerate(sums) for l, s in enumerate(ss)]
    n, n_t = len(flat), len(sums)
    outs = []
    for ss, kd in zip(sums, kinds):
        s = ss[0]
        shp = (s.shape[0], s.shape[1] // N_CHIPS) if kd == "col" else s.shape[1:]
        outs += [jax.ShapeDtypeStruct((len(ss), N_CHIPS) + tuple(shp), s.dtype)] * 2
    per = 7

    def body(*refs):
        src = refs[:n]
        mine = [refs[n + 2 * i] for i in range(n_t)]
        sib = [refs[n + 2 * i + 1] for i in range(n_t)]
        send_sems, recv_sems, local_sems = refs[n + 2 * n_t:]
        x, y, c = _place()
        q = 2 * x + y
        chips = _other_chips(x, y)
        sibling = (x, y, 1 - c)

        def copy(j, m, src_ref, dst_ref, to):
            return pltpu.make_async_remote_copy(
                src_ref=src_ref, dst_ref=dst_ref, send_sem=send_sems.at[j * per + m], recv_sem=recv_sems.at[j * per + m],
                device_id=to, device_id_type=MESH)

        local, sends = [], []
        for j, (i, l, _) in enumerate(flat):
            own = _slice_of_half(kinds[i], src[j], q)
            local.append(pltpu.make_async_copy(own, mine[i].at[l, q], local_sems.at[j]))
            sends.append(copy(j, 3, own, sib[i].at[l, q], sibling))
            for k, chip in enumerate(chips):
                sends.append(copy(j, k, _slice_of_half(kinds[i], src[j], 2 * chip[0] + chip[1]), mine[i].at[l, q], (*chip, c)))
        for cp in local + sends:
            cp.start()
        for j, (i, l, _) in enumerate(flat):
            for k, chip in enumerate(chips):
                slot = 2 * chip[0] + chip[1]
                copy(j, k, mine[i].at[l, slot], mine[i].at[l, slot], (*chip, c)).wait_recv()
                sends.append(copy(j, 4 + k, mine[i].at[l, slot], sib[i].at[l, slot], sibling))
                sends[-1].start()
        for j, (i, l, _) in enumerate(flat):
            copy(j, 3, sib[i].at[l, q], sib[i].at[l, q], sibling).wait_recv()
            for k, chip in enumerate(chips):
                slot = 2 * chip[0] + chip[1]
                copy(j, 4 + k, sib[i].at[l, slot], sib[i].at[l, slot], sibling).wait_recv()
        for cp in sends:
            cp.wait_send()
        for cp in local:
            cp.wait()

    return pl.pallas_call(
        body,
        name=name,
        in_specs=[ANY] * n,
        out_specs=[ANY] * (2 * n_t),
        out_shape=outs,
        scratch_shapes=[pltpu.SemaphoreType.DMA((per * n,)), pltpu.SemaphoreType.DMA((per * n,)), pltpu.SemaphoreType.DMA((n,))],
    )(*[s for _, _, s in flat])


def _reduce_adamw(mine, sib, w, m, v, c, name):
    ly, _, r, cols = mine.shape
    tr = _pick(r, (128, 64, 32, 16, 8))
    steps = r // tr
    c1 = 1.0 - ADAM_B1 ** ADAM_STEP
    c2 = 1.0 - ADAM_B2 ** ADAM_STEP

    def body(c_ref, mine_ref, sib_ref, w_ref, m_ref, v_ref, g_ref, d_ref, nm_ref, nv_ref):
        def total(ref):
            acc = ref[0].astype(F32)
            for s in range(1, N_CHIPS):
                acc = acc + ref[s].astype(F32)
            return acc

        gv = jnp.where(pl.program_id(1) == c_ref[0], total(mine_ref), total(sib_ref))
        nm = ADAM_B1 * m_ref[...] + (1.0 - ADAM_B1) * gv
        nv = ADAM_B2 * v_ref[...] + (1.0 - ADAM_B2) * (gv * gv)
        g_ref[...] = gv
        d_ref[...] = -ADAM_LR * ((nm / c1) / (jnp.sqrt(nv / c2) + ADAM_EPS) + ADAM_WD * w_ref[...])
        nm_ref[...] = nm
        nv_ref[...] = nv

    slot_spec = pl.BlockSpec((None, N_CHIPS, tr, cols), lambda l, h, i, c_ref: (l, 0, i, 0))
    spec = pl.BlockSpec((None, tr, cols), lambda l, h, i, c_ref: (l, h * steps + i, 0))
    shp = jax.ShapeDtypeStruct(w.shape, F32)
    return pl.pallas_call(
        body,
        name=name,
        grid_spec=pltpu.PrefetchScalarGridSpec(
            num_scalar_prefetch=1, grid=(ly, N_CORES, steps),
            in_specs=[slot_spec, slot_spec, spec, spec, spec], out_specs=[spec] * 4),
        out_shape=[shp] * 4,
        compiler_params=_params(("parallel", "parallel", "parallel")),
    )(c, mine, sib, w, m, v)


def _allreduce_small(v, name):
    r, w = v.shape

    def body(v_ref, o_ref, buf_ref, send_sems, recv_sems):
        x, y, c = _place()
        me = 4 * x + 2 * y + c

        def peer(k):
            return x ^ (k >> 2), y ^ ((k >> 1) & 1), c ^ (k & 1)

        def remote(k, slot):
            return pltpu.make_async_remote_copy(
                src_ref=v_ref, dst_ref=buf_ref.at[slot], send_sem=send_sems.at[k - 1], recv_sem=recv_sems.at[k - 1],
                device_id=peer(k), device_id_type=MESH)

        sends = [remote(k, me) for k in range(1, N_DEV)]
        for cp in sends:
            cp.start()
        buf_ref[me] = v_ref[...]
        for k in range(1, N_DEV):
            px, py, pc = peer(k)
            remote(k, 4 * px + 2 * py + pc).wait_recv()
        for cp in sends:
            cp.wait_send()
        acc = buf_ref[0]
        for dev in range(1, N_DEV):
            acc = acc + buf_ref[dev]
        o_ref[...] = acc

    vm = pl.BlockSpec(memory_space=pltpu.VMEM)
    return pl.pallas_call(
        body,
        name=name,
        in_specs=[vm],
        out_specs=vm,
        out_shape=jax.ShapeDtypeStruct((r, w), F32),
        scratch_shapes=[pltpu.VMEM((N_DEV, r, w), F32), pltpu.SemaphoreType.DMA((N_DEV - 1,)), pltpu.SemaphoreType.DMA((N_DEV - 1,))],
        compiler_params=pltpu.CompilerParams(vmem_limit_bytes=VMEM_LIMIT_BYTES),
    )(v)


def _adamw(w, g, m, v, name):
    ly, r, c = w.shape
    tr = _pick(r, (256, 128, 64, 32, 16, 8))
    c1 = 1.0 - ADAM_B1 ** ADAM_STEP
    c2 = 1.0 - ADAM_B2 ** ADAM_STEP

    def body(w_ref, g_ref, m_ref, v_ref, d_ref, nm_ref, nv_ref):
        gv = g_ref[...]
        nm = ADAM_B1 * m_ref[...] + (1.0 - ADAM_B1) * gv
        nv = ADAM_B2 * v_ref[...] + (1.0 - ADAM_B2) * (gv * gv)
        d_ref[...] = -ADAM_LR * ((nm / c1) / (jnp.sqrt(nv / c2) + ADAM_EPS) + ADAM_WD * w_ref[...])
        nm_ref[...] = nm
        nv_ref[...] = nv

    spec = pl.BlockSpec((None, tr, c), lambda l, i: (l, i, 0))
    shp = jax.ShapeDtypeStruct((ly, r, c), F32)
    return pl.pallas_call(
        body,
        name=name,
        grid=(ly, r // tr),
        in_specs=[spec] * 4,
        out_specs=[spec] * 3,
        out_shape=[shp] * 3,
        compiler_params=_params(("parallel", "parallel")),
    )(w, g, m, v)


def _rope_tables(seq):
    pos = jnp.arange(seq, dtype=F32)
    inv_freq = 1.0 / (ROPE_THETA ** (jnp.arange(0, HEAD_DIM, 2, dtype=F32) / HEAD_DIM))
    ang = (pos[:, None] * inv_freq[None, :]).T
    cos, sin = jnp.cos(ang), jnp.sin(ang)
    return jnp.concatenate([cos, cos], axis=0), jnp.concatenate([-sin, sin], axis=0)


def _pack(vs, fill=0.0):
    p = jnp.concatenate([v.reshape(-1) for v in vs])
    size = -(-p.shape[0] // 8192) * 8192
    return jnp.pad(p, (0, size - p.shape[0]), constant_values=fill).reshape(-1, 1024)


def _unpack(p, like):
    p = p.reshape(-1)
    out, o = [], 0
    for v in like:
        n = int(math.prod(v.shape))
        out.append(p[o:o + n].reshape(v.shape))
        o += n
    return out


def kernel(x, norm_mix, norm_ffn, norm_final, conv_w_in, conv_w_conv, conv_w_out, attn_w_qkv, attn_b_qkv, attn_sinks, attn_w_o, attn_b_o, ffn_w_in, ffn_w_conv, ffn_w_down, loss_target, m_norm_mix, m_norm_ffn, m_norm_final, m_conv_w_in, m_conv_w_conv, m_conv_w_out, m_attn_w_qkv, m_attn_b_qkv, m_attn_sinks, m_attn_w_o, m_attn_b_o, m_ffn_w_in, m_ffn_w_conv, m_ffn_w_down, v_norm_mix, v_norm_ffn, v_norm_final, v_conv_w_in, v_conv_w_conv, v_conv_w_out, v_attn_w_qkv, v_attn_b_qkv, v_attn_sinks, v_attn_w_o, v_attn_b_o, v_ffn_w_in, v_ffn_w_conv, v_ffn_w_down):
    bsz, seq, d = x.shape
    t = bsz * seq
    depth = norm_mix.shape[0]
    n_conv, n_attn = conv_w_in.shape[0], attn_w_qkv.shape[0]
    xq, yq, cq = _place()
    q = 2 * xq + yq

    big = [conv_w_in, conv_w_out, attn_w_qkv, attn_w_o, ffn_w_in, ffn_w_down]
    axes = [2, 1, 2, 1, 2, 1]
    q_arr = q.astype(jnp.int32).reshape(1)
    c_arr = cq.astype(jnp.int32).reshape(1)
    w_cin, w_cout, w_qkv, w_o, w_fin, w_fdown = _gather_weights(
        [_place_shard(w, ax, q_arr, f"place_shard{n}") for n, (w, ax) in enumerate(zip(big, axes))], axes, "gather_weights")

    small_cols = [conv_w_conv, attn_b_qkv, attn_b_o, ffn_w_conv]

    def placed(v):
        width = v.shape[-1]
        full = jnp.zeros(v.shape[:-1] + (N_CHIPS * width,), F32)
        return lax.dynamic_update_slice_in_dim(full, v * (1.0 / N_CORES), q * width, axis=v.ndim - 1)

    full_cols = [placed(v) for v in small_cols]
    wc_conv, b_qkv, b_o, wf_conv = _unpack(_allreduce_small(_pack(full_cols), "gather_small"), full_cols)
    cos_t, sin_t = _rope_tables(seq)

    xs = x.reshape(t, d)
    saved = []
    for i in range(depth):
        j = i // 2
        h = _rms_fwd(xs, norm_mix[i:i + 1], f"norm_mix_fwd{i}")
        if i % 2 == 0:
            pre = _mm(h, w_cin, "nn", F32, layer=j, tm=1024, tn=768, tk=4096, name=f"conv_in_fwd{i}")
            mixed = _convgate_fwd(pre, wc_conv[j], seq, f"conv_gate_fwd{i}")
            x_mid = _mm(mixed, w_cout, "nn", F32, layer=j, residual=xs, tm=512, tn=1024, tk=4096, name=f"conv_out_fwd{i}")
            lse = None
        else:
            pre = _mm(h, w_qkv, "nn", F32, layer=j, bias=b_qkv[j:j + 1], tm=1024, tn=768, tk=4096, name=f"qkv_fwd{i}")
            mixed, lse = _attn_fwd(pre, attn_sinks[j], cos_t, sin_t, bsz, seq, f"attn_fwd{i}")
            x_mid = _mm(mixed, w_o, "nn", F32, layer=j, bias=b_o[j:j + 1], residual=xs, tm=512, tn=1024, tk=4096,
                        name=f"attn_out_fwd{i}")
        h2 = _rms_fwd(x_mid, norm_ffn[i:i + 1], f"norm_ffn_fwd{i}")
        gu = _mm(h2, w_fin, "nn", F32, layer=i, n_outer=True, tm=512, tn=1408, tk=4096, name=f"ffn_in_fwd{i}")
        act = _ffngate_fwd(gu, wf_conv[i], seq, f"ffn_gate_fwd{i}")
        x_next = _mm(act, w_fdown, "nn", F32, layer=i, residual=x_mid, tm=512, tn=1024, tk=4096, name=f"ffn_down_fwd{i}")
        saved.append((xs, h, pre, mixed, lse, x_mid, h2, gu, act))
        xs = x_next

    dx, dxb, sq, dg_final = _loss_head(xs, loss_target.reshape(t, d), norm_final.reshape(1, d), "loss_head")
    loss = lax.psum(0.5 * jnp.sum(sq) / d, ("x", "y", "c"))

    g_norm_mix, g_norm_ffn = [None] * depth, [None] * depth
    g_cin, g_cconv, g_cout = [None] * n_conv, [None] * n_conv, [None] * n_conv
    g_qkv, g_bqkv, g_sinks, g_o, g_bo = ([None] * n_attn for _ in range(5))
    g_fin, g_fconv, g_fdown = [None] * depth, [None] * depth, [None] * depth
    for i in reversed(range(depth)):
        j = i // 2
        x_in, h, pre, mixed, lse, x_mid, h2, gu, act = saved[i]
        da = _mm(dxb, w_fdown, "nt", F32, layer=i, n_outer=True, tm=512, tn=1408, tk=4096, name=f"ffn_down_dx{i}")
        g_fdown[i] = _mm(act, dxb, "tn", BF16, tm=1408, tn=1024, tk=512, name=f"ffn_down_dw{i}")
        dgu, dwc = _ffngate_bwd(gu, da, wf_conv[i], seq, f"ffn_gate_bwd{i}")
        g_fconv[i] = dwc[:3]
        g_fin[i] = _mm(h2, dgu, "tn", BF16, tm=1024, tn=1408, tk=512, name=f"ffn_in_dw{i}")
        dh2 = _mm(dgu, w_fin, "nt", F32, layer=i, tm=1024, tn=1024, tk=1408, name=f"ffn_in_dx{i}")
        dx, dxb, dg, colsum = _rms_bwd(x_mid, dh2, norm_ffn[i:i + 1], dx, f"norm_ffn_bwd{i}")
        g_norm_ffn[i] = jnp.sum(dg, axis=0)
        if i % 2 == 0:
            dmix = _mm(dxb, w_cout, "nt", F32, layer=j, tm=512, tn=1024, tk=4096, name=f"conv_out_dx{i}")
            g_cout[j] = _mm(mixed, dxb, "tn", BF16, tm=1024, tn=1024, tk=512, name=f"conv_out_dw{i}")
            dpre, dwc = _convgate_bwd(pre, dmix, wc_conv[j], seq, f"conv_gate_bwd{i}")
            g_cconv[j] = dwc[:3]
            g_cin[j] = _mm(h, dpre, "tn", BF16, tm=1024, tn=768, tk=512, name=f"conv_in_dw{i}")
            dh = _mm(dpre, w_cin, "nt", F32, layer=j, tm=1024, tn=1024, tk=1536, name=f"conv_in_dx{i}")
        else:
            g_bo[j] = jnp.sum(colsum, axis=0)
            dmix = _mm(dxb, w_o, "nt", F32, layer=j, tm=512, tn=1024, tk=4096, name=f"attn_out_dx{i}")
            g_o[j] = _mm(mixed, dxb, "tn", BF16, tm=1024, tn=1024, tk=512, name=f"attn_out_dw{i}")
            dpre, dbias, dsk = _attn_bwd(pre, mixed, lse, dmix, attn_sinks[j], cos_t, sin_t, bsz, seq, f"attn_bwd{i}")
            g_bqkv[j] = jnp.sum(dbias, axis=0)
            g_sinks[j] = jnp.sum(dsk, axis=1)
            g_qkv[j] = _mm(h, dpre, "tn", BF16, tm=1024, tn=768, tk=512, name=f"qkv_dw{i}")
            dh = _mm(dpre, w_qkv, "nt", F32, layer=j, tm=512, tn=1024, tk=1536, name=f"qkv_dx{i}")
        dx, dxb, dg, _ = _rms_bwd(x_in, dh, norm_mix[i:i + 1], dx, f"norm_mix_bwd{i}")
        g_norm_mix[i] = jnp.sum(dg, axis=0)
    grad_x = dx.reshape(bsz, seq, d)

    tensors = [(g_cin, "col"), (g_cout, "row"), (g_qkv, "col"), (g_o, "row"), (g_fin, "col"), (g_fdown, "row")]
    t_kinds = [kd for _, kd in tensors]
    flat, kinds = [], []
    for gs, kd in tensors:
        for g in gs:
            flat.append(g if kd == "col" else g.reshape(N_CHIPS, g.shape[0] // N_CHIPS, g.shape[1]))
            kinds.append(kd)
    recv = _pair_exchange(flat, kinds, "grad_pair_exchange")
    sums = [_pair_sum(g, r, kd, c_arr, f"grad_pair_sum{n}") for n, (g, r, kd) in enumerate(zip(flat, recv, kinds))]
    by_tensor, pos = [], 0
    for gs, _ in tensors:
        by_tensor.append(sums[pos:pos + len(gs)])
        pos += len(gs)
    slots = _chip_scatter(by_tensor, t_kinds, "grad_chip_scatter")
    big_w = [conv_w_in, conv_w_out, attn_w_qkv, attn_w_o, ffn_w_in, ffn_w_down]
    big_m = [m_conv_w_in, m_conv_w_out, m_attn_w_qkv, m_attn_w_o, m_ffn_w_in, m_ffn_w_down]
    big_v = [v_conv_w_in, v_conv_w_out, v_attn_w_qkv, v_attn_w_o, v_ffn_w_in, v_ffn_w_down]
    big_names = ["conv_w_in", "conv_w_out", "attn_w_qkv", "attn_w_o", "ffn_w_in", "ffn_w_down"]
    big_upd = [_reduce_adamw(slots[2 * n], slots[2 * n + 1], big_w[n], big_m[n], big_v[n], c_arr, f"adamw_{nm}")
               for n, nm in enumerate(big_names)]

    small = [jnp.stack(g_norm_mix), jnp.stack(g_norm_ffn), jnp.sum(dg_final, axis=0), jnp.stack(g_cconv),
             jnp.stack(g_bqkv), jnp.stack(g_sinks), jnp.stack(g_bo), jnp.stack(g_fconv)]
    sg = _unpack(_allreduce_small(_pack(small), "grad_small_allreduce"), small)

    def my_cols(v, like):
        width = like.shape[-1]
        return lax.dynamic_slice_in_dim(v, q * width, width, axis=v.ndim - 1)

    small_w = [norm_mix, norm_ffn, norm_final, conv_w_conv, attn_b_qkv, attn_sinks, attn_b_o, ffn_w_conv]
    small_m = [m_norm_mix, m_norm_ffn, m_norm_final, m_conv_w_conv, m_attn_b_qkv, m_attn_sinks, m_attn_b_o, m_ffn_w_conv]
    small_v = [v_norm_mix, v_norm_ffn, v_norm_final, v_conv_w_conv, v_attn_b_qkv, v_attn_sinks, v_attn_b_o, v_ffn_w_conv]
    small_g = [sg[0], sg[1], sg[2], my_cols(sg[3], conv_w_conv), my_cols(sg[4], attn_b_qkv), sg[5],
               my_cols(sg[6], attn_b_o), my_cols(sg[7], ffn_w_conv)]

    upd = {nm: tuple(u[1:]) for nm, u in zip(big_names, big_upd)}
    sd, sm, sv = _adamw(_pack(small_w)[None], _pack(small_g)[None], _pack(small_m)[None], _pack(small_v, 1.0)[None],
                        "adamw_small")
    sd, sm, sv = _unpack(sd, small_w), _unpack(sm, small_w), _unpack(sv, small_w)
    names = ["norm_mix", "norm_ffn", "norm_final", "conv_w_in", "conv_w_conv", "conv_w_out", "attn_w_qkv", "attn_b_qkv",
             "attn_sinks", "attn_w_o", "attn_b_o", "ffn_w_in", "ffn_w_conv", "ffn_w_down"]
    small_names = ["norm_mix", "norm_ffn", "norm_final", "conv_w_conv", "attn_b_qkv", "attn_sinks", "attn_b_o", "ffn_w_conv"]
    grads = dict(zip(small_names, small_g))
    grads.update({nm: u[0] for nm, u in zip(big_names, big_upd)})
    for n, nm in enumerate(small_names):
        upd[nm] = (sd[n], sm[n], sv[n])
    return (loss, grad_x, *[grads[nm] for nm in names], *[upd[nm][0] for nm in names],
            *[upd[nm][1] for nm in names], *[upd[nm][2] for nm in names])
```

```python
import math

import jax
import jax.numpy as jnp
from jax import lax
from jax.experimental import pallas as pl
from jax.experimental.pallas import tpu as pltpu

F32 = jnp.float32
BF16 = jnp.bfloat16

HEAD_DIM = 64
GROUP = 4
WINDOW = 128
EPS = 1e-5
ROPE_THETA = 10000.0
ADAM_LR, ADAM_B1, ADAM_B2, ADAM_EPS, ADAM_WD, ADAM_STEP = 0.001, 0.9, 0.999, 1e-08, 0.01, 10

N_CHIPS = 4
N_CORES = 2
N_DEV = 8
HALO = 16
VMEM_LIMIT_BYTES = 56 * 1024 * 1024
MESH = pl.DeviceIdType.MESH
ANY = pl.BlockSpec(memory_space=pl.ANY)
SMEM = pl.BlockSpec(memory_space=pltpu.SMEM)
NEG = float(jnp.finfo(jnp.float32).min)
ROW_TILES = (512, 256, 128, 64, 32, 16, 8)


def _pick(dim, cands):
    for c in cands:
        if dim % c == 0:
            return c
    return dim


def _params(sem):
    return pltpu.CompilerParams(dimension_semantics=sem, vmem_limit_bytes=VMEM_LIMIT_BYTES)


_DIMS = {"nn": (((1,), (0,)), ((), ())), "nt": (((1,), (1,)), ((), ())), "tn": (((0,), (0,)), ((), ()))}


def _mm(a, b, mode, out_dtype, *, layer=None, bias=None, residual=None, n_outer=False, tm, tn, tk, name):
    b2 = b.shape[1:] if layer is not None else b.shape
    if mode == "nn":
        (m, k), n = a.shape, b2[1]
    elif mode == "nt":
        (m, k), n = a.shape, b2[0]
    else:
        (k, m), n = a.shape, b2[1]
    tm, tn, tk = min(tm, m), min(tn, n), min(tk, k)
    assert m % tm == 0 and n % tn == 0 and k % tk == 0, (name, a.shape, b.shape, tm, tn, tk)
    nk = k // tk

    def at(f):
        return (lambda p0, p1, p2: f(p1, p0, p2)) if n_outer else f

    a_spec = pl.BlockSpec((tk, tm), at(lambda i, j, l: (l, i))) if mode == "tn" else pl.BlockSpec((tm, tk), at(lambda i, j, l: (i, l)))
    if layer is None:
        b_spec = (pl.BlockSpec((tn, tk), at(lambda i, j, l: (j, l))) if mode == "nt"
                  else pl.BlockSpec((tk, tn), at(lambda i, j, l: (l, j))))
    elif mode == "nt":
        b_spec = pl.BlockSpec((None, tn, tk), at(lambda i, j, l: (layer, j, l)))
    else:
        b_spec = pl.BlockSpec((None, tk, tn), at(lambda i, j, l: (layer, l, j)))
    in_specs, args = [a_spec, b_spec], [a, b]
    if bias is not None:
        in_specs.append(pl.BlockSpec((1, tn), at(lambda i, j, l: (0, j))))
        args.append(bias)
    if residual is not None:
        in_specs.append(pl.BlockSpec((tm, tn), at(lambda i, j, l: (i, j))))
        args.append(residual)
    has_bias, has_res = bias is not None, residual is not None

    def body(*refs):
        a_ref, b_ref = refs[0], refs[1]
        pos = 2
        bias_ref = res_ref = None
        if has_bias:
            bias_ref, pos = refs[pos], pos + 1
        if has_res:
            res_ref, pos = refs[pos], pos + 1
        o_ref = refs[pos]
        acc_ref = refs[pos + 1] if nk > 1 else None

        def finish(acc):
            if has_bias:
                acc = acc + bias_ref[...]
            if has_res:
                acc = acc + res_ref[...]
            o_ref[...] = acc.astype(o_ref.dtype)

        if nk == 1:
            finish(lax.dot_general(a_ref[...], b_ref[...], _DIMS[mode], preferred_element_type=F32))
            return
        l = pl.program_id(2)
        part = lax.dot_general(a_ref[...], b_ref[...], _DIMS[mode], preferred_element_type=F32)

        @pl.when(l == 0)
        def _():
            acc_ref[...] = part

        @pl.when(l > 0)
        def _():
            acc_ref[...] += part

        @pl.when(l == nk - 1)
        def _():
            finish(acc_ref[...])

    return pl.pallas_call(
        body,
        name=name,
        grid=(n // tn, m // tm, nk) if n_outer else (m // tm, n // tn, nk),
        in_specs=in_specs,
        out_specs=pl.BlockSpec((tm, tn), at(lambda i, j, l: (i, j))),
        out_shape=jax.ShapeDtypeStruct((m, n), out_dtype),
        scratch_shapes=[pltpu.VMEM((tm, tn), F32)] if nk > 1 else [],
        compiler_params=_params(("parallel", "parallel", "arbitrary")),
    )(*args)


def _fold8(v):
    r, d = v.shape
    return jnp.sum(v.reshape(r // 8, 8, d), axis=0)


def _rms_fwd(x, g, name):
    t, d = x.shape
    tm = _pick(t, ROW_TILES)

    def body(x_ref, g_ref, h_ref):
        xv = x_ref[...]
        r = lax.rsqrt(jnp.mean(xv * xv, axis=-1, keepdims=True) + EPS)
        h_ref[...] = (xv * r * g_ref[...]).astype(BF16)

    return pl.pallas_call(
        body,
        name=name,
        grid=(t // tm,),
        in_specs=[pl.BlockSpec((tm, d), lambda i: (i, 0)), pl.BlockSpec((1, d), lambda i: (0, 0))],
        out_specs=pl.BlockSpec((tm, d), lambda i: (i, 0)),
        out_shape=jax.ShapeDtypeStruct((t, d), BF16),
        compiler_params=_params(("parallel",)),
    )(x, g)


def _rms_bwd(x, dh, g, dx_in, name):
    t, d = x.shape
    tm = _pick(t, ROW_TILES)

    def body(x_ref, dh_ref, g_ref, dxi_ref, dx_ref, dxb_ref, dg_ref, cs_ref):
        i = pl.program_id(0)
        xv = x_ref[...]
        r = lax.rsqrt(jnp.mean(xv * xv, axis=-1, keepdims=True) + EPS)
        xhat = xv * r
        dy = dh_ref[...]
        gdy = dy * g_ref[...]
        dx = dxi_ref[...] + r * (gdy - xhat * jnp.mean(gdy * xhat, axis=-1, keepdims=True))
        dx_ref[...] = dx
        dxb_ref[...] = dx.astype(BF16)

        @pl.when(i == 0)
        def _():
            dg_ref[...] = jnp.zeros_like(dg_ref)
            cs_ref[...] = jnp.zeros_like(cs_ref)

        dg_ref[...] += _fold8(dy * xhat)
        cs_ref[...] += _fold8(dx)

    row = pl.BlockSpec((tm, d), lambda i: (i, 0))
    acc = pl.BlockSpec((8, d), lambda i: (0, 0))
    return pl.pallas_call(
        body,
        name=name,
        grid=(t // tm,),
        in_specs=[row, row, pl.BlockSpec((1, d), lambda i: (0, 0)), row],
        out_specs=[row, row, acc, acc],
        out_shape=[jax.ShapeDtypeStruct((t, d), F32), jax.ShapeDtypeStruct((t, d), BF16),
                   jax.ShapeDtypeStruct((8, d), F32), jax.ShapeDtypeStruct((8, d), F32)],
        compiler_params=_params(("arbitrary",)),
    )(x, dh, g, dx_in)


def _loss_head(x, target, g, name):
    t, d = x.shape
    tm = _pick(t, ROW_TILES)
    inv_d = 1.0 / d

    def body(x_ref, t_ref, g_ref, dx_ref, dxb_ref, sq_ref, dg_ref):
        i = pl.program_id(0)
        xv = x_ref[...]
        gv = g_ref[...]
        r = lax.rsqrt(jnp.mean(xv * xv, axis=-1, keepdims=True) + EPS)
        xhat = xv * r
        err = xhat * gv - t_ref[...]
        dy = err * inv_d
        gdy = dy * gv
        dx = r * (gdy - xhat * jnp.mean(gdy * xhat, axis=-1, keepdims=True))
        dx_ref[...] = dx
        dxb_ref[...] = dx.astype(BF16)

        @pl.when(i == 0)
        def _():
            sq_ref[...] = jnp.zeros_like(sq_ref)
            dg_ref[...] = jnp.zeros_like(dg_ref)

        sq_ref[...] += _fold8(err * err)
        dg_ref[...] += _fold8(dy * xhat)

    row = pl.BlockSpec((tm, d), lambda i: (i, 0))
    acc = pl.BlockSpec((8, d), lambda i: (0, 0))
    return pl.pallas_call(
        body,
        name=name,
        grid=(t // tm,),
        in_specs=[row, row, pl.BlockSpec((1, d), lambda i: (0, 0))],
        out_specs=[row, row, acc, acc],
        out_shape=[jax.ShapeDtypeStruct((t, d), F32), jax.ShapeDtypeStruct((t, d), BF16),
                   jax.ShapeDtypeStruct((8, d), F32), jax.ShapeDtypeStruct((8, d), F32)],
        compiler_params=_params(("arbitrary",)),
    )(x, target, g)


def _rows(tm):
    return lax.broadcasted_iota(jnp.int32, (tm, 1), 0)


def _shift_down(u, before2):
    r8 = _rows(8)
    s1, s2 = pltpu.roll(u, 1, 0), pltpu.roll(u, 2, 0)
    top1 = jnp.where(r8 == 0, before2[1:2], s1[:8])
    top2 = jnp.where(r8 == 0, before2[0:1], jnp.where(r8 == 1, before2[1:2], s2[:8]))
    return jnp.concatenate([top1, s1[8:]], axis=0), jnp.concatenate([top2, s2[8:]], axis=0)


def _shift_up(u, after2):
    tm = u.shape[0]
    r8 = _rows(8)
    s1, s2 = pltpu.roll(u, tm - 1, 0), pltpu.roll(u, tm - 2, 0)
    bot1 = jnp.where(r8 == 7, after2[0:1], s1[tm - 8:])
    bot2 = jnp.where(r8 == 6, after2[0:1], jnp.where(r8 == 7, after2[1:2], s2[tm - 8:]))
    return jnp.concatenate([s1[:tm - 8], bot1], axis=0), jnp.concatenate([s2[:tm - 8], bot2], axis=0)


def _conv_tile(seq):
    return _pick(seq, (256, 128, 64, 32, 16, 8))


def _halo_specs(tm, width, n_tiles):
    per = tm // HALO
    before = pl.BlockSpec((HALO, width), lambda i: (jnp.maximum(i * per - 1, 0), 0))
    after = pl.BlockSpec((HALO, width), lambda i: (jnp.minimum((i + 1) * per, n_tiles * per - 1), 0))
    return before, after


def _convgate_fwd(bcv, w, seq, name):
    t, d3 = bcv.shape
    d = d3 // 3
    tm = _conv_tile(seq)
    tps = seq // tm
    before, _ = _halo_specs(tm, d3, t // tm)

    def body(x_ref, xb_ref, w_ref, y_ref):
        i = pl.program_id(0)
        inner = (i % tps != 0).astype(F32)
        u = x_ref[:, d:2 * d].astype(F32) * x_ref[:, 2 * d:].astype(F32)
        xb = xb_ref[:, d:].astype(F32)[HALO - 2:]
        s1, s2 = _shift_down(u, xb[:, :d] * xb[:, d:] * inner)
        z = w_ref[2:3] * u + w_ref[1:2] * s1 + w_ref[0:1] * s2
        y_ref[...] = (x_ref[:, :d].astype(F32) * z).astype(BF16)

    return pl.pallas_call(
        body,
        name=name,
        grid=(t // tm,),
        in_specs=[pl.BlockSpec((tm, d3), lambda i: (i, 0)), before, pl.BlockSpec((3, d), lambda i: (0, 0))],
        out_specs=pl.BlockSpec((tm, d), lambda i: (i, 0)),
        out_shape=jax.ShapeDtypeStruct((t, d), BF16),
        compiler_params=_params(("parallel",)),
    )(bcv, bcv, w)


def _convgate_bwd(bcv, dy, w, seq, name):
    t, d3 = bcv.shape
    d = d3 // 3
    tm = _conv_tile(seq)
    tps = seq // tm
    before, after = _halo_specs(tm, d3, t // tm)
    _, after_dy = _halo_specs(tm, d, t // tm)

    def body(x_ref, xb_ref, xa_ref, dy_ref, dya_ref, w_ref, dx_ref, dw_ref):
        i = pl.program_id(0)
        inner_lo = (i % tps != 0).astype(F32)
        inner_hi = (i % tps != tps - 1).astype(F32)
        w0, w1, w2 = w_ref[0:1], w_ref[1:2], w_ref[2:3]
        b, c, v = x_ref[:, :d].astype(F32), x_ref[:, d:2 * d].astype(F32), x_ref[:, 2 * d:].astype(F32)
        u = c * v
        xb = xb_ref[:, d:].astype(F32)[HALO - 2:]
        s1, s2 = _shift_down(u, xb[:, :d] * xb[:, d:] * inner_lo)
        z = w2 * u + w1 * s1 + w0 * s2
        dyv = dy_ref[...].astype(F32)
        dz = dyv * b
        dza = dya_ref[...].astype(F32)[0:2] * xa_ref[:, :d].astype(F32)[0:2] * inner_hi
        n1, n2 = _shift_up(dz, dza)
        du = w2 * dz + w1 * n1 + w0 * n2
        dx_ref[:, :d] = (dyv * z).astype(BF16)
        dx_ref[:, d:2 * d] = (du * v).astype(BF16)
        dx_ref[:, 2 * d:] = (du * c).astype(BF16)

        @pl.when(i == 0)
        def _():
            dw_ref[...] = jnp.zeros_like(dw_ref)

        dw_ref[0:1] += jnp.sum(dz * s2, axis=0, keepdims=True)
        dw_ref[1:2] += jnp.sum(dz * s1, axis=0, keepdims=True)
        dw_ref[2:3] += jnp.sum(dz * u, axis=0, keepdims=True)

    return pl.pallas_call(
        body,
        name=name,
        grid=(t // tm,),
        in_specs=[pl.BlockSpec((tm, d3), lambda i: (i, 0)), before, after,
                  pl.BlockSpec((tm, d), lambda i: (i, 0)), after_dy, pl.BlockSpec((3, d), lambda i: (0, 0))],
        out_specs=[pl.BlockSpec((tm, d3), lambda i: (i, 0)), pl.BlockSpec((8, d), lambda i: (0, 0))],
        out_shape=[jax.ShapeDtypeStruct((t, d3), BF16), jax.ShapeDtypeStruct((8, d), F32)],
        compiler_params=_params(("arbitrary",)),
    )(bcv, bcv, bcv, dy, dy, w)


def _sigmoid(x):
    return 1.0 / (1.0 + jnp.exp(-x))


def _ffngate_fwd(gu, w, seq, name):
    t, f2 = gu.shape
    f = f2 // 2
    tm = _conv_tile(seq)
    tps = seq // tm
    before, _ = _halo_specs(tm, f2, t // tm)

    def body(x_ref, xb_ref, w_ref, a_ref):
        i = pl.program_id(0)
        inner = (i % tps != 0).astype(F32)
        g = x_ref[:, :f].astype(F32)
        s1, s2 = _shift_down(g, xb_ref[:, :f].astype(F32)[HALO - 2:] * inner)
        gc = w_ref[2:3] * g + w_ref[1:2] * s1 + w_ref[0:1] * s2
        a_ref[...] = (gc * _sigmoid(gc) * x_ref[:, f:].astype(F32)).astype(BF16)

    return pl.pallas_call(
        body,
        name=name,
        grid=(t // tm,),
        in_specs=[pl.BlockSpec((tm, f2), lambda i: (i, 0)), before, pl.BlockSpec((3, f), lambda i: (0, 0))],
        out_specs=pl.BlockSpec((tm, f), lambda i: (i, 0)),
        out_shape=jax.ShapeDtypeStruct((t, f), BF16),
        compiler_params=_params(("parallel",)),
    )(gu, gu, w)


def _ffngate_bwd(gu, da, w, seq, name):
    t, f2 = gu.shape
    f = f2 // 2
    tm = _conv_tile(seq)
    tps = seq // tm
    before, after = _halo_specs(tm, f2, t // tm)
    _, after_da = _halo_specs(tm, f, t // tm)

    def body(x_ref, xb_ref, xa_ref, da_ref, daa_ref, w_ref, dx_ref, dw_ref):
        i = pl.program_id(0)
        inner_lo = (i % tps != 0).astype(F32)
        inner_hi = (i % tps != tps - 1).astype(F32)
        w0, w1, w2 = w_ref[0:1], w_ref[1:2], w_ref[2:3]

        def dgate(gc, uv, dav):
            sg = _sigmoid(gc)
            return dav * uv * (sg * (1.0 + gc * (1.0 - sg))), dav * (gc * sg)

        g, u = x_ref[:, :f].astype(F32), x_ref[:, f:].astype(F32)
        s1, s2 = _shift_down(g, xb_ref[:, :f].astype(F32)[HALO - 2:] * inner_lo)
        gc = w2 * g + w1 * s1 + w0 * s2
        dgc, du = dgate(gc, u, da_ref[...].astype(F32))
        ga = xa_ref[:, :f].astype(F32)
        a1, a2 = _shift_down(ga, x_ref[tm - HALO:, :f].astype(F32)[HALO - 2:])
        gca = w2 * ga + w1 * a1 + w0 * a2
        dgca, _ = dgate(gca, xa_ref[:, f:].astype(F32), daa_ref[...].astype(F32))
        n1, n2 = _shift_up(dgc, dgca[0:2] * inner_hi)
        dx_ref[:, :f] = (w2 * dgc + w1 * n1 + w0 * n2).astype(BF16)
        dx_ref[:, f:] = du.astype(BF16)

        @pl.when(i == 0)
        def _():
            dw_ref[...] = jnp.zeros_like(dw_ref)

        dw_ref[0:1] += jnp.sum(dgc * s2, axis=0, keepdims=True)
        dw_ref[1:2] += jnp.sum(dgc * s1, axis=0, keepdims=True)
        dw_ref[2:3] += jnp.sum(dgc * g, axis=0, keepdims=True)

    return pl.pallas_call(
        body,
        name=name,
        grid=(t // tm,),
        in_specs=[pl.BlockSpec((tm, f2), lambda i: (i, 0)), before, after,
                  pl.BlockSpec((tm, f), lambda i: (i, 0)), after_da, pl.BlockSpec((3, f), lambda i: (0, 0))],
        out_specs=[pl.BlockSpec((tm, f2), lambda i: (i, 0)), pl.BlockSpec((8, f), lambda i: (0, 0))],
        out_shape=[jax.ShapeDtypeStruct((t, f2), BF16), jax.ShapeDtypeStruct((8, f), F32)],
        compiler_params=_params(("arbitrary",)),
    )(gu, gu, gu, da, da, w)


def _swap_halves(xt):
    half = HEAD_DIM // 2
    return jnp.concatenate([xt[half:], xt[:half]], axis=0)


def _rope(xt, cos, sin):
    return xt * cos + _swap_halves(xt) * sin


def _unrope(dxt, cos, sin):
    return dxt * cos - _swap_halves(dxt) * sin


def _band_masks(n):
    kj = lax.broadcasted_iota(jnp.int32, (WINDOW, WINDOW), 0)
    qi = lax.broadcasted_iota(jnp.int32, (WINDOW, WINDOW), 1)
    return kj <= qi, jnp.logical_and(kj > qi, n > 0)


def _tn(a, b):
    return lax.dot_general(a, b, _DIMS["tn"], preferred_element_type=F32)


def _nt(a, b):
    return lax.dot_general(a, b, _DIMS["nt"], preferred_element_type=F32)


def _nn(a, b):
    return jnp.dot(a, b, preferred_element_type=F32)


def _attn_fwd(qkv, sinks, cos_t, sin_t, bsz, seq, name):
    t, qw = qkv.shape
    d = qw * 2 // 3
    kvw = d // GROUP
    n_heads, n_kv = d // HEAD_DIM, kvw // HEAD_DIM
    nb = seq // WINDOW
    scale = HEAD_DIM ** -0.5

    def body(sink_ref, xc_ref, xp_ref, cc_ref, sc_ref, cp_ref, sp_ref, o_ref, lse_ref, xt_ref, pt_ref, ot_ref):
        n = pl.program_id(1)
        xt_ref[...] = xc_ref[...].T
        pt_ref[...] = xp_ref[:, d:].T
        cos_c, sin_c, cos_p, sin_p = cc_ref[...], sc_ref[...], cp_ref[...], sp_ref[...]
        valid_c, valid_p = _band_masks(n)
        for j in range(n_kv):
            ko = j * HEAD_DIM
            kc = _rope(xt_ref[d + ko:d + ko + HEAD_DIM, :], cos_c, sin_c).astype(BF16)
            kp = _rope(pt_ref[ko:ko + HEAD_DIM, :], cos_p, sin_p).astype(BF16)
            vc = xt_ref[d + kvw + ko:d + kvw + ko + HEAD_DIM, :].astype(BF16)
            vp = pt_ref[kvw + ko:kvw + ko + HEAD_DIM, :].astype(BF16)
            for g in range(GROUP):
                h = j * GROUP + g
                qo = h * HEAD_DIM
                q = _rope(xt_ref[qo:qo + HEAD_DIM, :], cos_c, sin_c).astype(BF16)
                s_c = jnp.where(valid_c, _tn(kc, q) * scale, NEG)
                s_p = jnp.where(valid_p, _tn(kp, q) * scale, NEG)
                sink = sink_ref[h]
                m = jnp.maximum(jnp.maximum(jnp.max(s_c, axis=0, keepdims=True),
                                            jnp.max(s_p, axis=0, keepdims=True)), sink)
                p_c = jnp.exp(s_c - m)
                p_p = jnp.exp(s_p - m)
                den = jnp.sum(p_c, axis=0, keepdims=True) + jnp.sum(p_p, axis=0, keepdims=True) + jnp.exp(sink - m)
                inv = 1.0 / den
                ot_ref[qo:qo + HEAD_DIM, :] = (_nn(vc, (p_c * inv).astype(BF16)) + _nn(vp, (p_p * inv).astype(BF16)))
                lse_ref[h:h + 1, :] = m + jnp.log(den)
        o_ref[...] = ot_ref[...].T.astype(BF16)

    cur = lambda b, n: (b * nb + n, 0)
    prev = lambda b, n: (b * nb + jnp.maximum(n - 1, 0), 0)
    tab_c = pl.BlockSpec((HEAD_DIM, WINDOW), lambda b, n: (0, n))
    tab_p = pl.BlockSpec((HEAD_DIM, WINDOW), lambda b, n: (0, jnp.maximum(n - 1, 0)))
    return pl.pallas_call(
        body,
        name=name,
        grid=(bsz, nb),
        in_specs=[SMEM, pl.BlockSpec((WINDOW, qw), cur), pl.BlockSpec((WINDOW, qw), prev), tab_c, tab_c, tab_p, tab_p],
        out_specs=[pl.BlockSpec((WINDOW, d), cur), pl.BlockSpec((n_heads, WINDOW), lambda b, n: (0, b * nb + n))],
        out_shape=[jax.ShapeDtypeStruct((t, d), BF16), jax.ShapeDtypeStruct((n_heads, t), F32)],
        scratch_shapes=[pltpu.VMEM((qw, WINDOW), F32), pltpu.VMEM((2 * kvw, WINDOW), F32), pltpu.VMEM((d, WINDOW), F32)],
        compiler_params=_params(("parallel", "arbitrary")),
    )(sinks, qkv, qkv, cos_t, sin_t, cos_t, sin_t)


def _attn_bwd(qkv, o, lse, do, sinks, cos_t, sin_t, bsz, seq, name):
    t, qw = qkv.shape
    d = qw * 2 // 3
    kvw = d // GROUP
    n_heads, n_kv = d // HEAD_DIM, kvw // HEAD_DIM
    nb = seq // WINDOW
    scale = HEAD_DIM ** -0.5

    def body(sink_ref, xc_ref, xp_ref, xn_ref, oc_ref, on_ref, doc_ref, don_ref, lc_ref, ln_ref,
             cc_ref, sc_ref, cp_ref, sp_ref, cn_ref, sn_ref,
             dx_ref, db_ref, dsk_ref, xt_ref, pt_ref, qn_ref, otc_ref, otn_ref, dtc_ref, dtn_ref, gt_ref):
        b, n = pl.program_id(0), pl.program_id(1)
        xt_ref[...] = xc_ref[...].T
        pt_ref[...] = xp_ref[:, d:].T
        qn_ref[...] = xn_ref[:, :d].T
        otc_ref[...] = oc_ref[...].astype(F32).T
        otn_ref[...] = on_ref[...].astype(F32).T
        dtc_ref[...] = doc_ref[...].T
        dtn_ref[...] = don_ref[...].T
        cos_c, sin_c, cos_p, sin_p, cos_n, sin_n = (cc_ref[...], sc_ref[...], cp_ref[...], sp_ref[...],
                                                    cn_ref[...], sn_ref[...])
        valid_c, valid_p = _band_masks(n)
        kj = lax.broadcasted_iota(jnp.int32, (WINDOW, WINDOW), 0)
        qi = lax.broadcasted_iota(jnp.int32, (WINDOW, WINDOW), 1)
        valid_n = jnp.logical_and(kj > qi, n < nb - 1)

        @pl.when(jnp.logical_and(b == 0, n == 0))
        def _():
            db_ref[...] = jnp.zeros_like(db_ref)
            dsk_ref[...] = jnp.zeros_like(dsk_ref)

        for j in range(n_kv):
            ko = j * HEAD_DIM
            kc = _rope(xt_ref[d + ko:d + ko + HEAD_DIM, :], cos_c, sin_c).astype(BF16)
            kp = _rope(pt_ref[ko:ko + HEAD_DIM, :], cos_p, sin_p).astype(BF16)
            vc = xt_ref[d + kvw + ko:d + kvw + ko + HEAD_DIM, :].astype(BF16)
            vp = pt_ref[kvw + ko:kvw + ko + HEAD_DIM, :].astype(BF16)
            dk = jnp.zeros((HEAD_DIM, WINDOW), F32)
            dv = jnp.zeros((HEAD_DIM, WINDOW), F32)
            for g in range(GROUP):
                h = j * GROUP + g
                qo = h * HEAD_DIM
                sink = sink_ref[h]
                q = _rope(xt_ref[qo:qo + HEAD_DIM, :], cos_c, sin_c).astype(BF16)
                do_h = dtc_ref[qo:qo + HEAD_DIM, :]
                do_b = do_h.astype(BF16)
                lse_h = lc_ref[h:h + 1, :]
                delta = jnp.sum(otc_ref[qo:qo + HEAD_DIM, :] * do_h, axis=0, keepdims=True)
                p_c = jnp.exp(jnp.where(valid_c, _tn(kc, q) * scale, NEG) - lse_h)
                p_p = jnp.exp(jnp.where(valid_p, _tn(kp, q) * scale, NEG) - lse_h)
                ds_c = (p_c * (_tn(vc, do_b) - delta)).astype(BF16)
                ds_p = (p_p * (_tn(vp, do_b) - delta)).astype(BF16)
                dq = (_nn(kc, ds_c) + _nn(kp, ds_p)) * scale
                gt_ref[qo:qo + HEAD_DIM, :] = _unrope(dq, cos_c, sin_c)
                dsk_ref[h:h + 1, :] += -jnp.exp(sink - lse_h) * delta
                dv += _nt(do_b, p_c.astype(BF16))
                dk += _nt(q, ds_c)
                q2 = _rope(qn_ref[qo:qo + HEAD_DIM, :], cos_n, sin_n).astype(BF16)
                do2 = dtn_ref[qo:qo + HEAD_DIM, :]
                do2_b = do2.astype(BF16)
                delta2 = jnp.sum(otn_ref[qo:qo + HEAD_DIM, :] * do2, axis=0, keepdims=True)
                p_n = jnp.exp(jnp.where(valid_n, _tn(kc, q2) * scale, NEG) - ln_ref[h:h + 1, :])
                ds_n = (p_n * (_tn(vc, do2_b) - delta2)).astype(BF16)
                dv += _nt(do2_b, p_n.astype(BF16))
                dk += _nt(q2, ds_n)
            gt_ref[d + ko:d + ko + HEAD_DIM, :] = _unrope(dk * scale, cos_c, sin_c)
            gt_ref[d + kvw + ko:d + kvw + ko + HEAD_DIM, :] = dv
        dx = gt_ref[...].T
        dx_ref[...] = dx.astype(BF16)
        db_ref[...] += _fold8(dx)

    cur = lambda b, n: (b * nb + n, 0)
    prev = lambda b, n: (b * nb + jnp.maximum(n - 1, 0), 0)
    nxt = lambda b, n: (b * nb + jnp.minimum(n + 1, nb - 1), 0)
    stat_c = pl.BlockSpec((n_heads, WINDOW), lambda b, n: (0, b * nb + n))
    stat_n = pl.BlockSpec((n_heads, WINDOW), lambda b, n: (0, b * nb + jnp.minimum(n + 1, nb - 1)))
    tab_c = pl.BlockSpec((HEAD_DIM, WINDOW), lambda b, n: (0, n))
    tab_p = pl.BlockSpec((HEAD_DIM, WINDOW), lambda b, n: (0, jnp.maximum(n - 1, 0)))
    tab_n = pl.BlockSpec((HEAD_DIM, WINDOW), lambda b, n: (0, jnp.minimum(n + 1, nb - 1)))
    return pl.pallas_call(
        body,
        name=name,
        grid=(bsz, nb),
        in_specs=[SMEM, pl.BlockSpec((WINDOW, qw), cur), pl.BlockSpec((WINDOW, qw), prev), pl.BlockSpec((WINDOW, qw), nxt),
                  pl.BlockSpec((WINDOW, d), cur), pl.BlockSpec((WINDOW, d), nxt),
                  pl.BlockSpec((WINDOW, d), cur), pl.BlockSpec((WINDOW, d), nxt),
                  stat_c, stat_n, tab_c, tab_c, tab_p, tab_p, tab_n, tab_n],
        out_specs=[pl.BlockSpec((WINDOW, qw), cur), pl.BlockSpec((8, qw), lambda b, n: (0, 0)),
                   pl.BlockSpec((n_heads, WINDOW), lambda b, n: (0, 0))],
        out_shape=[jax.ShapeDtypeStruct((t, qw), BF16), jax.ShapeDtypeStruct((8, qw), F32),
                   jax.ShapeDtypeStruct((n_heads, WINDOW), F32)],
        scratch_shapes=[pltpu.VMEM((qw, WINDOW), F32), pltpu.VMEM((2 * kvw, WINDOW), F32), pltpu.VMEM((d, WINDOW), F32),
                        pltpu.VMEM((d, WINDOW), F32), pltpu.VMEM((d, WINDOW), F32), pltpu.VMEM((d, WINDOW), F32),
                        pltpu.VMEM((d, WINDOW), F32), pltpu.VMEM((qw, WINDOW), F32)],
        compiler_params=_params(("arbitrary", "arbitrary")),
    )(sinks, qkv, qkv, qkv, o, o, do, do, lse, lse, cos_t, sin_t, cos_t, sin_t, cos_t, sin_t)


def _place():
    return lax.axis_index("x"), lax.axis_index("y"), lax.axis_index("c")


def _other_chips(x, y):
    return [(1 - x, y), (x, 1 - y), (1 - x, 1 - y)]


def _place_shard(w, axis, q, name):
    ly, k, n = w.shape
    tr = _pick(k, (256, 128, 64, 32, 16, 8))
    steps = k // tr
    shape = (ly, k * N_CHIPS, n) if axis == 1 else (ly, k, n * N_CHIPS)
    if axis == 1:
        out_spec = pl.BlockSpec((None, tr, n), lambda l, i, q_ref: (l, q_ref[0] * steps + i, 0))
    else:
        out_spec = pl.BlockSpec((None, tr, n), lambda l, i, q_ref: (l, i, q_ref[0]))

    def body(q_ref, w_ref, o_ref):
        del q_ref
        o_ref[...] = w_ref[...].astype(BF16)

    return pl.pallas_call(
        body,
        name=name,
        grid_spec=pltpu.PrefetchScalarGridSpec(
            num_scalar_prefetch=1, grid=(ly, steps),
            in_specs=[pl.BlockSpec((None, tr, n), lambda l, i, q_ref: (l, i, 0))], out_specs=out_spec),
        out_shape=jax.ShapeDtypeStruct(shape, BF16),
        compiler_params=_params(("parallel", "parallel")),
    )(q, w)


def _gather_weights(fulls, axes, name):
    n = len(fulls)

    def body(*refs):
        dst = refs[n:2 * n]
        send_sems, recv_sems = refs[2 * n:]
        x, y, c = _place()
        chips = _other_chips(x, y)

        def half(i, px, py, pc):
            ref, blk = dst[i], 2 * px + py
            if axes[i] == 1:
                rows = ref.shape[1] // (2 * N_CHIPS)
                return ref.at[:, pl.ds(pl.multiple_of((2 * blk + pc) * rows, 8), rows), :]
            rows, width = ref.shape[1] // 2, ref.shape[2] // N_CHIPS
            return ref.at[:, pl.ds(pl.multiple_of(pc * rows, 8), rows), pl.ds(pl.multiple_of(blk * width, 128), width)]

        def copy(i, m, piece, to):
            return pltpu.make_async_remote_copy(
                src_ref=half(i, *piece), dst_ref=half(i, *piece), send_sem=send_sems.at[i * 6 + m],
                recv_sem=recv_sems.at[i * 6 + m], device_id=to, device_id_type=MESH)

        sends = [copy(i, k, (x, y, c), (*chip, c)) for i in range(n) for k, chip in enumerate(chips)]
        for cp in sends:
            cp.start()
        passed = []
        for i in range(n):
            for k, chip in enumerate(chips):
                copy(i, k, (*chip, c), (*chip, c)).wait_recv()
                passed.append(copy(i, 3 + k, (*chip, c), (x, y, 1 - c)))
                passed[-1].start()
        for i in range(n):
            for k, chip in enumerate(chips):
                copy(i, 3 + k, (*chip, 1 - c), (x, y, 1 - c)).wait_recv()
        for cp in sends + passed:
            cp.wait_send()

    return pl.pallas_call(
        body,
        name=name,
        in_specs=[ANY] * n,
        out_specs=[ANY] * n,
        out_shape=[jax.ShapeDtypeStruct(f.shape, f.dtype) for f in fulls],
        input_output_aliases={i: i for i in range(n)},
        scratch_shapes=[pltpu.SemaphoreType.DMA((6 * n,)), pltpu.SemaphoreType.DMA((6 * n,))],
    )(*fulls)


def _half_shape(kind, shape):
    if kind == "col":
        return (shape[0] // 2, shape[1])
    return (N_CHIPS, shape[1] // 2, shape[2])


def _half_of(kind, ref, h):
    if kind == "col":
        r = ref.shape[0] // 2
        return ref.at[pl.ds(pl.multiple_of(h * r, 8), r), :]
    r = ref.shape[1] // 2
    return ref.at[:, pl.ds(pl.multiple_of(h * r, 8), r), :]


def _slice_of_half(kind, ref, s):
    if kind == "col":
        w = ref.shape[1] // N_CHIPS
        return ref.at[:, pl.ds(pl.multiple_of(s * w, 128), w)]
    return ref.at[s]


def _pair_exchange(grads, kinds, name):
    n = len(grads)
    outs = [jax.ShapeDtypeStruct(_half_shape(kd, g.shape), g.dtype) for g, kd in zip(grads, kinds)]

    def body(*refs):
        src, dst = refs[:n], refs[n:2 * n]
        send_sems, recv_sems = refs[2 * n:]
        x, y, c = _place()
        copies = [pltpu.make_async_remote_copy(
            src_ref=_half_of(kinds[i], src[i], 1 - c), dst_ref=dst[i],
            send_sem=send_sems.at[i], recv_sem=recv_sems.at[i], device_id=(x, y, 1 - c), device_id_type=MESH)
            for i in range(n)]
        for cp in copies:
            cp.start()
        for cp in copies:
            cp.wait_recv()
        for cp in copies:
            cp.wait_send()

    return pl.pallas_call(
        body,
        name=name,
        in_specs=[ANY] * n,
        out_specs=[ANY] * n,
        out_shape=outs,
        scratch_shapes=[pltpu.SemaphoreType.DMA((n,)), pltpu.SemaphoreType.DMA((n,))],
    )(*grads)


def _pair_sum(grad, recv, kind, c, name):
    if kind == "col":
        k, n = grad.shape
        rows = k // 2
        tr = _pick(rows, (256, 128, 64, 32, 16, 8))
        steps = rows // tr
        grid = (steps,)
        g_spec = pl.BlockSpec((tr, n), lambda i, c_ref: (c_ref[0] * steps + i, 0))
        r_spec = pl.BlockSpec((tr, n), lambda i, c_ref: (i, 0))
        g_in = grad
    else:
        _, k4, n = grad.shape
        r8 = k4 // 2
        grid = (N_CHIPS,)
        g_spec = pl.BlockSpec((None, None, r8, n), lambda s, c_ref: (s, c_ref[0], 0, 0))
        r_spec = pl.BlockSpec((None, r8, n), lambda s, c_ref: (s, 0, 0))
        g_in = grad.reshape(N_CHIPS, 2, r8, n)

    def body(c_ref, g_ref, r_ref, o_ref):
        del c_ref
        o_ref[...] = (g_ref[...].astype(F32) + r_ref[...].astype(F32)).astype(o_ref.dtype)

    return pl.pallas_call(
        body,
        name=name,
        grid_spec=pltpu.PrefetchScalarGridSpec(num_scalar_prefetch=1, grid=grid, in_specs=[g_spec, r_spec], out_specs=r_spec),
        out_shape=jax.ShapeDtypeStruct(recv.shape, recv.dtype),
        compiler_params=_params(("parallel",)),
    )(c, g_in, recv)


def _chip_scatter(sums, kinds, name):
    flat = [(i, l, s) for i, ss in enumerate(sums) for l, s in enumerate(ss)]
    n, n_t = len(flat), len(sums)
    outs = []
    for ss, kd in zip(sums, kinds):
        s = ss[0]
        shp = (s.shape[0], s.shape[1] // N_CHIPS) if kd == "col" else s.shape[1:]
        outs += [jax.ShapeDtypeStruct((len(ss), N_CHIPS) + tuple(shp), s.dtype)] * 2
    per = 7

    def body(*refs):
        src = refs[:n]
        mine = [refs[n + 2 * i] for i in range(n_t)]
        sib = [refs[n + 2 * i + 1] for i in range(n_t)]
        send_sems, recv_sems, local_sems = refs[n + 2 * n_t:]
        x, y, c = _place()
        q = 2 * x + y
        chips = _other_chips(x, y)
        sibling = (x, y, 1 - c)

        def copy(j, m, src_ref, dst_ref, to):
            return pltpu.make_async_remote_copy(
                src_ref=src_ref, dst_ref=dst_ref, send_sem=send_sems.at[j * per + m], recv_sem=recv_sems.at[j * per + m],
                device_id=to, device_id_type=MESH)

        local, sends = [], []
        for j, (i, l, _) in enumerate(flat):
            own = _slice_of_half(kinds[i], src[j], q)
            local.append(pltpu.make_async_copy(own, mine[i].at[l, q], local_sems.at[j]))
            sends.append(copy(j, 3, own, sib[i].at[l, q], sibling))
            for k, chip in enumerate(chips):
                sends.append(copy(j, k, _slice_of_half(kinds[i], src[j], 2 * chip[0] + chip[1]), mine[i].at[l, q], (*chip, c)))
        for cp in local + sends:
            cp.start()
        for j, (i, l, _) in enumerate(flat):
            for k, chip in enumerate(chips):
                slot = 2 * chip[0] + chip[1]
                copy(j, k, mine[i].at[l, slot], mine[i].at[l, slot], (*chip, c)).wait_recv()
                sends.append(copy(j, 4 + k, mine[i].at[l, slot], sib[i].at[l, slot], sibling))
                sends[-1].start()
        for j, (i, l, _) in enumerate(flat):
            copy(j, 3, sib[i].at[l, q], sib[i].at[l, q], sibling).wait_recv()
            for k, chip in enumerate(chips):
                slot = 2 * chip[0] + chip[1]
                copy(j, 4 + k, sib[i].at[l, slot], sib[i].at[l, slot], sibling).wait_recv()
        for cp in sends:
            cp.wait_send()
        for cp in local:
            cp.wait()

    return pl.pallas_call(
        body,
        name=name,
        in_specs=[ANY] * n,
        out_specs=[ANY] * (2 * n_t),
        out_shape=outs,
        scratch_shapes=[pltpu.SemaphoreType.DMA((per * n,)), pltpu.SemaphoreType.DMA((per * n,)), pltpu.SemaphoreType.DMA((n,))],
    )(*[s for _, _, s in flat])


def _reduce_adamw(mine, sib, w, m, v, c, name):
    ly, _, r, cols = mine.shape
    tr = _pick(r, (128, 64, 32, 16, 8))
    steps = r // tr
    c1 = 1.0 - ADAM_B1 ** ADAM_STEP
    c2 = 1.0 - ADAM_B2 ** ADAM_STEP

    def body(c_ref, mine_ref, sib_ref, w_ref, m_ref, v_ref, g_ref, d_ref, nm_ref, nv_ref):
        def total(ref):
            acc = ref[0].astype(F32)
            for s in range(1, N_CHIPS):
                acc = acc + ref[s].astype(F32)
            return acc

        gv = jnp.where(pl.program_id(1) == c_ref[0], total(mine_ref), total(sib_ref))
        nm = ADAM_B1 * m_ref[...] + (1.0 - ADAM_B1) * gv
        nv = ADAM_B2 * v_ref[...] + (1.0 - ADAM_B2) * (gv * gv)
        g_ref[...] = gv
        d_ref[...] = -ADAM_LR * ((nm / c1) / (jnp.sqrt(nv / c2) + ADAM_EPS) + ADAM_WD * w_ref[...])
        nm_ref[...] = nm
        nv_ref[...] = nv

    slot_spec = pl.BlockSpec((None, N_CHIPS, tr, cols), lambda l, h, i, c_ref: (l, 0, i, 0))
    spec = pl.BlockSpec((None, tr, cols), lambda l, h, i, c_ref: (l, h * steps + i, 0))
    shp = jax.ShapeDtypeStruct(w.shape, F32)
    return pl.pallas_call(
        body,
        name=name,
        grid_spec=pltpu.PrefetchScalarGridSpec(
            num_scalar_prefetch=1, grid=(ly, N_CORES, steps),
            in_specs=[slot_spec, slot_spec, spec, spec, spec], out_specs=[spec] * 4),
        out_shape=[shp] * 4,
        compiler_params=_params(("parallel", "parallel", "parallel")),
    )(c, mine, sib, w, m, v)


def _allreduce_small(v, name):
    r, w = v.shape

    def body(v_ref, o_ref, buf_ref, send_sems, recv_sems):
        x, y, c = _place()
        me = 4 * x + 2 * y + c

        def peer(k):
            return x ^ (k >> 2), y ^ ((k >> 1) & 1), c ^ (k & 1)

        def remote(k, slot):
            return pltpu.make_async_remote_copy(
                src_ref=v_ref, dst_ref=buf_ref.at[slot], send_sem=send_sems.at[k - 1], recv_sem=recv_sems.at[k - 1],
                device_id=peer(k), device_id_type=MESH)

        sends = [remote(k, me) for k in range(1, N_DEV)]
        for cp in sends:
            cp.start()
        buf_ref[me] = v_ref[...]
        for k in range(1, N_DEV):
            px, py, pc = peer(k)
            remote(k, 4 * px + 2 * py + pc).wait_recv()
        for cp in sends:
            cp.wait_send()
        acc = buf_ref[0]
        for dev in range(1, N_DEV):
            acc = acc + buf_ref[dev]
        o_ref[...] = acc

    vm = pl.BlockSpec(memory_space=pltpu.VMEM)
    return pl.pallas_call(
        body,
        name=name,
        in_specs=[vm],
        out_specs=vm,
        out_shape=jax.ShapeDtypeStruct((r, w), F32),
        scratch_shapes=[pltpu.VMEM((N_DEV, r, w), F32), pltpu.SemaphoreType.DMA((N_DEV - 1,)), pltpu.SemaphoreType.DMA((N_DEV - 1,))],
        compiler_params=pltpu.CompilerParams(vmem_limit_bytes=VMEM_LIMIT_BYTES),
    )(v)


def _adamw(w, g, m, v, name):
    ly, r, c = w.shape
    tr = _pick(r, (256, 128, 64, 32, 16, 8))
    c1 = 1.0 - ADAM_B1 ** ADAM_STEP
    c2 = 1.0 - ADAM_B2 ** ADAM_STEP

    def body(w_ref, g_ref, m_ref, v_ref, d_ref, nm_ref, nv_ref):
        gv = g_ref[...]
        nm = ADAM_B1 * m_ref[...] + (1.0 - ADAM_B1) * gv
        nv = ADAM_B2 * v_ref[...] + (1.0 - ADAM_B2) * (gv * gv)
        d_ref[...] = -ADAM_LR * ((nm / c1) / (jnp.sqrt(nv / c2) + ADAM_EPS) + ADAM_WD * w_ref[...])
        nm_ref[...] = nm
        nv_ref[...] = nv

    spec = pl.BlockSpec((None, tr, c), lambda l, i: (l, i, 0))
    shp = jax.ShapeDtypeStruct((ly, r, c), F32)
    return pl.pallas_call(
        body,
        name=name,
        grid=(ly, r // tr),
        in_specs=[spec] * 4,
        out_specs=[spec] * 3,
        out_shape=[shp] * 3,
        compiler_params=_params(("parallel", "parallel")),
    )(w, g, m, v)


def _rope_tables(seq):
    pos = jnp.arange(seq, dtype=F32)
    inv_freq = 1.0 / (ROPE_THETA ** (jnp.arange(0, HEAD_DIM, 2, dtype=F32) / HEAD_DIM))
    ang = (pos[:, None] * inv_freq[None, :]).T
    cos, sin = jnp.cos(ang), jnp.sin(ang)
    return jnp.concatenate([cos, cos], axis=0), jnp.concatenate([-sin, sin], axis=0)


def _pack(vs, fill=0.0):
    p = jnp.concatenate([v.reshape(-1) for v in vs])
    size = -(-p.shape[0] // 8192) * 8192
    return jnp.pad(p, (0, size - p.shape[0]), constant_values=fill).reshape(-1, 1024)


def _unpack(p, like):
    p = p.reshape(-1)
    out, o = [], 0
    for v in like:
        n = int(math.prod(v.shape))
        out.append(p[o:o + n].reshape(v.shape))
        o += n
    return out


def kernel(x, norm_mix, norm_ffn, norm_final, conv_w_in, conv_w_conv, conv_w_out, attn_w_qkv, attn_b_qkv, attn_sinks, attn_w_o, attn_b_o, ffn_w_in, ffn_w_conv, ffn_w_down, loss_target, m_norm_mix, m_norm_ffn, m_norm_final, m_conv_w_in, m_conv_w_conv, m_conv_w_out, m_attn_w_qkv, m_attn_b_qkv, m_attn_sinks, m_attn_w_o, m_attn_b_o, m_ffn_w_in, m_ffn_w_conv, m_ffn_w_down, v_norm_mix, v_norm_ffn, v_norm_final, v_conv_w_in, v_conv_w_conv, v_conv_w_out, v_attn_w_qkv, v_attn_b_qkv, v_attn_sinks, v_attn_w_o, v_attn_b_o, v_ffn_w_in, v_ffn_w_conv, v_ffn_w_down):
    bsz, seq, d = x.shape
    t = bsz * seq
    depth = norm_mix.shape[0]
    n_conv, n_attn = conv_w_in.shape[0], attn_w_qkv.shape[0]
    xq, yq, cq = _place()
    q = 2 * xq + yq

    big = [conv_w_in, conv_w_out, attn_w_qkv, attn_w_o, ffn_w_in, ffn_w_down]
    axes = [2, 1, 2, 1, 2, 1]
    q_arr = q.astype(jnp.int32).reshape(1)
    c_arr = cq.astype(jnp.int32).reshape(1)
    w_cin, w_cout, w_qkv, w_o, w_fin, w_fdown = _gather_weights(
        [_place_shard(w, ax, q_arr, f"place_shard{n}") for n, (w, ax) in enumerate(zip(big, axes))], axes, "gather_weights")

    small_cols = [conv_w_conv, attn_b_qkv, attn_b_o, ffn_w_conv]

    def placed(v):
        width = v.shape[-1]
        full = jnp.zeros(v.shape[:-1] + (N_CHIPS * width,), F32)
        return lax.dynamic_update_slice_in_dim(full, v * (1.0 / N_CORES), q * width, axis=v.ndim - 1)

    full_cols = [placed(v) for v in small_cols]
    wc_conv, b_qkv, b_o, wf_conv = _unpack(_allreduce_small(_pack(full_cols), "gather_small"), full_cols)
    cos_t, sin_t = _rope_tables(seq)

    xs = x.reshape(t, d)
    saved = []
    for i in range(depth):
        j = i // 2
        h = _rms_fwd(xs, norm_mix[i:i + 1], f"norm_mix_fwd{i}")
        if i % 2 == 0:
            pre = _mm(h, w_cin, "nn", BF16, layer=j, tm=1024, tn=768, tk=4096, name=f"conv_in_fwd{i}")
            mixed = _convgate_fwd(pre, wc_conv[j], seq, f"conv_gate_fwd{i}")
            x_mid = _mm(mixed, w_cout, "nn", F32, layer=j, residual=xs, tm=512, tn=1024, tk=4096, name=f"conv_out_fwd{i}")
            lse = None
        else:
            pre = _mm(h, w_qkv, "nn", F32, layer=j, bias=b_qkv[j:j + 1], tm=1024, tn=768, tk=4096, name=f"qkv_fwd{i}")
            mixed, lse = _attn_fwd(pre, attn_sinks[j], cos_t, sin_t, bsz, seq, f"attn_fwd{i}")
            x_mid = _mm(mixed, w_o, "nn", F32, layer=j, bias=b_o[j:j + 1], residual=xs, tm=512, tn=1024, tk=4096,
                        name=f"attn_out_fwd{i}")
        h2 = _rms_fwd(x_mid, norm_ffn[i:i + 1], f"norm_ffn_fwd{i}")
        gu = _mm(h2, w_fin, "nn", BF16, layer=i, n_outer=True, tm=512, tn=1408, tk=4096, name=f"ffn_in_fwd{i}")
        act = _ffngate_fwd(gu, wf_conv[i], seq, f"ffn_gate_fwd{i}")
        x_next = _mm(act, w_fdown, "nn", F32, layer=i, residual=x_mid, tm=512, tn=1024, tk=4096, name=f"ffn_down_fwd{i}")
        saved.append((xs, h, pre, mixed, lse, x_mid, h2, gu, act))
        xs = x_next

    dx, dxb, sq, dg_final = _loss_head(xs, loss_target.reshape(t, d), norm_final.reshape(1, d), "loss_head")
    loss = lax.psum(0.5 * jnp.sum(sq) / d, ("x", "y", "c"))

    g_norm_mix, g_norm_ffn = [None] * depth, [None] * depth
    g_cin, g_cconv, g_cout = [None] * n_conv, [None] * n_conv, [None] * n_conv
    g_qkv, g_bqkv, g_sinks, g_o, g_bo = ([None] * n_attn for _ in range(5))
    g_fin, g_fconv, g_fdown = [None] * depth, [None] * depth, [None] * depth
    for i in reversed(range(depth)):
        j = i // 2
        x_in, h, pre, mixed, lse, x_mid, h2, gu, act = saved[i]
        da = _mm(dxb, w_fdown, "nt", BF16, layer=i, n_outer=True, tm=512, tn=1408, tk=4096, name=f"ffn_down_dx{i}")
        g_fdown[i] = _mm(act, dxb, "tn", BF16, tm=1408, tn=1024, tk=2048, name=f"ffn_down_dw{i}")
        dgu, dwc = _ffngate_bwd(gu, da, wf_conv[i], seq, f"ffn_gate_bwd{i}")
        g_fconv[i] = dwc[:3]
        g_fin[i] = _mm(h2, dgu, "tn", BF16, tm=1024, tn=1408, tk=2048, name=f"ffn_in_dw{i}")
        dh2 = _mm(dgu, w_fin, "nt", F32, layer=i, tm=1024, tn=1024, tk=1408, name=f"ffn_in_dx{i}")
        dx, dxb, dg, colsum = _rms_bwd(x_mid, dh2, norm_ffn[i:i + 1], dx, f"norm_ffn_bwd{i}")
        g_norm_ffn[i] = jnp.sum(dg, axis=0)
        if i % 2 == 0:
            dmix = _mm(dxb, w_cout, "nt", BF16, layer=j, tm=512, tn=1024, tk=4096, name=f"conv_out_dx{i}")
            g_cout[j] = _mm(mixed, dxb, "tn", BF16, tm=1024, tn=1024, tk=2048, name=f"conv_out_dw{i}")
            dpre, dwc = _convgate_bwd(pre, dmix, wc_conv[j], seq, f"conv_gate_bwd{i}")
            g_cconv[j] = dwc[:3]
            g_cin[j] = _mm(h, dpre, "tn", BF16, tm=1024, tn=1536, tk=2048, name=f"conv_in_dw{i}")
            dh = _mm(dpre, w_cin, "nt", F32, layer=j, tm=1024, tn=1024, tk=1536, name=f"conv_in_dx{i}")
        else:
            g_bo[j] = jnp.sum(colsum, axis=0)
            dmix = _mm(dxb, w_o, "nt", F32, layer=j, tm=512, tn=1024, tk=4096, name=f"attn_out_dx{i}")
            g_o[j] = _mm(mixed, dxb, "tn", BF16, tm=1024, tn=1024, tk=2048, name=f"attn_out_dw{i}")
            dpre, dbias, dsk = _attn_bwd(pre, mixed, lse, dmix, attn_sinks[j], cos_t, sin_t, bsz, seq, f"attn_bwd{i}")
            g_bqkv[j] = jnp.sum(dbias, axis=0)
            g_sinks[j] = jnp.sum(dsk, axis=1)
            g_qkv[j] = _mm(h, dpre, "tn", BF16, tm=1024, tn=1536, tk=2048, name=f"qkv_dw{i}")
            dh = _mm(dpre, w_qkv, "nt", F32, layer=j, tm=512, tn=1024, tk=1536, name=f"qkv_dx{i}")
        dx, dxb, dg, _ = _rms_bwd(x_in, dh, norm_mix[i:i + 1], dx, f"norm_mix_bwd{i}")
        g_norm_mix[i] = jnp.sum(dg, axis=0)
    grad_x = dx.reshape(bsz, seq, d)

    tensors = [(g_cin, "col"), (g_cout, "row"), (g_qkv, "col"), (g_o, "row"), (g_fin, "col"), (g_fdown, "row")]
    t_kinds = [kd for _, kd in tensors]
    flat, kinds = [], []
    for gs, kd in tensors:
        for g in gs:
            flat.append(g if kd == "col" else g.reshape(N_CHIPS, g.shape[0] // N_CHIPS, g.shape[1]))
            kinds.append(kd)
    recv = _pair_exchange(flat, kinds, "grad_pair_exchange")
    sums = [_pair_sum(g, r, kd, c_arr, f"grad_pair_sum{n}") for n, (g, r, kd) in enumerate(zip(flat, recv, kinds))]
    by_tensor, pos = [], 0
    for gs, _ in tensors:
        by_tensor.append(sums[pos:pos + len(gs)])
        pos += len(gs)
    slots = _chip_scatter(by_tensor, t_kinds, "grad_chip_scatter")
    big_w = [conv_w_in, conv_w_out, attn_w_qkv, attn_w_o, ffn_w_in, ffn_w_down]
    big_m = [m_conv_w_in, m_conv_w_out, m_attn_w_qkv, m_attn_w_o, m_ffn_w_in, m_ffn_w_down]
    big_v = [v_conv_w_in, v_conv_w_out, v_attn_w_qkv, v_attn_w_o, v_ffn_w_in, v_ffn_w_down]
    big_names = ["conv_w_in", "conv_w_out", "attn_w_qkv", "attn_w_o", "ffn_w_in", "ffn_w_down"]
    big_upd = [_reduce_adamw(slots[2 * n], slots[2 * n + 1], big_w[n], big_m[n], big_v[n], c_arr, f"adamw_{nm}")
               for n, nm in enumerate(big_names)]

    small = [jnp.stack(g_norm_mix), jnp.stack(g_norm_ffn), jnp.sum(dg_final, axis=0), jnp.stack(g_cconv),
             jnp.stack(g_bqkv), jnp.stack(g_sinks), jnp.stack(g_bo), jnp.stack(g_fconv)]
    sg = _unpack(_allreduce_small(_pack(small), "grad_small_allreduce"), small)

    def my_cols(v, like):
        width = like.shape[-1]
        return lax.dynamic_slice_in_dim(v, q * width, width, axis=v.ndim - 1)

    small_w = [norm_mix, norm_ffn, norm_final, conv_w_conv, attn_b_qkv, attn_sinks, attn_b_o, ffn_w_conv]
    small_m = [m_norm_mix, m_norm_ffn, m_norm_final, m_conv_w_conv, m_attn_b_qkv, m_attn_sinks, m_attn_b_o, m_ffn_w_conv]
    small_v = [v_norm_mix, v_norm_ffn, v_norm_final, v_conv_w_conv, v_attn_b_qkv, v_attn_sinks, v_attn_b_o, v_ffn_w_conv]
    small_g = [sg[0], sg[1], sg[2], my_cols(sg[3], conv_w_conv), my_cols(sg[4], attn_b_qkv), sg[5],
               my_cols(sg[6], attn_b_o), my_cols(sg[7], ffn_w_conv)]

    upd = {nm: tuple(u[1:]) for nm, u in zip(big_names, big_upd)}
    sd, sm, sv = _adamw(_pack(small_w)[None], _pack(small_g)[None], _pack(small_m)[None], _pack(small_v, 1.0)[None],
                        "adamw_small")
    sd, sm, sv = _unpack(sd, small_w), _unpack(sm, small_w), _unpack(sv, small_w)
    names = ["norm_mix", "norm_ffn", "norm_final", "conv_w_in", "conv_w_conv", "conv_w_out", "attn_w_qkv", "attn_b_qkv",
             "attn_sinks", "attn_w_o", "attn_b_o", "ffn_w_in", "ffn_w_conv", "ffn_w_down"]
    small_names = ["norm_mix", "norm_ffn", "norm_final", "conv_w_conv", "attn_b_qkv", "attn_sinks", "attn_b_o", "ffn_w_conv"]
    grads = dict(zip(small_names, small_g))
    grads.update({nm: u[0] for nm, u in zip(big_names, big_upd)})
    for n, nm in enumerate(small_names):
        upd[nm] = (sd[n], sm[n], sv[n])
    return (loss, grad_x, *[grads[nm] for nm in names], *[upd[nm][0] for nm in names],
            *[upd[nm][1] for nm in names], *[upd[nm][2] for nm in names])
```

```python
import math

import jax
import jax.numpy as jnp
from jax import lax
from jax.experimental import pallas as pl
from jax.experimental.pallas import tpu as pltpu

F32 = jnp.float32
BF16 = jnp.bfloat16

HEAD_DIM = 64
GROUP = 4
WINDOW = 128
EPS = 1e-5
ROPE_THETA = 10000.0
ADAM_LR, ADAM_B1, ADAM_B2, ADAM_EPS, ADAM_WD, ADAM_STEP = 0.001, 0.9, 0.999, 1e-08, 0.01, 10

N_CHIPS = 4
N_CORES = 2
N_DEV = 8
HALO = 16
VMEM_LIMIT_BYTES = 56 * 1024 * 1024
MESH = pl.DeviceIdType.MESH
ANY = pl.BlockSpec(memory_space=pl.ANY)
SMEM = pl.BlockSpec(memory_space=pltpu.SMEM)
NEG = float(jnp.finfo(jnp.float32).min)
ROW_TILES = (512, 256, 128, 64, 32, 16, 8)


def _pick(dim, cands):
    for c in cands:
        if dim % c == 0:
            return c
    return dim


def _params(sem):
    return pltpu.CompilerParams(dimension_semantics=sem, vmem_limit_bytes=VMEM_LIMIT_BYTES)


_DIMS = {"nn": (((1,), (0,)), ((), ())), "nt": (((1,), (1,)), ((), ())), "tn": (((0,), (0,)), ((), ()))}


def _mm(a, b, mode, out_dtype, *, layer=None, bias=None, residual=None, n_outer=False, tm, tn, tk, name):
    b2 = b.shape[1:] if layer is not None else b.shape
    if mode == "nn":
        (m, k), n = a.shape, b2[1]
    elif mode == "nt":
        (m, k), n = a.shape, b2[0]
    else:
        (k, m), n = a.shape, b2[1]
    tm, tn, tk = min(tm, m), min(tn, n), min(tk, k)
    assert m % tm == 0 and n % tn == 0 and k % tk == 0, (name, a.shape, b.shape, tm, tn, tk)
    nk = k // tk

    def at(f):
        return (lambda p0, p1, p2: f(p1, p0, p2)) if n_outer else f

    a_spec = pl.BlockSpec((tk, tm), at(lambda i, j, l: (l, i))) if mode == "tn" else pl.BlockSpec((tm, tk), at(lambda i, j, l: (i, l)))
    if layer is None:
        b_spec = (pl.BlockSpec((tn, tk), at(lambda i, j, l: (j, l))) if mode == "nt"
                  else pl.BlockSpec((tk, tn), at(lambda i, j, l: (l, j))))
    elif mode == "nt":
        b_spec = pl.BlockSpec((None, tn, tk), at(lambda i, j, l: (layer, j, l)))
    else:
        b_spec = pl.BlockSpec((None, tk, tn), at(lambda i, j, l: (layer, l, j)))
    in_specs, args = [a_spec, b_spec], [a, b]
    if bias is not None:
        in_specs.append(pl.BlockSpec((1, tn), at(lambda i, j, l: (0, j))))
        args.append(bias)
    if residual is not None:
        in_specs.append(pl.BlockSpec((tm, tn), at(lambda i, j, l: (i, j))))
        args.append(residual)
    has_bias, has_res = bias is not None, residual is not None

    def body(*refs):
        a_ref, b_ref = refs[0], refs[1]
        pos = 2
        bias_ref = res_ref = None
        if has_bias:
            bias_ref, pos = refs[pos], pos + 1
        if has_res:
            res_ref, pos = refs[pos], pos + 1
        o_ref = refs[pos]
        acc_ref = refs[pos + 1] if nk > 1 else None

        def finish(acc):
            if has_bias:
                acc = acc + bias_ref[...]
            if has_res:
                acc = acc + res_ref[...]
            o_ref[...] = acc.astype(o_ref.dtype)

        if nk == 1:
            finish(lax.dot_general(a_ref[...], b_ref[...], _DIMS[mode], preferred_element_type=F32))
            return
        l = pl.program_id(2)
        part = lax.dot_general(a_ref[...], b_ref[...], _DIMS[mode], preferred_element_type=F32)

        @pl.when(l == 0)
        def _():
            acc_ref[...] = part

        @pl.when(l > 0)
        def _():
            acc_ref[...] += part

        @pl.when(l == nk - 1)
        def _():
            finish(acc_ref[...])

    return pl.pallas_call(
        body,
        name=name,
        grid=(n // tn, m // tm, nk) if n_outer else (m // tm, n // tn, nk),
        in_specs=in_specs,
        out_specs=pl.BlockSpec((tm, tn), at(lambda i, j, l: (i, j))),
        out_shape=jax.ShapeDtypeStruct((m, n), out_dtype),
        scratch_shapes=[pltpu.VMEM((tm, tn), F32)] if nk > 1 else [],
        compiler_params=_params(("parallel", "parallel", "arbitrary")),
    )(*args)


def _fold8(v):
    r, d = v.shape
    return jnp.sum(v.reshape(r // 8, 8, d), axis=0)


def _rms_fwd(x, g, name):
    t, d = x.shape
    tm = _pick(t, ROW_TILES)

    def body(x_ref, g_ref, h_ref):
        xv = x_ref[...]
        r = lax.rsqrt(jnp.mean(xv * xv, axis=-1, keepdims=True) + EPS)
        h_ref[...] = (xv * r * g_ref[...]).astype(BF16)

    return pl.pallas_call(
        body,
        name=name,
        grid=(t // tm,),
        in_specs=[pl.BlockSpec((tm, d), lambda i: (i, 0)), pl.BlockSpec((1, d), lambda i: (0, 0))],
        out_specs=pl.BlockSpec((tm, d), lambda i: (i, 0)),
        out_shape=jax.ShapeDtypeStruct((t, d), BF16),
        compiler_params=_params(("parallel",)),
    )(x, g)


def _rms_bwd(x, dh, g, dx_in, name):
    t, d = x.shape
    tm = _pick(t, ROW_TILES)

    def body(x_ref, dh_ref, g_ref, dxi_ref, dx_ref, dxb_ref, dg_ref, cs_ref):
        i = pl.program_id(0)
        xv = x_ref[...]
        r = lax.rsqrt(jnp.mean(xv * xv, axis=-1, keepdims=True) + EPS)
        xhat = xv * r
        dy = dh_ref[...]
        gdy = dy * g_ref[...]
        dx = dxi_ref[...] + r * (gdy - xhat * jnp.mean(gdy * xhat, axis=-1, keepdims=True))
        dx_ref[...] = dx
        dxb_ref[...] = dx.astype(BF16)

        @pl.when(i == 0)
        def _():
            dg_ref[...] = jnp.zeros_like(dg_ref)
            cs_ref[...] = jnp.zeros_like(cs_ref)

        dg_ref[...] += _fold8(dy * xhat)
        cs_ref[...] += _fold8(dx)

    row = pl.BlockSpec((tm, d), lambda i: (i, 0))
    acc = pl.BlockSpec((8, d), lambda i: (0, 0))
    return pl.pallas_call(
        body,
        name=name,
        grid=(t // tm,),
        in_specs=[row, row, pl.BlockSpec((1, d), lambda i: (0, 0)), row],
        out_specs=[row, row, acc, acc],
        out_shape=[jax.ShapeDtypeStruct((t, d), F32), jax.ShapeDtypeStruct((t, d), BF16),
                   jax.ShapeDtypeStruct((8, d), F32), jax.ShapeDtypeStruct((8, d), F32)],
        compiler_params=_params(("arbitrary",)),
    )(x, dh, g, dx_in)


def _loss_head(x, target, g, name):
    t, d = x.shape
    tm = _pick(t, ROW_TILES)
    inv_d = 1.0 / d

    def body(x_ref, t_ref, g_ref, dx_ref, dxb_ref, sq_ref, dg_ref):
        i = pl.program_id(0)
        xv = x_ref[...]
        gv = g_ref[...]
        r = lax.rsqrt(jnp.mean(xv * xv, axis=-1, keepdims=True) + EPS)
        xhat = xv * r
        err = xhat * gv - t_ref[...]
        dy = err * inv_d
        gdy = dy * gv
        dx = r * (gdy - xhat * jnp.mean(gdy * xhat, axis=-1, keepdims=True))
        dx_ref[...] = dx
        dxb_ref[...] = dx.astype(BF16)

        @pl.when(i == 0)
        def _():
            sq_ref[...] = jnp.zeros_like(sq_ref)
            dg_ref[...] = jnp.zeros_like(dg_ref)

        sq_ref[...] += _fold8(err * err)
        dg_ref[...] += _fold8(dy * xhat)

    row = pl.BlockSpec((tm, d), lambda i: (i, 0))
    acc = pl.BlockSpec((8, d), lambda i: (0, 0))
    return pl.pallas_call(
        body,
        name=name,
        grid=(t // tm,),
        in_specs=[row, row, pl.BlockSpec((1, d), lambda i: (0, 0))],
        out_specs=[row, row, acc, acc],
        out_shape=[jax.ShapeDtypeStruct((t, d), F32), jax.ShapeDtypeStruct((t, d), BF16),
                   jax.ShapeDtypeStruct((8, d), F32), jax.ShapeDtypeStruct((8, d), F32)],
        compiler_params=_params(("arbitrary",)),
    )(x, target, g)


def _rows(tm):
    return lax.broadcasted_iota(jnp.int32, (tm, 1), 0)


def _shift_down(u, before2):
    r8 = _rows(8)
    s1, s2 = pltpu.roll(u, 1, 0), pltpu.roll(u, 2, 0)
    top1 = jnp.where(r8 == 0, before2[1:2], s1[:8])
    top2 = jnp.where(r8 == 0, before2[0:1], jnp.where(r8 == 1, before2[1:2], s2[:8]))
    return jnp.concatenate([top1, s1[8:]], axis=0), jnp.concatenate([top2, s2[8:]], axis=0)


def _shift_up(u, after2):
    tm = u.shape[0]
    r8 = _rows(8)
    s1, s2 = pltpu.roll(u, tm - 1, 0), pltpu.roll(u, tm - 2, 0)
    bot1 = jnp.where(r8 == 7, after2[0:1], s1[tm - 8:])
    bot2 = jnp.where(r8 == 6, after2[0:1], jnp.where(r8 == 7, after2[1:2], s2[tm - 8:]))
    return jnp.concatenate([s1[:tm - 8], bot1], axis=0), jnp.concatenate([s2[:tm - 8], bot2], axis=0)


def _conv_tile(seq):
    return _pick(seq, (256, 128, 64, 32, 16, 8))


def _halo_specs(tm, width, n_tiles):
    per = tm // HALO
    before = pl.BlockSpec((HALO, width), lambda i: (jnp.maximum(i * per - 1, 0), 0))
    after = pl.BlockSpec((HALO, width), lambda i: (jnp.minimum((i + 1) * per, n_tiles * per - 1), 0))
    return before, after


def _convgate_fwd(bcv, w, seq, name):
    t, d3 = bcv.shape
    d = d3 // 3
    tm = _conv_tile(seq)
    tps = seq // tm
    before, _ = _halo_specs(tm, d3, t // tm)

    def body(x_ref, xb_ref, w_ref, y_ref):
        i = pl.program_id(0)
        inner = (i % tps != 0).astype(F32)
        u = x_ref[:, d:2 * d].astype(F32) * x_ref[:, 2 * d:].astype(F32)
        xb = xb_ref[:, d:].astype(F32)[HALO - 2:]
        s1, s2 = _shift_down(u, xb[:, :d] * xb[:, d:] * inner)
        z = w_ref[2:3] * u + w_ref[1:2] * s1 + w_ref[0:1] * s2
        y_ref[...] = (x_ref[:, :d].astype(F32) * z).astype(BF16)

    return pl.pallas_call(
        body,
        name=name,
        grid=(t // tm,),
        in_specs=[pl.BlockSpec((tm, d3), lambda i: (i, 0)), before, pl.BlockSpec((3, d), lambda i: (0, 0))],
        out_specs=pl.BlockSpec((tm, d), lambda i: (i, 0)),
        out_shape=jax.ShapeDtypeStruct((t, d), BF16),
        compiler_params=_params(("parallel",)),
    )(bcv, bcv, w)


def _convgate_bwd(bcv, dy, w, seq, name):
    t, d3 = bcv.shape
    d = d3 // 3
    tm = _conv_tile(seq)
    tps = seq // tm
    before, after = _halo_specs(tm, d3, t // tm)
    _, after_dy = _halo_specs(tm, d, t // tm)

    def body(x_ref, xb_ref, xa_ref, dy_ref, dya_ref, w_ref, dx_ref, dw_ref):
        i = pl.program_id(0)
        inner_lo = (i % tps != 0).astype(F32)
        inner_hi = (i % tps != tps - 1).astype(F32)
        w0, w1, w2 = w_ref[0:1], w_ref[1:2], w_ref[2:3]
        b, c, v = x_ref[:, :d].astype(F32), x_ref[:, d:2 * d].astype(F32), x_ref[:, 2 * d:].astype(F32)
        u = c * v
        xb = xb_ref[:, d:].astype(F32)[HALO - 2:]
        s1, s2 = _shift_down(u, xb[:, :d] * xb[:, d:] * inner_lo)
        z = w2 * u + w1 * s1 + w0 * s2
        dyv = dy_ref[...].astype(F32)
        dz = dyv * b
        dza = dya_ref[...].astype(F32)[0:2] * xa_ref[:, :d].astype(F32)[0:2] * inner_hi
        n1, n2 = _shift_up(dz, dza)
        du = w2 * dz + w1 * n1 + w0 * n2
        dx_ref[:, :d] = (dyv * z).astype(BF16)
        dx_ref[:, d:2 * d] = (du * v).astype(BF16)
        dx_ref[:, 2 * d:] = (du * c).astype(BF16)

        @pl.when(i == 0)
        def _():
            dw_ref[...] = jnp.zeros_like(dw_ref)

        dw_ref[0:1] += jnp.sum(dz * s2, axis=0, keepdims=True)
        dw_ref[1:2] += jnp.sum(dz * s1, axis=0, keepdims=True)
        dw_ref[2:3] += jnp.sum(dz * u, axis=0, keepdims=True)

    return pl.pallas_call(
        body,
        name=name,
        grid=(t // tm,),
        in_specs=[pl.BlockSpec((tm, d3), lambda i: (i, 0)), before, after,
                  pl.BlockSpec((tm, d), lambda i: (i, 0)), after_dy, pl.BlockSpec((3, d), lambda i: (0, 0))],
        out_specs=[pl.BlockSpec((tm, d3), lambda i: (i, 0)), pl.BlockSpec((8, d), lambda i: (0, 0))],
        out_shape=[jax.ShapeDtypeStruct((t, d3), BF16), jax.ShapeDtypeStruct((8, d), F32)],
        compiler_params=_params(("arbitrary",)),
    )(bcv, bcv, bcv, dy, dy, w)


def _sigmoid(x):
    return 1.0 / (1.0 + jnp.exp(-x))


def _ffngate_fwd(gu, w, seq, name):
    t, f2 = gu.shape
    f = f2 // 2
    tm = _conv_tile(seq)
    tps = seq // tm
    before, _ = _halo_specs(tm, f2, t // tm)

    def body(x_ref, xb_ref, w_ref, a_ref):
        i = pl.program_id(0)
        inner = (i % tps != 0).astype(F32)
        g = x_ref[:, :f].astype(F32)
        s1, s2 = _shift_down(g, xb_ref[:, :f].astype(F32)[HALO - 2:] * inner)
        gc = w_ref[2:3] * g + w_ref[1:2] * s1 + w_ref[0:1] * s2
        a_ref[...] = (gc * _sigmoid(gc) * x_ref[:, f:].astype(F32)).astype(BF16)

    return pl.pallas_call(
        body,
        name=name,
        grid=(t // tm,),
        in_specs=[pl.BlockSpec((tm, f2), lambda i: (i, 0)), before, pl.BlockSpec((3, f), lambda i: (0, 0))],
        out_specs=pl.BlockSpec((tm, f), lambda i: (i, 0)),
        out_shape=jax.ShapeDtypeStruct((t, f), BF16),
        compiler_params=_params(("parallel",)),
    )(gu, gu, w)


def _ffngate_bwd(gu, da, w, seq, name):
    t, f2 = gu.shape
    f = f2 // 2
    tm = _conv_tile(seq)
    tps = seq // tm
    before, after = _halo_specs(tm, f2, t // tm)
    _, after_da = _halo_specs(tm, f, t // tm)

    def body(x_ref, xb_ref, xa_ref, da_ref, daa_ref, w_ref, dx_ref, dw_ref):
        i = pl.program_id(0)
        inner_lo = (i % tps != 0).astype(F32)
        inner_hi = (i % tps != tps - 1).astype(F32)
        w0, w1, w2 = w_ref[0:1], w_ref[1:2], w_ref[2:3]

        def dgate(gc, uv, dav):
            sg = _sigmoid(gc)
            return dav * uv * (sg * (1.0 + gc * (1.0 - sg))), dav * (gc * sg)

        g, u = x_ref[:, :f].astype(F32), x_ref[:, f:].astype(F32)
        s1, s2 = _shift_down(g, xb_ref[:, :f].astype(F32)[HALO - 2:] * inner_lo)
        gc = w2 * g + w1 * s1 + w0 * s2
        dgc, du = dgate(gc, u, da_ref[...].astype(F32))
        ga = xa_ref[:, :f].astype(F32)
        a1, a2 = _shift_down(ga, x_ref[tm - HALO:, :f].astype(F32)[HALO - 2:])
        gca = w2 * ga + w1 * a1 + w0 * a2
        dgca, _ = dgate(gca, xa_ref[:, f:].astype(F32), daa_ref[...].astype(F32))
        n1, n2 = _shift_up(dgc, dgca[0:2] * inner_hi)
        dx_ref[:, :f] = (w2 * dgc + w1 * n1 + w0 * n2).astype(BF16)
        dx_ref[:, f:] = du.astype(BF16)

        @pl.when(i == 0)
        def _():
            dw_ref[...] = jnp.zeros_like(dw_ref)

        dw_ref[0:1] += jnp.sum(dgc * s2, axis=0, keepdims=True)
        dw_ref[1:2] += jnp.sum(dgc * s1, axis=0, keepdims=True)
        dw_ref[2:3] += jnp.sum(dgc * g, axis=0, keepdims=True)

    return pl.pallas_call(
        body,
        name=name,
        grid=(t // tm,),
        in_specs=[pl.BlockSpec((tm, f2), lambda i: (i, 0)), before, after,
                  pl.BlockSpec((tm, f), lambda i: (i, 0)), after_da, pl.BlockSpec((3, f), lambda i: (0, 0))],
        out_specs=[pl.BlockSpec((tm, f2), lambda i: (i, 0)), pl.BlockSpec((8, f), lambda i: (0, 0))],
        out_shape=[jax.ShapeDtypeStruct((t, f2), BF16), jax.ShapeDtypeStruct((8, f), F32)],
        compiler_params=_params(("arbitrary",)),
    )(gu, gu, gu, da, da, w)


def _swap_halves(xt):
    half = HEAD_DIM // 2
    return jnp.concatenate([xt[half:], xt[:half]], axis=0)


def _rope(xt, cos, sin):
    return xt * cos + _swap_halves(xt) * sin


def _unrope(dxt, cos, sin):
    return dxt * cos - _swap_halves(dxt) * sin


def _key_query():
    kj = lax.broadcasted_iota(jnp.int32, (WINDOW, GROUP * WINDOW), 0)
    qi = lax.broadcasted_iota(jnp.int32, (WINDOW, GROUP * WINDOW), 1) & (WINDOW - 1)
    return kj, qi


def _band_masks(n):
    kj, qi = _key_query()
    return kj <= qi, jnp.logical_and(kj > qi, n > 0)


def _lanes(v):
    return jnp.concatenate([v] * GROUP, axis=1)


def _group(ref, j):
    return jnp.concatenate([ref[(j * GROUP + g) * HEAD_DIM:(j * GROUP + g + 1) * HEAD_DIM, :] for g in range(GROUP)], axis=1)


def _group_rows(ref, j):
    return jnp.concatenate([ref[j * GROUP + g:j * GROUP + g + 1, :] for g in range(GROUP)], axis=1)


def _group_sinks(sink_ref, j):
    return jnp.concatenate([jnp.full((1, WINDOW), sink_ref[j * GROUP + g], F32) for g in range(GROUP)], axis=1)


def _tn(a, b):
    return lax.dot_general(a, b, _DIMS["tn"], preferred_element_type=F32)


def _nt(a, b):
    return lax.dot_general(a, b, _DIMS["nt"], preferred_element_type=F32)


def _nn(a, b):
    return jnp.dot(a, b, preferred_element_type=F32)


def _attn_fwd(qkv, sinks, cos_t, sin_t, bsz, seq, name):
    t, qw = qkv.shape
    d = qw * 2 // 3
    kvw = d // GROUP
    n_heads, n_kv = d // HEAD_DIM, kvw // HEAD_DIM
    nb = seq // WINDOW
    scale = HEAD_DIM ** -0.5

    def body(sink_ref, xc_ref, xp_ref, cc_ref, sc_ref, cp_ref, sp_ref, o_ref, lse_ref, xt_ref, pt_ref, ot_ref):
        n = pl.program_id(1)
        xt_ref[...] = xc_ref[...].T
        pt_ref[...] = xp_ref[:, d:].T
        cos_c, sin_c, cos_p, sin_p = cc_ref[...], sc_ref[...], cp_ref[...], sp_ref[...]
        cos_g, sin_g = _lanes(cos_c), _lanes(sin_c)
        valid_c, valid_p = _band_masks(n)
        for j in range(n_kv):
            ko = j * HEAD_DIM
            kc = _rope(xt_ref[d + ko:d + ko + HEAD_DIM, :], cos_c, sin_c).astype(BF16)
            kp = _rope(pt_ref[ko:ko + HEAD_DIM, :], cos_p, sin_p).astype(BF16)
            vc = xt_ref[d + kvw + ko:d + kvw + ko + HEAD_DIM, :].astype(BF16)
            vp = pt_ref[kvw + ko:kvw + ko + HEAD_DIM, :].astype(BF16)
            q = _rope(_group(xt_ref, j), cos_g, sin_g).astype(BF16)
            sink = _group_sinks(sink_ref, j)
            s_c = jnp.where(valid_c, _tn(kc, q) * scale, NEG)
            s_p = jnp.where(valid_p, _tn(kp, q) * scale, NEG)
            m = jnp.maximum(jnp.maximum(jnp.max(s_c, axis=0, keepdims=True), jnp.max(s_p, axis=0, keepdims=True)), sink)
            p_c = jnp.exp(s_c - m)
            p_p = jnp.exp(s_p - m)
            den = jnp.sum(p_c, axis=0, keepdims=True) + jnp.sum(p_p, axis=0, keepdims=True) + jnp.exp(sink - m)
            inv = 1.0 / den
            o_g = _nn(vc, (p_c * inv).astype(BF16)) + _nn(vp, (p_p * inv).astype(BF16))
            lse_g = m + jnp.log(den)
            for g in range(GROUP):
                h = j * GROUP + g
                ot_ref[h * HEAD_DIM:(h + 1) * HEAD_DIM, :] = o_g[:, g * WINDOW:(g + 1) * WINDOW]
                lse_ref[h:h + 1, :] = lse_g[:, g * WINDOW:(g + 1) * WINDOW]
        o_ref[...] = ot_ref[...].T.astype(BF16)

    cur = lambda b, n: (b * nb + n, 0)
    prev = lambda b, n: (b * nb + jnp.maximum(n - 1, 0), 0)
    tab_c = pl.BlockSpec((HEAD_DIM, WINDOW), lambda b, n: (0, n))
    tab_p = pl.BlockSpec((HEAD_DIM, WINDOW), lambda b, n: (0, jnp.maximum(n - 1, 0)))
    return pl.pallas_call(
        body,
        name=name,
        grid=(bsz, nb),
        in_specs=[SMEM, pl.BlockSpec((WINDOW, qw), cur), pl.BlockSpec((WINDOW, qw), prev), tab_c, tab_c, tab_p, tab_p],
        out_specs=[pl.BlockSpec((WINDOW, d), cur), pl.BlockSpec((n_heads, WINDOW), lambda b, n: (0, b * nb + n))],
        out_shape=[jax.ShapeDtypeStruct((t, d), BF16), jax.ShapeDtypeStruct((n_heads, t), F32)],
        scratch_shapes=[pltpu.VMEM((qw, WINDOW), F32), pltpu.VMEM((2 * kvw, WINDOW), F32), pltpu.VMEM((d, WINDOW), F32)],
        compiler_params=_params(("parallel", "arbitrary")),
    )(sinks, qkv, qkv, cos_t, sin_t, cos_t, sin_t)


def _attn_bwd(qkv, o, lse, do, sinks, cos_t, sin_t, bsz, seq, name):
    t, qw = qkv.shape
    d = qw * 2 // 3
    kvw = d // GROUP
    n_heads, n_kv = d // HEAD_DIM, kvw // HEAD_DIM
    nb = seq // WINDOW
    scale = HEAD_DIM ** -0.5

    def body(sink_ref, xc_ref, xp_ref, xn_ref, oc_ref, on_ref, doc_ref, don_ref, lc_ref, ln_ref,
             cc_ref, sc_ref, cp_ref, sp_ref, cn_ref, sn_ref,
             dx_ref, db_ref, dsk_ref, xt_ref, pt_ref, qn_ref, otc_ref, otn_ref, dtc_ref, dtn_ref, gt_ref):
        b, n = pl.program_id(0), pl.program_id(1)
        xt_ref[...] = xc_ref[...].T
        pt_ref[...] = xp_ref[:, d:].T
        qn_ref[...] = xn_ref[:, :d].T
        otc_ref[...] = oc_ref[...].astype(F32).T
        otn_ref[...] = on_ref[...].astype(F32).T
        dtc_ref[...] = doc_ref[...].T
        dtn_ref[...] = don_ref[...].T
        cos_c, sin_c, cos_p, sin_p, cos_n, sin_n = (cc_ref[...], sc_ref[...], cp_ref[...], sp_ref[...],
                                                    cn_ref[...], sn_ref[...])
        cos_g, sin_g, cos_gn, sin_gn = _lanes(cos_c), _lanes(sin_c), _lanes(cos_n), _lanes(sin_n)
        valid_c, valid_p = _band_masks(n)
        kj, qi = _key_query()
        valid_n = jnp.logical_and(kj > qi, n < nb - 1)

        @pl.when(jnp.logical_and(b == 0, n == 0))
        def _():
            db_ref[...] = jnp.zeros_like(db_ref)
            dsk_ref[...] = jnp.zeros_like(dsk_ref)

        for j in range(n_kv):
            ko = j * HEAD_DIM
            kc = _rope(xt_ref[d + ko:d + ko + HEAD_DIM, :], cos_c, sin_c).astype(BF16)
            kp = _rope(pt_ref[ko:ko + HEAD_DIM, :], cos_p, sin_p).astype(BF16)
            vc = xt_ref[d + kvw + ko:d + kvw + ko + HEAD_DIM, :].astype(BF16)
            vp = pt_ref[kvw + ko:kvw + ko + HEAD_DIM, :].astype(BF16)
            q = _rope(_group(xt_ref, j), cos_g, sin_g).astype(BF16)
            do_g = _group(dtc_ref, j)
            do_b = do_g.astype(BF16)
            lse_g = _group_rows(lc_ref, j)
            delta = jnp.sum(_group(otc_ref, j) * do_g, axis=0, keepdims=True)
            p_c = jnp.exp(jnp.where(valid_c, _tn(kc, q) * scale, NEG) - lse_g)
            p_p = jnp.exp(jnp.where(valid_p, _tn(kp, q) * scale, NEG) - lse_g)
            ds_c = (p_c * (_tn(vc, do_b) - delta)).astype(BF16)
            ds_p = (p_p * (_tn(vp, do_b) - delta)).astype(BF16)
            dq = _unrope((_nn(kc, ds_c) + _nn(kp, ds_p)) * scale, cos_g, sin_g)
            dsk = -jnp.exp(_group_sinks(sink_ref, j) - lse_g) * delta
            for g in range(GROUP):
                h = j * GROUP + g
                gt_ref[h * HEAD_DIM:(h + 1) * HEAD_DIM, :] = dq[:, g * WINDOW:(g + 1) * WINDOW]
                dsk_ref[h:h + 1, :] += dsk[:, g * WINDOW:(g + 1) * WINDOW]
            q2 = _rope(_group(qn_ref, j), cos_gn, sin_gn).astype(BF16)
            do2 = _group(dtn_ref, j)
            do2_b = do2.astype(BF16)
            delta2 = jnp.sum(_group(otn_ref, j) * do2, axis=0, keepdims=True)
            p_n = jnp.exp(jnp.where(valid_n, _tn(kc, q2) * scale, NEG) - _group_rows(ln_ref, j))
            ds_n = (p_n * (_tn(vc, do2_b) - delta2)).astype(BF16)
            dv = _nt(do_b, p_c.astype(BF16)) + _nt(do2_b, p_n.astype(BF16))
            dk = _nt(q, ds_c) + _nt(q2, ds_n)
            gt_ref[d + ko:d + ko + HEAD_DIM, :] = _unrope(dk * scale, cos_c, sin_c)
            gt_ref[d + kvw + ko:d + kvw + ko + HEAD_DIM, :] = dv
        dx = gt_ref[...].T
        dx_ref[...] = dx.astype(BF16)
        db_ref[...] += _fold8(dx)

    cur = lambda b, n: (b * nb + n, 0)
    prev = lambda b, n: (b * nb + jnp.maximum(n - 1, 0), 0)
    nxt = lambda b, n: (b * nb + jnp.minimum(n + 1, nb - 1), 0)
    stat_c = pl.BlockSpec((n_heads, WINDOW), lambda b, n: (0, b * nb + n))
    stat_n = pl.BlockSpec((n_heads, WINDOW), lambda b, n: (0, b * nb + jnp.minimum(n + 1, nb - 1)))
    tab_c = pl.BlockSpec((HEAD_DIM, WINDOW), lambda b, n: (0, n))
    tab_p = pl.BlockSpec((HEAD_DIM, WINDOW), lambda b, n: (0, jnp.maximum(n - 1, 0)))
    tab_n = pl.BlockSpec((HEAD_DIM, WINDOW), lambda b, n: (0, jnp.minimum(n + 1, nb - 1)))
    return pl.pallas_call(
        body,
        name=name,
        grid=(bsz, nb),
        in_specs=[SMEM, pl.BlockSpec((WINDOW, qw), cur), pl.BlockSpec((WINDOW, qw), prev), pl.BlockSpec((WINDOW, qw), nxt),
                  pl.BlockSpec((WINDOW, d), cur), pl.BlockSpec((WINDOW, d), nxt),
                  pl.BlockSpec((WINDOW, d), cur), pl.BlockSpec((WINDOW, d), nxt),
                  stat_c, stat_n, tab_c, tab_c, tab_p, tab_p, tab_n, tab_n],
        out_specs=[pl.BlockSpec((WINDOW, qw), cur), pl.BlockSpec((8, qw), lambda b, n: (0, 0)),
                   pl.BlockSpec((n_heads, WINDOW), lambda b, n: (0, 0))],
        out_shape=[jax.ShapeDtypeStruct((t, qw), BF16), jax.ShapeDtypeStruct((8, qw), F32),
                   jax.ShapeDtypeStruct((n_heads, WINDOW), F32)],
        scratch_shapes=[pltpu.VMEM((qw, WINDOW), F32), pltpu.VMEM((2 * kvw, WINDOW), F32), pltpu.VMEM((d, WINDOW), F32),
                        pltpu.VMEM((d, WINDOW), F32), pltpu.VMEM((d, WINDOW), F32), pltpu.VMEM((d, WINDOW), F32),
                        pltpu.VMEM((d, WINDOW), F32), pltpu.VMEM((qw, WINDOW), F32)],
        compiler_params=_params(("arbitrary", "arbitrary")),
    )(sinks, qkv, qkv, qkv, o, o, do, do, lse, lse, cos_t, sin_t, cos_t, sin_t, cos_t, sin_t)


def _place():
    return lax.axis_index("x"), lax.axis_index("y"), lax.axis_index("c")


def _other_chips(x, y):
    return [(1 - x, y), (x, 1 - y), (1 - x, 1 - y)]


def _place_shard(w, axis, q, name):
    ly, k, n = w.shape
    tr = _pick(k, (256, 128, 64, 32, 16, 8))
    steps = k // tr
    shape = (ly, k * N_CHIPS, n) if axis == 1 else (ly, k, n * N_CHIPS)
    if axis == 1:
        out_spec = pl.BlockSpec((None, tr, n), lambda l, i, q_ref: (l, q_ref[0] * steps + i, 0))
    else:
        out_spec = pl.BlockSpec((None, tr, n), lambda l, i, q_ref: (l, i, q_ref[0]))

    def body(q_ref, w_ref, o_ref):
        del q_ref
        o_ref[...] = w_ref[...].astype(BF16)

    return pl.pallas_call(
        body,
        name=name,
        grid_spec=pltpu.PrefetchScalarGridSpec(
            num_scalar_prefetch=1, grid=(ly, steps),
            in_specs=[pl.BlockSpec((None, tr, n), lambda l, i, q_ref: (l, i, 0))], out_specs=out_spec),
        out_shape=jax.ShapeDtypeStruct(shape, BF16),
        compiler_params=_params(("parallel", "parallel")),
    )(q, w)


def _gather_weights(fulls, axes, name):
    n = len(fulls)

    def body(*refs):
        dst = refs[n:2 * n]
        send_sems, recv_sems = refs[2 * n:]
        x, y, c = _place()
        chips = _other_chips(x, y)

        def half(i, px, py, pc):
            ref, blk = dst[i], 2 * px + py
            if axes[i] == 1:
                rows = ref.shape[1] // (2 * N_CHIPS)
                return ref.at[:, pl.ds(pl.multiple_of((2 * blk + pc) * rows, 8), rows), :]
            rows, width = ref.shape[1] // 2, ref.shape[2] // N_CHIPS
            return ref.at[:, pl.ds(pl.multiple_of(pc * rows, 8), rows), pl.ds(pl.multiple_of(blk * width, 128), width)]

        def copy(i, m, piece, to):
            return pltpu.make_async_remote_copy(
                src_ref=half(i, *piece), dst_ref=half(i, *piece), send_sem=send_sems.at[i * 6 + m],
                recv_sem=recv_sems.at[i * 6 + m], device_id=to, device_id_type=MESH)

        sends = [copy(i, k, (x, y, c), (*chip, c)) for i in range(n) for k, chip in enumerate(chips)]
        for cp in sends:
            cp.start()
        passed = []
        for i in range(n):
            for k, chip in enumerate(chips):
                copy(i, k, (*chip, c), (*chip, c)).wait_recv()
                passed.append(copy(i, 3 + k, (*chip, c), (x, y, 1 - c)))
                passed[-1].start()
        for i in range(n):
            for k, chip in enumerate(chips):
                copy(i, 3 + k, (*chip, 1 - c), (x, y, 1 - c)).wait_recv()
        for cp in sends + passed:
            cp.wait_send()

    return pl.pallas_call(
        body,
        name=name,
        in_specs=[ANY] * n,
        out_specs=[ANY] * n,
        out_shape=[jax.ShapeDtypeStruct(f.shape, f.dtype) for f in fulls],
        input_output_aliases={i: i for i in range(n)},
        scratch_shapes=[pltpu.SemaphoreType.DMA((6 * n,)), pltpu.SemaphoreType.DMA((6 * n,))],
    )(*fulls)


def _half_shape(kind, shape):
    if kind == "col":
        return (shape[0] // 2, shape[1])
    return (N_CHIPS, shape[1] // 2, shape[2])


def _half_of(kind, ref, h):
    if kind == "col":
        r = ref.shape[0] // 2
        return ref.at[pl.ds(pl.multiple_of(h * r, 8), r), :]
    r = ref.shape[1] // 2
    return ref.at[:, pl.ds(pl.multiple_of(h * r, 8), r), :]


def _slice_of_half(kind, ref, s):
    if kind == "col":
        w = ref.shape[1] // N_CHIPS
        return ref.at[:, pl.ds(pl.multiple_of(s * w, 128), w)]
    return ref.at[s]


def _pair_exchange(grads, kinds, name):
    n = len(grads)
    outs = [jax.ShapeDtypeStruct(_half_shape(kd, g.shape), g.dtype) for g, kd in zip(grads, kinds)]

    def body(*refs):
        src, dst = refs[:n], refs[n:2 * n]
        send_sems, recv_sems = refs[2 * n:]
        x, y, c = _place()
        copies = [pltpu.make_async_remote_copy(
            src_ref=_half_of(kinds[i], src[i], 1 - c), dst_ref=dst[i],
            send_sem=send_sems.at[i], recv_sem=recv_sems.at[i], device_id=(x, y, 1 - c), device_id_type=MESH)
            for i in range(n)]
        for cp in copies:
            cp.start()
        for cp in copies:
            cp.wait_recv()
        for cp in copies:
            cp.wait_send()

    return pl.pallas_call(
        body,
        name=name,
        in_specs=[ANY] * n,
        out_specs=[ANY] * n,
        out_shape=outs,
        scratch_shapes=[pltpu.SemaphoreType.DMA((n,)), pltpu.SemaphoreType.DMA((n,))],
    )(*grads)


def _pair_sum(grad, recv, kind, c, name):
    if kind == "col":
        k, n = grad.shape
        rows = k // 2
        tr = _pick(rows, (256, 128, 64, 32, 16, 8))
        steps = rows // tr
        grid = (steps,)
        g_spec = pl.BlockSpec((tr, n), lambda i, c_ref: (c_ref[0] * steps + i, 0))
        r_spec = pl.BlockSpec((tr, n), lambda i, c_ref: (i, 0))
        g_in = grad
    else:
        _, k4, n = grad.shape
        r8 = k4 // 2
        grid = (N_CHIPS,)
        g_spec = pl.BlockSpec((None, None, r8, n), lambda s, c_ref: (s, c_ref[0], 0, 0))
        r_spec = pl.BlockSpec((None, r8, n), lambda s, c_ref: (s, 0, 0))
        g_in = grad.reshape(N_CHIPS, 2, r8, n)

    def body(c_ref, g_ref, r_ref, o_ref):
        del c_ref
        o_ref[...] = (g_ref[...].astype(F32) + r_ref[...].astype(F32)).astype(o_ref.dtype)

    return pl.pallas_call(
        body,
        name=name,
        grid_spec=pltpu.PrefetchScalarGridSpec(num_scalar_prefetch=1, grid=grid, in_specs=[g_spec, r_spec], out_specs=r_spec),
        out_shape=jax.ShapeDtypeStruct(recv.shape, recv.dtype),
        compiler_params=_params(("parallel",)),
    )(c, g_in, recv)


def _chip_scatter(sums, kinds, name):
    flat = [(i, l, s) for i, ss in enumerate(sums) for l, s in enumerate(ss)]
    n, n_t = len(flat), len(sums)
    outs = []
    for ss, kd in zip(sums, kinds):
        s = ss[0]
        shp = (s.shape[0], s.shape[1] // N_CHIPS) if kd == "col" else s.shape[1:]
        outs += [jax.ShapeDtypeStruct((len(ss), N_CHIPS) + tuple(shp), s.dtype)] * 2
    per = 7

    def body(*refs):
        src = refs[:n]
        mine = [refs[n + 2 * i] for i in range(n_t)]
        sib = [refs[n + 2 * i + 1] for i in range(n_t)]
        send_sems, recv_sems, local_sems = refs[n + 2 * n_t:]
        x, y, c = _place()
        q = 2 * x + y
        chips = _other_chips(x, y)
        sibling = (x, y, 1 - c)

        def copy(j, m, src_ref, dst_ref, to):
            return pltpu.make_async_remote_copy(
                src_ref=src_ref, dst_ref=dst_ref, send_sem=send_sems.at[j * per + m], recv_sem=recv_sems.at[j * per + m],
                device_id=to, device_id_type=MESH)

        local, sends = [], []
        for j, (i, l, _) in enumerate(flat):
            own = _slice_of_half(kinds[i], src[j], q)
            local.append(pltpu.make_async_copy(own, mine[i].at[l, q], local_sems.at[j]))
            sends.append(copy(j, 3, own, sib[i].at[l, q], sibling))
            for k, chip in enumerate(chips):
                sends.append(copy(j, k, _slice_of_half(kinds[i], src[j], 2 * chip[0] + chip[1]), mine[i].at[l, q], (*chip, c)))
        for cp in local + sends:
            cp.start()
        for j, (i, l, _) in enumerate(flat):
            for k, chip in enumerate(chips):
                slot = 2 * chip[0] + chip[1]
                copy(j, k, mine[i].at[l, slot], mine[i].at[l, slot], (*chip, c)).wait_recv()
                sends.append(copy(j, 4 + k, mine[i].at[l, slot], sib[i].at[l, slot], sibling))
                sends[-1].start()
        for j, (i, l, _) in enumerate(flat):
            copy(j, 3, sib[i].at[l, q], sib[i].at[l, q], sibling).wait_recv()
            for k, chip in enumerate(chips):
                slot = 2 * chip[0] + chip[1]
                copy(j, 4 + k, sib[i].at[l, slot], sib[i].at[l, slot], sibling).wait_recv()
        for cp in sends:
            cp.wait_send()
        for cp in local:
            cp.wait()

    return pl.pallas_call(
        body,
        name=name,
        in_specs=[ANY] * n,
        out_specs=[ANY] * (2 * n_t),
        out_shape=outs,
        scratch_shapes=[pltpu.SemaphoreType.DMA((per * n,)), pltpu.SemaphoreType.DMA((per * n,)), pltpu.SemaphoreType.DMA((n,))],
    )(*[s for _, _, s in flat])


def _reduce_adamw(mine, sib, w, m, v, c, name):
    ly, _, r, cols = mine.shape
    tr = _pick(r, (128, 64, 32, 16, 8))
    steps = r // tr
    c1 = 1.0 - ADAM_B1 ** ADAM_STEP
    c2 = 1.0 - ADAM_B2 ** ADAM_STEP

    def body(c_ref, mine_ref, sib_ref, w_ref, m_ref, v_ref, g_ref, d_ref, nm_ref, nv_ref):
        def total(ref):
            acc = ref[0].astype(F32)
            for s in range(1, N_CHIPS):
                acc = acc + ref[s].astype(F32)
            return acc

        gv = jnp.where(pl.program_id(1) == c_ref[0], total(mine_ref), total(sib_ref))
        nm = ADAM_B1 * m_ref[...] + (1.0 - ADAM_B1) * gv
        nv = ADAM_B2 * v_ref[...] + (1.0 - ADAM_B2) * (gv * gv)
        g_ref[...] = gv
        d_ref[...] = -ADAM_LR * ((nm / c1) / (jnp.sqrt(nv / c2) + ADAM_EPS) + ADAM_WD * w_ref[...])
        nm_ref[...] = nm
        nv_ref[...] = nv

    slot_spec = pl.BlockSpec((None, N_CHIPS, tr, cols), lambda l, h, i, c_ref: (l, 0, i, 0))
    spec = pl.BlockSpec((None, tr, cols), lambda l, h, i, c_ref: (l, h * steps + i, 0))
    shp = jax.ShapeDtypeStruct(w.shape, F32)
    return pl.pallas_call(
        body,
        name=name,
        grid_spec=pltpu.PrefetchScalarGridSpec(
            num_scalar_prefetch=1, grid=(ly, N_CORES, steps),
            in_specs=[slot_spec, slot_spec, spec, spec, spec], out_specs=[spec] * 4),
        out_shape=[shp] * 4,
        compiler_params=_params(("parallel", "parallel", "parallel")),
    )(c, mine, sib, w, m, v)


def _allreduce_small(v, name):
    r, w = v.shape

    def body(v_ref, o_ref, buf_ref, send_sems, recv_sems):
        x, y, c = _place()
        me = 4 * x + 2 * y + c

        def peer(k):
            return x ^ (k >> 2), y ^ ((k >> 1) & 1), c ^ (k & 1)

        def remote(k, slot):
            return pltpu.make_async_remote_copy(
                src_ref=v_ref, dst_ref=buf_ref.at[slot], send_sem=send_sems.at[k - 1], recv_sem=recv_sems.at[k - 1],
                device_id=peer(k), device_id_type=MESH)

        sends = [remote(k, me) for k in range(1, N_DEV)]
        for cp in sends:
            cp.start()
        buf_ref[me] = v_ref[...]
        for k in range(1, N_DEV):
            px, py, pc = peer(k)
            remote(k, 4 * px + 2 * py + pc).wait_recv()
        for cp in sends:
            cp.wait_send()
        acc = buf_ref[0]
        for dev in range(1, N_DEV):
            acc = acc + buf_ref[dev]
        o_ref[...] = acc

    vm = pl.BlockSpec(memory_space=pltpu.VMEM)
    return pl.pallas_call(
        body,
        name=name,
        in_specs=[vm],
        out_specs=vm,
        out_shape=jax.ShapeDtypeStruct((r, w), F32),
        scratch_shapes=[pltpu.VMEM((N_DEV, r, w), F32), pltpu.SemaphoreType.DMA((N_DEV - 1,)), pltpu.SemaphoreType.DMA((N_DEV - 1,))],
        compiler_params=pltpu.CompilerParams(vmem_limit_bytes=VMEM_LIMIT_BYTES),
    )(v)


def _adamw(w, g, m, v, name):
    ly, r, c = w.shape
    tr = _pick(r, (256, 128, 64, 32, 16, 8))
    c1 = 1.0 - ADAM_B1 ** ADAM_STEP
    c2 = 1.0 - ADAM_B2 ** ADAM_STEP

    def body(w_ref, g_ref, m_ref, v_ref, d_ref, nm_ref, nv_ref):
        gv = g_ref[...]
        nm = ADAM_B1 * m_ref[...] + (1.0 - ADAM_B1) * gv
        nv = ADAM_B2 * v_ref[...] + (1.0 - ADAM_B2) * (gv * gv)
        d_ref[...] = -ADAM_LR * ((nm / c1) / (jnp.sqrt(nv / c2) + ADAM_EPS) + ADAM_WD * w_ref[...])
        nm_ref[...] = nm
        nv_ref[...] = nv

    spec = pl.BlockSpec((None, tr, c), lambda l, i: (l, i, 0))
    shp = jax.ShapeDtypeStruct((ly, r, c), F32)
    return pl.pallas_call(
        body,
        name=name,
        grid=(ly, r // tr),
        in_specs=[spec] * 4,
        out_specs=[spec] * 3,
        out_shape=[shp] * 3,
        compiler_params=_params(("parallel", "parallel")),
    )(w, g, m, v)


def _rope_tables(seq):
    pos = jnp.arange(seq, dtype=F32)
    inv_freq = 1.0 / (ROPE_THETA ** (jnp.arange(0, HEAD_DIM, 2, dtype=F32) / HEAD_DIM))
    ang = (pos[:, None] * inv_freq[None, :]).T
    cos, sin = jnp.cos(ang), jnp.sin(ang)
    return jnp.concatenate([cos, cos], axis=0), jnp.concatenate([-sin, sin], axis=0)


def _pack(vs, fill=0.0):
    p = jnp.concatenate([v.reshape(-1) for v in vs])
    size = -(-p.shape[0] // 8192) * 8192
    return jnp.pad(p, (0, size - p.shape[0]), constant_values=fill).reshape(-1, 1024)


def _unpack(p, like):
    p = p.reshape(-1)
    out, o = [], 0
    for v in like:
        n = int(math.prod(v.shape))
        out.append(p[o:o + n].reshape(v.shape))
        o += n
    return out


def kernel(x, norm_mix, norm_ffn, norm_final, conv_w_in, conv_w_conv, conv_w_out, attn_w_qkv, attn_b_qkv, attn_sinks, attn_w_o, attn_b_o, ffn_w_in, ffn_w_conv, ffn_w_down, loss_target, m_norm_mix, m_norm_ffn, m_norm_final, m_conv_w_in, m_conv_w_conv, m_conv_w_out, m_attn_w_qkv, m_attn_b_qkv, m_attn_sinks, m_attn_w_o, m_attn_b_o, m_ffn_w_in, m_ffn_w_conv, m_ffn_w_down, v_norm_mix, v_norm_ffn, v_norm_final, v_conv_w_in, v_conv_w_conv, v_conv_w_out, v_attn_w_qkv, v_attn_b_qkv, v_attn_sinks, v_attn_w_o, v_attn_b_o, v_ffn_w_in, v_ffn_w_conv, v_ffn_w_down):
    bsz, seq, d = x.shape
    t = bsz * seq
    depth = norm_mix.shape[0]
    n_conv, n_attn = conv_w_in.shape[0], attn_w_qkv.shape[0]
    xq, yq, cq = _place()
    q = 2 * xq + yq

    big = [conv_w_in, conv_w_out, attn_w_qkv, attn_w_o, ffn_w_in, ffn_w_down]
    axes = [2, 1, 2, 1, 2, 1]
    q_arr = q.astype(jnp.int32).reshape(1)
    c_arr = cq.astype(jnp.int32).reshape(1)
    w_cin, w_cout, w_qkv, w_o, w_fin, w_fdown = _gather_weights(
        [_place_shard(w, ax, q_arr, f"place_shard{n}") for n, (w, ax) in enumerate(zip(big, axes))], axes, "gather_weights")

    small_cols = [conv_w_conv, attn_b_qkv, attn_b_o, ffn_w_conv]

    def placed(v):
        width = v.shape[-1]
        full = jnp.zeros(v.shape[:-1] + (N_CHIPS * width,), F32)
        return lax.dynamic_update_slice_in_dim(full, v * (1.0 / N_CORES), q * width, axis=v.ndim - 1)

    full_cols = [placed(v) for v in small_cols]
    wc_conv, b_qkv, b_o, wf_conv = _unpack(_allreduce_small(_pack(full_cols), "gather_small"), full_cols)
    cos_t, sin_t = _rope_tables(seq)

    xs = x.reshape(t, d)
    saved = []
    for i in range(depth):
        j = i // 2
        h = _rms_fwd(xs, norm_mix[i:i + 1], f"norm_mix_fwd{i}")
        if i % 2 == 0:
            pre = _mm(h, w_cin, "nn", BF16, layer=j, tm=1024, tn=768, tk=4096, name=f"conv_in_fwd{i}")
            mixed = _convgate_fwd(pre, wc_conv[j], seq, f"conv_gate_fwd{i}")
            x_mid = _mm(mixed, w_cout, "nn", F32, layer=j, residual=xs, tm=512, tn=1024, tk=4096, name=f"conv_out_fwd{i}")
            lse = None
        else:
            pre = _mm(h, w_qkv, "nn", F32, layer=j, bias=b_qkv[j:j + 1], tm=1024, tn=768, tk=4096, name=f"qkv_fwd{i}")
            mixed, lse = _attn_fwd(pre, attn_sinks[j], cos_t, sin_t, bsz, seq, f"attn_fwd{i}")
            x_mid = _mm(mixed, w_o, "nn", F32, layer=j, bias=b_o[j:j + 1], residual=xs, tm=512, tn=1024, tk=4096,
                        name=f"attn_out_fwd{i}")
        h2 = _rms_fwd(x_mid, norm_ffn[i:i + 1], f"norm_ffn_fwd{i}")
        gu = _mm(h2, w_fin, "nn", BF16, layer=i, n_outer=True, tm=512, tn=1408, tk=4096, name=f"ffn_in_fwd{i}")
        act = _ffngate_fwd(gu, wf_conv[i], seq, f"ffn_gate_fwd{i}")
        x_next = _mm(act, w_fdown, "nn", F32, layer=i, residual=x_mid, tm=512, tn=1024, tk=4096, name=f"ffn_down_fwd{i}")
        saved.append((xs, h, pre, mixed, lse, x_mid, h2, gu, act))
        xs = x_next

    dx, dxb, sq, dg_final = _loss_head(xs, loss_target.reshape(t, d), norm_final.reshape(1, d), "loss_head")
    loss = lax.psum(0.5 * jnp.sum(sq) / d, ("x", "y", "c"))

    g_norm_mix, g_norm_ffn = [None] * depth, [None] * depth
    g_cin, g_cconv, g_cout = [None] * n_conv, [None] * n_conv, [None] * n_conv
    g_qkv, g_bqkv, g_sinks, g_o, g_bo = ([None] * n_attn for _ in range(5))
    g_fin, g_fconv, g_fdown = [None] * depth, [None] * depth, [None] * depth
    for i in reversed(range(depth)):
        j = i // 2
        x_in, h, pre, mixed, lse, x_mid, h2, gu, act = saved[i]
        da = _mm(dxb, w_fdown, "nt", BF16, layer=i, n_outer=True, tm=512, tn=1408, tk=4096, name=f"ffn_down_dx{i}")
        g_fdown[i] = _mm(act, dxb, "tn", BF16, tm=1408, tn=1024, tk=2048, name=f"ffn_down_dw{i}")
        dgu, dwc = _ffngate_bwd(gu, da, wf_conv[i], seq, f"ffn_gate_bwd{i}")
        g_fconv[i] = dwc[:3]
        g_fin[i] = _mm(h2, dgu, "tn", BF16, tm=1024, tn=1408, tk=2048, name=f"ffn_in_dw{i}")
        dh2 = _mm(dgu, w_fin, "nt", F32, layer=i, tm=512, tn=1024, tk=8192, name=f"ffn_in_dx{i}")
        dx, dxb, dg, colsum = _rms_bwd(x_mid, dh2, norm_ffn[i:i + 1], dx, f"norm_ffn_bwd{i}")
        g_norm_ffn[i] = jnp.sum(dg, axis=0)
        if i % 2 == 0:
            dmix = _mm(dxb, w_cout, "nt", BF16, layer=j, tm=512, tn=1024, tk=4096, name=f"conv_out_dx{i}")
            g_cout[j] = _mm(mixed, dxb, "tn", BF16, tm=1024, tn=1024, tk=2048, name=f"conv_out_dw{i}")
            dpre, dwc = _convgate_bwd(pre, dmix, wc_conv[j], seq, f"conv_gate_bwd{i}")
            g_cconv[j] = dwc[:3]
            g_cin[j] = _mm(h, dpre, "tn", BF16, tm=1024, tn=1536, tk=2048, name=f"conv_in_dw{i}")
            dh = _mm(dpre, w_cin, "nt", F32, layer=j, tm=512, tn=1024, tk=8192, name=f"conv_in_dx{i}")
        else:
            g_bo[j] = jnp.sum(colsum, axis=0)
            dmix = _mm(dxb, w_o, "nt", F32, layer=j, tm=512, tn=1024, tk=4096, name=f"attn_out_dx{i}")
            g_o[j] = _mm(mixed, dxb, "tn", BF16, tm=1024, tn=1024, tk=2048, name=f"attn_out_dw{i}")
            dpre, dbias, dsk = _attn_bwd(pre, mixed, lse, dmix, attn_sinks[j], cos_t, sin_t, bsz, seq, f"attn_bwd{i}")
            g_bqkv[j] = jnp.sum(dbias, axis=0)
            g_sinks[j] = jnp.sum(dsk, axis=1)
            g_qkv[j] = _mm(h, dpre, "tn", BF16, tm=1024, tn=1536, tk=2048, name=f"qkv_dw{i}")
            dh = _mm(dpre, w_qkv, "nt", F32, layer=j, tm=512, tn=1024, tk=1536, name=f"qkv_dx{i}")
        dx, dxb, dg, _ = _rms_bwd(x_in, dh, norm_mix[i:i + 1], dx, f"norm_mix_bwd{i}")
        g_norm_mix[i] = jnp.sum(dg, axis=0)
    grad_x = dx.reshape(bsz, seq, d)

    tensors = [(g_cin, "col"), (g_cout, "row"), (g_qkv, "col"), (g_o, "row"), (g_fin, "col"), (g_fdown, "row")]
    t_kinds = [kd for _, kd in tensors]
    flat, kinds = [], []
    for gs, kd in tensors:
        for g in gs:
            flat.append(g if kd == "col" else g.reshape(N_CHIPS, g.shape[0] // N_CHIPS, g.shape[1]))
            kinds.append(kd)
    recv = _pair_exchange(flat, kinds, "grad_pair_exchange")
    sums = [_pair_sum(g, r, kd, c_arr, f"grad_pair_sum{n}") for n, (g, r, kd) in enumerate(zip(flat, recv, kinds))]
    by_tensor, pos = [], 0
    for gs, _ in tensors:
        by_tensor.append(sums[pos:pos + len(gs)])
        pos += len(gs)
    slots = _chip_scatter(by_tensor, t_kinds, "grad_chip_scatter")
    big_w = [conv_w_in, conv_w_out, attn_w_qkv, attn_w_o, ffn_w_in, ffn_w_down]
    big_m = [m_conv_w_in, m_conv_w_out, m_attn_w_qkv, m_attn_w_o, m_ffn_w_in, m_ffn_w_down]
    big_v = [v_conv_w_in, v_conv_w_out, v_attn_w_qkv, v_attn_w_o, v_ffn_w_in, v_ffn_w_down]
    big_names = ["conv_w_in", "conv_w_out", "attn_w_qkv", "attn_w_o", "ffn_w_in", "ffn_w_down"]
    big_upd = [_reduce_adamw(slots[2 * n], slots[2 * n + 1], big_w[n], big_m[n], big_v[n], c_arr, f"adamw_{nm}")
               for n, nm in enumerate(big_names)]

    small = [jnp.stack(g_norm_mix), jnp.stack(g_norm_ffn), jnp.sum(dg_final, axis=0), jnp.stack(g_cconv),
             jnp.stack(g_bqkv), jnp.stack(g_sinks), jnp.stack(g_bo), jnp.stack(g_fconv)]
    sg = _unpack(_allreduce_small(_pack(small), "grad_small_allreduce"), small)

    def my_cols(v, like):
        width = like.shape[-1]
        return lax.dynamic_slice_in_dim(v, q * width, width, axis=v.ndim - 1)

    small_w = [norm_mix, norm_ffn, norm_final, conv_w_conv, attn_b_qkv, attn_sinks, attn_b_o, ffn_w_conv]
    small_m = [m_norm_mix, m_norm_ffn, m_norm_final, m_conv_w_conv, m_attn_b_qkv, m_attn_sinks, m_attn_b_o, m_ffn_w_conv]
    small_v = [v_norm_mix, v_norm_ffn, v_norm_final, v_conv_w_conv, v_attn_b_qkv, v_attn_sinks, v_attn_b_o, v_ffn_w_conv]
    small_g = [sg[0], sg[1], sg[2], my_cols(sg[3], conv_w_conv), my_cols(sg[4], attn_b_qkv), sg[5],
               my_cols(sg[6], attn_b_o), my_cols(sg[7], ffn_w_conv)]

    upd = {nm: tuple(u[1:]) for nm, u in zip(big_names, big_upd)}
    sd, sm, sv = _adamw(_pack(small_w)[None], _pack(small_g)[None], _pack(small_m)[None], _pack(small_v, 1.0)[None],
                        "adamw_small")
    sd, sm, sv = _unpack(sd, small_w), _unpack(sm, small_w), _unpack(sv, small_w)
    names = ["norm_mix", "norm_ffn", "norm_final", "conv_w_in", "conv_w_conv", "conv_w_out", "attn_w_qkv", "attn_b_qkv",
             "attn_sinks", "attn_w_o", "attn_b_o", "ffn_w_in", "ffn_w_conv", "ffn_w_down"]
    small_names = ["norm_mix", "norm_ffn", "norm_final", "conv_w_conv", "attn_b_qkv", "attn_sinks", "attn_b_o", "ffn_w_conv"]
    grads = dict(zip(small_names, small_g))
    grads.update({nm: u[0] for nm, u in zip(big_names, big_upd)})
    for n, nm in enumerate(small_names):
        upd[nm] = (sd[n], sm[n], sv[n])
    return (loss, grad_x, *[grads[nm] for nm in names], *[upd[nm][0] for nm in names],
            *[upd[nm][1] for nm in names], *[upd[nm][2] for nm in names])
```

```python
import math

import jax
import jax.numpy as jnp
from jax import lax
from jax.experimental import pallas as pl
from jax.experimental.pallas import tpu as pltpu

F32 = jnp.float32
BF16 = jnp.bfloat16

HEAD_DIM = 64
GROUP = 4
WINDOW = 128
EPS = 1e-5
ROPE_THETA = 10000.0
ADAM_LR, ADAM_B1, ADAM_B2, ADAM_EPS, ADAM_WD, ADAM_STEP = 0.001, 0.9, 0.999, 1e-08, 0.01, 10

N_CHIPS = 4
N_CORES = 2
N_DEV = 8
HALO = 16
VMEM_LIMIT_BYTES = 56 * 1024 * 1024
MESH = pl.DeviceIdType.MESH
ANY = pl.BlockSpec(memory_space=pl.ANY)
SMEM = pl.BlockSpec(memory_space=pltpu.SMEM)
NEG = float(jnp.finfo(jnp.float32).min)
ROW_TILES = (512, 256, 128, 64, 32, 16, 8)


def _pick(dim, cands):
    for c in cands:
        if dim % c == 0:
            return c
    return dim


def _params(sem):
    return pltpu.CompilerParams(dimension_semantics=sem, vmem_limit_bytes=VMEM_LIMIT_BYTES)


_DIMS = {"nn": (((1,), (0,)), ((), ())), "nt": (((1,), (1,)), ((), ())), "tn": (((0,), (0,)), ((), ()))}


def _mm(a, b, mode, out_dtype, *, layer=None, bias=None, residual=None, n_outer=False, tm, tn, tk, name):
    b2 = b.shape[1:] if layer is not None else b.shape
    if mode == "nn":
        (m, k), n = a.shape, b2[1]
    elif mode == "nt":
        (m, k), n = a.shape, b2[0]
    else:
        (k, m), n = a.shape, b2[1]
    tm, tn, tk = min(tm, m), min(tn, n), min(tk, k)
    assert m % tm == 0 and n % tn == 0 and k % tk == 0, (name, a.shape, b.shape, tm, tn, tk)
    nk = k // tk

    def at(f):
        return (lambda p0, p1, p2: f(p1, p0, p2)) if n_outer else f

    a_spec = pl.BlockSpec((tk, tm), at(lambda i, j, l: (l, i))) if mode == "tn" else pl.BlockSpec((tm, tk), at(lambda i, j, l: (i, l)))
    if layer is None:
        b_spec = (pl.BlockSpec((tn, tk), at(lambda i, j, l: (j, l))) if mode == "nt"
                  else pl.BlockSpec((tk, tn), at(lambda i, j, l: (l, j))))
    elif mode == "nt":
        b_spec = pl.BlockSpec((None, tn, tk), at(lambda i, j, l: (layer, j, l)))
    else:
        b_spec = pl.BlockSpec((None, tk, tn), at(lambda i, j, l: (layer, l, j)))
    in_specs, args = [a_spec, b_spec], [a, b]
    if bias is not None:
        in_specs.append(pl.BlockSpec((1, tn), at(lambda i, j, l: (0, j))))
        args.append(bias)
    if residual is not None:
        in_specs.append(pl.BlockSpec((tm, tn), at(lambda i, j, l: (i, j))))
        args.append(residual)
    has_bias, has_res = bias is not None, residual is not None

    def body(*refs):
        a_ref, b_ref = refs[0], refs[1]
        pos = 2
        bias_ref = res_ref = None
        if has_bias:
            bias_ref, pos = refs[pos], pos + 1
        if has_res:
            res_ref, pos = refs[pos], pos + 1
        o_ref = refs[pos]
        acc_ref = refs[pos + 1] if nk > 1 else None

        def finish(acc):
            if has_bias:
                acc = acc + bias_ref[...]
            if has_res:
                acc = acc + res_ref[...]
            o_ref[...] = acc.astype(o_ref.dtype)

        if nk == 1:
            finish(lax.dot_general(a_ref[...], b_ref[...], _DIMS[mode], preferred_element_type=F32))
            return
        l = pl.program_id(2)
        part = lax.dot_general(a_ref[...], b_ref[...], _DIMS[mode], preferred_element_type=F32)

        @pl.when(l == 0)
        def _():
            acc_ref[...] = part

        @pl.when(l > 0)
        def _():
            acc_ref[...] += part

        @pl.when(l == nk - 1)
        def _():
            finish(acc_ref[...])

    return pl.pallas_call(
        body,
        name=name,
        grid=(n // tn, m // tm, nk) if n_outer else (m // tm, n // tn, nk),
        in_specs=in_specs,
        out_specs=pl.BlockSpec((tm, tn), at(lambda i, j, l: (i, j))),
        out_shape=jax.ShapeDtypeStruct((m, n), out_dtype),
        scratch_shapes=[pltpu.VMEM((tm, tn), F32)] if nk > 1 else [],
        compiler_params=_params(("parallel", "parallel", "arbitrary")),
    )(*args)


def _fold8(v):
    r, d = v.shape
    return jnp.sum(v.reshape(r // 8, 8, d), axis=0)


def _rms_fwd(x, g, name):
    t, d = x.shape
    tm = _pick(t, ROW_TILES)

    def body(x_ref, g_ref, h_ref):
        xv = x_ref[...]
        r = lax.rsqrt(jnp.mean(xv * xv, axis=-1, keepdims=True) + EPS)
        h_ref[...] = (xv * r * g_ref[...]).astype(BF16)

    return pl.pallas_call(
        body,
        name=name,
        grid=(t // tm,),
        in_specs=[pl.BlockSpec((tm, d), lambda i: (i, 0)), pl.BlockSpec((1, d), lambda i: (0, 0))],
        out_specs=pl.BlockSpec((tm, d), lambda i: (i, 0)),
        out_shape=jax.ShapeDtypeStruct((t, d), BF16),
        compiler_params=_params(("parallel",)),
    )(x, g)


def _rms_bwd(x, dh, g, dx_in, name):
    t, d = x.shape
    tm = _pick(t, ROW_TILES)

    def body(x_ref, dh_ref, g_ref, dxi_ref, dx_ref, dxb_ref, dg_ref, cs_ref):
        i = pl.program_id(0)
        xv = x_ref[...]
        r = lax.rsqrt(jnp.mean(xv * xv, axis=-1, keepdims=True) + EPS)
        xhat = xv * r
        dy = dh_ref[...]
        gdy = dy * g_ref[...]
        dx = dxi_ref[...] + r * (gdy - xhat * jnp.mean(gdy * xhat, axis=-1, keepdims=True))
        dx_ref[...] = dx
        dxb_ref[...] = dx.astype(BF16)

        @pl.when(i == 0)
        def _():
            dg_ref[...] = jnp.zeros_like(dg_ref)
            cs_ref[...] = jnp.zeros_like(cs_ref)

        dg_ref[...] += _fold8(dy * xhat)
        cs_ref[...] += _fold8(dx)

    row = pl.BlockSpec((tm, d), lambda i: (i, 0))
    acc = pl.BlockSpec((8, d), lambda i: (0, 0))
    return pl.pallas_call(
        body,
        name=name,
        grid=(t // tm,),
        in_specs=[row, row, pl.BlockSpec((1, d), lambda i: (0, 0)), row],
        out_specs=[row, row, acc, acc],
        out_shape=[jax.ShapeDtypeStruct((t, d), F32), jax.ShapeDtypeStruct((t, d), BF16),
                   jax.ShapeDtypeStruct((8, d), F32), jax.ShapeDtypeStruct((8, d), F32)],
        compiler_params=_params(("arbitrary",)),
    )(x, dh, g, dx_in)


def _loss_head(x, target, g, name):
    t, d = x.shape
    tm = _pick(t, ROW_TILES)
    inv_d = 1.0 / d

    def body(x_ref, t_ref, g_ref, dx_ref, dxb_ref, sq_ref, dg_ref):
        i = pl.program_id(0)
        xv = x_ref[...]
        gv = g_ref[...]
        r = lax.rsqrt(jnp.mean(xv * xv, axis=-1, keepdims=True) + EPS)
        xhat = xv * r
        err = xhat * gv - t_ref[...]
        dy = err * inv_d
        gdy = dy * gv
        dx = r * (gdy - xhat * jnp.mean(gdy * xhat, axis=-1, keepdims=True))
        dx_ref[...] = dx
        dxb_ref[...] = dx.astype(BF16)

        @pl.when(i == 0)
        def _():
            sq_ref[...] = jnp.zeros_like(sq_ref)
            dg_ref[...] = jnp.zeros_like(dg_ref)

        sq_ref[...] += _fold8(err * err)
        dg_ref[...] += _fold8(dy * xhat)

    row = pl.BlockSpec((tm, d), lambda i: (i, 0))
    acc = pl.BlockSpec((8, d), lambda i: (0, 0))
    return pl.pallas_call(
        body,
        name=name,
        grid=(t // tm,),
        in_specs=[row, row, pl.BlockSpec((1, d), lambda i: (0, 0))],
        out_specs=[row, row, acc, acc],
        out_shape=[jax.ShapeDtypeStruct((t, d), F32), jax.ShapeDtypeStruct((t, d), BF16),
                   jax.ShapeDtypeStruct((8, d), F32), jax.ShapeDtypeStruct((8, d), F32)],
        compiler_params=_params(("arbitrary",)),
    )(x, target, g)


def _rows(tm):
    return lax.broadcasted_iota(jnp.int32, (tm, 1), 0)


def _shift_down(u, before2):
    r8 = _rows(8)
    s1, s2 = pltpu.roll(u, 1, 0), pltpu.roll(u, 2, 0)
    top1 = jnp.where(r8 == 0, before2[1:2], s1[:8])
    top2 = jnp.where(r8 == 0, before2[0:1], jnp.where(r8 == 1, before2[1:2], s2[:8]))
    return jnp.concatenate([top1, s1[8:]], axis=0), jnp.concatenate([top2, s2[8:]], axis=0)


def _shift_up(u, after2):
    tm = u.shape[0]
    r8 = _rows(8)
    s1, s2 = pltpu.roll(u, tm - 1, 0), pltpu.roll(u, tm - 2, 0)
    bot1 = jnp.where(r8 == 7, after2[0:1], s1[tm - 8:])
    bot2 = jnp.where(r8 == 6, after2[0:1], jnp.where(r8 == 7, after2[1:2], s2[tm - 8:]))
    return jnp.concatenate([s1[:tm - 8], bot1], axis=0), jnp.concatenate([s2[:tm - 8], bot2], axis=0)


def _shift_matrix(tm, up):
    r = lax.broadcasted_iota(jnp.int32, (2 * tm, tm), 0)
    c = lax.broadcasted_iota(jnp.int32, (2 * tm, tm), 1)
    t = jnp.where(r >= tm, r - tm, r)
    k = jnp.where(r >= tm, 2, 1)
    return (c == (t + k if up else t - k)).astype(BF16)


def _shift_down_mxu(u, before2):
    tm = u.shape[0]
    moved = jnp.dot(_shift_matrix(tm, False), u.astype(BF16), preferred_element_type=F32)
    r8 = _rows(8)
    s1, s2 = moved[:tm], moved[tm:]
    top1 = s1[:8] + jnp.where(r8 == 0, before2[1:2], 0.0)
    top2 = s2[:8] + jnp.where(r8 == 0, before2[0:1], jnp.where(r8 == 1, before2[1:2], 0.0))
    return jnp.concatenate([top1, s1[8:]], axis=0), jnp.concatenate([top2, s2[8:]], axis=0)


def _shift_up_mxu(u, after2):
    tm = u.shape[0]
    moved = jnp.dot(_shift_matrix(tm, True), u.astype(BF16), preferred_element_type=F32)
    r8 = _rows(8)
    s1, s2 = moved[:tm], moved[tm:]
    bot1 = s1[tm - 8:] + jnp.where(r8 == 7, after2[0:1], 0.0)
    bot2 = s2[tm - 8:] + jnp.where(r8 == 6, after2[0:1], jnp.where(r8 == 7, after2[1:2], 0.0))
    return jnp.concatenate([s1[:tm - 8], bot1], axis=0), jnp.concatenate([s2[:tm - 8], bot2], axis=0)


def _conv_tile(seq):
    return _pick(seq, (256, 128, 64, 32, 16, 8))


def _halo_specs(tm, width, n_tiles):
    per = tm // HALO
    before = pl.BlockSpec((HALO, width), lambda i: (jnp.maximum(i * per - 1, 0), 0))
    after = pl.BlockSpec((HALO, width), lambda i: (jnp.minimum((i + 1) * per, n_tiles * per - 1), 0))
    return before, after


def _convgate_fwd(bcv, w, seq, name):
    t, d3 = bcv.shape
    d = d3 // 3
    tm = _conv_tile(seq)
    tps = seq // tm
    before, _ = _halo_specs(tm, d3, t // tm)

    def body(x_ref, xb_ref, w_ref, y_ref):
        i = pl.program_id(0)
        inner = (i % tps != 0).astype(F32)
        u = x_ref[:, d:2 * d].astype(F32) * x_ref[:, 2 * d:].astype(F32)
        xb = xb_ref[:, d:].astype(F32)[HALO - 2:]
        s1, s2 = _shift_down(u, xb[:, :d] * xb[:, d:] * inner)
        z = w_ref[2:3] * u + w_ref[1:2] * s1 + w_ref[0:1] * s2
        y_ref[...] = (x_ref[:, :d].astype(F32) * z).astype(BF16)

    return pl.pallas_call(
        body,
        name=name,
        grid=(t // tm,),
        in_specs=[pl.BlockSpec((tm, d3), lambda i: (i, 0)), before, pl.BlockSpec((3, d), lambda i: (0, 0))],
        out_specs=pl.BlockSpec((tm, d), lambda i: (i, 0)),
        out_shape=jax.ShapeDtypeStruct((t, d), BF16),
        compiler_params=_params(("parallel",)),
    )(bcv, bcv, w)


def _convgate_bwd(bcv, dy, w, seq, name):
    t, d3 = bcv.shape
    d = d3 // 3
    tm = _conv_tile(seq)
    tps = seq // tm
    before, after = _halo_specs(tm, d3, t // tm)
    _, after_dy = _halo_specs(tm, d, t // tm)

    def body(x_ref, xb_ref, xa_ref, dy_ref, dya_ref, w_ref, dx_ref, dw_ref):
        i = pl.program_id(0)
        inner_lo = (i % tps != 0).astype(F32)
        inner_hi = (i % tps != tps - 1).astype(F32)
        w0, w1, w2 = w_ref[0:1], w_ref[1:2], w_ref[2:3]
        b, c, v = x_ref[:, :d].astype(F32), x_ref[:, d:2 * d].astype(F32), x_ref[:, 2 * d:].astype(F32)
        u = c * v
        xb = xb_ref[:, d:].astype(F32)[HALO - 2:]
        s1, s2 = _shift_down(u, xb[:, :d] * xb[:, d:] * inner_lo)
        z = w2 * u + w1 * s1 + w0 * s2
        dyv = dy_ref[...].astype(F32)
        dz = dyv * b
        dza = dya_ref[...].astype(F32)[0:2] * xa_ref[:, :d].astype(F32)[0:2] * inner_hi
        n1, n2 = _shift_up(dz, dza)
        du = w2 * dz + w1 * n1 + w0 * n2
        dx_ref[:, :d] = (dyv * z).astype(BF16)
        dx_ref[:, d:2 * d] = (du * v).astype(BF16)
        dx_ref[:, 2 * d:] = (du * c).astype(BF16)

        @pl.when(i == 0)
        def _():
            dw_ref[...] = jnp.zeros_like(dw_ref)

        dw_ref[0:1] += jnp.sum(dz * s2, axis=0, keepdims=True)
        dw_ref[1:2] += jnp.sum(dz * s1, axis=0, keepdims=True)
        dw_ref[2:3] += jnp.sum(dz * u, axis=0, keepdims=True)

    return pl.pallas_call(
        body,
        name=name,
        grid=(t // tm,),
        in_specs=[pl.BlockSpec((tm, d3), lambda i: (i, 0)), before, after,
                  pl.BlockSpec((tm, d), lambda i: (i, 0)), after_dy, pl.BlockSpec((3, d), lambda i: (0, 0))],
        out_specs=[pl.BlockSpec((tm, d3), lambda i: (i, 0)), pl.BlockSpec((8, d), lambda i: (0, 0))],
        out_shape=[jax.ShapeDtypeStruct((t, d3), BF16), jax.ShapeDtypeStruct((8, d), F32)],
        compiler_params=_params(("arbitrary",)),
    )(bcv, bcv, bcv, dy, dy, w)


def _sigmoid(x):
    return 1.0 / (1.0 + jnp.exp(-x))


def _ffngate_fwd(gu, w, seq, name):
    t, f2 = gu.shape
    f = f2 // 2
    tm = _conv_tile(seq)
    tps = seq // tm
    before, _ = _halo_specs(tm, f2, t // tm)

    def body(x_ref, xb_ref, w_ref, a_ref):
        i = pl.program_id(0)
        inner = (i % tps != 0).astype(F32)
        s1, s2 = _shift_down_mxu(x_ref[:, :f], xb_ref[:, :f].astype(F32)[HALO - 2:] * inner)
        gc = w_ref[2:3] * x_ref[:, :f].astype(F32) + w_ref[1:2] * s1 + w_ref[0:1] * s2
        a_ref[...] = (gc * _sigmoid(gc) * x_ref[:, f:].astype(F32)).astype(BF16)

    return pl.pallas_call(
        body,
        name=name,
        grid=(t // tm,),
        in_specs=[pl.BlockSpec((tm, f2), lambda i: (i, 0)), before, pl.BlockSpec((3, f), lambda i: (0, 0))],
        out_specs=pl.BlockSpec((tm, f), lambda i: (i, 0)),
        out_shape=jax.ShapeDtypeStruct((t, f), BF16),
        compiler_params=_params(("parallel",)),
    )(gu, gu, w)


def _ffngate_bwd(gu, da, w, seq, name):
    t, f2 = gu.shape
    f = f2 // 2
    tm = _conv_tile(seq)
    tps = seq // tm
    before, after = _halo_specs(tm, f2, t // tm)
    _, after_da = _halo_specs(tm, f, t // tm)

    def body(x_ref, xb_ref, xa_ref, da_ref, daa_ref, w_ref, dx_ref, dw_ref):
        i = pl.program_id(0)
        inner_lo = (i % tps != 0).astype(F32)
        inner_hi = (i % tps != tps - 1).astype(F32)
        w0, w1, w2 = w_ref[0:1], w_ref[1:2], w_ref[2:3]

        def dgate(gc, uv, dav):
            sg = _sigmoid(gc)
            return dav * uv * (sg * (1.0 + gc * (1.0 - sg))), dav * (gc * sg)

        g, u = x_ref[:, :f].astype(F32), x_ref[:, f:].astype(F32)
        s1, s2 = _shift_down_mxu(x_ref[:, :f], xb_ref[:, :f].astype(F32)[HALO - 2:] * inner_lo)
        gc = w2 * g + w1 * s1 + w0 * s2
        dgc, du = dgate(gc, u, da_ref[...].astype(F32))
        ga = xa_ref[:, :f].astype(F32)
        a1, a2 = _shift_down(ga, x_ref[tm - HALO:, :f].astype(F32)[HALO - 2:])
        gca = w2 * ga + w1 * a1 + w0 * a2
        dgca, _ = dgate(gca, xa_ref[:, f:].astype(F32), daa_ref[...].astype(F32))
        n1, n2 = _shift_up_mxu(dgc, dgca[0:2] * inner_hi)
        dx_ref[:, :f] = (w2 * dgc + w1 * n1 + w0 * n2).astype(BF16)
        dx_ref[:, f:] = du.astype(BF16)

        @pl.when(i == 0)
        def _():
            dw_ref[...] = jnp.zeros_like(dw_ref)

        dw_ref[0:1] += jnp.sum(dgc * s2, axis=0, keepdims=True)
        dw_ref[1:2] += jnp.sum(dgc * s1, axis=0, keepdims=True)
        dw_ref[2:3] += jnp.sum(dgc * g, axis=0, keepdims=True)

    return pl.pallas_call(
        body,
        name=name,
        grid=(t // tm,),
        in_specs=[pl.BlockSpec((tm, f2), lambda i: (i, 0)), before, after,
                  pl.BlockSpec((tm, f), lambda i: (i, 0)), after_da, pl.BlockSpec((3, f), lambda i: (0, 0))],
        out_specs=[pl.BlockSpec((tm, f2), lambda i: (i, 0)), pl.BlockSpec((8, f), lambda i: (0, 0))],
        out_shape=[jax.ShapeDtypeStruct((t, f2), BF16), jax.ShapeDtypeStruct((8, f), F32)],
        compiler_params=_params(("arbitrary",)),
    )(gu, gu, gu, da, da, w)


def _swap_halves(xt):
    half = HEAD_DIM // 2
    return jnp.concatenate([xt[half:], xt[:half]], axis=0)


def _rope(xt, cos, sin):
    return xt * cos + _swap_halves(xt) * sin


def _unrope(dxt, cos, sin):
    return dxt * cos - _swap_halves(dxt) * sin


def _key_query(count):
    kj = lax.broadcasted_iota(jnp.int32, (WINDOW, count * WINDOW), 0)
    qi = lax.broadcasted_iota(jnp.int32, (WINDOW, count * WINDOW), 1) & (WINDOW - 1)
    return kj, qi


def _band_masks(n, count):
    kj, qi = _key_query(count)
    return kj <= qi, jnp.logical_and(kj > qi, n > 0)


def _lanes(v, count):
    return jnp.concatenate([v] * count, axis=1) if count > 1 else v


def _heads(ref, h0, count):
    parts = [ref[(h0 + g) * HEAD_DIM:(h0 + g + 1) * HEAD_DIM, :] for g in range(count)]
    return jnp.concatenate(parts, axis=1) if count > 1 else parts[0]


def _head_rows(ref, h0, count):
    parts = [ref[h0 + g:h0 + g + 1, :] for g in range(count)]
    return jnp.concatenate(parts, axis=1) if count > 1 else parts[0]


def _head_sinks(sink_ref, h0, count):
    parts = [jnp.full((1, WINDOW), sink_ref[h0 + g], F32) for g in range(count)]
    return jnp.concatenate(parts, axis=1) if count > 1 else parts[0]


def _tn(a, b):
    return lax.dot_general(a, b, _DIMS["tn"], preferred_element_type=F32)


def _nt(a, b):
    return lax.dot_general(a, b, _DIMS["nt"], preferred_element_type=F32)


def _nn(a, b):
    return jnp.dot(a, b, preferred_element_type=F32)


def _attn_fwd(qkv, sinks, cos_t, sin_t, bsz, seq, name, hp):
    t, qw = qkv.shape
    d = qw * 2 // 3
    kvw = d // GROUP
    n_heads, n_kv = d // HEAD_DIM, kvw // HEAD_DIM
    nb = seq // WINDOW
    scale = HEAD_DIM ** -0.5

    def body(sink_ref, xc_ref, xp_ref, cc_ref, sc_ref, cp_ref, sp_ref, o_ref, lse_ref, xt_ref, pt_ref, ot_ref):
        n = pl.program_id(1)
        xt_ref[...] = xc_ref[...].T
        pt_ref[...] = xp_ref[:, d:].T
        cos_c, sin_c, cos_p, sin_p = cc_ref[...], sc_ref[...], cp_ref[...], sp_ref[...]
        cos_g, sin_g = _lanes(cos_c, hp), _lanes(sin_c, hp)
        valid_c, valid_p = _band_masks(n, hp)
        for j in range(n_kv):
            ko = j * HEAD_DIM
            kc = _rope(xt_ref[d + ko:d + ko + HEAD_DIM, :], cos_c, sin_c).astype(BF16)
            kp = _rope(pt_ref[ko:ko + HEAD_DIM, :], cos_p, sin_p).astype(BF16)
            vc = xt_ref[d + kvw + ko:d + kvw + ko + HEAD_DIM, :].astype(BF16)
            vp = pt_ref[kvw + ko:kvw + ko + HEAD_DIM, :].astype(BF16)
            for h0 in range(j * GROUP, (j + 1) * GROUP, hp):
                q = _rope(_heads(xt_ref, h0, hp), cos_g, sin_g).astype(BF16)
                sink = _head_sinks(sink_ref, h0, hp)
                s_c = jnp.where(valid_c, _tn(kc, q) * scale, NEG)
                s_p = jnp.where(valid_p, _tn(kp, q) * scale, NEG)
                m = jnp.maximum(jnp.maximum(jnp.max(s_c, axis=0, keepdims=True), jnp.max(s_p, axis=0, keepdims=True)), sink)
                p_c = jnp.exp(s_c - m)
                p_p = jnp.exp(s_p - m)
                den = jnp.sum(p_c, axis=0, keepdims=True) + jnp.sum(p_p, axis=0, keepdims=True) + jnp.exp(sink - m)
                inv = 1.0 / den
                o_g = _nn(vc, (p_c * inv).astype(BF16)) + _nn(vp, (p_p * inv).astype(BF16))
                lse_g = m + jnp.log(den)
                for g in range(hp):
                    h = h0 + g
                    ot_ref[h * HEAD_DIM:(h + 1) * HEAD_DIM, :] = o_g[:, g * WINDOW:(g + 1) * WINDOW]
                    lse_ref[h:h + 1, :] = lse_g[:, g * WINDOW:(g + 1) * WINDOW]
        o_ref[...] = ot_ref[...].T.astype(BF16)

    cur = lambda b, n: (b * nb + n, 0)
    prev = lambda b, n: (b * nb + jnp.maximum(n - 1, 0), 0)
    tab_c = pl.BlockSpec((HEAD_DIM, WINDOW), lambda b, n: (0, n))
    tab_p = pl.BlockSpec((HEAD_DIM, WINDOW), lambda b, n: (0, jnp.maximum(n - 1, 0)))
    return pl.pallas_call(
        body,
        name=name,
        grid=(bsz, nb),
        in_specs=[SMEM, pl.BlockSpec((WINDOW, qw), cur), pl.BlockSpec((WINDOW, qw), prev), tab_c, tab_c, tab_p, tab_p],
        out_specs=[pl.BlockSpec((WINDOW, d), cur), pl.BlockSpec((n_heads, WINDOW), lambda b, n: (0, b * nb + n))],
        out_shape=[jax.ShapeDtypeStruct((t, d), BF16), jax.ShapeDtypeStruct((n_heads, t), F32)],
        scratch_shapes=[pltpu.VMEM((qw, WINDOW), F32), pltpu.VMEM((2 * kvw, WINDOW), F32), pltpu.VMEM((d, WINDOW), F32)],
        compiler_params=_params(("parallel", "arbitrary")),
    )(sinks, qkv, qkv, cos_t, sin_t, cos_t, sin_t)


def _attn_bwd(qkv, o, lse, do, sinks, cos_t, sin_t, bsz, seq, name, hp):
    t, qw = qkv.shape
    d = qw * 2 // 3
    kvw = d // GROUP
    n_heads, n_kv = d // HEAD_DIM, kvw // HEAD_DIM
    nb = seq // WINDOW
    scale = HEAD_DIM ** -0.5

    def body(sink_ref, xc_ref, xp_ref, xn_ref, oc_ref, on_ref, doc_ref, don_ref, lc_ref, ln_ref,
             cc_ref, sc_ref, cp_ref, sp_ref, cn_ref, sn_ref,
             dx_ref, db_ref, dsk_ref, xt_ref, pt_ref, qn_ref, otc_ref, otn_ref, dtc_ref, dtn_ref, gt_ref):
        b, n = pl.program_id(0), pl.program_id(1)
        xt_ref[...] = xc_ref[...].T
        pt_ref[...] = xp_ref[:, d:].T
        qn_ref[...] = xn_ref[:, :d].T
        otc_ref[...] = oc_ref[...].astype(F32).T
        otn_ref[...] = on_ref[...].astype(F32).T
        dtc_ref[...] = doc_ref[...].T
        dtn_ref[...] = don_ref[...].T
        cos_c, sin_c, cos_p, sin_p, cos_n, sin_n = (cc_ref[...], sc_ref[...], cp_ref[...], sp_ref[...],
                                                    cn_ref[...], sn_ref[...])
        cos_g, sin_g, cos_gn, sin_gn = _lanes(cos_c, hp), _lanes(sin_c, hp), _lanes(cos_n, hp), _lanes(sin_n, hp)
        valid_c, valid_p = _band_masks(n, hp)
        kj, qi = _key_query(hp)
        valid_n = jnp.logical_and(kj > qi, n < nb - 1)

        @pl.when(jnp.logical_and(b == 0, n == 0))
        def _():
            db_ref[...] = jnp.zeros_like(db_ref)
            dsk_ref[...] = jnp.zeros_like(dsk_ref)

        for j in range(n_kv):
            ko = j * HEAD_DIM
            kc = _rope(xt_ref[d + ko:d + ko + HEAD_DIM, :], cos_c, sin_c).astype(BF16)
            kp = _rope(pt_ref[ko:ko + HEAD_DIM, :], cos_p, sin_p).astype(BF16)
            vc = xt_ref[d + kvw + ko:d + kvw + ko + HEAD_DIM, :].astype(BF16)
            vp = pt_ref[kvw + ko:kvw + ko + HEAD_DIM, :].astype(BF16)
            dk = jnp.zeros((HEAD_DIM, WINDOW), F32)
            dv = jnp.zeros((HEAD_DIM, WINDOW), F32)
            for h0 in range(j * GROUP, (j + 1) * GROUP, hp):
                q = _rope(_heads(xt_ref, h0, hp), cos_g, sin_g).astype(BF16)
                do_g = _heads(dtc_ref, h0, hp)
                do_b = do_g.astype(BF16)
                lse_g = _head_rows(lc_ref, h0, hp)
                delta = jnp.sum(_heads(otc_ref, h0, hp) * do_g, axis=0, keepdims=True)
                p_c = jnp.exp(jnp.where(valid_c, _tn(kc, q) * scale, NEG) - lse_g)
                p_p = jnp.exp(jnp.where(valid_p, _tn(kp, q) * scale, NEG) - lse_g)
                ds_c = (p_c * (_tn(vc, do_b) - delta)).astype(BF16)
                ds_p = (p_p * (_tn(vp, do_b) - delta)).astype(BF16)
                dq = _unrope((_nn(kc, ds_c) + _nn(kp, ds_p)) * scale, cos_g, sin_g)
                dsk = -jnp.exp(_head_sinks(sink_ref, h0, hp) - lse_g) * delta
                for g in range(hp):
                    h = h0 + g
                    gt_ref[h * HEAD_DIM:(h + 1) * HEAD_DIM, :] = dq[:, g * WINDOW:(g + 1) * WINDOW]
                    dsk_ref[h:h + 1, :] += dsk[:, g * WINDOW:(g + 1) * WINDOW]
                q2 = _rope(_heads(qn_ref, h0, hp), cos_gn, sin_gn).astype(BF16)
                do2 = _heads(dtn_ref, h0, hp)
                do2_b = do2.astype(BF16)
                delta2 = jnp.sum(_heads(otn_ref, h0, hp) * do2, axis=0, keepdims=True)
                p_n = jnp.exp(jnp.where(valid_n, _tn(kc, q2) * scale, NEG) - _head_rows(ln_ref, h0, hp))
                ds_n = (p_n * (_tn(vc, do2_b) - delta2)).astype(BF16)
                dv += _nt(do_b, p_c.astype(BF16)) + _nt(do2_b, p_n.astype(BF16))
                dk += _nt(q, ds_c) + _nt(q2, ds_n)
            gt_ref[d + ko:d + ko + HEAD_DIM, :] = _unrope(dk * scale, cos_c, sin_c)
            gt_ref[d + kvw + ko:d + kvw + ko + HEAD_DIM, :] = dv
        dx = gt_ref[...].T
        dx_ref[...] = dx.astype(BF16)
        db_ref[...] += _fold8(dx)

    cur = lambda b, n: (b * nb + n, 0)
    prev = lambda b, n: (b * nb + jnp.maximum(n - 1, 0), 0)
    nxt = lambda b, n: (b * nb + jnp.minimum(n + 1, nb - 1), 0)
    stat_c = pl.BlockSpec((n_heads, WINDOW), lambda b, n: (0, b * nb + n))
    stat_n = pl.BlockSpec((n_heads, WINDOW), lambda b, n: (0, b * nb + jnp.minimum(n + 1, nb - 1)))
    tab_c = pl.BlockSpec((HEAD_DIM, WINDOW), lambda b, n: (0, n))
    tab_p = pl.BlockSpec((HEAD_DIM, WINDOW), lambda b, n: (0, jnp.maximum(n - 1, 0)))
    tab_n = pl.BlockSpec((HEAD_DIM, WINDOW), lambda b, n: (0, jnp.minimum(n + 1, nb - 1)))
    return pl.pallas_call(
        body,
        name=name,
        grid=(bsz, nb),
        in_specs=[SMEM, pl.BlockSpec((WINDOW, qw), cur), pl.BlockSpec((WINDOW, qw), prev), pl.BlockSpec((WINDOW, qw), nxt),
                  pl.BlockSpec((WINDOW, d), cur), pl.BlockSpec((WINDOW, d), nxt),
                  pl.BlockSpec((WINDOW, d), cur), pl.BlockSpec((WINDOW, d), nxt),
                  stat_c, stat_n, tab_c, tab_c, tab_p, tab_p, tab_n, tab_n],
        out_specs=[pl.BlockSpec((WINDOW, qw), cur), pl.BlockSpec((8, qw), lambda b, n: (0, 0)),
                   pl.BlockSpec((n_heads, WINDOW), lambda b, n: (0, 0))],
        out_shape=[jax.ShapeDtypeStruct((t, qw), BF16), jax.ShapeDtypeStruct((8, qw), F32),
                   jax.ShapeDtypeStruct((n_heads, WINDOW), F32)],
        scratch_shapes=[pltpu.VMEM((qw, WINDOW), F32), pltpu.VMEM((2 * kvw, WINDOW), F32), pltpu.VMEM((d, WINDOW), F32),
                        pltpu.VMEM((d, WINDOW), F32), pltpu.VMEM((d, WINDOW), F32), pltpu.VMEM((d, WINDOW), F32),
                        pltpu.VMEM((d, WINDOW), F32), pltpu.VMEM((qw, WINDOW), F32)],
        compiler_params=_params(("arbitrary", "arbitrary")),
    )(sinks, qkv, qkv, qkv, o, o, do, do, lse, lse, cos_t, sin_t, cos_t, sin_t, cos_t, sin_t)


def _place():
    return lax.axis_index("x"), lax.axis_index("y"), lax.axis_index("c")


def _other_chips(x, y):
    return [(1 - x, y), (x, 1 - y), (1 - x, 1 - y)]


def _place_shard(w, axis, q, name):
    ly, k, n = w.shape
    tr = _pick(k, (256, 128, 64, 32, 16, 8))
    steps = k // tr
    shape = (ly, k * N_CHIPS, n) if axis == 1 else (ly, k, n * N_CHIPS)
    if axis == 1:
        out_spec = pl.BlockSpec((None, tr, n), lambda l, i, q_ref: (l, q_ref[0] * steps + i, 0))
    else:
        out_spec = pl.BlockSpec((None, tr, n), lambda l, i, q_ref: (l, i, q_ref[0]))

    def body(q_ref, w_ref, o_ref):
        del q_ref
        o_ref[...] = w_ref[...].astype(BF16)

    return pl.pallas_call(
        body,
        name=name,
        grid_spec=pltpu.PrefetchScalarGridSpec(
            num_scalar_prefetch=1, grid=(ly, steps),
            in_specs=[pl.BlockSpec((None, tr, n), lambda l, i, q_ref: (l, i, 0))], out_specs=out_spec),
        out_shape=jax.ShapeDtypeStruct(shape, BF16),
        compiler_params=_params(("parallel", "parallel")),
    )(q, w)


def _gather_weights(fulls, axes, name):
    n = len(fulls)

    def body(*refs):
        dst = refs[n:2 * n]
        send_sems, recv_sems = refs[2 * n:]
        x, y, c = _place()
        chips = _other_chips(x, y)

        def half(i, px, py, pc):
            ref, blk = dst[i], 2 * px + py
            if axes[i] == 1:
                rows = ref.shape[1] // (2 * N_CHIPS)
                return ref.at[:, pl.ds(pl.multiple_of((2 * blk + pc) * rows, 8), rows), :]
            rows, width = ref.shape[1] // 2, ref.shape[2] // N_CHIPS
            return ref.at[:, pl.ds(pl.multiple_of(pc * rows, 8), rows), pl.ds(pl.multiple_of(blk * width, 128), width)]

        def copy(i, m, piece, to):
            return pltpu.make_async_remote_copy(
                src_ref=half(i, *piece), dst_ref=half(i, *piece), send_sem=send_sems.at[i * 6 + m],
                recv_sem=recv_sems.at[i * 6 + m], device_id=to, device_id_type=MESH)

        sends = [copy(i, k, (x, y, c), (*chip, c)) for i in range(n) for k, chip in enumerate(chips)]
        for cp in sends:
            cp.start()
        passed = []
        for i in range(n):
            for k, chip in enumerate(chips):
                copy(i, k, (*chip, c), (*chip, c)).wait_recv()
                passed.append(copy(i, 3 + k, (*chip, c), (x, y, 1 - c)))
                passed[-1].start()
        for i in range(n):
            for k, chip in enumerate(chips):
                copy(i, 3 + k, (*chip, 1 - c), (x, y, 1 - c)).wait_recv()
        for cp in sends + passed:
            cp.wait_send()

    return pl.pallas_call(
        body,
        name=name,
        in_specs=[ANY] * n,
        out_specs=[ANY] * n,
        out_shape=[jax.ShapeDtypeStruct(f.shape, f.dtype) for f in fulls],
        input_output_aliases={i: i for i in range(n)},
        scratch_shapes=[pltpu.SemaphoreType.DMA((6 * n,)), pltpu.SemaphoreType.DMA((6 * n,))],
    )(*fulls)


def _half_shape(kind, shape):
    if kind == "col":
        return (shape[0] // 2, shape[1])
    return (N_CHIPS, shape[1] // 2, shape[2])


def _half_of(kind, ref, h):
    if kind == "col":
        r = ref.shape[0] // 2
        return ref.at[pl.ds(pl.multiple_of(h * r, 8), r), :]
    r = ref.shape[1] // 2
    return ref.at[:, pl.ds(pl.multiple_of(h * r, 8), r), :]


def _slice_of_half(kind, ref, s):
    if kind == "col":
        w = ref.shape[1] // N_CHIPS
        return ref.at[:, pl.ds(pl.multiple_of(s * w, 128), w)]
    return ref.at[s]


def _pair_exchange(grads, kinds, name):
    n = len(grads)
    outs = [jax.ShapeDtypeStruct(_half_shape(kd, g.shape), g.dtype) for g, kd in zip(grads, kinds)]

    def body(*refs):
        src, dst = refs[:n], refs[n:2 * n]
        send_sems, recv_sems = refs[2 * n:]
        x, y, c = _place()
        copies = [pltpu.make_async_remote_copy(
            src_ref=_half_of(kinds[i], src[i], 1 - c), dst_ref=dst[i],
            send_sem=send_sems.at[i], recv_sem=recv_sems.at[i], device_id=(x, y, 1 - c), device_id_type=MESH)
            for i in range(n)]
        for cp in copies:
            cp.start()
        for cp in copies:
            cp.wait_recv()
        for cp in copies:
            cp.wait_send()

    return pl.pallas_call(
        body,
        name=name,
        in_specs=[ANY] * n,
        out_specs=[ANY] * n,
        out_shape=outs,
        scratch_shapes=[pltpu.SemaphoreType.DMA((n,)), pltpu.SemaphoreType.DMA((n,))],
    )(*grads)


def _pair_sum(grad, recv, kind, c, name):
    if kind == "col":
        k, n = grad.shape
        rows = k // 2
        tr = _pick(rows, (256, 128, 64, 32, 16, 8))
        steps = rows // tr
        grid = (steps,)
        g_spec = pl.BlockSpec((tr, n), lambda i, c_ref: (c_ref[0] * steps + i, 0))
        r_spec = pl.BlockSpec((tr, n), lambda i, c_ref: (i, 0))
        g_in = grad
    else:
        _, k4, n = grad.shape
        r8 = k4 // 2
        grid = (N_CHIPS,)
        g_spec = pl.BlockSpec((None, None, r8, n), lambda s, c_ref: (s, c_ref[0], 0, 0))
        r_spec = pl.BlockSpec((None, r8, n), lambda s, c_ref: (s, 0, 0))
        g_in = grad.reshape(N_CHIPS, 2, r8, n)

    def body(c_ref, g_ref, r_ref, o_ref):
        del c_ref
        o_ref[...] = (g_ref[...].astype(F32) + r_ref[...].astype(F32)).astype(o_ref.dtype)

    return pl.pallas_call(
        body,
        name=name,
        grid_spec=pltpu.PrefetchScalarGridSpec(num_scalar_prefetch=1, grid=grid, in_specs=[g_spec, r_spec], out_specs=r_spec),
        out_shape=jax.ShapeDtypeStruct(recv.shape, recv.dtype),
        compiler_params=_params(("parallel",)),
    )(c, g_in, recv)


def _chip_scatter(sums, kinds, name):
    flat = [(i, l, s) for i, ss in enumerate(sums) for l, s in enumerate(ss)]
    n, n_t = len(flat), len(sums)
    outs = []
    for ss, kd in zip(sums, kinds):
        s = ss[0]
        shp = (s.shape[0], s.shape[1] // N_CHIPS) if kd == "col" else s.shape[1:]
        outs += [jax.ShapeDtypeStruct((len(ss), N_CHIPS) + tuple(shp), s.dtype)] * 2
    per = 7

    def body(*refs):
        src = refs[:n]
        mine = [refs[n + 2 * i] for i in range(n_t)]
        sib = [refs[n + 2 * i + 1] for i in range(n_t)]
        send_sems, recv_sems, local_sems = refs[n + 2 * n_t:]
        x, y, c = _place()
        q = 2 * x + y
        chips = _other_chips(x, y)
        sibling = (x, y, 1 - c)

        def copy(j, m, src_ref, dst_ref, to):
            return pltpu.make_async_remote_copy(
                src_ref=src_ref, dst_ref=dst_ref, send_sem=send_sems.at[j * per + m], recv_sem=recv_sems.at[j * per + m],
                device_id=to, device_id_type=MESH)

        local, sends = [], []
        for j, (i, l, _) in enumerate(flat):
            own = _slice_of_half(kinds[i], src[j], q)
            local.append(pltpu.make_async_copy(own, mine[i].at[l, q], local_sems.at[j]))
            sends.append(copy(j, 3, own, sib[i].at[l, q], sibling))
            for k, chip in enumerate(chips):
                sends.append(copy(j, k, _slice_of_half(kinds[i], src[j], 2 * chip[0] + chip[1]), mine[i].at[l, q], (*chip, c)))
        for cp in local + sends:
            cp.start()
        for j, (i, l, _) in enumerate(flat):
            for k, chip in enumerate(chips):
                slot = 2 * chip[0] + chip[1]
                copy(j, k, mine[i].at[l, slot], mine[i].at[l, slot], (*chip, c)).wait_recv()
                sends.append(copy(j, 4 + k, mine[i].at[l, slot], sib[i].at[l, slot], sibling))
                sends[-1].start()
        for j, (i, l, _) in enumerate(flat):
            copy(j, 3, sib[i].at[l, q], sib[i].at[l, q], sibling).wait_recv()
            for k, chip in enumerate(chips):
                slot = 2 * chip[0] + chip[1]
                copy(j, 4 + k, sib[i].at[l, slot], sib[i].at[l, slot], sibling).wait_recv()
        for cp in sends:
            cp.wait_send()
        for cp in local:
            cp.wait()

    return pl.pallas_call(
        body,
        name=name,
        in_specs=[ANY] * n,
        out_specs=[ANY] * (2 * n_t),
        out_shape=outs,
        scratch_shapes=[pltpu.SemaphoreType.DMA((per * n,)), pltpu.SemaphoreType.DMA((per * n,)), pltpu.SemaphoreType.DMA((n,))],
    )(*[s for _, _, s in flat])


def _reduce_adamw(mine, sib, w, m, v, c, name):
    ly, _, r, cols = mine.shape
    tr = _pick(r, (128, 64, 32, 16, 8))
    steps = r // tr
    c1 = 1.0 - ADAM_B1 ** ADAM_STEP
    c2 = 1.0 - ADAM_B2 ** ADAM_STEP

    def body(c_ref, mine_ref, sib_ref, w_ref, m_ref, v_ref, g_ref, d_ref, nm_ref, nv_ref):
        def total(ref):
            acc = ref[0].astype(F32)
            for s in range(1, N_CHIPS):
                acc = acc + ref[s].astype(F32)
            return acc

        gv = jnp.where(pl.program_id(1) == c_ref[0], total(mine_ref), total(sib_ref))
        nm = ADAM_B1 * m_ref[...] + (1.0 - ADAM_B1) * gv
        nv = ADAM_B2 * v_ref[...] + (1.0 - ADAM_B2) * (gv * gv)
        g_ref[...] = gv
        d_ref[...] = -ADAM_LR * ((nm / c1) / (jnp.sqrt(nv / c2) + ADAM_EPS) + ADAM_WD * w_ref[...])
        nm_ref[...] = nm
        nv_ref[...] = nv

    slot_spec = pl.BlockSpec((None, N_CHIPS, tr, cols), lambda l, h, i, c_ref: (l, 0, i, 0))
    spec = pl.BlockSpec((None, tr, cols), lambda l, h, i, c_ref: (l, h * steps + i, 0))
    shp = jax.ShapeDtypeStruct(w.shape, F32)
    return pl.pallas_call(
        body,
        name=name,
        grid_spec=pltpu.PrefetchScalarGridSpec(
            num_scalar_prefetch=1, grid=(ly, N_CORES, steps),
            in_specs=[slot_spec, slot_spec, spec, spec, spec], out_specs=[spec] * 4),
        out_shape=[shp] * 4,
        compiler_params=_params(("parallel", "parallel", "parallel")),
    )(c, mine, sib, w, m, v)


def _allreduce_small(v, name):
    r, w = v.shape

    def body(v_ref, o_ref, buf_ref, send_sems, recv_sems):
        x, y, c = _place()
        me = 4 * x + 2 * y + c

        def peer(k):
            return x ^ (k >> 2), y ^ ((k >> 1) & 1), c ^ (k & 1)

        def remote(k, slot):
            return pltpu.make_async_remote_copy(
                src_ref=v_ref, dst_ref=buf_ref.at[slot], send_sem=send_sems.at[k - 1], recv_sem=recv_sems.at[k - 1],
                device_id=peer(k), device_id_type=MESH)

        sends = [remote(k, me) for k in range(1, N_DEV)]
        for cp in sends:
            cp.start()
        buf_ref[me] = v_ref[...]
        for k in range(1, N_DEV):
            px, py, pc = peer(k)
            remote(k, 4 * px + 2 * py + pc).wait_recv()
        for cp in sends:
            cp.wait_send()
        acc = buf_ref[0]
        for dev in range(1, N_DEV):
            acc = acc + buf_ref[dev]
        o_ref[...] = acc

    vm = pl.BlockSpec(memory_space=pltpu.VMEM)
    return pl.pallas_call(
        body,
        name=name,
        in_specs=[vm],
        out_specs=vm,
        out_shape=jax.ShapeDtypeStruct((r, w), F32),
        scratch_shapes=[pltpu.VMEM((N_DEV, r, w), F32), pltpu.SemaphoreType.DMA((N_DEV - 1,)), pltpu.SemaphoreType.DMA((N_DEV - 1,))],
        compiler_params=pltpu.CompilerParams(vmem_limit_bytes=VMEM_LIMIT_BYTES),
    )(v)


def _adamw(w, g, m, v, name):
    ly, r, c = w.shape
    tr = _pick(r, (256, 128, 64, 32, 16, 8))
    c1 = 1.0 - ADAM_B1 ** ADAM_STEP
    c2 = 1.0 - ADAM_B2 ** ADAM_STEP

    def body(w_ref, g_ref, m_ref, v_ref, d_ref, nm_ref, nv_ref):
        gv = g_ref[...]
        nm = ADAM_B1 * m_ref[...] + (1.0 - ADAM_B1) * gv
        nv = ADAM_B2 * v_ref[...] + (1.0 - ADAM_B2) * (gv * gv)
        d_ref[...] = -ADAM_LR * ((nm / c1) / (jnp.sqrt(nv / c2) + ADAM_EPS) + ADAM_WD * w_ref[...])
        nm_ref[...] = nm
        nv_ref[...] = nv

    spec = pl.BlockSpec((None, tr, c), lambda l, i: (l, i, 0))
    shp = jax.ShapeDtypeStruct((ly, r, c), F32)
    return pl.pallas_call(
        body,
        name=name,
        grid=(ly, r // tr),
        in_specs=[spec] * 4,
        out_specs=[spec] * 3,
        out_shape=[shp] * 3,
        compiler_params=_params(("parallel", "parallel")),
    )(w, g, m, v)


def _rope_tables(seq):
    pos = jnp.arange(seq, dtype=F32)
    inv_freq = 1.0 / (ROPE_THETA ** (jnp.arange(0, HEAD_DIM, 2, dtype=F32) / HEAD_DIM))
    ang = (pos[:, None] * inv_freq[None, :]).T
    cos, sin = jnp.cos(ang), jnp.sin(ang)
    return jnp.concatenate([cos, cos], axis=0), jnp.concatenate([-sin, sin], axis=0)


def _pack(vs, fill=0.0):
    p = jnp.concatenate([v.reshape(-1) for v in vs])
    size = -(-p.shape[0] // 8192) * 8192
    return jnp.pad(p, (0, size - p.shape[0]), constant_values=fill).reshape(-1, 1024)


def _unpack(p, like):
    p = p.reshape(-1)
    out, o = [], 0
    for v in like:
        n = int(math.prod(v.shape))
        out.append(p[o:o + n].reshape(v.shape))
        o += n
    return out


def kernel(x, norm_mix, norm_ffn, norm_final, conv_w_in, conv_w_conv, conv_w_out, attn_w_qkv, attn_b_qkv, attn_sinks, attn_w_o, attn_b_o, ffn_w_in, ffn_w_conv, ffn_w_down, loss_target, m_norm_mix, m_norm_ffn, m_norm_final, m_conv_w_in, m_conv_w_conv, m_conv_w_out, m_attn_w_qkv, m_attn_b_qkv, m_attn_sinks, m_attn_w_o, m_attn_b_o, m_ffn_w_in, m_ffn_w_conv, m_ffn_w_down, v_norm_mix, v_norm_ffn, v_norm_final, v_conv_w_in, v_conv_w_conv, v_conv_w_out, v_attn_w_qkv, v_attn_b_qkv, v_attn_sinks, v_attn_w_o, v_attn_b_o, v_ffn_w_in, v_ffn_w_conv, v_ffn_w_down):
    bsz, seq, d = x.shape
    t = bsz * seq
    depth = norm_mix.shape[0]
    n_conv, n_attn = conv_w_in.shape[0], attn_w_qkv.shape[0]
    xq, yq, cq = _place()
    q = 2 * xq + yq

    big = [conv_w_in, conv_w_out, attn_w_qkv, attn_w_o, ffn_w_in, ffn_w_down]
    axes = [2, 1, 2, 1, 2, 1]
    q_arr = q.astype(jnp.int32).reshape(1)
    c_arr = cq.astype(jnp.int32).reshape(1)
    w_cin, w_cout, w_qkv, w_o, w_fin, w_fdown = _gather_weights(
        [_place_shard(w, ax, q_arr, f"place_shard{n}") for n, (w, ax) in enumerate(zip(big, axes))], axes, "gather_weights")

    small_cols = [conv_w_conv, attn_b_qkv, attn_b_o, ffn_w_conv]

    def placed(v):
        width = v.shape[-1]
        full = jnp.zeros(v.shape[:-1] + (N_CHIPS * width,), F32)
        return lax.dynamic_update_slice_in_dim(full, v * (1.0 / N_CORES), q * width, axis=v.ndim - 1)

    full_cols = [placed(v) for v in small_cols]
    wc_conv, b_qkv, b_o, wf_conv = _unpack(_allreduce_small(_pack(full_cols), "gather_small"), full_cols)
    cos_t, sin_t = _rope_tables(seq)

    xs = x.reshape(t, d)
    saved = []
    for i in range(depth):
        j = i // 2
        h = _rms_fwd(xs, norm_mix[i:i + 1], f"norm_mix_fwd{i}")
        if i % 2 == 0:
            pre = _mm(h, w_cin, "nn", BF16, layer=j, tm=1024, tn=768, tk=4096, name=f"conv_in_fwd{i}")
            mixed = _convgate_fwd(pre, wc_conv[j], seq, f"conv_gate_fwd{i}")
            x_mid = _mm(mixed, w_cout, "nn", F32, layer=j, residual=xs, tm=512, tn=1024, tk=4096, name=f"conv_out_fwd{i}")
            lse = None
        else:
            pre = _mm(h, w_qkv, "nn", F32, layer=j, bias=b_qkv[j:j + 1], tm=1024, tn=768, tk=4096, name=f"qkv_fwd{i}")
            mixed, lse = _attn_fwd(pre, attn_sinks[j], cos_t, sin_t, bsz, seq, f"attn_fwd{i}", hp=1)
            x_mid = _mm(mixed, w_o, "nn", F32, layer=j, bias=b_o[j:j + 1], residual=xs, tm=512, tn=1024, tk=4096,
                        name=f"attn_out_fwd{i}")
        h2 = _rms_fwd(x_mid, norm_ffn[i:i + 1], f"norm_ffn_fwd{i}")
        gu = _mm(h2, w_fin, "nn", BF16, layer=i, n_outer=True, tm=512, tn=1408, tk=4096, name=f"ffn_in_fwd{i}")
        act = _ffngate_fwd(gu, wf_conv[i], seq, f"ffn_gate_fwd{i}")
        x_next = _mm(act, w_fdown, "nn", F32, layer=i, residual=x_mid, tm=512, tn=1024, tk=4096, name=f"ffn_down_fwd{i}")
        saved.append((xs, h, pre, mixed, lse, x_mid, h2, gu, act))
        xs = x_next

    dx, dxb, sq, dg_final = _loss_head(xs, loss_target.reshape(t, d), norm_final.reshape(1, d), "loss_head")
    loss = lax.psum(0.5 * jnp.sum(sq) / d, ("x", "y", "c"))

    g_norm_mix, g_norm_ffn = [None] * depth, [None] * depth
    g_cin, g_cconv, g_cout = [None] * n_conv, [None] * n_conv, [None] * n_conv
    g_qkv, g_bqkv, g_sinks, g_o, g_bo = ([None] * n_attn for _ in range(5))
    g_fin, g_fconv, g_fdown = [None] * depth, [None] * depth, [None] * depth
    for i in reversed(range(depth)):
        j = i // 2
        x_in, h, pre, mixed, lse, x_mid, h2, gu, act = saved[i]
        da = _mm(dxb, w_fdown, "nt", BF16, layer=i, n_outer=True, tm=512, tn=1408, tk=4096, name=f"ffn_down_dx{i}")
        g_fdown[i] = _mm(act, dxb, "tn", BF16, tm=1408, tn=1024, tk=2048, name=f"ffn_down_dw{i}")
        dgu, dwc = _ffngate_bwd(gu, da, wf_conv[i], seq, f"ffn_gate_bwd{i}")
        g_fconv[i] = dwc[:3]
        g_fin[i] = _mm(h2, dgu, "tn", BF16, tm=1024, tn=1408, tk=2048, name=f"ffn_in_dw{i}")
        dh2 = _mm(dgu, w_fin, "nt", F32, layer=i, tm=512, tn=1024, tk=8192, name=f"ffn_in_dx{i}")
        dx, dxb, dg, colsum = _rms_bwd(x_mid, dh2, norm_ffn[i:i + 1], dx, f"norm_ffn_bwd{i}")
        g_norm_ffn[i] = jnp.sum(dg, axis=0)
        if i % 2 == 0:
            dmix = _mm(dxb, w_cout, "nt", BF16, layer=j, tm=512, tn=1024, tk=4096, name=f"conv_out_dx{i}")
            g_cout[j] = _mm(mixed, dxb, "tn", BF16, tm=1024, tn=1024, tk=2048, name=f"conv_out_dw{i}")
            dpre, dwc = _convgate_bwd(pre, dmix, wc_conv[j], seq, f"conv_gate_bwd{i}")
            g_cconv[j] = dwc[:3]
            g_cin[j] = _mm(h, dpre, "tn", BF16, tm=1024, tn=1536, tk=2048, name=f"conv_in_dw{i}")
            dh = _mm(dpre, w_cin, "nt", F32, layer=j, tm=512, tn=1024, tk=8192, name=f"conv_in_dx{i}")
        else:
            g_bo[j] = jnp.sum(colsum, axis=0)
            dmix = _mm(dxb, w_o, "nt", F32, layer=j, tm=512, tn=1024, tk=4096, name=f"attn_out_dx{i}")
            g_o[j] = _mm(mixed, dxb, "tn", BF16, tm=1024, tn=1024, tk=2048, name=f"attn_out_dw{i}")
            dpre, dbias, dsk = _attn_bwd(pre, mixed, lse, dmix, attn_sinks[j], cos_t, sin_t, bsz, seq, f"attn_bwd{i}",
                                         hp=GROUP)
            g_bqkv[j] = jnp.sum(dbias, axis=0)
            g_sinks[j] = jnp.sum(dsk, axis=1)
            g_qkv[j] = _mm(h, dpre, "tn", BF16, tm=1024, tn=1536, tk=2048, name=f"qkv_dw{i}")
            dh = _mm(dpre, w_qkv, "nt", F32, layer=j, tm=512, tn=1024, tk=1536, name=f"qkv_dx{i}")
        dx, dxb, dg, _ = _rms_bwd(x_in, dh, norm_mix[i:i + 1], dx, f"norm_mix_bwd{i}")
        g_norm_mix[i] = jnp.sum(dg, axis=0)
    grad_x = dx.reshape(bsz, seq, d)

    tensors = [(g_cin, "col"), (g_cout, "row"), (g_qkv, "col"), (g_o, "row"), (g_fin, "col"), (g_fdown, "row")]
    t_kinds = [kd for _, kd in tensors]
    flat, kinds = [], []
    for gs, kd in tensors:
        for g in gs:
            flat.append(g if kd == "col" else g.reshape(N_CHIPS, g.shape[0] // N_CHIPS, g.shape[1]))
            kinds.append(kd)
    recv = _pair_exchange(flat, kinds, "grad_pair_exchange")
    sums = [_pair_sum(g, r, kd, c_arr, f"grad_pair_sum{n}") for n, (g, r, kd) in enumerate(zip(flat, recv, kinds))]
    by_tensor, pos = [], 0
    for gs, _ in tensors:
        by_tensor.append(sums[pos:pos + len(gs)])
        pos += len(gs)
    slots = _chip_scatter(by_tensor, t_kinds, "grad_chip_scatter")
    big_w = [conv_w_in, conv_w_out, attn_w_qkv, attn_w_o, ffn_w_in, ffn_w_down]
    big_m = [m_conv_w_in, m_conv_w_out, m_attn_w_qkv, m_attn_w_o, m_ffn_w_in, m_ffn_w_down]
    big_v = [v_conv_w_in, v_conv_w_out, v_attn_w_qkv, v_attn_w_o, v_ffn_w_in, v_ffn_w_down]
    big_names = ["conv_w_in", "conv_w_out", "attn_w_qkv", "attn_w_o", "ffn_w_in", "ffn_w_down"]
    big_upd = [_reduce_adamw(slots[2 * n], slots[2 * n + 1], big_w[n], big_m[n], big_v[n], c_arr, f"adamw_{nm}")
               for n, nm in enumerate(big_names)]

    small = [jnp.stack(g_norm_mix), jnp.stack(g_norm_ffn), jnp.sum(dg_final, axis=0), jnp.stack(g_cconv),
             jnp.stack(g_bqkv), jnp.stack(g_sinks), jnp.stack(g_bo), jnp.stack(g_fconv)]
    sg = _unpack(_allreduce_small(_pack(small), "grad_small_allreduce"), small)

    def my_cols(v, like):
        width = like.shape[-1]
        return lax.dynamic_slice_in_dim(v, q * width, width, axis=v.ndim - 1)

    small_w = [norm_mix, norm_ffn, norm_final, conv_w_conv, attn_b_qkv, attn_sinks, attn_b_o, ffn_w_conv]
    small_m = [m_norm_mix, m_norm_ffn, m_norm_final, m_conv_w_conv, m_attn_b_qkv, m_attn_sinks, m_attn_b_o, m_ffn_w_conv]
    small_v = [v_norm_mix, v_norm_ffn, v_norm_final, v_conv_w_conv, v_attn_b_qkv, v_attn_sinks, v_attn_b_o, v_ffn_w_conv]
    small_g = [sg[0], sg[1], sg[2], my_cols(sg[3], conv_w_conv), my_cols(sg[4], attn_b_qkv), sg[5],
               my_cols(sg[6], attn_b_o), my_cols(sg[7], ffn_w_conv)]

    upd = {nm: tuple(u[1:]) for nm, u in zip(big_names, big_upd)}
    sd, sm, sv = _adamw(_pack(small_w)[None], _pack(small_g)[None], _pack(small_m)[None], _pack(small_v, 1.0)[None],
                        "adamw_small")
    sd, sm, sv = _unpack(sd, small_w), _unpack(sm, small_w), _unpack(sv, small_w)
    names = ["norm_mix", "norm_ffn", "norm_final", "conv_w_in", "conv_w_conv", "conv_w_out", "attn_w_qkv", "attn_b_qkv",
             "attn_sinks", "attn_w_o", "attn_b_o", "ffn_w_in", "ffn_w_conv", "ffn_w_down"]
    small_names = ["norm_mix", "norm_ffn", "norm_final", "conv_w_conv", "attn_b_qkv", "attn_sinks", "attn_b_o", "ffn_w_conv"]
    grads = dict(zip(small_names, small_g))
    grads.update({nm: u[0] for nm, u in zip(big_names, big_upd)})
    for n, nm in enumerate(small_names):
        upd[nm] = (sd[n], sm[n], sv[n])
    return (loss, grad_x, *[grads[nm] for nm in names], *[upd[nm][0] for nm in names],
            *[upd[nm][1] for nm in names], *[upd[nm][2] for nm in names])
```

```python
import math

import jax
import jax.numpy as jnp
from jax import lax
from jax.experimental import pallas as pl
from jax.experimental.pallas import tpu as pltpu

F32 = jnp.float32
BF16 = jnp.bfloat16

HEAD_DIM = 64
GROUP = 4
WINDOW = 128
EPS = 1e-5
ROPE_THETA = 10000.0
ADAM_LR, ADAM_B1, ADAM_B2, ADAM_EPS, ADAM_WD, ADAM_STEP = 0.001, 0.9, 0.999, 1e-08, 0.01, 10

N_CHIPS = 4
N_CORES = 2
N_DEV = 8
HALO = 16
VMEM_LIMIT_BYTES = 56 * 1024 * 1024
MESH = pl.DeviceIdType.MESH
ANY = pl.BlockSpec(memory_space=pl.ANY)
SMEM = pl.BlockSpec(memory_space=pltpu.SMEM)
NEG = float(jnp.finfo(jnp.float32).min)
ROW_TILES = (512, 256, 128, 64, 32, 16, 8)


def _pick(dim, cands):
    for c in cands:
        if dim % c == 0:
            return c
    return dim


def _params(sem):
    return pltpu.CompilerParams(dimension_semantics=sem, vmem_limit_bytes=VMEM_LIMIT_BYTES)


_DIMS = {"nn": (((1,), (0,)), ((), ())), "nt": (((1,), (1,)), ((), ())), "tn": (((0,), (0,)), ((), ()))}


def _mm(a, b, mode, out_dtype, *, layer=None, bias=None, residual=None, n_outer=False, tm, tn, tk, name):
    b2 = b.shape[1:] if layer is not None else b.shape
    if mode == "nn":
        (m, k), n = a.shape, b2[1]
    elif mode == "nt":
        (m, k), n = a.shape, b2[0]
    else:
        (k, m), n = a.shape, b2[1]
    tm, tn, tk = min(tm, m), min(tn, n), min(tk, k)
    assert m % tm == 0 and n % tn == 0 and k % tk == 0, (name, a.shape, b.shape, tm, tn, tk)
    nk = k // tk

    def at(f):
        return (lambda p0, p1, p2: f(p1, p0, p2)) if n_outer else f

    a_spec = pl.BlockSpec((tk, tm), at(lambda i, j, l: (l, i))) if mode == "tn" else pl.BlockSpec((tm, tk), at(lambda i, j, l: (i, l)))
    if layer is None:
        b_spec = (pl.BlockSpec((tn, tk), at(lambda i, j, l: (j, l))) if mode == "nt"
                  else pl.BlockSpec((tk, tn), at(lambda i, j, l: (l, j))))
    elif mode == "nt":
        b_spec = pl.BlockSpec((None, tn, tk), at(lambda i, j, l: (layer, j, l)))
    else:
        b_spec = pl.BlockSpec((None, tk, tn), at(lambda i, j, l: (layer, l, j)))
    in_specs, args = [a_spec, b_spec], [a, b]
    if bias is not None:
        in_specs.append(pl.BlockSpec((1, tn), at(lambda i, j, l: (0, j))))
        args.append(bias)
    if residual is not None:
        in_specs.append(pl.BlockSpec((tm, tn), at(lambda i, j, l: (i, j))))
        args.append(residual)
    has_bias, has_res = bias is not None, residual is not None

    def body(*refs):
        a_ref, b_ref = refs[0], refs[1]
        pos = 2
        bias_ref = res_ref = None
        if has_bias:
            bias_ref, pos = refs[pos], pos + 1
        if has_res:
            res_ref, pos = refs[pos], pos + 1
        o_ref = refs[pos]
        acc_ref = refs[pos + 1] if nk > 1 else None

        def finish(acc):
            if has_bias:
                acc = acc + bias_ref[...]
            if has_res:
                acc = acc + res_ref[...]
            o_ref[...] = acc.astype(o_ref.dtype)

        if nk == 1:
            finish(lax.dot_general(a_ref[...], b_ref[...], _DIMS[mode], preferred_element_type=F32))
            return
        l = pl.program_id(2)
        part = lax.dot_general(a_ref[...], b_ref[...], _DIMS[mode], preferred_element_type=F32)

        @pl.when(l == 0)
        def _():
            acc_ref[...] = part

        @pl.when(l > 0)
        def _():
            acc_ref[...] += part

        @pl.when(l == nk - 1)
        def _():
            finish(acc_ref[...])

    return pl.pallas_call(
        body,
        name=name,
        grid=(n // tn, m // tm, nk) if n_outer else (m // tm, n // tn, nk),
        in_specs=in_specs,
        out_specs=pl.BlockSpec((tm, tn), at(lambda i, j, l: (i, j))),
        out_shape=jax.ShapeDtypeStruct((m, n), out_dtype),
        scratch_shapes=[pltpu.VMEM((tm, tn), F32)] if nk > 1 else [],
        compiler_params=_params(("parallel", "parallel", "arbitrary")),
    )(*args)


def _fold8(v):
    r, d = v.shape
    return jnp.sum(v.reshape(r // 8, 8, d), axis=0)


def _rms_fwd(x, g, name):
    t, d = x.shape
    tm = _pick(t, ROW_TILES)

    def body(x_ref, g_ref, h_ref):
        xv = x_ref[...]
        r = lax.rsqrt(jnp.mean(xv * xv, axis=-1, keepdims=True) + EPS)
        h_ref[...] = (xv * r * g_ref[...]).astype(BF16)

    return pl.pallas_call(
        body,
        name=name,
        grid=(t // tm,),
        in_specs=[pl.BlockSpec((tm, d), lambda i: (i, 0)), pl.BlockSpec((1, d), lambda i: (0, 0))],
        out_specs=pl.BlockSpec((tm, d), lambda i: (i, 0)),
        out_shape=jax.ShapeDtypeStruct((t, d), BF16),
        compiler_params=_params(("parallel",)),
    )(x, g)


def _rms_bwd(x, dh, g, dx_in, name):
    t, d = x.shape
    tm = _pick(t, ROW_TILES)

    def body(x_ref, dh_ref, g_ref, dxi_ref, dx_ref, dxb_ref, dg_ref, cs_ref):
        i = pl.program_id(0)
        xv = x_ref[...]
        r = lax.rsqrt(jnp.mean(xv * xv, axis=-1, keepdims=True) + EPS)
        xhat = xv * r
        dy = dh_ref[...]
        gdy = dy * g_ref[...]
        dx = dxi_ref[...] + r * (gdy - xhat * jnp.mean(gdy * xhat, axis=-1, keepdims=True))
        dx_ref[...] = dx
        dxb_ref[...] = dx.astype(BF16)

        @pl.when(i == 0)
        def _():
            dg_ref[...] = jnp.zeros_like(dg_ref)
            cs_ref[...] = jnp.zeros_like(cs_ref)

        dg_ref[...] += _fold8(dy * xhat)
        cs_ref[...] += _fold8(dx)

    row = pl.BlockSpec((tm, d), lambda i: (i, 0))
    acc = pl.BlockSpec((8, d), lambda i: (0, 0))
    return pl.pallas_call(
        body,
        name=name,
        grid=(t // tm,),
        in_specs=[row, row, pl.BlockSpec((1, d), lambda i: (0, 0)), row],
        out_specs=[row, row, acc, acc],
        out_shape=[jax.ShapeDtypeStruct((t, d), F32), jax.ShapeDtypeStruct((t, d), BF16),
                   jax.ShapeDtypeStruct((8, d), F32), jax.ShapeDtypeStruct((8, d), F32)],
        compiler_params=_params(("arbitrary",)),
    )(x, dh, g, dx_in)


def _loss_head(x, target, g, name):
    t, d = x.shape
    tm = _pick(t, ROW_TILES)
    inv_d = 1.0 / d

    def body(x_ref, t_ref, g_ref, dx_ref, dxb_ref, sq_ref, dg_ref):
        i = pl.program_id(0)
        xv = x_ref[...]
        gv = g_ref[...]
        r = lax.rsqrt(jnp.mean(xv * xv, axis=-1, keepdims=True) + EPS)
        xhat = xv * r
        err = xhat * gv - t_ref[...]
        dy = err * inv_d
        gdy = dy * gv
        dx = r * (gdy - xhat * jnp.mean(gdy * xhat, axis=-1, keepdims=True))
        dx_ref[...] = dx
        dxb_ref[...] = dx.astype(BF16)

        @pl.when(i == 0)
        def _():
            sq_ref[...] = jnp.zeros_like(sq_ref)
            dg_ref[...] = jnp.zeros_like(dg_ref)

        sq_ref[...] += _fold8(err * err)
        dg_ref[...] += _fold8(dy * xhat)

    row = pl.BlockSpec((tm, d), lambda i: (i, 0))
    acc = pl.BlockSpec((8, d), lambda i: (0, 0))
    return pl.pallas_call(
        body,
        name=name,
        grid=(t // tm,),
        in_specs=[row, row, pl.BlockSpec((1, d), lambda i: (0, 0))],
        out_specs=[row, row, acc, acc],
        out_shape=[jax.ShapeDtypeStruct((t, d), F32), jax.ShapeDtypeStruct((t, d), BF16),
                   jax.ShapeDtypeStruct((8, d), F32), jax.ShapeDtypeStruct((8, d), F32)],
        compiler_params=_params(("arbitrary",)),
    )(x, target, g)


def _rows(tm):
    return lax.broadcasted_iota(jnp.int32, (tm, 1), 0)


def _shift_down(u, before2):
    r8 = _rows(8)
    s1, s2 = pltpu.roll(u, 1, 0), pltpu.roll(u, 2, 0)
    top1 = jnp.where(r8 == 0, before2[1:2], s1[:8])
    top2 = jnp.where(r8 == 0, before2[0:1], jnp.where(r8 == 1, before2[1:2], s2[:8]))
    return jnp.concatenate([top1, s1[8:]], axis=0), jnp.concatenate([top2, s2[8:]], axis=0)


def _shift_up(u, after2):
    tm = u.shape[0]
    r8 = _rows(8)
    s1, s2 = pltpu.roll(u, tm - 1, 0), pltpu.roll(u, tm - 2, 0)
    bot1 = jnp.where(r8 == 7, after2[0:1], s1[tm - 8:])
    bot2 = jnp.where(r8 == 6, after2[0:1], jnp.where(r8 == 7, after2[1:2], s2[tm - 8:]))
    return jnp.concatenate([s1[:tm - 8], bot1], axis=0), jnp.concatenate([s2[:tm - 8], bot2], axis=0)


def _shift_matrix(tm, up):
    r = lax.broadcasted_iota(jnp.int32, (2 * tm, tm), 0)
    c = lax.broadcasted_iota(jnp.int32, (2 * tm, tm), 1)
    t = jnp.where(r >= tm, r - tm, r)
    k = jnp.where(r >= tm, 2, 1)
    return (c == (t + k if up else t - k)).astype(BF16)


def _shift_down_mxu(u, before2):
    tm = u.shape[0]
    moved = jnp.dot(_shift_matrix(tm, False), u.astype(BF16), preferred_element_type=F32)
    r8 = _rows(8)
    s1, s2 = moved[:tm], moved[tm:]
    top1 = s1[:8] + jnp.where(r8 == 0, before2[1:2], 0.0)
    top2 = s2[:8] + jnp.where(r8 == 0, before2[0:1], jnp.where(r8 == 1, before2[1:2], 0.0))
    return jnp.concatenate([top1, s1[8:]], axis=0), jnp.concatenate([top2, s2[8:]], axis=0)


def _shift_up_mxu(u, after2):
    tm = u.shape[0]
    moved = jnp.dot(_shift_matrix(tm, True), u.astype(BF16), preferred_element_type=F32)
    r8 = _rows(8)
    s1, s2 = moved[:tm], moved[tm:]
    bot1 = s1[tm - 8:] + jnp.where(r8 == 7, after2[0:1], 0.0)
    bot2 = s2[tm - 8:] + jnp.where(r8 == 6, after2[0:1], jnp.where(r8 == 7, after2[1:2], 0.0))
    return jnp.concatenate([s1[:tm - 8], bot1], axis=0), jnp.concatenate([s2[:tm - 8], bot2], axis=0)


def _conv_tile(seq):
    return _pick(seq, (256, 128, 64, 32, 16, 8))


def _halo_specs(tm, width, n_tiles):
    per = tm // HALO
    before = pl.BlockSpec((HALO, width), lambda i: (jnp.maximum(i * per - 1, 0), 0))
    after = pl.BlockSpec((HALO, width), lambda i: (jnp.minimum((i + 1) * per, n_tiles * per - 1), 0))
    return before, after


def _convgate_fwd(bcv, w, seq, name):
    t, d3 = bcv.shape
    d = d3 // 3
    tm = _conv_tile(seq)
    tps = seq // tm
    before, _ = _halo_specs(tm, d3, t // tm)

    def body(x_ref, xb_ref, w_ref, y_ref):
        i = pl.program_id(0)
        inner = (i % tps != 0).astype(F32)
        u = x_ref[:, d:2 * d].astype(F32) * x_ref[:, 2 * d:].astype(F32)
        xb = xb_ref[:, d:].astype(F32)[HALO - 2:]
        s1, s2 = _shift_down(u, xb[:, :d] * xb[:, d:] * inner)
        z = w_ref[2:3] * u + w_ref[1:2] * s1 + w_ref[0:1] * s2
        y_ref[...] = (x_ref[:, :d].astype(F32) * z).astype(BF16)

    return pl.pallas_call(
        body,
        name=name,
        grid=(t // tm,),
        in_specs=[pl.BlockSpec((tm, d3), lambda i: (i, 0)), before, pl.BlockSpec((3, d), lambda i: (0, 0))],
        out_specs=pl.BlockSpec((tm, d), lambda i: (i, 0)),
        out_shape=jax.ShapeDtypeStruct((t, d), BF16),
        compiler_params=_params(("parallel",)),
    )(bcv, bcv, w)


def _convgate_bwd(bcv, dy, w, seq, name):
    t, d3 = bcv.shape
    d = d3 // 3
    tm = _conv_tile(seq)
    tps = seq // tm
    before, after = _halo_specs(tm, d3, t // tm)
    _, after_dy = _halo_specs(tm, d, t // tm)

    def body(x_ref, xb_ref, xa_ref, dy_ref, dya_ref, w_ref, dx_ref, dw_ref):
        i = pl.program_id(0)
        inner_lo = (i % tps != 0).astype(F32)
        inner_hi = (i % tps != tps - 1).astype(F32)
        w0, w1, w2 = w_ref[0:1], w_ref[1:2], w_ref[2:3]
        b, c, v = x_ref[:, :d].astype(F32), x_ref[:, d:2 * d].astype(F32), x_ref[:, 2 * d:].astype(F32)
        u = c * v
        xb = xb_ref[:, d:].astype(F32)[HALO - 2:]
        s1, s2 = _shift_down(u, xb[:, :d] * xb[:, d:] * inner_lo)
        z = w2 * u + w1 * s1 + w0 * s2
        dyv = dy_ref[...].astype(F32)
        dz = dyv * b
        dza = dya_ref[...].astype(F32)[0:2] * xa_ref[:, :d].astype(F32)[0:2] * inner_hi
        n1, n2 = _shift_up(dz, dza)
        du = w2 * dz + w1 * n1 + w0 * n2
        dx_ref[:, :d] = (dyv * z).astype(BF16)
        dx_ref[:, d:2 * d] = (du * v).astype(BF16)
        dx_ref[:, 2 * d:] = (du * c).astype(BF16)

        @pl.when(i == 0)
        def _():
            dw_ref[...] = jnp.zeros_like(dw_ref)

        dw_ref[0:1] += jnp.sum(dz * s2, axis=0, keepdims=True)
        dw_ref[1:2] += jnp.sum(dz * s1, axis=0, keepdims=True)
        dw_ref[2:3] += jnp.sum(dz * u, axis=0, keepdims=True)

    return pl.pallas_call(
        body,
        name=name,
        grid=(t // tm,),
        in_specs=[pl.BlockSpec((tm, d3), lambda i: (i, 0)), before, after,
                  pl.BlockSpec((tm, d), lambda i: (i, 0)), after_dy, pl.BlockSpec((3, d), lambda i: (0, 0))],
        out_specs=[pl.BlockSpec((tm, d3), lambda i: (i, 0)), pl.BlockSpec((8, d), lambda i: (0, 0))],
        out_shape=[jax.ShapeDtypeStruct((t, d3), BF16), jax.ShapeDtypeStruct((8, d), F32)],
        compiler_params=_params(("arbitrary",)),
    )(bcv, bcv, bcv, dy, dy, w)


def _sigmoid(x):
    return 1.0 / (1.0 + jnp.exp(-x))


def _ffngate_fwd(gu, w, seq, name):
    t, f2 = gu.shape
    f = f2 // 2
    tm = _conv_tile(seq)
    tps = seq // tm
    before, _ = _halo_specs(tm, f2, t // tm)

    def body(x_ref, xb_ref, w_ref, a_ref):
        i = pl.program_id(0)
        inner = (i % tps != 0).astype(F32)
        s1, s2 = _shift_down_mxu(x_ref[:, :f], xb_ref[:, :f].astype(F32)[HALO - 2:] * inner)
        gc = w_ref[2:3] * x_ref[:, :f].astype(F32) + w_ref[1:2] * s1 + w_ref[0:1] * s2
        a_ref[...] = (gc * _sigmoid(gc) * x_ref[:, f:].astype(F32)).astype(BF16)

    return pl.pallas_call(
        body,
        name=name,
        grid=(t // tm,),
        in_specs=[pl.BlockSpec((tm, f2), lambda i: (i, 0)), before, pl.BlockSpec((3, f), lambda i: (0, 0))],
        out_specs=pl.BlockSpec((tm, f), lambda i: (i, 0)),
        out_shape=jax.ShapeDtypeStruct((t, f), BF16),
        compiler_params=_params(("parallel",)),
    )(gu, gu, w)


def _ffngate_bwd(gu, da, w, seq, name):
    t, f2 = gu.shape
    f = f2 // 2
    tm = _conv_tile(seq)
    tps = seq // tm
    before, after = _halo_specs(tm, f2, t // tm)
    _, after_da = _halo_specs(tm, f, t // tm)

    def body(x_ref, xb_ref, xa_ref, da_ref, daa_ref, w_ref, dx_ref, dw_ref):
        i = pl.program_id(0)
        inner_lo = (i % tps != 0).astype(F32)
        inner_hi = (i % tps != tps - 1).astype(F32)
        w0, w1, w2 = w_ref[0:1], w_ref[1:2], w_ref[2:3]

        def dgate(gc, uv, dav):
            sg = _sigmoid(gc)
            return dav * uv * (sg * (1.0 + gc * (1.0 - sg))), dav * (gc * sg)

        g, u = x_ref[:, :f].astype(F32), x_ref[:, f:].astype(F32)
        s1, s2 = _shift_down_mxu(x_ref[:, :f], xb_ref[:, :f].astype(F32)[HALO - 2:] * inner_lo)
        gc = w2 * g + w1 * s1 + w0 * s2
        dgc, du = dgate(gc, u, da_ref[...].astype(F32))
        ga = xa_ref[:, :f].astype(F32)
        a1, a2 = _shift_down(ga, x_ref[tm - HALO:, :f].astype(F32)[HALO - 2:])
        gca = w2 * ga + w1 * a1 + w0 * a2
        dgca, _ = dgate(gca, xa_ref[:, f:].astype(F32), daa_ref[...].astype(F32))
        n1, n2 = _shift_up_mxu(dgc, dgca[0:2] * inner_hi)
        dx_ref[:, :f] = (w2 * dgc + w1 * n1 + w0 * n2).astype(BF16)
        dx_ref[:, f:] = du.astype(BF16)

        @pl.when(i == 0)
        def _():
            dw_ref[...] = jnp.zeros_like(dw_ref)

        dw_ref[0:1] += jnp.sum(dgc * s2, axis=0, keepdims=True)
        dw_ref[1:2] += jnp.sum(dgc * s1, axis=0, keepdims=True)
        dw_ref[2:3] += jnp.sum(dgc * g, axis=0, keepdims=True)

    return pl.pallas_call(
        body,
        name=name,
        grid=(t // tm,),
        in_specs=[pl.BlockSpec((tm, f2), lambda i: (i, 0)), before, after,
                  pl.BlockSpec((tm, f), lambda i: (i, 0)), after_da, pl.BlockSpec((3, f), lambda i: (0, 0))],
        out_specs=[pl.BlockSpec((tm, f2), lambda i: (i, 0)), pl.BlockSpec((8, f), lambda i: (0, 0))],
        out_shape=[jax.ShapeDtypeStruct((t, f2), BF16), jax.ShapeDtypeStruct((8, f), F32)],
        compiler_params=_params(("arbitrary",)),
    )(gu, gu, gu, da, da, w)


def _swap_halves(xt):
    half = HEAD_DIM // 2
    return jnp.concatenate([xt[half:], xt[:half]], axis=0)


def _rope(xt, cos, sin):
    return xt * cos + _swap_halves(xt) * sin


def _unrope(dxt, cos, sin):
    return dxt * cos - _swap_halves(dxt) * sin


def _key_query(count):
    kj = lax.broadcasted_iota(jnp.int32, (WINDOW, count * WINDOW), 0)
    qi = lax.broadcasted_iota(jnp.int32, (WINDOW, count * WINDOW), 1) & (WINDOW - 1)
    return kj, qi


def _band_masks(n, count):
    kj, qi = _key_query(count)
    return kj <= qi, jnp.logical_and(kj > qi, n > 0)


def _lanes(v, count):
    return jnp.concatenate([v] * count, axis=1) if count > 1 else v


def _heads(ref, h0, count):
    parts = [ref[(h0 + g) * HEAD_DIM:(h0 + g + 1) * HEAD_DIM, :] for g in range(count)]
    return jnp.concatenate(parts, axis=1) if count > 1 else parts[0]


def _head_rows(ref, h0, count):
    parts = [ref[h0 + g:h0 + g + 1, :] for g in range(count)]
    return jnp.concatenate(parts, axis=1) if count > 1 else parts[0]


def _head_sinks(sink_ref, h0, count):
    parts = [jnp.full((1, WINDOW), sink_ref[h0 + g], F32) for g in range(count)]
    return jnp.concatenate(parts, axis=1) if count > 1 else parts[0]


def _tn(a, b):
    return lax.dot_general(a, b, _DIMS["tn"], preferred_element_type=F32)


def _nt(a, b):
    return lax.dot_general(a, b, _DIMS["nt"], preferred_element_type=F32)


def _nn(a, b):
    return jnp.dot(a, b, preferred_element_type=F32)


def _attn_fwd(qkv, sinks, cos_t, sin_t, bsz, seq, name, hp):
    t, qw = qkv.shape
    d = qw * 2 // 3
    kvw = d // GROUP
    n_heads, n_kv = d // HEAD_DIM, kvw // HEAD_DIM
    nb = seq // WINDOW
    scale = HEAD_DIM ** -0.5

    def body(sink_ref, xc_ref, xp_ref, cc_ref, sc_ref, cp_ref, sp_ref, o_ref, lse_ref, xt_ref, pt_ref, ot_ref):
        n = pl.program_id(1)
        xt_ref[...] = xc_ref[...].T
        pt_ref[...] = xp_ref[:, d:].T
        cos_c, sin_c, cos_p, sin_p = cc_ref[...], sc_ref[...], cp_ref[...], sp_ref[...]
        cos_g, sin_g = _lanes(cos_c, hp), _lanes(sin_c, hp)
        valid_c, valid_p = _band_masks(n, hp)
        for j in range(n_kv):
            ko = j * HEAD_DIM
            kc = _rope(xt_ref[d + ko:d + ko + HEAD_DIM, :], cos_c, sin_c).astype(BF16)
            kp = _rope(pt_ref[ko:ko + HEAD_DIM, :], cos_p, sin_p).astype(BF16)
            vc = xt_ref[d + kvw + ko:d + kvw + ko + HEAD_DIM, :].astype(BF16)
            vp = pt_ref[kvw + ko:kvw + ko + HEAD_DIM, :].astype(BF16)
            for h0 in range(j * GROUP, (j + 1) * GROUP, hp):
                q = _rope(_heads(xt_ref, h0, hp), cos_g, sin_g).astype(BF16)
                sink = _head_sinks(sink_ref, h0, hp)
                s_c = jnp.where(valid_c, _tn(kc, q) * scale, NEG)
                s_p = jnp.where(valid_p, _tn(kp, q) * scale, NEG)
                m = jnp.maximum(jnp.maximum(jnp.max(s_c, axis=0, keepdims=True), jnp.max(s_p, axis=0, keepdims=True)), sink)
                p_c = jnp.exp(s_c - m)
                p_p = jnp.exp(s_p - m)
                den = jnp.sum(p_c, axis=0, keepdims=True) + jnp.sum(p_p, axis=0, keepdims=True) + jnp.exp(sink - m)
                inv = 1.0 / den
                o_g = _nn(vc, (p_c * inv).astype(BF16)) + _nn(vp, (p_p * inv).astype(BF16))
                lse_g = m + jnp.log(den)
                for g in range(hp):
                    h = h0 + g
                    ot_ref[h * HEAD_DIM:(h + 1) * HEAD_DIM, :] = o_g[:, g * WINDOW:(g + 1) * WINDOW]
                    lse_ref[h:h + 1, :] = lse_g[:, g * WINDOW:(g + 1) * WINDOW]
        o_ref[...] = ot_ref[...].T.astype(BF16)

    cur = lambda b, n: (b * nb + n, 0)
    prev = lambda b, n: (b * nb + jnp.maximum(n - 1, 0), 0)
    tab_c = pl.BlockSpec((HEAD_DIM, WINDOW), lambda b, n: (0, n))
    tab_p = pl.BlockSpec((HEAD_DIM, WINDOW), lambda b, n: (0, jnp.maximum(n - 1, 0)))
    return pl.pallas_call(
        body,
        name=name,
        grid=(bsz, nb),
        in_specs=[SMEM, pl.BlockSpec((WINDOW, qw), cur), pl.BlockSpec((WINDOW, qw), prev), tab_c, tab_c, tab_p, tab_p],
        out_specs=[pl.BlockSpec((WINDOW, d), cur), pl.BlockSpec((n_heads, WINDOW), lambda b, n: (0, b * nb + n))],
        out_shape=[jax.ShapeDtypeStruct((t, d), BF16), jax.ShapeDtypeStruct((n_heads, t), F32)],
        scratch_shapes=[pltpu.VMEM((qw, WINDOW), F32), pltpu.VMEM((2 * kvw, WINDOW), F32), pltpu.VMEM((d, WINDOW), F32)],
        compiler_params=_params(("parallel", "arbitrary")),
    )(sinks, qkv, qkv, cos_t, sin_t, cos_t, sin_t)


def _attn_bwd(qkv, o, lse, do, sinks, cos_t, sin_t, bsz, seq, name, hp):
    t, qw = qkv.shape
    d = qw * 2 // 3
    kvw = d // GROUP
    n_heads, n_kv = d // HEAD_DIM, kvw // HEAD_DIM
    nb = seq // WINDOW
    scale = HEAD_DIM ** -0.5

    def body(sink_ref, xc_ref, xp_ref, xn_ref, oc_ref, on_ref, doc_ref, don_ref, lc_ref, ln_ref,
             cc_ref, sc_ref, cp_ref, sp_ref, cn_ref, sn_ref,
             dx_ref, db_ref, dsk_ref, xt_ref, pt_ref, qn_ref, otc_ref, otn_ref, dtc_ref, dtn_ref, gt_ref):
        b, n = pl.program_id(0), pl.program_id(1)
        xt_ref[...] = xc_ref[...].T
        pt_ref[...] = xp_ref[:, d:].T
        qn_ref[...] = xn_ref[:, :d].T
        otc_ref[...] = oc_ref[...].astype(F32).T
        otn_ref[...] = on_ref[...].astype(F32).T
        dtc_ref[...] = doc_ref[...].T
        dtn_ref[...] = don_ref[...].T
        cos_c, sin_c, cos_p, sin_p, cos_n, sin_n = (cc_ref[...], sc_ref[...], cp_ref[...], sp_ref[...],
                                                    cn_ref[...], sn_ref[...])
        cos_g, sin_g, cos_gn, sin_gn = _lanes(cos_c, hp), _lanes(sin_c, hp), _lanes(cos_n, hp), _lanes(sin_n, hp)
        valid_c, valid_p = _band_masks(n, hp)
        kj, qi = _key_query(hp)
        valid_n = jnp.logical_and(kj > qi, n < nb - 1)

        @pl.when(jnp.logical_and(b == 0, n == 0))
        def _():
            db_ref[...] = jnp.zeros_like(db_ref)
            dsk_ref[...] = jnp.zeros_like(dsk_ref)

        for j in range(n_kv):
            ko = j * HEAD_DIM
            kc = _rope(xt_ref[d + ko:d + ko + HEAD_DIM, :], cos_c, sin_c).astype(BF16)
            kp = _rope(pt_ref[ko:ko + HEAD_DIM, :], cos_p, sin_p).astype(BF16)
            vc = xt_ref[d + kvw + ko:d + kvw + ko + HEAD_DIM, :].astype(BF16)
            vp = pt_ref[kvw + ko:kvw + ko + HEAD_DIM, :].astype(BF16)
            dk = jnp.zeros((HEAD_DIM, WINDOW), F32)
            dv = jnp.zeros((HEAD_DIM, WINDOW), F32)
            for h0 in range(j * GROUP, (j + 1) * GROUP, hp):
                q = _rope(_heads(xt_ref, h0, hp), cos_g, sin_g).astype(BF16)
                do_g = _heads(dtc_ref, h0, hp)
                do_b = do_g.astype(BF16)
                lse_g = _head_rows(lc_ref, h0, hp)
                delta = jnp.sum(_heads(otc_ref, h0, hp) * do_g, axis=0, keepdims=True)
                p_c = jnp.exp(jnp.where(valid_c, _tn(kc, q) * scale, NEG) - lse_g)
                p_p = jnp.exp(jnp.where(valid_p, _tn(kp, q) * scale, NEG) - lse_g)
                ds_c = (p_c * (_tn(vc, do_b) - delta)).astype(BF16)
                ds_p = (p_p * (_tn(vp, do_b) - delta)).astype(BF16)
                dq = _unrope((_nn(kc, ds_c) + _nn(kp, ds_p)) * scale, cos_g, sin_g)
                dsk = -jnp.exp(_head_sinks(sink_ref, h0, hp) - lse_g) * delta
                for g in range(hp):
                    h = h0 + g
                    gt_ref[h * HEAD_DIM:(h + 1) * HEAD_DIM, :] = dq[:, g * WINDOW:(g + 1) * WINDOW]
                    dsk_ref[h:h + 1, :] += dsk[:, g * WINDOW:(g + 1) * WINDOW]
                q2 = _rope(_heads(qn_ref, h0, hp), cos_gn, sin_gn).astype(BF16)
                do2 = _heads(dtn_ref, h0, hp)
                do2_b = do2.astype(BF16)
                delta2 = jnp.sum(_heads(otn_ref, h0, hp) * do2, axis=0, keepdims=True)
                p_n = jnp.exp(jnp.where(valid_n, _tn(kc, q2) * scale, NEG) - _head_rows(ln_ref, h0, hp))
                ds_n = (p_n * (_tn(vc, do2_b) - delta2)).astype(BF16)
                dv += _nt(do_b, p_c.astype(BF16)) + _nt(do2_b, p_n.astype(BF16))
                dk += _nt(q, ds_c) + _nt(q2, ds_n)
            gt_ref[d + ko:d + ko + HEAD_DIM, :] = _unrope(dk * scale, cos_c, sin_c)
            gt_ref[d + kvw + ko:d + kvw + ko + HEAD_DIM, :] = dv
        dx = gt_ref[...].T
        dx_ref[...] = dx.astype(BF16)
        db_ref[...] += _fold8(dx)

    cur = lambda b, n: (b * nb + n, 0)
    prev = lambda b, n: (b * nb + jnp.maximum(n - 1, 0), 0)
    nxt = lambda b, n: (b * nb + jnp.minimum(n + 1, nb - 1), 0)
    stat_c = pl.BlockSpec((n_heads, WINDOW), lambda b, n: (0, b * nb + n))
    stat_n = pl.BlockSpec((n_heads, WINDOW), lambda b, n: (0, b * nb + jnp.minimum(n + 1, nb - 1)))
    tab_c = pl.BlockSpec((HEAD_DIM, WINDOW), lambda b, n: (0, n))
    tab_p = pl.BlockSpec((HEAD_DIM, WINDOW), lambda b, n: (0, jnp.maximum(n - 1, 0)))
    tab_n = pl.BlockSpec((HEAD_DIM, WINDOW), lambda b, n: (0, jnp.minimum(n + 1, nb - 1)))
    return pl.pallas_call(
        body,
        name=name,
        grid=(bsz, nb),
        in_specs=[SMEM, pl.BlockSpec((WINDOW, qw), cur), pl.BlockSpec((WINDOW, qw), prev), pl.BlockSpec((WINDOW, qw), nxt),
                  pl.BlockSpec((WINDOW, d), cur), pl.BlockSpec((WINDOW, d), nxt),
                  pl.BlockSpec((WINDOW, d), cur), pl.BlockSpec((WINDOW, d), nxt),
                  stat_c, stat_n, tab_c, tab_c, tab_p, tab_p, tab_n, tab_n],
        out_specs=[pl.BlockSpec((WINDOW, qw), cur), pl.BlockSpec((8, qw), lambda b, n: (0, 0)),
                   pl.BlockSpec((n_heads, WINDOW), lambda b, n: (0, 0))],
        out_shape=[jax.ShapeDtypeStruct((t, qw), BF16), jax.ShapeDtypeStruct((8, qw), F32),
                   jax.ShapeDtypeStruct((n_heads, WINDOW), F32)],
        scratch_shapes=[pltpu.VMEM((qw, WINDOW), F32), pltpu.VMEM((2 * kvw, WINDOW), F32), pltpu.VMEM((d, WINDOW), F32),
                        pltpu.VMEM((d, WINDOW), F32), pltpu.VMEM((d, WINDOW), F32), pltpu.VMEM((d, WINDOW), F32),
                        pltpu.VMEM((d, WINDOW), F32), pltpu.VMEM((qw, WINDOW), F32)],
        compiler_params=_params(("arbitrary", "arbitrary")),
    )(sinks, qkv, qkv, qkv, o, o, do, do, lse, lse, cos_t, sin_t, cos_t, sin_t, cos_t, sin_t)


def _place():
    return lax.axis_index("x"), lax.axis_index("y"), lax.axis_index("c")


def _other_chips(x, y):
    return [(1 - x, y), (x, 1 - y), (1 - x, 1 - y)]


def _place_shard(w, axis, q, name):
    ly, k, n = w.shape
    tr = _pick(k, (256, 128, 64, 32, 16, 8))
    steps = k // tr
    shape = (ly, k * N_CHIPS, n) if axis == 1 else (ly, k, n * N_CHIPS)
    if axis == 1:
        out_spec = pl.BlockSpec((None, tr, n), lambda l, i, q_ref: (l, q_ref[0] * steps + i, 0))
    else:
        out_spec = pl.BlockSpec((None, tr, n), lambda l, i, q_ref: (l, i, q_ref[0]))

    def body(q_ref, w_ref, o_ref):
        del q_ref
        o_ref[...] = w_ref[...].astype(BF16)

    return pl.pallas_call(
        body,
        name=name,
        grid_spec=pltpu.PrefetchScalarGridSpec(
            num_scalar_prefetch=1, grid=(ly, steps),
            in_specs=[pl.BlockSpec((None, tr, n), lambda l, i, q_ref: (l, i, 0))], out_specs=out_spec),
        out_shape=jax.ShapeDtypeStruct(shape, BF16),
        compiler_params=_params(("parallel", "parallel")),
    )(q, w)


def _half_block(ref, axis, layer, px, py, pc):
    blk = 2 * px + py
    if axis == 1:
        rows = ref.shape[1] // (2 * N_CHIPS)
        return ref.at[layer, pl.ds(pl.multiple_of((2 * blk + pc) * rows, 8), rows), :]
    rows, width = ref.shape[1] // 2, ref.shape[2] // N_CHIPS
    return ref.at[layer, pl.ds(pl.multiple_of(pc * rows, 8), rows), pl.ds(pl.multiple_of(blk * width, 128), width)]


def _gather_copy(refs, axes, pieces, send_sems, recv_sems, p, k, stage, whose):
    x, y, c = _place()
    chip = _other_chips(x, y)[k]
    i, layer = pieces[p]
    if stage == 0:
        origin = (x, y, c) if whose == "mine" else (*chip, c)
        to = (*chip, c)
    else:
        origin = (*chip, c) if whose == "mine" else (*chip, 1 - c)
        to = (x, y, 1 - c)
    blk = _half_block(refs[i], axes[i], layer, *origin)
    return pltpu.make_async_remote_copy(src_ref=blk, dst_ref=blk, send_sem=send_sems.at[p * 3 + k],
                                        recv_sem=recv_sems.at[p * 3 + k], device_id=to, device_id_type=MESH)


def _gather_weights(fulls, axes, pieces, name):
    n, m = len(fulls), 3 * len(pieces)

    def body(*refs):
        dst = refs[n:2 * n]
        sems = refs[2 * n:]
        todo = [(p, k) for p in range(len(pieces)) for k in range(3)]
        sends = [_gather_copy(dst, axes, pieces, sems[0], sems[1], p, k, 0, "mine") for p, k in todo]
        for cp in sends:
            cp.start()
        for p, k in todo:
            _gather_copy(dst, axes, pieces, sems[0], sems[1], p, k, 0, "theirs").wait_recv()
            sends.append(_gather_copy(dst, axes, pieces, sems[2], sems[3], p, k, 1, "mine"))
            sends[-1].start()
        for p, k in todo:
            _gather_copy(dst, axes, pieces, sems[2], sems[3], p, k, 1, "theirs").wait_recv()
        for cp in sends:
            cp.wait_send()

    return pl.pallas_call(
        body,
        name=name,
        in_specs=[ANY] * n,
        out_specs=[ANY] * n,
        out_shape=[jax.ShapeDtypeStruct(f.shape, f.dtype) for f in fulls],
        input_output_aliases={i: i for i in range(n)},
        scratch_shapes=[pltpu.SemaphoreType.DMA((m,))] * 4,
    )(*fulls)


HBM_SPEC = pl.BlockSpec(memory_space=pltpu.HBM)
SEM_SPEC = pl.BlockSpec(memory_space=pltpu.SEMAPHORE)


def _gather_start(fulls, axes, pieces, stage, name):
    n, m = len(fulls), 3 * len(pieces)

    def body(*refs):
        src = refs[:n]
        send_sems, recv_sems = refs[2 * n], refs[2 * n + 1]
        for p in range(len(pieces)):
            for k in range(3):
                _gather_copy(src, axes, pieces, send_sems, recv_sems, p, k, stage, "mine").start()

    out = pl.pallas_call(
        body,
        name=name,
        in_specs=[HBM_SPEC] * n,
        out_specs=[HBM_SPEC] * n + [SEM_SPEC, SEM_SPEC],
        out_shape=[pltpu.HBM(f.shape, f.dtype) for f in fulls] + [pltpu.SemaphoreType.DMA((m,)), pltpu.SemaphoreType.DMA((m,))],
        input_output_aliases={i: i for i in range(n)},
        compiler_params=pltpu.CompilerParams(has_side_effects=pltpu.SideEffectType.DATAFLOW_SIDE_EFFECTING),
    )(*[pltpu.with_memory_space_constraint(f, pltpu.HBM) for f in fulls])
    return list(out[:n]), out[n], out[n + 1]


def _gather_wait(fulls, send_sems, recv_sems, after, axes, pieces, stage, name):
    n = len(fulls)

    def body(*refs):
        src = refs[:n]
        s_sems, r_sems = refs[n], refs[n + 1]
        for p in range(len(pieces)):
            for k in range(3):
                _gather_copy(src, axes, pieces, s_sems, r_sems, p, k, stage, "mine").wait_send()
                _gather_copy(src, axes, pieces, s_sems, r_sems, p, k, stage, "theirs").wait_recv()

    out = pl.pallas_call(
        body,
        name=name,
        in_specs=[HBM_SPEC] * n + [SEM_SPEC, SEM_SPEC, ANY],
        out_specs=[HBM_SPEC] * n,
        out_shape=[pltpu.HBM(f.shape, f.dtype) for f in fulls],
        input_output_aliases={i: i for i in range(n)},
        compiler_params=pltpu.CompilerParams(has_side_effects=pltpu.SideEffectType.DATAFLOW_SIDE_EFFECTING),
    )(*fulls, send_sems, recv_sems, after)
    return list(out)


def _half_shape(kind, shape):
    if kind == "col":
        return (shape[0] // 2, shape[1])
    return (N_CHIPS, shape[1] // 2, shape[2])


def _half_of(kind, ref, h):
    if kind == "col":
        r = ref.shape[0] // 2
        return ref.at[pl.ds(pl.multiple_of(h * r, 8), r), :]
    r = ref.shape[1] // 2
    return ref.at[:, pl.ds(pl.multiple_of(h * r, 8), r), :]


def _slice_of_half(kind, ref, s):
    if kind == "col":
        w = ref.shape[1] // N_CHIPS
        return ref.at[:, pl.ds(pl.multiple_of(s * w, 128), w)]
    return ref.at[s]


def _pair_exchange(grads, kinds, name):
    n = len(grads)
    outs = [jax.ShapeDtypeStruct(_half_shape(kd, g.shape), g.dtype) for g, kd in zip(grads, kinds)]

    def body(*refs):
        src, dst = refs[:n], refs[n:2 * n]
        send_sems, recv_sems = refs[2 * n:]
        x, y, c = _place()
        copies = [pltpu.make_async_remote_copy(
            src_ref=_half_of(kinds[i], src[i], 1 - c), dst_ref=dst[i],
            send_sem=send_sems.at[i], recv_sem=recv_sems.at[i], device_id=(x, y, 1 - c), device_id_type=MESH)
            for i in range(n)]
        for cp in copies:
            cp.start()
        for cp in copies:
            cp.wait_recv()
        for cp in copies:
            cp.wait_send()

    return pl.pallas_call(
        body,
        name=name,
        in_specs=[ANY] * n,
        out_specs=[ANY] * n,
        out_shape=outs,
        scratch_shapes=[pltpu.SemaphoreType.DMA((n,)), pltpu.SemaphoreType.DMA((n,))],
    )(*grads)


def _pair_sum(grad, recv, kind, c, name):
    if kind == "col":
        k, n = grad.shape
        rows = k // 2
        tr = _pick(rows, (256, 128, 64, 32, 16, 8))
        steps = rows // tr
        grid = (steps,)
        g_spec = pl.BlockSpec((tr, n), lambda i, c_ref: (c_ref[0] * steps + i, 0))
        r_spec = pl.BlockSpec((tr, n), lambda i, c_ref: (i, 0))
        g_in = grad
    else:
        _, k4, n = grad.shape
        r8 = k4 // 2
        grid = (N_CHIPS,)
        g_spec = pl.BlockSpec((None, None, r8, n), lambda s, c_ref: (s, c_ref[0], 0, 0))
        r_spec = pl.BlockSpec((None, r8, n), lambda s, c_ref: (s, 0, 0))
        g_in = grad.reshape(N_CHIPS, 2, r8, n)

    def body(c_ref, g_ref, r_ref, o_ref):
        del c_ref
        o_ref[...] = (g_ref[...].astype(F32) + r_ref[...].astype(F32)).astype(o_ref.dtype)

    return pl.pallas_call(
        body,
        name=name,
        grid_spec=pltpu.PrefetchScalarGridSpec(num_scalar_prefetch=1, grid=grid, in_specs=[g_spec, r_spec], out_specs=r_spec),
        out_shape=jax.ShapeDtypeStruct(recv.shape, recv.dtype),
        compiler_params=_params(("parallel",)),
    )(c, g_in, recv)


def _chip_scatter(sums, kinds, name):
    flat = [(i, l, s) for i, ss in enumerate(sums) for l, s in enumerate(ss)]
    n, n_t = len(flat), len(sums)
    outs = []
    for ss, kd in zip(sums, kinds):
        s = ss[0]
        shp = (s.shape[0], s.shape[1] // N_CHIPS) if kd == "col" else s.shape[1:]
        outs += [jax.ShapeDtypeStruct((len(ss), N_CHIPS) + tuple(shp), s.dtype)] * 2
    per = 7

    def body(*refs):
        src = refs[:n]
        mine = [refs[n + 2 * i] for i in range(n_t)]
        sib = [refs[n + 2 * i + 1] for i in range(n_t)]
        send_sems, recv_sems, local_sems = refs[n + 2 * n_t:]
        x, y, c = _place()
        q = 2 * x + y
        chips = _other_chips(x, y)
        sibling = (x, y, 1 - c)

        def copy(j, m, src_ref, dst_ref, to):
            return pltpu.make_async_remote_copy(
                src_ref=src_ref, dst_ref=dst_ref, send_sem=send_sems.at[j * per + m], recv_sem=recv_sems.at[j * per + m],
                device_id=to, device_id_type=MESH)

        local, sends = [], []
        for j, (i, l, _) in enumerate(flat):
            own = _slice_of_half(kinds[i], src[j], q)
            local.append(pltpu.make_async_copy(own, mine[i].at[l, q], local_sems.at[j]))
            sends.append(copy(j, 3, own, sib[i].at[l, q], sibling))
            for k, chip in enumerate(chips):
                sends.append(copy(j, k, _slice_of_half(kinds[i], src[j], 2 * chip[0] + chip[1]), mine[i].at[l, q], (*chip, c)))
        for cp in local + sends:
            cp.start()
        for j, (i, l, _) in enumerate(flat):
            for k, chip in enumerate(chips):
                slot = 2 * chip[0] + chip[1]
                copy(j, k, mine[i].at[l, slot], mine[i].at[l, slot], (*chip, c)).wait_recv()
                sends.append(copy(j, 4 + k, mine[i].at[l, slot], sib[i].at[l, slot], sibling))
                sends[-1].start()
        for j, (i, l, _) in enumerate(flat):
            copy(j, 3, sib[i].at[l, q], sib[i].at[l, q], sibling).wait_recv()
            for k, chip in enumerate(chips):
                slot = 2 * chip[0] + chip[1]
                copy(j, 4 + k, sib[i].at[l, slot], sib[i].at[l, slot], sibling).wait_recv()
        for cp in sends:
            cp.wait_send()
        for cp in local:
            cp.wait()

    return pl.pallas_call(
        body,
        name=name,
        in_specs=[ANY] * n,
        out_specs=[ANY] * (2 * n_t),
        out_shape=outs,
        scratch_shapes=[pltpu.SemaphoreType.DMA((per * n,)), pltpu.SemaphoreType.DMA((per * n,)), pltpu.SemaphoreType.DMA((n,))],
    )(*[s for _, _, s in flat])


def _reduce_adamw(mine, sib, w, m, v, c, name):
    ly, _, r, cols = mine.shape
    tr = _pick(r, (128, 64, 32, 16, 8))
    steps = r // tr
    c1 = 1.0 - ADAM_B1 ** ADAM_STEP
    c2 = 1.0 - ADAM_B2 ** ADAM_STEP

    def body(c_ref, mine_ref, sib_ref, w_ref, m_ref, v_ref, g_ref, d_ref, nm_ref, nv_ref):
        def total(ref):
            acc = ref[0].astype(F32)
            for s in range(1, N_CHIPS):
                acc = acc + ref[s].astype(F32)
            return acc

        gv = jnp.where(pl.program_id(1) == c_ref[0], total(mine_ref), total(sib_ref))
        nm = ADAM_B1 * m_ref[...] + (1.0 - ADAM_B1) * gv
        nv = ADAM_B2 * v_ref[...] + (1.0 - ADAM_B2) * (gv * gv)
        g_ref[...] = gv
        d_ref[...] = -ADAM_LR * ((nm / c1) / (jnp.sqrt(nv / c2) + ADAM_EPS) + ADAM_WD * w_ref[...])
        nm_ref[...] = nm
        nv_ref[...] = nv

    slot_spec = pl.BlockSpec((None, N_CHIPS, tr, cols), lambda l, h, i, c_ref: (l, 0, i, 0))
    spec = pl.BlockSpec((None, tr, cols), lambda l, h, i, c_ref: (l, h * steps + i, 0))
    shp = jax.ShapeDtypeStruct(w.shape, F32)
    return pl.pallas_call(
        body,
        name=name,
        grid_spec=pltpu.PrefetchScalarGridSpec(
            num_scalar_prefetch=1, grid=(ly, N_CORES, steps),
            in_specs=[slot_spec, slot_spec, spec, spec, spec], out_specs=[spec] * 4),
        out_shape=[shp] * 4,
        compiler_params=_params(("parallel", "parallel", "parallel")),
    )(c, mine, sib, w, m, v)


def _allreduce_small(v, name):
    r, w = v.shape

    def body(v_ref, o_ref, buf_ref, send_sems, recv_sems):
        x, y, c = _place()
        me = 4 * x + 2 * y + c

        def peer(k):
            return x ^ (k >> 2), y ^ ((k >> 1) & 1), c ^ (k & 1)

        def remote(k, slot):
            return pltpu.make_async_remote_copy(
                src_ref=v_ref, dst_ref=buf_ref.at[slot], send_sem=send_sems.at[k - 1], recv_sem=recv_sems.at[k - 1],
                device_id=peer(k), device_id_type=MESH)

        sends = [remote(k, me) for k in range(1, N_DEV)]
        for cp in sends:
            cp.start()
        buf_ref[me] = v_ref[...]
        for k in range(1, N_DEV):
            px, py, pc = peer(k)
            remote(k, 4 * px + 2 * py + pc).wait_recv()
        for cp in sends:
            cp.wait_send()
        acc = buf_ref[0]
        for dev in range(1, N_DEV):
            acc = acc + buf_ref[dev]
        o_ref[...] = acc

    vm = pl.BlockSpec(memory_space=pltpu.VMEM)
    return pl.pallas_call(
        body,
        name=name,
        in_specs=[vm],
        out_specs=vm,
        out_shape=jax.ShapeDtypeStruct((r, w), F32),
        scratch_shapes=[pltpu.VMEM((N_DEV, r, w), F32), pltpu.SemaphoreType.DMA((N_DEV - 1,)), pltpu.SemaphoreType.DMA((N_DEV - 1,))],
        compiler_params=pltpu.CompilerParams(vmem_limit_bytes=VMEM_LIMIT_BYTES),
    )(v)


def _adamw(w, g, m, v, name):
    ly, r, c = w.shape
    tr = _pick(r, (256, 128, 64, 32, 16, 8))
    c1 = 1.0 - ADAM_B1 ** ADAM_STEP
    c2 = 1.0 - ADAM_B2 ** ADAM_STEP

    def body(w_ref, g_ref, m_ref, v_ref, d_ref, nm_ref, nv_ref):
        gv = g_ref[...]
        nm = ADAM_B1 * m_ref[...] + (1.0 - ADAM_B1) * gv
        nv = ADAM_B2 * v_ref[...] + (1.0 - ADAM_B2) * (gv * gv)
        d_ref[...] = -ADAM_LR * ((nm / c1) / (jnp.sqrt(nv / c2) + ADAM_EPS) + ADAM_WD * w_ref[...])
        nm_ref[...] = nm
        nv_ref[...] = nv

    spec = pl.BlockSpec((None, tr, c), lambda l, i: (l, i, 0))
    shp = jax.ShapeDtypeStruct((ly, r, c), F32)
    return pl.pallas_call(
        body,
        name=name,
        grid=(ly, r // tr),
        in_specs=[spec] * 4,
        out_specs=[spec] * 3,
        out_shape=[shp] * 3,
        compiler_params=_params(("parallel", "parallel")),
    )(w, g, m, v)


def _rope_tables(seq):
    pos = jnp.arange(seq, dtype=F32)
    inv_freq = 1.0 / (ROPE_THETA ** (jnp.arange(0, HEAD_DIM, 2, dtype=F32) / HEAD_DIM))
    ang = (pos[:, None] * inv_freq[None, :]).T
    cos, sin = jnp.cos(ang), jnp.sin(ang)
    return jnp.concatenate([cos, cos], axis=0), jnp.concatenate([-sin, sin], axis=0)


def _pack(vs, fill=0.0):
    p = jnp.concatenate([v.reshape(-1) for v in vs])
    size = -(-p.shape[0] // 8192) * 8192
    return jnp.pad(p, (0, size - p.shape[0]), constant_values=fill).reshape(-1, 1024)


def _unpack(p, like):
    p = p.reshape(-1)
    out, o = [], 0
    for v in like:
        n = int(math.prod(v.shape))
        out.append(p[o:o + n].reshape(v.shape))
        o += n
    return out


def kernel(x, norm_mix, norm_ffn, norm_final, conv_w_in, conv_w_conv, conv_w_out, attn_w_qkv, attn_b_qkv, attn_sinks, attn_w_o, attn_b_o, ffn_w_in, ffn_w_conv, ffn_w_down, loss_target, m_norm_mix, m_norm_ffn, m_norm_final, m_conv_w_in, m_conv_w_conv, m_conv_w_out, m_attn_w_qkv, m_attn_b_qkv, m_attn_sinks, m_attn_w_o, m_attn_b_o, m_ffn_w_in, m_ffn_w_conv, m_ffn_w_down, v_norm_mix, v_norm_ffn, v_norm_final, v_conv_w_in, v_conv_w_conv, v_conv_w_out, v_attn_w_qkv, v_attn_b_qkv, v_attn_sinks, v_attn_w_o, v_attn_b_o, v_ffn_w_in, v_ffn_w_conv, v_ffn_w_down):
    bsz, seq, d = x.shape
    t = bsz * seq
    depth = norm_mix.shape[0]
    n_conv, n_attn = conv_w_in.shape[0], attn_w_qkv.shape[0]
    xq, yq, cq = _place()
    q = 2 * xq + yq

    big = [conv_w_in, conv_w_out, attn_w_qkv, attn_w_o, ffn_w_in, ffn_w_down]
    axes = [2, 1, 2, 1, 2, 1]
    q_arr = q.astype(jnp.int32).reshape(1)
    c_arr = cq.astype(jnp.int32).reshape(1)
    weights = [_place_shard(w, ax, q_arr, f"place_shard{n}") for n, (w, ax) in enumerate(zip(big, axes))]

    def pieces_of(i):
        return [(0, i // 2), (1, i // 2), (4, i), (5, i)] if i % 2 == 0 else [(2, i // 2), (3, i // 2), (4, i), (5, i)]

    weights = _gather_weights(weights, axes, pieces_of(0), "gather_layer0")

    small_cols = [conv_w_conv, attn_b_qkv, attn_b_o, ffn_w_conv]

    def placed(v):
        width = v.shape[-1]
        full = jnp.zeros(v.shape[:-1] + (N_CHIPS * width,), F32)
        return lax.dynamic_update_slice_in_dim(full, v * (1.0 / N_CORES), q * width, axis=v.ndim - 1)

    full_cols = [placed(v) for v in small_cols]
    wc_conv, b_qkv, b_o, wf_conv = _unpack(_allreduce_small(_pack(full_cols), "gather_small"), full_cols)
    cos_t, sin_t = _rope_tables(seq)

    xs = x.reshape(t, d)
    saved = []
    for i in range(depth):
        j = i // 2
        ahead = pieces_of(i + 1) if i + 1 < depth else None
        if ahead:
            weights, s_sems, r_sems = _gather_start(weights, axes, ahead, 0, f"gather_ici_start{i + 1}")
        w_cin, w_cout, w_qkv, w_o, w_fin, w_fdown = weights
        h = _rms_fwd(xs, norm_mix[i:i + 1], f"norm_mix_fwd{i}")
        if i % 2 == 0:
            pre = _mm(h, w_cin, "nn", BF16, layer=j, tm=1024, tn=768, tk=4096, name=f"conv_in_fwd{i}")
            mixed = _convgate_fwd(pre, wc_conv[j], seq, f"conv_gate_fwd{i}")
            x_mid = _mm(mixed, w_cout, "nn", F32, layer=j, residual=xs, tm=512, tn=1024, tk=4096, name=f"conv_out_fwd{i}")
            lse = None
        else:
            pre = _mm(h, w_qkv, "nn", F32, layer=j, bias=b_qkv[j:j + 1], tm=1024, tn=768, tk=4096, name=f"qkv_fwd{i}")
            mixed, lse = _attn_fwd(pre, attn_sinks[j], cos_t, sin_t, bsz, seq, f"attn_fwd{i}", hp=1)
            x_mid = _mm(mixed, w_o, "nn", F32, layer=j, bias=b_o[j:j + 1], residual=xs, tm=512, tn=1024, tk=4096,
                        name=f"attn_out_fwd{i}")
        if ahead:
            weights = _gather_wait(weights, s_sems, r_sems, x_mid, axes, ahead, 0, f"gather_ici_wait{i + 1}")
            weights, s_sems, r_sems = _gather_start(weights, axes, ahead, 1, f"gather_pass_start{i + 1}")
        w_cin, w_cout, w_qkv, w_o, w_fin, w_fdown = weights
        h2 = _rms_fwd(x_mid, norm_ffn[i:i + 1], f"norm_ffn_fwd{i}")
        gu = _mm(h2, w_fin, "nn", BF16, layer=i, n_outer=True, tm=512, tn=1408, tk=4096, name=f"ffn_in_fwd{i}")
        act = _ffngate_fwd(gu, wf_conv[i], seq, f"ffn_gate_fwd{i}")
        x_next = _mm(act, w_fdown, "nn", F32, layer=i, residual=x_mid, tm=512, tn=1024, tk=4096, name=f"ffn_down_fwd{i}")
        if ahead:
            weights = _gather_wait(weights, s_sems, r_sems, x_next, axes, ahead, 1, f"gather_pass_wait{i + 1}")
        saved.append((xs, h, pre, mixed, lse, x_mid, h2, gu, act))
        xs = x_next
    w_cin, w_cout, w_qkv, w_o, w_fin, w_fdown = weights

    dx, dxb, sq, dg_final = _loss_head(xs, loss_target.reshape(t, d), norm_final.reshape(1, d), "loss_head")
    loss = lax.psum(0.5 * jnp.sum(sq) / d, ("x", "y", "c"))

    g_norm_mix, g_norm_ffn = [None] * depth, [None] * depth
    g_cin, g_cconv, g_cout = [None] * n_conv, [None] * n_conv, [None] * n_conv
    g_qkv, g_bqkv, g_sinks, g_o, g_bo = ([None] * n_attn for _ in range(5))
    g_fin, g_fconv, g_fdown = [None] * depth, [None] * depth, [None] * depth
    for i in reversed(range(depth)):
        j = i // 2
        x_in, h, pre, mixed, lse, x_mid, h2, gu, act = saved[i]
        da = _mm(dxb, w_fdown, "nt", BF16, layer=i, n_outer=True, tm=512, tn=1408, tk=4096, name=f"ffn_down_dx{i}")
        g_fdown[i] = _mm(act, dxb, "tn", BF16, tm=1408, tn=1024, tk=2048, name=f"ffn_down_dw{i}")
        dgu, dwc = _ffngate_bwd(gu, da, wf_conv[i], seq, f"ffn_gate_bwd{i}")
        g_fconv[i] = dwc[:3]
        g_fin[i] = _mm(h2, dgu, "tn", BF16, tm=1024, tn=1408, tk=2048, name=f"ffn_in_dw{i}")
        dh2 = _mm(dgu, w_fin, "nt", F32, layer=i, tm=512, tn=1024, tk=8192, name=f"ffn_in_dx{i}")
        dx, dxb, dg, colsum = _rms_bwd(x_mid, dh2, norm_ffn[i:i + 1], dx, f"norm_ffn_bwd{i}")
        g_norm_ffn[i] = jnp.sum(dg, axis=0)
        if i % 2 == 0:
            dmix = _mm(dxb, w_cout, "nt", BF16, layer=j, tm=512, tn=1024, tk=4096, name=f"conv_out_dx{i}")
            g_cout[j] = _mm(mixed, dxb, "tn", BF16, tm=1024, tn=1024, tk=2048, name=f"conv_out_dw{i}")
            dpre, dwc = _convgate_bwd(pre, dmix, wc_conv[j], seq, f"conv_gate_bwd{i}")
            g_cconv[j] = dwc[:3]
            g_cin[j] = _mm(h, dpre, "tn", BF16, tm=1024, tn=1536, tk=2048, name=f"conv_in_dw{i}")
            dh = _mm(dpre, w_cin, "nt", F32, layer=j, tm=512, tn=1024, tk=8192, name=f"conv_in_dx{i}")
        else:
            g_bo[j] = jnp.sum(colsum, axis=0)
            dmix = _mm(dxb, w_o, "nt", F32, layer=j, tm=512, tn=1024, tk=4096, name=f"attn_out_dx{i}")
            g_o[j] = _mm(mixed, dxb, "tn", BF16, tm=1024, tn=1024, tk=2048, name=f"attn_out_dw{i}")
            dpre, dbias, dsk = _attn_bwd(pre, mixed, lse, dmix, attn_sinks[j], cos_t, sin_t, bsz, seq, f"attn_bwd{i}",
                                         hp=GROUP)
            g_bqkv[j] = jnp.sum(dbias, axis=0)
            g_sinks[j] = jnp.sum(dsk, axis=1)
            g_qkv[j] = _mm(h, dpre, "tn", BF16, tm=1024, tn=1536, tk=2048, name=f"qkv_dw{i}")
            dh = _mm(dpre, w_qkv, "nt", F32, layer=j, tm=512, tn=1024, tk=1536, name=f"qkv_dx{i}")
        dx, dxb, dg, _ = _rms_bwd(x_in, dh, norm_mix[i:i + 1], dx, f"norm_mix_bwd{i}")
        g_norm_mix[i] = jnp.sum(dg, axis=0)
    grad_x = dx.reshape(bsz, seq, d)

    tensors = [(g_cin, "col"), (g_cout, "row"), (g_qkv, "col"), (g_o, "row"), (g_fin, "col"), (g_fdown, "row")]
    t_kinds = [kd for _, kd in tensors]
    flat, kinds = [], []
    for gs, kd in tensors:
        for g in gs:
            flat.append(g if kd == "col" else g.reshape(N_CHIPS, g.shape[0] // N_CHIPS, g.shape[1]))
            kinds.append(kd)
    recv = _pair_exchange(flat, kinds, "grad_pair_exchange")
    sums = [_pair_sum(g, r, kd, c_arr, f"grad_pair_sum{n}") for n, (g, r, kd) in enumerate(zip(flat, recv, kinds))]
    by_tensor, pos = [], 0
    for gs, _ in tensors:
        by_tensor.append(sums[pos:pos + len(gs)])
        pos += len(gs)
    slots = _chip_scatter(by_tensor, t_kinds, "grad_chip_scatter")
    big_w = [conv_w_in, conv_w_out, attn_w_qkv, attn_w_o, ffn_w_in, ffn_w_down]
    big_m = [m_conv_w_in, m_conv_w_out, m_attn_w_qkv, m_attn_w_o, m_ffn_w_in, m_ffn_w_down]
    big_v = [v_conv_w_in, v_conv_w_out, v_attn_w_qkv, v_attn_w_o, v_ffn_w_in, v_ffn_w_down]
    big_names = ["conv_w_in", "conv_w_out", "attn_w_qkv", "attn_w_o", "ffn_w_in", "ffn_w_down"]
    big_upd = [_reduce_adamw(slots[2 * n], slots[2 * n + 1], big_w[n], big_m[n], big_v[n], c_arr, f"adamw_{nm}")
               for n, nm in enumerate(big_names)]

    small = [jnp.stack(g_norm_mix), jnp.stack(g_norm_ffn), jnp.sum(dg_final, axis=0), jnp.stack(g_cconv),
             jnp.stack(g_bqkv), jnp.stack(g_sinks), jnp.stack(g_bo), jnp.stack(g_fconv)]
    sg = _unpack(_allreduce_small(_pack(small), "grad_small_allreduce"), small)

    def my_cols(v, like):
        width = like.shape[-1]
        return lax.dynamic_slice_in_dim(v, q * width, width, axis=v.ndim - 1)

    small_w = [norm_mix, norm_ffn, norm_final, conv_w_conv, attn_b_qkv, attn_sinks, attn_b_o, ffn_w_conv]
    small_m = [m_norm_mix, m_norm_ffn, m_norm_final, m_conv_w_conv, m_attn_b_qkv, m_attn_sinks, m_attn_b_o, m_ffn_w_conv]
    small_v = [v_norm_mix, v_norm_ffn, v_norm_final, v_conv_w_conv, v_attn_b_qkv, v_attn_sinks, v_attn_b_o, v_ffn_w_conv]
    small_g = [sg[0], sg[1], sg[2], my_cols(sg[3], conv_w_conv), my_cols(sg[4], attn_b_qkv), sg[5],
               my_cols(sg[6], attn_b_o), my_cols(sg[7], ffn_w_conv)]

    upd = {nm: tuple(u[1:]) for nm, u in zip(big_names, big_upd)}
    sd, sm, sv = _adamw(_pack(small_w)[None], _pack(small_g)[None], _pack(small_m)[None], _pack(small_v, 1.0)[None],
                        "adamw_small")
    sd, sm, sv = _unpack(sd, small_w), _unpack(sm, small_w), _unpack(sv, small_w)
    names = ["norm_mix", "norm_ffn", "norm_final", "conv_w_in", "conv_w_conv", "conv_w_out", "attn_w_qkv", "attn_b_qkv",
             "attn_sinks", "attn_w_o", "attn_b_o", "ffn_w_in", "ffn_w_conv", "ffn_w_down"]
    small_names = ["norm_mix", "norm_ffn", "norm_final", "conv_w_conv", "attn_b_qkv", "attn_sinks", "attn_b_o", "ffn_w_conv"]
    grads = dict(zip(small_names, small_g))
    grads.update({nm: u[0] for nm, u in zip(big_names, big_upd)})
    for n, nm in enumerate(small_names):
        upd[nm] = (sd[n], sm[n], sv[n])
    return (loss, grad_x, *[grads[nm] for nm in names], *[upd[nm][0] for nm in names],
            *[upd[nm][1] for nm in names], *[upd[nm][2] for nm in names])
```

```python
import math

import jax
import jax.numpy as jnp
from jax import lax
from jax.experimental import pallas as pl
from jax.experimental.pallas import tpu as pltpu

F32 = jnp.float32
BF16 = jnp.bfloat16

HEAD_DIM = 64
GROUP = 4
WINDOW = 128
EPS = 1e-5
ROPE_THETA = 10000.0
ADAM_LR, ADAM_B1, ADAM_B2, ADAM_EPS, ADAM_WD, ADAM_STEP = 0.001, 0.9, 0.999, 1e-08, 0.01, 10

N_CHIPS = 4
N_CORES = 2
N_DEV = 8
HALO = 16
VMEM_LIMIT_BYTES = 56 * 1024 * 1024
MESH = pl.DeviceIdType.MESH
ANY = pl.BlockSpec(memory_space=pl.ANY)
SMEM = pl.BlockSpec(memory_space=pltpu.SMEM)
NEG = float(jnp.finfo(jnp.float32).min)
ROW_TILES = (512, 256, 128, 64, 32, 16, 8)


def _pick(dim, cands):
    for c in cands:
        if dim % c == 0:
            return c
    return dim


def _params(sem):
    return pltpu.CompilerParams(dimension_semantics=sem, vmem_limit_bytes=VMEM_LIMIT_BYTES)


_DIMS = {"nn": (((1,), (0,)), ((), ())), "nt": (((1,), (1,)), ((), ())), "tn": (((0,), (0,)), ((), ()))}


def _mm(a, b, mode, out_dtype, *, layer=None, bias=None, residual=None, n_outer=False, tm, tn, tk, name):
    b2 = b.shape[1:] if layer is not None else b.shape
    if mode == "nn":
        (m, k), n = a.shape, b2[1]
    elif mode == "nt":
        (m, k), n = a.shape, b2[0]
    else:
        (k, m), n = a.shape, b2[1]
    tm, tn, tk = min(tm, m), min(tn, n), min(tk, k)
    assert m % tm == 0 and n % tn == 0 and k % tk == 0, (name, a.shape, b.shape, tm, tn, tk)
    nk = k // tk

    def at(f):
        return (lambda p0, p1, p2: f(p1, p0, p2)) if n_outer else f

    a_spec = pl.BlockSpec((tk, tm), at(lambda i, j, l: (l, i))) if mode == "tn" else pl.BlockSpec((tm, tk), at(lambda i, j, l: (i, l)))
    if layer is None:
        b_spec = (pl.BlockSpec((tn, tk), at(lambda i, j, l: (j, l))) if mode == "nt"
                  else pl.BlockSpec((tk, tn), at(lambda i, j, l: (l, j))))
    elif mode == "nt":
        b_spec = pl.BlockSpec((None, tn, tk), at(lambda i, j, l: (layer, j, l)))
    else:
        b_spec = pl.BlockSpec((None, tk, tn), at(lambda i, j, l: (layer, l, j)))
    in_specs, args = [a_spec, b_spec], [a, b]
    if bias is not None:
        in_specs.append(pl.BlockSpec((1, tn), at(lambda i, j, l: (0, j))))
        args.append(bias)
    if residual is not None:
        in_specs.append(pl.BlockSpec((tm, tn), at(lambda i, j, l: (i, j))))
        args.append(residual)
    has_bias, has_res = bias is not None, residual is not None

    def body(*refs):
        a_ref, b_ref = refs[0], refs[1]
        pos = 2
        bias_ref = res_ref = None
        if has_bias:
            bias_ref, pos = refs[pos], pos + 1
        if has_res:
            res_ref, pos = refs[pos], pos + 1
        o_ref = refs[pos]
        acc_ref = refs[pos + 1] if nk > 1 else None

        def finish(acc):
            if has_bias:
                acc = acc + bias_ref[...]
            if has_res:
                acc = acc + res_ref[...]
            o_ref[...] = acc.astype(o_ref.dtype)

        if nk == 1:
            finish(lax.dot_general(a_ref[...], b_ref[...], _DIMS[mode], preferred_element_type=F32))
            return
        l = pl.program_id(2)
        part = lax.dot_general(a_ref[...], b_ref[...], _DIMS[mode], preferred_element_type=F32)

        @pl.when(l == 0)
        def _():
            acc_ref[...] = part

        @pl.when(l > 0)
        def _():
            acc_ref[...] += part

        @pl.when(l == nk - 1)
        def _():
            finish(acc_ref[...])

    return pl.pallas_call(
        body,
        name=name,
        grid=(n // tn, m // tm, nk) if n_outer else (m // tm, n // tn, nk),
        in_specs=in_specs,
        out_specs=pl.BlockSpec((tm, tn), at(lambda i, j, l: (i, j))),
        out_shape=jax.ShapeDtypeStruct((m, n), out_dtype),
        scratch_shapes=[pltpu.VMEM((tm, tn), F32)] if nk > 1 else [],
        compiler_params=_params(("parallel", "parallel", "arbitrary")),
    )(*args)


def _fold8(v):
    r, d = v.shape
    return jnp.sum(v.reshape(r // 8, 8, d), axis=0)


def _rms_fwd(x, g, name):
    t, d = x.shape
    tm = _pick(t, ROW_TILES)

    def body(x_ref, g_ref, h_ref):
        xv = x_ref[...]
        r = lax.rsqrt(jnp.mean(xv * xv, axis=-1, keepdims=True) + EPS)
        h_ref[...] = (xv * r * g_ref[...]).astype(BF16)

    return pl.pallas_call(
        body,
        name=name,
        grid=(t // tm,),
        in_specs=[pl.BlockSpec((tm, d), lambda i: (i, 0)), pl.BlockSpec((1, d), lambda i: (0, 0))],
        out_specs=pl.BlockSpec((tm, d), lambda i: (i, 0)),
        out_shape=jax.ShapeDtypeStruct((t, d), BF16),
        compiler_params=_params(("parallel",)),
    )(x, g)


def _rms_bwd(x, dh, g, dx_in, name):
    t, d = x.shape
    tm = _pick(t, ROW_TILES)

    def body(x_ref, dh_ref, g_ref, dxi_ref, dx_ref, dxb_ref, dg_ref, cs_ref):
        i = pl.program_id(0)
        xv = x_ref[...]
        r = lax.rsqrt(jnp.mean(xv * xv, axis=-1, keepdims=True) + EPS)
        xhat = xv * r
        dy = dh_ref[...]
        gdy = dy * g_ref[...]
        dx = dxi_ref[...] + r * (gdy - xhat * jnp.mean(gdy * xhat, axis=-1, keepdims=True))
        dx_ref[...] = dx
        dxb_ref[...] = dx.astype(BF16)

        @pl.when(i == 0)
        def _():
            dg_ref[...] = jnp.zeros_like(dg_ref)
            cs_ref[...] = jnp.zeros_like(cs_ref)

        dg_ref[...] += _fold8(dy * xhat)
        cs_ref[...] += _fold8(dx)

    row = pl.BlockSpec((tm, d), lambda i: (i, 0))
    acc = pl.BlockSpec((8, d), lambda i: (0, 0))
    return pl.pallas_call(
        body,
        name=name,
        grid=(t // tm,),
        in_specs=[row, row, pl.BlockSpec((1, d), lambda i: (0, 0)), row],
        out_specs=[row, row, acc, acc],
        out_shape=[jax.ShapeDtypeStruct((t, d), F32), jax.ShapeDtypeStruct((t, d), BF16),
                   jax.ShapeDtypeStruct((8, d), F32), jax.ShapeDtypeStruct((8, d), F32)],
        compiler_params=_params(("arbitrary",)),
    )(x, dh, g, dx_in)


def _loss_head(x, target, g, name):
    t, d = x.shape
    tm = _pick(t, ROW_TILES)
    inv_d = 1.0 / d

    def body(x_ref, t_ref, g_ref, dx_ref, dxb_ref, sq_ref, dg_ref):
        i = pl.program_id(0)
        xv = x_ref[...]
        gv = g_ref[...]
        r = lax.rsqrt(jnp.mean(xv * xv, axis=-1, keepdims=True) + EPS)
        xhat = xv * r
        err = xhat * gv - t_ref[...]
        dy = err * inv_d
        gdy = dy * gv
        dx = r * (gdy - xhat * jnp.mean(gdy * xhat, axis=-1, keepdims=True))
        dx_ref[...] = dx
        dxb_ref[...] = dx.astype(BF16)

        @pl.when(i == 0)
        def _():
            sq_ref[...] = jnp.zeros_like(sq_ref)
            dg_ref[...] = jnp.zeros_like(dg_ref)

        sq_ref[...] += _fold8(err * err)
        dg_ref[...] += _fold8(dy * xhat)

    row = pl.BlockSpec((tm, d), lambda i: (i, 0))
    acc = pl.BlockSpec((8, d), lambda i: (0, 0))
    return pl.pallas_call(
        body,
        name=name,
        grid=(t // tm,),
        in_specs=[row, row, pl.BlockSpec((1, d), lambda i: (0, 0))],
        out_specs=[row, row, acc, acc],
        out_shape=[jax.ShapeDtypeStruct((t, d), F32), jax.ShapeDtypeStruct((t, d), BF16),
                   jax.ShapeDtypeStruct((8, d), F32), jax.ShapeDtypeStruct((8, d), F32)],
        compiler_params=_params(("arbitrary",)),
    )(x, target, g)


def _rows(tm):
    return lax.broadcasted_iota(jnp.int32, (tm, 1), 0)


def _shift_down(u, before2):
    r8 = _rows(8)
    s1, s2 = pltpu.roll(u, 1, 0), pltpu.roll(u, 2, 0)
    top1 = jnp.where(r8 == 0, before2[1:2], s1[:8])
    top2 = jnp.where(r8 == 0, before2[0:1], jnp.where(r8 == 1, before2[1:2], s2[:8]))
    return jnp.concatenate([top1, s1[8:]], axis=0), jnp.concatenate([top2, s2[8:]], axis=0)


def _shift_up(u, after2):
    tm = u.shape[0]
    r8 = _rows(8)
    s1, s2 = pltpu.roll(u, tm - 1, 0), pltpu.roll(u, tm - 2, 0)
    bot1 = jnp.where(r8 == 7, after2[0:1], s1[tm - 8:])
    bot2 = jnp.where(r8 == 6, after2[0:1], jnp.where(r8 == 7, after2[1:2], s2[tm - 8:]))
    return jnp.concatenate([s1[:tm - 8], bot1], axis=0), jnp.concatenate([s2[:tm - 8], bot2], axis=0)


def _shift_matrix(tm, up):
    r = lax.broadcasted_iota(jnp.int32, (2 * tm, tm), 0)
    c = lax.broadcasted_iota(jnp.int32, (2 * tm, tm), 1)
    t = jnp.where(r >= tm, r - tm, r)
    k = jnp.where(r >= tm, 2, 1)
    return (c == (t + k if up else t - k)).astype(BF16)


def _shift_down_mxu(u, before2):
    tm = u.shape[0]
    moved = jnp.dot(_shift_matrix(tm, False), u.astype(BF16), preferred_element_type=F32)
    r8 = _rows(8)
    s1, s2 = moved[:tm], moved[tm:]
    top1 = s1[:8] + jnp.where(r8 == 0, before2[1:2], 0.0)
    top2 = s2[:8] + jnp.where(r8 == 0, before2[0:1], jnp.where(r8 == 1, before2[1:2], 0.0))
    return jnp.concatenate([top1, s1[8:]], axis=0), jnp.concatenate([top2, s2[8:]], axis=0)


def _shift_up_mxu(u, after2):
    tm = u.shape[0]
    moved = jnp.dot(_shift_matrix(tm, True), u.astype(BF16), preferred_element_type=F32)
    r8 = _rows(8)
    s1, s2 = moved[:tm], moved[tm:]
    bot1 = s1[tm - 8:] + jnp.where(r8 == 7, after2[0:1], 0.0)
    bot2 = s2[tm - 8:] + jnp.where(r8 == 6, after2[0:1], jnp.where(r8 == 7, after2[1:2], 0.0))
    return jnp.concatenate([s1[:tm - 8], bot1], axis=0), jnp.concatenate([s2[:tm - 8], bot2], axis=0)


def _conv_tile(seq):
    return _pick(seq, (256, 128, 64, 32, 16, 8))


def _halo_specs(tm, width, n_tiles):
    per = tm // HALO
    before = pl.BlockSpec((HALO, width), lambda i: (jnp.maximum(i * per - 1, 0), 0))
    after = pl.BlockSpec((HALO, width), lambda i: (jnp.minimum((i + 1) * per, n_tiles * per - 1), 0))
    return before, after


def _convgate_fwd(bcv, w, seq, name):
    t, d3 = bcv.shape
    d = d3 // 3
    tm = _conv_tile(seq)
    tps = seq // tm
    before, _ = _halo_specs(tm, d3, t // tm)

    def body(x_ref, xb_ref, w_ref, y_ref):
        i = pl.program_id(0)
        inner = (i % tps != 0).astype(F32)
        u = x_ref[:, d:2 * d].astype(F32) * x_ref[:, 2 * d:].astype(F32)
        xb = xb_ref[:, d:].astype(F32)[HALO - 2:]
        s1, s2 = _shift_down(u, xb[:, :d] * xb[:, d:] * inner)
        z = w_ref[2:3] * u + w_ref[1:2] * s1 + w_ref[0:1] * s2
        y_ref[...] = (x_ref[:, :d].astype(F32) * z).astype(BF16)

    return pl.pallas_call(
        body,
        name=name,
        grid=(t // tm,),
        in_specs=[pl.BlockSpec((tm, d3), lambda i: (i, 0)), before, pl.BlockSpec((3, d), lambda i: (0, 0))],
        out_specs=pl.BlockSpec((tm, d), lambda i: (i, 0)),
        out_shape=jax.ShapeDtypeStruct((t, d), BF16),
        compiler_params=_params(("parallel",)),
    )(bcv, bcv, w)


def _convgate_bwd(bcv, dy, w, seq, name):
    t, d3 = bcv.shape
    d = d3 // 3
    tm = _conv_tile(seq)
    tps = seq // tm
    before, after = _halo_specs(tm, d3, t // tm)
    _, after_dy = _halo_specs(tm, d, t // tm)

    def body(x_ref, xb_ref, xa_ref, dy_ref, dya_ref, w_ref, dx_ref, dw_ref):
        i = pl.program_id(0)
        inner_lo = (i % tps != 0).astype(F32)
        inner_hi = (i % tps != tps - 1).astype(F32)
        w0, w1, w2 = w_ref[0:1], w_ref[1:2], w_ref[2:3]
        b, c, v = x_ref[:, :d].astype(F32), x_ref[:, d:2 * d].astype(F32), x_ref[:, 2 * d:].astype(F32)
        u = c * v
        xb = xb_ref[:, d:].astype(F32)[HALO - 2:]
        s1, s2 = _shift_down(u, xb[:, :d] * xb[:, d:] * inner_lo)
        z = w2 * u + w1 * s1 + w0 * s2
        dyv = dy_ref[...].astype(F32)
        dz = dyv * b
        dza = dya_ref[...].astype(F32)[0:2] * xa_ref[:, :d].astype(F32)[0:2] * inner_hi
        n1, n2 = _shift_up(dz, dza)
        du = w2 * dz + w1 * n1 + w0 * n2
        dx_ref[:, :d] = (dyv * z).astype(BF16)
        dx_ref[:, d:2 * d] = (du * v).astype(BF16)
        dx_ref[:, 2 * d:] = (du * c).astype(BF16)

        @pl.when(i == 0)
        def _():
            dw_ref[...] = jnp.zeros_like(dw_ref)

        dw_ref[0:1] += jnp.sum(dz * s2, axis=0, keepdims=True)
        dw_ref[1:2] += jnp.sum(dz * s1, axis=0, keepdims=True)
        dw_ref[2:3] += jnp.sum(dz * u, axis=0, keepdims=True)

    return pl.pallas_call(
        body,
        name=name,
        grid=(t // tm,),
        in_specs=[pl.BlockSpec((tm, d3), lambda i: (i, 0)), before, after,
                  pl.BlockSpec((tm, d), lambda i: (i, 0)), after_dy, pl.BlockSpec((3, d), lambda i: (0, 0))],
        out_specs=[pl.BlockSpec((tm, d3), lambda i: (i, 0)), pl.BlockSpec((8, d), lambda i: (0, 0))],
        out_shape=[jax.ShapeDtypeStruct((t, d3), BF16), jax.ShapeDtypeStruct((8, d), F32)],
        compiler_params=_params(("arbitrary",)),
    )(bcv, bcv, bcv, dy, dy, w)


def _sigmoid(x):
    return 1.0 / (1.0 + jnp.exp(-x))


def _ffngate_fwd(gu, w, seq, name):
    t, f2 = gu.shape
    f = f2 // 2
    tm = _conv_tile(seq)
    tps = seq // tm
    before, _ = _halo_specs(tm, f2, t // tm)

    def body(x_ref, xb_ref, w_ref, a_ref):
        i = pl.program_id(0)
        inner = (i % tps != 0).astype(F32)
        s1, s2 = _shift_down_mxu(x_ref[:, :f], xb_ref[:, :f].astype(F32)[HALO - 2:] * inner)
        gc = w_ref[2:3] * x_ref[:, :f].astype(F32) + w_ref[1:2] * s1 + w_ref[0:1] * s2
        a_ref[...] = (gc * _sigmoid(gc) * x_ref[:, f:].astype(F32)).astype(BF16)

    return pl.pallas_call(
        body,
        name=name,
        grid=(t // tm,),
        in_specs=[pl.BlockSpec((tm, f2), lambda i: (i, 0)), before, pl.BlockSpec((3, f), lambda i: (0, 0))],
        out_specs=pl.BlockSpec((tm, f), lambda i: (i, 0)),
        out_shape=jax.ShapeDtypeStruct((t, f), BF16),
        compiler_params=_params(("parallel",)),
    )(gu, gu, w)


def _ffngate_bwd(gu, da, w, seq, name):
    t, f2 = gu.shape
    f = f2 // 2
    tm = _conv_tile(seq)
    tps = seq // tm
    before, after = _halo_specs(tm, f2, t // tm)
    _, after_da = _halo_specs(tm, f, t // tm)

    def body(x_ref, xb_ref, xa_ref, da_ref, daa_ref, w_ref, dx_ref, dw_ref):
        i = pl.program_id(0)
        inner_lo = (i % tps != 0).astype(F32)
        inner_hi = (i % tps != tps - 1).astype(F32)
        w0, w1, w2 = w_ref[0:1], w_ref[1:2], w_ref[2:3]

        def dgate(gc, uv, dav):
            sg = _sigmoid(gc)
            return dav * uv * (sg * (1.0 + gc * (1.0 - sg))), dav * (gc * sg)

        g, u = x_ref[:, :f].astype(F32), x_ref[:, f:].astype(F32)
        s1, s2 = _shift_down_mxu(x_ref[:, :f], xb_ref[:, :f].astype(F32)[HALO - 2:] * inner_lo)
        gc = w2 * g + w1 * s1 + w0 * s2
        dgc, du = dgate(gc, u, da_ref[...].astype(F32))
        ga = xa_ref[:, :f].astype(F32)
        a1, a2 = _shift_down(ga, x_ref[tm - HALO:, :f].astype(F32)[HALO - 2:])
        gca = w2 * ga + w1 * a1 + w0 * a2
        dgca, _ = dgate(gca, xa_ref[:, f:].astype(F32), daa_ref[...].astype(F32))
        n1, n2 = _shift_up_mxu(dgc, dgca[0:2] * inner_hi)
        dx_ref[:, :f] = (w2 * dgc + w1 * n1 + w0 * n2).astype(BF16)
        dx_ref[:, f:] = du.astype(BF16)

        @pl.when(i == 0)
        def _():
            dw_ref[...] = jnp.zeros_like(dw_ref)

        dw_ref[0:1] += jnp.sum(dgc * s2, axis=0, keepdims=True)
        dw_ref[1:2] += jnp.sum(dgc * s1, axis=0, keepdims=True)
        dw_ref[2:3] += jnp.sum(dgc * g, axis=0, keepdims=True)

    return pl.pallas_call(
        body,
        name=name,
        grid=(t // tm,),
        in_specs=[pl.BlockSpec((tm, f2), lambda i: (i, 0)), before, after,
                  pl.BlockSpec((tm, f), lambda i: (i, 0)), after_da, pl.BlockSpec((3, f), lambda i: (0, 0))],
        out_specs=[pl.BlockSpec((tm, f2), lambda i: (i, 0)), pl.BlockSpec((8, f), lambda i: (0, 0))],
        out_shape=[jax.ShapeDtypeStruct((t, f2), BF16), jax.ShapeDtypeStruct((8, f), F32)],
        compiler_params=_params(("arbitrary",)),
    )(gu, gu, gu, da, da, w)


def _swap_halves(xt):
    half = HEAD_DIM // 2
    return jnp.concatenate([xt[half:], xt[:half]], axis=0)


def _rope(xt, cos, sin):
    return xt * cos + _swap_halves(xt) * sin


def _unrope(dxt, cos, sin):
    return dxt * cos - _swap_halves(dxt) * sin


def _key_query(count):
    kj = lax.broadcasted_iota(jnp.int32, (WINDOW, count * WINDOW), 0)
    qi = lax.broadcasted_iota(jnp.int32, (WINDOW, count * WINDOW), 1) & (WINDOW - 1)
    return kj, qi


def _band_masks(n, count):
    kj, qi = _key_query(count)
    return kj <= qi, jnp.logical_and(kj > qi, n > 0)


def _lanes(v, count):
    return jnp.concatenate([v] * count, axis=1) if count > 1 else v


def _heads(ref, h0, count):
    parts = [ref[(h0 + g) * HEAD_DIM:(h0 + g + 1) * HEAD_DIM, :] for g in range(count)]
    return jnp.concatenate(parts, axis=1) if count > 1 else parts[0]


def _head_rows(ref, h0, count):
    parts = [ref[h0 + g:h0 + g + 1, :] for g in range(count)]
    return jnp.concatenate(parts, axis=1) if count > 1 else parts[0]


def _head_sinks(sink_ref, h0, count):
    parts = [jnp.full((1, WINDOW), sink_ref[h0 + g], F32) for g in range(count)]
    return jnp.concatenate(parts, axis=1) if count > 1 else parts[0]


def _tn(a, b):
    return lax.dot_general(a, b, _DIMS["tn"], preferred_element_type=F32)


def _nt(a, b):
    return lax.dot_general(a, b, _DIMS["nt"], preferred_element_type=F32)


def _nn(a, b):
    return jnp.dot(a, b, preferred_element_type=F32)


def _attn_fwd(qkv, sinks, cos_t, sin_t, bsz, seq, name, hp):
    t, qw = qkv.shape
    d = qw * 2 // 3
    kvw = d // GROUP
    n_heads, n_kv = d // HEAD_DIM, kvw // HEAD_DIM
    nb = seq // WINDOW
    scale = HEAD_DIM ** -0.5

    def body(sink_ref, xc_ref, xp_ref, cc_ref, sc_ref, cp_ref, sp_ref, o_ref, lse_ref, xt_ref, pt_ref, ot_ref):
        n = pl.program_id(1)
        xt_ref[...] = xc_ref[...].T
        pt_ref[...] = xp_ref[:, d:].T
        cos_c, sin_c, cos_p, sin_p = cc_ref[...], sc_ref[...], cp_ref[...], sp_ref[...]
        cos_g, sin_g = _lanes(cos_c, hp), _lanes(sin_c, hp)
        valid_c, valid_p = _band_masks(n, hp)
        for j in range(n_kv):
            ko = j * HEAD_DIM
            kc = _rope(xt_ref[d + ko:d + ko + HEAD_DIM, :], cos_c, sin_c).astype(BF16)
            kp = _rope(pt_ref[ko:ko + HEAD_DIM, :], cos_p, sin_p).astype(BF16)
            vc = xt_ref[d + kvw + ko:d + kvw + ko + HEAD_DIM, :].astype(BF16)
            vp = pt_ref[kvw + ko:kvw + ko + HEAD_DIM, :].astype(BF16)
            for h0 in range(j * GROUP, (j + 1) * GROUP, hp):
                q = _rope(_heads(xt_ref, h0, hp), cos_g, sin_g).astype(BF16)
                sink = _head_sinks(sink_ref, h0, hp)
                s_c = jnp.where(valid_c, _tn(kc, q) * scale, NEG)
                s_p = jnp.where(valid_p, _tn(kp, q) * scale, NEG)
                m = jnp.maximum(jnp.maximum(jnp.max(s_c, axis=0, keepdims=True), jnp.max(s_p, axis=0, keepdims=True)), sink)
                p_c = jnp.exp(s_c - m)
                p_p = jnp.exp(s_p - m)
                den = jnp.sum(p_c, axis=0, keepdims=True) + jnp.sum(p_p, axis=0, keepdims=True) + jnp.exp(sink - m)
                inv = 1.0 / den
                o_g = _nn(vc, (p_c * inv).astype(BF16)) + _nn(vp, (p_p * inv).astype(BF16))
                lse_g = m + jnp.log(den)
                for g in range(hp):
                    h = h0 + g
                    ot_ref[h * HEAD_DIM:(h + 1) * HEAD_DIM, :] = o_g[:, g * WINDOW:(g + 1) * WINDOW]
                    lse_ref[h:h + 1, :] = lse_g[:, g * WINDOW:(g + 1) * WINDOW]
        o_ref[...] = ot_ref[...].T.astype(BF16)

    cur = lambda b, n: (b * nb + n, 0)
    prev = lambda b, n: (b * nb + jnp.maximum(n - 1, 0), 0)
    tab_c = pl.BlockSpec((HEAD_DIM, WINDOW), lambda b, n: (0, n))
    tab_p = pl.BlockSpec((HEAD_DIM, WINDOW), lambda b, n: (0, jnp.maximum(n - 1, 0)))
    return pl.pallas_call(
        body,
        name=name,
        grid=(bsz, nb),
        in_specs=[SMEM, pl.BlockSpec((WINDOW, qw), cur), pl.BlockSpec((WINDOW, qw), prev), tab_c, tab_c, tab_p, tab_p],
        out_specs=[pl.BlockSpec((WINDOW, d), cur), pl.BlockSpec((n_heads, WINDOW), lambda b, n: (0, b * nb + n))],
        out_shape=[jax.ShapeDtypeStruct((t, d), BF16), jax.ShapeDtypeStruct((n_heads, t), F32)],
        scratch_shapes=[pltpu.VMEM((qw, WINDOW), F32), pltpu.VMEM((2 * kvw, WINDOW), F32), pltpu.VMEM((d, WINDOW), F32)],
        compiler_params=_params(("parallel", "arbitrary")),
    )(sinks, qkv, qkv, cos_t, sin_t, cos_t, sin_t)


def _attn_bwd(qkv, o, lse, do, sinks, cos_t, sin_t, bsz, seq, name, hp):
    t, qw = qkv.shape
    d = qw * 2 // 3
    kvw = d // GROUP
    n_heads, n_kv = d // HEAD_DIM, kvw // HEAD_DIM
    nb = seq // WINDOW
    scale = HEAD_DIM ** -0.5

    def body(sink_ref, xc_ref, xp_ref, xn_ref, oc_ref, on_ref, doc_ref, don_ref, lc_ref, ln_ref,
             cc_ref, sc_ref, cp_ref, sp_ref, cn_ref, sn_ref,
             dx_ref, db_ref, dsk_ref, xt_ref, pt_ref, qn_ref, otc_ref, otn_ref, dtc_ref, dtn_ref, gt_ref):
        b, n = pl.program_id(0), pl.program_id(1)
        xt_ref[...] = xc_ref[...].T
        pt_ref[...] = xp_ref[:, d:].T
        qn_ref[...] = xn_ref[:, :d].T
        otc_ref[...] = oc_ref[...].astype(F32).T
        otn_ref[...] = on_ref[...].astype(F32).T
        dtc_ref[...] = doc_ref[...].T
        dtn_ref[...] = don_ref[...].T
        cos_c, sin_c, cos_p, sin_p, cos_n, sin_n = (cc_ref[...], sc_ref[...], cp_ref[...], sp_ref[...],
                                                    cn_ref[...], sn_ref[...])
        cos_g, sin_g, cos_gn, sin_gn = _lanes(cos_c, hp), _lanes(sin_c, hp), _lanes(cos_n, hp), _lanes(sin_n, hp)
        valid_c, valid_p = _band_masks(n, hp)
        kj, qi = _key_query(hp)
        valid_n = jnp.logical_and(kj > qi, n < nb - 1)

        @pl.when(jnp.logical_and(b == 0, n == 0))
        def _():
            db_ref[...] = jnp.zeros_like(db_ref)
            dsk_ref[...] = jnp.zeros_like(dsk_ref)

        for j in range(n_kv):
            ko = j * HEAD_DIM
            kc = _rope(xt_ref[d + ko:d + ko + HEAD_DIM, :], cos_c, sin_c).astype(BF16)
            kp = _rope(pt_ref[ko:ko + HEAD_DIM, :], cos_p, sin_p).astype(BF16)
            vc = xt_ref[d + kvw + ko:d + kvw + ko + HEAD_DIM, :].astype(BF16)
            vp = pt_ref[kvw + ko:kvw + ko + HEAD_DIM, :].astype(BF16)
            dk = jnp.zeros((HEAD_DIM, WINDOW), F32)
            dv = jnp.zeros((HEAD_DIM, WINDOW), F32)
            for h0 in range(j * GROUP, (j + 1) * GROUP, hp):
                q = _rope(_heads(xt_ref, h0, hp), cos_g, sin_g).astype(BF16)
                do_g = _heads(dtc_ref, h0, hp)
                do_b = do_g.astype(BF16)
                lse_g = _head_rows(lc_ref, h0, hp)
                delta = jnp.sum(_heads(otc_ref, h0, hp) * do_g, axis=0, keepdims=True)
                p_c = jnp.exp(jnp.where(valid_c, _tn(kc, q) * scale, NEG) - lse_g)
                p_p = jnp.exp(jnp.where(valid_p, _tn(kp, q) * scale, NEG) - lse_g)
                ds_c = (p_c * (_tn(vc, do_b) - delta)).astype(BF16)
                ds_p = (p_p * (_tn(vp, do_b) - delta)).astype(BF16)
                dq = _unrope((_nn(kc, ds_c) + _nn(kp, ds_p)) * scale, cos_g, sin_g)
                dsk = -jnp.exp(_head_sinks(sink_ref, h0, hp) - lse_g) * delta
                for g in range(hp):
                    h = h0 + g
                    gt_ref[h * HEAD_DIM:(h + 1) * HEAD_DIM, :] = dq[:, g * WINDOW:(g + 1) * WINDOW]
                    dsk_ref[h:h + 1, :] += dsk[:, g * WINDOW:(g + 1) * WINDOW]
                q2 = _rope(_heads(qn_ref, h0, hp), cos_gn, sin_gn).astype(BF16)
                do2 = _heads(dtn_ref, h0, hp)
                do2_b = do2.astype(BF16)
                delta2 = jnp.sum(_heads(otn_ref, h0, hp) * do2, axis=0, keepdims=True)
                p_n = jnp.exp(jnp.where(valid_n, _tn(kc, q2) * scale, NEG) - _head_rows(ln_ref, h0, hp))
                ds_n = (p_n * (_tn(vc, do2_b) - delta2)).astype(BF16)
                dv += _nt(do_b, p_c.astype(BF16)) + _nt(do2_b, p_n.astype(BF16))
                dk += _nt(q, ds_c) + _nt(q2, ds_n)
            gt_ref[d + ko:d + ko + HEAD_DIM, :] = _unrope(dk * scale, cos_c, sin_c)
            gt_ref[d + kvw + ko:d + kvw + ko + HEAD_DIM, :] = dv
        dx = gt_ref[...].T
        dx_ref[...] = dx.astype(BF16)
        db_ref[...] += _fold8(dx)

    cur = lambda b, n: (b * nb + n, 0)
    prev = lambda b, n: (b * nb + jnp.maximum(n - 1, 0), 0)
    nxt = lambda b, n: (b * nb + jnp.minimum(n + 1, nb - 1), 0)
    stat_c = pl.BlockSpec((n_heads, WINDOW), lambda b, n: (0, b * nb + n))
    stat_n = pl.BlockSpec((n_heads, WINDOW), lambda b, n: (0, b * nb + jnp.minimum(n + 1, nb - 1)))
    tab_c = pl.BlockSpec((HEAD_DIM, WINDOW), lambda b, n: (0, n))
    tab_p = pl.BlockSpec((HEAD_DIM, WINDOW), lambda b, n: (0, jnp.maximum(n - 1, 0)))
    tab_n = pl.BlockSpec((HEAD_DIM, WINDOW), lambda b, n: (0, jnp.minimum(n + 1, nb - 1)))
    return pl.pallas_call(
        body,
        name=name,
        grid=(bsz, nb),
        in_specs=[SMEM, pl.BlockSpec((WINDOW, qw), cur), pl.BlockSpec((WINDOW, qw), prev), pl.BlockSpec((WINDOW, qw), nxt),
                  pl.BlockSpec((WINDOW, d), cur), pl.BlockSpec((WINDOW, d), nxt),
                  pl.BlockSpec((WINDOW, d), cur), pl.BlockSpec((WINDOW, d), nxt),
                  stat_c, stat_n, tab_c, tab_c, tab_p, tab_p, tab_n, tab_n],
        out_specs=[pl.BlockSpec((WINDOW, qw), cur), pl.BlockSpec((8, qw), lambda b, n: (0, 0)),
                   pl.BlockSpec((n_heads, WINDOW), lambda b, n: (0, 0))],
        out_shape=[jax.ShapeDtypeStruct((t, qw), BF16), jax.ShapeDtypeStruct((8, qw), F32),
                   jax.ShapeDtypeStruct((n_heads, WINDOW), F32)],
        scratch_shapes=[pltpu.VMEM((qw, WINDOW), F32), pltpu.VMEM((2 * kvw, WINDOW), F32), pltpu.VMEM((d, WINDOW), F32),
                        pltpu.VMEM((d, WINDOW), F32), pltpu.VMEM((d, WINDOW), F32), pltpu.VMEM((d, WINDOW), F32),
                        pltpu.VMEM((d, WINDOW), F32), pltpu.VMEM((qw, WINDOW), F32)],
        compiler_params=_params(("arbitrary", "arbitrary")),
    )(sinks, qkv, qkv, qkv, o, o, do, do, lse, lse, cos_t, sin_t, cos_t, sin_t, cos_t, sin_t)


def _place():
    return lax.axis_index("x"), lax.axis_index("y"), lax.axis_index("c")


def _other_chips(x, y):
    return [(1 - x, y), (x, 1 - y), (1 - x, 1 - y)]


def _place_shard(w, axis, q, name):
    ly, k, n = w.shape
    tr = _pick(k, (256, 128, 64, 32, 16, 8))
    steps = k // tr
    shape = (ly, k * N_CHIPS, n) if axis == 1 else (ly, k, n * N_CHIPS)
    if axis == 1:
        out_spec = pl.BlockSpec((None, tr, n), lambda l, i, q_ref: (l, q_ref[0] * steps + i, 0))
    else:
        out_spec = pl.BlockSpec((None, tr, n), lambda l, i, q_ref: (l, i, q_ref[0]))

    def body(q_ref, w_ref, o_ref):
        del q_ref
        o_ref[...] = w_ref[...].astype(BF16)

    return pl.pallas_call(
        body,
        name=name,
        grid_spec=pltpu.PrefetchScalarGridSpec(
            num_scalar_prefetch=1, grid=(ly, steps),
            in_specs=[pl.BlockSpec((None, tr, n), lambda l, i, q_ref: (l, i, 0))], out_specs=out_spec),
        out_shape=jax.ShapeDtypeStruct(shape, BF16),
        compiler_params=_params(("parallel", "parallel")),
    )(q, w)


def _half_block(ref, axis, layer, px, py, pc):
    blk = 2 * px + py
    if axis == 1:
        rows = ref.shape[1] // (2 * N_CHIPS)
        return ref.at[layer, pl.ds(pl.multiple_of((2 * blk + pc) * rows, 8), rows), :]
    rows, width = ref.shape[1] // 2, ref.shape[2] // N_CHIPS
    return ref.at[layer, pl.ds(pl.multiple_of(pc * rows, 8), rows), pl.ds(pl.multiple_of(blk * width, 128), width)]


def _gather_copy(refs, axes, pieces, send_sems, recv_sems, p, k, stage, whose):
    x, y, c = _place()
    chip = _other_chips(x, y)[k]
    i, layer = pieces[p]
    if stage == 0:
        origin = (x, y, c) if whose == "mine" else (*chip, c)
        to = (*chip, c)
    else:
        origin = (*chip, c) if whose == "mine" else (*chip, 1 - c)
        to = (x, y, 1 - c)
    blk = _half_block(refs[i], axes[i], layer, *origin)
    return pltpu.make_async_remote_copy(src_ref=blk, dst_ref=blk, send_sem=send_sems.at[p * 3 + k],
                                        recv_sem=recv_sems.at[p * 3 + k], device_id=to, device_id_type=MESH)


def _gather_weights(fulls, axes, pieces, name):
    n, m = len(fulls), 3 * len(pieces)

    def body(*refs):
        dst = refs[n:2 * n]
        sems = refs[2 * n:]
        todo = [(p, k) for p in range(len(pieces)) for k in range(3)]
        sends = [_gather_copy(dst, axes, pieces, sems[0], sems[1], p, k, 0, "mine") for p, k in todo]
        for cp in sends:
            cp.start()
        for p, k in todo:
            _gather_copy(dst, axes, pieces, sems[0], sems[1], p, k, 0, "theirs").wait_recv()
            sends.append(_gather_copy(dst, axes, pieces, sems[2], sems[3], p, k, 1, "mine"))
            sends[-1].start()
        for p, k in todo:
            _gather_copy(dst, axes, pieces, sems[2], sems[3], p, k, 1, "theirs").wait_recv()
        for cp in sends:
            cp.wait_send()

    return pl.pallas_call(
        body,
        name=name,
        in_specs=[ANY] * n,
        out_specs=[ANY] * n,
        out_shape=[jax.ShapeDtypeStruct(f.shape, f.dtype) for f in fulls],
        input_output_aliases={i: i for i in range(n)},
        scratch_shapes=[pltpu.SemaphoreType.DMA((m,))] * 4,
    )(*fulls)


HBM_SPEC = pl.BlockSpec(memory_space=pltpu.HBM)
SEM_SPEC = pl.BlockSpec(memory_space=pltpu.SEMAPHORE)


def _gather_start(fulls, axes, pieces, stage, name):
    n, m = len(fulls), 3 * len(pieces)

    def body(*refs):
        src = refs[:n]
        send_sems, recv_sems = refs[2 * n], refs[2 * n + 1]
        for p in range(len(pieces)):
            for k in range(3):
                _gather_copy(src, axes, pieces, send_sems, recv_sems, p, k, stage, "mine").start()

    out = pl.pallas_call(
        body,
        name=name,
        in_specs=[HBM_SPEC] * n,
        out_specs=[HBM_SPEC] * n + [SEM_SPEC, SEM_SPEC],
        out_shape=[pltpu.HBM(f.shape, f.dtype) for f in fulls] + [pltpu.SemaphoreType.DMA((m,)), pltpu.SemaphoreType.DMA((m,))],
        input_output_aliases={i: i for i in range(n)},
        compiler_params=pltpu.CompilerParams(has_side_effects=pltpu.SideEffectType.DATAFLOW_SIDE_EFFECTING),
    )(*[pltpu.with_memory_space_constraint(f, pltpu.HBM) for f in fulls])
    return list(out[:n]), out[n], out[n + 1]


def _gather_wait(fulls, send_sems, recv_sems, after, axes, pieces, stage, name):
    n = len(fulls)

    def body(*refs):
        src = refs[:n]
        s_sems, r_sems = refs[n], refs[n + 1]
        for p in range(len(pieces)):
            for k in range(3):
                _gather_copy(src, axes, pieces, s_sems, r_sems, p, k, stage, "mine").wait_send()
                _gather_copy(src, axes, pieces, s_sems, r_sems, p, k, stage, "theirs").wait_recv()

    out = pl.pallas_call(
        body,
        name=name,
        in_specs=[HBM_SPEC] * n + [SEM_SPEC, SEM_SPEC, ANY],
        out_specs=[HBM_SPEC] * n,
        out_shape=[pltpu.HBM(f.shape, f.dtype) for f in fulls],
        input_output_aliases={i: i for i in range(n)},
        compiler_params=pltpu.CompilerParams(has_side_effects=pltpu.SideEffectType.DATAFLOW_SIDE_EFFECTING),
    )(*fulls, send_sems, recv_sems, after)
    return list(out)


def _half_shape(kind, shape):
    if kind == "col":
        return (shape[0] // 2, shape[1])
    return (N_CHIPS, shape[1] // 2, shape[2])


def _half_of(kind, ref, h):
    if kind == "col":
        r = ref.shape[0] // 2
        return ref.at[pl.ds(pl.multiple_of(h * r, 8), r), :]
    r = ref.shape[1] // 2
    return ref.at[:, pl.ds(pl.multiple_of(h * r, 8), r), :]


def _pair_exchange(grads, kinds, name):
    n = len(grads)
    outs = [jax.ShapeDtypeStruct(_half_shape(kd, g.shape), g.dtype) for g, kd in zip(grads, kinds)]

    def body(*refs):
        src, dst = refs[:n], refs[n:2 * n]
        send_sems, recv_sems = refs[2 * n:]
        x, y, c = _place()
        copies = [pltpu.make_async_remote_copy(
            src_ref=_half_of(kinds[i], src[i], 1 - c), dst_ref=dst[i],
            send_sem=send_sems.at[i], recv_sem=recv_sems.at[i], device_id=(x, y, 1 - c), device_id_type=MESH)
            for i in range(n)]
        for cp in copies:
            cp.start()
        for cp in copies:
            cp.wait_recv()
        for cp in copies:
            cp.wait_send()

    return pl.pallas_call(
        body,
        name=name,
        in_specs=[ANY] * n,
        out_specs=[ANY] * n,
        out_shape=outs,
        scratch_shapes=[pltpu.SemaphoreType.DMA((n,)), pltpu.SemaphoreType.DMA((n,))],
    )(*grads)


def _slot_shape(kind, shape):
    return (shape[0] // 2, shape[1] // N_CHIPS) if kind == "col" else (shape[1] // 2, shape[2])


def _pair_sum(grad, recv, kind, c, own, layer, name):
    r, cols = own.shape[2:]
    if kind == "col":
        tr = _pick(r, (256, 128, 64, 32, 16, 8))
        steps = r // tr
        grid = (N_CHIPS, steps)
        g_spec = pl.BlockSpec((tr, cols), lambda s, i, c_ref: (c_ref[0] * steps + i, s))
        r_spec = pl.BlockSpec((tr, cols), lambda s, i, c_ref: (i, s))
        o_spec = pl.BlockSpec((None, None, tr, cols), lambda s, i, c_ref: (layer, s, i, 0))
        g_in = grad
    else:
        grid = (N_CHIPS, 1)
        g_spec = pl.BlockSpec((None, None, r, cols), lambda s, i, c_ref: (s, c_ref[0], 0, 0))
        r_spec = pl.BlockSpec((None, r, cols), lambda s, i, c_ref: (s, 0, 0))
        o_spec = pl.BlockSpec((None, None, r, cols), lambda s, i, c_ref: (layer, s, 0, 0))
        g_in = grad.reshape(N_CHIPS, 2, r, cols)

    def body(c_ref, g_ref, r_ref, own_ref, o_ref):
        del c_ref, own_ref
        o_ref[...] = (g_ref[...].astype(F32) + r_ref[...].astype(F32)).astype(o_ref.dtype)

    return pl.pallas_call(
        body,
        name=name,
        grid_spec=pltpu.PrefetchScalarGridSpec(num_scalar_prefetch=1, grid=grid, in_specs=[g_spec, r_spec, ANY], out_specs=o_spec),
        out_shape=jax.ShapeDtypeStruct(own.shape, own.dtype),
        input_output_aliases={3: 0},
        compiler_params=_params(("parallel", "parallel")),
    )(c, g_in, recv, own)


def _scatter_copy(own, mine, sib, pieces, send_sems, recv_sems, p, k, stage, whose):
    x, y, c = _place()
    q = 2 * x + y
    i, layer = pieces[p]
    per = 4 if stage == 0 else 3
    if k == 3:
        src, dst, to = own[i].at[layer, q], sib[i].at[layer, q], (x, y, 1 - c)
    else:
        chip = _other_chips(x, y)[k]
        slot = 2 * chip[0] + chip[1]
        if stage == 0:
            to = (*chip, c)
            src, dst = (own[i].at[layer, slot], mine[i].at[layer, q]) if whose == "mine" else (own[i].at[layer, q], mine[i].at[layer, slot])
        else:
            to = (x, y, 1 - c)
            src, dst = mine[i].at[layer, slot], sib[i].at[layer, slot]
    return pltpu.make_async_remote_copy(src_ref=src, dst_ref=dst, send_sem=send_sems.at[p * per + k],
                                        recv_sem=recv_sems.at[p * per + k], device_id=to, device_id_type=MESH)


def _scatter_start(own, mine, sib, pieces, stage, name):
    n = len(own)
    per = 4 if stage == 0 else 3
    m = per * len(pieces)

    def body(*refs):
        o, mi, si = refs[:n], refs[n:2 * n], refs[2 * n:3 * n]
        send_sems, recv_sems = refs[6 * n], refs[6 * n + 1]
        for p in range(len(pieces)):
            for k in range(per):
                _scatter_copy(o, mi, si, pieces, send_sems, recv_sems, p, k, stage, "mine").start()

    arrays = list(own) + list(mine) + list(sib)
    out = pl.pallas_call(
        body,
        name=name,
        in_specs=[HBM_SPEC] * (3 * n),
        out_specs=[HBM_SPEC] * (3 * n) + [SEM_SPEC, SEM_SPEC],
        out_shape=[pltpu.HBM(a.shape, a.dtype) for a in arrays] + [pltpu.SemaphoreType.DMA((m,)), pltpu.SemaphoreType.DMA((m,))],
        input_output_aliases={i: i for i in range(3 * n)},
        compiler_params=pltpu.CompilerParams(has_side_effects=pltpu.SideEffectType.DATAFLOW_SIDE_EFFECTING),
    )(*[pltpu.with_memory_space_constraint(a, pltpu.HBM) for a in arrays])
    return list(out[:n]), list(out[n:2 * n]), list(out[2 * n:3 * n]), out[3 * n], out[3 * n + 1]


def _scatter_wait(own, mine, sib, send_sems, recv_sems, after, pieces, stage, name):
    n = len(own)
    per = 4 if stage == 0 else 3

    def body(*refs):
        o, mi, si = refs[:n], refs[n:2 * n], refs[2 * n:3 * n]
        s_sems, r_sems = refs[3 * n], refs[3 * n + 1]
        for p in range(len(pieces)):
            for k in range(per):
                _scatter_copy(o, mi, si, pieces, s_sems, r_sems, p, k, stage, "mine").wait_send()
                _scatter_copy(o, mi, si, pieces, s_sems, r_sems, p, k, stage, "theirs").wait_recv()

    arrays = list(own) + list(mine) + list(sib)
    out = pl.pallas_call(
        body,
        name=name,
        in_specs=[HBM_SPEC] * (3 * n) + [SEM_SPEC, SEM_SPEC, ANY],
        out_specs=[HBM_SPEC] * (3 * n),
        out_shape=[pltpu.HBM(a.shape, a.dtype) for a in arrays],
        input_output_aliases={i: i for i in range(3 * n)},
        compiler_params=pltpu.CompilerParams(has_side_effects=pltpu.SideEffectType.DATAFLOW_SIDE_EFFECTING),
    )(*arrays, send_sems, recv_sems, after)
    return list(out[:n]), list(out[n:2 * n]), list(out[2 * n:])


def _reduce_adamw(own, mine, sib, w, m, v, qc, name):
    ly, _, r, cols = mine.shape
    tr = _pick(r, (128, 64, 32, 16, 8))
    steps = r // tr
    c1 = 1.0 - ADAM_B1 ** ADAM_STEP
    c2 = 1.0 - ADAM_B2 ** ADAM_STEP

    def body(qc_ref, own_ref, mine_ref, sib_ref, w_ref, m_ref, v_ref, g_ref, d_ref, nm_ref, nv_ref):
        q = qc_ref[0]
        mine_sum = sib_sum = None
        for s in range(N_CHIPS):
            a = jnp.where(q == s, own_ref[...], mine_ref[s]).astype(F32)
            b = sib_ref[s].astype(F32)
            mine_sum = a if s == 0 else mine_sum + a
            sib_sum = b if s == 0 else sib_sum + b
        gv = jnp.where(pl.program_id(1) == qc_ref[1], mine_sum, sib_sum)
        nm = ADAM_B1 * m_ref[...] + (1.0 - ADAM_B1) * gv
        nv = ADAM_B2 * v_ref[...] + (1.0 - ADAM_B2) * (gv * gv)
        g_ref[...] = gv
        d_ref[...] = -ADAM_LR * ((nm / c1) / (jnp.sqrt(nv / c2) + ADAM_EPS) + ADAM_WD * w_ref[...])
        nm_ref[...] = nm
        nv_ref[...] = nv

    own_spec = pl.BlockSpec((None, None, tr, cols), lambda l, h, i, qc_ref: (l, qc_ref[0], i, 0))
    slot_spec = pl.BlockSpec((None, N_CHIPS, tr, cols), lambda l, h, i, qc_ref: (l, 0, i, 0))
    spec = pl.BlockSpec((None, tr, cols), lambda l, h, i, qc_ref: (l, h * steps + i, 0))
    shp = jax.ShapeDtypeStruct(w.shape, F32)
    return pl.pallas_call(
        body,
        name=name,
        grid_spec=pltpu.PrefetchScalarGridSpec(
            num_scalar_prefetch=1, grid=(ly, N_CORES, steps),
            in_specs=[own_spec, slot_spec, slot_spec, spec, spec, spec], out_specs=[spec] * 4),
        out_shape=[shp] * 4,
        compiler_params=_params(("parallel", "parallel", "parallel")),
    )(qc, own, mine, sib, w, m, v)


def _allreduce_small(v, name):
    r, w = v.shape

    def body(v_ref, o_ref, buf_ref, send_sems, recv_sems):
        x, y, c = _place()
        me = 4 * x + 2 * y + c

        def peer(k):
            return x ^ (k >> 2), y ^ ((k >> 1) & 1), c ^ (k & 1)

        def remote(k, slot):
            return pltpu.make_async_remote_copy(
                src_ref=v_ref, dst_ref=buf_ref.at[slot], send_sem=send_sems.at[k - 1], recv_sem=recv_sems.at[k - 1],
                device_id=peer(k), device_id_type=MESH)

        sends = [remote(k, me) for k in range(1, N_DEV)]
        for cp in sends:
            cp.start()
        buf_ref[me] = v_ref[...]
        for k in range(1, N_DEV):
            px, py, pc = peer(k)
            remote(k, 4 * px + 2 * py + pc).wait_recv()
        for cp in sends:
            cp.wait_send()
        acc = buf_ref[0]
        for dev in range(1, N_DEV):
            acc = acc + buf_ref[dev]
        o_ref[...] = acc

    vm = pl.BlockSpec(memory_space=pltpu.VMEM)
    return pl.pallas_call(
        body,
        name=name,
        in_specs=[vm],
        out_specs=vm,
        out_shape=jax.ShapeDtypeStruct((r, w), F32),
        scratch_shapes=[pltpu.VMEM((N_DEV, r, w), F32), pltpu.SemaphoreType.DMA((N_DEV - 1,)), pltpu.SemaphoreType.DMA((N_DEV - 1,))],
        compiler_params=pltpu.CompilerParams(vmem_limit_bytes=VMEM_LIMIT_BYTES),
    )(v)


def _adamw(w, g, m, v, name):
    ly, r, c = w.shape
    tr = _pick(r, (256, 128, 64, 32, 16, 8))
    c1 = 1.0 - ADAM_B1 ** ADAM_STEP
    c2 = 1.0 - ADAM_B2 ** ADAM_STEP

    def body(w_ref, g_ref, m_ref, v_ref, d_ref, nm_ref, nv_ref):
        gv = g_ref[...]
        nm = ADAM_B1 * m_ref[...] + (1.0 - ADAM_B1) * gv
        nv = ADAM_B2 * v_ref[...] + (1.0 - ADAM_B2) * (gv * gv)
        d_ref[...] = -ADAM_LR * ((nm / c1) / (jnp.sqrt(nv / c2) + ADAM_EPS) + ADAM_WD * w_ref[...])
        nm_ref[...] = nm
        nv_ref[...] = nv

    spec = pl.BlockSpec((None, tr, c), lambda l, i: (l, i, 0))
    shp = jax.ShapeDtypeStruct((ly, r, c), F32)
    return pl.pallas_call(
        body,
        name=name,
        grid=(ly, r // tr),
        in_specs=[spec] * 4,
        out_specs=[spec] * 3,
        out_shape=[shp] * 3,
        compiler_params=_params(("parallel", "parallel")),
    )(w, g, m, v)


def _rope_tables(seq):
    pos = jnp.arange(seq, dtype=F32)
    inv_freq = 1.0 / (ROPE_THETA ** (jnp.arange(0, HEAD_DIM, 2, dtype=F32) / HEAD_DIM))
    ang = (pos[:, None] * inv_freq[None, :]).T
    cos, sin = jnp.cos(ang), jnp.sin(ang)
    return jnp.concatenate([cos, cos], axis=0), jnp.concatenate([-sin, sin], axis=0)


def _pack(vs, fill=0.0):
    p = jnp.concatenate([v.reshape(-1) for v in vs])
    size = -(-p.shape[0] // 8192) * 8192
    return jnp.pad(p, (0, size - p.shape[0]), constant_values=fill).reshape(-1, 1024)


def _unpack(p, like):
    p = p.reshape(-1)
    out, o = [], 0
    for v in like:
        n = int(math.prod(v.shape))
        out.append(p[o:o + n].reshape(v.shape))
        o += n
    return out


def kernel(x, norm_mix, norm_ffn, norm_final, conv_w_in, conv_w_conv, conv_w_out, attn_w_qkv, attn_b_qkv, attn_sinks, attn_w_o, attn_b_o, ffn_w_in, ffn_w_conv, ffn_w_down, loss_target, m_norm_mix, m_norm_ffn, m_norm_final, m_conv_w_in, m_conv_w_conv, m_conv_w_out, m_attn_w_qkv, m_attn_b_qkv, m_attn_sinks, m_attn_w_o, m_attn_b_o, m_ffn_w_in, m_ffn_w_conv, m_ffn_w_down, v_norm_mix, v_norm_ffn, v_norm_final, v_conv_w_in, v_conv_w_conv, v_conv_w_out, v_attn_w_qkv, v_attn_b_qkv, v_attn_sinks, v_attn_w_o, v_attn_b_o, v_ffn_w_in, v_ffn_w_conv, v_ffn_w_down):
    bsz, seq, d = x.shape
    t = bsz * seq
    depth = norm_mix.shape[0]
    n_conv, n_attn = conv_w_in.shape[0], attn_w_qkv.shape[0]
    xq, yq, cq = _place()
    q = 2 * xq + yq

    big = [conv_w_in, conv_w_out, attn_w_qkv, attn_w_o, ffn_w_in, ffn_w_down]
    axes = [2, 1, 2, 1, 2, 1]
    q_arr = q.astype(jnp.int32).reshape(1)
    c_arr = cq.astype(jnp.int32).reshape(1)
    weights = [_place_shard(w, ax, q_arr, f"place_shard{n}") for n, (w, ax) in enumerate(zip(big, axes))]

    def pieces_of(i):
        return [(0, i // 2), (1, i // 2), (4, i), (5, i)] if i % 2 == 0 else [(2, i // 2), (3, i // 2), (4, i), (5, i)]

    weights = _gather_weights(weights, axes, pieces_of(0), "gather_layer0")

    small_cols = [conv_w_conv, attn_b_qkv, attn_b_o, ffn_w_conv]

    def placed(v):
        width = v.shape[-1]
        full = jnp.zeros(v.shape[:-1] + (N_CHIPS * width,), F32)
        return lax.dynamic_update_slice_in_dim(full, v * (1.0 / N_CORES), q * width, axis=v.ndim - 1)

    full_cols = [placed(v) for v in small_cols]
    wc_conv, b_qkv, b_o, wf_conv = _unpack(_allreduce_small(_pack(full_cols), "gather_small"), full_cols)
    cos_t, sin_t = _rope_tables(seq)

    xs = x.reshape(t, d)
    saved = []
    for i in range(depth):
        j = i // 2
        ahead = pieces_of(i + 1) if i + 1 < depth else None
        if ahead:
            weights, s_sems, r_sems = _gather_start(weights, axes, ahead, 0, f"gather_ici_start{i + 1}")
        w_cin, w_cout, w_qkv, w_o, w_fin, w_fdown = weights
        h = _rms_fwd(xs, norm_mix[i:i + 1], f"norm_mix_fwd{i}")
        if i % 2 == 0:
            pre = _mm(h, w_cin, "nn", BF16, layer=j, tm=1024, tn=768, tk=4096, name=f"conv_in_fwd{i}")
            mixed = _convgate_fwd(pre, wc_conv[j], seq, f"conv_gate_fwd{i}")
            x_mid = _mm(mixed, w_cout, "nn", F32, layer=j, residual=xs, tm=512, tn=1024, tk=4096, name=f"conv_out_fwd{i}")
            lse = None
        else:
            pre = _mm(h, w_qkv, "nn", F32, layer=j, bias=b_qkv[j:j + 1], tm=1024, tn=768, tk=4096, name=f"qkv_fwd{i}")
            mixed, lse = _attn_fwd(pre, attn_sinks[j], cos_t, sin_t, bsz, seq, f"attn_fwd{i}", hp=1)
            x_mid = _mm(mixed, w_o, "nn", F32, layer=j, bias=b_o[j:j + 1], residual=xs, tm=512, tn=1024, tk=4096,
                        name=f"attn_out_fwd{i}")
        if ahead:
            weights = _gather_wait(weights, s_sems, r_sems, x_mid, axes, ahead, 0, f"gather_ici_wait{i + 1}")
            weights, s_sems, r_sems = _gather_start(weights, axes, ahead, 1, f"gather_pass_start{i + 1}")
        w_cin, w_cout, w_qkv, w_o, w_fin, w_fdown = weights
        h2 = _rms_fwd(x_mid, norm_ffn[i:i + 1], f"norm_ffn_fwd{i}")
        gu = _mm(h2, w_fin, "nn", BF16, layer=i, n_outer=True, tm=512, tn=1408, tk=4096, name=f"ffn_in_fwd{i}")
        act = _ffngate_fwd(gu, wf_conv[i], seq, f"ffn_gate_fwd{i}")
        x_next = _mm(act, w_fdown, "nn", F32, layer=i, residual=x_mid, tm=512, tn=1024, tk=4096, name=f"ffn_down_fwd{i}")
        if ahead:
            weights = _gather_wait(weights, s_sems, r_sems, x_next, axes, ahead, 1, f"gather_pass_wait{i + 1}")
        saved.append((xs, h, pre, mixed, lse, x_mid, h2, gu, act))
        xs = x_next
    w_cin, w_cout, w_qkv, w_o, w_fin, w_fdown = weights

    dx, dxb, sq, dg_final = _loss_head(xs, loss_target.reshape(t, d), norm_final.reshape(1, d), "loss_head")
    loss = lax.psum(0.5 * jnp.sum(sq) / d, ("x", "y", "c"))

    g_norm_mix, g_norm_ffn = [None] * depth, [None] * depth
    g_cin, g_cconv, g_cout = [None] * n_conv, [None] * n_conv, [None] * n_conv
    g_qkv, g_bqkv, g_sinks, g_o, g_bo = ([None] * n_attn for _ in range(5))
    g_fin, g_fconv, g_fdown = [None] * depth, [None] * depth, [None] * depth

    kinds6 = ["col", "row", "col", "row", "col", "row"]
    layers6 = [n_conv, n_conv, n_attn, n_attn, depth, depth]
    big_w = [conv_w_in, conv_w_out, attn_w_qkv, attn_w_o, ffn_w_in, ffn_w_down]

    def slot_stack(n):
        k, cols = big_w[n].shape[1], big_w[n].shape[2]
        r = k // 2
        return lax.empty((layers6[n], N_CHIPS, r, cols), BF16)

    own = [slot_stack(n) for n in range(6)]
    mine = [slot_stack(n) for n in range(6)]
    sib = [slot_stack(n) for n in range(6)]
    flight = {}

    def scatter(i, stage, action, after=None):
        ts = [ti for ti, _ in pieces_of(i)]
        local = [(n, l) for n, (_, l) in enumerate(pieces_of(i))]
        sub = ([own[ti] for ti in ts], [mine[ti] for ti in ts], [sib[ti] for ti in ts])
        label = f"grad_{'ici' if stage == 0 else 'pass'}_{action}{i}"
        if action == "start":
            o, mi, si, s_sems, r_sems = _scatter_start(*sub, local, stage, label)
            flight[i] = (s_sems, r_sems)
        else:
            o, mi, si = _scatter_wait(*sub, *flight[i], after, local, stage, label)
        for n, ti in enumerate(ts):
            own[ti], mine[ti], sib[ti] = o[n], mi[n], si[n]

    def reduce_begin(i):
        grads = {0: g_cin, 1: g_cout, 2: g_qkv, 3: g_o, 4: g_fin, 5: g_fdown}
        parts, kinds = [], []
        for ti, l in pieces_of(i):
            g = grads[ti][l]
            parts.append(g if kinds6[ti] == "col" else g.reshape(N_CHIPS, g.shape[0] // N_CHIPS, g.shape[1]))
            kinds.append(kinds6[ti])
        recv = _pair_exchange(parts, kinds, f"grad_pair_exchange{i}")
        for (ti, l), g, r in zip(pieces_of(i), parts, recv):
            own[ti] = _pair_sum(g, r, kinds6[ti], c_arr, own[ti], l, f"grad_pair_sum{i}_{ti}")
        scatter(i, 0, "start")

    for i in reversed(range(depth)):
        j = i // 2
        x_in, h, pre, mixed, lse, x_mid, h2, gu, act = saved[i]
        da = _mm(dxb, w_fdown, "nt", BF16, layer=i, n_outer=True, tm=512, tn=1408, tk=4096, name=f"ffn_down_dx{i}")
        g_fdown[i] = _mm(act, dxb, "tn", BF16, tm=1408, tn=1024, tk=2048, name=f"ffn_down_dw{i}")
        dgu, dwc = _ffngate_bwd(gu, da, wf_conv[i], seq, f"ffn_gate_bwd{i}")
        g_fconv[i] = dwc[:3]
        g_fin[i] = _mm(h2, dgu, "tn", BF16, tm=1024, tn=1408, tk=2048, name=f"ffn_in_dw{i}")
        dh2 = _mm(dgu, w_fin, "nt", F32, layer=i, tm=512, tn=1024, tk=8192, name=f"ffn_in_dx{i}")
        dx, dxb, dg, colsum = _rms_bwd(x_mid, dh2, norm_ffn[i:i + 1], dx, f"norm_ffn_bwd{i}")
        g_norm_ffn[i] = jnp.sum(dg, axis=0)
        if i + 1 < depth:
            scatter(i + 1, 0, "wait", dx)
            scatter(i + 1, 1, "start")
        if i % 2 == 0:
            dmix = _mm(dxb, w_cout, "nt", BF16, layer=j, tm=512, tn=1024, tk=4096, name=f"conv_out_dx{i}")
            g_cout[j] = _mm(mixed, dxb, "tn", BF16, tm=1024, tn=1024, tk=2048, name=f"conv_out_dw{i}")
            dpre, dwc = _convgate_bwd(pre, dmix, wc_conv[j], seq, f"conv_gate_bwd{i}")
            g_cconv[j] = dwc[:3]
            g_cin[j] = _mm(h, dpre, "tn", BF16, tm=1024, tn=1536, tk=2048, name=f"conv_in_dw{i}")
            dh = _mm(dpre, w_cin, "nt", F32, layer=j, tm=512, tn=1024, tk=8192, name=f"conv_in_dx{i}")
        else:
            g_bo[j] = jnp.sum(colsum, axis=0)
            dmix = _mm(dxb, w_o, "nt", F32, layer=j, tm=512, tn=1024, tk=4096, name=f"attn_out_dx{i}")
            g_o[j] = _mm(mixed, dxb, "tn", BF16, tm=1024, tn=1024, tk=2048, name=f"attn_out_dw{i}")
            dpre, dbias, dsk = _attn_bwd(pre, mixed, lse, dmix, attn_sinks[j], cos_t, sin_t, bsz, seq, f"attn_bwd{i}",
                                         hp=GROUP)
            g_bqkv[j] = jnp.sum(dbias, axis=0)
            g_sinks[j] = jnp.sum(dsk, axis=1)
            g_qkv[j] = _mm(h, dpre, "tn", BF16, tm=1024, tn=1536, tk=2048, name=f"qkv_dw{i}")
            dh = _mm(dpre, w_qkv, "nt", F32, layer=j, tm=512, tn=1024, tk=1536, name=f"qkv_dx{i}")
        dx, dxb, dg, _ = _rms_bwd(x_in, dh, norm_mix[i:i + 1], dx, f"norm_mix_bwd{i}")
        g_norm_mix[i] = jnp.sum(dg, axis=0)
        if i + 1 < depth:
            scatter(i + 1, 1, "wait", dx)
        reduce_begin(i)
    grad_x = dx.reshape(bsz, seq, d)

    scatter(0, 0, "wait", dx)
    scatter(0, 1, "start")
    scatter(0, 1, "wait", dx)
    big_m = [m_conv_w_in, m_conv_w_out, m_attn_w_qkv, m_attn_w_o, m_ffn_w_in, m_ffn_w_down]
    big_v = [v_conv_w_in, v_conv_w_out, v_attn_w_qkv, v_attn_w_o, v_ffn_w_in, v_ffn_w_down]
    big_names = ["conv_w_in", "conv_w_out", "attn_w_qkv", "attn_w_o", "ffn_w_in", "ffn_w_down"]
    qc_arr = jnp.stack([q, cq]).astype(jnp.int32)
    big_upd = [_reduce_adamw(own[n], mine[n], sib[n], big_w[n], big_m[n], big_v[n], qc_arr, f"adamw_{nm}")
               for n, nm in enumerate(big_names)]

    small = [jnp.stack(g_norm_mix), jnp.stack(g_norm_ffn), jnp.sum(dg_final, axis=0), jnp.stack(g_cconv),
             jnp.stack(g_bqkv), jnp.stack(g_sinks), jnp.stack(g_bo), jnp.stack(g_fconv)]
    sg = _unpack(_allreduce_small(_pack(small), "grad_small_allreduce"), small)

    def my_cols(v, like):
        width = like.shape[-1]
        return lax.dynamic_slice_in_dim(v, q * width, width, axis=v.ndim - 1)

    small_w = [norm_mix, norm_ffn, norm_final, conv_w_conv, attn_b_qkv, attn_sinks, attn_b_o, ffn_w_conv]
    small_m = [m_norm_mix, m_norm_ffn, m_norm_final, m_conv_w_conv, m_attn_b_qkv, m_attn_sinks, m_attn_b_o, m_ffn_w_conv]
    small_v = [v_norm_mix, v_norm_ffn, v_norm_final, v_conv_w_conv, v_attn_b_qkv, v_attn_sinks, v_attn_b_o, v_ffn_w_conv]
    small_g = [sg[0], sg[1], sg[2], my_cols(sg[3], conv_w_conv), my_cols(sg[4], attn_b_qkv), sg[5],
               my_cols(sg[6], attn_b_o), my_cols(sg[7], ffn_w_conv)]

    upd = {nm: tuple(u[1:]) for nm, u in zip(big_names, big_upd)}
    sd, sm, sv = _adamw(_pack(small_w)[None], _pack(small_g)[None], _pack(small_m)[None], _pack(small_v, 1.0)[None],
                        "adamw_small")
    sd, sm, sv = _unpack(sd, small_w), _unpack(sm, small_w), _unpack(sv, small_w)
    names = ["norm_mix", "norm_ffn", "norm_final", "conv_w_in", "conv_w_conv", "conv_w_out", "attn_w_qkv", "attn_b_qkv",
             "attn_sinks", "attn_w_o", "attn_b_o", "ffn_w_in", "ffn_w_conv", "ffn_w_down"]
    small_names = ["norm_mix", "norm_ffn", "norm_final", "conv_w_conv", "attn_b_qkv", "attn_sinks", "attn_b_o", "ffn_w_conv"]
    grads = dict(zip(small_names, small_g))
    grads.update({nm: u[0] for nm, u in zip(big_names, big_upd)})
    for n, nm in enumerate(small_names):
        upd[nm] = (sd[n], sm[n], sv[n])
    return (loss, grad_x, *[grads[nm] for nm in names], *[upd[nm][0] for nm in names],
            *[upd[nm][1] for nm in names], *[upd[nm][2] for nm in names])
```

```python
import math

import jax
import jax.numpy as jnp
from jax import lax
from jax.experimental import pallas as pl
from jax.experimental.pallas import tpu as pltpu

F32 = jnp.float32
BF16 = jnp.bfloat16

HEAD_DIM = 64
GROUP = 4
WINDOW = 128
EPS = 1e-5
ROPE_THETA = 10000.0
ADAM_LR, ADAM_B1, ADAM_B2, ADAM_EPS, ADAM_WD, ADAM_STEP = 0.001, 0.9, 0.999, 1e-08, 0.01, 10

N_CHIPS = 4
N_CORES = 2
N_DEV = 8
HALO = 16
VMEM_LIMIT_BYTES = 56 * 1024 * 1024
MESH = pl.DeviceIdType.MESH
ANY = pl.BlockSpec(memory_space=pl.ANY)
SMEM = pl.BlockSpec(memory_space=pltpu.SMEM)
NEG = float(jnp.finfo(jnp.float32).min)
ROW_TILES = (512, 256, 128, 64, 32, 16, 8)


def _pick(dim, cands):
    for c in cands:
        if dim % c == 0:
            return c
    return dim


def _params(sem):
    return pltpu.CompilerParams(dimension_semantics=sem, vmem_limit_bytes=VMEM_LIMIT_BYTES)


_DIMS = {"nn": (((1,), (0,)), ((), ())), "nt": (((1,), (1,)), ((), ())), "tn": (((0,), (0,)), ((), ()))}


def _mm(a, b, mode, out_dtype, *, layer=None, bias=None, residual=None, n_outer=False, tm, tn, tk, name):
    b2 = b.shape[1:] if layer is not None else b.shape
    if mode == "nn":
        (m, k), n = a.shape, b2[1]
    elif mode == "nt":
        (m, k), n = a.shape, b2[0]
    else:
        (k, m), n = a.shape, b2[1]
    tm, tn, tk = min(tm, m), min(tn, n), min(tk, k)
    assert m % tm == 0 and n % tn == 0 and k % tk == 0, (name, a.shape, b.shape, tm, tn, tk)
    nk = k // tk

    def at(f):
        return (lambda p0, p1, p2: f(p1, p0, p2)) if n_outer else f

    a_spec = pl.BlockSpec((tk, tm), at(lambda i, j, l: (l, i))) if mode == "tn" else pl.BlockSpec((tm, tk), at(lambda i, j, l: (i, l)))
    if layer is None:
        b_spec = (pl.BlockSpec((tn, tk), at(lambda i, j, l: (j, l))) if mode == "nt"
                  else pl.BlockSpec((tk, tn), at(lambda i, j, l: (l, j))))
    elif mode == "nt":
        b_spec = pl.BlockSpec((None, tn, tk), at(lambda i, j, l: (layer, j, l)))
    else:
        b_spec = pl.BlockSpec((None, tk, tn), at(lambda i, j, l: (layer, l, j)))
    in_specs, args = [a_spec, b_spec], [a, b]
    if bias is not None:
        in_specs.append(pl.BlockSpec((1, tn), at(lambda i, j, l: (0, j))))
        args.append(bias)
    if residual is not None:
        in_specs.append(pl.BlockSpec((tm, tn), at(lambda i, j, l: (i, j))))
        args.append(residual)
    has_bias, has_res = bias is not None, residual is not None

    def body(*refs):
        a_ref, b_ref = refs[0], refs[1]
        pos = 2
        bias_ref = res_ref = None
        if has_bias:
            bias_ref, pos = refs[pos], pos + 1
        if has_res:
            res_ref, pos = refs[pos], pos + 1
        o_ref = refs[pos]
        acc_ref = refs[pos + 1] if nk > 1 else None

        def finish(acc):
            if has_bias:
                acc = acc + bias_ref[...]
            if has_res:
                acc = acc + res_ref[...]
            o_ref[...] = acc.astype(o_ref.dtype)

        if nk == 1:
            finish(lax.dot_general(a_ref[...], b_ref[...], _DIMS[mode], preferred_element_type=F32))
            return
        l = pl.program_id(2)
        part = lax.dot_general(a_ref[...], b_ref[...], _DIMS[mode], preferred_element_type=F32)

        @pl.when(l == 0)
        def _():
            acc_ref[...] = part

        @pl.when(l > 0)
        def _():
            acc_ref[...] += part

        @pl.when(l == nk - 1)
        def _():
            finish(acc_ref[...])

    return pl.pallas_call(
        body,
        name=name,
        grid=(n // tn, m // tm, nk) if n_outer else (m // tm, n // tn, nk),
        in_specs=in_specs,
        out_specs=pl.BlockSpec((tm, tn), at(lambda i, j, l: (i, j))),
        out_shape=jax.ShapeDtypeStruct((m, n), out_dtype),
        scratch_shapes=[pltpu.VMEM((tm, tn), F32)] if nk > 1 else [],
        compiler_params=_params(("parallel", "parallel", "arbitrary")),
    )(*args)


def _fold8(v):
    r, d = v.shape
    return jnp.sum(v.reshape(r // 8, 8, d), axis=0)


def _rms_fwd(x, g, name):
    t, d = x.shape
    tm = _pick(t, ROW_TILES)

    def body(x_ref, g_ref, h_ref):
        xv = x_ref[...]
        r = lax.rsqrt(jnp.mean(xv * xv, axis=-1, keepdims=True) + EPS)
        h_ref[...] = (xv * r * g_ref[...]).astype(BF16)

    return pl.pallas_call(
        body,
        name=name,
        grid=(t // tm,),
        in_specs=[pl.BlockSpec((tm, d), lambda i: (i, 0)), pl.BlockSpec((1, d), lambda i: (0, 0))],
        out_specs=pl.BlockSpec((tm, d), lambda i: (i, 0)),
        out_shape=jax.ShapeDtypeStruct((t, d), BF16),
        compiler_params=_params(("parallel",)),
    )(x, g)


def _rms_bwd(x, dh, g, dx_in, name):
    t, d = x.shape
    tm = _pick(t, ROW_TILES)

    def body(x_ref, dh_ref, g_ref, dxi_ref, dx_ref, dxb_ref, dg_ref, cs_ref):
        i = pl.program_id(0)
        xv = x_ref[...]
        r = lax.rsqrt(jnp.mean(xv * xv, axis=-1, keepdims=True) + EPS)
        xhat = xv * r
        dy = dh_ref[...]
        gdy = dy * g_ref[...]
        dx = dxi_ref[...] + r * (gdy - xhat * jnp.mean(gdy * xhat, axis=-1, keepdims=True))
        dx_ref[...] = dx
        dxb_ref[...] = dx.astype(BF16)

        @pl.when(i == 0)
        def _():
            dg_ref[...] = jnp.zeros_like(dg_ref)
            cs_ref[...] = jnp.zeros_like(cs_ref)

        dg_ref[...] += _fold8(dy * xhat)
        cs_ref[...] += _fold8(dx)

    row = pl.BlockSpec((tm, d), lambda i: (i, 0))
    acc = pl.BlockSpec((8, d), lambda i: (0, 0))
    return pl.pallas_call(
        body,
        name=name,
        grid=(t // tm,),
        in_specs=[row, row, pl.BlockSpec((1, d), lambda i: (0, 0)), row],
        out_specs=[row, row, acc, acc],
        out_shape=[jax.ShapeDtypeStruct((t, d), F32), jax.ShapeDtypeStruct((t, d), BF16),
                   jax.ShapeDtypeStruct((8, d), F32), jax.ShapeDtypeStruct((8, d), F32)],
        compiler_params=_params(("arbitrary",)),
    )(x, dh, g, dx_in)


def _loss_head(x, target, g, name):
    t, d = x.shape
    tm = _pick(t, ROW_TILES)
    inv_d = 1.0 / d

    def body(x_ref, t_ref, g_ref, dx_ref, dxb_ref, sq_ref, dg_ref):
        i = pl.program_id(0)
        xv = x_ref[...]
        gv = g_ref[...]
        r = lax.rsqrt(jnp.mean(xv * xv, axis=-1, keepdims=True) + EPS)
        xhat = xv * r
        err = xhat * gv - t_ref[...]
        dy = err * inv_d
        gdy = dy * gv
        dx = r * (gdy - xhat * jnp.mean(gdy * xhat, axis=-1, keepdims=True))
        dx_ref[...] = dx
        dxb_ref[...] = dx.astype(BF16)

        @pl.when(i == 0)
        def _():
            sq_ref[...] = jnp.zeros_like(sq_ref)
            dg_ref[...] = jnp.zeros_like(dg_ref)

        sq_ref[...] += _fold8(err * err)
        dg_ref[...] += _fold8(dy * xhat)

    row = pl.BlockSpec((tm, d), lambda i: (i, 0))
    acc = pl.BlockSpec((8, d), lambda i: (0, 0))
    return pl.pallas_call(
        body,
        name=name,
        grid=(t // tm,),
        in_specs=[row, row, pl.BlockSpec((1, d), lambda i: (0, 0))],
        out_specs=[row, row, acc, acc],
        out_shape=[jax.ShapeDtypeStruct((t, d), F32), jax.ShapeDtypeStruct((t, d), BF16),
                   jax.ShapeDtypeStruct((8, d), F32), jax.ShapeDtypeStruct((8, d), F32)],
        compiler_params=_params(("arbitrary",)),
    )(x, target, g)


def _rows(tm):
    return lax.broadcasted_iota(jnp.int32, (tm, 1), 0)


def _shift_down(u, before2):
    r8 = _rows(8)
    s1, s2 = pltpu.roll(u, 1, 0), pltpu.roll(u, 2, 0)
    top1 = jnp.where(r8 == 0, before2[1:2], s1[:8])
    top2 = jnp.where(r8 == 0, before2[0:1], jnp.where(r8 == 1, before2[1:2], s2[:8]))
    return jnp.concatenate([top1, s1[8:]], axis=0), jnp.concatenate([top2, s2[8:]], axis=0)


def _shift_up(u, after2):
    tm = u.shape[0]
    r8 = _rows(8)
    s1, s2 = pltpu.roll(u, tm - 1, 0), pltpu.roll(u, tm - 2, 0)
    bot1 = jnp.where(r8 == 7, after2[0:1], s1[tm - 8:])
    bot2 = jnp.where(r8 == 6, after2[0:1], jnp.where(r8 == 7, after2[1:2], s2[tm - 8:]))
    return jnp.concatenate([s1[:tm - 8], bot1], axis=0), jnp.concatenate([s2[:tm - 8], bot2], axis=0)


def _shift_matrix(tm, up):
    r = lax.broadcasted_iota(jnp.int32, (2 * tm, tm), 0)
    c = lax.broadcasted_iota(jnp.int32, (2 * tm, tm), 1)
    t = jnp.where(r >= tm, r - tm, r)
    k = jnp.where(r >= tm, 2, 1)
    return (c == (t + k if up else t - k)).astype(BF16)


def _shift_down_mxu(u, before2):
    tm = u.shape[0]
    moved = jnp.dot(_shift_matrix(tm, False), u.astype(BF16), preferred_element_type=F32)
    r8 = _rows(8)
    s1, s2 = moved[:tm], moved[tm:]
    top1 = s1[:8] + jnp.where(r8 == 0, before2[1:2], 0.0)
    top2 = s2[:8] + jnp.where(r8 == 0, before2[0:1], jnp.where(r8 == 1, before2[1:2], 0.0))
    return jnp.concatenate([top1, s1[8:]], axis=0), jnp.concatenate([top2, s2[8:]], axis=0)


def _shift_up_mxu(u, after2):
    tm = u.shape[0]
    moved = jnp.dot(_shift_matrix(tm, True), u.astype(BF16), preferred_element_type=F32)
    r8 = _rows(8)
    s1, s2 = moved[:tm], moved[tm:]
    bot1 = s1[tm - 8:] + jnp.where(r8 == 7, after2[0:1], 0.0)
    bot2 = s2[tm - 8:] + jnp.where(r8 == 6, after2[0:1], jnp.where(r8 == 7, after2[1:2], 0.0))
    return jnp.concatenate([s1[:tm - 8], bot1], axis=0), jnp.concatenate([s2[:tm - 8], bot2], axis=0)


def _conv_tile(seq):
    return _pick(seq, (256, 128, 64, 32, 16, 8))


def _halo_specs(tm, width, n_tiles):
    per = tm // HALO
    before = pl.BlockSpec((HALO, width), lambda i: (jnp.maximum(i * per - 1, 0), 0))
    after = pl.BlockSpec((HALO, width), lambda i: (jnp.minimum((i + 1) * per, n_tiles * per - 1), 0))
    return before, after


def _convgate_fwd(bcv, w, seq, name):
    t, d3 = bcv.shape
    d = d3 // 3
    tm = _conv_tile(seq)
    tps = seq // tm
    before, _ = _halo_specs(tm, d3, t // tm)

    def body(x_ref, xb_ref, w_ref, y_ref):
        i = pl.program_id(0)
        inner = (i % tps != 0).astype(F32)
        u = x_ref[:, d:2 * d].astype(F32) * x_ref[:, 2 * d:].astype(F32)
        xb = xb_ref[:, d:].astype(F32)[HALO - 2:]
        s1, s2 = _shift_down(u, xb[:, :d] * xb[:, d:] * inner)
        z = w_ref[2:3] * u + w_ref[1:2] * s1 + w_ref[0:1] * s2
        y_ref[...] = (x_ref[:, :d].astype(F32) * z).astype(BF16)

    return pl.pallas_call(
        body,
        name=name,
        grid=(t // tm,),
        in_specs=[pl.BlockSpec((tm, d3), lambda i: (i, 0)), before, pl.BlockSpec((3, d), lambda i: (0, 0))],
        out_specs=pl.BlockSpec((tm, d), lambda i: (i, 0)),
        out_shape=jax.ShapeDtypeStruct((t, d), BF16),
        compiler_params=_params(("parallel",)),
    )(bcv, bcv, w)


def _convgate_bwd(bcv, dy, w, seq, name):
    t, d3 = bcv.shape
    d = d3 // 3
    tm = _conv_tile(seq)
    tps = seq // tm
    before, after = _halo_specs(tm, d3, t // tm)
    _, after_dy = _halo_specs(tm, d, t // tm)

    def body(x_ref, xb_ref, xa_ref, dy_ref, dya_ref, w_ref, dx_ref, dw_ref):
        i = pl.program_id(0)
        inner_lo = (i % tps != 0).astype(F32)
        inner_hi = (i % tps != tps - 1).astype(F32)
        w0, w1, w2 = w_ref[0:1], w_ref[1:2], w_ref[2:3]
        b, c, v = x_ref[:, :d].astype(F32), x_ref[:, d:2 * d].astype(F32), x_ref[:, 2 * d:].astype(F32)
        u = c * v
        xb = xb_ref[:, d:].astype(F32)[HALO - 2:]
        s1, s2 = _shift_down(u, xb[:, :d] * xb[:, d:] * inner_lo)
        z = w2 * u + w1 * s1 + w0 * s2
        dyv = dy_ref[...].astype(F32)
        dz = dyv * b
        dza = dya_ref[...].astype(F32)[0:2] * xa_ref[:, :d].astype(F32)[0:2] * inner_hi
        n1, n2 = _shift_up(dz, dza)
        du = w2 * dz + w1 * n1 + w0 * n2
        dx_ref[:, :d] = (dyv * z).astype(BF16)
        dx_ref[:, d:2 * d] = (du * v).astype(BF16)
        dx_ref[:, 2 * d:] = (du * c).astype(BF16)

        @pl.when(i == 0)
        def _():
            dw_ref[...] = jnp.zeros_like(dw_ref)

        dw_ref[0:1] += jnp.sum(dz * s2, axis=0, keepdims=True)
        dw_ref[1:2] += jnp.sum(dz * s1, axis=0, keepdims=True)
        dw_ref[2:3] += jnp.sum(dz * u, axis=0, keepdims=True)

    return pl.pallas_call(
        body,
        name=name,
        grid=(t // tm,),
        in_specs=[pl.BlockSpec((tm, d3), lambda i: (i, 0)), before, after,
                  pl.BlockSpec((tm, d), lambda i: (i, 0)), after_dy, pl.BlockSpec((3, d), lambda i: (0, 0))],
        out_specs=[pl.BlockSpec((tm, d3), lambda i: (i, 0)), pl.BlockSpec((8, d), lambda i: (0, 0))],
        out_shape=[jax.ShapeDtypeStruct((t, d3), BF16), jax.ShapeDtypeStruct((8, d), F32)],
        compiler_params=_params(("arbitrary",)),
    )(bcv, bcv, bcv, dy, dy, w)


def _sigmoid(x):
    return 1.0 / (1.0 + jnp.exp(-x))


def _ffngate_fwd(gu, w, seq, name):
    t, f2 = gu.shape
    f = f2 // 2
    tm = _conv_tile(seq)
    tps = seq // tm
    before, _ = _halo_specs(tm, f2, t // tm)

    def body(x_ref, xb_ref, w_ref, a_ref):
        i = pl.program_id(0)
        inner = (i % tps != 0).astype(F32)
        s1, s2 = _shift_down_mxu(x_ref[:, :f], xb_ref[:, :f].astype(F32)[HALO - 2:] * inner)
        gc = w_ref[2:3] * x_ref[:, :f].astype(F32) + w_ref[1:2] * s1 + w_ref[0:1] * s2
        a_ref[...] = (gc * _sigmoid(gc) * x_ref[:, f:].astype(F32)).astype(BF16)

    return pl.pallas_call(
        body,
        name=name,
        grid=(t // tm,),
        in_specs=[pl.BlockSpec((tm, f2), lambda i: (i, 0)), before, pl.BlockSpec((3, f), lambda i: (0, 0))],
        out_specs=pl.BlockSpec((tm, f), lambda i: (i, 0)),
        out_shape=jax.ShapeDtypeStruct((t, f), BF16),
        compiler_params=_params(("parallel",)),
    )(gu, gu, w)


def _ffngate_bwd(gu, da, w, seq, name):
    t, f2 = gu.shape
    f = f2 // 2
    tm = _conv_tile(seq)
    tps = seq // tm
    before, after = _halo_specs(tm, f2, t // tm)
    _, after_da = _halo_specs(tm, f, t // tm)

    def body(x_ref, xb_ref, xa_ref, da_ref, daa_ref, w_ref, dx_ref, dw_ref):
        i = pl.program_id(0)
        inner_lo = (i % tps != 0).astype(F32)
        inner_hi = (i % tps != tps - 1).astype(F32)
        w0, w1, w2 = w_ref[0:1], w_ref[1:2], w_ref[2:3]

        def dgate(gc, uv, dav):
            sg = _sigmoid(gc)
            return dav * uv * (sg * (1.0 + gc * (1.0 - sg))), dav * (gc * sg)

        g, u = x_ref[:, :f].astype(F32), x_ref[:, f:].astype(F32)
        s1, s2 = _shift_down_mxu(x_ref[:, :f], xb_ref[:, :f].astype(F32)[HALO - 2:] * inner_lo)
        gc = w2 * g + w1 * s1 + w0 * s2
        dgc, du = dgate(gc, u, da_ref[...].astype(F32))
        ga = xa_ref[:, :f].astype(F32)
        a1, a2 = _shift_down(ga, x_ref[tm - HALO:, :f].astype(F32)[HALO - 2:])
        gca = w2 * ga + w1 * a1 + w0 * a2
        dgca, _ = dgate(gca, xa_ref[:, f:].astype(F32), daa_ref[...].astype(F32))
        n1, n2 = _shift_up_mxu(dgc, dgca[0:2] * inner_hi)
        dx_ref[:, :f] = (w2 * dgc + w1 * n1 + w0 * n2).astype(BF16)
        dx_ref[:, f:] = du.astype(BF16)

        @pl.when(i == 0)
        def _():
            dw_ref[...] = jnp.zeros_like(dw_ref)

        dw_ref[0:1] += jnp.sum(dgc * s2, axis=0, keepdims=True)
        dw_ref[1:2] += jnp.sum(dgc * s1, axis=0, keepdims=True)
        dw_ref[2:3] += jnp.sum(dgc * g, axis=0, keepdims=True)

    return pl.pallas_call(
        body,
        name=name,
        grid=(t // tm,),
        in_specs=[pl.BlockSpec((tm, f2), lambda i: (i, 0)), before, after,
                  pl.BlockSpec((tm, f), lambda i: (i, 0)), after_da, pl.BlockSpec((3, f), lambda i: (0, 0))],
        out_specs=[pl.BlockSpec((tm, f2), lambda i: (i, 0)), pl.BlockSpec((8, f), lambda i: (0, 0))],
        out_shape=[jax.ShapeDtypeStruct((t, f2), BF16), jax.ShapeDtypeStruct((8, f), F32)],
        compiler_params=_params(("arbitrary",)),
    )(gu, gu, gu, da, da, w)


def _swap_halves(xt):
    half = HEAD_DIM // 2
    return jnp.concatenate([xt[half:], xt[:half]], axis=0)


def _rope(xt, cos, sin):
    return xt * cos + _swap_halves(xt) * sin


def _unrope(dxt, cos, sin):
    return dxt * cos - _swap_halves(dxt) * sin


def _key_query(count):
    kj = lax.broadcasted_iota(jnp.int32, (WINDOW, count * WINDOW), 0)
    qi = lax.broadcasted_iota(jnp.int32, (WINDOW, count * WINDOW), 1) & (WINDOW - 1)
    return kj, qi


def _band_masks(n, count):
    kj, qi = _key_query(count)
    return kj <= qi, jnp.logical_and(kj > qi, n > 0)


def _lanes(v, count):
    return jnp.concatenate([v] * count, axis=1) if count > 1 else v


def _heads(ref, h0, count):
    parts = [ref[(h0 + g) * HEAD_DIM:(h0 + g + 1) * HEAD_DIM, :] for g in range(count)]
    return jnp.concatenate(parts, axis=1) if count > 1 else parts[0]


def _head_rows(ref, h0, count):
    parts = [ref[h0 + g:h0 + g + 1, :] for g in range(count)]
    return jnp.concatenate(parts, axis=1) if count > 1 else parts[0]


def _head_sinks(sink_ref, h0, count):
    parts = [jnp.full((1, WINDOW), sink_ref[h0 + g], F32) for g in range(count)]
    return jnp.concatenate(parts, axis=1) if count > 1 else parts[0]


def _tn(a, b):
    return lax.dot_general(a, b, _DIMS["tn"], preferred_element_type=F32)


def _nt(a, b):
    return lax.dot_general(a, b, _DIMS["nt"], preferred_element_type=F32)


def _nn(a, b):
    return jnp.dot(a, b, preferred_element_type=F32)


def _attn_fwd(qkv, sinks, cos_t, sin_t, bsz, seq, name, hp):
    t, qw = qkv.shape
    d = qw * 2 // 3
    kvw = d // GROUP
    n_heads, n_kv = d // HEAD_DIM, kvw // HEAD_DIM
    nb = seq // WINDOW
    scale = HEAD_DIM ** -0.5

    def body(sink_ref, xc_ref, xp_ref, cc_ref, sc_ref, cp_ref, sp_ref, o_ref, lse_ref, xt_ref, pt_ref, ot_ref):
        n = pl.program_id(1)
        xt_ref[...] = xc_ref[...].T
        pt_ref[...] = xp_ref[:, d:].T
        cos_c, sin_c, cos_p, sin_p = cc_ref[...], sc_ref[...], cp_ref[...], sp_ref[...]
        cos_g, sin_g = _lanes(cos_c, hp), _lanes(sin_c, hp)
        valid_c, valid_p = _band_masks(n, hp)
        for j in range(n_kv):
            ko = j * HEAD_DIM
            kc = _rope(xt_ref[d + ko:d + ko + HEAD_DIM, :], cos_c, sin_c).astype(BF16)
            kp = _rope(pt_ref[ko:ko + HEAD_DIM, :], cos_p, sin_p).astype(BF16)
            vc = xt_ref[d + kvw + ko:d + kvw + ko + HEAD_DIM, :].astype(BF16)
            vp = pt_ref[kvw + ko:kvw + ko + HEAD_DIM, :].astype(BF16)
            for h0 in range(j * GROUP, (j + 1) * GROUP, hp):
                q = _rope(_heads(xt_ref, h0, hp), cos_g, sin_g).astype(BF16)
                sink = _head_sinks(sink_ref, h0, hp)
                s_c = jnp.where(valid_c, _tn(kc, q) * scale, NEG)
                s_p = jnp.where(valid_p, _tn(kp, q) * scale, NEG)
                m = jnp.maximum(jnp.maximum(jnp.max(s_c, axis=0, keepdims=True), jnp.max(s_p, axis=0, keepdims=True)), sink)
                p_c = jnp.exp(s_c - m)
                p_p = jnp.exp(s_p - m)
                den = jnp.sum(p_c, axis=0, keepdims=True) + jnp.sum(p_p, axis=0, keepdims=True) + jnp.exp(sink - m)
                inv = 1.0 / den
                o_g = _nn(vc, (p_c * inv).astype(BF16)) + _nn(vp, (p_p * inv).astype(BF16))
                lse_g = m + jnp.log(den)
                for g in range(hp):
                    h = h0 + g
                    ot_ref[h * HEAD_DIM:(h + 1) * HEAD_DIM, :] = o_g[:, g * WINDOW:(g + 1) * WINDOW]
                    lse_ref[h:h + 1, :] = lse_g[:, g * WINDOW:(g + 1) * WINDOW]
        o_ref[...] = ot_ref[...].T.astype(BF16)

    cur = lambda b, n: (b * nb + n, 0)
    prev = lambda b, n: (b * nb + jnp.maximum(n - 1, 0), 0)
    tab_c = pl.BlockSpec((HEAD_DIM, WINDOW), lambda b, n: (0, n))
    tab_p = pl.BlockSpec((HEAD_DIM, WINDOW), lambda b, n: (0, jnp.maximum(n - 1, 0)))
    return pl.pallas_call(
        body,
        name=name,
        grid=(bsz, nb),
        in_specs=[SMEM, pl.BlockSpec((WINDOW, qw), cur), pl.BlockSpec((WINDOW, qw), prev), tab_c, tab_c, tab_p, tab_p],
        out_specs=[pl.BlockSpec((WINDOW, d), cur), pl.BlockSpec((n_heads, WINDOW), lambda b, n: (0, b * nb + n))],
        out_shape=[jax.ShapeDtypeStruct((t, d), BF16), jax.ShapeDtypeStruct((n_heads, t), F32)],
        scratch_shapes=[pltpu.VMEM((qw, WINDOW), F32), pltpu.VMEM((2 * kvw, WINDOW), F32), pltpu.VMEM((d, WINDOW), F32)],
        compiler_params=_params(("parallel", "arbitrary")),
    )(sinks, qkv, qkv, cos_t, sin_t, cos_t, sin_t)


def _attn_bwd(qkv, o, lse, do, sinks, cos_t, sin_t, bsz, seq, name, hp):
    t, qw = qkv.shape
    d = qw * 2 // 3
    kvw = d // GROUP
    n_heads, n_kv = d // HEAD_DIM, kvw // HEAD_DIM
    nb = seq // WINDOW
    scale = HEAD_DIM ** -0.5

    def body(sink_ref, xc_ref, xp_ref, xn_ref, oc_ref, on_ref, doc_ref, don_ref, lc_ref, ln_ref,
             cc_ref, sc_ref, cp_ref, sp_ref, cn_ref, sn_ref,
             dx_ref, db_ref, dsk_ref, xt_ref, pt_ref, qn_ref, otc_ref, otn_ref, dtc_ref, dtn_ref, gt_ref):
        b, n = pl.program_id(0), pl.program_id(1)
        xt_ref[...] = xc_ref[...].T
        pt_ref[...] = xp_ref[:, d:].T
        qn_ref[...] = xn_ref[:, :d].T
        otc_ref[...] = oc_ref[...].astype(F32).T
        otn_ref[...] = on_ref[...].astype(F32).T
        dtc_ref[...] = doc_ref[...].T
        dtn_ref[...] = don_ref[...].T
        cos_c, sin_c, cos_p, sin_p, cos_n, sin_n = (cc_ref[...], sc_ref[...], cp_ref[...], sp_ref[...],
                                                    cn_ref[...], sn_ref[...])
        cos_g, sin_g, cos_gn, sin_gn = _lanes(cos_c, hp), _lanes(sin_c, hp), _lanes(cos_n, hp), _lanes(sin_n, hp)
        valid_c, valid_p = _band_masks(n, hp)
        kj, qi = _key_query(hp)
        valid_n = jnp.logical_and(kj > qi, n < nb - 1)

        @pl.when(jnp.logical_and(b == 0, n == 0))
        def _():
            db_ref[...] = jnp.zeros_like(db_ref)
            dsk_ref[...] = jnp.zeros_like(dsk_ref)

        for j in range(n_kv):
            ko = j * HEAD_DIM
            kc = _rope(xt_ref[d + ko:d + ko + HEAD_DIM, :], cos_c, sin_c).astype(BF16)
            kp = _rope(pt_ref[ko:ko + HEAD_DIM, :], cos_p, sin_p).astype(BF16)
            vc = xt_ref[d + kvw + ko:d + kvw + ko + HEAD_DIM, :].astype(BF16)
            vp = pt_ref[kvw + ko:kvw + ko + HEAD_DIM, :].astype(BF16)
            dk = jnp.zeros((HEAD_DIM, WINDOW), F32)
            dv = jnp.zeros((HEAD_DIM, WINDOW), F32)
            for h0 in range(j * GROUP, (j + 1) * GROUP, hp):
                q = _rope(_heads(xt_ref, h0, hp), cos_g, sin_g).astype(BF16)
                do_g = _heads(dtc_ref, h0, hp)
                do_b = do_g.astype(BF16)
                lse_g = _head_rows(lc_ref, h0, hp)
                delta = jnp.sum(_heads(otc_ref, h0, hp) * do_g, axis=0, keepdims=True)
                p_c = jnp.exp(jnp.where(valid_c, _tn(kc, q) * scale, NEG) - lse_g)
                p_p = jnp.exp(jnp.where(valid_p, _tn(kp, q) * scale, NEG) - lse_g)
                ds_c = (p_c * (_tn(vc, do_b) - delta)).astype(BF16)
                ds_p = (p_p * (_tn(vp, do_b) - delta)).astype(BF16)
                dq = _unrope((_nn(kc, ds_c) + _nn(kp, ds_p)) * scale, cos_g, sin_g)
                dsk = -jnp.exp(_head_sinks(sink_ref, h0, hp) - lse_g) * delta
                for g in range(hp):
                    h = h0 + g
                    gt_ref[h * HEAD_DIM:(h + 1) * HEAD_DIM, :] = dq[:, g * WINDOW:(g + 1) * WINDOW]
                    dsk_ref[h:h + 1, :] += dsk[:, g * WINDOW:(g + 1) * WINDOW]
                q2 = _rope(_heads(qn_ref, h0, hp), cos_gn, sin_gn).astype(BF16)
                do2 = _heads(dtn_ref, h0, hp)
                do2_b = do2.astype(BF16)
                delta2 = jnp.sum(_heads(otn_ref, h0, hp) * do2, axis=0, keepdims=True)
                p_n = jnp.exp(jnp.where(valid_n, _tn(kc, q2) * scale, NEG) - _head_rows(ln_ref, h0, hp))
                ds_n = (p_n * (_tn(vc, do2_b) - delta2)).astype(BF16)
                dv += _nt(do_b, p_c.astype(BF16)) + _nt(do2_b, p_n.astype(BF16))
                dk += _nt(q, ds_c) + _nt(q2, ds_n)
            gt_ref[d + ko:d + ko + HEAD_DIM, :] = _unrope(dk * scale, cos_c, sin_c)
            gt_ref[d + kvw + ko:d + kvw + ko + HEAD_DIM, :] = dv
        dx = gt_ref[...].T
        dx_ref[...] = dx.astype(BF16)
        db_ref[...] += _fold8(dx)

    cur = lambda b, n: (b * nb + n, 0)
    prev = lambda b, n: (b * nb + jnp.maximum(n - 1, 0), 0)
    nxt = lambda b, n: (b * nb + jnp.minimum(n + 1, nb - 1), 0)
    stat_c = pl.BlockSpec((n_heads, WINDOW), lambda b, n: (0, b * nb + n))
    stat_n = pl.BlockSpec((n_heads, WINDOW), lambda b, n: (0, b * nb + jnp.minimum(n + 1, nb - 1)))
    tab_c = pl.BlockSpec((HEAD_DIM, WINDOW), lambda b, n: (0, n))
    tab_p = pl.BlockSpec((HEAD_DIM, WINDOW), lambda b, n: (0, jnp.maximum(n - 1, 0)))
    tab_n = pl.BlockSpec((HEAD_DIM, WINDOW), lambda b, n: (0, jnp.minimum(n + 1, nb - 1)))
    return pl.pallas_call(
        body,
        name=name,
        grid=(bsz, nb),
        in_specs=[SMEM, pl.BlockSpec((WINDOW, qw), cur), pl.BlockSpec((WINDOW, qw), prev), pl.BlockSpec((WINDOW, qw), nxt),
                  pl.BlockSpec((WINDOW, d), cur), pl.BlockSpec((WINDOW, d), nxt),
                  pl.BlockSpec((WINDOW, d), cur), pl.BlockSpec((WINDOW, d), nxt),
                  stat_c, stat_n, tab_c, tab_c, tab_p, tab_p, tab_n, tab_n],
        out_specs=[pl.BlockSpec((WINDOW, qw), cur), pl.BlockSpec((8, qw), lambda b, n: (0, 0)),
                   pl.BlockSpec((n_heads, WINDOW), lambda b, n: (0, 0))],
        out_shape=[jax.ShapeDtypeStruct((t, qw), BF16), jax.ShapeDtypeStruct((8, qw), F32),
                   jax.ShapeDtypeStruct((n_heads, WINDOW), F32)],
        scratch_shapes=[pltpu.VMEM((qw, WINDOW), F32), pltpu.VMEM((2 * kvw, WINDOW), F32), pltpu.VMEM((d, WINDOW), F32),
                        pltpu.VMEM((d, WINDOW), F32), pltpu.VMEM((d, WINDOW), F32), pltpu.VMEM((d, WINDOW), F32),
                        pltpu.VMEM((d, WINDOW), F32), pltpu.VMEM((qw, WINDOW), F32)],
        compiler_params=_params(("arbitrary", "arbitrary")),
    )(sinks, qkv, qkv, qkv, o, o, do, do, lse, lse, cos_t, sin_t, cos_t, sin_t, cos_t, sin_t)


def _place():
    return lax.axis_index("x"), lax.axis_index("y"), lax.axis_index("c")


def _other_chips(x, y):
    return [(1 - x, y), (x, 1 - y), (1 - x, 1 - y)]


def _place_shard(w, axis, q, name):
    ly, k, n = w.shape
    tr = _pick(k, (256, 128, 64, 32, 16, 8))
    steps = k // tr
    shape = (ly, k * N_CHIPS, n) if axis == 1 else (ly, k, n * N_CHIPS)
    if axis == 1:
        out_spec = pl.BlockSpec((None, tr, n), lambda l, i, q_ref: (l, q_ref[0] * steps + i, 0))
    else:
        out_spec = pl.BlockSpec((None, tr, n), lambda l, i, q_ref: (l, i, q_ref[0]))

    def body(q_ref, w_ref, o_ref):
        del q_ref
        o_ref[...] = w_ref[...].astype(BF16)

    return pl.pallas_call(
        body,
        name=name,
        grid_spec=pltpu.PrefetchScalarGridSpec(
            num_scalar_prefetch=1, grid=(ly, steps),
            in_specs=[pl.BlockSpec((None, tr, n), lambda l, i, q_ref: (l, i, 0))], out_specs=out_spec),
        out_shape=jax.ShapeDtypeStruct(shape, BF16),
        compiler_params=_params(("parallel", "parallel")),
    )(q, w)


def _half_block(ref, axis, layer, px, py, pc):
    blk = 2 * px + py
    if axis == 1:
        rows = ref.shape[1] // (2 * N_CHIPS)
        return ref.at[layer, pl.ds(pl.multiple_of((2 * blk + pc) * rows, 8), rows), :]
    rows, width = ref.shape[1] // 2, ref.shape[2] // N_CHIPS
    return ref.at[layer, pl.ds(pl.multiple_of(pc * rows, 8), rows), pl.ds(pl.multiple_of(blk * width, 128), width)]


def _gather_copy(refs, axes, pieces, send_sems, recv_sems, p, k, stage, whose):
    x, y, c = _place()
    chip = _other_chips(x, y)[k]
    i, layer = pieces[p]
    if stage == 0:
        origin = (x, y, c) if whose == "mine" else (*chip, c)
        to = (*chip, c)
    else:
        origin = (*chip, c) if whose == "mine" else (*chip, 1 - c)
        to = (x, y, 1 - c)
    blk = _half_block(refs[i], axes[i], layer, *origin)
    return pltpu.make_async_remote_copy(src_ref=blk, dst_ref=blk, send_sem=send_sems.at[p * 3 + k],
                                        recv_sem=recv_sems.at[p * 3 + k], device_id=to, device_id_type=MESH)


def _gather_weights(fulls, axes, pieces, name):
    n, m = len(fulls), 3 * len(pieces)

    def body(*refs):
        dst = refs[n:2 * n]
        sems = refs[2 * n:]
        todo = [(p, k) for p in range(len(pieces)) for k in range(3)]
        sends = [_gather_copy(dst, axes, pieces, sems[0], sems[1], p, k, 0, "mine") for p, k in todo]
        for cp in sends:
            cp.start()
        for p, k in todo:
            _gather_copy(dst, axes, pieces, sems[0], sems[1], p, k, 0, "theirs").wait_recv()
            sends.append(_gather_copy(dst, axes, pieces, sems[2], sems[3], p, k, 1, "mine"))
            sends[-1].start()
        for p, k in todo:
            _gather_copy(dst, axes, pieces, sems[2], sems[3], p, k, 1, "theirs").wait_recv()
        for cp in sends:
            cp.wait_send()

    return pl.pallas_call(
        body,
        name=name,
        in_specs=[ANY] * n,
        out_specs=[ANY] * n,
        out_shape=[jax.ShapeDtypeStruct(f.shape, f.dtype) for f in fulls],
        input_output_aliases={i: i for i in range(n)},
        scratch_shapes=[pltpu.SemaphoreType.DMA((m,))] * 4,
    )(*fulls)


HBM_SPEC = pl.BlockSpec(memory_space=pltpu.HBM)
SEM_SPEC = pl.BlockSpec(memory_space=pltpu.SEMAPHORE)


def _gather_start(fulls, axes, pieces, stage, name):
    n, m = len(fulls), 3 * len(pieces)

    def body(*refs):
        src = refs[:n]
        send_sems, recv_sems = refs[2 * n], refs[2 * n + 1]
        for p in range(len(pieces)):
            for k in range(3):
                _gather_copy(src, axes, pieces, send_sems, recv_sems, p, k, stage, "mine").start()

    out = pl.pallas_call(
        body,
        name=name,
        in_specs=[HBM_SPEC] * n,
        out_specs=[HBM_SPEC] * n + [SEM_SPEC, SEM_SPEC],
        out_shape=[pltpu.HBM(f.shape, f.dtype) for f in fulls] + [pltpu.SemaphoreType.DMA((m,)), pltpu.SemaphoreType.DMA((m,))],
        input_output_aliases={i: i for i in range(n)},
        compiler_params=pltpu.CompilerParams(has_side_effects=pltpu.SideEffectType.DATAFLOW_SIDE_EFFECTING),
    )(*[pltpu.with_memory_space_constraint(f, pltpu.HBM) for f in fulls])
    return list(out[:n]), out[n], out[n + 1]


def _gather_wait(fulls, send_sems, recv_sems, after, axes, pieces, stage, name):
    n = len(fulls)

    def body(*refs):
        src = refs[:n]
        s_sems, r_sems = refs[n], refs[n + 1]
        for p in range(len(pieces)):
            for k in range(3):
                _gather_copy(src, axes, pieces, s_sems, r_sems, p, k, stage, "mine").wait_send()
                _gather_copy(src, axes, pieces, s_sems, r_sems, p, k, stage, "theirs").wait_recv()

    out = pl.pallas_call(
        body,
        name=name,
        in_specs=[HBM_SPEC] * n + [SEM_SPEC, SEM_SPEC, ANY],
        out_specs=[HBM_SPEC] * n,
        out_shape=[pltpu.HBM(f.shape, f.dtype) for f in fulls],
        input_output_aliases={i: i for i in range(n)},
        compiler_params=pltpu.CompilerParams(has_side_effects=pltpu.SideEffectType.DATAFLOW_SIDE_EFFECTING),
    )(*fulls, send_sems, recv_sems, after)
    return list(out)


def _half_shape(kind, shape):
    if kind == "col":
        return (shape[0] // 2, shape[1])
    return (N_CHIPS, shape[1] // 2, shape[2])


def _half_of(kind, ref, h):
    if kind == "col":
        r = ref.shape[0] // 2
        return ref.at[pl.ds(pl.multiple_of(h * r, 8), r), :]
    r = ref.shape[1] // 2
    return ref.at[:, pl.ds(pl.multiple_of(h * r, 8), r), :]


def _pair_exchange(grads, kinds, name):
    n = len(grads)
    outs = [jax.ShapeDtypeStruct(_half_shape(kd, g.shape), g.dtype) for g, kd in zip(grads, kinds)]

    def body(*refs):
        src, dst = refs[:n], refs[n:2 * n]
        send_sems, recv_sems = refs[2 * n:]
        x, y, c = _place()
        copies = [pltpu.make_async_remote_copy(
            src_ref=_half_of(kinds[i], src[i], 1 - c), dst_ref=dst[i],
            send_sem=send_sems.at[i], recv_sem=recv_sems.at[i], device_id=(x, y, 1 - c), device_id_type=MESH)
            for i in range(n)]
        for cp in copies:
            cp.start()
        for cp in copies:
            cp.wait_recv()
        for cp in copies:
            cp.wait_send()

    return pl.pallas_call(
        body,
        name=name,
        in_specs=[ANY] * n,
        out_specs=[ANY] * n,
        out_shape=outs,
        scratch_shapes=[pltpu.SemaphoreType.DMA((n,)), pltpu.SemaphoreType.DMA((n,))],
    )(*grads)


def _slot_shape(kind, shape):
    return (shape[0] // 2, shape[1] // N_CHIPS) if kind == "col" else (shape[1] // 2, shape[2])


def _pair_sum(grad, recv, kind, c, own, layer, name):
    r, cols = own.shape[2:]
    if kind == "col":
        tr = _pick(r, (256, 128, 64, 32, 16, 8))
        steps = r // tr
        grid = (N_CHIPS, steps)
        g_spec = pl.BlockSpec((tr, cols), lambda s, i, c_ref: (c_ref[0] * steps + i, s))
        r_spec = pl.BlockSpec((tr, cols), lambda s, i, c_ref: (i, s))
        o_spec = pl.BlockSpec((None, None, tr, cols), lambda s, i, c_ref: (layer, s, i, 0))
        g_in = grad
    else:
        grid = (N_CHIPS, 1)
        g_spec = pl.BlockSpec((None, None, r, cols), lambda s, i, c_ref: (s, c_ref[0], 0, 0))
        r_spec = pl.BlockSpec((None, r, cols), lambda s, i, c_ref: (s, 0, 0))
        o_spec = pl.BlockSpec((None, None, r, cols), lambda s, i, c_ref: (layer, s, 0, 0))
        g_in = grad.reshape(N_CHIPS, 2, r, cols)

    def body(c_ref, g_ref, r_ref, own_ref, o_ref):
        del c_ref, own_ref
        o_ref[...] = (g_ref[...].astype(F32) + r_ref[...].astype(F32)).astype(o_ref.dtype)

    return pl.pallas_call(
        body,
        name=name,
        grid_spec=pltpu.PrefetchScalarGridSpec(num_scalar_prefetch=1, grid=grid, in_specs=[g_spec, r_spec, ANY], out_specs=o_spec),
        out_shape=jax.ShapeDtypeStruct(own.shape, own.dtype),
        input_output_aliases={3: 0},
        compiler_params=_params(("parallel", "parallel")),
    )(c, g_in, recv, own)


def _scatter_copy(own, mine, sib, pieces, send_sems, recv_sems, p, k, stage, whose):
    x, y, c = _place()
    q = 2 * x + y
    i, layer = pieces[p]
    per = 4 if stage == 0 else 3
    if k == 3:
        src, dst, to = own[i].at[layer, q], sib[i].at[layer, q], (x, y, 1 - c)
    else:
        chip = _other_chips(x, y)[k]
        slot = 2 * chip[0] + chip[1]
        if stage == 0:
            to = (*chip, c)
            src, dst = (own[i].at[layer, slot], mine[i].at[layer, q]) if whose == "mine" else (own[i].at[layer, q], mine[i].at[layer, slot])
        else:
            to = (x, y, 1 - c)
            src, dst = mine[i].at[layer, slot], sib[i].at[layer, slot]
    return pltpu.make_async_remote_copy(src_ref=src, dst_ref=dst, send_sem=send_sems.at[p * per + k],
                                        recv_sem=recv_sems.at[p * per + k], device_id=to, device_id_type=MESH)


def _scatter_start(own, mine, sib, token, pieces, stage, name):
    n = len(own)
    per = 4 if stage == 0 else 3
    m = per * len(pieces)
    n_arr = 3 * n + 1

    def body(*refs):
        o, mi, si = refs[:n], refs[n:2 * n], refs[2 * n:3 * n]
        send_sems, recv_sems = refs[2 * n_arr], refs[2 * n_arr + 1]
        for p in range(len(pieces)):
            for k in range(per):
                _scatter_copy(o, mi, si, pieces, send_sems, recv_sems, p, k, stage, "mine").start()

    arrays = list(own) + list(mine) + list(sib) + [token]
    out = pl.pallas_call(
        body,
        name=name,
        in_specs=[HBM_SPEC] * n_arr,
        out_specs=[HBM_SPEC] * n_arr + [SEM_SPEC, SEM_SPEC],
        out_shape=[pltpu.HBM(a.shape, a.dtype) for a in arrays] + [pltpu.SemaphoreType.DMA((m,)), pltpu.SemaphoreType.DMA((m,))],
        input_output_aliases={i: i for i in range(n_arr)},
        compiler_params=pltpu.CompilerParams(has_side_effects=pltpu.SideEffectType.DATAFLOW_SIDE_EFFECTING),
    )(*[pltpu.with_memory_space_constraint(a, pltpu.HBM) for a in arrays])
    return list(out[:n]), list(out[n:2 * n]), list(out[2 * n:3 * n]), out[3 * n], out[n_arr], out[n_arr + 1]


def _scatter_wait(own, mine, sib, token, send_sems, recv_sems, pieces, stage, name):
    n = len(own)
    per = 4 if stage == 0 else 3
    n_arr = 3 * n + 1

    def body(*refs):
        o, mi, si = refs[:n], refs[n:2 * n], refs[2 * n:3 * n]
        s_sems, r_sems = refs[n_arr], refs[n_arr + 1]
        for p in range(len(pieces)):
            for k in range(per):
                _scatter_copy(o, mi, si, pieces, s_sems, r_sems, p, k, stage, "mine").wait_send()
                _scatter_copy(o, mi, si, pieces, s_sems, r_sems, p, k, stage, "theirs").wait_recv()

    arrays = list(own) + list(mine) + list(sib) + [token]
    out = pl.pallas_call(
        body,
        name=name,
        in_specs=[HBM_SPEC] * n_arr + [SEM_SPEC, SEM_SPEC],
        out_specs=[HBM_SPEC] * n_arr,
        out_shape=[pltpu.HBM(a.shape, a.dtype) for a in arrays],
        input_output_aliases={i: i for i in range(n_arr)},
        compiler_params=pltpu.CompilerParams(has_side_effects=pltpu.SideEffectType.DATAFLOW_SIDE_EFFECTING),
    )(*arrays, send_sems, recv_sems)
    return list(out[:n]), list(out[n:2 * n]), list(out[2 * n:3 * n]), out[3 * n]


def _reduce_adamw(own, mine, sib, w, m, v, qc, name):
    ly, _, r, cols = mine.shape
    tr = _pick(r, (128, 64, 32, 16, 8))
    steps = r // tr
    c1 = 1.0 - ADAM_B1 ** ADAM_STEP
    c2 = 1.0 - ADAM_B2 ** ADAM_STEP

    def body(qc_ref, own_ref, mine_ref, sib_ref, w_ref, m_ref, v_ref, g_ref, d_ref, nm_ref, nv_ref):
        q = qc_ref[0]
        mine_sum = sib_sum = None
        for s in range(N_CHIPS):
            a = jnp.where(q == s, own_ref[...], mine_ref[s]).astype(F32)
            b = sib_ref[s].astype(F32)
            mine_sum = a if s == 0 else mine_sum + a
            sib_sum = b if s == 0 else sib_sum + b
        gv = jnp.where(pl.program_id(1) == qc_ref[1], mine_sum, sib_sum)
        nm = ADAM_B1 * m_ref[...] + (1.0 - ADAM_B1) * gv
        nv = ADAM_B2 * v_ref[...] + (1.0 - ADAM_B2) * (gv * gv)
        g_ref[...] = gv
        d_ref[...] = -ADAM_LR * ((nm / c1) / (jnp.sqrt(nv / c2) + ADAM_EPS) + ADAM_WD * w_ref[...])
        nm_ref[...] = nm
        nv_ref[...] = nv

    own_spec = pl.BlockSpec((None, None, tr, cols), lambda l, h, i, qc_ref: (l, qc_ref[0], i, 0))
    slot_spec = pl.BlockSpec((None, N_CHIPS, tr, cols), lambda l, h, i, qc_ref: (l, 0, i, 0))
    spec = pl.BlockSpec((None, tr, cols), lambda l, h, i, qc_ref: (l, h * steps + i, 0))
    shp = jax.ShapeDtypeStruct(w.shape, F32)
    return pl.pallas_call(
        body,
        name=name,
        grid_spec=pltpu.PrefetchScalarGridSpec(
            num_scalar_prefetch=1, grid=(ly, N_CORES, steps),
            in_specs=[own_spec, slot_spec, slot_spec, spec, spec, spec], out_specs=[spec] * 4),
        out_shape=[shp] * 4,
        compiler_params=_params(("parallel", "parallel", "parallel")),
    )(qc, own, mine, sib, w, m, v)


def _allreduce_small(v, name):
    r, w = v.shape

    def body(v_ref, o_ref, buf_ref, send_sems, recv_sems):
        x, y, c = _place()
        me = 4 * x + 2 * y + c

        def peer(k):
            return x ^ (k >> 2), y ^ ((k >> 1) & 1), c ^ (k & 1)

        def remote(k, slot):
            return pltpu.make_async_remote_copy(
                src_ref=v_ref, dst_ref=buf_ref.at[slot], send_sem=send_sems.at[k - 1], recv_sem=recv_sems.at[k - 1],
                device_id=peer(k), device_id_type=MESH)

        sends = [remote(k, me) for k in range(1, N_DEV)]
        for cp in sends:
            cp.start()
        buf_ref[me] = v_ref[...]
        for k in range(1, N_DEV):
            px, py, pc = peer(k)
            remote(k, 4 * px + 2 * py + pc).wait_recv()
        for cp in sends:
            cp.wait_send()
        acc = buf_ref[0]
        for dev in range(1, N_DEV):
            acc = acc + buf_ref[dev]
        o_ref[...] = acc

    vm = pl.BlockSpec(memory_space=pltpu.VMEM)
    return pl.pallas_call(
        body,
        name=name,
        in_specs=[vm],
        out_specs=vm,
        out_shape=jax.ShapeDtypeStruct((r, w), F32),
        scratch_shapes=[pltpu.VMEM((N_DEV, r, w), F32), pltpu.SemaphoreType.DMA((N_DEV - 1,)), pltpu.SemaphoreType.DMA((N_DEV - 1,))],
        compiler_params=pltpu.CompilerParams(vmem_limit_bytes=VMEM_LIMIT_BYTES),
    )(v)


def _adamw(w, g, m, v, name):
    ly, r, c = w.shape
    tr = _pick(r, (256, 128, 64, 32, 16, 8))
    c1 = 1.0 - ADAM_B1 ** ADAM_STEP
    c2 = 1.0 - ADAM_B2 ** ADAM_STEP

    def body(w_ref, g_ref, m_ref, v_ref, d_ref, nm_ref, nv_ref):
        gv = g_ref[...]
        nm = ADAM_B1 * m_ref[...] + (1.0 - ADAM_B1) * gv
        nv = ADAM_B2 * v_ref[...] + (1.0 - ADAM_B2) * (gv * gv)
        d_ref[...] = -ADAM_LR * ((nm / c1) / (jnp.sqrt(nv / c2) + ADAM_EPS) + ADAM_WD * w_ref[...])
        nm_ref[...] = nm
        nv_ref[...] = nv

    spec = pl.BlockSpec((None, tr, c), lambda l, i: (l, i, 0))
    shp = jax.ShapeDtypeStruct((ly, r, c), F32)
    return pl.pallas_call(
        body,
        name=name,
        grid=(ly, r // tr),
        in_specs=[spec] * 4,
        out_specs=[spec] * 3,
        out_shape=[shp] * 3,
        compiler_params=_params(("parallel", "parallel")),
    )(w, g, m, v)


def _rope_tables(seq):
    pos = jnp.arange(seq, dtype=F32)
    inv_freq = 1.0 / (ROPE_THETA ** (jnp.arange(0, HEAD_DIM, 2, dtype=F32) / HEAD_DIM))
    ang = (pos[:, None] * inv_freq[None, :]).T
    cos, sin = jnp.cos(ang), jnp.sin(ang)
    return jnp.concatenate([cos, cos], axis=0), jnp.concatenate([-sin, sin], axis=0)


def _pack(vs, fill=0.0):
    p = jnp.concatenate([v.reshape(-1) for v in vs])
    size = -(-p.shape[0] // 8192) * 8192
    return jnp.pad(p, (0, size - p.shape[0]), constant_values=fill).reshape(-1, 1024)


def _unpack(p, like):
    p = p.reshape(-1)
    out, o = [], 0
    for v in like:
        n = int(math.prod(v.shape))
        out.append(p[o:o + n].reshape(v.shape))
        o += n
    return out


def kernel(x, norm_mix, norm_ffn, norm_final, conv_w_in, conv_w_conv, conv_w_out, attn_w_qkv, attn_b_qkv, attn_sinks, attn_w_o, attn_b_o, ffn_w_in, ffn_w_conv, ffn_w_down, loss_target, m_norm_mix, m_norm_ffn, m_norm_final, m_conv_w_in, m_conv_w_conv, m_conv_w_out, m_attn_w_qkv, m_attn_b_qkv, m_attn_sinks, m_attn_w_o, m_attn_b_o, m_ffn_w_in, m_ffn_w_conv, m_ffn_w_down, v_norm_mix, v_norm_ffn, v_norm_final, v_conv_w_in, v_conv_w_conv, v_conv_w_out, v_attn_w_qkv, v_attn_b_qkv, v_attn_sinks, v_attn_w_o, v_attn_b_o, v_ffn_w_in, v_ffn_w_conv, v_ffn_w_down):
    bsz, seq, d = x.shape
    t = bsz * seq
    depth = norm_mix.shape[0]
    n_conv, n_attn = conv_w_in.shape[0], attn_w_qkv.shape[0]
    xq, yq, cq = _place()
    q = 2 * xq + yq

    big = [conv_w_in, conv_w_out, attn_w_qkv, attn_w_o, ffn_w_in, ffn_w_down]
    axes = [2, 1, 2, 1, 2, 1]
    q_arr = q.astype(jnp.int32).reshape(1)
    c_arr = cq.astype(jnp.int32).reshape(1)
    weights = [_place_shard(w, ax, q_arr, f"place_shard{n}") for n, (w, ax) in enumerate(zip(big, axes))]

    def pieces_of(i):
        return [(0, i // 2), (1, i // 2), (4, i), (5, i)] if i % 2 == 0 else [(2, i // 2), (3, i // 2), (4, i), (5, i)]

    weights = _gather_weights(weights, axes, pieces_of(0), "gather_layer0")

    small_cols = [conv_w_conv, attn_b_qkv, attn_b_o, ffn_w_conv]

    def placed(v):
        width = v.shape[-1]
        full = jnp.zeros(v.shape[:-1] + (N_CHIPS * width,), F32)
        return lax.dynamic_update_slice_in_dim(full, v * (1.0 / N_CORES), q * width, axis=v.ndim - 1)

    full_cols = [placed(v) for v in small_cols]
    wc_conv, b_qkv, b_o, wf_conv = _unpack(_allreduce_small(_pack(full_cols), "gather_small"), full_cols)
    cos_t, sin_t = _rope_tables(seq)

    xs = x.reshape(t, d)
    saved = []
    for i in range(depth):
        j = i // 2
        ahead = pieces_of(i + 1) if i + 1 < depth else None
        if ahead:
            weights, s_sems, r_sems = _gather_start(weights, axes, ahead, 0, f"gather_ici_start{i + 1}")
        w_cin, w_cout, w_qkv, w_o, w_fin, w_fdown = weights
        h = _rms_fwd(xs, norm_mix[i:i + 1], f"norm_mix_fwd{i}")
        if i % 2 == 0:
            pre = _mm(h, w_cin, "nn", BF16, layer=j, tm=1024, tn=768, tk=4096, name=f"conv_in_fwd{i}")
            mixed = _convgate_fwd(pre, wc_conv[j], seq, f"conv_gate_fwd{i}")
            x_mid = _mm(mixed, w_cout, "nn", F32, layer=j, residual=xs, tm=512, tn=1024, tk=4096, name=f"conv_out_fwd{i}")
            lse = None
        else:
            pre = _mm(h, w_qkv, "nn", F32, layer=j, bias=b_qkv[j:j + 1], tm=1024, tn=768, tk=4096, name=f"qkv_fwd{i}")
            mixed, lse = _attn_fwd(pre, attn_sinks[j], cos_t, sin_t, bsz, seq, f"attn_fwd{i}", hp=1)
            x_mid = _mm(mixed, w_o, "nn", F32, layer=j, bias=b_o[j:j + 1], residual=xs, tm=512, tn=1024, tk=4096,
                        name=f"attn_out_fwd{i}")
        if ahead:
            weights = _gather_wait(weights, s_sems, r_sems, x_mid, axes, ahead, 0, f"gather_ici_wait{i + 1}")
            weights, s_sems, r_sems = _gather_start(weights, axes, ahead, 1, f"gather_pass_start{i + 1}")
        w_cin, w_cout, w_qkv, w_o, w_fin, w_fdown = weights
        h2 = _rms_fwd(x_mid, norm_ffn[i:i + 1], f"norm_ffn_fwd{i}")
        gu = _mm(h2, w_fin, "nn", BF16, layer=i, n_outer=True, tm=512, tn=1408, tk=4096, name=f"ffn_in_fwd{i}")
        act = _ffngate_fwd(gu, wf_conv[i], seq, f"ffn_gate_fwd{i}")
        x_next = _mm(act, w_fdown, "nn", F32, layer=i, residual=x_mid, tm=512, tn=1024, tk=4096, name=f"ffn_down_fwd{i}")
        if ahead:
            weights = _gather_wait(weights, s_sems, r_sems, x_next, axes, ahead, 1, f"gather_pass_wait{i + 1}")
        saved.append((xs, h, pre, mixed, lse, x_mid, h2, gu, act))
        xs = x_next
    w_cin, w_cout, w_qkv, w_o, w_fin, w_fdown = weights

    dx, dxb, sq, dg_final = _loss_head(xs, loss_target.reshape(t, d), norm_final.reshape(1, d), "loss_head")
    loss = lax.psum(0.5 * jnp.sum(sq) / d, ("x", "y", "c"))

    g_norm_mix, g_norm_ffn = [None] * depth, [None] * depth
    g_cin, g_cconv, g_cout = [None] * n_conv, [None] * n_conv, [None] * n_conv
    g_qkv, g_bqkv, g_sinks, g_o, g_bo = ([None] * n_attn for _ in range(5))
    g_fin, g_fconv, g_fdown = [None] * depth, [None] * depth, [None] * depth

    kinds6 = ["col", "row", "col", "row", "col", "row"]
    layers6 = [n_conv, n_conv, n_attn, n_attn, depth, depth]
    big_w = [conv_w_in, conv_w_out, attn_w_qkv, attn_w_o, ffn_w_in, ffn_w_down]

    def slot_stack(n):
        k, cols = big_w[n].shape[1], big_w[n].shape[2]
        r = k // 2
        return lax.empty((layers6[n], N_CHIPS, r, cols), BF16)

    own = [slot_stack(n) for n in range(6)]
    mine = [slot_stack(n) for n in range(6)]
    sib = [slot_stack(n) for n in range(6)]
    flight = {}

    def scatter(i, stage, action, token):
        ts = [ti for ti, _ in pieces_of(i)]
        local = [(n, l) for n, (_, l) in enumerate(pieces_of(i))]
        sub = ([own[ti] for ti in ts], [mine[ti] for ti in ts], [sib[ti] for ti in ts], token)
        label = f"grad_{'ici' if stage == 0 else 'pass'}_{action}{i}"
        if action == "start":
            o, mi, si, token, s_sems, r_sems = _scatter_start(*sub, local, stage, label)
            flight[i] = (s_sems, r_sems)
        else:
            o, mi, si, token = _scatter_wait(*sub, *flight[i], local, stage, label)
        for n, ti in enumerate(ts):
            own[ti], mine[ti], sib[ti] = o[n], mi[n], si[n]
        return token

    def reduce_begin(i, token):
        grads = {0: g_cin, 1: g_cout, 2: g_qkv, 3: g_o, 4: g_fin, 5: g_fdown}
        parts, kinds = [], []
        for ti, l in pieces_of(i):
            g = grads[ti][l]
            parts.append(g if kinds6[ti] == "col" else g.reshape(N_CHIPS, g.shape[0] // N_CHIPS, g.shape[1]))
            kinds.append(kinds6[ti])
        recv = _pair_exchange(parts, kinds, f"grad_pair_exchange{i}")
        for (ti, l), g, r in zip(pieces_of(i), parts, recv):
            own[ti] = _pair_sum(g, r, kinds6[ti], c_arr, own[ti], l, f"grad_pair_sum{i}_{ti}")
        return scatter(i, 0, "start", token)

    for i in reversed(range(depth)):
        j = i // 2
        x_in, h, pre, mixed, lse, x_mid, h2, gu, act = saved[i]
        da = _mm(dxb, w_fdown, "nt", BF16, layer=i, n_outer=True, tm=512, tn=1408, tk=4096, name=f"ffn_down_dx{i}")
        g_fdown[i] = _mm(act, dxb, "tn", BF16, tm=1408, tn=1024, tk=2048, name=f"ffn_down_dw{i}")
        dgu, dwc = _ffngate_bwd(gu, da, wf_conv[i], seq, f"ffn_gate_bwd{i}")
        g_fconv[i] = dwc[:3]
        g_fin[i] = _mm(h2, dgu, "tn", BF16, tm=1024, tn=1408, tk=2048, name=f"ffn_in_dw{i}")
        dh2 = _mm(dgu, w_fin, "nt", F32, layer=i, tm=512, tn=1024, tk=8192, name=f"ffn_in_dx{i}")
        dx, dxb, dg, colsum = _rms_bwd(x_mid, dh2, norm_ffn[i:i + 1], dx, f"norm_ffn_bwd{i}")
        g_norm_ffn[i] = jnp.sum(dg, axis=0)
        if i + 1 < depth:
            dxb = scatter(i + 1, 1, "start", scatter(i + 1, 0, "wait", dxb))
        if i % 2 == 0:
            dmix = _mm(dxb, w_cout, "nt", BF16, layer=j, tm=512, tn=1024, tk=4096, name=f"conv_out_dx{i}")
            g_cout[j] = _mm(mixed, dxb, "tn", BF16, tm=1024, tn=1024, tk=2048, name=f"conv_out_dw{i}")
            dpre, dwc = _convgate_bwd(pre, dmix, wc_conv[j], seq, f"conv_gate_bwd{i}")
            g_cconv[j] = dwc[:3]
            g_cin[j] = _mm(h, dpre, "tn", BF16, tm=1024, tn=1536, tk=2048, name=f"conv_in_dw{i}")
            dh = _mm(dpre, w_cin, "nt", F32, layer=j, tm=512, tn=1024, tk=8192, name=f"conv_in_dx{i}")
        else:
            g_bo[j] = jnp.sum(colsum, axis=0)
            dmix = _mm(dxb, w_o, "nt", F32, layer=j, tm=512, tn=1024, tk=4096, name=f"attn_out_dx{i}")
            g_o[j] = _mm(mixed, dxb, "tn", BF16, tm=1024, tn=1024, tk=2048, name=f"attn_out_dw{i}")
            dpre, dbias, dsk = _attn_bwd(pre, mixed, lse, dmix, attn_sinks[j], cos_t, sin_t, bsz, seq, f"attn_bwd{i}",
                                         hp=GROUP)
            g_bqkv[j] = jnp.sum(dbias, axis=0)
            g_sinks[j] = jnp.sum(dsk, axis=1)
            g_qkv[j] = _mm(h, dpre, "tn", BF16, tm=1024, tn=1536, tk=2048, name=f"qkv_dw{i}")
            dh = _mm(dpre, w_qkv, "nt", F32, layer=j, tm=512, tn=1024, tk=1536, name=f"qkv_dx{i}")
        dx, dxb, dg, _ = _rms_bwd(x_in, dh, norm_mix[i:i + 1], dx, f"norm_mix_bwd{i}")
        g_norm_mix[i] = jnp.sum(dg, axis=0)
        if i + 1 < depth:
            dxb = scatter(i + 1, 1, "wait", dxb)
        dxb = reduce_begin(i, dxb)
    grad_x = dx.reshape(bsz, seq, d)

    scatter(0, 1, "wait", scatter(0, 1, "start", scatter(0, 0, "wait", dxb)))
    big_m = [m_conv_w_in, m_conv_w_out, m_attn_w_qkv, m_attn_w_o, m_ffn_w_in, m_ffn_w_down]
    big_v = [v_conv_w_in, v_conv_w_out, v_attn_w_qkv, v_attn_w_o, v_ffn_w_in, v_ffn_w_down]
    big_names = ["conv_w_in", "conv_w_out", "attn_w_qkv", "attn_w_o", "ffn_w_in", "ffn_w_down"]
    qc_arr = jnp.stack([q, cq]).astype(jnp.int32)
    big_upd = [_reduce_adamw(own[n], mine[n], sib[n], big_w[n], big_m[n], big_v[n], qc_arr, f"adamw_{nm}")
               for n, nm in enumerate(big_names)]

    small = [jnp.stack(g_norm_mix), jnp.stack(g_norm_ffn), jnp.sum(dg_final, axis=0), jnp.stack(g_cconv),
             jnp.stack(g_bqkv), jnp.stack(g_sinks), jnp.stack(g_bo), jnp.stack(g_fconv)]
    sg = _unpack(_allreduce_small(_pack(small), "grad_small_allreduce"), small)

    def my_cols(v, like):
        width = like.shape[-1]
        return lax.dynamic_slice_in_dim(v, q * width, width, axis=v.ndim - 1)

    small_w = [norm_mix, norm_ffn, norm_final, conv_w_conv, attn_b_qkv, attn_sinks, attn_b_o, ffn_w_conv]
    small_m = [m_norm_mix, m_norm_ffn, m_norm_final, m_conv_w_conv, m_attn_b_qkv, m_attn_sinks, m_attn_b_o, m_ffn_w_conv]
    small_v = [v_norm_mix, v_norm_ffn, v_norm_final, v_conv_w_conv, v_attn_b_qkv, v_attn_sinks, v_attn_b_o, v_ffn_w_conv]
    small_g = [sg[0], sg[1], sg[2], my_cols(sg[3], conv_w_conv), my_cols(sg[4], attn_b_qkv), sg[5],
               my_cols(sg[6], attn_b_o), my_cols(sg[7], ffn_w_conv)]

    upd = {nm: tuple(u[1:]) for nm, u in zip(big_names, big_upd)}
    sd, sm, sv = _adamw(_pack(small_w)[None], _pack(small_g)[None], _pack(small_m)[None], _pack(small_v, 1.0)[None],
                        "adamw_small")
    sd, sm, sv = _unpack(sd, small_w), _unpack(sm, small_w), _unpack(sv, small_w)
    names = ["norm_mix", "norm_ffn", "norm_final", "conv_w_in", "conv_w_conv", "conv_w_out", "attn_w_qkv", "attn_b_qkv",
             "attn_sinks", "attn_w_o", "attn_b_o", "ffn_w_in", "ffn_w_conv", "ffn_w_down"]
    small_names = ["norm_mix", "norm_ffn", "norm_final", "conv_w_conv", "attn_b_qkv", "attn_sinks", "attn_b_o", "ffn_w_conv"]
    grads = dict(zip(small_names, small_g))
    grads.update({nm: u[0] for nm, u in zip(big_names, big_upd)})
    for n, nm in enumerate(small_names):
        upd[nm] = (sd[n], sm[n], sv[n])
    return (loss, grad_x, *[grads[nm] for nm in names], *[upd[nm][0] for nm in names],
            *[upd[nm][1] for nm in names], *[upd[nm][2] for nm in names])
```

```python
import math

import jax
import jax.numpy as jnp
from jax import lax
from jax.experimental import pallas as pl
from jax.experimental.pallas import tpu as pltpu

F32 = jnp.float32
BF16 = jnp.bfloat16

HEAD_DIM = 64
GROUP = 4
WINDOW = 128
EPS = 1e-5
ROPE_THETA = 10000.0
ADAM_LR, ADAM_B1, ADAM_B2, ADAM_EPS, ADAM_WD, ADAM_STEP = 0.001, 0.9, 0.999, 1e-08, 0.01, 10

N_CHIPS = 4
N_CORES = 2
N_DEV = 8
HALO = 16
VMEM_LIMIT_BYTES = 56 * 1024 * 1024
MESH = pl.DeviceIdType.MESH
ANY = pl.BlockSpec(memory_space=pl.ANY)
SMEM = pl.BlockSpec(memory_space=pltpu.SMEM)
NEG = float(jnp.finfo(jnp.float32).min)
ROW_TILES = (512, 256, 128, 64, 32, 16, 8)


def _pick(dim, cands):
    for c in cands:
        if dim % c == 0:
            return c
    return dim


def _params(sem):
    return pltpu.CompilerParams(dimension_semantics=sem, vmem_limit_bytes=VMEM_LIMIT_BYTES)


_DIMS = {"nn": (((1,), (0,)), ((), ())), "nt": (((1,), (1,)), ((), ())), "tn": (((0,), (0,)), ((), ()))}


def _mm(a, b, mode, out_dtype, *, layer=None, bias=None, residual=None, n_outer=False, tm, tn, tk, name):
    b2 = b.shape[1:] if layer is not None else b.shape
    if mode == "nn":
        (m, k), n = a.shape, b2[1]
    elif mode == "nt":
        (m, k), n = a.shape, b2[0]
    else:
        (k, m), n = a.shape, b2[1]
    tm, tn, tk = min(tm, m), min(tn, n), min(tk, k)
    assert m % tm == 0 and n % tn == 0 and k % tk == 0, (name, a.shape, b.shape, tm, tn, tk)
    nk = k // tk

    def at(f):
        return (lambda p0, p1, p2: f(p1, p0, p2)) if n_outer else f

    a_spec = pl.BlockSpec((tk, tm), at(lambda i, j, l: (l, i))) if mode == "tn" else pl.BlockSpec((tm, tk), at(lambda i, j, l: (i, l)))
    if layer is None:
        b_spec = (pl.BlockSpec((tn, tk), at(lambda i, j, l: (j, l))) if mode == "nt"
                  else pl.BlockSpec((tk, tn), at(lambda i, j, l: (l, j))))
    elif mode == "nt":
        b_spec = pl.BlockSpec((None, tn, tk), at(lambda i, j, l: (layer, j, l)))
    else:
        b_spec = pl.BlockSpec((None, tk, tn), at(lambda i, j, l: (layer, l, j)))
    in_specs, args = [a_spec, b_spec], [a, b]
    if bias is not None:
        in_specs.append(pl.BlockSpec((1, tn), at(lambda i, j, l: (0, j))))
        args.append(bias)
    if residual is not None:
        in_specs.append(pl.BlockSpec((tm, tn), at(lambda i, j, l: (i, j))))
        args.append(residual)
    has_bias, has_res = bias is not None, residual is not None

    def body(*refs):
        a_ref, b_ref = refs[0], refs[1]
        pos = 2
        bias_ref = res_ref = None
        if has_bias:
            bias_ref, pos = refs[pos], pos + 1
        if has_res:
            res_ref, pos = refs[pos], pos + 1
        o_ref = refs[pos]
        acc_ref = refs[pos + 1] if nk > 1 else None

        def finish(acc):
            if has_bias:
                acc = acc + bias_ref[...]
            if has_res:
                acc = acc + res_ref[...]
            o_ref[...] = acc.astype(o_ref.dtype)

        if nk == 1:
            finish(lax.dot_general(a_ref[...], b_ref[...], _DIMS[mode], preferred_element_type=F32))
            return
        l = pl.program_id(2)
        part = lax.dot_general(a_ref[...], b_ref[...], _DIMS[mode], preferred_element_type=F32)

        @pl.when(l == 0)
        def _():
            acc_ref[...] = part

        @pl.when(l > 0)
        def _():
            acc_ref[...] += part

        @pl.when(l == nk - 1)
        def _():
            finish(acc_ref[...])

    return pl.pallas_call(
        body,
        name=name,
        grid=(n // tn, m // tm, nk) if n_outer else (m // tm, n // tn, nk),
        in_specs=in_specs,
        out_specs=pl.BlockSpec((tm, tn), at(lambda i, j, l: (i, j))),
        out_shape=jax.ShapeDtypeStruct((m, n), out_dtype),
        scratch_shapes=[pltpu.VMEM((tm, tn), F32)] if nk > 1 else [],
        compiler_params=_params(("parallel", "parallel", "arbitrary")),
    )(*args)


def _fold8(v):
    r, d = v.shape
    return jnp.sum(v.reshape(r // 8, 8, d), axis=0)


def _rms_fwd(x, g, name):
    t, d = x.shape
    tm = _pick(t, ROW_TILES)

    def body(x_ref, g_ref, h_ref):
        xv = x_ref[...]
        r = lax.rsqrt(jnp.mean(xv * xv, axis=-1, keepdims=True) + EPS)
        h_ref[...] = (xv * r * g_ref[...]).astype(BF16)

    return pl.pallas_call(
        body,
        name=name,
        grid=(t // tm,),
        in_specs=[pl.BlockSpec((tm, d), lambda i: (i, 0)), pl.BlockSpec((1, d), lambda i: (0, 0))],
        out_specs=pl.BlockSpec((tm, d), lambda i: (i, 0)),
        out_shape=jax.ShapeDtypeStruct((t, d), BF16),
        compiler_params=_params(("parallel",)),
    )(x, g)


def _rms_bwd(x, dpre, w, layer, g, dx_in, name):
    t, d = x.shape
    k = dpre.shape[1]
    tm = _pick(t, ROW_TILES)

    def body(x_ref, dp_ref, w_ref, g_ref, dxi_ref, dx_ref, dxb_ref, dg_ref, cs_ref):
        i = pl.program_id(0)
        xv = x_ref[...]
        r = lax.rsqrt(jnp.mean(xv * xv, axis=-1, keepdims=True) + EPS)
        xhat = xv * r
        dy = lax.dot_general(dp_ref[...], w_ref[...], _DIMS["nt"], preferred_element_type=F32)
        gdy = dy * g_ref[...]
        dx = dxi_ref[...] + r * (gdy - xhat * jnp.mean(gdy * xhat, axis=-1, keepdims=True))
        dx_ref[...] = dx
        dxb_ref[...] = dx.astype(BF16)

        @pl.when(i == 0)
        def _():
            dg_ref[...] = jnp.zeros_like(dg_ref)
            cs_ref[...] = jnp.zeros_like(cs_ref)

        dg_ref[...] += _fold8(dy * xhat)
        cs_ref[...] += _fold8(dx)

    row = pl.BlockSpec((tm, d), lambda i: (i, 0))
    acc = pl.BlockSpec((8, d), lambda i: (0, 0))
    w_spec = pl.BlockSpec((None, d, k), lambda i: (layer, 0, 0), pipeline_mode=pl.Buffered(1))
    return pl.pallas_call(
        body,
        name=name,
        grid=(t // tm,),
        in_specs=[row, pl.BlockSpec((tm, k), lambda i: (i, 0)), w_spec, pl.BlockSpec((1, d), lambda i: (0, 0)), row],
        out_specs=[row, row, acc, acc],
        out_shape=[jax.ShapeDtypeStruct((t, d), F32), jax.ShapeDtypeStruct((t, d), BF16),
                   jax.ShapeDtypeStruct((8, d), F32), jax.ShapeDtypeStruct((8, d), F32)],
        compiler_params=_params(("arbitrary",)),
    )(x, dpre, w, g, dx_in)


def _loss_head(x, target, g, name):
    t, d = x.shape
    tm = _pick(t, ROW_TILES)
    inv_d = 1.0 / d

    def body(x_ref, t_ref, g_ref, dx_ref, dxb_ref, sq_ref, dg_ref):
        i = pl.program_id(0)
        xv = x_ref[...]
        gv = g_ref[...]
        r = lax.rsqrt(jnp.mean(xv * xv, axis=-1, keepdims=True) + EPS)
        xhat = xv * r
        err = xhat * gv - t_ref[...]
        dy = err * inv_d
        gdy = dy * gv
        dx = r * (gdy - xhat * jnp.mean(gdy * xhat, axis=-1, keepdims=True))
        dx_ref[...] = dx
        dxb_ref[...] = dx.astype(BF16)

        @pl.when(i == 0)
        def _():
            sq_ref[...] = jnp.zeros_like(sq_ref)
            dg_ref[...] = jnp.zeros_like(dg_ref)

        sq_ref[...] += _fold8(err * err)
        dg_ref[...] += _fold8(dy * xhat)

    row = pl.BlockSpec((tm, d), lambda i: (i, 0))
    acc = pl.BlockSpec((8, d), lambda i: (0, 0))
    return pl.pallas_call(
        body,
        name=name,
        grid=(t // tm,),
        in_specs=[row, row, pl.BlockSpec((1, d), lambda i: (0, 0))],
        out_specs=[row, row, acc, acc],
        out_shape=[jax.ShapeDtypeStruct((t, d), F32), jax.ShapeDtypeStruct((t, d), BF16),
                   jax.ShapeDtypeStruct((8, d), F32), jax.ShapeDtypeStruct((8, d), F32)],
        compiler_params=_params(("arbitrary",)),
    )(x, target, g)


def _rows(tm):
    return lax.broadcasted_iota(jnp.int32, (tm, 1), 0)


def _shift_down(u, before2):
    r8 = _rows(8)
    s1, s2 = pltpu.roll(u, 1, 0), pltpu.roll(u, 2, 0)
    top1 = jnp.where(r8 == 0, before2[1:2], s1[:8])
    top2 = jnp.where(r8 == 0, before2[0:1], jnp.where(r8 == 1, before2[1:2], s2[:8]))
    return jnp.concatenate([top1, s1[8:]], axis=0), jnp.concatenate([top2, s2[8:]], axis=0)


def _shift_up(u, after2):
    tm = u.shape[0]
    r8 = _rows(8)
    s1, s2 = pltpu.roll(u, tm - 1, 0), pltpu.roll(u, tm - 2, 0)
    bot1 = jnp.where(r8 == 7, after2[0:1], s1[tm - 8:])
    bot2 = jnp.where(r8 == 6, after2[0:1], jnp.where(r8 == 7, after2[1:2], s2[tm - 8:]))
    return jnp.concatenate([s1[:tm - 8], bot1], axis=0), jnp.concatenate([s2[:tm - 8], bot2], axis=0)


def _shift_matrix(tm, up):
    r = lax.broadcasted_iota(jnp.int32, (2 * tm, tm), 0)
    c = lax.broadcasted_iota(jnp.int32, (2 * tm, tm), 1)
    t = jnp.where(r >= tm, r - tm, r)
    k = jnp.where(r >= tm, 2, 1)
    return (c == (t + k if up else t - k)).astype(BF16)


def _shift_down_mxu(u, before2):
    tm = u.shape[0]
    moved = jnp.dot(_shift_matrix(tm, False), u.astype(BF16), preferred_element_type=F32)
    r8 = _rows(8)
    s1, s2 = moved[:tm], moved[tm:]
    top1 = s1[:8] + jnp.where(r8 == 0, before2[1:2], 0.0)
    top2 = s2[:8] + jnp.where(r8 == 0, before2[0:1], jnp.where(r8 == 1, before2[1:2], 0.0))
    return jnp.concatenate([top1, s1[8:]], axis=0), jnp.concatenate([top2, s2[8:]], axis=0)


def _shift_up_mxu(u, after2):
    tm = u.shape[0]
    moved = jnp.dot(_shift_matrix(tm, True), u.astype(BF16), preferred_element_type=F32)
    r8 = _rows(8)
    s1, s2 = moved[:tm], moved[tm:]
    bot1 = s1[tm - 8:] + jnp.where(r8 == 7, after2[0:1], 0.0)
    bot2 = s2[tm - 8:] + jnp.where(r8 == 6, after2[0:1], jnp.where(r8 == 7, after2[1:2], 0.0))
    return jnp.concatenate([s1[:tm - 8], bot1], axis=0), jnp.concatenate([s2[:tm - 8], bot2], axis=0)


def _conv_tile(seq):
    return _pick(seq, (256, 128, 64, 32, 16, 8))


def _halo_specs(tm, width, n_tiles):
    per = tm // HALO
    before = pl.BlockSpec((HALO, width), lambda i: (jnp.maximum(i * per - 1, 0), 0))
    after = pl.BlockSpec((HALO, width), lambda i: (jnp.minimum((i + 1) * per, n_tiles * per - 1), 0))
    return before, after


def _convgate_fwd(bcv, w, seq, name):
    t, d3 = bcv.shape
    d = d3 // 3
    tm = _conv_tile(seq)
    tps = seq // tm
    before, _ = _halo_specs(tm, d3, t // tm)

    def body(x_ref, xb_ref, w_ref, y_ref):
        i = pl.program_id(0)
        inner = (i % tps != 0).astype(F32)
        u = x_ref[:, d:2 * d].astype(F32) * x_ref[:, 2 * d:].astype(F32)
        xb = xb_ref[:, d:].astype(F32)[HALO - 2:]
        s1, s2 = _shift_down(u, xb[:, :d] * xb[:, d:] * inner)
        z = w_ref[2:3] * u + w_ref[1:2] * s1 + w_ref[0:1] * s2
        y_ref[...] = (x_ref[:, :d].astype(F32) * z).astype(BF16)

    return pl.pallas_call(
        body,
        name=name,
        grid=(t // tm,),
        in_specs=[pl.BlockSpec((tm, d3), lambda i: (i, 0)), before, pl.BlockSpec((3, d), lambda i: (0, 0))],
        out_specs=pl.BlockSpec((tm, d), lambda i: (i, 0)),
        out_shape=jax.ShapeDtypeStruct((t, d), BF16),
        compiler_params=_params(("parallel",)),
    )(bcv, bcv, w)


def _convgate_bwd(bcv, dy, w, seq, name):
    t, d3 = bcv.shape
    d = d3 // 3
    tm = _conv_tile(seq)
    tps = seq // tm
    before, after = _halo_specs(tm, d3, t // tm)
    _, after_dy = _halo_specs(tm, d, t // tm)

    def body(x_ref, xb_ref, xa_ref, dy_ref, dya_ref, w_ref, dx_ref, dw_ref):
        i = pl.program_id(0)
        inner_lo = (i % tps != 0).astype(F32)
        inner_hi = (i % tps != tps - 1).astype(F32)
        w0, w1, w2 = w_ref[0:1], w_ref[1:2], w_ref[2:3]
        b, c, v = x_ref[:, :d].astype(F32), x_ref[:, d:2 * d].astype(F32), x_ref[:, 2 * d:].astype(F32)
        u = c * v
        xb = xb_ref[:, d:].astype(F32)[HALO - 2:]
        s1, s2 = _shift_down(u, xb[:, :d] * xb[:, d:] * inner_lo)
        z = w2 * u + w1 * s1 + w0 * s2
        dyv = dy_ref[...].astype(F32)
        dz = dyv * b
        dza = dya_ref[...].astype(F32)[0:2] * xa_ref[:, :d].astype(F32)[0:2] * inner_hi
        n1, n2 = _shift_up(dz, dza)
        du = w2 * dz + w1 * n1 + w0 * n2
        dx_ref[:, :d] = (dyv * z).astype(BF16)
        dx_ref[:, d:2 * d] = (du * v).astype(BF16)
        dx_ref[:, 2 * d:] = (du * c).astype(BF16)

        @pl.when(i == 0)
        def _():
            dw_ref[...] = jnp.zeros_like(dw_ref)

        dw_ref[0:1] += jnp.sum(dz * s2, axis=0, keepdims=True)
        dw_ref[1:2] += jnp.sum(dz * s1, axis=0, keepdims=True)
        dw_ref[2:3] += jnp.sum(dz * u, axis=0, keepdims=True)

    return pl.pallas_call(
        body,
        name=name,
        grid=(t // tm,),
        in_specs=[pl.BlockSpec((tm, d3), lambda i: (i, 0)), before, after,
                  pl.BlockSpec((tm, d), lambda i: (i, 0)), after_dy, pl.BlockSpec((3, d), lambda i: (0, 0))],
        out_specs=[pl.BlockSpec((tm, d3), lambda i: (i, 0)), pl.BlockSpec((8, d), lambda i: (0, 0))],
        out_shape=[jax.ShapeDtypeStruct((t, d3), BF16), jax.ShapeDtypeStruct((8, d), F32)],
        compiler_params=_params(("arbitrary",)),
    )(bcv, bcv, bcv, dy, dy, w)


def _sigmoid(x):
    return 1.0 / (1.0 + jnp.exp(-x))


def _ffngate_fwd(gu, w, seq, name):
    t, f2 = gu.shape
    f = f2 // 2
    tm = _conv_tile(seq)
    tps = seq // tm
    before, _ = _halo_specs(tm, f2, t // tm)

    def body(x_ref, xb_ref, w_ref, a_ref):
        i = pl.program_id(0)
        inner = (i % tps != 0).astype(F32)
        s1, s2 = _shift_down_mxu(x_ref[:, :f], xb_ref[:, :f].astype(F32)[HALO - 2:] * inner)
        gc = w_ref[2:3] * x_ref[:, :f].astype(F32) + w_ref[1:2] * s1 + w_ref[0:1] * s2
        a_ref[...] = (gc * _sigmoid(gc) * x_ref[:, f:].astype(F32)).astype(BF16)

    return pl.pallas_call(
        body,
        name=name,
        grid=(t // tm,),
        in_specs=[pl.BlockSpec((tm, f2), lambda i: (i, 0)), before, pl.BlockSpec((3, f), lambda i: (0, 0))],
        out_specs=pl.BlockSpec((tm, f), lambda i: (i, 0)),
        out_shape=jax.ShapeDtypeStruct((t, f), BF16),
        compiler_params=_params(("parallel",)),
    )(gu, gu, w)


def _ffngate_bwd(gu, da, w, seq, name):
    t, f2 = gu.shape
    f = f2 // 2
    tm = _conv_tile(seq)
    tps = seq // tm
    before, after = _halo_specs(tm, f2, t // tm)
    _, after_da = _halo_specs(tm, f, t // tm)

    def body(x_ref, xb_ref, xa_ref, da_ref, daa_ref, w_ref, dx_ref, dw_ref):
        i = pl.program_id(0)
        inner_lo = (i % tps != 0).astype(F32)
        inner_hi = (i % tps != tps - 1).astype(F32)
        w0, w1, w2 = w_ref[0:1], w_ref[1:2], w_ref[2:3]

        def dgate(gc, uv, dav):
            sg = _sigmoid(gc)
            return dav * uv * (sg * (1.0 + gc * (1.0 - sg))), dav * (gc * sg)

        g, u = x_ref[:, :f].astype(F32), x_ref[:, f:].astype(F32)
        s1, s2 = _shift_down_mxu(x_ref[:, :f], xb_ref[:, :f].astype(F32)[HALO - 2:] * inner_lo)
        gc = w2 * g + w1 * s1 + w0 * s2
        dgc, du = dgate(gc, u, da_ref[...].astype(F32))
        ga = xa_ref[:, :f].astype(F32)
        a1, a2 = _shift_down(ga, x_ref[tm - HALO:, :f].astype(F32)[HALO - 2:])
        gca = w2 * ga + w1 * a1 + w0 * a2
        dgca, _ = dgate(gca, xa_ref[:, f:].astype(F32), daa_ref[...].astype(F32))
        n1, n2 = _shift_up_mxu(dgc, dgca[0:2] * inner_hi)
        dx_ref[:, :f] = (w2 * dgc + w1 * n1 + w0 * n2).astype(BF16)
        dx_ref[:, f:] = du.astype(BF16)

        @pl.when(i == 0)
        def _():
            dw_ref[...] = jnp.zeros_like(dw_ref)

        dw_ref[0:1] += jnp.sum(dgc * s2, axis=0, keepdims=True)
        dw_ref[1:2] += jnp.sum(dgc * s1, axis=0, keepdims=True)
        dw_ref[2:3] += jnp.sum(dgc * g, axis=0, keepdims=True)

    return pl.pallas_call(
        body,
        name=name,
        grid=(t // tm,),
        in_specs=[pl.BlockSpec((tm, f2), lambda i: (i, 0)), before, after,
                  pl.BlockSpec((tm, f), lambda i: (i, 0)), after_da, pl.BlockSpec((3, f), lambda i: (0, 0))],
        out_specs=[pl.BlockSpec((tm, f2), lambda i: (i, 0)), pl.BlockSpec((8, f), lambda i: (0, 0))],
        out_shape=[jax.ShapeDtypeStruct((t, f2), BF16), jax.ShapeDtypeStruct((8, f), F32)],
        compiler_params=_params(("arbitrary",)),
    )(gu, gu, gu, da, da, w)


def _swap_halves(xt):
    half = HEAD_DIM // 2
    return jnp.concatenate([xt[half:], xt[:half]], axis=0)


def _rope(xt, cos, sin):
    return xt * cos + _swap_halves(xt) * sin


def _unrope(dxt, cos, sin):
    return dxt * cos - _swap_halves(dxt) * sin


def _key_query(count):
    kj = lax.broadcasted_iota(jnp.int32, (WINDOW, count * WINDOW), 0)
    qi = lax.broadcasted_iota(jnp.int32, (WINDOW, count * WINDOW), 1) & (WINDOW - 1)
    return kj, qi


def _band_masks(n, count):
    kj, qi = _key_query(count)
    return kj <= qi, jnp.logical_and(kj > qi, n > 0)


def _lanes(v, count):
    return jnp.concatenate([v] * count, axis=1) if count > 1 else v


def _heads(ref, h0, count):
    parts = [ref[(h0 + g) * HEAD_DIM:(h0 + g + 1) * HEAD_DIM, :] for g in range(count)]
    return jnp.concatenate(parts, axis=1) if count > 1 else parts[0]


def _head_rows(ref, h0, count):
    parts = [ref[h0 + g:h0 + g + 1, :] for g in range(count)]
    return jnp.concatenate(parts, axis=1) if count > 1 else parts[0]


def _head_sinks(sink_ref, h0, count):
    parts = [jnp.full((1, WINDOW), sink_ref[h0 + g], F32) for g in range(count)]
    return jnp.concatenate(parts, axis=1) if count > 1 else parts[0]


def _tn(a, b):
    return lax.dot_general(a, b, _DIMS["tn"], preferred_element_type=F32)


def _nt(a, b):
    return lax.dot_general(a, b, _DIMS["nt"], preferred_element_type=F32)


def _nn(a, b):
    return jnp.dot(a, b, preferred_element_type=F32)


def _attn_fwd(qkv, sinks, cos_t, sin_t, bsz, seq, name, hp):
    t, qw = qkv.shape
    d = qw * 2 // 3
    kvw = d // GROUP
    n_heads, n_kv = d // HEAD_DIM, kvw // HEAD_DIM
    nb = seq // WINDOW
    scale = HEAD_DIM ** -0.5

    def body(sink_ref, xc_ref, xp_ref, cc_ref, sc_ref, cp_ref, sp_ref, o_ref, lse_ref, xt_ref, pt_ref, ot_ref):
        n = pl.program_id(1)
        xt_ref[...] = xc_ref[...].T
        pt_ref[...] = xp_ref[:, d:].T
        cos_c, sin_c, cos_p, sin_p = cc_ref[...], sc_ref[...], cp_ref[...], sp_ref[...]
        cos_g, sin_g = _lanes(cos_c, hp), _lanes(sin_c, hp)
        valid_c, valid_p = _band_masks(n, hp)
        for j in range(n_kv):
            ko = j * HEAD_DIM
            kc = _rope(xt_ref[d + ko:d + ko + HEAD_DIM, :], cos_c, sin_c).astype(BF16)
            kp = _rope(pt_ref[ko:ko + HEAD_DIM, :], cos_p, sin_p).astype(BF16)
            vc = xt_ref[d + kvw + ko:d + kvw + ko + HEAD_DIM, :].astype(BF16)
            vp = pt_ref[kvw + ko:kvw + ko + HEAD_DIM, :].astype(BF16)
            for h0 in range(j * GROUP, (j + 1) * GROUP, hp):
                q = _rope(_heads(xt_ref, h0, hp), cos_g, sin_g).astype(BF16)
                sink = _head_sinks(sink_ref, h0, hp)
                s_c = jnp.where(valid_c, _tn(kc, q) * scale, NEG)
                s_p = jnp.where(valid_p, _tn(kp, q) * scale, NEG)
                m = jnp.maximum(jnp.maximum(jnp.max(s_c, axis=0, keepdims=True), jnp.max(s_p, axis=0, keepdims=True)), sink)
                p_c = jnp.exp(s_c - m)
                p_p = jnp.exp(s_p - m)
                den = jnp.sum(p_c, axis=0, keepdims=True) + jnp.sum(p_p, axis=0, keepdims=True) + jnp.exp(sink - m)
                inv = 1.0 / den
                o_g = _nn(vc, (p_c * inv).astype(BF16)) + _nn(vp, (p_p * inv).astype(BF16))
                lse_g = m + jnp.log(den)
                for g in range(hp):
                    h = h0 + g
                    ot_ref[h * HEAD_DIM:(h + 1) * HEAD_DIM, :] = o_g[:, g * WINDOW:(g + 1) * WINDOW]
                    lse_ref[h:h + 1, :] = lse_g[:, g * WINDOW:(g + 1) * WINDOW]
        o_ref[...] = ot_ref[...].T.astype(BF16)

    cur = lambda b, n: (b * nb + n, 0)
    prev = lambda b, n: (b * nb + jnp.maximum(n - 1, 0), 0)
    tab_c = pl.BlockSpec((HEAD_DIM, WINDOW), lambda b, n: (0, n))
    tab_p = pl.BlockSpec((HEAD_DIM, WINDOW), lambda b, n: (0, jnp.maximum(n - 1, 0)))
    return pl.pallas_call(
        body,
        name=name,
        grid=(bsz, nb),
        in_specs=[SMEM, pl.BlockSpec((WINDOW, qw), cur), pl.BlockSpec((WINDOW, qw), prev), tab_c, tab_c, tab_p, tab_p],
        out_specs=[pl.BlockSpec((WINDOW, d), cur), pl.BlockSpec((n_heads, WINDOW), lambda b, n: (0, b * nb + n))],
        out_shape=[jax.ShapeDtypeStruct((t, d), BF16), jax.ShapeDtypeStruct((n_heads, t), F32)],
        scratch_shapes=[pltpu.VMEM((qw, WINDOW), F32), pltpu.VMEM((2 * kvw, WINDOW), F32), pltpu.VMEM((d, WINDOW), F32)],
        compiler_params=_params(("parallel", "arbitrary")),
    )(sinks, qkv, qkv, cos_t, sin_t, cos_t, sin_t)


def _attn_bwd(qkv, o, lse, do, sinks, cos_t, sin_t, bsz, seq, name, hp):
    t, qw = qkv.shape
    d = qw * 2 // 3
    kvw = d // GROUP
    n_heads, n_kv = d // HEAD_DIM, kvw // HEAD_DIM
    nb = seq // WINDOW
    scale = HEAD_DIM ** -0.5

    def body(sink_ref, xc_ref, xp_ref, xn_ref, oc_ref, on_ref, doc_ref, don_ref, lc_ref, ln_ref,
             cc_ref, sc_ref, cp_ref, sp_ref, cn_ref, sn_ref,
             dx_ref, db_ref, dsk_ref, xt_ref, pt_ref, qn_ref, otc_ref, otn_ref, dtc_ref, dtn_ref, gt_ref):
        b, n = pl.program_id(0), pl.program_id(1)
        xt_ref[...] = xc_ref[...].T
        pt_ref[...] = xp_ref[:, d:].T
        qn_ref[...] = xn_ref[:, :d].T
        otc_ref[...] = oc_ref[...].astype(F32).T
        otn_ref[...] = on_ref[...].astype(F32).T
        dtc_ref[...] = doc_ref[...].T
        dtn_ref[...] = don_ref[...].T
        cos_c, sin_c, cos_p, sin_p, cos_n, sin_n = (cc_ref[...], sc_ref[...], cp_ref[...], sp_ref[...],
                                                    cn_ref[...], sn_ref[...])
        cos_g, sin_g, cos_gn, sin_gn = _lanes(cos_c, hp), _lanes(sin_c, hp), _lanes(cos_n, hp), _lanes(sin_n, hp)
        valid_c, valid_p = _band_masks(n, hp)
        kj, qi = _key_query(hp)
        valid_n = jnp.logical_and(kj > qi, n < nb - 1)

        @pl.when(jnp.logical_and(b == 0, n == 0))
        def _():
            db_ref[...] = jnp.zeros_like(db_ref)
            dsk_ref[...] = jnp.zeros_like(dsk_ref)

        for j in range(n_kv):
            ko = j * HEAD_DIM
            kc = _rope(xt_ref[d + ko:d + ko + HEAD_DIM, :], cos_c, sin_c).astype(BF16)
            kp = _rope(pt_ref[ko:ko + HEAD_DIM, :], cos_p, sin_p).astype(BF16)
            vc = xt_ref[d + kvw + ko:d + kvw + ko + HEAD_DIM, :].astype(BF16)
            vp = pt_ref[kvw + ko:kvw + ko + HEAD_DIM, :].astype(BF16)
            dk = jnp.zeros((HEAD_DIM, WINDOW), F32)
            dv = jnp.zeros((HEAD_DIM, WINDOW), F32)
            for h0 in range(j * GROUP, (j + 1) * GROUP, hp):
                q = _rope(_heads(xt_ref, h0, hp), cos_g, sin_g).astype(BF16)
                do_g = _heads(dtc_ref, h0, hp)
                do_b = do_g.astype(BF16)
                lse_g = _head_rows(lc_ref, h0, hp)
                delta = jnp.sum(_heads(otc_ref, h0, hp) * do_g, axis=0, keepdims=True)
                p_c = jnp.exp(jnp.where(valid_c, _tn(kc, q) * scale, NEG) - lse_g)
                p_p = jnp.exp(jnp.where(valid_p, _tn(kp, q) * scale, NEG) - lse_g)
                ds_c = (p_c * (_tn(vc, do_b) - delta)).astype(BF16)
                ds_p = (p_p * (_tn(vp, do_b) - delta)).astype(BF16)
                dq = _unrope((_nn(kc, ds_c) + _nn(kp, ds_p)) * scale, cos_g, sin_g)
                dsk = -jnp.exp(_head_sinks(sink_ref, h0, hp) - lse_g) * delta
                for g in range(hp):
                    h = h0 + g
                    gt_ref[h * HEAD_DIM:(h + 1) * HEAD_DIM, :] = dq[:, g * WINDOW:(g + 1) * WINDOW]
                    dsk_ref[h:h + 1, :] += dsk[:, g * WINDOW:(g + 1) * WINDOW]
                q2 = _rope(_heads(qn_ref, h0, hp), cos_gn, sin_gn).astype(BF16)
                do2 = _heads(dtn_ref, h0, hp)
                do2_b = do2.astype(BF16)
                delta2 = jnp.sum(_heads(otn_ref, h0, hp) * do2, axis=0, keepdims=True)
                p_n = jnp.exp(jnp.where(valid_n, _tn(kc, q2) * scale, NEG) - _head_rows(ln_ref, h0, hp))
                ds_n = (p_n * (_tn(vc, do2_b) - delta2)).astype(BF16)
                dv += _nt(do_b, p_c.astype(BF16)) + _nt(do2_b, p_n.astype(BF16))
                dk += _nt(q, ds_c) + _nt(q2, ds_n)
            gt_ref[d + ko:d + ko + HEAD_DIM, :] = _unrope(dk * scale, cos_c, sin_c)
            gt_ref[d + kvw + ko:d + kvw + ko + HEAD_DIM, :] = dv
        dx = gt_ref[...].T
        dx_ref[...] = dx.astype(BF16)
        db_ref[...] += _fold8(dx)

    cur = lambda b, n: (b * nb + n, 0)
    prev = lambda b, n: (b * nb + jnp.maximum(n - 1, 0), 0)
    nxt = lambda b, n: (b * nb + jnp.minimum(n + 1, nb - 1), 0)
    stat_c = pl.BlockSpec((n_heads, WINDOW), lambda b, n: (0, b * nb + n))
    stat_n = pl.BlockSpec((n_heads, WINDOW), lambda b, n: (0, b * nb + jnp.minimum(n + 1, nb - 1)))
    tab_c = pl.BlockSpec((HEAD_DIM, WINDOW), lambda b, n: (0, n))
    tab_p = pl.BlockSpec((HEAD_DIM, WINDOW), lambda b, n: (0, jnp.maximum(n - 1, 0)))
    tab_n = pl.BlockSpec((HEAD_DIM, WINDOW), lambda b, n: (0, jnp.minimum(n + 1, nb - 1)))
    return pl.pallas_call(
        body,
        name=name,
        grid=(bsz, nb),
        in_specs=[SMEM, pl.BlockSpec((WINDOW, qw), cur), pl.BlockSpec((WINDOW, qw), prev), pl.BlockSpec((WINDOW, qw), nxt),
                  pl.BlockSpec((WINDOW, d), cur), pl.BlockSpec((WINDOW, d), nxt),
                  pl.BlockSpec((WINDOW, d), cur), pl.BlockSpec((WINDOW, d), nxt),
                  stat_c, stat_n, tab_c, tab_c, tab_p, tab_p, tab_n, tab_n],
        out_specs=[pl.BlockSpec((WINDOW, qw), cur), pl.BlockSpec((8, qw), lambda b, n: (0, 0)),
                   pl.BlockSpec((n_heads, WINDOW), lambda b, n: (0, 0))],
        out_shape=[jax.ShapeDtypeStruct((t, qw), BF16), jax.ShapeDtypeStruct((8, qw), F32),
                   jax.ShapeDtypeStruct((n_heads, WINDOW), F32)],
        scratch_shapes=[pltpu.VMEM((qw, WINDOW), F32), pltpu.VMEM((2 * kvw, WINDOW), F32), pltpu.VMEM((d, WINDOW), F32),
                        pltpu.VMEM((d, WINDOW), F32), pltpu.VMEM((d, WINDOW), F32), pltpu.VMEM((d, WINDOW), F32),
                        pltpu.VMEM((d, WINDOW), F32), pltpu.VMEM((qw, WINDOW), F32)],
        compiler_params=_params(("arbitrary", "arbitrary")),
    )(sinks, qkv, qkv, qkv, o, o, do, do, lse, lse, cos_t, sin_t, cos_t, sin_t, cos_t, sin_t)


def _place():
    return lax.axis_index("x"), lax.axis_index("y"), lax.axis_index("c")


def _other_chips(x, y):
    return [(1 - x, y), (x, 1 - y), (1 - x, 1 - y)]


def _place_shard(w, axis, q, name):
    ly, k, n = w.shape
    tr = _pick(k, (256, 128, 64, 32, 16, 8))
    steps = k // tr
    shape = (ly, k * N_CHIPS, n) if axis == 1 else (ly, k, n * N_CHIPS)
    if axis == 1:
        out_spec = pl.BlockSpec((None, tr, n), lambda l, i, q_ref: (l, q_ref[0] * steps + i, 0))
    else:
        out_spec = pl.BlockSpec((None, tr, n), lambda l, i, q_ref: (l, i, q_ref[0]))

    def body(q_ref, w_ref, o_ref):
        del q_ref
        o_ref[...] = w_ref[...].astype(BF16)

    return pl.pallas_call(
        body,
        name=name,
        grid_spec=pltpu.PrefetchScalarGridSpec(
            num_scalar_prefetch=1, grid=(ly, steps),
            in_specs=[pl.BlockSpec((None, tr, n), lambda l, i, q_ref: (l, i, 0))], out_specs=out_spec),
        out_shape=jax.ShapeDtypeStruct(shape, BF16),
        compiler_params=_params(("parallel", "parallel")),
    )(q, w)


def _half_block(ref, axis, layer, px, py, pc):
    blk = 2 * px + py
    if axis == 1:
        rows = ref.shape[1] // (2 * N_CHIPS)
        return ref.at[layer, pl.ds(pl.multiple_of((2 * blk + pc) * rows, 8), rows), :]
    rows, width = ref.shape[1] // 2, ref.shape[2] // N_CHIPS
    return ref.at[layer, pl.ds(pl.multiple_of(pc * rows, 8), rows), pl.ds(pl.multiple_of(blk * width, 128), width)]


def _gather_copy(refs, axes, pieces, send_sems, recv_sems, p, k, stage, whose):
    x, y, c = _place()
    chip = _other_chips(x, y)[k]
    i, layer = pieces[p]
    if stage == 0:
        origin = (x, y, c) if whose == "mine" else (*chip, c)
        to = (*chip, c)
    else:
        origin = (*chip, c) if whose == "mine" else (*chip, 1 - c)
        to = (x, y, 1 - c)
    blk = _half_block(refs[i], axes[i], layer, *origin)
    return pltpu.make_async_remote_copy(src_ref=blk, dst_ref=blk, send_sem=send_sems.at[p * 3 + k],
                                        recv_sem=recv_sems.at[p * 3 + k], device_id=to, device_id_type=MESH)


def _gather_weights(fulls, axes, pieces, name):
    n, m = len(fulls), 3 * len(pieces)

    def body(*refs):
        dst = refs[n:2 * n]
        sems = refs[2 * n:]
        todo = [(p, k) for p in range(len(pieces)) for k in range(3)]
        sends = [_gather_copy(dst, axes, pieces, sems[0], sems[1], p, k, 0, "mine") for p, k in todo]
        for cp in sends:
            cp.start()
        for p, k in todo:
            _gather_copy(dst, axes, pieces, sems[0], sems[1], p, k, 0, "theirs").wait_recv()
            sends.append(_gather_copy(dst, axes, pieces, sems[2], sems[3], p, k, 1, "mine"))
            sends[-1].start()
        for p, k in todo:
            _gather_copy(dst, axes, pieces, sems[2], sems[3], p, k, 1, "theirs").wait_recv()
        for cp in sends:
            cp.wait_send()

    return pl.pallas_call(
        body,
        name=name,
        in_specs=[ANY] * n,
        out_specs=[ANY] * n,
        out_shape=[jax.ShapeDtypeStruct(f.shape, f.dtype) for f in fulls],
        input_output_aliases={i: i for i in range(n)},
        scratch_shapes=[pltpu.SemaphoreType.DMA((m,))] * 4,
    )(*fulls)


HBM_SPEC = pl.BlockSpec(memory_space=pltpu.HBM)
SEM_SPEC = pl.BlockSpec(memory_space=pltpu.SEMAPHORE)


def _gather_start(fulls, axes, pieces, stage, name):
    n, m = len(fulls), 3 * len(pieces)

    def body(*refs):
        src = refs[:n]
        send_sems, recv_sems = refs[2 * n], refs[2 * n + 1]
        for p in range(len(pieces)):
            for k in range(3):
                _gather_copy(src, axes, pieces, send_sems, recv_sems, p, k, stage, "mine").start()

    out = pl.pallas_call(
        body,
        name=name,
        in_specs=[HBM_SPEC] * n,
        out_specs=[HBM_SPEC] * n + [SEM_SPEC, SEM_SPEC],
        out_shape=[pltpu.HBM(f.shape, f.dtype) for f in fulls] + [pltpu.SemaphoreType.DMA((m,)), pltpu.SemaphoreType.DMA((m,))],
        input_output_aliases={i: i for i in range(n)},
        compiler_params=pltpu.CompilerParams(has_side_effects=pltpu.SideEffectType.DATAFLOW_SIDE_EFFECTING),
    )(*[pltpu.with_memory_space_constraint(f, pltpu.HBM) for f in fulls])
    return list(out[:n]), out[n], out[n + 1]


def _gather_wait(fulls, send_sems, recv_sems, after, axes, pieces, stage, name):
    n = len(fulls)

    def body(*refs):
        src = refs[:n]
        s_sems, r_sems = refs[n], refs[n + 1]
        for p in range(len(pieces)):
            for k in range(3):
                _gather_copy(src, axes, pieces, s_sems, r_sems, p, k, stage, "mine").wait_send()
                _gather_copy(src, axes, pieces, s_sems, r_sems, p, k, stage, "theirs").wait_recv()

    out = pl.pallas_call(
        body,
        name=name,
        in_specs=[HBM_SPEC] * n + [SEM_SPEC, SEM_SPEC, ANY],
        out_specs=[HBM_SPEC] * n,
        out_shape=[pltpu.HBM(f.shape, f.dtype) for f in fulls],
        input_output_aliases={i: i for i in range(n)},
        compiler_params=pltpu.CompilerParams(has_side_effects=pltpu.SideEffectType.DATAFLOW_SIDE_EFFECTING),
    )(*fulls, send_sems, recv_sems, after)
    return list(out)


def _half_shape(kind, shape):
    if kind == "col":
        return (shape[0] // 2, shape[1])
    return (N_CHIPS, shape[1] // 2, shape[2])


def _half_of(kind, ref, h):
    if kind == "col":
        r = ref.shape[0] // 2
        return ref.at[pl.ds(pl.multiple_of(h * r, 8), r), :]
    r = ref.shape[1] // 2
    return ref.at[:, pl.ds(pl.multiple_of(h * r, 8), r), :]


def _pair_exchange(grads, kinds, name):
    n = len(grads)
    outs = [jax.ShapeDtypeStruct(_half_shape(kd, g.shape), g.dtype) for g, kd in zip(grads, kinds)]

    def body(*refs):
        src, dst = refs[:n], refs[n:2 * n]
        send_sems, recv_sems = refs[2 * n:]
        x, y, c = _place()
        copies = [pltpu.make_async_remote_copy(
            src_ref=_half_of(kinds[i], src[i], 1 - c), dst_ref=dst[i],
            send_sem=send_sems.at[i], recv_sem=recv_sems.at[i], device_id=(x, y, 1 - c), device_id_type=MESH)
            for i in range(n)]
        for cp in copies:
            cp.start()
        for cp in copies:
            cp.wait_recv()
        for cp in copies:
            cp.wait_send()

    return pl.pallas_call(
        body,
        name=name,
        in_specs=[ANY] * n,
        out_specs=[ANY] * n,
        out_shape=outs,
        scratch_shapes=[pltpu.SemaphoreType.DMA((n,)), pltpu.SemaphoreType.DMA((n,))],
    )(*grads)


def _slot_shape(kind, shape):
    return (shape[0] // 2, shape[1] // N_CHIPS) if kind == "col" else (shape[1] // 2, shape[2])


def _pair_sum(grad, recv, kind, c, own, layer, name):
    r, cols = own.shape[2:]
    if kind == "col":
        tr = _pick(r, (256, 128, 64, 32, 16, 8))
        steps = r // tr
        grid = (N_CHIPS, steps)
        g_spec = pl.BlockSpec((tr, cols), lambda s, i, c_ref: (c_ref[0] * steps + i, s))
        r_spec = pl.BlockSpec((tr, cols), lambda s, i, c_ref: (i, s))
        o_spec = pl.BlockSpec((None, None, tr, cols), lambda s, i, c_ref: (layer, s, i, 0))
        g_in = grad
    else:
        grid = (N_CHIPS, 1)
        g_spec = pl.BlockSpec((None, None, r, cols), lambda s, i, c_ref: (s, c_ref[0], 0, 0))
        r_spec = pl.BlockSpec((None, r, cols), lambda s, i, c_ref: (s, 0, 0))
        o_spec = pl.BlockSpec((None, None, r, cols), lambda s, i, c_ref: (layer, s, 0, 0))
        g_in = grad.reshape(N_CHIPS, 2, r, cols)

    def body(c_ref, g_ref, r_ref, own_ref, o_ref):
        del c_ref, own_ref
        o_ref[...] = (g_ref[...].astype(F32) + r_ref[...].astype(F32)).astype(o_ref.dtype)

    return pl.pallas_call(
        body,
        name=name,
        grid_spec=pltpu.PrefetchScalarGridSpec(num_scalar_prefetch=1, grid=grid, in_specs=[g_spec, r_spec, ANY], out_specs=o_spec),
        out_shape=jax.ShapeDtypeStruct(own.shape, own.dtype),
        input_output_aliases={3: 0},
        compiler_params=_params(("parallel", "parallel")),
    )(c, g_in, recv, own)


def _scatter_copy(own, mine, sib, pieces, send_sems, recv_sems, p, k, stage, whose):
    x, y, c = _place()
    q = 2 * x + y
    i, layer = pieces[p]
    per = 4 if stage == 0 else 3
    if k == 3:
        src, dst, to = own[i].at[layer, q], sib[i].at[layer, q], (x, y, 1 - c)
    else:
        chip = _other_chips(x, y)[k]
        slot = 2 * chip[0] + chip[1]
        if stage == 0:
            to = (*chip, c)
            src, dst = (own[i].at[layer, slot], mine[i].at[layer, q]) if whose == "mine" else (own[i].at[layer, q], mine[i].at[layer, slot])
        else:
            to = (x, y, 1 - c)
            src, dst = mine[i].at[layer, slot], sib[i].at[layer, slot]
    return pltpu.make_async_remote_copy(src_ref=src, dst_ref=dst, send_sem=send_sems.at[p * per + k],
                                        recv_sem=recv_sems.at[p * per + k], device_id=to, device_id_type=MESH)


def _scatter_start(own, mine, sib, token, pieces, stage, name):
    n = len(own)
    per = 4 if stage == 0 else 3
    m = per * len(pieces)
    n_arr = 3 * n + 1

    def body(*refs):
        o, mi, si = refs[:n], refs[n:2 * n], refs[2 * n:3 * n]
        send_sems, recv_sems = refs[2 * n_arr], refs[2 * n_arr + 1]
        for p in range(len(pieces)):
            for k in range(per):
                _scatter_copy(o, mi, si, pieces, send_sems, recv_sems, p, k, stage, "mine").start()

    arrays = list(own) + list(mine) + list(sib) + [token]
    out = pl.pallas_call(
        body,
        name=name,
        in_specs=[HBM_SPEC] * n_arr,
        out_specs=[HBM_SPEC] * n_arr + [SEM_SPEC, SEM_SPEC],
        out_shape=[pltpu.HBM(a.shape, a.dtype) for a in arrays] + [pltpu.SemaphoreType.DMA((m,)), pltpu.SemaphoreType.DMA((m,))],
        input_output_aliases={i: i for i in range(n_arr)},
        compiler_params=pltpu.CompilerParams(has_side_effects=pltpu.SideEffectType.DATAFLOW_SIDE_EFFECTING),
    )(*[pltpu.with_memory_space_constraint(a, pltpu.HBM) for a in arrays])
    return list(out[:n]), list(out[n:2 * n]), list(out[2 * n:3 * n]), out[3 * n], out[n_arr], out[n_arr + 1]


def _scatter_wait(own, mine, sib, token, send_sems, recv_sems, pieces, stage, name):
    n = len(own)
    per = 4 if stage == 0 else 3
    n_arr = 3 * n + 1

    def body(*refs):
        o, mi, si = refs[:n], refs[n:2 * n], refs[2 * n:3 * n]
        s_sems, r_sems = refs[n_arr], refs[n_arr + 1]
        for p in range(len(pieces)):
            for k in range(per):
                _scatter_copy(o, mi, si, pieces, s_sems, r_sems, p, k, stage, "mine").wait_send()
                _scatter_copy(o, mi, si, pieces, s_sems, r_sems, p, k, stage, "theirs").wait_recv()

    arrays = list(own) + list(mine) + list(sib) + [token]
    out = pl.pallas_call(
        body,
        name=name,
        in_specs=[HBM_SPEC] * n_arr + [SEM_SPEC, SEM_SPEC],
        out_specs=[HBM_SPEC] * n_arr,
        out_shape=[pltpu.HBM(a.shape, a.dtype) for a in arrays],
        input_output_aliases={i: i for i in range(n_arr)},
        compiler_params=pltpu.CompilerParams(has_side_effects=pltpu.SideEffectType.DATAFLOW_SIDE_EFFECTING),
    )(*arrays, send_sems, recv_sems)
    return list(out[:n]), list(out[n:2 * n]), list(out[2 * n:3 * n]), out[3 * n]


def _reduce_adamw(own, mine, sib, w, m, v, qc, name):
    ly, _, r, cols = mine.shape
    tr = _pick(r, (128, 64, 32, 16, 8))
    steps = r // tr
    c1 = 1.0 - ADAM_B1 ** ADAM_STEP
    c2 = 1.0 - ADAM_B2 ** ADAM_STEP

    def body(qc_ref, own_ref, mine_ref, sib_ref, w_ref, m_ref, v_ref, g_ref, d_ref, nm_ref, nv_ref):
        q = qc_ref[0]
        mine_sum = sib_sum = None
        for s in range(N_CHIPS):
            a = jnp.where(q == s, own_ref[...], mine_ref[s]).astype(F32)
            b = sib_ref[s].astype(F32)
            mine_sum = a if s == 0 else mine_sum + a
            sib_sum = b if s == 0 else sib_sum + b
        gv = jnp.where(pl.program_id(1) == qc_ref[1], mine_sum, sib_sum)
        nm = ADAM_B1 * m_ref[...] + (1.0 - ADAM_B1) * gv
        nv = ADAM_B2 * v_ref[...] + (1.0 - ADAM_B2) * (gv * gv)
        g_ref[...] = gv
        d_ref[...] = -ADAM_LR * ((nm / c1) / (jnp.sqrt(nv / c2) + ADAM_EPS) + ADAM_WD * w_ref[...])
        nm_ref[...] = nm
        nv_ref[...] = nv

    own_spec = pl.BlockSpec((None, None, tr, cols), lambda l, h, i, qc_ref: (l, qc_ref[0], i, 0))
    slot_spec = pl.BlockSpec((None, N_CHIPS, tr, cols), lambda l, h, i, qc_ref: (l, 0, i, 0))
    spec = pl.BlockSpec((None, tr, cols), lambda l, h, i, qc_ref: (l, h * steps + i, 0))
    shp = jax.ShapeDtypeStruct(w.shape, F32)
    return pl.pallas_call(
        body,
        name=name,
        grid_spec=pltpu.PrefetchScalarGridSpec(
            num_scalar_prefetch=1, grid=(ly, N_CORES, steps),
            in_specs=[own_spec, slot_spec, slot_spec, spec, spec, spec], out_specs=[spec] * 4),
        out_shape=[shp] * 4,
        compiler_params=_params(("parallel", "parallel", "parallel")),
    )(qc, own, mine, sib, w, m, v)


def _allreduce_small(v, name):
    r, w = v.shape

    def body(v_ref, o_ref, buf_ref, send_sems, recv_sems):
        x, y, c = _place()
        me = 4 * x + 2 * y + c

        def peer(k):
            return x ^ (k >> 2), y ^ ((k >> 1) & 1), c ^ (k & 1)

        def remote(k, slot):
            return pltpu.make_async_remote_copy(
                src_ref=v_ref, dst_ref=buf_ref.at[slot], send_sem=send_sems.at[k - 1], recv_sem=recv_sems.at[k - 1],
                device_id=peer(k), device_id_type=MESH)

        sends = [remote(k, me) for k in range(1, N_DEV)]
        for cp in sends:
            cp.start()
        buf_ref[me] = v_ref[...]
        for k in range(1, N_DEV):
            px, py, pc = peer(k)
            remote(k, 4 * px + 2 * py + pc).wait_recv()
        for cp in sends:
            cp.wait_send()
        acc = buf_ref[0]
        for dev in range(1, N_DEV):
            acc = acc + buf_ref[dev]
        o_ref[...] = acc

    vm = pl.BlockSpec(memory_space=pltpu.VMEM)
    return pl.pallas_call(
        body,
        name=name,
        in_specs=[vm],
        out_specs=vm,
        out_shape=jax.ShapeDtypeStruct((r, w), F32),
        scratch_shapes=[pltpu.VMEM((N_DEV, r, w), F32), pltpu.SemaphoreType.DMA((N_DEV - 1,)), pltpu.SemaphoreType.DMA((N_DEV - 1,))],
        compiler_params=pltpu.CompilerParams(vmem_limit_bytes=VMEM_LIMIT_BYTES),
    )(v)


def _adamw(w, g, m, v, name):
    ly, r, c = w.shape
    tr = _pick(r, (256, 128, 64, 32, 16, 8))
    c1 = 1.0 - ADAM_B1 ** ADAM_STEP
    c2 = 1.0 - ADAM_B2 ** ADAM_STEP

    def body(w_ref, g_ref, m_ref, v_ref, d_ref, nm_ref, nv_ref):
        gv = g_ref[...]
        nm = ADAM_B1 * m_ref[...] + (1.0 - ADAM_B1) * gv
        nv = ADAM_B2 * v_ref[...] + (1.0 - ADAM_B2) * (gv * gv)
        d_ref[...] = -ADAM_LR * ((nm / c1) / (jnp.sqrt(nv / c2) + ADAM_EPS) + ADAM_WD * w_ref[...])
        nm_ref[...] = nm
        nv_ref[...] = nv

    spec = pl.BlockSpec((None, tr, c), lambda l, i: (l, i, 0))
    shp = jax.ShapeDtypeStruct((ly, r, c), F32)
    return pl.pallas_call(
        body,
        name=name,
        grid=(ly, r // tr),
        in_specs=[spec] * 4,
        out_specs=[spec] * 3,
        out_shape=[shp] * 3,
        compiler_params=_params(("parallel", "parallel")),
    )(w, g, m, v)


def _rope_tables(seq):
    pos = jnp.arange(seq, dtype=F32)
    inv_freq = 1.0 / (ROPE_THETA ** (jnp.arange(0, HEAD_DIM, 2, dtype=F32) / HEAD_DIM))
    ang = (pos[:, None] * inv_freq[None, :]).T
    cos, sin = jnp.cos(ang), jnp.sin(ang)
    return jnp.concatenate([cos, cos], axis=0), jnp.concatenate([-sin, sin], axis=0)


def _pack(vs, fill=0.0):
    p = jnp.concatenate([v.reshape(-1) for v in vs])
    size = -(-p.shape[0] // 8192) * 8192
    return jnp.pad(p, (0, size - p.shape[0]), constant_values=fill).reshape(-1, 1024)


def _unpack(p, like):
    p = p.reshape(-1)
    out, o = [], 0
    for v in like:
        n = int(math.prod(v.shape))
        out.append(p[o:o + n].reshape(v.shape))
        o += n
    return out


def kernel(x, norm_mix, norm_ffn, norm_final, conv_w_in, conv_w_conv, conv_w_out, attn_w_qkv, attn_b_qkv, attn_sinks, attn_w_o, attn_b_o, ffn_w_in, ffn_w_conv, ffn_w_down, loss_target, m_norm_mix, m_norm_ffn, m_norm_final, m_conv_w_in, m_conv_w_conv, m_conv_w_out, m_attn_w_qkv, m_attn_b_qkv, m_attn_sinks, m_attn_w_o, m_attn_b_o, m_ffn_w_in, m_ffn_w_conv, m_ffn_w_down, v_norm_mix, v_norm_ffn, v_norm_final, v_conv_w_in, v_conv_w_conv, v_conv_w_out, v_attn_w_qkv, v_attn_b_qkv, v_attn_sinks, v_attn_w_o, v_attn_b_o, v_ffn_w_in, v_ffn_w_conv, v_ffn_w_down):
    bsz, seq, d = x.shape
    t = bsz * seq
    depth = norm_mix.shape[0]
    n_conv, n_attn = conv_w_in.shape[0], attn_w_qkv.shape[0]
    xq, yq, cq = _place()
    q = 2 * xq + yq

    big = [conv_w_in, conv_w_out, attn_w_qkv, attn_w_o, ffn_w_in, ffn_w_down]
    axes = [2, 1, 2, 1, 2, 1]
    q_arr = q.astype(jnp.int32).reshape(1)
    c_arr = cq.astype(jnp.int32).reshape(1)
    weights = [_place_shard(w, ax, q_arr, f"place_shard{n}") for n, (w, ax) in enumerate(zip(big, axes))]

    def pieces_of(i):
        return [(0, i // 2), (1, i // 2), (4, i), (5, i)] if i % 2 == 0 else [(2, i // 2), (3, i // 2), (4, i), (5, i)]

    weights = _gather_weights(weights, axes, pieces_of(0)[:2], "gather_mixer0")

    small_cols = [conv_w_conv, attn_b_qkv, attn_b_o, ffn_w_conv]

    def placed(v):
        width = v.shape[-1]
        full = jnp.zeros(v.shape[:-1] + (N_CHIPS * width,), F32)
        return lax.dynamic_update_slice_in_dim(full, v * (1.0 / N_CORES), q * width, axis=v.ndim - 1)

    full_cols = [placed(v) for v in small_cols]
    wc_conv, b_qkv, b_o, wf_conv = _unpack(_allreduce_small(_pack(full_cols), "gather_small"), full_cols)
    cos_t, sin_t = _rope_tables(seq)

    xs = x.reshape(t, d)
    saved = []
    sems = {}

    def fetch(pieces, stage, tag):
        nonlocal weights
        weights, sems[tag, 0], sems[tag, 1] = _gather_start(weights, axes, pieces, stage, f"gather_{'ici' if stage == 0 else 'pass'}_start{tag}")

    def settle(pieces, stage, tag, after):
        nonlocal weights
        weights = _gather_wait(weights, sems[tag, 0], sems[tag, 1], after, axes, pieces, stage,
                               f"gather_{'ici' if stage == 0 else 'pass'}_wait{tag}")

    for i in range(depth):
        j = i // 2
        ahead = pieces_of(i + 1) if i + 1 < depth else None
        if i == 0:
            fetch(pieces_of(0)[2:], 0, "0")
        elif ahead:
            fetch(ahead, 0, str(i + 1))
        w_cin, w_cout, w_qkv, w_o, w_fin, w_fdown = weights
        h = _rms_fwd(xs, norm_mix[i:i + 1], f"norm_mix_fwd{i}")
        if i % 2 == 0:
            pre = _mm(h, w_cin, "nn", BF16, layer=j, tm=1024, tn=768, tk=4096, name=f"conv_in_fwd{i}")
            mixed = _convgate_fwd(pre, wc_conv[j], seq, f"conv_gate_fwd{i}")
            x_mid = _mm(mixed, w_cout, "nn", F32, layer=j, residual=xs, tm=512, tn=1024, tk=4096, name=f"conv_out_fwd{i}")
            lse = None
        else:
            pre = _mm(h, w_qkv, "nn", F32, layer=j, bias=b_qkv[j:j + 1], tm=1024, tn=768, tk=4096, name=f"qkv_fwd{i}")
            mixed, lse = _attn_fwd(pre, attn_sinks[j], cos_t, sin_t, bsz, seq, f"attn_fwd{i}", hp=1)
            x_mid = _mm(mixed, w_o, "nn", F32, layer=j, bias=b_o[j:j + 1], residual=xs, tm=512, tn=1024, tk=4096,
                        name=f"attn_out_fwd{i}")
        if i == 0:
            settle(pieces_of(0)[2:], 0, "0", x_mid)
            fetch(pieces_of(0)[2:], 1, "0")
        elif ahead:
            settle(ahead, 0, str(i + 1), x_mid)
            fetch(ahead, 1, str(i + 1))
        h2 = _rms_fwd(x_mid, norm_ffn[i:i + 1], f"norm_ffn_fwd{i}")
        if i == 0:
            settle(pieces_of(0)[2:], 1, "0", h2)
            fetch(ahead, 0, "1")
        w_cin, w_cout, w_qkv, w_o, w_fin, w_fdown = weights
        gu = _mm(h2, w_fin, "nn", BF16, layer=i, n_outer=True, tm=2048, tn=1408, tk=4096, name=f"ffn_in_fwd{i}")
        if i == 0:
            settle(ahead, 0, "1", gu)
            fetch(ahead, 1, "1")
            w_cin, w_cout, w_qkv, w_o, w_fin, w_fdown = weights
        act = _ffngate_fwd(gu, wf_conv[i], seq, f"ffn_gate_fwd{i}")
        x_next = _mm(act, w_fdown, "nn", F32, layer=i, residual=x_mid, tm=1024, tn=1024, tk=4096, name=f"ffn_down_fwd{i}")
        if ahead:
            settle(ahead, 1, str(i + 1), x_next)
        saved.append((xs, h, pre, mixed, lse, x_mid, h2, gu, act))
        xs = x_next
    w_cin, w_cout, w_qkv, w_o, w_fin, w_fdown = weights

    dx, dxb, sq, dg_final = _loss_head(xs, loss_target.reshape(t, d), norm_final.reshape(1, d), "loss_head")
    loss = lax.psum(0.5 * jnp.sum(sq) / d, ("x", "y", "c"))

    g_norm_mix, g_norm_ffn = [None] * depth, [None] * depth
    g_cin, g_cconv, g_cout = [None] * n_conv, [None] * n_conv, [None] * n_conv
    g_qkv, g_bqkv, g_sinks, g_o, g_bo = ([None] * n_attn for _ in range(5))
    g_fin, g_fconv, g_fdown = [None] * depth, [None] * depth, [None] * depth

    kinds6 = ["col", "row", "col", "row", "col", "row"]
    layers6 = [n_conv, n_conv, n_attn, n_attn, depth, depth]
    big_w = [conv_w_in, conv_w_out, attn_w_qkv, attn_w_o, ffn_w_in, ffn_w_down]

    def slot_stack(n):
        k, cols = big_w[n].shape[1], big_w[n].shape[2]
        r = k // 2
        return lax.empty((layers6[n], N_CHIPS, r, cols), BF16)

    own = [slot_stack(n) for n in range(6)]
    mine = [slot_stack(n) for n in range(6)]
    sib = [slot_stack(n) for n in range(6)]
    flight = {}

    def group(i, part):
        return f"{part}{i}", (pieces_of(i)[:2] if part == "m" else pieces_of(i)[2:])

    def scatter(grp, stage, action, token):
        tag, pieces = grp
        ts = [ti for ti, _ in pieces]
        local = [(n, l) for n, (_, l) in enumerate(pieces)]
        sub = ([own[ti] for ti in ts], [mine[ti] for ti in ts], [sib[ti] for ti in ts], token)
        label = f"grad_{'ici' if stage == 0 else 'pass'}_{action}_{tag}"
        if action == "start":
            o, mi, si, token, s_sems, r_sems = _scatter_start(*sub, local, stage, label)
            flight[tag] = (s_sems, r_sems)
        else:
            o, mi, si, token = _scatter_wait(*sub, *flight[tag], local, stage, label)
        for n, ti in enumerate(ts):
            own[ti], mine[ti], sib[ti] = o[n], mi[n], si[n]
        return token

    def reduce_begin(grp, token):
        tag, pieces = grp
        grads = {0: g_cin, 1: g_cout, 2: g_qkv, 3: g_o, 4: g_fin, 5: g_fdown}
        parts, kinds = [], []
        for ti, l in pieces:
            g = grads[ti][l]
            parts.append(g if kinds6[ti] == "col" else g.reshape(N_CHIPS, g.shape[0] // N_CHIPS, g.shape[1]))
            kinds.append(kinds6[ti])
        recv = _pair_exchange(parts, kinds, f"grad_pair_exchange_{tag}")
        for (ti, l), g, r in zip(pieces, parts, recv):
            own[ti] = _pair_sum(g, r, kinds6[ti], c_arr, own[ti], l, f"grad_pair_sum_{tag}_{ti}")
        return scatter(grp, 0, "start", token)

    for i in reversed(range(depth)):
        j = i // 2
        x_in, h, pre, mixed, lse, x_mid, h2, gu, act = saved[i]
        da = _mm(dxb, w_fdown, "nt", BF16, layer=i, n_outer=True, tm=2048, tn=1408, tk=4096, name=f"ffn_down_dx{i}")
        g_fdown[i] = _mm(act, dxb, "tn", BF16, tm=1408, tn=1024, tk=2048, name=f"ffn_down_dw{i}")
        dgu, dwc = _ffngate_bwd(gu, da, wf_conv[i], seq, f"ffn_gate_bwd{i}")
        g_fconv[i] = dwc[:3]
        g_fin[i] = _mm(h2, dgu, "tn", BF16, tm=1024, tn=1408, tk=2048, name=f"ffn_in_dw{i}")
        dx, dxb, dg, colsum = _rms_bwd(x_mid, dgu, w_fin, i, norm_ffn[i:i + 1], dx, f"ffn_in_dx_norm_bwd{i}")
        g_norm_ffn[i] = jnp.sum(dg, axis=0)
        if i + 1 < depth:
            dxb = scatter(group(i + 1, "m"), 1, "start", scatter(group(i + 1, "m"), 0, "wait", dxb))
            dxb = scatter(group(i + 1, "f"), 1, "wait", dxb)
        dxb = reduce_begin(group(i, "f"), dxb)
        if i % 2 == 0:
            dmix = _mm(dxb, w_cout, "nt", BF16, layer=j, tm=512, tn=1024, tk=4096, name=f"conv_out_dx{i}")
            g_cout[j] = _mm(mixed, dxb, "tn", BF16, tm=1024, tn=1024, tk=2048, name=f"conv_out_dw{i}")
            dpre, dwc = _convgate_bwd(pre, dmix, wc_conv[j], seq, f"conv_gate_bwd{i}")
            g_cconv[j] = dwc[:3]
            g_cin[j] = _mm(h, dpre, "tn", BF16, tm=1024, tn=1536, tk=2048, name=f"conv_in_dw{i}")
            w_pre = w_cin
        else:
            g_bo[j] = jnp.sum(colsum, axis=0)
            dmix = _mm(dxb, w_o, "nt", F32, layer=j, tm=512, tn=1024, tk=4096, name=f"attn_out_dx{i}")
            g_o[j] = _mm(mixed, dxb, "tn", BF16, tm=1024, tn=1024, tk=2048, name=f"attn_out_dw{i}")
            dpre, dbias, dsk = _attn_bwd(pre, mixed, lse, dmix, attn_sinks[j], cos_t, sin_t, bsz, seq, f"attn_bwd{i}",
                                         hp=GROUP)
            g_bqkv[j] = jnp.sum(dbias, axis=0)
            g_sinks[j] = jnp.sum(dsk, axis=1)
            g_qkv[j] = _mm(h, dpre, "tn", BF16, tm=1024, tn=1536, tk=2048, name=f"qkv_dw{i}")
            w_pre = w_qkv
        dx, dxb, dg, _ = _rms_bwd(x_in, dpre, w_pre, j, norm_mix[i:i + 1], dx, f"mixer_in_dx_norm_bwd{i}")
        g_norm_mix[i] = jnp.sum(dg, axis=0)
        dxb = scatter(group(i, "f"), 1, "start", scatter(group(i, "f"), 0, "wait", dxb))
        if i + 1 < depth:
            dxb = scatter(group(i + 1, "m"), 1, "wait", dxb)
        dxb = reduce_begin(group(i, "m"), dxb)
    grad_x = dx.reshape(bsz, seq, d)

    dxb = scatter(group(0, "m"), 1, "start", scatter(group(0, "m"), 0, "wait", dxb))
    scatter(group(0, "m"), 1, "wait", scatter(group(0, "f"), 1, "wait", dxb))
    big_m = [m_conv_w_in, m_conv_w_out, m_attn_w_qkv, m_attn_w_o, m_ffn_w_in, m_ffn_w_down]
    big_v = [v_conv_w_in, v_conv_w_out, v_attn_w_qkv, v_attn_w_o, v_ffn_w_in, v_ffn_w_down]
    big_names = ["conv_w_in", "conv_w_out", "attn_w_qkv", "attn_w_o", "ffn_w_in", "ffn_w_down"]
    qc_arr = jnp.stack([q, cq]).astype(jnp.int32)
    big_upd = [_reduce_adamw(own[n], mine[n], sib[n], big_w[n], big_m[n], big_v[n], qc_arr, f"adamw_{nm}")
               for n, nm in enumerate(big_names)]

    small = [jnp.stack(g_norm_mix), jnp.stack(g_norm_ffn), jnp.sum(dg_final, axis=0), jnp.stack(g_cconv),
             jnp.stack(g_bqkv), jnp.stack(g_sinks), jnp.stack(g_bo), jnp.stack(g_fconv)]
    sg = _unpack(_allreduce_small(_pack(small), "grad_small_allreduce"), small)

    def my_cols(v, like):
        width = like.shape[-1]
        return lax.dynamic_slice_in_dim(v, q * width, width, axis=v.ndim - 1)

    small_w = [norm_mix, norm_ffn, norm_final, conv_w_conv, attn_b_qkv, attn_sinks, attn_b_o, ffn_w_conv]
    small_m = [m_norm_mix, m_norm_ffn, m_norm_final, m_conv_w_conv, m_attn_b_qkv, m_attn_sinks, m_attn_b_o, m_ffn_w_conv]
    small_v = [v_norm_mix, v_norm_ffn, v_norm_final, v_conv_w_conv, v_attn_b_qkv, v_attn_sinks, v_attn_b_o, v_ffn_w_conv]
    small_g = [sg[0], sg[1], sg[2], my_cols(sg[3], conv_w_conv), my_cols(sg[4], attn_b_qkv), sg[5],
               my_cols(sg[6], attn_b_o), my_cols(sg[7], ffn_w_conv)]

    upd = {nm: tuple(u[1:]) for nm, u in zip(big_names, big_upd)}
    sd, sm, sv = _adamw(_pack(small_w)[None], _pack(small_g)[None], _pack(small_m)[None], _pack(small_v, 1.0)[None],
                        "adamw_small")
    sd, sm, sv = _unpack(sd, small_w), _unpack(sm, small_w), _unpack(sv, small_w)
    names = ["norm_mix", "norm_ffn", "norm_final", "conv_w_in", "conv_w_conv", "conv_w_out", "attn_w_qkv", "attn_b_qkv",
             "attn_sinks", "attn_w_o", "attn_b_o", "ffn_w_in", "ffn_w_conv", "ffn_w_down"]
    small_names = ["norm_mix", "norm_ffn", "norm_final", "conv_w_conv", "attn_b_qkv", "attn_sinks", "attn_b_o", "ffn_w_conv"]
    grads = dict(zip(small_names, small_g))
    grads.update({nm: u[0] for nm, u in zip(big_names, big_upd)})
    for n, nm in enumerate(small_names):
        upd[nm] = (sd[n], sm[n], sv[n])
    return (loss, grad_x, *[grads[nm] for nm in names], *[upd[nm][0] for nm in names],
            *[upd[nm][1] for nm in names], *[upd[nm][2] for nm in names])
```

```python
import math

import jax
import jax.numpy as jnp
from jax import lax
from jax.experimental import pallas as pl
from jax.experimental.pallas import tpu as pltpu

F32 = jnp.float32
BF16 = jnp.bfloat16

HEAD_DIM = 64
GROUP = 4
WINDOW = 128
EPS = 1e-5
ROPE_THETA = 10000.0
ADAM_LR, ADAM_B1, ADAM_B2, ADAM_EPS, ADAM_WD, ADAM_STEP = 0.001, 0.9, 0.999, 1e-08, 0.01, 10

N_CHIPS = 4
N_CORES = 2
N_DEV = 8
HALO = 16
VMEM_LIMIT_BYTES = 56 * 1024 * 1024
MESH = pl.DeviceIdType.MESH
ANY = pl.BlockSpec(memory_space=pl.ANY)
SMEM = pl.BlockSpec(memory_space=pltpu.SMEM)
NEG = float(jnp.finfo(jnp.float32).min)
ROW_TILES = (512, 256, 128, 64, 32, 16, 8)


def _pick(dim, cands):
    for c in cands:
        if dim % c == 0:
            return c
    return dim


def _params(sem):
    return pltpu.CompilerParams(dimension_semantics=sem, vmem_limit_bytes=VMEM_LIMIT_BYTES)


_DIMS = {"nn": (((1,), (0,)), ((), ())), "nt": (((1,), (1,)), ((), ())), "tn": (((0,), (0,)), ((), ()))}


def _mm(a, b, mode, out_dtype, *, layer=None, bias=None, residual=None, n_outer=False, tm, tn, tk, name):
    b2 = b.shape[1:] if layer is not None else b.shape
    if mode == "nn":
        (m, k), n = a.shape, b2[1]
    elif mode == "nt":
        (m, k), n = a.shape, b2[0]
    else:
        (k, m), n = a.shape, b2[1]
    tm, tn, tk = min(tm, m), min(tn, n), min(tk, k)
    assert m % tm == 0 and n % tn == 0 and k % tk == 0, (name, a.shape, b.shape, tm, tn, tk)
    nk = k // tk

    def at(f):
        return (lambda p0, p1, p2: f(p1, p0, p2)) if n_outer else f

    a_spec = pl.BlockSpec((tk, tm), at(lambda i, j, l: (l, i))) if mode == "tn" else pl.BlockSpec((tm, tk), at(lambda i, j, l: (i, l)))
    if layer is None:
        b_spec = (pl.BlockSpec((tn, tk), at(lambda i, j, l: (j, l))) if mode == "nt"
                  else pl.BlockSpec((tk, tn), at(lambda i, j, l: (l, j))))
    elif mode == "nt":
        b_spec = pl.BlockSpec((None, tn, tk), at(lambda i, j, l: (layer, j, l)))
    else:
        b_spec = pl.BlockSpec((None, tk, tn), at(lambda i, j, l: (layer, l, j)))
    in_specs, args = [a_spec, b_spec], [a, b]
    if bias is not None:
        in_specs.append(pl.BlockSpec((1, tn), at(lambda i, j, l: (0, j))))
        args.append(bias)
    if residual is not None:
        in_specs.append(pl.BlockSpec((tm, tn), at(lambda i, j, l: (i, j))))
        args.append(residual)
    has_bias, has_res = bias is not None, residual is not None

    def body(*refs):
        a_ref, b_ref = refs[0], refs[1]
        pos = 2
        bias_ref = res_ref = None
        if has_bias:
            bias_ref, pos = refs[pos], pos + 1
        if has_res:
            res_ref, pos = refs[pos], pos + 1
        o_ref = refs[pos]
        acc_ref = refs[pos + 1] if nk > 1 else None

        def finish(acc):
            if has_bias:
                acc = acc + bias_ref[...]
            if has_res:
                acc = acc + res_ref[...]
            o_ref[...] = acc.astype(o_ref.dtype)

        if nk == 1:
            finish(lax.dot_general(a_ref[...], b_ref[...], _DIMS[mode], preferred_element_type=F32))
            return
        l = pl.program_id(2)
        part = lax.dot_general(a_ref[...], b_ref[...], _DIMS[mode], preferred_element_type=F32)

        @pl.when(l == 0)
        def _():
            acc_ref[...] = part

        @pl.when(l > 0)
        def _():
            acc_ref[...] += part

        @pl.when(l == nk - 1)
        def _():
            finish(acc_ref[...])

    return pl.pallas_call(
        body,
        name=name,
        grid=(n // tn, m // tm, nk) if n_outer else (m // tm, n // tn, nk),
        in_specs=in_specs,
        out_specs=pl.BlockSpec((tm, tn), at(lambda i, j, l: (i, j))),
        out_shape=jax.ShapeDtypeStruct((m, n), out_dtype),
        scratch_shapes=[pltpu.VMEM((tm, tn), F32)] if nk > 1 else [],
        compiler_params=_params(("parallel", "parallel", "arbitrary")),
    )(*args)


def _fold8(v):
    r, d = v.shape
    return jnp.sum(v.reshape(r // 8, 8, d), axis=0)


def _rms_fwd(x, g, name):
    t, d = x.shape
    tm = _pick(t, ROW_TILES)

    def body(x_ref, g_ref, h_ref):
        xv = x_ref[...]
        r = lax.rsqrt(jnp.mean(xv * xv, axis=-1, keepdims=True) + EPS)
        h_ref[...] = (xv * r * g_ref[...]).astype(BF16)

    return pl.pallas_call(
        body,
        name=name,
        grid=(t // tm,),
        in_specs=[pl.BlockSpec((tm, d), lambda i: (i, 0)), pl.BlockSpec((1, d), lambda i: (0, 0))],
        out_specs=pl.BlockSpec((tm, d), lambda i: (i, 0)),
        out_shape=jax.ShapeDtypeStruct((t, d), BF16),
        compiler_params=_params(("parallel",)),
    )(x, g)


def _rms_bwd(x, dpre, w, layer, g, dx_in, name):
    t, d = x.shape
    k = dpre.shape[1]
    tm = _pick(t, ROW_TILES)

    def body(x_ref, dp_ref, w_ref, g_ref, dxi_ref, dx_ref, dxb_ref, dg_ref, cs_ref):
        i = pl.program_id(0)
        xv = x_ref[...]
        r = lax.rsqrt(jnp.mean(xv * xv, axis=-1, keepdims=True) + EPS)
        xhat = xv * r
        dy = lax.dot_general(dp_ref[...], w_ref[...], _DIMS["nt"], preferred_element_type=F32)
        gdy = dy * g_ref[...]
        dx = dxi_ref[...] + r * (gdy - xhat * jnp.mean(gdy * xhat, axis=-1, keepdims=True))
        dx_ref[...] = dx
        dxb_ref[...] = dx.astype(BF16)

        @pl.when(i == 0)
        def _():
            dg_ref[...] = jnp.zeros_like(dg_ref)
            cs_ref[...] = jnp.zeros_like(cs_ref)

        dg_ref[...] += _fold8(dy * xhat)
        cs_ref[...] += _fold8(dx)

    row = pl.BlockSpec((tm, d), lambda i: (i, 0))
    acc = pl.BlockSpec((8, d), lambda i: (0, 0))
    w_spec = pl.BlockSpec((None, d, k), lambda i: (layer, 0, 0), pipeline_mode=pl.Buffered(1))
    return pl.pallas_call(
        body,
        name=name,
        grid=(t // tm,),
        in_specs=[row, pl.BlockSpec((tm, k), lambda i: (i, 0)), w_spec, pl.BlockSpec((1, d), lambda i: (0, 0)), row],
        out_specs=[row, row, acc, acc],
        out_shape=[jax.ShapeDtypeStruct((t, d), F32), jax.ShapeDtypeStruct((t, d), BF16),
                   jax.ShapeDtypeStruct((8, d), F32), jax.ShapeDtypeStruct((8, d), F32)],
        compiler_params=_params(("arbitrary",)),
    )(x, dpre, w, g, dx_in)


def _loss_head(x, target, g, name):
    t, d = x.shape
    tm = _pick(t, ROW_TILES)
    inv_d = 1.0 / d

    def body(x_ref, t_ref, g_ref, dx_ref, dxb_ref, sq_ref, dg_ref):
        i = pl.program_id(0)
        xv = x_ref[...]
        gv = g_ref[...]
        r = lax.rsqrt(jnp.mean(xv * xv, axis=-1, keepdims=True) + EPS)
        xhat = xv * r
        err = xhat * gv - t_ref[...]
        dy = err * inv_d
        gdy = dy * gv
        dx = r * (gdy - xhat * jnp.mean(gdy * xhat, axis=-1, keepdims=True))
        dx_ref[...] = dx
        dxb_ref[...] = dx.astype(BF16)

        @pl.when(i == 0)
        def _():
            sq_ref[...] = jnp.zeros_like(sq_ref)
            dg_ref[...] = jnp.zeros_like(dg_ref)

        sq_ref[...] += _fold8(err * err)
        dg_ref[...] += _fold8(dy * xhat)

    row = pl.BlockSpec((tm, d), lambda i: (i, 0))
    acc = pl.BlockSpec((8, d), lambda i: (0, 0))
    return pl.pallas_call(
        body,
        name=name,
        grid=(t // tm,),
        in_specs=[row, row, pl.BlockSpec((1, d), lambda i: (0, 0))],
        out_specs=[row, row, acc, acc],
        out_shape=[jax.ShapeDtypeStruct((t, d), F32), jax.ShapeDtypeStruct((t, d), BF16),
                   jax.ShapeDtypeStruct((8, d), F32), jax.ShapeDtypeStruct((8, d), F32)],
        compiler_params=_params(("arbitrary",)),
    )(x, target, g)


def _rows(tm):
    return lax.broadcasted_iota(jnp.int32, (tm, 1), 0)


def _shift_down(u, before2):
    r8 = _rows(8)
    s1, s2 = pltpu.roll(u, 1, 0), pltpu.roll(u, 2, 0)
    top1 = jnp.where(r8 == 0, before2[1:2], s1[:8])
    top2 = jnp.where(r8 == 0, before2[0:1], jnp.where(r8 == 1, before2[1:2], s2[:8]))
    return jnp.concatenate([top1, s1[8:]], axis=0), jnp.concatenate([top2, s2[8:]], axis=0)


def _shift_up(u, after2):
    tm = u.shape[0]
    r8 = _rows(8)
    s1, s2 = pltpu.roll(u, tm - 1, 0), pltpu.roll(u, tm - 2, 0)
    bot1 = jnp.where(r8 == 7, after2[0:1], s1[tm - 8:])
    bot2 = jnp.where(r8 == 6, after2[0:1], jnp.where(r8 == 7, after2[1:2], s2[tm - 8:]))
    return jnp.concatenate([s1[:tm - 8], bot1], axis=0), jnp.concatenate([s2[:tm - 8], bot2], axis=0)


def _shift_matrix(tm, up):
    r = lax.broadcasted_iota(jnp.int32, (2 * tm, tm), 0)
    c = lax.broadcasted_iota(jnp.int32, (2 * tm, tm), 1)
    t = jnp.where(r >= tm, r - tm, r)
    k = jnp.where(r >= tm, 2, 1)
    return (c == (t + k if up else t - k)).astype(BF16)


def _shift_down_mxu(u, before2):
    tm = u.shape[0]
    moved = jnp.dot(_shift_matrix(tm, False), u.astype(BF16), preferred_element_type=F32)
    r8 = _rows(8)
    s1, s2 = moved[:tm], moved[tm:]
    top1 = s1[:8] + jnp.where(r8 == 0, before2[1:2], 0.0)
    top2 = s2[:8] + jnp.where(r8 == 0, before2[0:1], jnp.where(r8 == 1, before2[1:2], 0.0))
    return jnp.concatenate([top1, s1[8:]], axis=0), jnp.concatenate([top2, s2[8:]], axis=0)


def _shift_up_mxu(u, after2):
    tm = u.shape[0]
    moved = jnp.dot(_shift_matrix(tm, True), u.astype(BF16), preferred_element_type=F32)
    r8 = _rows(8)
    s1, s2 = moved[:tm], moved[tm:]
    bot1 = s1[tm - 8:] + jnp.where(r8 == 7, after2[0:1], 0.0)
    bot2 = s2[tm - 8:] + jnp.where(r8 == 6, after2[0:1], jnp.where(r8 == 7, after2[1:2], 0.0))
    return jnp.concatenate([s1[:tm - 8], bot1], axis=0), jnp.concatenate([s2[:tm - 8], bot2], axis=0)


def _conv_tile(seq):
    return _pick(seq, (256, 128, 64, 32, 16, 8))


def _halo_specs(tm, width, n_tiles):
    per = tm // HALO
    before = pl.BlockSpec((HALO, width), lambda i: (jnp.maximum(i * per - 1, 0), 0))
    after = pl.BlockSpec((HALO, width), lambda i: (jnp.minimum((i + 1) * per, n_tiles * per - 1), 0))
    return before, after


def _convgate_fwd(bcv, w, seq, name):
    t, d3 = bcv.shape
    d = d3 // 3
    tm = _conv_tile(seq)
    tps = seq // tm
    before, _ = _halo_specs(tm, d3, t // tm)

    def body(x_ref, xb_ref, w_ref, y_ref):
        i = pl.program_id(0)
        inner = (i % tps != 0).astype(F32)
        u = x_ref[:, d:2 * d].astype(F32) * x_ref[:, 2 * d:].astype(F32)
        xb = xb_ref[:, d:].astype(F32)[HALO - 2:]
        s1, s2 = _shift_down(u, xb[:, :d] * xb[:, d:] * inner)
        z = w_ref[2:3] * u + w_ref[1:2] * s1 + w_ref[0:1] * s2
        y_ref[...] = (x_ref[:, :d].astype(F32) * z).astype(BF16)

    return pl.pallas_call(
        body,
        name=name,
        grid=(t // tm,),
        in_specs=[pl.BlockSpec((tm, d3), lambda i: (i, 0)), before, pl.BlockSpec((3, d), lambda i: (0, 0))],
        out_specs=pl.BlockSpec((tm, d), lambda i: (i, 0)),
        out_shape=jax.ShapeDtypeStruct((t, d), BF16),
        compiler_params=_params(("parallel",)),
    )(bcv, bcv, w)


def _convgate_bwd(bcv, dy, w, seq, name):
    t, d3 = bcv.shape
    d = d3 // 3
    tm = _conv_tile(seq)
    tps = seq // tm
    before, after = _halo_specs(tm, d3, t // tm)
    _, after_dy = _halo_specs(tm, d, t // tm)

    def body(x_ref, xb_ref, xa_ref, dy_ref, dya_ref, w_ref, dx_ref, dw_ref):
        i = pl.program_id(0)
        inner_lo = (i % tps != 0).astype(F32)
        inner_hi = (i % tps != tps - 1).astype(F32)
        w0, w1, w2 = w_ref[0:1], w_ref[1:2], w_ref[2:3]
        b, c, v = x_ref[:, :d].astype(F32), x_ref[:, d:2 * d].astype(F32), x_ref[:, 2 * d:].astype(F32)
        u = c * v
        xb = xb_ref[:, d:].astype(F32)[HALO - 2:]
        s1, s2 = _shift_down(u, xb[:, :d] * xb[:, d:] * inner_lo)
        z = w2 * u + w1 * s1 + w0 * s2
        dyv = dy_ref[...].astype(F32)
        dz = dyv * b
        dza = dya_ref[...].astype(F32)[0:2] * xa_ref[:, :d].astype(F32)[0:2] * inner_hi
        n1, n2 = _shift_up(dz, dza)
        du = w2 * dz + w1 * n1 + w0 * n2
        dx_ref[:, :d] = (dyv * z).astype(BF16)
        dx_ref[:, d:2 * d] = (du * v).astype(BF16)
        dx_ref[:, 2 * d:] = (du * c).astype(BF16)

        @pl.when(i == 0)
        def _():
            dw_ref[...] = jnp.zeros_like(dw_ref)

        dw_ref[0:1] += jnp.sum(dz * s2, axis=0, keepdims=True)
        dw_ref[1:2] += jnp.sum(dz * s1, axis=0, keepdims=True)
        dw_ref[2:3] += jnp.sum(dz * u, axis=0, keepdims=True)

    return pl.pallas_call(
        body,
        name=name,
        grid=(t // tm,),
        in_specs=[pl.BlockSpec((tm, d3), lambda i: (i, 0)), before, after,
                  pl.BlockSpec((tm, d), lambda i: (i, 0)), after_dy, pl.BlockSpec((3, d), lambda i: (0, 0))],
        out_specs=[pl.BlockSpec((tm, d3), lambda i: (i, 0)), pl.BlockSpec((8, d), lambda i: (0, 0))],
        out_shape=[jax.ShapeDtypeStruct((t, d3), BF16), jax.ShapeDtypeStruct((8, d), F32)],
        compiler_params=_params(("arbitrary",)),
    )(bcv, bcv, bcv, dy, dy, w)


def _sigmoid(x):
    return 1.0 / (1.0 + jnp.exp(-x))


def _ffn_gate_down_fwd(gu, w, w_down, layer, resid, seq, name):
    t, f2 = gu.shape
    f = f2 // 2
    d = w_down.shape[2]
    sub = _conv_tile(seq)
    tm = _pick(seq, (2 * sub, sub))
    tps = seq // tm
    before, _ = _halo_specs(tm, f2, t // tm)

    def body(x_ref, xb_ref, w_ref, wd_ref, res_ref, o_ref, a_ref):
        i = pl.program_id(0)
        inner = (i % tps != 0).astype(F32)
        for r0 in range(0, tm, sub):
            rows = slice(r0, r0 + sub)
            if r0 == 0:
                halo = xb_ref[:, :f].astype(F32)[HALO - 2:] * inner
            else:
                halo = x_ref[r0 - HALO:r0, :f].astype(F32)[HALO - 2:]
            s1, s2 = _shift_down_mxu(x_ref[rows, :f], halo)
            gc = w_ref[2:3] * x_ref[rows, :f].astype(F32) + w_ref[1:2] * s1 + w_ref[0:1] * s2
            act = (gc * _sigmoid(gc) * x_ref[rows, f:].astype(F32)).astype(BF16)
            a_ref[rows, :] = act
            o_ref[rows, :] = jnp.dot(act, wd_ref[...], preferred_element_type=F32) + res_ref[rows, :]

    return pl.pallas_call(
        body,
        name=name,
        grid=(t // tm,),
        in_specs=[pl.BlockSpec((tm, f2), lambda i: (i, 0)), before, pl.BlockSpec((3, f), lambda i: (0, 0)),
                  pl.BlockSpec((None, f, d), lambda i: (layer, 0, 0), pipeline_mode=pl.Buffered(1)),
                  pl.BlockSpec((tm, d), lambda i: (i, 0))],
        out_specs=[pl.BlockSpec((tm, d), lambda i: (i, 0)), pl.BlockSpec((tm, f), lambda i: (i, 0))],
        out_shape=[jax.ShapeDtypeStruct((t, d), F32), jax.ShapeDtypeStruct((t, f), BF16)],
        compiler_params=_params(("parallel",)),
    )(gu, gu, w, w_down, resid)


def _ffngate_bwd(gu, da, w, seq, name):
    t, f2 = gu.shape
    f = f2 // 2
    tm = _conv_tile(seq)
    tps = seq // tm
    before, after = _halo_specs(tm, f2, t // tm)
    _, after_da = _halo_specs(tm, f, t // tm)

    def body(x_ref, xb_ref, xa_ref, da_ref, daa_ref, w_ref, dx_ref, dw_ref):
        i = pl.program_id(0)
        inner_lo = (i % tps != 0).astype(F32)
        inner_hi = (i % tps != tps - 1).astype(F32)
        w0, w1, w2 = w_ref[0:1], w_ref[1:2], w_ref[2:3]

        def dgate(gc, uv, dav):
            sg = _sigmoid(gc)
            return dav * uv * (sg * (1.0 + gc * (1.0 - sg))), dav * (gc * sg)

        g, u = x_ref[:, :f].astype(F32), x_ref[:, f:].astype(F32)
        s1, s2 = _shift_down_mxu(x_ref[:, :f], xb_ref[:, :f].astype(F32)[HALO - 2:] * inner_lo)
        gc = w2 * g + w1 * s1 + w0 * s2
        dgc, du = dgate(gc, u, da_ref[...].astype(F32))
        ga = xa_ref[:, :f].astype(F32)
        a1, a2 = _shift_down(ga, x_ref[tm - HALO:, :f].astype(F32)[HALO - 2:])
        gca = w2 * ga + w1 * a1 + w0 * a2
        dgca, _ = dgate(gca, xa_ref[:, f:].astype(F32), daa_ref[...].astype(F32))
        n1, n2 = _shift_up_mxu(dgc, dgca[0:2] * inner_hi)
        dx_ref[:, :f] = (w2 * dgc + w1 * n1 + w0 * n2).astype(BF16)
        dx_ref[:, f:] = du.astype(BF16)

        @pl.when(i == 0)
        def _():
            dw_ref[...] = jnp.zeros_like(dw_ref)

        dw_ref[0:1] += jnp.sum(dgc * s2, axis=0, keepdims=True)
        dw_ref[1:2] += jnp.sum(dgc * s1, axis=0, keepdims=True)
        dw_ref[2:3] += jnp.sum(dgc * g, axis=0, keepdims=True)

    return pl.pallas_call(
        body,
        name=name,
        grid=(t // tm,),
        in_specs=[pl.BlockSpec((tm, f2), lambda i: (i, 0)), before, after,
                  pl.BlockSpec((tm, f), lambda i: (i, 0)), after_da, pl.BlockSpec((3, f), lambda i: (0, 0))],
        out_specs=[pl.BlockSpec((tm, f2), lambda i: (i, 0)), pl.BlockSpec((8, f), lambda i: (0, 0))],
        out_shape=[jax.ShapeDtypeStruct((t, f2), BF16), jax.ShapeDtypeStruct((8, f), F32)],
        compiler_params=_params(("arbitrary",)),
    )(gu, gu, gu, da, da, w)


def _swap_halves(xt):
    half = HEAD_DIM // 2
    return jnp.concatenate([xt[half:], xt[:half]], axis=0)


def _rope(xt, cos, sin):
    return xt * cos + _swap_halves(xt) * sin


def _unrope(dxt, cos, sin):
    return dxt * cos - _swap_halves(dxt) * sin


def _key_query(count):
    kj = lax.broadcasted_iota(jnp.int32, (WINDOW, count * WINDOW), 0)
    qi = lax.broadcasted_iota(jnp.int32, (WINDOW, count * WINDOW), 1) & (WINDOW - 1)
    return kj, qi


def _band_masks(n, count):
    kj, qi = _key_query(count)
    return kj <= qi, jnp.logical_and(kj > qi, n > 0)


def _lanes(v, count):
    return jnp.concatenate([v] * count, axis=1) if count > 1 else v


def _heads(ref, h0, count):
    parts = [ref[(h0 + g) * HEAD_DIM:(h0 + g + 1) * HEAD_DIM, :] for g in range(count)]
    return jnp.concatenate(parts, axis=1) if count > 1 else parts[0]


def _head_rows(ref, h0, count):
    parts = [ref[h0 + g:h0 + g + 1, :] for g in range(count)]
    return jnp.concatenate(parts, axis=1) if count > 1 else parts[0]


def _head_sinks(sink_ref, h0, count):
    parts = [jnp.full((1, WINDOW), sink_ref[h0 + g], F32) for g in range(count)]
    return jnp.concatenate(parts, axis=1) if count > 1 else parts[0]


def _tn(a, b):
    return lax.dot_general(a, b, _DIMS["tn"], preferred_element_type=F32)


def _nt(a, b):
    return lax.dot_general(a, b, _DIMS["nt"], preferred_element_type=F32)


def _nn(a, b):
    return jnp.dot(a, b, preferred_element_type=F32)


def _attn_fwd(qkv, sinks, cos_t, sin_t, bsz, seq, name, hp):
    t, qw = qkv.shape
    d = qw * 2 // 3
    kvw = d // GROUP
    n_heads, n_kv = d // HEAD_DIM, kvw // HEAD_DIM
    nb = seq // WINDOW
    scale = HEAD_DIM ** -0.5

    def body(sink_ref, xc_ref, xp_ref, cc_ref, sc_ref, cp_ref, sp_ref, o_ref, lse_ref, xt_ref, pt_ref, ot_ref):
        n = pl.program_id(1)
        xt_ref[...] = xc_ref[...].T
        pt_ref[...] = xp_ref[:, d:].T
        cos_c, sin_c, cos_p, sin_p = cc_ref[...], sc_ref[...], cp_ref[...], sp_ref[...]
        cos_g, sin_g = _lanes(cos_c, hp), _lanes(sin_c, hp)
        valid_c, valid_p = _band_masks(n, hp)
        for j in range(n_kv):
            ko = j * HEAD_DIM
            kc = _rope(xt_ref[d + ko:d + ko + HEAD_DIM, :], cos_c, sin_c).astype(BF16)
            kp = _rope(pt_ref[ko:ko + HEAD_DIM, :], cos_p, sin_p).astype(BF16)
            vc = xt_ref[d + kvw + ko:d + kvw + ko + HEAD_DIM, :].astype(BF16)
            vp = pt_ref[kvw + ko:kvw + ko + HEAD_DIM, :].astype(BF16)
            for h0 in range(j * GROUP, (j + 1) * GROUP, hp):
                q = _rope(_heads(xt_ref, h0, hp), cos_g, sin_g).astype(BF16)
                sink = _head_sinks(sink_ref, h0, hp)
                s_c = jnp.where(valid_c, _tn(kc, q) * scale, NEG)
                s_p = jnp.where(valid_p, _tn(kp, q) * scale, NEG)
                m = jnp.maximum(jnp.maximum(jnp.max(s_c, axis=0, keepdims=True), jnp.max(s_p, axis=0, keepdims=True)), sink)
                p_c = jnp.exp(s_c - m)
                p_p = jnp.exp(s_p - m)
                den = jnp.sum(p_c, axis=0, keepdims=True) + jnp.sum(p_p, axis=0, keepdims=True) + jnp.exp(sink - m)
                inv = 1.0 / den
                o_g = _nn(vc, (p_c * inv).astype(BF16)) + _nn(vp, (p_p * inv).astype(BF16))
                lse_g = m + jnp.log(den)
                for g in range(hp):
                    h = h0 + g
                    ot_ref[h * HEAD_DIM:(h + 1) * HEAD_DIM, :] = o_g[:, g * WINDOW:(g + 1) * WINDOW]
                    lse_ref[h:h + 1, :] = lse_g[:, g * WINDOW:(g + 1) * WINDOW]
        o_ref[...] = ot_ref[...].T.astype(BF16)

    cur = lambda b, n: (b * nb + n, 0)
    prev = lambda b, n: (b * nb + jnp.maximum(n - 1, 0), 0)
    tab_c = pl.BlockSpec((HEAD_DIM, WINDOW), lambda b, n: (0, n))
    tab_p = pl.BlockSpec((HEAD_DIM, WINDOW), lambda b, n: (0, jnp.maximum(n - 1, 0)))
    return pl.pallas_call(
        body,
        name=name,
        grid=(bsz, nb),
        in_specs=[SMEM, pl.BlockSpec((WINDOW, qw), cur), pl.BlockSpec((WINDOW, qw), prev), tab_c, tab_c, tab_p, tab_p],
        out_specs=[pl.BlockSpec((WINDOW, d), cur), pl.BlockSpec((n_heads, WINDOW), lambda b, n: (0, b * nb + n))],
        out_shape=[jax.ShapeDtypeStruct((t, d), BF16), jax.ShapeDtypeStruct((n_heads, t), F32)],
        scratch_shapes=[pltpu.VMEM((qw, WINDOW), F32), pltpu.VMEM((2 * kvw, WINDOW), F32), pltpu.VMEM((d, WINDOW), F32)],
        compiler_params=_params(("parallel", "arbitrary")),
    )(sinks, qkv, qkv, cos_t, sin_t, cos_t, sin_t)


def _attn_bwd(qkv, o, lse, do, sinks, cos_t, sin_t, bsz, seq, name, hp):
    t, qw = qkv.shape
    d = qw * 2 // 3
    kvw = d // GROUP
    n_heads, n_kv = d // HEAD_DIM, kvw // HEAD_DIM
    nb = seq // WINDOW
    scale = HEAD_DIM ** -0.5

    def body(sink_ref, xc_ref, xp_ref, xn_ref, oc_ref, on_ref, doc_ref, don_ref, lc_ref, ln_ref,
             cc_ref, sc_ref, cp_ref, sp_ref, cn_ref, sn_ref,
             dx_ref, db_ref, dsk_ref, xt_ref, pt_ref, qn_ref, otc_ref, otn_ref, dtc_ref, dtn_ref, gt_ref):
        b, n = pl.program_id(0), pl.program_id(1)
        xt_ref[...] = xc_ref[...].T
        pt_ref[...] = xp_ref[:, d:].T
        qn_ref[...] = xn_ref[:, :d].T
        otc_ref[...] = oc_ref[...].astype(F32).T
        otn_ref[...] = on_ref[...].astype(F32).T
        dtc_ref[...] = doc_ref[...].T
        dtn_ref[...] = don_ref[...].T
        cos_c, sin_c, cos_p, sin_p, cos_n, sin_n = (cc_ref[...], sc_ref[...], cp_ref[...], sp_ref[...],
                                                    cn_ref[...], sn_ref[...])
        cos_g, sin_g, cos_gn, sin_gn = _lanes(cos_c, hp), _lanes(sin_c, hp), _lanes(cos_n, hp), _lanes(sin_n, hp)
        valid_c, valid_p = _band_masks(n, hp)
        kj, qi = _key_query(hp)
        valid_n = jnp.logical_and(kj > qi, n < nb - 1)

        @pl.when(jnp.logical_and(b == 0, n == 0))
        def _():
            db_ref[...] = jnp.zeros_like(db_ref)
            dsk_ref[...] = jnp.zeros_like(dsk_ref)

        for j in range(n_kv):
            ko = j * HEAD_DIM
            kc = _rope(xt_ref[d + ko:d + ko + HEAD_DIM, :], cos_c, sin_c).astype(BF16)
            kp = _rope(pt_ref[ko:ko + HEAD_DIM, :], cos_p, sin_p).astype(BF16)
            vc = xt_ref[d + kvw + ko:d + kvw + ko + HEAD_DIM, :].astype(BF16)
            vp = pt_ref[kvw + ko:kvw + ko + HEAD_DIM, :].astype(BF16)
            dk = jnp.zeros((HEAD_DIM, WINDOW), F32)
            dv = jnp.zeros((HEAD_DIM, WINDOW), F32)
            for h0 in range(j * GROUP, (j + 1) * GROUP, hp):
                q = _rope(_heads(xt_ref, h0, hp), cos_g, sin_g).astype(BF16)
                do_g = _heads(dtc_ref, h0, hp)
                do_b = do_g.astype(BF16)
                lse_g = _head_rows(lc_ref, h0, hp)
                delta = jnp.sum(_heads(otc_ref, h0, hp) * do_g, axis=0, keepdims=True)
                p_c = jnp.exp(jnp.where(valid_c, _tn(kc, q) * scale, NEG) - lse_g)
                p_p = jnp.exp(jnp.where(valid_p, _tn(kp, q) * scale, NEG) - lse_g)
                ds_c = (p_c * (_tn(vc, do_b) - delta)).astype(BF16)
                ds_p = (p_p * (_tn(vp, do_b) - delta)).astype(BF16)
                dq = _unrope((_nn(kc, ds_c) + _nn(kp, ds_p)) * scale, cos_g, sin_g)
                dsk = -jnp.exp(_head_sinks(sink_ref, h0, hp) - lse_g) * delta
                for g in range(hp):
                    h = h0 + g
                    gt_ref[h * HEAD_DIM:(h + 1) * HEAD_DIM, :] = dq[:, g * WINDOW:(g + 1) * WINDOW]
                    dsk_ref[h:h + 1, :] += dsk[:, g * WINDOW:(g + 1) * WINDOW]
                q2 = _rope(_heads(qn_ref, h0, hp), cos_gn, sin_gn).astype(BF16)
                do2 = _heads(dtn_ref, h0, hp)
                do2_b = do2.astype(BF16)
                delta2 = jnp.sum(_heads(otn_ref, h0, hp) * do2, axis=0, keepdims=True)
                p_n = jnp.exp(jnp.where(valid_n, _tn(kc, q2) * scale, NEG) - _head_rows(ln_ref, h0, hp))
                ds_n = (p_n * (_tn(vc, do2_b) - delta2)).astype(BF16)
                dv += _nt(do_b, p_c.astype(BF16)) + _nt(do2_b, p_n.astype(BF16))
                dk += _nt(q, ds_c) + _nt(q2, ds_n)
            gt_ref[d + ko:d + ko + HEAD_DIM, :] = _unrope(dk * scale, cos_c, sin_c)
            gt_ref[d + kvw + ko:d + kvw + ko + HEAD_DIM, :] = dv
        dx = gt_ref[...].T
        dx_ref[...] = dx.astype(BF16)
        db_ref[...] += _fold8(dx)

    cur = lambda b, n: (b * nb + n, 0)
    prev = lambda b, n: (b * nb + jnp.maximum(n - 1, 0), 0)
    nxt = lambda b, n: (b * nb + jnp.minimum(n + 1, nb - 1), 0)
    stat_c = pl.BlockSpec((n_heads, WINDOW), lambda b, n: (0, b * nb + n))
    stat_n = pl.BlockSpec((n_heads, WINDOW), lambda b, n: (0, b * nb + jnp.minimum(n + 1, nb - 1)))
    tab_c = pl.BlockSpec((HEAD_DIM, WINDOW), lambda b, n: (0, n))
    tab_p = pl.BlockSpec((HEAD_DIM, WINDOW), lambda b, n: (0, jnp.maximum(n - 1, 0)))
    tab_n = pl.BlockSpec((HEAD_DIM, WINDOW), lambda b, n: (0, jnp.minimum(n + 1, nb - 1)))
    return pl.pallas_call(
        body,
        name=name,
        grid=(bsz, nb),
        in_specs=[SMEM, pl.BlockSpec((WINDOW, qw), cur), pl.BlockSpec((WINDOW, qw), prev), pl.BlockSpec((WINDOW, qw), nxt),
                  pl.BlockSpec((WINDOW, d), cur), pl.BlockSpec((WINDOW, d), nxt),
                  pl.BlockSpec((WINDOW, d), cur), pl.BlockSpec((WINDOW, d), nxt),
                  stat_c, stat_n, tab_c, tab_c, tab_p, tab_p, tab_n, tab_n],
        out_specs=[pl.BlockSpec((WINDOW, qw), cur), pl.BlockSpec((8, qw), lambda b, n: (0, 0)),
                   pl.BlockSpec((n_heads, WINDOW), lambda b, n: (0, 0))],
        out_shape=[jax.ShapeDtypeStruct((t, qw), BF16), jax.ShapeDtypeStruct((8, qw), F32),
                   jax.ShapeDtypeStruct((n_heads, WINDOW), F32)],
        scratch_shapes=[pltpu.VMEM((qw, WINDOW), F32), pltpu.VMEM((2 * kvw, WINDOW), F32), pltpu.VMEM((d, WINDOW), F32),
                        pltpu.VMEM((d, WINDOW), F32), pltpu.VMEM((d, WINDOW), F32), pltpu.VMEM((d, WINDOW), F32),
                        pltpu.VMEM((d, WINDOW), F32), pltpu.VMEM((qw, WINDOW), F32)],
        compiler_params=_params(("arbitrary", "arbitrary")),
    )(sinks, qkv, qkv, qkv, o, o, do, do, lse, lse, cos_t, sin_t, cos_t, sin_t, cos_t, sin_t)


def _place():
    return lax.axis_index("x"), lax.axis_index("y"), lax.axis_index("c")


def _other_chips(x, y):
    return [(1 - x, y), (x, 1 - y), (1 - x, 1 - y)]


def _place_shard(w, axis, q, name):
    ly, k, n = w.shape
    tr = _pick(k, (256, 128, 64, 32, 16, 8))
    steps = k // tr
    shape = (ly, k * N_CHIPS, n) if axis == 1 else (ly, k, n * N_CHIPS)
    if axis == 1:
        out_spec = pl.BlockSpec((None, tr, n), lambda l, i, q_ref: (l, q_ref[0] * steps + i, 0))
    else:
        out_spec = pl.BlockSpec((None, tr, n), lambda l, i, q_ref: (l, i, q_ref[0]))

    def body(q_ref, w_ref, o_ref):
        del q_ref
        o_ref[...] = w_ref[...].astype(BF16)

    return pl.pallas_call(
        body,
        name=name,
        grid_spec=pltpu.PrefetchScalarGridSpec(
            num_scalar_prefetch=1, grid=(ly, steps),
            in_specs=[pl.BlockSpec((None, tr, n), lambda l, i, q_ref: (l, i, 0))], out_specs=out_spec),
        out_shape=jax.ShapeDtypeStruct(shape, BF16),
        compiler_params=_params(("parallel", "parallel")),
    )(q, w)


def _half_block(ref, axis, layer, px, py, pc):
    blk = 2 * px + py
    if axis == 1:
        rows = ref.shape[1] // (2 * N_CHIPS)
        return ref.at[layer, pl.ds(pl.multiple_of((2 * blk + pc) * rows, 8), rows), :]
    rows, width = ref.shape[1] // 2, ref.shape[2] // N_CHIPS
    return ref.at[layer, pl.ds(pl.multiple_of(pc * rows, 8), rows), pl.ds(pl.multiple_of(blk * width, 128), width)]


def _gather_copy(refs, axes, pieces, send_sems, recv_sems, p, k, stage, whose):
    x, y, c = _place()
    chip = _other_chips(x, y)[k]
    i, layer = pieces[p]
    if stage == 0:
        origin = (x, y, c) if whose == "mine" else (*chip, c)
        to = (*chip, c)
    else:
        origin = (*chip, c) if whose == "mine" else (*chip, 1 - c)
        to = (x, y, 1 - c)
    blk = _half_block(refs[i], axes[i], layer, *origin)
    return pltpu.make_async_remote_copy(src_ref=blk, dst_ref=blk, send_sem=send_sems.at[p * 3 + k],
                                        recv_sem=recv_sems.at[p * 3 + k], device_id=to, device_id_type=MESH)


def _gather_weights(fulls, axes, pieces, name):
    n, m = len(fulls), 3 * len(pieces)

    def body(*refs):
        dst = refs[n:2 * n]
        sems = refs[2 * n:]
        todo = [(p, k) for p in range(len(pieces)) for k in range(3)]
        sends = [_gather_copy(dst, axes, pieces, sems[0], sems[1], p, k, 0, "mine") for p, k in todo]
        for cp in sends:
            cp.start()
        for p, k in todo:
            _gather_copy(dst, axes, pieces, sems[0], sems[1], p, k, 0, "theirs").wait_recv()
            sends.append(_gather_copy(dst, axes, pieces, sems[2], sems[3], p, k, 1, "mine"))
            sends[-1].start()
        for p, k in todo:
            _gather_copy(dst, axes, pieces, sems[2], sems[3], p, k, 1, "theirs").wait_recv()
        for cp in sends:
            cp.wait_send()

    return pl.pallas_call(
        body,
        name=name,
        in_specs=[ANY] * n,
        out_specs=[ANY] * n,
        out_shape=[jax.ShapeDtypeStruct(f.shape, f.dtype) for f in fulls],
        input_output_aliases={i: i for i in range(n)},
        scratch_shapes=[pltpu.SemaphoreType.DMA((m,))] * 4,
    )(*fulls)


HBM_SPEC = pl.BlockSpec(memory_space=pltpu.HBM)
SEM_SPEC = pl.BlockSpec(memory_space=pltpu.SEMAPHORE)


def _gather_start(fulls, axes, pieces, stage, name):
    n, m = len(fulls), 3 * len(pieces)

    def body(*refs):
        src = refs[:n]
        send_sems, recv_sems = refs[2 * n], refs[2 * n + 1]
        for p in range(len(pieces)):
            for k in range(3):
                _gather_copy(src, axes, pieces, send_sems, recv_sems, p, k, stage, "mine").start()

    out = pl.pallas_call(
        body,
        name=name,
        in_specs=[HBM_SPEC] * n,
        out_specs=[HBM_SPEC] * n + [SEM_SPEC, SEM_SPEC],
        out_shape=[pltpu.HBM(f.shape, f.dtype) for f in fulls] + [pltpu.SemaphoreType.DMA((m,)), pltpu.SemaphoreType.DMA((m,))],
        input_output_aliases={i: i for i in range(n)},
        compiler_params=pltpu.CompilerParams(has_side_effects=pltpu.SideEffectType.DATAFLOW_SIDE_EFFECTING),
    )(*[pltpu.with_memory_space_constraint(f, pltpu.HBM) for f in fulls])
    return list(out[:n]), out[n], out[n + 1]


def _gather_wait(fulls, send_sems, recv_sems, after, axes, pieces, stage, name):
    n = len(fulls)

    def body(*refs):
        src = refs[:n]
        s_sems, r_sems = refs[n], refs[n + 1]
        for p in range(len(pieces)):
            for k in range(3):
                _gather_copy(src, axes, pieces, s_sems, r_sems, p, k, stage, "mine").wait_send()
                _gather_copy(src, axes, pieces, s_sems, r_sems, p, k, stage, "theirs").wait_recv()

    out = pl.pallas_call(
        body,
        name=name,
        in_specs=[HBM_SPEC] * n + [SEM_SPEC, SEM_SPEC, ANY],
        out_specs=[HBM_SPEC] * n,
        out_shape=[pltpu.HBM(f.shape, f.dtype) for f in fulls],
        input_output_aliases={i: i for i in range(n)},
        compiler_params=pltpu.CompilerParams(has_side_effects=pltpu.SideEffectType.DATAFLOW_SIDE_EFFECTING),
    )(*fulls, send_sems, recv_sems, after)
    return list(out)


def _half_shape(kind, shape):
    if kind == "col":
        return (shape[0] // 2, shape[1])
    return (N_CHIPS, shape[1] // 2, shape[2])


def _half_of(kind, ref, h):
    if kind == "col":
        r = ref.shape[0] // 2
        return ref.at[pl.ds(pl.multiple_of(h * r, 8), r), :]
    r = ref.shape[1] // 2
    return ref.at[:, pl.ds(pl.multiple_of(h * r, 8), r), :]


def _pair_copy(src, dst, kinds, send_sems, recv_sems, i):
    x, y, c = _place()
    return pltpu.make_async_remote_copy(src_ref=_half_of(kinds[i], src[i], 1 - c), dst_ref=dst[i], send_sem=send_sems.at[i],
                                        recv_sem=recv_sems.at[i], device_id=(x, y, 1 - c), device_id_type=MESH)


def _pair_start(grads, kinds, token, name):
    n = len(grads)
    lands = [pltpu.HBM(_half_shape(kd, g.shape), g.dtype) for g, kd in zip(grads, kinds)]

    def body(*refs):
        src, dst = refs[:n], refs[2 * n + 2:3 * n + 2]
        send_sems, recv_sems = refs[3 * n + 2], refs[3 * n + 3]
        for i in range(n):
            _pair_copy(src, dst, kinds, send_sems, recv_sems, i).start()

    arrays = list(grads) + [token]
    out = pl.pallas_call(
        body,
        name=name,
        in_specs=[HBM_SPEC] * (n + 1),
        out_specs=[HBM_SPEC] * (2 * n + 1) + [SEM_SPEC, SEM_SPEC],
        out_shape=[pltpu.HBM(a.shape, a.dtype) for a in arrays] + lands + [pltpu.SemaphoreType.DMA((n,)), pltpu.SemaphoreType.DMA((n,))],
        input_output_aliases={i: i for i in range(n + 1)},
        compiler_params=pltpu.CompilerParams(has_side_effects=pltpu.SideEffectType.DATAFLOW_SIDE_EFFECTING),
    )(*[pltpu.with_memory_space_constraint(a, pltpu.HBM) for a in arrays])
    return list(out[:n]), out[n], list(out[n + 1:2 * n + 1]), out[2 * n + 1], out[2 * n + 2]


def _pair_wait(grads, kinds, token, lands, send_sems, recv_sems, name):
    n = len(grads)

    def body(*refs):
        src, dst = refs[:n], refs[n + 1:2 * n + 1]
        s_sems, r_sems = refs[2 * n + 1], refs[2 * n + 2]
        for i in range(n):
            cp = _pair_copy(src, dst, kinds, s_sems, r_sems, i)
            cp.wait_send()
            cp.wait_recv()

    arrays = list(grads) + [token] + list(lands)
    out = pl.pallas_call(
        body,
        name=name,
        in_specs=[HBM_SPEC] * (2 * n + 1) + [SEM_SPEC, SEM_SPEC],
        out_specs=[HBM_SPEC] * (2 * n + 1),
        out_shape=[pltpu.HBM(a.shape, a.dtype) for a in arrays],
        input_output_aliases={i: i for i in range(2 * n + 1)},
        compiler_params=pltpu.CompilerParams(has_side_effects=pltpu.SideEffectType.DATAFLOW_SIDE_EFFECTING),
    )(*arrays, send_sems, recv_sems)
    return list(out[:n]), out[n], list(out[n + 1:])


def _pair_sum(grad, recv, kind, c, own, layer, name):
    r, cols = own.shape[2:]
    if kind == "col":
        tr = _pick(r, (256, 128, 64, 32, 16, 8))
        steps = r // tr
        grid = (N_CHIPS, steps)
        g_spec = pl.BlockSpec((tr, cols), lambda s, i, c_ref: (c_ref[0] * steps + i, s))
        r_spec = pl.BlockSpec((tr, cols), lambda s, i, c_ref: (i, s))
        o_spec = pl.BlockSpec((None, None, tr, cols), lambda s, i, c_ref: (layer, s, i, 0))
        g_in = grad
    else:
        grid = (N_CHIPS, 1)
        g_spec = pl.BlockSpec((None, None, r, cols), lambda s, i, c_ref: (s, c_ref[0], 0, 0))
        r_spec = pl.BlockSpec((None, r, cols), lambda s, i, c_ref: (s, 0, 0))
        o_spec = pl.BlockSpec((None, None, r, cols), lambda s, i, c_ref: (layer, s, 0, 0))
        g_in = grad.reshape(N_CHIPS, 2, r, cols)

    def body(c_ref, g_ref, r_ref, own_ref, o_ref):
        del c_ref, own_ref
        o_ref[...] = (g_ref[...].astype(F32) + r_ref[...].astype(F32)).astype(o_ref.dtype)

    return pl.pallas_call(
        body,
        name=name,
        grid_spec=pltpu.PrefetchScalarGridSpec(num_scalar_prefetch=1, grid=grid, in_specs=[g_spec, r_spec, ANY], out_specs=o_spec),
        out_shape=jax.ShapeDtypeStruct(own.shape, own.dtype),
        input_output_aliases={3: 0},
        compiler_params=_params(("parallel", "parallel")),
    )(c, g_in, recv, own)


def _scatter_copy(own, mine, sib, pieces, send_sems, recv_sems, p, k, stage, whose):
    x, y, c = _place()
    q = 2 * x + y
    i, layer = pieces[p]
    per = 4 if stage == 0 else 3
    if k == 3:
        src, dst, to = own[i].at[layer, q], sib[i].at[layer, q], (x, y, 1 - c)
    else:
        chip = _other_chips(x, y)[k]
        slot = 2 * chip[0] + chip[1]
        if stage == 0:
            to = (*chip, c)
            src, dst = (own[i].at[layer, slot], mine[i].at[layer, q]) if whose == "mine" else (own[i].at[layer, q], mine[i].at[layer, slot])
        else:
            to = (x, y, 1 - c)
            src, dst = mine[i].at[layer, slot], sib[i].at[layer, slot]
    return pltpu.make_async_remote_copy(src_ref=src, dst_ref=dst, send_sem=send_sems.at[p * per + k],
                                        recv_sem=recv_sems.at[p * per + k], device_id=to, device_id_type=MESH)


def _scatter_start(own, mine, sib, token, pieces, stage, name):
    n = len(own)
    per = 4 if stage == 0 else 3
    m = per * len(pieces)
    n_arr = 3 * n + 1

    def body(*refs):
        o, mi, si = refs[:n], refs[n:2 * n], refs[2 * n:3 * n]
        send_sems, recv_sems = refs[2 * n_arr], refs[2 * n_arr + 1]
        for p in range(len(pieces)):
            for k in range(per):
                _scatter_copy(o, mi, si, pieces, send_sems, recv_sems, p, k, stage, "mine").start()

    arrays = list(own) + list(mine) + list(sib) + [token]
    out = pl.pallas_call(
        body,
        name=name,
        in_specs=[HBM_SPEC] * n_arr,
        out_specs=[HBM_SPEC] * n_arr + [SEM_SPEC, SEM_SPEC],
        out_shape=[pltpu.HBM(a.shape, a.dtype) for a in arrays] + [pltpu.SemaphoreType.DMA((m,)), pltpu.SemaphoreType.DMA((m,))],
        input_output_aliases={i: i for i in range(n_arr)},
        compiler_params=pltpu.CompilerParams(has_side_effects=pltpu.SideEffectType.DATAFLOW_SIDE_EFFECTING),
    )(*[pltpu.with_memory_space_constraint(a, pltpu.HBM) for a in arrays])
    return list(out[:n]), list(out[n:2 * n]), list(out[2 * n:3 * n]), out[3 * n], out[n_arr], out[n_arr + 1]


def _scatter_wait(own, mine, sib, token, send_sems, recv_sems, pieces, stage, name):
    n = len(own)
    per = 4 if stage == 0 else 3
    n_arr = 3 * n + 1

    def body(*refs):
        o, mi, si = refs[:n], refs[n:2 * n], refs[2 * n:3 * n]
        s_sems, r_sems = refs[n_arr], refs[n_arr + 1]
        for p in range(len(pieces)):
            for k in range(per):
                _scatter_copy(o, mi, si, pieces, s_sems, r_sems, p, k, stage, "mine").wait_send()
                _scatter_copy(o, mi, si, pieces, s_sems, r_sems, p, k, stage, "theirs").wait_recv()

    arrays = list(own) + list(mine) + list(sib) + [token]
    out = pl.pallas_call(
        body,
        name=name,
        in_specs=[HBM_SPEC] * n_arr + [SEM_SPEC, SEM_SPEC],
        out_specs=[HBM_SPEC] * n_arr,
        out_shape=[pltpu.HBM(a.shape, a.dtype) for a in arrays],
        input_output_aliases={i: i for i in range(n_arr)},
        compiler_params=pltpu.CompilerParams(has_side_effects=pltpu.SideEffectType.DATAFLOW_SIDE_EFFECTING),
    )(*arrays, send_sems, recv_sems)
    return list(out[:n]), list(out[n:2 * n]), list(out[2 * n:3 * n]), out[3 * n]


def _reduce_adamw(own, mine, sib, w, m, v, qc, name):
    ly, _, r, cols = mine.shape
    tr = _pick(r, (128, 64, 32, 16, 8))
    steps = r // tr
    c1 = 1.0 - ADAM_B1 ** ADAM_STEP
    c2 = 1.0 - ADAM_B2 ** ADAM_STEP

    def body(qc_ref, own_ref, mine_ref, sib_ref, w_ref, m_ref, v_ref, g_ref, d_ref, nm_ref, nv_ref):
        q = qc_ref[0]
        mine_sum = sib_sum = None
        for s in range(N_CHIPS):
            a = jnp.where(q == s, own_ref[...], mine_ref[s]).astype(F32)
            b = sib_ref[s].astype(F32)
            mine_sum = a if s == 0 else mine_sum + a
            sib_sum = b if s == 0 else sib_sum + b
        gv = jnp.where(pl.program_id(1) == qc_ref[1], mine_sum, sib_sum)
        nm = ADAM_B1 * m_ref[...] + (1.0 - ADAM_B1) * gv
        nv = ADAM_B2 * v_ref[...] + (1.0 - ADAM_B2) * (gv * gv)
        g_ref[...] = gv
        d_ref[...] = -ADAM_LR * ((nm / c1) / (jnp.sqrt(nv / c2) + ADAM_EPS) + ADAM_WD * w_ref[...])
        nm_ref[...] = nm
        nv_ref[...] = nv

    own_spec = pl.BlockSpec((None, None, tr, cols), lambda l, h, i, qc_ref: (l, qc_ref[0], i, 0))
    slot_spec = pl.BlockSpec((None, N_CHIPS, tr, cols), lambda l, h, i, qc_ref: (l, 0, i, 0))
    spec = pl.BlockSpec((None, tr, cols), lambda l, h, i, qc_ref: (l, h * steps + i, 0))
    shp = jax.ShapeDtypeStruct(w.shape, F32)
    return pl.pallas_call(
        body,
        name=name,
        grid_spec=pltpu.PrefetchScalarGridSpec(
            num_scalar_prefetch=1, grid=(ly, N_CORES, steps),
            in_specs=[own_spec, slot_spec, slot_spec, spec, spec, spec], out_specs=[spec] * 4),
        out_shape=[shp] * 4,
        compiler_params=_params(("parallel", "parallel", "parallel")),
    )(qc, own, mine, sib, w, m, v)


def _allreduce_small(v, name):
    r, w = v.shape

    def body(v_ref, o_ref, buf_ref, send_sems, recv_sems):
        x, y, c = _place()
        me = 4 * x + 2 * y + c

        def peer(k):
            return x ^ (k >> 2), y ^ ((k >> 1) & 1), c ^ (k & 1)

        def remote(k, slot):
            return pltpu.make_async_remote_copy(
                src_ref=v_ref, dst_ref=buf_ref.at[slot], send_sem=send_sems.at[k - 1], recv_sem=recv_sems.at[k - 1],
                device_id=peer(k), device_id_type=MESH)

        sends = [remote(k, me) for k in range(1, N_DEV)]
        for cp in sends:
            cp.start()
        buf_ref[me] = v_ref[...]
        for k in range(1, N_DEV):
            px, py, pc = peer(k)
            remote(k, 4 * px + 2 * py + pc).wait_recv()
        for cp in sends:
            cp.wait_send()
        acc = buf_ref[0]
        for dev in range(1, N_DEV):
            acc = acc + buf_ref[dev]
        o_ref[...] = acc

    vm = pl.BlockSpec(memory_space=pltpu.VMEM)
    return pl.pallas_call(
        body,
        name=name,
        in_specs=[vm],
        out_specs=vm,
        out_shape=jax.ShapeDtypeStruct((r, w), F32),
        scratch_shapes=[pltpu.VMEM((N_DEV, r, w), F32), pltpu.SemaphoreType.DMA((N_DEV - 1,)), pltpu.SemaphoreType.DMA((N_DEV - 1,))],
        compiler_params=pltpu.CompilerParams(vmem_limit_bytes=VMEM_LIMIT_BYTES),
    )(v)


def _adamw(w, g, m, v, name):
    ly, r, c = w.shape
    tr = _pick(r, (256, 128, 64, 32, 16, 8))
    c1 = 1.0 - ADAM_B1 ** ADAM_STEP
    c2 = 1.0 - ADAM_B2 ** ADAM_STEP

    def body(w_ref, g_ref, m_ref, v_ref, d_ref, nm_ref, nv_ref):
        gv = g_ref[...]
        nm = ADAM_B1 * m_ref[...] + (1.0 - ADAM_B1) * gv
        nv = ADAM_B2 * v_ref[...] + (1.0 - ADAM_B2) * (gv * gv)
        d_ref[...] = -ADAM_LR * ((nm / c1) / (jnp.sqrt(nv / c2) + ADAM_EPS) + ADAM_WD * w_ref[...])
        nm_ref[...] = nm
        nv_ref[...] = nv

    spec = pl.BlockSpec((None, tr, c), lambda l, i: (l, i, 0))
    shp = jax.ShapeDtypeStruct((ly, r, c), F32)
    return pl.pallas_call(
        body,
        name=name,
        grid=(ly, r // tr),
        in_specs=[spec] * 4,
        out_specs=[spec] * 3,
        out_shape=[shp] * 3,
        compiler_params=_params(("parallel", "parallel")),
    )(w, g, m, v)


def _rope_tables(seq):
    pos = jnp.arange(seq, dtype=F32)
    inv_freq = 1.0 / (ROPE_THETA ** (jnp.arange(0, HEAD_DIM, 2, dtype=F32) / HEAD_DIM))
    ang = (pos[:, None] * inv_freq[None, :]).T
    cos, sin = jnp.cos(ang), jnp.sin(ang)
    return jnp.concatenate([cos, cos], axis=0), jnp.concatenate([-sin, sin], axis=0)


def _pack(vs, fill=0.0):
    p = jnp.concatenate([v.reshape(-1) for v in vs])
    size = -(-p.shape[0] // 8192) * 8192
    return jnp.pad(p, (0, size - p.shape[0]), constant_values=fill).reshape(-1, 1024)


def _unpack(p, like):
    p = p.reshape(-1)
    out, o = [], 0
    for v in like:
        n = int(math.prod(v.shape))
        out.append(p[o:o + n].reshape(v.shape))
        o += n
    return out


def kernel(x, norm_mix, norm_ffn, norm_final, conv_w_in, conv_w_conv, conv_w_out, attn_w_qkv, attn_b_qkv, attn_sinks, attn_w_o, attn_b_o, ffn_w_in, ffn_w_conv, ffn_w_down, loss_target, m_norm_mix, m_norm_ffn, m_norm_final, m_conv_w_in, m_conv_w_conv, m_conv_w_out, m_attn_w_qkv, m_attn_b_qkv, m_attn_sinks, m_attn_w_o, m_attn_b_o, m_ffn_w_in, m_ffn_w_conv, m_ffn_w_down, v_norm_mix, v_norm_ffn, v_norm_final, v_conv_w_in, v_conv_w_conv, v_conv_w_out, v_attn_w_qkv, v_attn_b_qkv, v_attn_sinks, v_attn_w_o, v_attn_b_o, v_ffn_w_in, v_ffn_w_conv, v_ffn_w_down):
    bsz, seq, d = x.shape
    t = bsz * seq
    depth = norm_mix.shape[0]
    n_conv, n_attn = conv_w_in.shape[0], attn_w_qkv.shape[0]
    xq, yq, cq = _place()
    q = 2 * xq + yq

    big = [conv_w_in, conv_w_out, attn_w_qkv, attn_w_o, ffn_w_in, ffn_w_down]
    axes = [2, 1, 2, 1, 2, 1]
    q_arr = q.astype(jnp.int32).reshape(1)
    c_arr = cq.astype(jnp.int32).reshape(1)
    weights = [_place_shard(w, ax, q_arr, f"place_shard{n}") for n, (w, ax) in enumerate(zip(big, axes))]

    def pieces_of(i):
        return [(0, i // 2), (1, i // 2), (4, i), (5, i)] if i % 2 == 0 else [(2, i // 2), (3, i // 2), (4, i), (5, i)]

    weights = _gather_weights(weights, axes, pieces_of(0)[:2], "gather_mixer0")

    small_cols = [conv_w_conv, attn_b_qkv, attn_b_o, ffn_w_conv]

    def placed(v):
        width = v.shape[-1]
        full = jnp.zeros(v.shape[:-1] + (N_CHIPS * width,), F32)
        return lax.dynamic_update_slice_in_dim(full, v * (1.0 / N_CORES), q * width, axis=v.ndim - 1)

    full_cols = [placed(v) for v in small_cols]
    wc_conv, b_qkv, b_o, wf_conv = _unpack(_allreduce_small(_pack(full_cols), "gather_small"), full_cols)
    cos_t, sin_t = _rope_tables(seq)

    xs = x.reshape(t, d)
    saved = []
    sems = {}

    def fetch(pieces, stage, tag):
        nonlocal weights
        weights, sems[tag, 0], sems[tag, 1] = _gather_start(weights, axes, pieces, stage, f"gather_{'ici' if stage == 0 else 'pass'}_start{tag}")

    def settle(pieces, stage, tag, after):
        nonlocal weights
        weights = _gather_wait(weights, sems[tag, 0], sems[tag, 1], after, axes, pieces, stage,
                               f"gather_{'ici' if stage == 0 else 'pass'}_wait{tag}")

    for i in range(depth):
        j = i // 2
        ahead = pieces_of(i + 1) if i + 1 < depth else None
        if i == 0:
            fetch(pieces_of(0)[2:], 0, "0")
        elif ahead:
            fetch(ahead, 0, str(i + 1))
        w_cin, w_cout, w_qkv, w_o, w_fin, w_fdown = weights
        h = _rms_fwd(xs, norm_mix[i:i + 1], f"norm_mix_fwd{i}")
        if i % 2 == 0:
            pre = _mm(h, w_cin, "nn", BF16, layer=j, tm=1024, tn=768, tk=4096, name=f"conv_in_fwd{i}")
            mixed = _convgate_fwd(pre, wc_conv[j], seq, f"conv_gate_fwd{i}")
            x_mid = _mm(mixed, w_cout, "nn", F32, layer=j, residual=xs, tm=512, tn=1024, tk=4096, name=f"conv_out_fwd{i}")
            lse = None
        else:
            pre = _mm(h, w_qkv, "nn", F32, layer=j, bias=b_qkv[j:j + 1], tm=1024, tn=768, tk=4096, name=f"qkv_fwd{i}")
            mixed, lse = _attn_fwd(pre, attn_sinks[j], cos_t, sin_t, bsz, seq, f"attn_fwd{i}", hp=1)
            x_mid = _mm(mixed, w_o, "nn", F32, layer=j, bias=b_o[j:j + 1], residual=xs, tm=512, tn=1024, tk=4096,
                        name=f"attn_out_fwd{i}")
        if i == 0:
            settle(pieces_of(0)[2:], 0, "0", x_mid)
            fetch(pieces_of(0)[2:], 1, "0")
        elif ahead:
            settle(ahead, 0, str(i + 1), x_mid)
            fetch(ahead, 1, str(i + 1))
        h2 = _rms_fwd(x_mid, norm_ffn[i:i + 1], f"norm_ffn_fwd{i}")
        if i == 0:
            settle(pieces_of(0)[2:], 1, "0", h2)
            fetch(ahead, 0, "1")
        w_cin, w_cout, w_qkv, w_o, w_fin, w_fdown = weights
        gu = _mm(h2, w_fin, "nn", BF16, layer=i, n_outer=True, tm=2048, tn=1408, tk=4096, name=f"ffn_in_fwd{i}")
        if i == 0:
            settle(ahead, 0, "1", gu)
            fetch(ahead, 1, "1")
            w_cin, w_cout, w_qkv, w_o, w_fin, w_fdown = weights
        x_next, act = _ffn_gate_down_fwd(gu, wf_conv[i], w_fdown, i, x_mid, seq, f"ffn_gate_down_fwd{i}")
        if ahead:
            settle(ahead, 1, str(i + 1), x_next)
        saved.append((xs, h, pre, mixed, lse, x_mid, h2, gu, act))
        xs = x_next
    w_cin, w_cout, w_qkv, w_o, w_fin, w_fdown = weights

    dx, dxb, sq, dg_final = _loss_head(xs, loss_target.reshape(t, d), norm_final.reshape(1, d), "loss_head")
    loss = lax.psum(0.5 * jnp.sum(sq) / d, ("x", "y", "c"))

    g_norm_mix, g_norm_ffn = [None] * depth, [None] * depth
    g_cin, g_cconv, g_cout = [None] * n_conv, [None] * n_conv, [None] * n_conv
    g_qkv, g_bqkv, g_sinks, g_o, g_bo = ([None] * n_attn for _ in range(5))
    g_fin, g_fconv, g_fdown = [None] * depth, [None] * depth, [None] * depth

    kinds6 = ["col", "row", "col", "row", "col", "row"]
    layers6 = [n_conv, n_conv, n_attn, n_attn, depth, depth]
    big_w = [conv_w_in, conv_w_out, attn_w_qkv, attn_w_o, ffn_w_in, ffn_w_down]

    def slot_stack(n):
        k, cols = big_w[n].shape[1], big_w[n].shape[2]
        r = k // 2
        return lax.empty((layers6[n], N_CHIPS, r, cols), BF16)

    own = [slot_stack(n) for n in range(6)]
    mine = [slot_stack(n) for n in range(6)]
    sib = [slot_stack(n) for n in range(6)]
    flight = {}

    def group(i, part):
        return f"{part}{i}", (pieces_of(i)[:2] if part == "m" else pieces_of(i)[2:])

    def scatter(grp, stage, action, token):
        tag, pieces = grp
        ts = [ti for ti, _ in pieces]
        local = [(n, l) for n, (_, l) in enumerate(pieces)]
        sub = ([own[ti] for ti in ts], [mine[ti] for ti in ts], [sib[ti] for ti in ts], token)
        label = f"grad_{'ici' if stage == 0 else 'pass'}_{action}_{tag}"
        if action == "start":
            o, mi, si, token, s_sems, r_sems = _scatter_start(*sub, local, stage, label)
            flight[tag] = (s_sems, r_sems)
        else:
            o, mi, si, token = _scatter_wait(*sub, *flight[tag], local, stage, label)
        for n, ti in enumerate(ts):
            own[ti], mine[ti], sib[ti] = o[n], mi[n], si[n]
        return token

    def pair_begin(grp, token):
        tag, pieces = grp
        grads = {0: g_cin, 1: g_cout, 2: g_qkv, 3: g_o, 4: g_fin, 5: g_fdown}
        parts, kinds = [], []
        for ti, l in pieces:
            g = grads[ti][l]
            parts.append(g if kinds6[ti] == "col" else g.reshape(N_CHIPS, g.shape[0] // N_CHIPS, g.shape[1]))
            kinds.append(kinds6[ti])
        parts, token, lands, s_sems, r_sems = _pair_start(parts, kinds, token, f"grad_pair_start_{tag}")
        flight["pair" + tag] = (parts, kinds, lands, s_sems, r_sems)
        return token

    def pair_finish(grp, token):
        tag, pieces = grp
        parts, kinds, lands, s_sems, r_sems = flight["pair" + tag]
        parts, token, recv = _pair_wait(parts, kinds, token, lands, s_sems, r_sems, f"grad_pair_wait_{tag}")
        for (ti, l), g, r in zip(pieces, parts, recv):
            own[ti] = _pair_sum(g, r, kinds6[ti], c_arr, own[ti], l, f"grad_pair_sum_{tag}_{ti}")
        return scatter(grp, 0, "start", token)

    for i in reversed(range(depth)):
        j = i // 2
        x_in, h, pre, mixed, lse, x_mid, h2, gu, act = saved[i]
        da = _mm(dxb, w_fdown, "nt", BF16, layer=i, n_outer=True, tm=2048, tn=1408, tk=4096, name=f"ffn_down_dx{i}")
        g_fdown[i] = _mm(act, dxb, "tn", BF16, tm=1408, tn=1024, tk=2048, name=f"ffn_down_dw{i}")
        dgu, dwc = _ffngate_bwd(gu, da, wf_conv[i], seq, f"ffn_gate_bwd{i}")
        g_fconv[i] = dwc[:3]
        g_fin[i] = _mm(h2, dgu, "tn", BF16, tm=1024, tn=1408, tk=2048, name=f"ffn_in_dw{i}")
        dgu = pair_begin(group(i, "f"), dgu)
        dx, dxb, dg, colsum = _rms_bwd(x_mid, dgu, w_fin, i, norm_ffn[i:i + 1], dx, f"ffn_in_dx_norm_bwd{i}")
        g_norm_ffn[i] = jnp.sum(dg, axis=0)
        if i + 1 < depth:
            dxb = scatter(group(i + 1, "m"), 1, "start", scatter(group(i + 1, "m"), 0, "wait", dxb))
            dxb = scatter(group(i + 1, "f"), 1, "wait", dxb)
        dxb = pair_finish(group(i, "f"), dxb)
        if i % 2 == 0:
            dmix = _mm(dxb, w_cout, "nt", BF16, layer=j, tm=512, tn=1024, tk=4096, name=f"conv_out_dx{i}")
            g_cout[j] = _mm(mixed, dxb, "tn", BF16, tm=1024, tn=1024, tk=2048, name=f"conv_out_dw{i}")
            dpre, dwc = _convgate_bwd(pre, dmix, wc_conv[j], seq, f"conv_gate_bwd{i}")
            g_cconv[j] = dwc[:3]
            g_cin[j] = _mm(h, dpre, "tn", BF16, tm=1024, tn=1536, tk=2048, name=f"conv_in_dw{i}")
            w_pre = w_cin
        else:
            g_bo[j] = jnp.sum(colsum, axis=0)
            dmix = _mm(dxb, w_o, "nt", F32, layer=j, tm=512, tn=1024, tk=4096, name=f"attn_out_dx{i}")
            g_o[j] = _mm(mixed, dxb, "tn", BF16, tm=1024, tn=1024, tk=2048, name=f"attn_out_dw{i}")
            dpre, dbias, dsk = _attn_bwd(pre, mixed, lse, dmix, attn_sinks[j], cos_t, sin_t, bsz, seq, f"attn_bwd{i}",
                                         hp=GROUP)
            g_bqkv[j] = jnp.sum(dbias, axis=0)
            g_sinks[j] = jnp.sum(dsk, axis=1)
            g_qkv[j] = _mm(h, dpre, "tn", BF16, tm=1024, tn=1536, tk=2048, name=f"qkv_dw{i}")
            w_pre = w_qkv
        dpre = pair_begin(group(i, "m"), dpre)
        dx, dxb, dg, _ = _rms_bwd(x_in, dpre, w_pre, j, norm_mix[i:i + 1], dx, f"mixer_in_dx_norm_bwd{i}")
        g_norm_mix[i] = jnp.sum(dg, axis=0)
        dxb = scatter(group(i, "f"), 1, "start", scatter(group(i, "f"), 0, "wait", dxb))
        if i + 1 < depth:
            dxb = scatter(group(i + 1, "m"), 1, "wait", dxb)
        dxb = pair_finish(group(i, "m"), dxb)
    grad_x = dx.reshape(bsz, seq, d)

    dxb = scatter(group(0, "m"), 1, "start", scatter(group(0, "m"), 0, "wait", dxb))
    scatter(group(0, "m"), 1, "wait", scatter(group(0, "f"), 1, "wait", dxb))
    big_m = [m_conv_w_in, m_conv_w_out, m_attn_w_qkv, m_attn_w_o, m_ffn_w_in, m_ffn_w_down]
    big_v = [v_conv_w_in, v_conv_w_out, v_attn_w_qkv, v_attn_w_o, v_ffn_w_in, v_ffn_w_down]
    big_names = ["conv_w_in", "conv_w_out", "attn_w_qkv", "attn_w_o", "ffn_w_in", "ffn_w_down"]
    qc_arr = jnp.stack([q, cq]).astype(jnp.int32)
    big_upd = [_reduce_adamw(own[n], mine[n], sib[n], big_w[n], big_m[n], big_v[n], qc_arr, f"adamw_{nm}")
               for n, nm in enumerate(big_names)]

    small = [jnp.stack(g_norm_mix), jnp.stack(g_norm_ffn), jnp.sum(dg_final, axis=0), jnp.stack(g_cconv),
             jnp.stack(g_bqkv), jnp.stack(g_sinks), jnp.stack(g_bo), jnp.stack(g_fconv)]
    sg = _unpack(_allreduce_small(_pack(small), "grad_small_allreduce"), small)

    def my_cols(v, like):
        width = like.shape[-1]
        return lax.dynamic_slice_in_dim(v, q * width, width, axis=v.ndim - 1)

    small_w = [norm_mix, norm_ffn, norm_final, conv_w_conv, attn_b_qkv, attn_sinks, attn_b_o, ffn_w_conv]
    small_m = [m_norm_mix, m_norm_ffn, m_norm_final, m_conv_w_conv, m_attn_b_qkv, m_attn_sinks, m_attn_b_o, m_ffn_w_conv]
    small_v = [v_norm_mix, v_norm_ffn, v_norm_final, v_conv_w_conv, v_attn_b_qkv, v_attn_sinks, v_attn_b_o, v_ffn_w_conv]
    small_g = [sg[0], sg[1], sg[2], my_cols(sg[3], conv_w_conv), my_cols(sg[4], attn_b_qkv), sg[5],
               my_cols(sg[6], attn_b_o), my_cols(sg[7], ffn_w_conv)]

    upd = {nm: tuple(u[1:]) for nm, u in zip(big_names, big_upd)}
    sd, sm, sv = _adamw(_pack(small_w)[None], _pack(small_g)[None], _pack(small_m)[None], _pack(small_v, 1.0)[None],
                        "adamw_small")
    sd, sm, sv = _unpack(sd, small_w), _unpack(sm, small_w), _unpack(sv, small_w)
    names = ["norm_mix", "norm_ffn", "norm_final", "conv_w_in", "conv_w_conv", "conv_w_out", "attn_w_qkv", "attn_b_qkv",
             "attn_sinks", "attn_w_o", "attn_b_o", "ffn_w_in", "ffn_w_conv", "ffn_w_down"]
    small_names = ["norm_mix", "norm_ffn", "norm_final", "conv_w_conv", "attn_b_qkv", "attn_sinks", "attn_b_o", "ffn_w_conv"]
    grads = dict(zip(small_names, small_g))
    grads.update({nm: u[0] for nm, u in zip(big_names, big_upd)})
    for n, nm in enumerate(small_names):
        upd[nm] = (sd[n], sm[n], sv[n])
    return (loss, grad_x, *[grads[nm] for nm in names], *[upd[nm][0] for nm in names],
            *[upd[nm][1] for nm in names], *[upd[nm][2] for nm in names])
```

```python
import math

import jax
import jax.numpy as jnp
from jax import lax
from jax.experimental import pallas as pl
from jax.experimental.pallas import tpu as pltpu

F32 = jnp.float32
BF16 = jnp.bfloat16

HEAD_DIM = 64
GROUP = 4
WINDOW = 128
EPS = 1e-5
ROPE_THETA = 10000.0
ADAM_LR, ADAM_B1, ADAM_B2, ADAM_EPS, ADAM_WD, ADAM_STEP = 0.001, 0.9, 0.999, 1e-08, 0.01, 10

N_CHIPS = 4
N_CORES = 2
N_DEV = 8
HALO = 16
VMEM_LIMIT_BYTES = 56 * 1024 * 1024
MESH = pl.DeviceIdType.MESH
ANY = pl.BlockSpec(memory_space=pl.ANY)
SMEM = pl.BlockSpec(memory_space=pltpu.SMEM)
NEG = float(jnp.finfo(jnp.float32).min)
ROW_TILES = (512, 256, 128, 64, 32, 16, 8)


def _pick(dim, cands):
    for c in cands:
        if dim % c == 0:
            return c
    return dim


def _params(sem):
    return pltpu.CompilerParams(dimension_semantics=sem, vmem_limit_bytes=VMEM_LIMIT_BYTES)


_DIMS = {"nn": (((1,), (0,)), ((), ())), "nt": (((1,), (1,)), ((), ())), "tn": (((0,), (0,)), ((), ()))}


def _mm(a, b, mode, out_dtype, *, layer=None, bias=None, residual=None, norm_g=None, n_outer=False, tm, tn, tk, name):
    b2 = b.shape[1:] if layer is not None else b.shape
    if mode == "nn":
        (m, k), n = a.shape, b2[1]
    elif mode == "nt":
        (m, k), n = a.shape, b2[0]
    else:
        (k, m), n = a.shape, b2[1]
    tm, tn, tk = min(tm, m), min(tn, n), min(tk, k)
    assert m % tm == 0 and n % tn == 0 and k % tk == 0, (name, a.shape, b.shape, tm, tn, tk)
    nk = k // tk

    def at(f):
        return (lambda p0, p1, p2: f(p1, p0, p2)) if n_outer else f

    a_spec = pl.BlockSpec((tk, tm), at(lambda i, j, l: (l, i))) if mode == "tn" else pl.BlockSpec((tm, tk), at(lambda i, j, l: (i, l)))
    if layer is None:
        b_spec = (pl.BlockSpec((tn, tk), at(lambda i, j, l: (j, l))) if mode == "nt"
                  else pl.BlockSpec((tk, tn), at(lambda i, j, l: (l, j))))
    elif mode == "nt":
        b_spec = pl.BlockSpec((None, tn, tk), at(lambda i, j, l: (layer, j, l)))
    else:
        b_spec = pl.BlockSpec((None, tk, tn), at(lambda i, j, l: (layer, l, j)))
    in_specs, args = [a_spec, b_spec], [a, b]
    if bias is not None:
        in_specs.append(pl.BlockSpec((1, tn), at(lambda i, j, l: (0, j))))
        args.append(bias)
    if residual is not None:
        in_specs.append(pl.BlockSpec((tm, tn), at(lambda i, j, l: (i, j))))
        args.append(residual)
    if norm_g is not None:
        assert tn == n, (name, "the RMSNorm of the result needs whole rows in a tile")
        in_specs.append(pl.BlockSpec((1, tn), at(lambda i, j, l: (0, j))))
        args.append(norm_g)
    has_bias, has_res, has_norm = bias is not None, residual is not None, norm_g is not None

    def body(*refs):
        a_ref, b_ref = refs[0], refs[1]
        pos = 2
        bias_ref = res_ref = g_ref = h_ref = None
        if has_bias:
            bias_ref, pos = refs[pos], pos + 1
        if has_res:
            res_ref, pos = refs[pos], pos + 1
        if has_norm:
            g_ref, pos = refs[pos], pos + 1
        o_ref, pos = refs[pos], pos + 1
        if has_norm:
            h_ref, pos = refs[pos], pos + 1
        acc_ref = refs[pos] if nk > 1 else None

        def finish(acc):
            if has_bias:
                acc = acc + bias_ref[...]
            if has_res:
                acc = acc + res_ref[...]
            o_ref[...] = acc.astype(o_ref.dtype)
            if has_norm:
                h_ref[...] = _rms(acc, g_ref[...]).astype(BF16)

        if nk == 1:
            finish(lax.dot_general(a_ref[...], b_ref[...], _DIMS[mode], preferred_element_type=F32))
            return
        l = pl.program_id(2)
        part = lax.dot_general(a_ref[...], b_ref[...], _DIMS[mode], preferred_element_type=F32)

        @pl.when(l == 0)
        def _():
            acc_ref[...] = part

        @pl.when(l > 0)
        def _():
            acc_ref[...] += part

        @pl.when(l == nk - 1)
        def _():
            finish(acc_ref[...])

    o_spec = pl.BlockSpec((tm, tn), at(lambda i, j, l: (i, j)))
    o_shape = jax.ShapeDtypeStruct((m, n), out_dtype)
    return pl.pallas_call(
        body,
        name=name,
        grid=(n // tn, m // tm, nk) if n_outer else (m // tm, n // tn, nk),
        in_specs=in_specs,
        out_specs=[o_spec, o_spec] if has_norm else o_spec,
        out_shape=[o_shape, jax.ShapeDtypeStruct((m, n), BF16)] if has_norm else o_shape,
        scratch_shapes=[pltpu.VMEM((tm, tn), F32)] if nk > 1 else [],
        compiler_params=_params(("parallel", "parallel", "arbitrary")),
    )(*args)


def _rms(x, g):
    return x * lax.rsqrt(jnp.mean(x * x, axis=-1, keepdims=True) + EPS) * g


def _fold8(v):
    r, d = v.shape
    return jnp.sum(v.reshape(r // 8, 8, d), axis=0)


def _rms_fwd(x, g, name):
    t, d = x.shape
    tm = _pick(t, ROW_TILES)

    def body(x_ref, g_ref, h_ref):
        xv = x_ref[...]
        r = lax.rsqrt(jnp.mean(xv * xv, axis=-1, keepdims=True) + EPS)
        h_ref[...] = (xv * r * g_ref[...]).astype(BF16)

    return pl.pallas_call(
        body,
        name=name,
        grid=(t // tm,),
        in_specs=[pl.BlockSpec((tm, d), lambda i: (i, 0)), pl.BlockSpec((1, d), lambda i: (0, 0))],
        out_specs=pl.BlockSpec((tm, d), lambda i: (i, 0)),
        out_shape=jax.ShapeDtypeStruct((t, d), BF16),
        compiler_params=_params(("parallel",)),
    )(x, g)


def _rms_bwd(x, dpre, w, layer, g, dx_in, name):
    t, d = x.shape
    k = dpre.shape[1]
    tm = _pick(t, ROW_TILES)

    def body(x_ref, dp_ref, w_ref, g_ref, dxi_ref, dx_ref, dxb_ref, dg_ref, cs_ref):
        i = pl.program_id(0)
        xv = x_ref[...]
        r = lax.rsqrt(jnp.mean(xv * xv, axis=-1, keepdims=True) + EPS)
        xhat = xv * r
        dy = lax.dot_general(dp_ref[...], w_ref[...], _DIMS["nt"], preferred_element_type=F32)
        gdy = dy * g_ref[...]
        dx = dxi_ref[...] + r * (gdy - xhat * jnp.mean(gdy * xhat, axis=-1, keepdims=True))
        dx_ref[...] = dx
        dxb_ref[...] = dx.astype(BF16)

        @pl.when(i == 0)
        def _():
            dg_ref[...] = jnp.zeros_like(dg_ref)
            cs_ref[...] = jnp.zeros_like(cs_ref)

        dg_ref[...] += _fold8(dy * xhat)
        cs_ref[...] += _fold8(dx)

    row = pl.BlockSpec((tm, d), lambda i: (i, 0))
    acc = pl.BlockSpec((8, d), lambda i: (0, 0))
    w_spec = pl.BlockSpec((None, d, k), lambda i: (layer, 0, 0), pipeline_mode=pl.Buffered(1))
    return pl.pallas_call(
        body,
        name=name,
        grid=(t // tm,),
        in_specs=[row, pl.BlockSpec((tm, k), lambda i: (i, 0)), w_spec, pl.BlockSpec((1, d), lambda i: (0, 0)), row],
        out_specs=[row, row, acc, acc],
        out_shape=[jax.ShapeDtypeStruct((t, d), F32), jax.ShapeDtypeStruct((t, d), BF16),
                   jax.ShapeDtypeStruct((8, d), F32), jax.ShapeDtypeStruct((8, d), F32)],
        compiler_params=_params(("arbitrary",)),
    )(x, dpre, w, g, dx_in)


def _loss_head(x, target, g, name):
    t, d = x.shape
    tm = _pick(t, ROW_TILES)
    inv_d = 1.0 / d

    def body(x_ref, t_ref, g_ref, dx_ref, dxb_ref, sq_ref, dg_ref):
        i = pl.program_id(0)
        xv = x_ref[...]
        gv = g_ref[...]
        r = lax.rsqrt(jnp.mean(xv * xv, axis=-1, keepdims=True) + EPS)
        xhat = xv * r
        err = xhat * gv - t_ref[...]
        dy = err * inv_d
        gdy = dy * gv
        dx = r * (gdy - xhat * jnp.mean(gdy * xhat, axis=-1, keepdims=True))
        dx_ref[...] = dx
        dxb_ref[...] = dx.astype(BF16)

        @pl.when(i == 0)
        def _():
            sq_ref[...] = jnp.zeros_like(sq_ref)
            dg_ref[...] = jnp.zeros_like(dg_ref)

        sq_ref[...] += _fold8(err * err)
        dg_ref[...] += _fold8(dy * xhat)

    row = pl.BlockSpec((tm, d), lambda i: (i, 0))
    acc = pl.BlockSpec((8, d), lambda i: (0, 0))
    return pl.pallas_call(
        body,
        name=name,
        grid=(t // tm,),
        in_specs=[row, row, pl.BlockSpec((1, d), lambda i: (0, 0))],
        out_specs=[row, row, acc, acc],
        out_shape=[jax.ShapeDtypeStruct((t, d), F32), jax.ShapeDtypeStruct((t, d), BF16),
                   jax.ShapeDtypeStruct((8, d), F32), jax.ShapeDtypeStruct((8, d), F32)],
        compiler_params=_params(("arbitrary",)),
    )(x, target, g)


def _rows(tm):
    return lax.broadcasted_iota(jnp.int32, (tm, 1), 0)


def _shift_down(u, before2):
    r8 = _rows(8)
    s1, s2 = pltpu.roll(u, 1, 0), pltpu.roll(u, 2, 0)
    top1 = jnp.where(r8 == 0, before2[1:2], s1[:8])
    top2 = jnp.where(r8 == 0, before2[0:1], jnp.where(r8 == 1, before2[1:2], s2[:8]))
    return jnp.concatenate([top1, s1[8:]], axis=0), jnp.concatenate([top2, s2[8:]], axis=0)


def _shift_up(u, after2):
    tm = u.shape[0]
    r8 = _rows(8)
    s1, s2 = pltpu.roll(u, tm - 1, 0), pltpu.roll(u, tm - 2, 0)
    bot1 = jnp.where(r8 == 7, after2[0:1], s1[tm - 8:])
    bot2 = jnp.where(r8 == 6, after2[0:1], jnp.where(r8 == 7, after2[1:2], s2[tm - 8:]))
    return jnp.concatenate([s1[:tm - 8], bot1], axis=0), jnp.concatenate([s2[:tm - 8], bot2], axis=0)


def _shift_matrix(tm, up):
    r = lax.broadcasted_iota(jnp.int32, (2 * tm, tm), 0)
    c = lax.broadcasted_iota(jnp.int32, (2 * tm, tm), 1)
    t = jnp.where(r >= tm, r - tm, r)
    k = jnp.where(r >= tm, 2, 1)
    return (c == (t + k if up else t - k)).astype(BF16)


def _shift_down_mxu(u, before2):
    tm = u.shape[0]
    moved = jnp.dot(_shift_matrix(tm, False), u.astype(BF16), preferred_element_type=F32)
    r8 = _rows(8)
    s1, s2 = moved[:tm], moved[tm:]
    top1 = s1[:8] + jnp.where(r8 == 0, before2[1:2], 0.0)
    top2 = s2[:8] + jnp.where(r8 == 0, before2[0:1], jnp.where(r8 == 1, before2[1:2], 0.0))
    return jnp.concatenate([top1, s1[8:]], axis=0), jnp.concatenate([top2, s2[8:]], axis=0)


def _shift_up_mxu(u, after2):
    tm = u.shape[0]
    moved = jnp.dot(_shift_matrix(tm, True), u.astype(BF16), preferred_element_type=F32)
    r8 = _rows(8)
    s1, s2 = moved[:tm], moved[tm:]
    bot1 = s1[tm - 8:] + jnp.where(r8 == 7, after2[0:1], 0.0)
    bot2 = s2[tm - 8:] + jnp.where(r8 == 6, after2[0:1], jnp.where(r8 == 7, after2[1:2], 0.0))
    return jnp.concatenate([s1[:tm - 8], bot1], axis=0), jnp.concatenate([s2[:tm - 8], bot2], axis=0)


def _conv_tile(seq):
    return _pick(seq, (256, 128, 64, 32, 16, 8))


def _halo_specs(tm, width, n_tiles):
    per = tm // HALO
    before = pl.BlockSpec((HALO, width), lambda i: (jnp.maximum(i * per - 1, 0), 0))
    after = pl.BlockSpec((HALO, width), lambda i: (jnp.minimum((i + 1) * per, n_tiles * per - 1), 0))
    return before, after


def _convgate_fwd(bcv, w, seq, name):
    t, d3 = bcv.shape
    d = d3 // 3
    tm = _conv_tile(seq)
    tps = seq // tm
    before, _ = _halo_specs(tm, d3, t // tm)

    def body(x_ref, xb_ref, w_ref, y_ref):
        i = pl.program_id(0)
        inner = (i % tps != 0).astype(F32)
        u = x_ref[:, d:2 * d].astype(F32) * x_ref[:, 2 * d:].astype(F32)
        xb = xb_ref[:, d:].astype(F32)[HALO - 2:]
        s1, s2 = _shift_down(u, xb[:, :d] * xb[:, d:] * inner)
        z = w_ref[2:3] * u + w_ref[1:2] * s1 + w_ref[0:1] * s2
        y_ref[...] = (x_ref[:, :d].astype(F32) * z).astype(BF16)

    return pl.pallas_call(
        body,
        name=name,
        grid=(t // tm,),
        in_specs=[pl.BlockSpec((tm, d3), lambda i: (i, 0)), before, pl.BlockSpec((3, d), lambda i: (0, 0))],
        out_specs=pl.BlockSpec((tm, d), lambda i: (i, 0)),
        out_shape=jax.ShapeDtypeStruct((t, d), BF16),
        compiler_params=_params(("parallel",)),
    )(bcv, bcv, w)


def _convgate_bwd(bcv, dy, w, seq, name):
    t, d3 = bcv.shape
    d = d3 // 3
    tm = _conv_tile(seq)
    tps = seq // tm
    before, after = _halo_specs(tm, d3, t // tm)
    _, after_dy = _halo_specs(tm, d, t // tm)

    def body(x_ref, xb_ref, xa_ref, dy_ref, dya_ref, w_ref, dx_ref, dw_ref):
        i = pl.program_id(0)
        inner_lo = (i % tps != 0).astype(F32)
        inner_hi = (i % tps != tps - 1).astype(F32)
        w0, w1, w2 = w_ref[0:1], w_ref[1:2], w_ref[2:3]
        b, c, v = x_ref[:, :d].astype(F32), x_ref[:, d:2 * d].astype(F32), x_ref[:, 2 * d:].astype(F32)
        u = c * v
        xb = xb_ref[:, d:].astype(F32)[HALO - 2:]
        s1, s2 = _shift_down(u, xb[:, :d] * xb[:, d:] * inner_lo)
        z = w2 * u + w1 * s1 + w0 * s2
        dyv = dy_ref[...].astype(F32)
        dz = dyv * b
        dza = dya_ref[...].astype(F32)[0:2] * xa_ref[:, :d].astype(F32)[0:2] * inner_hi
        n1, n2 = _shift_up(dz, dza)
        du = w2 * dz + w1 * n1 + w0 * n2
        dx_ref[:, :d] = (dyv * z).astype(BF16)
        dx_ref[:, d:2 * d] = (du * v).astype(BF16)
        dx_ref[:, 2 * d:] = (du * c).astype(BF16)

        @pl.when(i == 0)
        def _():
            dw_ref[...] = jnp.zeros_like(dw_ref)

        dw_ref[0:1] += jnp.sum(dz * s2, axis=0, keepdims=True)
        dw_ref[1:2] += jnp.sum(dz * s1, axis=0, keepdims=True)
        dw_ref[2:3] += jnp.sum(dz * u, axis=0, keepdims=True)

    return pl.pallas_call(
        body,
        name=name,
        grid=(t // tm,),
        in_specs=[pl.BlockSpec((tm, d3), lambda i: (i, 0)), before, after,
                  pl.BlockSpec((tm, d), lambda i: (i, 0)), after_dy, pl.BlockSpec((3, d), lambda i: (0, 0))],
        out_specs=[pl.BlockSpec((tm, d3), lambda i: (i, 0)), pl.BlockSpec((8, d), lambda i: (0, 0))],
        out_shape=[jax.ShapeDtypeStruct((t, d3), BF16), jax.ShapeDtypeStruct((8, d), F32)],
        compiler_params=_params(("arbitrary",)),
    )(bcv, bcv, bcv, dy, dy, w)


def _sigmoid(x):
    return 1.0 / (1.0 + jnp.exp(-x))


def _ffn_gate_down_fwd(gu, w, w_down, layer, resid, norm_g, seq, name):
    t, f2 = gu.shape
    f = f2 // 2
    d = w_down.shape[2]
    sub = _conv_tile(seq)
    tm = _pick(seq, (2 * sub, sub))
    tps = seq // tm
    before, _ = _halo_specs(tm, f2, t // tm)

    def body(x_ref, xb_ref, w_ref, wd_ref, res_ref, g_ref, o_ref, a_ref, h_ref):
        i = pl.program_id(0)
        inner = (i % tps != 0).astype(F32)
        for r0 in range(0, tm, sub):
            rows = slice(r0, r0 + sub)
            if r0 == 0:
                halo = xb_ref[:, :f].astype(F32)[HALO - 2:] * inner
            else:
                halo = x_ref[r0 - HALO:r0, :f].astype(F32)[HALO - 2:]
            s1, s2 = _shift_down_mxu(x_ref[rows, :f], halo)
            gc = w_ref[2:3] * x_ref[rows, :f].astype(F32) + w_ref[1:2] * s1 + w_ref[0:1] * s2
            act = (gc * _sigmoid(gc) * x_ref[rows, f:].astype(F32)).astype(BF16)
            a_ref[rows, :] = act
            out = jnp.dot(act, wd_ref[...], preferred_element_type=F32) + res_ref[rows, :]
            o_ref[rows, :] = out
            h_ref[rows, :] = _rms(out, g_ref[...]).astype(BF16)

    row_d = pl.BlockSpec((tm, d), lambda i: (i, 0))
    return pl.pallas_call(
        body,
        name=name,
        grid=(t // tm,),
        in_specs=[pl.BlockSpec((tm, f2), lambda i: (i, 0)), before, pl.BlockSpec((3, f), lambda i: (0, 0)),
                  pl.BlockSpec((None, f, d), lambda i: (layer, 0, 0), pipeline_mode=pl.Buffered(1)),
                  row_d, pl.BlockSpec((1, d), lambda i: (0, 0))],
        out_specs=[row_d, pl.BlockSpec((tm, f), lambda i: (i, 0)), row_d],
        out_shape=[jax.ShapeDtypeStruct((t, d), F32), jax.ShapeDtypeStruct((t, f), BF16), jax.ShapeDtypeStruct((t, d), BF16)],
        compiler_params=_params(("parallel",)),
    )(gu, gu, w, w_down, resid, norm_g)


def _ffngate_bwd(gu, da, w, seq, name):
    t, f2 = gu.shape
    f = f2 // 2
    tm = _conv_tile(seq)
    tps = seq // tm
    before, after = _halo_specs(tm, f2, t // tm)
    _, after_da = _halo_specs(tm, f, t // tm)

    def body(x_ref, xb_ref, xa_ref, da_ref, daa_ref, w_ref, dx_ref, dw_ref):
        i = pl.program_id(0)
        inner_lo = (i % tps != 0).astype(F32)
        inner_hi = (i % tps != tps - 1).astype(F32)
        w0, w1, w2 = w_ref[0:1], w_ref[1:2], w_ref[2:3]

        def dgate(gc, uv, dav):
            sg = _sigmoid(gc)
            return dav * uv * (sg * (1.0 + gc * (1.0 - sg))), dav * (gc * sg)

        g, u = x_ref[:, :f].astype(F32), x_ref[:, f:].astype(F32)
        s1, s2 = _shift_down_mxu(x_ref[:, :f], xb_ref[:, :f].astype(F32)[HALO - 2:] * inner_lo)
        gc = w2 * g + w1 * s1 + w0 * s2
        dgc, du = dgate(gc, u, da_ref[...].astype(F32))
        ga = xa_ref[:, :f].astype(F32)
        a1, a2 = _shift_down(ga, x_ref[tm - HALO:, :f].astype(F32)[HALO - 2:])
        gca = w2 * ga + w1 * a1 + w0 * a2
        dgca, _ = dgate(gca, xa_ref[:, f:].astype(F32), daa_ref[...].astype(F32))
        n1, n2 = _shift_up_mxu(dgc, dgca[0:2] * inner_hi)
        dx_ref[:, :f] = (w2 * dgc + w1 * n1 + w0 * n2).astype(BF16)
        dx_ref[:, f:] = du.astype(BF16)

        @pl.when(i == 0)
        def _():
            dw_ref[...] = jnp.zeros_like(dw_ref)

        dw_ref[0:1] += jnp.sum(dgc * s2, axis=0, keepdims=True)
        dw_ref[1:2] += jnp.sum(dgc * s1, axis=0, keepdims=True)
        dw_ref[2:3] += jnp.sum(dgc * g, axis=0, keepdims=True)

    return pl.pallas_call(
        body,
        name=name,
        grid=(t // tm,),
        in_specs=[pl.BlockSpec((tm, f2), lambda i: (i, 0)), before, after,
                  pl.BlockSpec((tm, f), lambda i: (i, 0)), after_da, pl.BlockSpec((3, f), lambda i: (0, 0))],
        out_specs=[pl.BlockSpec((tm, f2), lambda i: (i, 0)), pl.BlockSpec((8, f), lambda i: (0, 0))],
        out_shape=[jax.ShapeDtypeStruct((t, f2), BF16), jax.ShapeDtypeStruct((8, f), F32)],
        compiler_params=_params(("arbitrary",)),
    )(gu, gu, gu, da, da, w)


def _swap_halves(xt):
    half = HEAD_DIM // 2
    return jnp.concatenate([xt[half:], xt[:half]], axis=0)


def _rope(xt, cos, sin):
    return xt * cos + _swap_halves(xt) * sin


def _unrope(dxt, cos, sin):
    return dxt * cos - _swap_halves(dxt) * sin


def _key_query(count):
    kj = lax.broadcasted_iota(jnp.int32, (WINDOW, count * WINDOW), 0)
    qi = lax.broadcasted_iota(jnp.int32, (WINDOW, count * WINDOW), 1) & (WINDOW - 1)
    return kj, qi


def _band_masks(n, count):
    kj, qi = _key_query(count)
    return kj <= qi, jnp.logical_and(kj > qi, n > 0)


def _lanes(v, count):
    return jnp.concatenate([v] * count, axis=1) if count > 1 else v


def _heads(ref, h0, count):
    parts = [ref[(h0 + g) * HEAD_DIM:(h0 + g + 1) * HEAD_DIM, :] for g in range(count)]
    return jnp.concatenate(parts, axis=1) if count > 1 else parts[0]


def _head_rows(ref, h0, count):
    parts = [ref[h0 + g:h0 + g + 1, :] for g in range(count)]
    return jnp.concatenate(parts, axis=1) if count > 1 else parts[0]


def _head_sinks(sink_ref, h0, count):
    parts = [jnp.full((1, WINDOW), sink_ref[h0 + g], F32) for g in range(count)]
    return jnp.concatenate(parts, axis=1) if count > 1 else parts[0]


def _tn(a, b):
    return lax.dot_general(a, b, _DIMS["tn"], preferred_element_type=F32)


def _nt(a, b):
    return lax.dot_general(a, b, _DIMS["nt"], preferred_element_type=F32)


def _nn(a, b):
    return jnp.dot(a, b, preferred_element_type=F32)


def _attn_fwd(qkv, sinks, cos_t, sin_t, bsz, seq, name, hp):
    t, qw = qkv.shape
    d = qw * 2 // 3
    kvw = d // GROUP
    n_heads, n_kv = d // HEAD_DIM, kvw // HEAD_DIM
    nb = seq // WINDOW
    scale = HEAD_DIM ** -0.5

    def body(sink_ref, xc_ref, xp_ref, cc_ref, sc_ref, cp_ref, sp_ref, o_ref, lse_ref, xt_ref, pt_ref, ot_ref):
        n = pl.program_id(1)
        xt_ref[...] = xc_ref[...].T
        pt_ref[...] = xp_ref[:, d:].T
        cos_c, sin_c, cos_p, sin_p = cc_ref[...], sc_ref[...], cp_ref[...], sp_ref[...]
        cos_g, sin_g = _lanes(cos_c, hp), _lanes(sin_c, hp)
        valid_c, valid_p = _band_masks(n, hp)
        for j in range(n_kv):
            ko = j * HEAD_DIM
            kc = _rope(xt_ref[d + ko:d + ko + HEAD_DIM, :], cos_c, sin_c).astype(BF16)
            kp = _rope(pt_ref[ko:ko + HEAD_DIM, :], cos_p, sin_p).astype(BF16)
            vc = xt_ref[d + kvw + ko:d + kvw + ko + HEAD_DIM, :].astype(BF16)
            vp = pt_ref[kvw + ko:kvw + ko + HEAD_DIM, :].astype(BF16)
            for h0 in range(j * GROUP, (j + 1) * GROUP, hp):
                q = _rope(_heads(xt_ref, h0, hp), cos_g, sin_g).astype(BF16)
                sink = _head_sinks(sink_ref, h0, hp)
                s_c = jnp.where(valid_c, _tn(kc, q) * scale, NEG)
                s_p = jnp.where(valid_p, _tn(kp, q) * scale, NEG)
                m = jnp.maximum(jnp.maximum(jnp.max(s_c, axis=0, keepdims=True), jnp.max(s_p, axis=0, keepdims=True)), sink)
                p_c = jnp.exp(s_c - m)
                p_p = jnp.exp(s_p - m)
                den = jnp.sum(p_c, axis=0, keepdims=True) + jnp.sum(p_p, axis=0, keepdims=True) + jnp.exp(sink - m)
                inv = 1.0 / den
                o_g = _nn(vc, (p_c * inv).astype(BF16)) + _nn(vp, (p_p * inv).astype(BF16))
                lse_g = m + jnp.log(den)
                for g in range(hp):
                    h = h0 + g
                    ot_ref[h * HEAD_DIM:(h + 1) * HEAD_DIM, :] = o_g[:, g * WINDOW:(g + 1) * WINDOW]
                    lse_ref[h:h + 1, :] = lse_g[:, g * WINDOW:(g + 1) * WINDOW]
        o_ref[...] = ot_ref[...].T.astype(BF16)

    cur = lambda b, n: (b * nb + n, 0)
    prev = lambda b, n: (b * nb + jnp.maximum(n - 1, 0), 0)
    tab_c = pl.BlockSpec((HEAD_DIM, WINDOW), lambda b, n: (0, n))
    tab_p = pl.BlockSpec((HEAD_DIM, WINDOW), lambda b, n: (0, jnp.maximum(n - 1, 0)))
    return pl.pallas_call(
        body,
        name=name,
        grid=(bsz, nb),
        in_specs=[SMEM, pl.BlockSpec((WINDOW, qw), cur), pl.BlockSpec((WINDOW, qw), prev), tab_c, tab_c, tab_p, tab_p],
        out_specs=[pl.BlockSpec((WINDOW, d), cur), pl.BlockSpec((n_heads, WINDOW), lambda b, n: (0, b * nb + n))],
        out_shape=[jax.ShapeDtypeStruct((t, d), BF16), jax.ShapeDtypeStruct((n_heads, t), F32)],
        scratch_shapes=[pltpu.VMEM((qw, WINDOW), F32), pltpu.VMEM((2 * kvw, WINDOW), F32), pltpu.VMEM((d, WINDOW), F32)],
        compiler_params=_params(("parallel", "arbitrary")),
    )(sinks, qkv, qkv, cos_t, sin_t, cos_t, sin_t)


def _attn_bwd(qkv, o, lse, do, sinks, cos_t, sin_t, bsz, seq, name, hp):
    t, qw = qkv.shape
    d = qw * 2 // 3
    kvw = d // GROUP
    n_heads, n_kv = d // HEAD_DIM, kvw // HEAD_DIM
    nb = seq // WINDOW
    scale = HEAD_DIM ** -0.5

    def body(sink_ref, xc_ref, xp_ref, xn_ref, oc_ref, on_ref, doc_ref, don_ref, lc_ref, ln_ref,
             cc_ref, sc_ref, cp_ref, sp_ref, cn_ref, sn_ref,
             dx_ref, db_ref, dsk_ref, xt_ref, pt_ref, qn_ref, otc_ref, otn_ref, dtc_ref, dtn_ref, gt_ref):
        b, n = pl.program_id(0), pl.program_id(1)
        xt_ref[...] = xc_ref[...].T
        pt_ref[...] = xp_ref[:, d:].T
        qn_ref[...] = xn_ref[:, :d].T
        otc_ref[...] = oc_ref[...].astype(F32).T
        otn_ref[...] = on_ref[...].astype(F32).T
        dtc_ref[...] = doc_ref[...].T
        dtn_ref[...] = don_ref[...].T
        cos_c, sin_c, cos_p, sin_p, cos_n, sin_n = (cc_ref[...], sc_ref[...], cp_ref[...], sp_ref[...],
                                                    cn_ref[...], sn_ref[...])
        cos_g, sin_g, cos_gn, sin_gn = _lanes(cos_c, hp), _lanes(sin_c, hp), _lanes(cos_n, hp), _lanes(sin_n, hp)
        valid_c, valid_p = _band_masks(n, hp)
        kj, qi = _key_query(hp)
        valid_n = jnp.logical_and(kj > qi, n < nb - 1)

        @pl.when(jnp.logical_and(b == 0, n == 0))
        def _():
            db_ref[...] = jnp.zeros_like(db_ref)
            dsk_ref[...] = jnp.zeros_like(dsk_ref)

        for j in range(n_kv):
            ko = j * HEAD_DIM
            kc = _rope(xt_ref[d + ko:d + ko + HEAD_DIM, :], cos_c, sin_c).astype(BF16)
            kp = _rope(pt_ref[ko:ko + HEAD_DIM, :], cos_p, sin_p).astype(BF16)
            vc = xt_ref[d + kvw + ko:d + kvw + ko + HEAD_DIM, :].astype(BF16)
            vp = pt_ref[kvw + ko:kvw + ko + HEAD_DIM, :].astype(BF16)
            dk = jnp.zeros((HEAD_DIM, WINDOW), F32)
            dv = jnp.zeros((HEAD_DIM, WINDOW), F32)
            for h0 in range(j * GROUP, (j + 1) * GROUP, hp):
                q = _rope(_heads(xt_ref, h0, hp), cos_g, sin_g).astype(BF16)
                do_g = _heads(dtc_ref, h0, hp)
                do_b = do_g.astype(BF16)
                lse_g = _head_rows(lc_ref, h0, hp)
                delta = jnp.sum(_heads(otc_ref, h0, hp) * do_g, axis=0, keepdims=True)
                p_c = jnp.exp(jnp.where(valid_c, _tn(kc, q) * scale, NEG) - lse_g)
                p_p = jnp.exp(jnp.where(valid_p, _tn(kp, q) * scale, NEG) - lse_g)
                ds_c = (p_c * (_tn(vc, do_b) - delta)).astype(BF16)
                ds_p = (p_p * (_tn(vp, do_b) - delta)).astype(BF16)
                dq = _unrope((_nn(kc, ds_c) + _nn(kp, ds_p)) * scale, cos_g, sin_g)
                dsk = -jnp.exp(_head_sinks(sink_ref, h0, hp) - lse_g) * delta
                for g in range(hp):
                    h = h0 + g
                    gt_ref[h * HEAD_DIM:(h + 1) * HEAD_DIM, :] = dq[:, g * WINDOW:(g + 1) * WINDOW]
                    dsk_ref[h:h + 1, :] += dsk[:, g * WINDOW:(g + 1) * WINDOW]
                q2 = _rope(_heads(qn_ref, h0, hp), cos_gn, sin_gn).astype(BF16)
                do2 = _heads(dtn_ref, h0, hp)
                do2_b = do2.astype(BF16)
                delta2 = jnp.sum(_heads(otn_ref, h0, hp) * do2, axis=0, keepdims=True)
                p_n = jnp.exp(jnp.where(valid_n, _tn(kc, q2) * scale, NEG) - _head_rows(ln_ref, h0, hp))
                ds_n = (p_n * (_tn(vc, do2_b) - delta2)).astype(BF16)
                dv += _nt(do_b, p_c.astype(BF16)) + _nt(do2_b, p_n.astype(BF16))
                dk += _nt(q, ds_c) + _nt(q2, ds_n)
            gt_ref[d + ko:d + ko + HEAD_DIM, :] = _unrope(dk * scale, cos_c, sin_c)
            gt_ref[d + kvw + ko:d + kvw + ko + HEAD_DIM, :] = dv
        dx = gt_ref[...].T
        dx_ref[...] = dx.astype(BF16)
        db_ref[...] += _fold8(dx)

    cur = lambda b, n: (b * nb + n, 0)
    prev = lambda b, n: (b * nb + jnp.maximum(n - 1, 0), 0)
    nxt = lambda b, n: (b * nb + jnp.minimum(n + 1, nb - 1), 0)
    stat_c = pl.BlockSpec((n_heads, WINDOW), lambda b, n: (0, b * nb + n))
    stat_n = pl.BlockSpec((n_heads, WINDOW), lambda b, n: (0, b * nb + jnp.minimum(n + 1, nb - 1)))
    tab_c = pl.BlockSpec((HEAD_DIM, WINDOW), lambda b, n: (0, n))
    tab_p = pl.BlockSpec((HEAD_DIM, WINDOW), lambda b, n: (0, jnp.maximum(n - 1, 0)))
    tab_n = pl.BlockSpec((HEAD_DIM, WINDOW), lambda b, n: (0, jnp.minimum(n + 1, nb - 1)))
    return pl.pallas_call(
        body,
        name=name,
        grid=(bsz, nb),
        in_specs=[SMEM, pl.BlockSpec((WINDOW, qw), cur), pl.BlockSpec((WINDOW, qw), prev), pl.BlockSpec((WINDOW, qw), nxt),
                  pl.BlockSpec((WINDOW, d), cur), pl.BlockSpec((WINDOW, d), nxt),
                  pl.BlockSpec((WINDOW, d), cur), pl.BlockSpec((WINDOW, d), nxt),
                  stat_c, stat_n, tab_c, tab_c, tab_p, tab_p, tab_n, tab_n],
        out_specs=[pl.BlockSpec((WINDOW, qw), cur), pl.BlockSpec((8, qw), lambda b, n: (0, 0)),
                   pl.BlockSpec((n_heads, WINDOW), lambda b, n: (0, 0))],
        out_shape=[jax.ShapeDtypeStruct((t, qw), BF16), jax.ShapeDtypeStruct((8, qw), F32),
                   jax.ShapeDtypeStruct((n_heads, WINDOW), F32)],
        scratch_shapes=[pltpu.VMEM((qw, WINDOW), F32), pltpu.VMEM((2 * kvw, WINDOW), F32), pltpu.VMEM((d, WINDOW), F32),
                        pltpu.VMEM((d, WINDOW), F32), pltpu.VMEM((d, WINDOW), F32), pltpu.VMEM((d, WINDOW), F32),
                        pltpu.VMEM((d, WINDOW), F32), pltpu.VMEM((qw, WINDOW), F32)],
        compiler_params=_params(("arbitrary", "arbitrary")),
    )(sinks, qkv, qkv, qkv, o, o, do, do, lse, lse, cos_t, sin_t, cos_t, sin_t, cos_t, sin_t)


def _place():
    return lax.axis_index("x"), lax.axis_index("y"), lax.axis_index("c")


def _other_chips(x, y):
    return [(1 - x, y), (x, 1 - y), (1 - x, 1 - y)]


def _place_shard(w, axis, q, name):
    ly, k, n = w.shape
    tr = _pick(k, (256, 128, 64, 32, 16, 8))
    steps = k // tr
    shape = (ly, k * N_CHIPS, n) if axis == 1 else (ly, k, n * N_CHIPS)
    if axis == 1:
        out_spec = pl.BlockSpec((None, tr, n), lambda l, i, q_ref: (l, q_ref[0] * steps + i, 0))
    else:
        out_spec = pl.BlockSpec((None, tr, n), lambda l, i, q_ref: (l, i, q_ref[0]))

    def body(q_ref, w_ref, o_ref):
        del q_ref
        o_ref[...] = w_ref[...].astype(BF16)

    return pl.pallas_call(
        body,
        name=name,
        grid_spec=pltpu.PrefetchScalarGridSpec(
            num_scalar_prefetch=1, grid=(ly, steps),
            in_specs=[pl.BlockSpec((None, tr, n), lambda l, i, q_ref: (l, i, 0))], out_specs=out_spec),
        out_shape=jax.ShapeDtypeStruct(shape, BF16),
        compiler_params=_params(("parallel", "parallel")),
    )(q, w)


def _half_block(ref, axis, layer, px, py, pc):
    blk = 2 * px + py
    if axis == 1:
        rows = ref.shape[1] // (2 * N_CHIPS)
        return ref.at[layer, pl.ds(pl.multiple_of((2 * blk + pc) * rows, 8), rows), :]
    rows, width = ref.shape[1] // 2, ref.shape[2] // N_CHIPS
    return ref.at[layer, pl.ds(pl.multiple_of(pc * rows, 8), rows), pl.ds(pl.multiple_of(blk * width, 128), width)]


def _gather_copy(refs, axes, pieces, send_sems, recv_sems, p, k, stage, whose):
    x, y, c = _place()
    chip = _other_chips(x, y)[k]
    i, layer = pieces[p]
    if stage == 0:
        origin = (x, y, c) if whose == "mine" else (*chip, c)
        to = (*chip, c)
    else:
        origin = (*chip, c) if whose == "mine" else (*chip, 1 - c)
        to = (x, y, 1 - c)
    blk = _half_block(refs[i], axes[i], layer, *origin)
    return pltpu.make_async_remote_copy(src_ref=blk, dst_ref=blk, send_sem=send_sems.at[p * 3 + k],
                                        recv_sem=recv_sems.at[p * 3 + k], device_id=to, device_id_type=MESH)


def _gather_weights(fulls, axes, pieces, name):
    n, m = len(fulls), 3 * len(pieces)

    def body(*refs):
        dst = refs[n:2 * n]
        sems = refs[2 * n:]
        todo = [(p, k) for p in range(len(pieces)) for k in range(3)]
        sends = [_gather_copy(dst, axes, pieces, sems[0], sems[1], p, k, 0, "mine") for p, k in todo]
        for cp in sends:
            cp.start()
        for p, k in todo:
            _gather_copy(dst, axes, pieces, sems[0], sems[1], p, k, 0, "theirs").wait_recv()
            sends.append(_gather_copy(dst, axes, pieces, sems[2], sems[3], p, k, 1, "mine"))
            sends[-1].start()
        for p, k in todo:
            _gather_copy(dst, axes, pieces, sems[2], sems[3], p, k, 1, "theirs").wait_recv()
        for cp in sends:
            cp.wait_send()

    return pl.pallas_call(
        body,
        name=name,
        in_specs=[ANY] * n,
        out_specs=[ANY] * n,
        out_shape=[jax.ShapeDtypeStruct(f.shape, f.dtype) for f in fulls],
        input_output_aliases={i: i for i in range(n)},
        scratch_shapes=[pltpu.SemaphoreType.DMA((m,))] * 4,
    )(*fulls)


HBM_SPEC = pl.BlockSpec(memory_space=pltpu.HBM)
SEM_SPEC = pl.BlockSpec(memory_space=pltpu.SEMAPHORE)


def _gather_start(fulls, axes, pieces, stage, name):
    n, m = len(fulls), 3 * len(pieces)

    def body(*refs):
        src = refs[:n]
        send_sems, recv_sems = refs[2 * n], refs[2 * n + 1]
        for p in range(len(pieces)):
            for k in range(3):
                _gather_copy(src, axes, pieces, send_sems, recv_sems, p, k, stage, "mine").start()

    out = pl.pallas_call(
        body,
        name=name,
        in_specs=[HBM_SPEC] * n,
        out_specs=[HBM_SPEC] * n + [SEM_SPEC, SEM_SPEC],
        out_shape=[pltpu.HBM(f.shape, f.dtype) for f in fulls] + [pltpu.SemaphoreType.DMA((m,)), pltpu.SemaphoreType.DMA((m,))],
        input_output_aliases={i: i for i in range(n)},
        compiler_params=pltpu.CompilerParams(has_side_effects=pltpu.SideEffectType.DATAFLOW_SIDE_EFFECTING),
    )(*[pltpu.with_memory_space_constraint(f, pltpu.HBM) for f in fulls])
    return list(out[:n]), out[n], out[n + 1]


def _gather_wait(fulls, send_sems, recv_sems, after, axes, pieces, stage, name):
    n = len(fulls)

    def body(*refs):
        src = refs[:n]
        s_sems, r_sems = refs[n], refs[n + 1]
        for p in range(len(pieces)):
            for k in range(3):
                _gather_copy(src, axes, pieces, s_sems, r_sems, p, k, stage, "mine").wait_send()
                _gather_copy(src, axes, pieces, s_sems, r_sems, p, k, stage, "theirs").wait_recv()

    out = pl.pallas_call(
        body,
        name=name,
        in_specs=[HBM_SPEC] * n + [SEM_SPEC, SEM_SPEC, ANY],
        out_specs=[HBM_SPEC] * n,
        out_shape=[pltpu.HBM(f.shape, f.dtype) for f in fulls],
        input_output_aliases={i: i for i in range(n)},
        compiler_params=pltpu.CompilerParams(has_side_effects=pltpu.SideEffectType.DATAFLOW_SIDE_EFFECTING),
    )(*fulls, send_sems, recv_sems, after)
    return list(out)


def _half_shape(kind, shape):
    if kind == "col":
        return (shape[0] // 2, shape[1])
    return (N_CHIPS, shape[1] // 2, shape[2])


def _half_of(kind, ref, h):
    if kind == "col":
        r = ref.shape[0] // 2
        return ref.at[pl.ds(pl.multiple_of(h * r, 8), r), :]
    r = ref.shape[1] // 2
    return ref.at[:, pl.ds(pl.multiple_of(h * r, 8), r), :]


def _pair_copy(src, dst, kinds, send_sems, recv_sems, i):
    x, y, c = _place()
    return pltpu.make_async_remote_copy(src_ref=_half_of(kinds[i], src[i], 1 - c), dst_ref=dst[i], send_sem=send_sems.at[i],
                                        recv_sem=recv_sems.at[i], device_id=(x, y, 1 - c), device_id_type=MESH)


def _pair_start(grads, kinds, token, name):
    n = len(grads)
    lands = [pltpu.HBM(_half_shape(kd, g.shape), g.dtype) for g, kd in zip(grads, kinds)]

    def body(*refs):
        src, dst = refs[:n], refs[2 * n + 2:3 * n + 2]
        send_sems, recv_sems = refs[3 * n + 2], refs[3 * n + 3]
        for i in range(n):
            _pair_copy(src, dst, kinds, send_sems, recv_sems, i).start()

    arrays = list(grads) + [token]
    out = pl.pallas_call(
        body,
        name=name,
        in_specs=[HBM_SPEC] * (n + 1),
        out_specs=[HBM_SPEC] * (2 * n + 1) + [SEM_SPEC, SEM_SPEC],
        out_shape=[pltpu.HBM(a.shape, a.dtype) for a in arrays] + lands + [pltpu.SemaphoreType.DMA((n,)), pltpu.SemaphoreType.DMA((n,))],
        input_output_aliases={i: i for i in range(n + 1)},
        compiler_params=pltpu.CompilerParams(has_side_effects=pltpu.SideEffectType.DATAFLOW_SIDE_EFFECTING),
    )(*[pltpu.with_memory_space_constraint(a, pltpu.HBM) for a in arrays])
    return list(out[:n]), out[n], list(out[n + 1:2 * n + 1]), out[2 * n + 1], out[2 * n + 2]


def _pair_wait(grads, kinds, token, lands, send_sems, recv_sems, name):
    n = len(grads)

    def body(*refs):
        src, dst = refs[:n], refs[n + 1:2 * n + 1]
        s_sems, r_sems = refs[2 * n + 1], refs[2 * n + 2]
        for i in range(n):
            cp = _pair_copy(src, dst, kinds, s_sems, r_sems, i)
            cp.wait_send()
            cp.wait_recv()

    arrays = list(grads) + [token] + list(lands)
    out = pl.pallas_call(
        body,
        name=name,
        in_specs=[HBM_SPEC] * (2 * n + 1) + [SEM_SPEC, SEM_SPEC],
        out_specs=[HBM_SPEC] * (2 * n + 1),
        out_shape=[pltpu.HBM(a.shape, a.dtype) for a in arrays],
        input_output_aliases={i: i for i in range(2 * n + 1)},
        compiler_params=pltpu.CompilerParams(has_side_effects=pltpu.SideEffectType.DATAFLOW_SIDE_EFFECTING),
    )(*arrays, send_sems, recv_sems)
    return list(out[:n]), out[n], list(out[n + 1:])


def _pair_sum(grad, recv, kind, c, own, layer, name):
    r, cols = own.shape[2:]
    if kind == "col":
        tr = _pick(r, (256, 128, 64, 32, 16, 8))
        steps = r // tr
        grid = (N_CHIPS, steps)
        g_spec = pl.BlockSpec((tr, cols), lambda s, i, c_ref: (c_ref[0] * steps + i, s))
        r_spec = pl.BlockSpec((tr, cols), lambda s, i, c_ref: (i, s))
        o_spec = pl.BlockSpec((None, None, tr, cols), lambda s, i, c_ref: (layer, s, i, 0))
        g_in = grad
    else:
        grid = (N_CHIPS, 1)
        g_spec = pl.BlockSpec((None, None, r, cols), lambda s, i, c_ref: (s, c_ref[0], 0, 0))
        r_spec = pl.BlockSpec((None, r, cols), lambda s, i, c_ref: (s, 0, 0))
        o_spec = pl.BlockSpec((None, None, r, cols), lambda s, i, c_ref: (layer, s, 0, 0))
        g_in = grad.reshape(N_CHIPS, 2, r, cols)

    def body(c_ref, g_ref, r_ref, own_ref, o_ref):
        del c_ref, own_ref
        o_ref[...] = (g_ref[...].astype(F32) + r_ref[...].astype(F32)).astype(o_ref.dtype)

    return pl.pallas_call(
        body,
        name=name,
        grid_spec=pltpu.PrefetchScalarGridSpec(num_scalar_prefetch=1, grid=grid, in_specs=[g_spec, r_spec, ANY], out_specs=o_spec),
        out_shape=jax.ShapeDtypeStruct(own.shape, own.dtype),
        input_output_aliases={3: 0},
        compiler_params=_params(("parallel", "parallel")),
    )(c, g_in, recv, own)


def _scatter_copy(own, mine, sib, pieces, send_sems, recv_sems, p, k, stage, whose):
    x, y, c = _place()
    q = 2 * x + y
    i, layer = pieces[p]
    per = 4 if stage == 0 else 3
    if k == 3:
        src, dst, to = own[i].at[layer, q], sib[i].at[layer, q], (x, y, 1 - c)
    else:
        chip = _other_chips(x, y)[k]
        slot = 2 * chip[0] + chip[1]
        if stage == 0:
            to = (*chip, c)
            src, dst = (own[i].at[layer, slot], mine[i].at[layer, q]) if whose == "mine" else (own[i].at[layer, q], mine[i].at[layer, slot])
        else:
            to = (x, y, 1 - c)
            src, dst = mine[i].at[layer, slot], sib[i].at[layer, slot]
    return pltpu.make_async_remote_copy(src_ref=src, dst_ref=dst, send_sem=send_sems.at[p * per + k],
                                        recv_sem=recv_sems.at[p * per + k], device_id=to, device_id_type=MESH)


def _scatter_start(own, mine, sib, token, pieces, stage, name):
    n = len(own)
    per = 4 if stage == 0 else 3
    m = per * len(pieces)
    n_arr = 3 * n + 1

    def body(*refs):
        o, mi, si = refs[:n], refs[n:2 * n], refs[2 * n:3 * n]
        send_sems, recv_sems = refs[2 * n_arr], refs[2 * n_arr + 1]
        for p in range(len(pieces)):
            for k in range(per):
                _scatter_copy(o, mi, si, pieces, send_sems, recv_sems, p, k, stage, "mine").start()

    arrays = list(own) + list(mine) + list(sib) + [token]
    out = pl.pallas_call(
        body,
        name=name,
        in_specs=[HBM_SPEC] * n_arr,
        out_specs=[HBM_SPEC] * n_arr + [SEM_SPEC, SEM_SPEC],
        out_shape=[pltpu.HBM(a.shape, a.dtype) for a in arrays] + [pltpu.SemaphoreType.DMA((m,)), pltpu.SemaphoreType.DMA((m,))],
        input_output_aliases={i: i for i in range(n_arr)},
        compiler_params=pltpu.CompilerParams(has_side_effects=pltpu.SideEffectType.DATAFLOW_SIDE_EFFECTING),
    )(*[pltpu.with_memory_space_constraint(a, pltpu.HBM) for a in arrays])
    return list(out[:n]), list(out[n:2 * n]), list(out[2 * n:3 * n]), out[3 * n], out[n_arr], out[n_arr + 1]


def _scatter_wait(own, mine, sib, token, send_sems, recv_sems, pieces, stage, name):
    n = len(own)
    per = 4 if stage == 0 else 3
    n_arr = 3 * n + 1

    def body(*refs):
        o, mi, si = refs[:n], refs[n:2 * n], refs[2 * n:3 * n]
        s_sems, r_sems = refs[n_arr], refs[n_arr + 1]
        for p in range(len(pieces)):
            for k in range(per):
                _scatter_copy(o, mi, si, pieces, s_sems, r_sems, p, k, stage, "mine").wait_send()
                _scatter_copy(o, mi, si, pieces, s_sems, r_sems, p, k, stage, "theirs").wait_recv()

    arrays = list(own) + list(mine) + list(sib) + [token]
    out = pl.pallas_call(
        body,
        name=name,
        in_specs=[HBM_SPEC] * n_arr + [SEM_SPEC, SEM_SPEC],
        out_specs=[HBM_SPEC] * n_arr,
        out_shape=[pltpu.HBM(a.shape, a.dtype) for a in arrays],
        input_output_aliases={i: i for i in range(n_arr)},
        compiler_params=pltpu.CompilerParams(has_side_effects=pltpu.SideEffectType.DATAFLOW_SIDE_EFFECTING),
    )(*arrays, send_sems, recv_sems)
    return list(out[:n]), list(out[n:2 * n]), list(out[2 * n:3 * n]), out[3 * n]


def _reduce_adamw(own, mine, sib, w, m, v, qc, name):
    ly, _, r, cols = mine.shape
    tr = _pick(r, (128, 64, 32, 16, 8))
    steps = r // tr
    c1 = 1.0 - ADAM_B1 ** ADAM_STEP
    c2 = 1.0 - ADAM_B2 ** ADAM_STEP

    def body(qc_ref, own_ref, mine_ref, sib_ref, w_ref, m_ref, v_ref, g_ref, d_ref, nm_ref, nv_ref):
        q = qc_ref[0]
        mine_sum = sib_sum = None
        for s in range(N_CHIPS):
            a = jnp.where(q == s, own_ref[...], mine_ref[s]).astype(F32)
            b = sib_ref[s].astype(F32)
            mine_sum = a if s == 0 else mine_sum + a
            sib_sum = b if s == 0 else sib_sum + b
        gv = jnp.where(pl.program_id(1) == qc_ref[1], mine_sum, sib_sum)
        nm = ADAM_B1 * m_ref[...] + (1.0 - ADAM_B1) * gv
        nv = ADAM_B2 * v_ref[...] + (1.0 - ADAM_B2) * (gv * gv)
        g_ref[...] = gv
        d_ref[...] = -ADAM_LR * ((nm / c1) / (jnp.sqrt(nv / c2) + ADAM_EPS) + ADAM_WD * w_ref[...])
        nm_ref[...] = nm
        nv_ref[...] = nv

    def mine_rows(h, i, qc_ref):
        return jnp.where(h == qc_ref[1], i, 0)

    def sib_rows(h, i, qc_ref):
        return jnp.where(h == qc_ref[1], 0, i)

    own_spec = pl.BlockSpec((None, None, tr, cols), lambda l, h, i, qc_ref: (l, qc_ref[0], mine_rows(h, i, qc_ref), 0))
    mine_spec = pl.BlockSpec((None, N_CHIPS, tr, cols), lambda l, h, i, qc_ref: (l, 0, mine_rows(h, i, qc_ref), 0))
    sib_spec = pl.BlockSpec((None, N_CHIPS, tr, cols), lambda l, h, i, qc_ref: (l, 0, sib_rows(h, i, qc_ref), 0))
    spec = pl.BlockSpec((None, tr, cols), lambda l, h, i, qc_ref: (l, h * steps + i, 0))
    shp = jax.ShapeDtypeStruct(w.shape, F32)
    return pl.pallas_call(
        body,
        name=name,
        grid_spec=pltpu.PrefetchScalarGridSpec(
            num_scalar_prefetch=1, grid=(ly, N_CORES, steps),
            in_specs=[own_spec, mine_spec, sib_spec, spec, spec, spec], out_specs=[spec] * 4),
        out_shape=[shp] * 4,
        compiler_params=_params(("parallel", "parallel", "parallel")),
    )(qc, own, mine, sib, w, m, v)


def _allreduce_small(v, name):
    r, w = v.shape

    def body(v_ref, o_ref, buf_ref, send_sems, recv_sems):
        x, y, c = _place()
        me = 4 * x + 2 * y + c

        def peer(k):
            return x ^ (k >> 2), y ^ ((k >> 1) & 1), c ^ (k & 1)

        def remote(k, slot):
            return pltpu.make_async_remote_copy(
                src_ref=v_ref, dst_ref=buf_ref.at[slot], send_sem=send_sems.at[k - 1], recv_sem=recv_sems.at[k - 1],
                device_id=peer(k), device_id_type=MESH)

        sends = [remote(k, me) for k in range(1, N_DEV)]
        for cp in sends:
            cp.start()
        buf_ref[me] = v_ref[...]
        for k in range(1, N_DEV):
            px, py, pc = peer(k)
            remote(k, 4 * px + 2 * py + pc).wait_recv()
        for cp in sends:
            cp.wait_send()
        acc = buf_ref[0]
        for dev in range(1, N_DEV):
            acc = acc + buf_ref[dev]
        o_ref[...] = acc

    vm = pl.BlockSpec(memory_space=pltpu.VMEM)
    return pl.pallas_call(
        body,
        name=name,
        in_specs=[vm],
        out_specs=vm,
        out_shape=jax.ShapeDtypeStruct((r, w), F32),
        scratch_shapes=[pltpu.VMEM((N_DEV, r, w), F32), pltpu.SemaphoreType.DMA((N_DEV - 1,)), pltpu.SemaphoreType.DMA((N_DEV - 1,))],
        compiler_params=pltpu.CompilerParams(vmem_limit_bytes=VMEM_LIMIT_BYTES),
    )(v)


def _adamw(w, g, m, v, name):
    ly, r, c = w.shape
    tr = _pick(r, (256, 128, 64, 32, 16, 8))
    c1 = 1.0 - ADAM_B1 ** ADAM_STEP
    c2 = 1.0 - ADAM_B2 ** ADAM_STEP

    def body(w_ref, g_ref, m_ref, v_ref, d_ref, nm_ref, nv_ref):
        gv = g_ref[...]
        nm = ADAM_B1 * m_ref[...] + (1.0 - ADAM_B1) * gv
        nv = ADAM_B2 * v_ref[...] + (1.0 - ADAM_B2) * (gv * gv)
        d_ref[...] = -ADAM_LR * ((nm / c1) / (jnp.sqrt(nv / c2) + ADAM_EPS) + ADAM_WD * w_ref[...])
        nm_ref[...] = nm
        nv_ref[...] = nv

    spec = pl.BlockSpec((None, tr, c), lambda l, i: (l, i, 0))
    shp = jax.ShapeDtypeStruct((ly, r, c), F32)
    return pl.pallas_call(
        body,
        name=name,
        grid=(ly, r // tr),
        in_specs=[spec] * 4,
        out_specs=[spec] * 3,
        out_shape=[shp] * 3,
        compiler_params=_params(("parallel", "parallel")),
    )(w, g, m, v)


def _rope_tables(seq):
    pos = jnp.arange(seq, dtype=F32)
    inv_freq = 1.0 / (ROPE_THETA ** (jnp.arange(0, HEAD_DIM, 2, dtype=F32) / HEAD_DIM))
    ang = (pos[:, None] * inv_freq[None, :]).T
    cos, sin = jnp.cos(ang), jnp.sin(ang)
    return jnp.concatenate([cos, cos], axis=0), jnp.concatenate([-sin, sin], axis=0)


def _pack(vs, fill=0.0):
    p = jnp.concatenate([v.reshape(-1) for v in vs])
    size = -(-p.shape[0] // 8192) * 8192
    return jnp.pad(p, (0, size - p.shape[0]), constant_values=fill).reshape(-1, 1024)


def _unpack(p, like):
    p = p.reshape(-1)
    out, o = [], 0
    for v in like:
        n = int(math.prod(v.shape))
        out.append(p[o:o + n].reshape(v.shape))
        o += n
    return out


def kernel(x, norm_mix, norm_ffn, norm_final, conv_w_in, conv_w_conv, conv_w_out, attn_w_qkv, attn_b_qkv, attn_sinks, attn_w_o, attn_b_o, ffn_w_in, ffn_w_conv, ffn_w_down, loss_target, m_norm_mix, m_norm_ffn, m_norm_final, m_conv_w_in, m_conv_w_conv, m_conv_w_out, m_attn_w_qkv, m_attn_b_qkv, m_attn_sinks, m_attn_w_o, m_attn_b_o, m_ffn_w_in, m_ffn_w_conv, m_ffn_w_down, v_norm_mix, v_norm_ffn, v_norm_final, v_conv_w_in, v_conv_w_conv, v_conv_w_out, v_attn_w_qkv, v_attn_b_qkv, v_attn_sinks, v_attn_w_o, v_attn_b_o, v_ffn_w_in, v_ffn_w_conv, v_ffn_w_down):
    bsz, seq, d = x.shape
    t = bsz * seq
    depth = norm_mix.shape[0]
    n_conv, n_attn = conv_w_in.shape[0], attn_w_qkv.shape[0]
    xq, yq, cq = _place()
    q = 2 * xq + yq

    big = [conv_w_in, conv_w_out, attn_w_qkv, attn_w_o, ffn_w_in, ffn_w_down]
    axes = [2, 1, 2, 1, 2, 1]
    q_arr = q.astype(jnp.int32).reshape(1)
    c_arr = cq.astype(jnp.int32).reshape(1)
    weights = [_place_shard(w, ax, q_arr, f"place_shard{n}") for n, (w, ax) in enumerate(zip(big, axes))]

    def pieces_of(i):
        return [(0, i // 2), (1, i // 2), (4, i), (5, i)] if i % 2 == 0 else [(2, i // 2), (3, i // 2), (4, i), (5, i)]

    weights = _gather_weights(weights, axes, pieces_of(0)[:2], "gather_mixer0")

    small_cols = [conv_w_conv, attn_b_qkv, attn_b_o, ffn_w_conv]

    def placed(v):
        width = v.shape[-1]
        full = jnp.zeros(v.shape[:-1] + (N_CHIPS * width,), F32)
        return lax.dynamic_update_slice_in_dim(full, v * (1.0 / N_CORES), q * width, axis=v.ndim - 1)

    full_cols = [placed(v) for v in small_cols]
    wc_conv, b_qkv, b_o, wf_conv = _unpack(_allreduce_small(_pack(full_cols), "gather_small"), full_cols)
    cos_t, sin_t = _rope_tables(seq)

    xs = x.reshape(t, d)
    saved = []
    sems = {}

    def fetch(pieces, stage, tag):
        nonlocal weights
        weights, sems[tag, 0], sems[tag, 1] = _gather_start(weights, axes, pieces, stage, f"gather_{'ici' if stage == 0 else 'pass'}_start{tag}")

    def settle(pieces, stage, tag, after):
        nonlocal weights
        weights = _gather_wait(weights, sems[tag, 0], sems[tag, 1], after, axes, pieces, stage,
                               f"gather_{'ici' if stage == 0 else 'pass'}_wait{tag}")

    for i in range(depth):
        j = i // 2
        ahead = pieces_of(i + 1) if i + 1 < depth else None
        if i == 0:
            fetch(pieces_of(0)[2:], 0, "0")
        elif ahead:
            fetch(ahead, 0, str(i + 1))
        w_cin, w_cout, w_qkv, w_o, w_fin, w_fdown = weights
        if i == 0:
            h = _rms_fwd(xs, norm_mix[0:1], "norm_mix_fwd0")
        g_ffn = norm_ffn[i:i + 1]
        if i % 2 == 0:
            pre = _mm(h, w_cin, "nn", BF16, layer=j, tm=1024, tn=768, tk=4096, name=f"conv_in_fwd{i}")
            mixed = _convgate_fwd(pre, wc_conv[j], seq, f"conv_gate_fwd{i}")
            if i == 0:
                settle(pieces_of(0)[2:], 0, "0", mixed)
                fetch(pieces_of(0)[2:], 1, "0")
                w_cin, w_cout, w_qkv, w_o, w_fin, w_fdown = weights
            x_mid, h2 = _mm(mixed, w_cout, "nn", F32, layer=j, residual=xs, norm_g=g_ffn, tm=512, tn=1024, tk=4096,
                            name=f"conv_out_fwd{i}")
            lse = None
        else:
            pre = _mm(h, w_qkv, "nn", F32, layer=j, bias=b_qkv[j:j + 1], tm=1024, tn=768, tk=4096, name=f"qkv_fwd{i}")
            mixed, lse = _attn_fwd(pre, attn_sinks[j], cos_t, sin_t, bsz, seq, f"attn_fwd{i}", hp=1)
            x_mid, h2 = _mm(mixed, w_o, "nn", F32, layer=j, bias=b_o[j:j + 1], residual=xs, norm_g=g_ffn, tm=512, tn=1024,
                            tk=4096, name=f"attn_out_fwd{i}")
        if i == 0:
            settle(pieces_of(0)[2:], 1, "0", h2)
            fetch(ahead, 0, "1")
        elif ahead:
            settle(ahead, 0, str(i + 1), x_mid)
            fetch(ahead, 1, str(i + 1))
        w_cin, w_cout, w_qkv, w_o, w_fin, w_fdown = weights
        gu = _mm(h2, w_fin, "nn", BF16, layer=i, n_outer=True, tm=2048, tn=1408, tk=4096, name=f"ffn_in_fwd{i}")
        if i == 0:
            settle(ahead, 0, "1", gu)
            fetch(ahead, 1, "1")
            w_cin, w_cout, w_qkv, w_o, w_fin, w_fdown = weights
        g_next = norm_mix[i + 1:i + 2] if i + 1 < depth else norm_final.reshape(1, d)
        x_next, act, h_next = _ffn_gate_down_fwd(gu, wf_conv[i], w_fdown, i, x_mid, g_next, seq, f"ffn_gate_down_fwd{i}")
        if ahead:
            settle(ahead, 1, str(i + 1), x_next)
        saved.append((xs, h, pre, mixed, lse, x_mid, h2, gu, act))
        xs, h = x_next, h_next
    w_cin, w_cout, w_qkv, w_o, w_fin, w_fdown = weights

    dx, dxb, sq, dg_final = _loss_head(xs, loss_target.reshape(t, d), norm_final.reshape(1, d), "loss_head")
    loss = lax.psum(0.5 * jnp.sum(sq) / d, ("x", "y", "c"))

    g_norm_mix, g_norm_ffn = [None] * depth, [None] * depth
    g_cin, g_cconv, g_cout = [None] * n_conv, [None] * n_conv, [None] * n_conv
    g_qkv, g_bqkv, g_sinks, g_o, g_bo = ([None] * n_attn for _ in range(5))
    g_fin, g_fconv, g_fdown = [None] * depth, [None] * depth, [None] * depth

    kinds6 = ["col", "row", "col", "row", "col", "row"]
    layers6 = [n_conv, n_conv, n_attn, n_attn, depth, depth]
    big_w = [conv_w_in, conv_w_out, attn_w_qkv, attn_w_o, ffn_w_in, ffn_w_down]

    def slot_stack(n):
        k, cols = big_w[n].shape[1], big_w[n].shape[2]
        r = k // 2
        return lax.empty((layers6[n], N_CHIPS, r, cols), BF16)

    own = [slot_stack(n) for n in range(6)]
    mine = [slot_stack(n) for n in range(6)]
    sib = [slot_stack(n) for n in range(6)]
    flight = {}

    def group(i, part):
        return f"{part}{i}", (pieces_of(i)[:2] if part == "m" else pieces_of(i)[2:])

    def scatter(grp, stage, action, token):
        tag, pieces = grp
        ts = [ti for ti, _ in pieces]
        local = [(n, l) for n, (_, l) in enumerate(pieces)]
        sub = ([own[ti] for ti in ts], [mine[ti] for ti in ts], [sib[ti] for ti in ts], token)
        label = f"grad_{'ici' if stage == 0 else 'pass'}_{action}_{tag}"
        if action == "start":
            o, mi, si, token, s_sems, r_sems = _scatter_start(*sub, local, stage, label)
            flight[tag] = (s_sems, r_sems)
        else:
            o, mi, si, token = _scatter_wait(*sub, *flight[tag], local, stage, label)
        for n, ti in enumerate(ts):
            own[ti], mine[ti], sib[ti] = o[n], mi[n], si[n]
        return token

    def pair_begin(grp, token):
        tag, pieces = grp
        grads = {0: g_cin, 1: g_cout, 2: g_qkv, 3: g_o, 4: g_fin, 5: g_fdown}
        parts, kinds = [], []
        for ti, l in pieces:
            g = grads[ti][l]
            parts.append(g if kinds6[ti] == "col" else g.reshape(N_CHIPS, g.shape[0] // N_CHIPS, g.shape[1]))
            kinds.append(kinds6[ti])
        parts, token, lands, s_sems, r_sems = _pair_start(parts, kinds, token, f"grad_pair_start_{tag}")
        flight["pair" + tag] = (parts, kinds, lands, s_sems, r_sems)
        return token

    def pair_finish(grp, token):
        tag, pieces = grp
        parts, kinds, lands, s_sems, r_sems = flight["pair" + tag]
        parts, token, recv = _pair_wait(parts, kinds, token, lands, s_sems, r_sems, f"grad_pair_wait_{tag}")
        for (ti, l), g, r in zip(pieces, parts, recv):
            own[ti] = _pair_sum(g, r, kinds6[ti], c_arr, own[ti], l, f"grad_pair_sum_{tag}_{ti}")
        return scatter(grp, 0, "start", token)

    for i in reversed(range(depth)):
        j = i // 2
        x_in, h, pre, mixed, lse, x_mid, h2, gu, act = saved[i]
        da = _mm(dxb, w_fdown, "nt", BF16, layer=i, n_outer=True, tm=2048, tn=1408, tk=4096, name=f"ffn_down_dx{i}")
        g_fdown[i] = _mm(act, dxb, "tn", BF16, tm=1408, tn=1024, tk=2048, name=f"ffn_down_dw{i}")
        dgu, dwc = _ffngate_bwd(gu, da, wf_conv[i], seq, f"ffn_gate_bwd{i}")
        g_fconv[i] = dwc[:3]
        g_fin[i] = _mm(h2, dgu, "tn", BF16, tm=1024, tn=1408, tk=2048, name=f"ffn_in_dw{i}")
        dgu = pair_begin(group(i, "f"), dgu)
        dx, dxb, dg, colsum = _rms_bwd(x_mid, dgu, w_fin, i, norm_ffn[i:i + 1], dx, f"ffn_in_dx_norm_bwd{i}")
        g_norm_ffn[i] = jnp.sum(dg, axis=0)
        if i + 1 < depth:
            dxb = scatter(group(i + 1, "m"), 1, "start", scatter(group(i + 1, "m"), 0, "wait", dxb))
            dxb = scatter(group(i + 1, "f"), 1, "wait", dxb)
        dxb = pair_finish(group(i, "f"), dxb)
        if i % 2 == 0:
            dmix = _mm(dxb, w_cout, "nt", BF16, layer=j, tm=512, tn=1024, tk=4096, name=f"conv_out_dx{i}")
            g_cout[j] = _mm(mixed, dxb, "tn", BF16, tm=1024, tn=1024, tk=2048, name=f"conv_out_dw{i}")
            dpre, dwc = _convgate_bwd(pre, dmix, wc_conv[j], seq, f"conv_gate_bwd{i}")
            g_cconv[j] = dwc[:3]
            g_cin[j] = _mm(h, dpre, "tn", BF16, tm=1024, tn=1536, tk=2048, name=f"conv_in_dw{i}")
            w_pre = w_cin
        else:
            g_bo[j] = jnp.sum(colsum, axis=0)
            dmix = _mm(dxb, w_o, "nt", F32, layer=j, tm=512, tn=1024, tk=4096, name=f"attn_out_dx{i}")
            g_o[j] = _mm(mixed, dxb, "tn", BF16, tm=1024, tn=1024, tk=2048, name=f"attn_out_dw{i}")
            dpre, dbias, dsk = _attn_bwd(pre, mixed, lse, dmix, attn_sinks[j], cos_t, sin_t, bsz, seq, f"attn_bwd{i}",
                                         hp=GROUP)
            g_bqkv[j] = jnp.sum(dbias, axis=0)
            g_sinks[j] = jnp.sum(dsk, axis=1)
            g_qkv[j] = _mm(h, dpre, "tn", BF16, tm=1024, tn=1536, tk=2048, name=f"qkv_dw{i}")
            w_pre = w_qkv
        dpre = pair_begin(group(i, "m"), dpre)
        dx, dxb, dg, _ = _rms_bwd(x_in, dpre, w_pre, j, norm_mix[i:i + 1], dx, f"mixer_in_dx_norm_bwd{i}")
        g_norm_mix[i] = jnp.sum(dg, axis=0)
        dxb = scatter(group(i, "f"), 1, "start", scatter(group(i, "f"), 0, "wait", dxb))
        if i + 1 < depth:
            dxb = scatter(group(i + 1, "m"), 1, "wait", dxb)
        dxb = pair_finish(group(i, "m"), dxb)
    grad_x = dx.reshape(bsz, seq, d)

    dxb = scatter(group(0, "m"), 1, "start", scatter(group(0, "m"), 0, "wait", dxb))
    scatter(group(0, "m"), 1, "wait", scatter(group(0, "f"), 1, "wait", dxb))
    big_m = [m_conv_w_in, m_conv_w_out, m_attn_w_qkv, m_attn_w_o, m_ffn_w_in, m_ffn_w_down]
    big_v = [v_conv_w_in, v_conv_w_out, v_attn_w_qkv, v_attn_w_o, v_ffn_w_in, v_ffn_w_down]
    big_names = ["conv_w_in", "conv_w_out", "attn_w_qkv", "attn_w_o", "ffn_w_in", "ffn_w_down"]
    qc_arr = jnp.stack([q, cq]).astype(jnp.int32)
    big_upd = [_reduce_adamw(own[n], mine[n], sib[n], big_w[n], big_m[n], big_v[n], qc_arr, f"adamw_{nm}")
               for n, nm in enumerate(big_names)]

    small = [jnp.stack(g_norm_mix), jnp.stack(g_norm_ffn), jnp.sum(dg_final, axis=0), jnp.stack(g_cconv),
             jnp.stack(g_bqkv), jnp.stack(g_sinks), jnp.stack(g_bo), jnp.stack(g_fconv)]
    sg = _unpack(_allreduce_small(_pack(small), "grad_small_allreduce"), small)

    def my_cols(v, like):
        width = like.shape[-1]
        return lax.dynamic_slice_in_dim(v, q * width, width, axis=v.ndim - 1)

    small_w = [norm_mix, norm_ffn, norm_final, conv_w_conv, attn_b_qkv, attn_sinks, attn_b_o, ffn_w_conv]
    small_m = [m_norm_mix, m_norm_ffn, m_norm_final, m_conv_w_conv, m_attn_b_qkv, m_attn_sinks, m_attn_b_o, m_ffn_w_conv]
    small_v = [v_norm_mix, v_norm_ffn, v_norm_final, v_conv_w_conv, v_attn_b_qkv, v_attn_sinks, v_attn_b_o, v_ffn_w_conv]
    small_g = [sg[0], sg[1], sg[2], my_cols(sg[3], conv_w_conv), my_cols(sg[4], attn_b_qkv), sg[5],
               my_cols(sg[6], attn_b_o), my_cols(sg[7], ffn_w_conv)]

    upd = {nm: tuple(u[1:]) for nm, u in zip(big_names, big_upd)}
    sd, sm, sv = _adamw(_pack(small_w)[None], _pack(small_g)[None], _pack(small_m)[None], _pack(small_v, 1.0)[None],
                        "adamw_small")
    sd, sm, sv = _unpack(sd, small_w), _unpack(sm, small_w), _unpack(sv, small_w)
    names = ["norm_mix", "norm_ffn", "norm_final", "conv_w_in", "conv_w_conv", "conv_w_out", "attn_w_qkv", "attn_b_qkv",
             "attn_sinks", "attn_w_o", "attn_b_o", "ffn_w_in", "ffn_w_conv", "ffn_w_down"]
    small_names = ["norm_mix", "norm_ffn", "norm_final", "conv_w_conv", "attn_b_qkv", "attn_sinks", "attn_b_o", "ffn_w_conv"]
    grads = dict(zip(small_names, small_g))
    grads.update({nm: u[0] for nm, u in zip(big_names, big_upd)})
    for n, nm in enumerate(small_names):
        upd[nm] = (sd[n], sm[n], sv[n])
    return (loss, grad_x, *[grads[nm] for nm in names], *[upd[nm][0] for nm in names],
            *[upd[nm][1] for nm in names], *[upd[nm][2] for nm in names])
```

```python
import math

import jax
import jax.numpy as jnp
from jax import lax
from jax.experimental import pallas as pl
from jax.experimental.pallas import tpu as pltpu

F32 = jnp.float32
BF16 = jnp.bfloat16

HEAD_DIM = 64
GROUP = 4
WINDOW = 128
EPS = 1e-5
ROPE_THETA = 10000.0
ADAM_LR, ADAM_B1, ADAM_B2, ADAM_EPS, ADAM_WD, ADAM_STEP = 0.001, 0.9, 0.999, 1e-08, 0.01, 10

N_CHIPS = 4
N_CORES = 2
N_DEV = 8
HALO = 16
VMEM_LIMIT_BYTES = 56 * 1024 * 1024
MESH = pl.DeviceIdType.MESH
ANY = pl.BlockSpec(memory_space=pl.ANY)
SMEM = pl.BlockSpec(memory_space=pltpu.SMEM)
NEG = float(jnp.finfo(jnp.float32).min)
ROW_TILES = (512, 256, 128, 64, 32, 16, 8)


def _pick(dim, cands):
    for c in cands:
        if dim % c == 0:
            return c
    return dim


def _params(sem):
    return pltpu.CompilerParams(dimension_semantics=sem, vmem_limit_bytes=VMEM_LIMIT_BYTES)


_DIMS = {"nn": (((1,), (0,)), ((), ())), "nt": (((1,), (1,)), ((), ())), "tn": (((0,), (0,)), ((), ()))}


def _mm(a, b, mode, out_dtype, *, layer=None, bias=None, residual=None, norm_g=None, n_outer=False, tm, tn, tk, name):
    b2 = b.shape[1:] if layer is not None else b.shape
    if mode == "nn":
        (m, k), n = a.shape, b2[1]
    elif mode == "nt":
        (m, k), n = a.shape, b2[0]
    else:
        (k, m), n = a.shape, b2[1]
    tm, tn, tk = min(tm, m), min(tn, n), min(tk, k)
    assert m % tm == 0 and n % tn == 0 and k % tk == 0, (name, a.shape, b.shape, tm, tn, tk)
    nk = k // tk

    def at(f):
        return (lambda p0, p1, p2: f(p1, p0, p2)) if n_outer else f

    a_spec = pl.BlockSpec((tk, tm), at(lambda i, j, l: (l, i))) if mode == "tn" else pl.BlockSpec((tm, tk), at(lambda i, j, l: (i, l)))
    if layer is None:
        b_spec = (pl.BlockSpec((tn, tk), at(lambda i, j, l: (j, l))) if mode == "nt"
                  else pl.BlockSpec((tk, tn), at(lambda i, j, l: (l, j))))
    elif mode == "nt":
        b_spec = pl.BlockSpec((None, tn, tk), at(lambda i, j, l: (layer, j, l)))
    else:
        b_spec = pl.BlockSpec((None, tk, tn), at(lambda i, j, l: (layer, l, j)))
    in_specs, args = [a_spec, b_spec], [a, b]
    if bias is not None:
        in_specs.append(pl.BlockSpec((1, tn), at(lambda i, j, l: (0, j))))
        args.append(bias)
    if residual is not None:
        in_specs.append(pl.BlockSpec((tm, tn), at(lambda i, j, l: (i, j))))
        args.append(residual)
    if norm_g is not None:
        assert tn == n, (name, "the RMSNorm of the result needs whole rows in a tile")
        in_specs.append(pl.BlockSpec((1, tn), at(lambda i, j, l: (0, j))))
        args.append(norm_g)
    has_bias, has_res, has_norm = bias is not None, residual is not None, norm_g is not None

    def body(*refs):
        a_ref, b_ref = refs[0], refs[1]
        pos = 2
        bias_ref = res_ref = g_ref = h_ref = None
        if has_bias:
            bias_ref, pos = refs[pos], pos + 1
        if has_res:
            res_ref, pos = refs[pos], pos + 1
        if has_norm:
            g_ref, pos = refs[pos], pos + 1
        o_ref, pos = refs[pos], pos + 1
        if has_norm:
            h_ref, pos = refs[pos], pos + 1
        acc_ref = refs[pos] if nk > 1 else None

        def finish(acc):
            if has_bias:
                acc = acc + bias_ref[...]
            if has_res:
                acc = acc + res_ref[...]
            o_ref[...] = acc.astype(o_ref.dtype)
            if has_norm:
                h_ref[...] = _rms(acc, g_ref[...]).astype(BF16)

        if nk == 1:
            finish(lax.dot_general(a_ref[...], b_ref[...], _DIMS[mode], preferred_element_type=F32))
            return
        l = pl.program_id(2)
        part = lax.dot_general(a_ref[...], b_ref[...], _DIMS[mode], preferred_element_type=F32)

        @pl.when(l == 0)
        def _():
            acc_ref[...] = part

        @pl.when(l > 0)
        def _():
            acc_ref[...] += part

        @pl.when(l == nk - 1)
        def _():
            finish(acc_ref[...])

    o_spec = pl.BlockSpec((tm, tn), at(lambda i, j, l: (i, j)))
    o_shape = jax.ShapeDtypeStruct((m, n), out_dtype)
    return pl.pallas_call(
        body,
        name=name,
        grid=(n // tn, m // tm, nk) if n_outer else (m // tm, n // tn, nk),
        in_specs=in_specs,
        out_specs=[o_spec, o_spec] if has_norm else o_spec,
        out_shape=[o_shape, jax.ShapeDtypeStruct((m, n), BF16)] if has_norm else o_shape,
        scratch_shapes=[pltpu.VMEM((tm, tn), F32)] if nk > 1 else [],
        compiler_params=_params(("parallel", "parallel", "arbitrary")),
    )(*args)


def _rms(x, g):
    return x * lax.rsqrt(jnp.mean(x * x, axis=-1, keepdims=True) + EPS) * g


def _fold8(v):
    r, d = v.shape
    return jnp.sum(v.reshape(r // 8, 8, d), axis=0)


def _rms_fwd(x, g, name):
    t, d = x.shape
    tm = _pick(t, ROW_TILES)

    def body(x_ref, g_ref, h_ref):
        xv = x_ref[...]
        r = lax.rsqrt(jnp.mean(xv * xv, axis=-1, keepdims=True) + EPS)
        h_ref[...] = (xv * r * g_ref[...]).astype(BF16)

    return pl.pallas_call(
        body,
        name=name,
        grid=(t // tm,),
        in_specs=[pl.BlockSpec((tm, d), lambda i: (i, 0)), pl.BlockSpec((1, d), lambda i: (0, 0))],
        out_specs=pl.BlockSpec((tm, d), lambda i: (i, 0)),
        out_shape=jax.ShapeDtypeStruct((t, d), BF16),
        compiler_params=_params(("parallel",)),
    )(x, g)


def _rms_bwd(x, dpre, w, layer, g, dx_in, name):
    t, d = x.shape
    k = dpre.shape[1]
    tm = _pick(t, ROW_TILES)

    def body(x_ref, dp_ref, w_ref, g_ref, dxi_ref, dx_ref, dxb_ref, dg_ref, cs_ref):
        i = pl.program_id(0)
        xv = x_ref[...]
        r = lax.rsqrt(jnp.mean(xv * xv, axis=-1, keepdims=True) + EPS)
        xhat = xv * r
        dy = lax.dot_general(dp_ref[...], w_ref[...], _DIMS["nt"], preferred_element_type=F32)
        gdy = dy * g_ref[...]
        dx = dxi_ref[...] + r * (gdy - xhat * jnp.mean(gdy * xhat, axis=-1, keepdims=True))
        dx_ref[...] = dx
        dxb_ref[...] = dx.astype(BF16)

        @pl.when(i == 0)
        def _():
            dg_ref[...] = jnp.zeros_like(dg_ref)
            cs_ref[...] = jnp.zeros_like(cs_ref)

        dg_ref[...] += _fold8(dy * xhat)
        cs_ref[...] += _fold8(dx)

    row = pl.BlockSpec((tm, d), lambda i: (i, 0))
    acc = pl.BlockSpec((8, d), lambda i: (0, 0))
    w_spec = pl.BlockSpec((None, d, k), lambda i: (layer, 0, 0), pipeline_mode=pl.Buffered(1))
    return pl.pallas_call(
        body,
        name=name,
        grid=(t // tm,),
        in_specs=[row, pl.BlockSpec((tm, k), lambda i: (i, 0)), w_spec, pl.BlockSpec((1, d), lambda i: (0, 0)), row],
        out_specs=[row, row, acc, acc],
        out_shape=[jax.ShapeDtypeStruct((t, d), F32), jax.ShapeDtypeStruct((t, d), BF16),
                   jax.ShapeDtypeStruct((8, d), F32), jax.ShapeDtypeStruct((8, d), F32)],
        compiler_params=_params(("arbitrary",)),
    )(x, dpre, w, g, dx_in)


def _loss_head(x, target, g, name):
    t, d = x.shape
    tm = _pick(t, ROW_TILES)
    inv_d = 1.0 / d

    def body(x_ref, t_ref, g_ref, dx_ref, dxb_ref, sq_ref, dg_ref):
        i = pl.program_id(0)
        xv = x_ref[...]
        gv = g_ref[...]
        r = lax.rsqrt(jnp.mean(xv * xv, axis=-1, keepdims=True) + EPS)
        xhat = xv * r
        err = xhat * gv - t_ref[...]
        dy = err * inv_d
        gdy = dy * gv
        dx = r * (gdy - xhat * jnp.mean(gdy * xhat, axis=-1, keepdims=True))
        dx_ref[...] = dx
        dxb_ref[...] = dx.astype(BF16)

        @pl.when(i == 0)
        def _():
            sq_ref[...] = jnp.zeros_like(sq_ref)
            dg_ref[...] = jnp.zeros_like(dg_ref)

        sq_ref[...] += _fold8(err * err)
        dg_ref[...] += _fold8(dy * xhat)

    row = pl.BlockSpec((tm, d), lambda i: (i, 0))
    acc = pl.BlockSpec((8, d), lambda i: (0, 0))
    return pl.pallas_call(
        body,
        name=name,
        grid=(t // tm,),
        in_specs=[row, row, pl.BlockSpec((1, d), lambda i: (0, 0))],
        out_specs=[row, row, acc, acc],
        out_shape=[jax.ShapeDtypeStruct((t, d), F32), jax.ShapeDtypeStruct((t, d), BF16),
                   jax.ShapeDtypeStruct((8, d), F32), jax.ShapeDtypeStruct((8, d), F32)],
        compiler_params=_params(("arbitrary",)),
    )(x, target, g)


def _rows(tm):
    return lax.broadcasted_iota(jnp.int32, (tm, 1), 0)


def _shift_down(u, before2):
    r8 = _rows(8)
    s1, s2 = pltpu.roll(u, 1, 0), pltpu.roll(u, 2, 0)
    top1 = jnp.where(r8 == 0, before2[1:2], s1[:8])
    top2 = jnp.where(r8 == 0, before2[0:1], jnp.where(r8 == 1, before2[1:2], s2[:8]))
    return jnp.concatenate([top1, s1[8:]], axis=0), jnp.concatenate([top2, s2[8:]], axis=0)


def _shift_up(u, after2):
    tm = u.shape[0]
    r8 = _rows(8)
    s1, s2 = pltpu.roll(u, tm - 1, 0), pltpu.roll(u, tm - 2, 0)
    bot1 = jnp.where(r8 == 7, after2[0:1], s1[tm - 8:])
    bot2 = jnp.where(r8 == 6, after2[0:1], jnp.where(r8 == 7, after2[1:2], s2[tm - 8:]))
    return jnp.concatenate([s1[:tm - 8], bot1], axis=0), jnp.concatenate([s2[:tm - 8], bot2], axis=0)


def _shift_matrix(tm, up):
    r = lax.broadcasted_iota(jnp.int32, (2 * tm, tm), 0)
    c = lax.broadcasted_iota(jnp.int32, (2 * tm, tm), 1)
    t = jnp.where(r >= tm, r - tm, r)
    k = jnp.where(r >= tm, 2, 1)
    return (c == (t + k if up else t - k)).astype(BF16)


def _shift_down_mxu(u, before2):
    tm = u.shape[0]
    moved = jnp.dot(_shift_matrix(tm, False), u.astype(BF16), preferred_element_type=F32)
    r8 = _rows(8)
    s1, s2 = moved[:tm], moved[tm:]
    top1 = s1[:8] + jnp.where(r8 == 0, before2[1:2], 0.0)
    top2 = s2[:8] + jnp.where(r8 == 0, before2[0:1], jnp.where(r8 == 1, before2[1:2], 0.0))
    return jnp.concatenate([top1, s1[8:]], axis=0), jnp.concatenate([top2, s2[8:]], axis=0)


def _shift_up_mxu(u, after2):
    tm = u.shape[0]
    moved = jnp.dot(_shift_matrix(tm, True), u.astype(BF16), preferred_element_type=F32)
    r8 = _rows(8)
    s1, s2 = moved[:tm], moved[tm:]
    bot1 = s1[tm - 8:] + jnp.where(r8 == 7, after2[0:1], 0.0)
    bot2 = s2[tm - 8:] + jnp.where(r8 == 6, after2[0:1], jnp.where(r8 == 7, after2[1:2], 0.0))
    return jnp.concatenate([s1[:tm - 8], bot1], axis=0), jnp.concatenate([s2[:tm - 8], bot2], axis=0)


def _conv_tile(seq):
    return _pick(seq, (256, 128, 64, 32, 16, 8))


def _halo_specs(tm, width, n_tiles):
    per = tm // HALO
    before = pl.BlockSpec((HALO, width), lambda i: (jnp.maximum(i * per - 1, 0), 0))
    after = pl.BlockSpec((HALO, width), lambda i: (jnp.minimum((i + 1) * per, n_tiles * per - 1), 0))
    return before, after


def _convgate_fwd(bcv, w, seq, name):
    t, d3 = bcv.shape
    d = d3 // 3
    tm = _conv_tile(seq)
    tps = seq // tm
    before, _ = _halo_specs(tm, d3, t // tm)

    def body(x_ref, xb_ref, w_ref, y_ref):
        i = pl.program_id(0)
        inner = (i % tps != 0).astype(F32)
        u = x_ref[:, d:2 * d].astype(F32) * x_ref[:, 2 * d:].astype(F32)
        xb = xb_ref[:, d:].astype(F32)[HALO - 2:]
        s1, s2 = _shift_down(u, xb[:, :d] * xb[:, d:] * inner)
        z = w_ref[2:3] * u + w_ref[1:2] * s1 + w_ref[0:1] * s2
        y_ref[...] = (x_ref[:, :d].astype(F32) * z).astype(BF16)

    return pl.pallas_call(
        body,
        name=name,
        grid=(t // tm,),
        in_specs=[pl.BlockSpec((tm, d3), lambda i: (i, 0)), before, pl.BlockSpec((3, d), lambda i: (0, 0))],
        out_specs=pl.BlockSpec((tm, d), lambda i: (i, 0)),
        out_shape=jax.ShapeDtypeStruct((t, d), BF16),
        compiler_params=_params(("parallel",)),
    )(bcv, bcv, w)


def _convgate_bwd(bcv, dy, w, seq, name):
    t, d3 = bcv.shape
    d = d3 // 3
    tm = _conv_tile(seq)
    tps = seq // tm
    before, after = _halo_specs(tm, d3, t // tm)
    _, after_dy = _halo_specs(tm, d, t // tm)

    def body(x_ref, xb_ref, xa_ref, dy_ref, dya_ref, w_ref, dx_ref, dw_ref):
        i = pl.program_id(0)
        inner_lo = (i % tps != 0).astype(F32)
        inner_hi = (i % tps != tps - 1).astype(F32)
        w0, w1, w2 = w_ref[0:1], w_ref[1:2], w_ref[2:3]
        b, c, v = x_ref[:, :d].astype(F32), x_ref[:, d:2 * d].astype(F32), x_ref[:, 2 * d:].astype(F32)
        u = c * v
        xb = xb_ref[:, d:].astype(F32)[HALO - 2:]
        s1, s2 = _shift_down(u, xb[:, :d] * xb[:, d:] * inner_lo)
        z = w2 * u + w1 * s1 + w0 * s2
        dyv = dy_ref[...].astype(F32)
        dz = dyv * b
        dza = dya_ref[...].astype(F32)[0:2] * xa_ref[:, :d].astype(F32)[0:2] * inner_hi
        n1, n2 = _shift_up(dz, dza)
        du = w2 * dz + w1 * n1 + w0 * n2
        dx_ref[:, :d] = (dyv * z).astype(BF16)
        dx_ref[:, d:2 * d] = (du * v).astype(BF16)
        dx_ref[:, 2 * d:] = (du * c).astype(BF16)

        @pl.when(i == 0)
        def _():
            dw_ref[...] = jnp.zeros_like(dw_ref)

        dw_ref[0:1] += jnp.sum(dz * s2, axis=0, keepdims=True)
        dw_ref[1:2] += jnp.sum(dz * s1, axis=0, keepdims=True)
        dw_ref[2:3] += jnp.sum(dz * u, axis=0, keepdims=True)

    return pl.pallas_call(
        body,
        name=name,
        grid=(t // tm,),
        in_specs=[pl.BlockSpec((tm, d3), lambda i: (i, 0)), before, after,
                  pl.BlockSpec((tm, d), lambda i: (i, 0)), after_dy, pl.BlockSpec((3, d), lambda i: (0, 0))],
        out_specs=[pl.BlockSpec((tm, d3), lambda i: (i, 0)), pl.BlockSpec((8, d), lambda i: (0, 0))],
        out_shape=[jax.ShapeDtypeStruct((t, d3), BF16), jax.ShapeDtypeStruct((8, d), F32)],
        compiler_params=_params(("arbitrary",)),
    )(bcv, bcv, bcv, dy, dy, w)


def _sigmoid(x):
    return 1.0 / (1.0 + jnp.exp(-x))


def _ffn_gate_down_fwd(gu, w, w_down, layer, resid, norm_g, seq, name):
    t, f2 = gu.shape
    f = f2 // 2
    d = w_down.shape[2]
    sub = _conv_tile(seq)
    tm = _pick(seq, (2 * sub, sub))
    tps = seq // tm
    before, _ = _halo_specs(tm, f2, t // tm)

    def body(x_ref, xb_ref, w_ref, wd_ref, res_ref, g_ref, o_ref, a_ref, h_ref):
        i = pl.program_id(0)
        inner = (i % tps != 0).astype(F32)
        for r0 in range(0, tm, sub):
            rows = slice(r0, r0 + sub)
            if r0 == 0:
                halo = xb_ref[:, :f].astype(F32)[HALO - 2:] * inner
            else:
                halo = x_ref[r0 - HALO:r0, :f].astype(F32)[HALO - 2:]
            s1, s2 = _shift_down_mxu(x_ref[rows, :f], halo)
            gc = w_ref[2:3] * x_ref[rows, :f].astype(F32) + w_ref[1:2] * s1 + w_ref[0:1] * s2
            act = (gc * _sigmoid(gc) * x_ref[rows, f:].astype(F32)).astype(BF16)
            a_ref[rows, :] = act
            out = jnp.dot(act, wd_ref[...], preferred_element_type=F32) + res_ref[rows, :]
            o_ref[rows, :] = out
            h_ref[rows, :] = _rms(out, g_ref[...]).astype(BF16)

    row_d = pl.BlockSpec((tm, d), lambda i: (i, 0))
    return pl.pallas_call(
        body,
        name=name,
        grid=(t // tm,),
        in_specs=[pl.BlockSpec((tm, f2), lambda i: (i, 0)), before, pl.BlockSpec((3, f), lambda i: (0, 0)),
                  pl.BlockSpec((None, f, d), lambda i: (layer, 0, 0), pipeline_mode=pl.Buffered(1)),
                  row_d, pl.BlockSpec((1, d), lambda i: (0, 0))],
        out_specs=[row_d, pl.BlockSpec((tm, f), lambda i: (i, 0)), row_d],
        out_shape=[jax.ShapeDtypeStruct((t, d), F32), jax.ShapeDtypeStruct((t, f), BF16), jax.ShapeDtypeStruct((t, d), BF16)],
        compiler_params=_params(("parallel",)),
    )(gu, gu, w, w_down, resid, norm_g)


def _ffngate_bwd(gu, da, w, seq, name):
    t, f2 = gu.shape
    f = f2 // 2
    tm = _conv_tile(seq)
    tps = seq // tm
    before, after = _halo_specs(tm, f2, t // tm)
    _, after_da = _halo_specs(tm, f, t // tm)

    def body(x_ref, xb_ref, xa_ref, da_ref, daa_ref, w_ref, dx_ref, dw_ref):
        i = pl.program_id(0)
        inner_lo = (i % tps != 0).astype(F32)
        inner_hi = (i % tps != tps - 1).astype(F32)
        w0, w1, w2 = w_ref[0:1], w_ref[1:2], w_ref[2:3]

        def dgate(gc, uv, dav):
            sg = _sigmoid(gc)
            return dav * uv * (sg * (1.0 + gc * (1.0 - sg))), dav * (gc * sg)

        g, u = x_ref[:, :f].astype(F32), x_ref[:, f:].astype(F32)
        s1, s2 = _shift_down_mxu(x_ref[:, :f], xb_ref[:, :f].astype(F32)[HALO - 2:] * inner_lo)
        gc = w2 * g + w1 * s1 + w0 * s2
        dgc, du = dgate(gc, u, da_ref[...].astype(F32))
        ga = xa_ref[:, :f].astype(F32)
        a1, a2 = _shift_down(ga, x_ref[tm - HALO:, :f].astype(F32)[HALO - 2:])
        gca = w2 * ga + w1 * a1 + w0 * a2
        dgca, _ = dgate(gca, xa_ref[:, f:].astype(F32), daa_ref[...].astype(F32))
        n1, n2 = _shift_up_mxu(dgc, dgca[0:2] * inner_hi)
        dx_ref[:, :f] = (w2 * dgc + w1 * n1 + w0 * n2).astype(BF16)
        dx_ref[:, f:] = du.astype(BF16)

        @pl.when(i == 0)
        def _():
            dw_ref[...] = jnp.zeros_like(dw_ref)

        dw_ref[0:1] += jnp.sum(dgc * s2, axis=0, keepdims=True)
        dw_ref[1:2] += jnp.sum(dgc * s1, axis=0, keepdims=True)
        dw_ref[2:3] += jnp.sum(dgc * g, axis=0, keepdims=True)

    return pl.pallas_call(
        body,
        name=name,
        grid=(t // tm,),
        in_specs=[pl.BlockSpec((tm, f2), lambda i: (i, 0)), before, after,
                  pl.BlockSpec((tm, f), lambda i: (i, 0)), after_da, pl.BlockSpec((3, f), lambda i: (0, 0))],
        out_specs=[pl.BlockSpec((tm, f2), lambda i: (i, 0)), pl.BlockSpec((8, f), lambda i: (0, 0))],
        out_shape=[jax.ShapeDtypeStruct((t, f2), BF16), jax.ShapeDtypeStruct((8, f), F32)],
        compiler_params=_params(("arbitrary",)),
    )(gu, gu, gu, da, da, w)


def _swap_halves(xt):
    half = HEAD_DIM // 2
    return jnp.concatenate([xt[half:], xt[:half]], axis=0)


def _rope(xt, cos, sin):
    return xt * cos + _swap_halves(xt) * sin


def _unrope(dxt, cos, sin):
    return dxt * cos - _swap_halves(dxt) * sin


def _key_query(count):
    kj = lax.broadcasted_iota(jnp.int32, (WINDOW, count * WINDOW), 0)
    qi = lax.broadcasted_iota(jnp.int32, (WINDOW, count * WINDOW), 1) & (WINDOW - 1)
    return kj, qi


def _band_masks(n, count):
    kj, qi = _key_query(count)
    return kj <= qi, jnp.logical_and(kj > qi, n > 0)


def _lanes(v, count):
    return jnp.concatenate([v] * count, axis=1) if count > 1 else v


def _heads(ref, h0, count):
    parts = [ref[(h0 + g) * HEAD_DIM:(h0 + g + 1) * HEAD_DIM, :] for g in range(count)]
    return jnp.concatenate(parts, axis=1) if count > 1 else parts[0]


def _head_rows(ref, h0, count):
    parts = [ref[h0 + g:h0 + g + 1, :] for g in range(count)]
    return jnp.concatenate(parts, axis=1) if count > 1 else parts[0]


def _head_sinks(sink_ref, h0, count):
    parts = [jnp.full((1, WINDOW), sink_ref[h0 + g], F32) for g in range(count)]
    return jnp.concatenate(parts, axis=1) if count > 1 else parts[0]


def _tn(a, b):
    return lax.dot_general(a, b, _DIMS["tn"], preferred_element_type=F32)


def _nt(a, b):
    return lax.dot_general(a, b, _DIMS["nt"], preferred_element_type=F32)


def _nn(a, b):
    return jnp.dot(a, b, preferred_element_type=F32)


def _attn_fwd(qkv, sinks, cos_t, sin_t, bsz, seq, name, hp):
    t, qw = qkv.shape
    d = qw * 2 // 3
    kvw = d // GROUP
    n_heads, n_kv = d // HEAD_DIM, kvw // HEAD_DIM
    nb = seq // WINDOW
    scale = HEAD_DIM ** -0.5

    def body(sink_ref, xc_ref, xp_ref, cc_ref, sc_ref, cp_ref, sp_ref, o_ref, lse_ref, xt_ref, pt_ref, ot_ref):
        n = pl.program_id(1)
        xt_ref[...] = xc_ref[...].T
        pt_ref[...] = xp_ref[:, d:].T
        cos_c, sin_c, cos_p, sin_p = cc_ref[...], sc_ref[...], cp_ref[...], sp_ref[...]
        cos_g, sin_g = _lanes(cos_c, hp), _lanes(sin_c, hp)
        valid_c, valid_p = _band_masks(n, hp)
        for j in range(n_kv):
            ko = j * HEAD_DIM
            kc = _rope(xt_ref[d + ko:d + ko + HEAD_DIM, :], cos_c, sin_c).astype(BF16)
            kp = _rope(pt_ref[ko:ko + HEAD_DIM, :], cos_p, sin_p).astype(BF16)
            vc = xt_ref[d + kvw + ko:d + kvw + ko + HEAD_DIM, :].astype(BF16)
            vp = pt_ref[kvw + ko:kvw + ko + HEAD_DIM, :].astype(BF16)
            for h0 in range(j * GROUP, (j + 1) * GROUP, hp):
                q = _rope(_heads(xt_ref, h0, hp), cos_g, sin_g).astype(BF16)
                sink = _head_sinks(sink_ref, h0, hp)
                s_c = jnp.where(valid_c, _tn(kc, q) * scale, NEG)
                s_p = jnp.where(valid_p, _tn(kp, q) * scale, NEG)
                m = jnp.maximum(jnp.maximum(jnp.max(s_c, axis=0, keepdims=True), jnp.max(s_p, axis=0, keepdims=True)), sink)
                p_c = jnp.exp(s_c - m)
                p_p = jnp.exp(s_p - m)
                den = jnp.sum(p_c, axis=0, keepdims=True) + jnp.sum(p_p, axis=0, keepdims=True) + jnp.exp(sink - m)
                inv = 1.0 / den
                o_g = _nn(vc, (p_c * inv).astype(BF16)) + _nn(vp, (p_p * inv).astype(BF16))
                lse_g = m + jnp.log(den)
                for g in range(hp):
                    h = h0 + g
                    ot_ref[h * HEAD_DIM:(h + 1) * HEAD_DIM, :] = o_g[:, g * WINDOW:(g + 1) * WINDOW]
                    lse_ref[h:h + 1, :] = lse_g[:, g * WINDOW:(g + 1) * WINDOW]
        o_ref[...] = ot_ref[...].T.astype(BF16)

    cur = lambda b, n: (b * nb + n, 0)
    prev = lambda b, n: (b * nb + jnp.maximum(n - 1, 0), 0)
    tab_c = pl.BlockSpec((HEAD_DIM, WINDOW), lambda b, n: (0, n))
    tab_p = pl.BlockSpec((HEAD_DIM, WINDOW), lambda b, n: (0, jnp.maximum(n - 1, 0)))
    return pl.pallas_call(
        body,
        name=name,
        grid=(bsz, nb),
        in_specs=[SMEM, pl.BlockSpec((WINDOW, qw), cur), pl.BlockSpec((WINDOW, qw), prev), tab_c, tab_c, tab_p, tab_p],
        out_specs=[pl.BlockSpec((WINDOW, d), cur), pl.BlockSpec((n_heads, WINDOW), lambda b, n: (0, b * nb + n))],
        out_shape=[jax.ShapeDtypeStruct((t, d), BF16), jax.ShapeDtypeStruct((n_heads, t), F32)],
        scratch_shapes=[pltpu.VMEM((qw, WINDOW), F32), pltpu.VMEM((2 * kvw, WINDOW), F32), pltpu.VMEM((d, WINDOW), F32)],
        compiler_params=_params(("parallel", "arbitrary")),
    )(sinks, qkv, qkv, cos_t, sin_t, cos_t, sin_t)


def _attn_bwd(qkv, o, lse, do, sinks, cos_t, sin_t, bsz, seq, name, hp):
    t, qw = qkv.shape
    d = qw * 2 // 3
    kvw = d // GROUP
    n_heads, n_kv = d // HEAD_DIM, kvw // HEAD_DIM
    nb = seq // WINDOW
    scale = HEAD_DIM ** -0.5

    def body(sink_ref, xc_ref, xp_ref, oc_ref, doc_ref, lc_ref, cc_ref, sc_ref, cp_ref, sp_ref,
             dx_ref, db_ref, dsk_ref, xt_ref, pt_ref, otc_ref, dtc_ref, gt_ref, carry_ref):
        b, n = pl.program_id(0), pl.program_id(1)
        live = n < nb
        xt_ref[...] = xc_ref[...].T
        pt_ref[...] = xp_ref[:, d:].T
        otc_ref[...] = oc_ref[...].astype(F32).T
        dtc_ref[...] = doc_ref[...].T
        cos_c, sin_c, cos_p, sin_p = cc_ref[...], sc_ref[...], cp_ref[...], sp_ref[...]
        cos_g, sin_g = _lanes(cos_c, hp), _lanes(sin_c, hp)
        kj, qi = _key_query(hp)
        valid_c = jnp.logical_and(kj <= qi, live)
        valid_p = jnp.logical_and(kj > qi, jnp.logical_and(n > 0, live))

        @pl.when(jnp.logical_and(b == 0, n == 0))
        def _():
            db_ref[...] = jnp.zeros_like(db_ref)
            dsk_ref[...] = jnp.zeros_like(dsk_ref)

        @pl.when(n == 0)
        def _():
            carry_ref[...] = jnp.zeros_like(carry_ref)

        for j in range(n_kv):
            ko = j * HEAD_DIM
            k_rows = slice(d + ko, d + ko + HEAD_DIM)
            v_rows = slice(d + kvw + ko, d + kvw + ko + HEAD_DIM)
            kc = _rope(xt_ref[k_rows, :], cos_c, sin_c).astype(BF16)
            kp = _rope(pt_ref[ko:ko + HEAD_DIM, :], cos_p, sin_p).astype(BF16)
            vc = xt_ref[v_rows, :].astype(BF16)
            vp = pt_ref[kvw + ko:kvw + ko + HEAD_DIM, :].astype(BF16)
            dk_c = jnp.zeros((HEAD_DIM, WINDOW), F32)
            dv_c = jnp.zeros((HEAD_DIM, WINDOW), F32)
            dk_p = jnp.zeros((HEAD_DIM, WINDOW), F32)
            dv_p = jnp.zeros((HEAD_DIM, WINDOW), F32)
            for h0 in range(j * GROUP, (j + 1) * GROUP, hp):
                q = _rope(_heads(xt_ref, h0, hp), cos_g, sin_g).astype(BF16)
                do_g = _heads(dtc_ref, h0, hp)
                do_b = do_g.astype(BF16)
                lse_g = _head_rows(lc_ref, h0, hp)
                delta = jnp.sum(_heads(otc_ref, h0, hp) * do_g, axis=0, keepdims=True)
                p_c = jnp.exp(jnp.where(valid_c, _tn(kc, q) * scale, NEG) - lse_g)
                p_p = jnp.exp(jnp.where(valid_p, _tn(kp, q) * scale, NEG) - lse_g)
                ds_c = (p_c * (_tn(vc, do_b) - delta)).astype(BF16)
                ds_p = (p_p * (_tn(vp, do_b) - delta)).astype(BF16)
                dq = _unrope((_nn(kc, ds_c) + _nn(kp, ds_p)) * scale, cos_g, sin_g)
                dsk = jnp.where(live, -jnp.exp(_head_sinks(sink_ref, h0, hp) - lse_g) * delta, 0.0)
                for g in range(hp):
                    rows = slice((h0 + g) * HEAD_DIM, (h0 + g + 1) * HEAD_DIM)
                    gt_ref[rows, :] = carry_ref[rows, :]
                    carry_ref[rows, :] = dq[:, g * WINDOW:(g + 1) * WINDOW]
                    dsk_ref[h0 + g:h0 + g + 1, :] += dsk[:, g * WINDOW:(g + 1) * WINDOW]
                dv_c += _nt(do_b, p_c.astype(BF16))
                dk_c += _nt(q, ds_c)
                dv_p += _nt(do_b, p_p.astype(BF16))
                dk_p += _nt(q, ds_p)
            gt_ref[k_rows, :] = _unrope((carry_ref[k_rows, :] + dk_p) * scale, cos_p, sin_p)
            gt_ref[v_rows, :] = carry_ref[v_rows, :] + dv_p
            carry_ref[k_rows, :] = dk_c
            carry_ref[v_rows, :] = dv_c
        dx = gt_ref[...].T
        dx_ref[...] = dx.astype(BF16)
        db_ref[...] += _fold8(dx)

    cur = lambda b, n: (b * nb + jnp.minimum(n, nb - 1), 0)
    prev = lambda b, n: (b * nb + jnp.maximum(jnp.minimum(n, nb - 1) - 1, 0), 0)
    done = lambda b, n: (b * nb + jnp.maximum(n - 1, 0), 0)
    stat_c = pl.BlockSpec((n_heads, WINDOW), lambda b, n: (0, b * nb + jnp.minimum(n, nb - 1)))
    tab_c = pl.BlockSpec((HEAD_DIM, WINDOW), lambda b, n: (0, jnp.minimum(n, nb - 1)))
    tab_p = pl.BlockSpec((HEAD_DIM, WINDOW), lambda b, n: (0, jnp.maximum(n - 1, 0)))
    return pl.pallas_call(
        body,
        name=name,
        grid=(bsz, nb + 1),
        in_specs=[SMEM, pl.BlockSpec((WINDOW, qw), cur), pl.BlockSpec((WINDOW, qw), prev),
                  pl.BlockSpec((WINDOW, d), cur), pl.BlockSpec((WINDOW, d), cur), stat_c, tab_c, tab_c, tab_p, tab_p],
        out_specs=[pl.BlockSpec((WINDOW, qw), done), pl.BlockSpec((8, qw), lambda b, n: (0, 0)),
                   pl.BlockSpec((n_heads, WINDOW), lambda b, n: (0, 0))],
        out_shape=[jax.ShapeDtypeStruct((t, qw), BF16), jax.ShapeDtypeStruct((8, qw), F32),
                   jax.ShapeDtypeStruct((n_heads, WINDOW), F32)],
        scratch_shapes=[pltpu.VMEM((qw, WINDOW), F32), pltpu.VMEM((2 * kvw, WINDOW), F32), pltpu.VMEM((d, WINDOW), F32),
                        pltpu.VMEM((d, WINDOW), F32), pltpu.VMEM((qw, WINDOW), F32), pltpu.VMEM((qw, WINDOW), F32)],
        compiler_params=_params(("arbitrary", "arbitrary")),
    )(sinks, qkv, qkv, o, do, lse, cos_t, sin_t, cos_t, sin_t)


def _place():
    return lax.axis_index("x"), lax.axis_index("y"), lax.axis_index("c")


def _other_chips(x, y):
    return [(1 - x, y), (x, 1 - y), (1 - x, 1 - y)]


def _place_shard(w, axis, q, name):
    ly, k, n = w.shape
    tr = _pick(k, (256, 128, 64, 32, 16, 8))
    steps = k // tr
    shape = (ly, k * N_CHIPS, n) if axis == 1 else (ly, k, n * N_CHIPS)
    if axis == 1:
        out_spec = pl.BlockSpec((None, tr, n), lambda l, i, q_ref: (l, q_ref[0] * steps + i, 0))
    else:
        out_spec = pl.BlockSpec((None, tr, n), lambda l, i, q_ref: (l, i, q_ref[0]))

    def body(q_ref, w_ref, o_ref):
        del q_ref
        o_ref[...] = w_ref[...].astype(BF16)

    return pl.pallas_call(
        body,
        name=name,
        grid_spec=pltpu.PrefetchScalarGridSpec(
            num_scalar_prefetch=1, grid=(ly, steps),
            in_specs=[pl.BlockSpec((None, tr, n), lambda l, i, q_ref: (l, i, 0))], out_specs=out_spec),
        out_shape=jax.ShapeDtypeStruct(shape, BF16),
        compiler_params=_params(("parallel", "parallel")),
    )(q, w)


def _half_block(ref, axis, layer, px, py, pc):
    blk = 2 * px + py
    if axis == 1:
        rows = ref.shape[1] // (2 * N_CHIPS)
        return ref.at[layer, pl.ds(pl.multiple_of((2 * blk + pc) * rows, 8), rows), :]
    rows, width = ref.shape[1] // 2, ref.shape[2] // N_CHIPS
    return ref.at[layer, pl.ds(pl.multiple_of(pc * rows, 8), rows), pl.ds(pl.multiple_of(blk * width, 128), width)]


def _gather_copy(refs, axes, pieces, send_sems, recv_sems, p, k, stage, whose):
    x, y, c = _place()
    chip = _other_chips(x, y)[k]
    i, layer = pieces[p]
    if stage == 0:
        origin = (x, y, c) if whose == "mine" else (*chip, c)
        to = (*chip, c)
    else:
        origin = (*chip, c) if whose == "mine" else (*chip, 1 - c)
        to = (x, y, 1 - c)
    blk = _half_block(refs[i], axes[i], layer, *origin)
    return pltpu.make_async_remote_copy(src_ref=blk, dst_ref=blk, send_sem=send_sems.at[p * 3 + k],
                                        recv_sem=recv_sems.at[p * 3 + k], device_id=to, device_id_type=MESH)


def _gather_weights(fulls, axes, pieces, name):
    n, m = len(fulls), 3 * len(pieces)

    def body(*refs):
        dst = refs[n:2 * n]
        sems = refs[2 * n:]
        todo = [(p, k) for p in range(len(pieces)) for k in range(3)]
        sends = [_gather_copy(dst, axes, pieces, sems[0], sems[1], p, k, 0, "mine") for p, k in todo]
        for cp in sends:
            cp.start()
        for p, k in todo:
            _gather_copy(dst, axes, pieces, sems[0], sems[1], p, k, 0, "theirs").wait_recv()
            sends.append(_gather_copy(dst, axes, pieces, sems[2], sems[3], p, k, 1, "mine"))
            sends[-1].start()
        for p, k in todo:
            _gather_copy(dst, axes, pieces, sems[2], sems[3], p, k, 1, "theirs").wait_recv()
        for cp in sends:
            cp.wait_send()

    return pl.pallas_call(
        body,
        name=name,
        in_specs=[ANY] * n,
        out_specs=[ANY] * n,
        out_shape=[jax.ShapeDtypeStruct(f.shape, f.dtype) for f in fulls],
        input_output_aliases={i: i for i in range(n)},
        scratch_shapes=[pltpu.SemaphoreType.DMA((m,))] * 4,
    )(*fulls)


HBM_SPEC = pl.BlockSpec(memory_space=pltpu.HBM)
SEM_SPEC = pl.BlockSpec(memory_space=pltpu.SEMAPHORE)


def _gather_start(fulls, axes, pieces, stage, name):
    n, m = len(fulls), 3 * len(pieces)

    def body(*refs):
        src = refs[:n]
        send_sems, recv_sems = refs[2 * n], refs[2 * n + 1]
        for p in range(len(pieces)):
            for k in range(3):
                _gather_copy(src, axes, pieces, send_sems, recv_sems, p, k, stage, "mine").start()

    out = pl.pallas_call(
        body,
        name=name,
        in_specs=[HBM_SPEC] * n,
        out_specs=[HBM_SPEC] * n + [SEM_SPEC, SEM_SPEC],
        out_shape=[pltpu.HBM(f.shape, f.dtype) for f in fulls] + [pltpu.SemaphoreType.DMA((m,)), pltpu.SemaphoreType.DMA((m,))],
        input_output_aliases={i: i for i in range(n)},
        compiler_params=pltpu.CompilerParams(has_side_effects=pltpu.SideEffectType.DATAFLOW_SIDE_EFFECTING),
    )(*[pltpu.with_memory_space_constraint(f, pltpu.HBM) for f in fulls])
    return list(out[:n]), out[n], out[n + 1]


def _gather_wait(fulls, send_sems, recv_sems, after, axes, pieces, stage, name):
    n = len(fulls)

    def body(*refs):
        src = refs[:n]
        s_sems, r_sems = refs[n], refs[n + 1]
        for p in range(len(pieces)):
            for k in range(3):
                _gather_copy(src, axes, pieces, s_sems, r_sems, p, k, stage, "mine").wait_send()
                _gather_copy(src, axes, pieces, s_sems, r_sems, p, k, stage, "theirs").wait_recv()

    out = pl.pallas_call(
        body,
        name=name,
        in_specs=[HBM_SPEC] * n + [SEM_SPEC, SEM_SPEC, ANY],
        out_specs=[HBM_SPEC] * n,
        out_shape=[pltpu.HBM(f.shape, f.dtype) for f in fulls],
        input_output_aliases={i: i for i in range(n)},
        compiler_params=pltpu.CompilerParams(has_side_effects=pltpu.SideEffectType.DATAFLOW_SIDE_EFFECTING),
    )(*fulls, send_sems, recv_sems, after)
    return list(out)


def _half_shape(kind, shape):
    if kind == "col":
        return (shape[0] // 2, shape[1])
    return (N_CHIPS, shape[1] // 2, shape[2])


def _half_of(kind, ref, h):
    if kind == "col":
        r = ref.shape[0] // 2
        return ref.at[pl.ds(pl.multiple_of(h * r, 8), r), :]
    r = ref.shape[1] // 2
    return ref.at[:, pl.ds(pl.multiple_of(h * r, 8), r), :]


def _pair_copy(src, dst, kinds, send_sems, recv_sems, i):
    x, y, c = _place()
    return pltpu.make_async_remote_copy(src_ref=_half_of(kinds[i], src[i], 1 - c), dst_ref=dst[i], send_sem=send_sems.at[i],
                                        recv_sem=recv_sems.at[i], device_id=(x, y, 1 - c), device_id_type=MESH)


def _pair_start(grads, kinds, token, name):
    n = len(grads)
    lands = [pltpu.HBM(_half_shape(kd, g.shape), g.dtype) for g, kd in zip(grads, kinds)]

    def body(*refs):
        src, dst = refs[:n], refs[2 * n + 2:3 * n + 2]
        send_sems, recv_sems = refs[3 * n + 2], refs[3 * n + 3]
        for i in range(n):
            _pair_copy(src, dst, kinds, send_sems, recv_sems, i).start()

    arrays = list(grads) + [token]
    out = pl.pallas_call(
        body,
        name=name,
        in_specs=[HBM_SPEC] * (n + 1),
        out_specs=[HBM_SPEC] * (2 * n + 1) + [SEM_SPEC, SEM_SPEC],
        out_shape=[pltpu.HBM(a.shape, a.dtype) for a in arrays] + lands + [pltpu.SemaphoreType.DMA((n,)), pltpu.SemaphoreType.DMA((n,))],
        input_output_aliases={i: i for i in range(n + 1)},
        compiler_params=pltpu.CompilerParams(has_side_effects=pltpu.SideEffectType.DATAFLOW_SIDE_EFFECTING),
    )(*[pltpu.with_memory_space_constraint(a, pltpu.HBM) for a in arrays])
    return list(out[:n]), out[n], list(out[n + 1:2 * n + 1]), out[2 * n + 1], out[2 * n + 2]


def _pair_wait(grads, kinds, token, lands, send_sems, recv_sems, name):
    n = len(grads)

    def body(*refs):
        src, dst = refs[:n], refs[n + 1:2 * n + 1]
        s_sems, r_sems = refs[2 * n + 1], refs[2 * n + 2]
        for i in range(n):
            cp = _pair_copy(src, dst, kinds, s_sems, r_sems, i)
            cp.wait_send()
            cp.wait_recv()

    arrays = list(grads) + [token] + list(lands)
    out = pl.pallas_call(
        body,
        name=name,
        in_specs=[HBM_SPEC] * (2 * n + 1) + [SEM_SPEC, SEM_SPEC],
        out_specs=[HBM_SPEC] * (2 * n + 1),
        out_shape=[pltpu.HBM(a.shape, a.dtype) for a in arrays],
        input_output_aliases={i: i for i in range(2 * n + 1)},
        compiler_params=pltpu.CompilerParams(has_side_effects=pltpu.SideEffectType.DATAFLOW_SIDE_EFFECTING),
    )(*arrays, send_sems, recv_sems)
    return list(out[:n]), out[n], list(out[n + 1:])


def _pair_sum(grad, recv, kind, c, own, layer, name):
    r, cols = own.shape[2:]
    if kind == "col":
        tr = _pick(r, (256, 128, 64, 32, 16, 8))
        steps = r // tr
        grid = (N_CHIPS, steps)
        g_spec = pl.BlockSpec((tr, cols), lambda s, i, c_ref: (c_ref[0] * steps + i, s))
        r_spec = pl.BlockSpec((tr, cols), lambda s, i, c_ref: (i, s))
        o_spec = pl.BlockSpec((None, None, tr, cols), lambda s, i, c_ref: (layer, s, i, 0))
        g_in = grad
    else:
        grid = (N_CHIPS, 1)
        g_spec = pl.BlockSpec((None, None, r, cols), lambda s, i, c_ref: (s, c_ref[0], 0, 0))
        r_spec = pl.BlockSpec((None, r, cols), lambda s, i, c_ref: (s, 0, 0))
        o_spec = pl.BlockSpec((None, None, r, cols), lambda s, i, c_ref: (layer, s, 0, 0))
        g_in = grad.reshape(N_CHIPS, 2, r, cols)

    def body(c_ref, g_ref, r_ref, own_ref, o_ref):
        del c_ref, own_ref
        o_ref[...] = (g_ref[...].astype(F32) + r_ref[...].astype(F32)).astype(o_ref.dtype)

    return pl.pallas_call(
        body,
        name=name,
        grid_spec=pltpu.PrefetchScalarGridSpec(num_scalar_prefetch=1, grid=grid, in_specs=[g_spec, r_spec, ANY], out_specs=o_spec),
        out_shape=jax.ShapeDtypeStruct(own.shape, own.dtype),
        input_output_aliases={3: 0},
        compiler_params=_params(("parallel", "parallel")),
    )(c, g_in, recv, own)


def _scatter_copy(own, mine, sib, pieces, send_sems, recv_sems, p, k, stage, whose):
    x, y, c = _place()
    q = 2 * x + y
    i, layer = pieces[p]
    per = 4 if stage == 0 else 3
    if k == 3:
        src, dst, to = own[i].at[layer, q], sib[i].at[layer, q], (x, y, 1 - c)
    else:
        chip = _other_chips(x, y)[k]
        slot = 2 * chip[0] + chip[1]
        if stage == 0:
            to = (*chip, c)
            src, dst = (own[i].at[layer, slot], mine[i].at[layer, q]) if whose == "mine" else (own[i].at[layer, q], mine[i].at[layer, slot])
        else:
            to = (x, y, 1 - c)
            src, dst = mine[i].at[layer, slot], sib[i].at[layer, slot]
    return pltpu.make_async_remote_copy(src_ref=src, dst_ref=dst, send_sem=send_sems.at[p * per + k],
                                        recv_sem=recv_sems.at[p * per + k], device_id=to, device_id_type=MESH)


def _scatter_start(own, mine, sib, token, pieces, stage, name):
    n = len(own)
    per = 4 if stage == 0 else 3
    m = per * len(pieces)
    n_arr = 3 * n + 1

    def body(*refs):
        o, mi, si = refs[:n], refs[n:2 * n], refs[2 * n:3 * n]
        send_sems, recv_sems = refs[2 * n_arr], refs[2 * n_arr + 1]
        for p in range(len(pieces)):
            for k in range(per):
                _scatter_copy(o, mi, si, pieces, send_sems, recv_sems, p, k, stage, "mine").start()

    arrays = list(own) + list(mine) + list(sib) + [token]
    out = pl.pallas_call(
        body,
        name=name,
        in_specs=[HBM_SPEC] * n_arr,
        out_specs=[HBM_SPEC] * n_arr + [SEM_SPEC, SEM_SPEC],
        out_shape=[pltpu.HBM(a.shape, a.dtype) for a in arrays] + [pltpu.SemaphoreType.DMA((m,)), pltpu.SemaphoreType.DMA((m,))],
        input_output_aliases={i: i for i in range(n_arr)},
        compiler_params=pltpu.CompilerParams(has_side_effects=pltpu.SideEffectType.DATAFLOW_SIDE_EFFECTING),
    )(*[pltpu.with_memory_space_constraint(a, pltpu.HBM) for a in arrays])
    return list(out[:n]), list(out[n:2 * n]), list(out[2 * n:3 * n]), out[3 * n], out[n_arr], out[n_arr + 1]


def _scatter_wait(own, mine, sib, token, send_sems, recv_sems, pieces, stage, name):
    n = len(own)
    per = 4 if stage == 0 else 3
    n_arr = 3 * n + 1

    def body(*refs):
        o, mi, si = refs[:n], refs[n:2 * n], refs[2 * n:3 * n]
        s_sems, r_sems = refs[n_arr], refs[n_arr + 1]
        for p in range(len(pieces)):
            for k in range(per):
                _scatter_copy(o, mi, si, pieces, s_sems, r_sems, p, k, stage, "mine").wait_send()
                _scatter_copy(o, mi, si, pieces, s_sems, r_sems, p, k, stage, "theirs").wait_recv()

    arrays = list(own) + list(mine) + list(sib) + [token]
    out = pl.pallas_call(
        body,
        name=name,
        in_specs=[HBM_SPEC] * n_arr + [SEM_SPEC, SEM_SPEC],
        out_specs=[HBM_SPEC] * n_arr,
        out_shape=[pltpu.HBM(a.shape, a.dtype) for a in arrays],
        input_output_aliases={i: i for i in range(n_arr)},
        compiler_params=pltpu.CompilerParams(has_side_effects=pltpu.SideEffectType.DATAFLOW_SIDE_EFFECTING),
    )(*arrays, send_sems, recv_sems)
    return list(out[:n]), list(out[n:2 * n]), list(out[2 * n:3 * n]), out[3 * n]


def _reduce_adamw(own, mine, sib, w, m, v, qc, name):
    ly, _, r, cols = mine.shape
    tr = _pick(r, (128, 64, 32, 16, 8))
    steps = r // tr
    c1 = 1.0 - ADAM_B1 ** ADAM_STEP
    c2 = 1.0 - ADAM_B2 ** ADAM_STEP

    def body(qc_ref, own_ref, mine_ref, sib_ref, w_ref, m_ref, v_ref, g_ref, d_ref, nm_ref, nv_ref):
        q = qc_ref[0]
        mine_sum = sib_sum = None
        for s in range(N_CHIPS):
            a = jnp.where(q == s, own_ref[...], mine_ref[s]).astype(F32)
            b = sib_ref[s].astype(F32)
            mine_sum = a if s == 0 else mine_sum + a
            sib_sum = b if s == 0 else sib_sum + b
        gv = jnp.where(pl.program_id(1) == qc_ref[1], mine_sum, sib_sum)
        nm = ADAM_B1 * m_ref[...] + (1.0 - ADAM_B1) * gv
        nv = ADAM_B2 * v_ref[...] + (1.0 - ADAM_B2) * (gv * gv)
        g_ref[...] = gv
        d_ref[...] = -ADAM_LR * ((nm / c1) / (jnp.sqrt(nv / c2) + ADAM_EPS) + ADAM_WD * w_ref[...])
        nm_ref[...] = nm
        nv_ref[...] = nv

    def mine_rows(h, i, qc_ref):
        return jnp.where(h == qc_ref[1], i, 0)

    def sib_rows(h, i, qc_ref):
        return jnp.where(h == qc_ref[1], 0, i)

    own_spec = pl.BlockSpec((None, None, tr, cols), lambda l, h, i, qc_ref: (l, qc_ref[0], mine_rows(h, i, qc_ref), 0))
    mine_spec = pl.BlockSpec((None, N_CHIPS, tr, cols), lambda l, h, i, qc_ref: (l, 0, mine_rows(h, i, qc_ref), 0))
    sib_spec = pl.BlockSpec((None, N_CHIPS, tr, cols), lambda l, h, i, qc_ref: (l, 0, sib_rows(h, i, qc_ref), 0))
    spec = pl.BlockSpec((None, tr, cols), lambda l, h, i, qc_ref: (l, h * steps + i, 0))
    shp = jax.ShapeDtypeStruct(w.shape, F32)
    return pl.pallas_call(
        body,
        name=name,
        grid_spec=pltpu.PrefetchScalarGridSpec(
            num_scalar_prefetch=1, grid=(ly, N_CORES, steps),
            in_specs=[own_spec, mine_spec, sib_spec, spec, spec, spec], out_specs=[spec] * 4),
        out_shape=[shp] * 4,
        compiler_params=_params(("parallel", "parallel", "parallel")),
    )(qc, own, mine, sib, w, m, v)


def _allreduce_small(v, name):
    r, w = v.shape

    def body(v_ref, o_ref, buf_ref, send_sems, recv_sems):
        x, y, c = _place()
        me = 4 * x + 2 * y + c

        def peer(k):
            return x ^ (k >> 2), y ^ ((k >> 1) & 1), c ^ (k & 1)

        def remote(k, slot):
            return pltpu.make_async_remote_copy(
                src_ref=v_ref, dst_ref=buf_ref.at[slot], send_sem=send_sems.at[k - 1], recv_sem=recv_sems.at[k - 1],
                device_id=peer(k), device_id_type=MESH)

        sends = [remote(k, me) for k in range(1, N_DEV)]
        for cp in sends:
            cp.start()
        buf_ref[me] = v_ref[...]
        for k in range(1, N_DEV):
            px, py, pc = peer(k)
            remote(k, 4 * px + 2 * py + pc).wait_recv()
        for cp in sends:
            cp.wait_send()
        acc = buf_ref[0]
        for dev in range(1, N_DEV):
            acc = acc + buf_ref[dev]
        o_ref[...] = acc

    vm = pl.BlockSpec(memory_space=pltpu.VMEM)
    return pl.pallas_call(
        body,
        name=name,
        in_specs=[vm],
        out_specs=vm,
        out_shape=jax.ShapeDtypeStruct((r, w), F32),
        scratch_shapes=[pltpu.VMEM((N_DEV, r, w), F32), pltpu.SemaphoreType.DMA((N_DEV - 1,)), pltpu.SemaphoreType.DMA((N_DEV - 1,))],
        compiler_params=pltpu.CompilerParams(vmem_limit_bytes=VMEM_LIMIT_BYTES),
    )(v)


def _adamw(w, g, m, v, name):
    ly, r, c = w.shape
    tr = _pick(r, (256, 128, 64, 32, 16, 8))
    c1 = 1.0 - ADAM_B1 ** ADAM_STEP
    c2 = 1.0 - ADAM_B2 ** ADAM_STEP

    def body(w_ref, g_ref, m_ref, v_ref, d_ref, nm_ref, nv_ref):
        gv = g_ref[...]
        nm = ADAM_B1 * m_ref[...] + (1.0 - ADAM_B1) * gv
        nv = ADAM_B2 * v_ref[...] + (1.0 - ADAM_B2) * (gv * gv)
        d_ref[...] = -ADAM_LR * ((nm / c1) / (jnp.sqrt(nv / c2) + ADAM_EPS) + ADAM_WD * w_ref[...])
        nm_ref[...] = nm
        nv_ref[...] = nv

    spec = pl.BlockSpec((None, tr, c), lambda l, i: (l, i, 0))
    shp = jax.ShapeDtypeStruct((ly, r, c), F32)
    return pl.pallas_call(
        body,
        name=name,
        grid=(ly, r // tr),
        in_specs=[spec] * 4,
        out_specs=[spec] * 3,
        out_shape=[shp] * 3,
        compiler_params=_params(("parallel", "parallel")),
    )(w, g, m, v)


def _rope_tables(seq):
    pos = jnp.arange(seq, dtype=F32)
    inv_freq = 1.0 / (ROPE_THETA ** (jnp.arange(0, HEAD_DIM, 2, dtype=F32) / HEAD_DIM))
    ang = (pos[:, None] * inv_freq[None, :]).T
    cos, sin = jnp.cos(ang), jnp.sin(ang)
    return jnp.concatenate([cos, cos], axis=0), jnp.concatenate([-sin, sin], axis=0)


def _pack(vs, fill=0.0):
    p = jnp.concatenate([v.reshape(-1) for v in vs])
    size = -(-p.shape[0] // 8192) * 8192
    return jnp.pad(p, (0, size - p.shape[0]), constant_values=fill).reshape(-1, 1024)


def _unpack(p, like):
    p = p.reshape(-1)
    out, o = [], 0
    for v in like:
        n = int(math.prod(v.shape))
        out.append(p[o:o + n].reshape(v.shape))
        o += n
    return out


def kernel(x, norm_mix, norm_ffn, norm_final, conv_w_in, conv_w_conv, conv_w_out, attn_w_qkv, attn_b_qkv, attn_sinks, attn_w_o, attn_b_o, ffn_w_in, ffn_w_conv, ffn_w_down, loss_target, m_norm_mix, m_norm_ffn, m_norm_final, m_conv_w_in, m_conv_w_conv, m_conv_w_out, m_attn_w_qkv, m_attn_b_qkv, m_attn_sinks, m_attn_w_o, m_attn_b_o, m_ffn_w_in, m_ffn_w_conv, m_ffn_w_down, v_norm_mix, v_norm_ffn, v_norm_final, v_conv_w_in, v_conv_w_conv, v_conv_w_out, v_attn_w_qkv, v_attn_b_qkv, v_attn_sinks, v_attn_w_o, v_attn_b_o, v_ffn_w_in, v_ffn_w_conv, v_ffn_w_down):
    bsz, seq, d = x.shape
    t = bsz * seq
    depth = norm_mix.shape[0]
    n_conv, n_attn = conv_w_in.shape[0], attn_w_qkv.shape[0]
    xq, yq, cq = _place()
    q = 2 * xq + yq

    big = [conv_w_in, conv_w_out, attn_w_qkv, attn_w_o, ffn_w_in, ffn_w_down]
    axes = [2, 1, 2, 1, 2, 1]
    q_arr = q.astype(jnp.int32).reshape(1)
    c_arr = cq.astype(jnp.int32).reshape(1)
    weights = [_place_shard(w, ax, q_arr, f"place_shard{n}") for n, (w, ax) in enumerate(zip(big, axes))]

    def pieces_of(i):
        return [(0, i // 2), (1, i // 2), (4, i), (5, i)] if i % 2 == 0 else [(2, i // 2), (3, i // 2), (4, i), (5, i)]

    weights = _gather_weights(weights, axes, pieces_of(0)[:2], "gather_mixer0")

    small_cols = [conv_w_conv, attn_b_qkv, attn_b_o, ffn_w_conv]

    def placed(v):
        width = v.shape[-1]
        full = jnp.zeros(v.shape[:-1] + (N_CHIPS * width,), F32)
        return lax.dynamic_update_slice_in_dim(full, v * (1.0 / N_CORES), q * width, axis=v.ndim - 1)

    full_cols = [placed(v) for v in small_cols]
    wc_conv, b_qkv, b_o, wf_conv = _unpack(_allreduce_small(_pack(full_cols), "gather_small"), full_cols)
    cos_t, sin_t = _rope_tables(seq)

    xs = x.reshape(t, d)
    saved = []
    sems = {}

    def fetch(pieces, stage, tag):
        nonlocal weights
        weights, sems[tag, 0], sems[tag, 1] = _gather_start(weights, axes, pieces, stage, f"gather_{'ici' if stage == 0 else 'pass'}_start{tag}")

    def settle(pieces, stage, tag, after):
        nonlocal weights
        weights = _gather_wait(weights, sems[tag, 0], sems[tag, 1], after, axes, pieces, stage,
                               f"gather_{'ici' if stage == 0 else 'pass'}_wait{tag}")

    for i in range(depth):
        j = i // 2
        ahead = pieces_of(i + 1) if i + 1 < depth else None
        if i == 0:
            fetch(pieces_of(0)[2:], 0, "0")
        elif ahead:
            fetch(ahead, 0, str(i + 1))
        w_cin, w_cout, w_qkv, w_o, w_fin, w_fdown = weights
        if i == 0:
            h = _rms_fwd(xs, norm_mix[0:1], "norm_mix_fwd0")
        g_ffn = norm_ffn[i:i + 1]
        if i % 2 == 0:
            pre = _mm(h, w_cin, "nn", BF16, layer=j, tm=1024, tn=768, tk=4096, name=f"conv_in_fwd{i}")
            mixed = _convgate_fwd(pre, wc_conv[j], seq, f"conv_gate_fwd{i}")
            if i == 0:
                settle(pieces_of(0)[2:], 0, "0", mixed)
                fetch(pieces_of(0)[2:], 1, "0")
                w_cin, w_cout, w_qkv, w_o, w_fin, w_fdown = weights
            x_mid, h2 = _mm(mixed, w_cout, "nn", F32, layer=j, residual=xs, norm_g=g_ffn, tm=512, tn=1024, tk=4096,
                            name=f"conv_out_fwd{i}")
            lse = None
        else:
            pre = _mm(h, w_qkv, "nn", F32, layer=j, bias=b_qkv[j:j + 1], tm=1024, tn=768, tk=4096, name=f"qkv_fwd{i}")
            mixed, lse = _attn_fwd(pre, attn_sinks[j], cos_t, sin_t, bsz, seq, f"attn_fwd{i}", hp=1)
            x_mid, h2 = _mm(mixed, w_o, "nn", F32, layer=j, bias=b_o[j:j + 1], residual=xs, norm_g=g_ffn, tm=512, tn=1024,
                            tk=4096, name=f"attn_out_fwd{i}")
        if i == 0:
            settle(pieces_of(0)[2:], 1, "0", h2)
            fetch(ahead, 0, "1")
        elif ahead:
            settle(ahead, 0, str(i + 1), x_mid)
            fetch(ahead, 1, str(i + 1))
        w_cin, w_cout, w_qkv, w_o, w_fin, w_fdown = weights
        gu = _mm(h2, w_fin, "nn", BF16, layer=i, n_outer=True, tm=2048, tn=1408, tk=4096, name=f"ffn_in_fwd{i}")
        if i == 0:
            settle(ahead, 0, "1", gu)
            fetch(ahead, 1, "1")
            w_cin, w_cout, w_qkv, w_o, w_fin, w_fdown = weights
        g_next = norm_mix[i + 1:i + 2] if i + 1 < depth else norm_final.reshape(1, d)
        x_next, act, h_next = _ffn_gate_down_fwd(gu, wf_conv[i], w_fdown, i, x_mid, g_next, seq, f"ffn_gate_down_fwd{i}")
        if ahead:
            settle(ahead, 1, str(i + 1), x_next)
        saved.append((xs, h, pre, mixed, lse, x_mid, h2, gu, act))
        xs, h = x_next, h_next
    w_cin, w_cout, w_qkv, w_o, w_fin, w_fdown = weights

    dx, dxb, sq, dg_final = _loss_head(xs, loss_target.reshape(t, d), norm_final.reshape(1, d), "loss_head")
    loss = lax.psum(0.5 * jnp.sum(sq) / d, ("x", "y", "c"))

    g_norm_mix, g_norm_ffn = [None] * depth, [None] * depth
    g_cin, g_cconv, g_cout = [None] * n_conv, [None] * n_conv, [None] * n_conv
    g_qkv, g_bqkv, g_sinks, g_o, g_bo = ([None] * n_attn for _ in range(5))
    g_fin, g_fconv, g_fdown = [None] * depth, [None] * depth, [None] * depth

    kinds6 = ["col", "row", "col", "row", "col", "row"]
    layers6 = [n_conv, n_conv, n_attn, n_attn, depth, depth]
    big_w = [conv_w_in, conv_w_out, attn_w_qkv, attn_w_o, ffn_w_in, ffn_w_down]

    def slot_stack(n):
        k, cols = big_w[n].shape[1], big_w[n].shape[2]
        r = k // 2
        return lax.empty((layers6[n], N_CHIPS, r, cols), BF16)

    own = [slot_stack(n) for n in range(6)]
    mine = [slot_stack(n) for n in range(6)]
    sib = [slot_stack(n) for n in range(6)]
    flight = {}

    def group(i, part):
        return f"{part}{i}", (pieces_of(i)[:2] if part == "m" else pieces_of(i)[2:])

    def scatter(grp, stage, action, token):
        tag, pieces = grp
        ts = [ti for ti, _ in pieces]
        local = [(n, l) for n, (_, l) in enumerate(pieces)]
        sub = ([own[ti] for ti in ts], [mine[ti] for ti in ts], [sib[ti] for ti in ts], token)
        label = f"grad_{'ici' if stage == 0 else 'pass'}_{action}_{tag}"
        if action == "start":
            o, mi, si, token, s_sems, r_sems = _scatter_start(*sub, local, stage, label)
            flight[tag] = (s_sems, r_sems)
        else:
            o, mi, si, token = _scatter_wait(*sub, *flight[tag], local, stage, label)
        for n, ti in enumerate(ts):
            own[ti], mine[ti], sib[ti] = o[n], mi[n], si[n]
        return token

    def pair_begin(grp, token):
        tag, pieces = grp
        grads = {0: g_cin, 1: g_cout, 2: g_qkv, 3: g_o, 4: g_fin, 5: g_fdown}
        parts, kinds = [], []
        for ti, l in pieces:
            g = grads[ti][l]
            parts.append(g if kinds6[ti] == "col" else g.reshape(N_CHIPS, g.shape[0] // N_CHIPS, g.shape[1]))
            kinds.append(kinds6[ti])
        parts, token, lands, s_sems, r_sems = _pair_start(parts, kinds, token, f"grad_pair_start_{tag}")
        flight["pair" + tag] = (parts, kinds, lands, s_sems, r_sems)
        return token

    def pair_finish(grp, token):
        tag, pieces = grp
        parts, kinds, lands, s_sems, r_sems = flight["pair" + tag]
        parts, token, recv = _pair_wait(parts, kinds, token, lands, s_sems, r_sems, f"grad_pair_wait_{tag}")
        for (ti, l), g, r in zip(pieces, parts, recv):
            own[ti] = _pair_sum(g, r, kinds6[ti], c_arr, own[ti], l, f"grad_pair_sum_{tag}_{ti}")
        return scatter(grp, 0, "start", token)

    for i in reversed(range(depth)):
        j = i // 2
        x_in, h, pre, mixed, lse, x_mid, h2, gu, act = saved[i]
        da = _mm(dxb, w_fdown, "nt", BF16, layer=i, n_outer=True, tm=2048, tn=1408, tk=4096, name=f"ffn_down_dx{i}")
        g_fdown[i] = _mm(act, dxb, "tn", BF16, tm=1408, tn=1024, tk=2048, name=f"ffn_down_dw{i}")
        dgu, dwc = _ffngate_bwd(gu, da, wf_conv[i], seq, f"ffn_gate_bwd{i}")
        g_fconv[i] = dwc[:3]
        g_fin[i] = _mm(h2, dgu, "tn", BF16, tm=1024, tn=1408, tk=2048, name=f"ffn_in_dw{i}")
        dgu = pair_begin(group(i, "f"), dgu)
        dx, dxb, dg, colsum = _rms_bwd(x_mid, dgu, w_fin, i, norm_ffn[i:i + 1], dx, f"ffn_in_dx_norm_bwd{i}")
        g_norm_ffn[i] = jnp.sum(dg, axis=0)
        if i + 1 < depth:
            dxb = scatter(group(i + 1, "m"), 1, "start", scatter(group(i + 1, "m"), 0, "wait", dxb))
            dxb = scatter(group(i + 1, "f"), 1, "wait", dxb)
        dxb = pair_finish(group(i, "f"), dxb)
        if i % 2 == 0:
            dmix = _mm(dxb, w_cout, "nt", BF16, layer=j, tm=512, tn=1024, tk=4096, name=f"conv_out_dx{i}")
            g_cout[j] = _mm(mixed, dxb, "tn", BF16, tm=1024, tn=1024, tk=2048, name=f"conv_out_dw{i}")
            dpre, dwc = _convgate_bwd(pre, dmix, wc_conv[j], seq, f"conv_gate_bwd{i}")
            g_cconv[j] = dwc[:3]
            g_cin[j] = _mm(h, dpre, "tn", BF16, tm=1024, tn=1536, tk=2048, name=f"conv_in_dw{i}")
            w_pre = w_cin
        else:
            g_bo[j] = jnp.sum(colsum, axis=0)
            dmix = _mm(dxb, w_o, "nt", F32, layer=j, tm=512, tn=1024, tk=4096, name=f"attn_out_dx{i}")
            g_o[j] = _mm(mixed, dxb, "tn", BF16, tm=1024, tn=1024, tk=2048, name=f"attn_out_dw{i}")
            dpre, dbias, dsk = _attn_bwd(pre, mixed, lse, dmix, attn_sinks[j], cos_t, sin_t, bsz, seq, f"attn_bwd{i}",
                                         hp=GROUP)
            g_bqkv[j] = jnp.sum(dbias, axis=0)
            g_sinks[j] = jnp.sum(dsk, axis=1)
            g_qkv[j] = _mm(h, dpre, "tn", BF16, tm=1024, tn=1536, tk=2048, name=f"qkv_dw{i}")
            w_pre = w_qkv
        dpre = pair_begin(group(i, "m"), dpre)
        dx, dxb, dg, _ = _rms_bwd(x_in, dpre, w_pre, j, norm_mix[i:i + 1], dx, f"mixer_in_dx_norm_bwd{i}")
        g_norm_mix[i] = jnp.sum(dg, axis=0)
        dxb = scatter(group(i, "f"), 1, "start", scatter(group(i, "f"), 0, "wait", dxb))
        if i + 1 < depth:
            dxb = scatter(group(i + 1, "m"), 1, "wait", dxb)
        dxb = pair_finish(group(i, "m"), dxb)
    grad_x = dx.reshape(bsz, seq, d)

    dxb = scatter(group(0, "m"), 1, "start", scatter(group(0, "m"), 0, "wait", dxb))
    scatter(group(0, "m"), 1, "wait", scatter(group(0, "f"), 1, "wait", dxb))
    big_m = [m_conv_w_in, m_conv_w_out, m_attn_w_qkv, m_attn_w_o, m_ffn_w_in, m_ffn_w_down]
    big_v = [v_conv_w_in, v_conv_w_out, v_attn_w_qkv, v_attn_w_o, v_ffn_w_in, v_ffn_w_down]
    big_names = ["conv_w_in", "conv_w_out", "attn_w_qkv", "attn_w_o", "ffn_w_in", "ffn_w_down"]
    qc_arr = jnp.stack([q, cq]).astype(jnp.int32)
    big_upd = [_reduce_adamw(own[n], mine[n], sib[n], big_w[n], big_m[n], big_v[n], qc_arr, f"adamw_{nm}")
               for n, nm in enumerate(big_names)]

    small = [jnp.stack(g_norm_mix), jnp.stack(g_norm_ffn), jnp.sum(dg_final, axis=0), jnp.stack(g_cconv),
             jnp.stack(g_bqkv), jnp.stack(g_sinks), jnp.stack(g_bo), jnp.stack(g_fconv)]
    sg = _unpack(_allreduce_small(_pack(small), "grad_small_allreduce"), small)

    def my_cols(v, like):
        width = like.shape[-1]
        return lax.dynamic_slice_in_dim(v, q * width, width, axis=v.ndim - 1)

    small_w = [norm_mix, norm_ffn, norm_final, conv_w_conv, attn_b_qkv, attn_sinks, attn_b_o, ffn_w_conv]
    small_m = [m_norm_mix, m_norm_ffn, m_norm_final, m_conv_w_conv, m_attn_b_qkv, m_attn_sinks, m_attn_b_o, m_ffn_w_conv]
    small_v = [v_norm_mix, v_norm_ffn, v_norm_final, v_conv_w_conv, v_attn_b_qkv, v_attn_sinks, v_attn_b_o, v_ffn_w_conv]
    small_g = [sg[0], sg[1], sg[2], my_cols(sg[3], conv_w_conv), my_cols(sg[4], attn_b_qkv), sg[5],
               my_cols(sg[6], attn_b_o), my_cols(sg[7], ffn_w_conv)]

    upd = {nm: tuple(u[1:]) for nm, u in zip(big_names, big_upd)}
    sd, sm, sv = _adamw(_pack(small_w)[None], _pack(small_g)[None], _pack(small_m)[None], _pack(small_v, 1.0)[None],
                        "adamw_small")
    sd, sm, sv = _unpack(sd, small_w), _unpack(sm, small_w), _unpack(sv, small_w)
    names = ["norm_mix", "norm_ffn", "norm_final", "conv_w_in", "conv_w_conv", "conv_w_out", "attn_w_qkv", "attn_b_qkv",
             "attn_sinks", "attn_w_o", "attn_b_o", "ffn_w_in", "ffn_w_conv", "ffn_w_down"]
    small_names = ["norm_mix", "norm_ffn", "norm_final", "conv_w_conv", "attn_b_qkv", "attn_sinks", "attn_b_o", "ffn_w_conv"]
    grads = dict(zip(small_names, small_g))
    grads.update({nm: u[0] for nm, u in zip(big_names, big_upd)})
    for n, nm in enumerate(small_names):
        upd[nm] = (sd[n], sm[n], sv[n])
    return (loss, grad_x, *[grads[nm] for nm in names], *[upd[nm][0] for nm in names],
            *[upd[nm][1] for nm in names], *[upd[nm][2] for nm in names])
```

```python
import math

import jax
import jax.numpy as jnp
from jax import lax
from jax.experimental import pallas as pl
from jax.experimental.pallas import tpu as pltpu

F32 = jnp.float32
BF16 = jnp.bfloat16

HEAD_DIM = 64
GROUP = 4
WINDOW = 128
EPS = 1e-5
ROPE_THETA = 10000.0
ADAM_LR, ADAM_B1, ADAM_B2, ADAM_EPS, ADAM_WD, ADAM_STEP = 0.001, 0.9, 0.999, 1e-08, 0.01, 10

N_CHIPS = 4
N_CORES = 2
N_DEV = 8
HALO = 16
VMEM_LIMIT_BYTES = 56 * 1024 * 1024
MESH = pl.DeviceIdType.MESH
ANY = pl.BlockSpec(memory_space=pl.ANY)
SMEM = pl.BlockSpec(memory_space=pltpu.SMEM)
NEG = float(jnp.finfo(jnp.float32).min)
ROW_TILES = (512, 256, 128, 64, 32, 16, 8)


def _pick(dim, cands):
    for c in cands:
        if dim % c == 0:
            return c
    return dim


def _params(sem):
    return pltpu.CompilerParams(dimension_semantics=sem, vmem_limit_bytes=VMEM_LIMIT_BYTES)


_DIMS = {"nn": (((1,), (0,)), ((), ())), "nt": (((1,), (1,)), ((), ())), "tn": (((0,), (0,)), ((), ()))}


def _mm(a, b, mode, out_dtype, *, layer=None, bias=None, residual=None, norm_g=None, n_outer=False, tm, tn, tk, name):
    b2 = b.shape[1:] if layer is not None else b.shape
    if mode == "nn":
        (m, k), n = a.shape, b2[1]
    elif mode == "nt":
        (m, k), n = a.shape, b2[0]
    else:
        (k, m), n = a.shape, b2[1]
    tm, tn, tk = min(tm, m), min(tn, n), min(tk, k)
    assert m % tm == 0 and n % tn == 0 and k % tk == 0, (name, a.shape, b.shape, tm, tn, tk)
    nk = k // tk

    def at(f):
        return (lambda p0, p1, p2: f(p1, p0, p2)) if n_outer else f

    a_spec = pl.BlockSpec((tk, tm), at(lambda i, j, l: (l, i))) if mode == "tn" else pl.BlockSpec((tm, tk), at(lambda i, j, l: (i, l)))
    if layer is None:
        b_spec = (pl.BlockSpec((tn, tk), at(lambda i, j, l: (j, l))) if mode == "nt"
                  else pl.BlockSpec((tk, tn), at(lambda i, j, l: (l, j))))
    elif mode == "nt":
        b_spec = pl.BlockSpec((None, tn, tk), at(lambda i, j, l: (layer, j, l)))
    else:
        b_spec = pl.BlockSpec((None, tk, tn), at(lambda i, j, l: (layer, l, j)))
    in_specs, args = [a_spec, b_spec], [a, b]
    if bias is not None:
        in_specs.append(pl.BlockSpec((1, tn), at(lambda i, j, l: (0, j))))
        args.append(bias)
    if residual is not None:
        in_specs.append(pl.BlockSpec((tm, tn), at(lambda i, j, l: (i, j))))
        args.append(residual)
    if norm_g is not None:
        assert tn == n, (name, "the RMSNorm of the result needs whole rows in a tile")
        in_specs.append(pl.BlockSpec((1, tn), at(lambda i, j, l: (0, j))))
        args.append(norm_g)
    has_bias, has_res, has_norm = bias is not None, residual is not None, norm_g is not None

    def body(*refs):
        a_ref, b_ref = refs[0], refs[1]
        pos = 2
        bias_ref = res_ref = g_ref = h_ref = None
        if has_bias:
            bias_ref, pos = refs[pos], pos + 1
        if has_res:
            res_ref, pos = refs[pos], pos + 1
        if has_norm:
            g_ref, pos = refs[pos], pos + 1
        o_ref, pos = refs[pos], pos + 1
        if has_norm:
            h_ref, pos = refs[pos], pos + 1
        acc_ref = refs[pos] if nk > 1 else None

        def finish(acc):
            if has_bias:
                acc = acc + bias_ref[...]
            if has_res:
                acc = acc + res_ref[...]
            o_ref[...] = acc.astype(o_ref.dtype)
            if has_norm:
                h_ref[...] = _rms(acc, g_ref[...]).astype(BF16)

        if nk == 1:
            finish(lax.dot_general(a_ref[...], b_ref[...], _DIMS[mode], preferred_element_type=F32))
            return
        l = pl.program_id(2)
        part = lax.dot_general(a_ref[...], b_ref[...], _DIMS[mode], preferred_element_type=F32)

        @pl.when(l == 0)
        def _():
            acc_ref[...] = part

        @pl.when(l > 0)
        def _():
            acc_ref[...] += part

        @pl.when(l == nk - 1)
        def _():
            finish(acc_ref[...])

    o_spec = pl.BlockSpec((tm, tn), at(lambda i, j, l: (i, j)))
    o_shape = jax.ShapeDtypeStruct((m, n), out_dtype)
    return pl.pallas_call(
        body,
        name=name,
        grid=(n // tn, m // tm, nk) if n_outer else (m // tm, n // tn, nk),
        in_specs=in_specs,
        out_specs=[o_spec, o_spec] if has_norm else o_spec,
        out_shape=[o_shape, jax.ShapeDtypeStruct((m, n), BF16)] if has_norm else o_shape,
        scratch_shapes=[pltpu.VMEM((tm, tn), F32)] if nk > 1 else [],
        compiler_params=_params(("parallel", "parallel", "arbitrary")),
    )(*args)


def _rms(x, g):
    return x * lax.rsqrt(jnp.mean(x * x, axis=-1, keepdims=True) + EPS) * g


def _fold8(v):
    r, d = v.shape
    return jnp.sum(v.reshape(r // 8, 8, d), axis=0)


def _rms_fwd(x, g, name):
    t, d = x.shape
    tm = _pick(t, ROW_TILES)

    def body(x_ref, g_ref, h_ref):
        xv = x_ref[...]
        r = lax.rsqrt(jnp.mean(xv * xv, axis=-1, keepdims=True) + EPS)
        h_ref[...] = (xv * r * g_ref[...]).astype(BF16)

    return pl.pallas_call(
        body,
        name=name,
        grid=(t // tm,),
        in_specs=[pl.BlockSpec((tm, d), lambda i: (i, 0)), pl.BlockSpec((1, d), lambda i: (0, 0))],
        out_specs=pl.BlockSpec((tm, d), lambda i: (i, 0)),
        out_shape=jax.ShapeDtypeStruct((t, d), BF16),
        compiler_params=_params(("parallel",)),
    )(x, g)


def _rms_bwd(x, dpre, w, layer, g, dx_in, name):
    t, d = x.shape
    k = dpre.shape[1]
    tm = _pick(t, ROW_TILES)

    def body(x_ref, dp_ref, w_ref, g_ref, dxi_ref, dx_ref, dxb_ref, dg_ref, cs_ref):
        i = pl.program_id(0)
        xv = x_ref[...]
        r = lax.rsqrt(jnp.mean(xv * xv, axis=-1, keepdims=True) + EPS)
        xhat = xv * r
        dy = lax.dot_general(dp_ref[...], w_ref[...], _DIMS["nt"], preferred_element_type=F32)
        gdy = dy * g_ref[...]
        dx = dxi_ref[...] + r * (gdy - xhat * jnp.mean(gdy * xhat, axis=-1, keepdims=True))
        dx_ref[...] = dx
        dxb_ref[...] = dx.astype(BF16)

        @pl.when(i == 0)
        def _():
            dg_ref[...] = jnp.zeros_like(dg_ref)
            cs_ref[...] = jnp.zeros_like(cs_ref)

        dg_ref[...] += _fold8(dy * xhat)
        cs_ref[...] += _fold8(dx)

    row = pl.BlockSpec((tm, d), lambda i: (i, 0))
    acc = pl.BlockSpec((8, d), lambda i: (0, 0))
    w_spec = pl.BlockSpec((None, d, k), lambda i: (layer, 0, 0), pipeline_mode=pl.Buffered(1))
    return pl.pallas_call(
        body,
        name=name,
        grid=(t // tm,),
        in_specs=[row, pl.BlockSpec((tm, k), lambda i: (i, 0)), w_spec, pl.BlockSpec((1, d), lambda i: (0, 0)), row],
        out_specs=[row, row, acc, acc],
        out_shape=[jax.ShapeDtypeStruct((t, d), F32), jax.ShapeDtypeStruct((t, d), BF16),
                   jax.ShapeDtypeStruct((8, d), F32), jax.ShapeDtypeStruct((8, d), F32)],
        compiler_params=_params(("arbitrary",)),
    )(x, dpre, w, g, dx_in)


def _loss_head(x, target, g, name):
    t, d = x.shape
    tm = _pick(t, ROW_TILES)
    inv_d = 1.0 / d

    def body(x_ref, t_ref, g_ref, dx_ref, dxb_ref, sq_ref, dg_ref):
        i = pl.program_id(0)
        xv = x_ref[...]
        gv = g_ref[...]
        r = lax.rsqrt(jnp.mean(xv * xv, axis=-1, keepdims=True) + EPS)
        xhat = xv * r
        err = xhat * gv - t_ref[...]
        dy = err * inv_d
        gdy = dy * gv
        dx = r * (gdy - xhat * jnp.mean(gdy * xhat, axis=-1, keepdims=True))
        dx_ref[...] = dx
        dxb_ref[...] = dx.astype(BF16)

        @pl.when(i == 0)
        def _():
            sq_ref[...] = jnp.zeros_like(sq_ref)
            dg_ref[...] = jnp.zeros_like(dg_ref)

        sq_ref[...] += _fold8(err * err)
        dg_ref[...] += _fold8(dy * xhat)

    row = pl.BlockSpec((tm, d), lambda i: (i, 0))
    acc = pl.BlockSpec((8, d), lambda i: (0, 0))
    return pl.pallas_call(
        body,
        name=name,
        grid=(t // tm,),
        in_specs=[row, row, pl.BlockSpec((1, d), lambda i: (0, 0))],
        out_specs=[row, row, acc, acc],
        out_shape=[jax.ShapeDtypeStruct((t, d), F32), jax.ShapeDtypeStruct((t, d), BF16),
                   jax.ShapeDtypeStruct((8, d), F32), jax.ShapeDtypeStruct((8, d), F32)],
        compiler_params=_params(("arbitrary",)),
    )(x, target, g)


def _rows(tm):
    return lax.broadcasted_iota(jnp.int32, (tm, 1), 0)


def _shift_down(u, before2):
    r8 = _rows(8)
    s1, s2 = pltpu.roll(u, 1, 0), pltpu.roll(u, 2, 0)
    top1 = jnp.where(r8 == 0, before2[1:2], s1[:8])
    top2 = jnp.where(r8 == 0, before2[0:1], jnp.where(r8 == 1, before2[1:2], s2[:8]))
    return jnp.concatenate([top1, s1[8:]], axis=0), jnp.concatenate([top2, s2[8:]], axis=0)


def _shift_up(u, after2):
    tm = u.shape[0]
    r8 = _rows(8)
    s1, s2 = pltpu.roll(u, tm - 1, 0), pltpu.roll(u, tm - 2, 0)
    bot1 = jnp.where(r8 == 7, after2[0:1], s1[tm - 8:])
    bot2 = jnp.where(r8 == 6, after2[0:1], jnp.where(r8 == 7, after2[1:2], s2[tm - 8:]))
    return jnp.concatenate([s1[:tm - 8], bot1], axis=0), jnp.concatenate([s2[:tm - 8], bot2], axis=0)


def _shift_matrix(tm, up):
    r = lax.broadcasted_iota(jnp.int32, (2 * tm, tm), 0)
    c = lax.broadcasted_iota(jnp.int32, (2 * tm, tm), 1)
    t = jnp.where(r >= tm, r - tm, r)
    k = jnp.where(r >= tm, 2, 1)
    return (c == (t + k if up else t - k)).astype(BF16)


def _shift_down_mxu(u, before2):
    tm = u.shape[0]
    moved = jnp.dot(_shift_matrix(tm, False), u.astype(BF16), preferred_element_type=F32)
    r8 = _rows(8)
    s1, s2 = moved[:tm], moved[tm:]
    top1 = s1[:8] + jnp.where(r8 == 0, before2[1:2], 0.0)
    top2 = s2[:8] + jnp.where(r8 == 0, before2[0:1], jnp.where(r8 == 1, before2[1:2], 0.0))
    return jnp.concatenate([top1, s1[8:]], axis=0), jnp.concatenate([top2, s2[8:]], axis=0)


def _shift_up_mxu(u, after2):
    tm = u.shape[0]
    moved = jnp.dot(_shift_matrix(tm, True), u.astype(BF16), preferred_element_type=F32)
    r8 = _rows(8)
    s1, s2 = moved[:tm], moved[tm:]
    bot1 = s1[tm - 8:] + jnp.where(r8 == 7, after2[0:1], 0.0)
    bot2 = s2[tm - 8:] + jnp.where(r8 == 6, after2[0:1], jnp.where(r8 == 7, after2[1:2], 0.0))
    return jnp.concatenate([s1[:tm - 8], bot1], axis=0), jnp.concatenate([s2[:tm - 8], bot2], axis=0)


def _conv_tile(seq):
    return _pick(seq, (256, 128, 64, 32, 16, 8))


def _halo_specs(tm, width, n_tiles):
    per = tm // HALO
    before = pl.BlockSpec((HALO, width), lambda i: (jnp.maximum(i * per - 1, 0), 0))
    after = pl.BlockSpec((HALO, width), lambda i: (jnp.minimum((i + 1) * per, n_tiles * per - 1), 0))
    return before, after


def _convgate_fwd(bcv, w, seq, name):
    t, d3 = bcv.shape
    d = d3 // 3
    tm = _conv_tile(seq)
    tps = seq // tm
    before, _ = _halo_specs(tm, d3, t // tm)

    def body(x_ref, xb_ref, w_ref, y_ref):
        i = pl.program_id(0)
        inner = (i % tps != 0).astype(F32)
        u = x_ref[:, d:2 * d].astype(F32) * x_ref[:, 2 * d:].astype(F32)
        xb = xb_ref[:, d:].astype(F32)[HALO - 2:]
        s1, s2 = _shift_down(u, xb[:, :d] * xb[:, d:] * inner)
        z = w_ref[2:3] * u + w_ref[1:2] * s1 + w_ref[0:1] * s2
        y_ref[...] = (x_ref[:, :d].astype(F32) * z).astype(BF16)

    return pl.pallas_call(
        body,
        name=name,
        grid=(t // tm,),
        in_specs=[pl.BlockSpec((tm, d3), lambda i: (i, 0)), before, pl.BlockSpec((3, d), lambda i: (0, 0))],
        out_specs=pl.BlockSpec((tm, d), lambda i: (i, 0)),
        out_shape=jax.ShapeDtypeStruct((t, d), BF16),
        compiler_params=_params(("parallel",)),
    )(bcv, bcv, w)


def _convgate_bwd(bcv, dy, w, seq, name):
    t, d3 = bcv.shape
    d = d3 // 3
    tm = _conv_tile(seq)
    tps = seq // tm
    before, after = _halo_specs(tm, d3, t // tm)
    _, after_dy = _halo_specs(tm, d, t // tm)

    def body(x_ref, xb_ref, xa_ref, dy_ref, dya_ref, w_ref, dx_ref, dw_ref):
        i = pl.program_id(0)
        inner_lo = (i % tps != 0).astype(F32)
        inner_hi = (i % tps != tps - 1).astype(F32)
        w0, w1, w2 = w_ref[0:1], w_ref[1:2], w_ref[2:3]
        b, c, v = x_ref[:, :d].astype(F32), x_ref[:, d:2 * d].astype(F32), x_ref[:, 2 * d:].astype(F32)
        u = c * v
        xb = xb_ref[:, d:].astype(F32)[HALO - 2:]
        s1, s2 = _shift_down(u, xb[:, :d] * xb[:, d:] * inner_lo)
        z = w2 * u + w1 * s1 + w0 * s2
        dyv = dy_ref[...].astype(F32)
        dz = dyv * b
        dza = dya_ref[...].astype(F32)[0:2] * xa_ref[:, :d].astype(F32)[0:2] * inner_hi
        n1, n2 = _shift_up(dz, dza)
        du = w2 * dz + w1 * n1 + w0 * n2
        dx_ref[:, :d] = (dyv * z).astype(BF16)
        dx_ref[:, d:2 * d] = (du * v).astype(BF16)
        dx_ref[:, 2 * d:] = (du * c).astype(BF16)

        @pl.when(i == 0)
        def _():
            dw_ref[...] = jnp.zeros_like(dw_ref)

        dw_ref[0:1] += jnp.sum(dz * s2, axis=0, keepdims=True)
        dw_ref[1:2] += jnp.sum(dz * s1, axis=0, keepdims=True)
        dw_ref[2:3] += jnp.sum(dz * u, axis=0, keepdims=True)

    return pl.pallas_call(
        body,
        name=name,
        grid=(t // tm,),
        in_specs=[pl.BlockSpec((tm, d3), lambda i: (i, 0)), before, after,
                  pl.BlockSpec((tm, d), lambda i: (i, 0)), after_dy, pl.BlockSpec((3, d), lambda i: (0, 0))],
        out_specs=[pl.BlockSpec((tm, d3), lambda i: (i, 0)), pl.BlockSpec((8, d), lambda i: (0, 0))],
        out_shape=[jax.ShapeDtypeStruct((t, d3), BF16), jax.ShapeDtypeStruct((8, d), F32)],
        compiler_params=_params(("arbitrary",)),
    )(bcv, bcv, bcv, dy, dy, w)


def _sigmoid(x):
    return 1.0 / (1.0 + jnp.exp(-x))


def _ffn_gate_down_fwd(gu, w, w_down, layer, resid, norm_g, seq, name):
    t, f2 = gu.shape
    f = f2 // 2
    d = w_down.shape[2]
    sub = _conv_tile(seq)
    tm = _pick(seq, (2 * sub, sub))
    tps = seq // tm
    before, _ = _halo_specs(tm, f2, t // tm)

    def body(x_ref, xb_ref, w_ref, wd_ref, res_ref, g_ref, o_ref, a_ref, h_ref):
        i = pl.program_id(0)
        inner = (i % tps != 0).astype(F32)
        for r0 in range(0, tm, sub):
            rows = slice(r0, r0 + sub)
            if r0 == 0:
                halo = xb_ref[:, :f].astype(F32)[HALO - 2:] * inner
            else:
                halo = x_ref[r0 - HALO:r0, :f].astype(F32)[HALO - 2:]
            s1, s2 = _shift_down_mxu(x_ref[rows, :f], halo)
            gc = w_ref[2:3] * x_ref[rows, :f].astype(F32) + w_ref[1:2] * s1 + w_ref[0:1] * s2
            act = (gc * _sigmoid(gc) * x_ref[rows, f:].astype(F32)).astype(BF16)
            a_ref[rows, :] = act
            out = jnp.dot(act, wd_ref[...], preferred_element_type=F32) + res_ref[rows, :]
            o_ref[rows, :] = out
            h_ref[rows, :] = _rms(out, g_ref[...]).astype(BF16)

    row_d = pl.BlockSpec((tm, d), lambda i: (i, 0))
    return pl.pallas_call(
        body,
        name=name,
        grid=(t // tm,),
        in_specs=[pl.BlockSpec((tm, f2), lambda i: (i, 0)), before, pl.BlockSpec((3, f), lambda i: (0, 0)),
                  pl.BlockSpec((None, f, d), lambda i: (layer, 0, 0), pipeline_mode=pl.Buffered(1)),
                  row_d, pl.BlockSpec((1, d), lambda i: (0, 0))],
        out_specs=[row_d, pl.BlockSpec((tm, f), lambda i: (i, 0)), row_d],
        out_shape=[jax.ShapeDtypeStruct((t, d), F32), jax.ShapeDtypeStruct((t, f), BF16), jax.ShapeDtypeStruct((t, d), BF16)],
        compiler_params=_params(("parallel",)),
    )(gu, gu, w, w_down, resid, norm_g)


def _ffngate_bwd(gu, da, w, seq, name):
    t, f2 = gu.shape
    f = f2 // 2
    tm = _conv_tile(seq)
    tps = seq // tm
    before, after = _halo_specs(tm, f2, t // tm)
    _, after_da = _halo_specs(tm, f, t // tm)

    def body(x_ref, xb_ref, xa_ref, da_ref, daa_ref, w_ref, dx_ref, dw_ref):
        i = pl.program_id(0)
        inner_lo = (i % tps != 0).astype(F32)
        inner_hi = (i % tps != tps - 1).astype(F32)
        w0, w1, w2 = w_ref[0:1], w_ref[1:2], w_ref[2:3]

        def dgate(gc, uv, dav):
            sg = _sigmoid(gc)
            return dav * uv * (sg * (1.0 + gc * (1.0 - sg))), dav * (gc * sg)

        g, u = x_ref[:, :f].astype(F32), x_ref[:, f:].astype(F32)
        s1, s2 = _shift_down_mxu(x_ref[:, :f], xb_ref[:, :f].astype(F32)[HALO - 2:] * inner_lo)
        gc = w2 * g + w1 * s1 + w0 * s2
        dgc, du = dgate(gc, u, da_ref[...].astype(F32))
        ga = xa_ref[:, :f].astype(F32)
        a1, a2 = _shift_down(ga, x_ref[tm - HALO:, :f].astype(F32)[HALO - 2:])
        gca = w2 * ga + w1 * a1 + w0 * a2
        dgca, _ = dgate(gca, xa_ref[:, f:].astype(F32), daa_ref[...].astype(F32))
        n1, n2 = _shift_up_mxu(dgc, dgca[0:2] * inner_hi)
        dx_ref[:, :f] = (w2 * dgc + w1 * n1 + w0 * n2).astype(BF16)
        dx_ref[:, f:] = du.astype(BF16)

        @pl.when(i == 0)
        def _():
            dw_ref[...] = jnp.zeros_like(dw_ref)

        dw_ref[0:1] += jnp.sum(dgc * s2, axis=0, keepdims=True)
        dw_ref[1:2] += jnp.sum(dgc * s1, axis=0, keepdims=True)
        dw_ref[2:3] += jnp.sum(dgc * g, axis=0, keepdims=True)

    return pl.pallas_call(
        body,
        name=name,
        grid=(t // tm,),
        in_specs=[pl.BlockSpec((tm, f2), lambda i: (i, 0)), before, after,
                  pl.BlockSpec((tm, f), lambda i: (i, 0)), after_da, pl.BlockSpec((3, f), lambda i: (0, 0))],
        out_specs=[pl.BlockSpec((tm, f2), lambda i: (i, 0)), pl.BlockSpec((8, f), lambda i: (0, 0))],
        out_shape=[jax.ShapeDtypeStruct((t, f2), BF16), jax.ShapeDtypeStruct((8, f), F32)],
        compiler_params=_params(("arbitrary",)),
    )(gu, gu, gu, da, da, w)


def _swap_halves(xt):
    half = HEAD_DIM // 2
    return jnp.concatenate([xt[half:], xt[:half]], axis=0)


def _rope(xt, cos, sin):
    return xt * cos + _swap_halves(xt) * sin


def _unrope(dxt, cos, sin):
    return dxt * cos - _swap_halves(dxt) * sin


def _key_query(count):
    kj = lax.broadcasted_iota(jnp.int32, (WINDOW, count * WINDOW), 0)
    qi = lax.broadcasted_iota(jnp.int32, (WINDOW, count * WINDOW), 1) & (WINDOW - 1)
    return kj, qi


def _band_masks(n, count):
    kj, qi = _key_query(count)
    return kj <= qi, jnp.logical_and(kj > qi, n > 0)


def _lanes(v, count):
    return jnp.concatenate([v] * count, axis=1) if count > 1 else v


def _heads(ref, h0, count):
    parts = [ref[(h0 + g) * HEAD_DIM:(h0 + g + 1) * HEAD_DIM, :] for g in range(count)]
    return jnp.concatenate(parts, axis=1) if count > 1 else parts[0]


def _head_rows(ref, h0, count):
    parts = [ref[h0 + g:h0 + g + 1, :] for g in range(count)]
    return jnp.concatenate(parts, axis=1) if count > 1 else parts[0]


def _head_sinks(sink_ref, h0, count):
    parts = [jnp.full((1, WINDOW), sink_ref[h0 + g], F32) for g in range(count)]
    return jnp.concatenate(parts, axis=1) if count > 1 else parts[0]


def _tn(a, b):
    return lax.dot_general(a, b, _DIMS["tn"], preferred_element_type=F32)


def _nt(a, b):
    return lax.dot_general(a, b, _DIMS["nt"], preferred_element_type=F32)


def _nn(a, b):
    return jnp.dot(a, b, preferred_element_type=F32)


def _attn_fwd(qkv, sinks, cos_t, sin_t, bsz, seq, name, hp):
    t, qw = qkv.shape
    d = qw * 2 // 3
    kvw = d // GROUP
    n_heads, n_kv = d // HEAD_DIM, kvw // HEAD_DIM
    nb = seq // WINDOW
    scale = HEAD_DIM ** -0.5

    def body(sink_ref, xc_ref, xp_ref, cc_ref, sc_ref, cp_ref, sp_ref, o_ref, lse_ref, xt_ref, pt_ref, ot_ref):
        n = pl.program_id(1)
        xt_ref[...] = xc_ref[...].T
        pt_ref[...] = xp_ref[:, d:].T
        cos_c, sin_c, cos_p, sin_p = cc_ref[...], sc_ref[...], cp_ref[...], sp_ref[...]
        cos_g, sin_g = _lanes(cos_c, hp), _lanes(sin_c, hp)
        valid_c, valid_p = _band_masks(n, hp)
        for j in range(n_kv):
            ko = j * HEAD_DIM
            kc = _rope(xt_ref[d + ko:d + ko + HEAD_DIM, :], cos_c, sin_c).astype(BF16)
            kp = _rope(pt_ref[ko:ko + HEAD_DIM, :], cos_p, sin_p).astype(BF16)
            vc = xt_ref[d + kvw + ko:d + kvw + ko + HEAD_DIM, :].astype(BF16)
            vp = pt_ref[kvw + ko:kvw + ko + HEAD_DIM, :].astype(BF16)
            for h0 in range(j * GROUP, (j + 1) * GROUP, hp):
                q = _rope(_heads(xt_ref, h0, hp), cos_g, sin_g).astype(BF16)
                sink = _head_sinks(sink_ref, h0, hp)
                s_c = jnp.where(valid_c, _tn(kc, q) * scale, NEG)
                s_p = jnp.where(valid_p, _tn(kp, q) * scale, NEG)
                m = jnp.maximum(jnp.maximum(jnp.max(s_c, axis=0, keepdims=True), jnp.max(s_p, axis=0, keepdims=True)), sink)
                p_c = jnp.exp(s_c - m)
                p_p = jnp.exp(s_p - m)
                den = jnp.sum(p_c, axis=0, keepdims=True) + jnp.sum(p_p, axis=0, keepdims=True) + jnp.exp(sink - m)
                inv = 1.0 / den
                o_g = _nn(vc, (p_c * inv).astype(BF16)) + _nn(vp, (p_p * inv).astype(BF16))
                lse_g = m + jnp.log(den)
                for g in range(hp):
                    h = h0 + g
                    ot_ref[h * HEAD_DIM:(h + 1) * HEAD_DIM, :] = o_g[:, g * WINDOW:(g + 1) * WINDOW]
                    lse_ref[h:h + 1, :] = lse_g[:, g * WINDOW:(g + 1) * WINDOW]
        o_ref[...] = ot_ref[...].T.astype(BF16)

    cur = lambda b, n: (b * nb + n, 0)
    prev = lambda b, n: (b * nb + jnp.maximum(n - 1, 0), 0)
    tab_c = pl.BlockSpec((HEAD_DIM, WINDOW), lambda b, n: (0, n))
    tab_p = pl.BlockSpec((HEAD_DIM, WINDOW), lambda b, n: (0, jnp.maximum(n - 1, 0)))
    return pl.pallas_call(
        body,
        name=name,
        grid=(bsz, nb),
        in_specs=[SMEM, pl.BlockSpec((WINDOW, qw), cur), pl.BlockSpec((WINDOW, qw), prev), tab_c, tab_c, tab_p, tab_p],
        out_specs=[pl.BlockSpec((WINDOW, d), cur), pl.BlockSpec((n_heads, WINDOW), lambda b, n: (0, b * nb + n))],
        out_shape=[jax.ShapeDtypeStruct((t, d), BF16), jax.ShapeDtypeStruct((n_heads, t), F32)],
        scratch_shapes=[pltpu.VMEM((qw, WINDOW), F32), pltpu.VMEM((2 * kvw, WINDOW), F32), pltpu.VMEM((d, WINDOW), F32)],
        compiler_params=_params(("parallel", "arbitrary")),
    )(sinks, qkv, qkv, cos_t, sin_t, cos_t, sin_t)


def _attn_bwd(qkv, o, lse, do, sinks, cos_t, sin_t, bsz, seq, name, hp):
    t, qw = qkv.shape
    d = qw * 2 // 3
    kvw = d // GROUP
    n_heads, n_kv = d // HEAD_DIM, kvw // HEAD_DIM
    nb = seq // WINDOW
    scale = HEAD_DIM ** -0.5

    def body(sink_ref, xc_ref, xp_ref, oc_ref, doc_ref, lc_ref, cc_ref, sc_ref, cp_ref, sp_ref,
             dx_ref, db_ref, dsk_ref, xt_ref, pt_ref, otc_ref, dtc_ref, gt_ref, carry_ref):
        b, n = pl.program_id(0), pl.program_id(1)
        live = n < nb
        xt_ref[...] = xc_ref[...].T
        pt_ref[...] = xp_ref[:, d:].T
        otc_ref[...] = oc_ref[...].astype(F32).T
        dtc_ref[...] = doc_ref[...].T
        cos_c, sin_c, cos_p, sin_p = cc_ref[...], sc_ref[...], cp_ref[...], sp_ref[...]
        cos_g, sin_g = _lanes(cos_c, hp), _lanes(sin_c, hp)
        kj, qi = _key_query(hp)
        valid_c = jnp.logical_and(kj <= qi, live)
        valid_p = jnp.logical_and(kj > qi, jnp.logical_and(n > 0, live))

        @pl.when(jnp.logical_and(b == 0, n == 0))
        def _():
            db_ref[...] = jnp.zeros_like(db_ref)
            dsk_ref[...] = jnp.zeros_like(dsk_ref)

        @pl.when(n == 0)
        def _():
            carry_ref[...] = jnp.zeros_like(carry_ref)

        for j in range(n_kv):
            ko = j * HEAD_DIM
            k_rows = slice(d + ko, d + ko + HEAD_DIM)
            v_rows = slice(d + kvw + ko, d + kvw + ko + HEAD_DIM)
            kc = _rope(xt_ref[k_rows, :], cos_c, sin_c).astype(BF16)
            kp = _rope(pt_ref[ko:ko + HEAD_DIM, :], cos_p, sin_p).astype(BF16)
            vc = xt_ref[v_rows, :].astype(BF16)
            vp = pt_ref[kvw + ko:kvw + ko + HEAD_DIM, :].astype(BF16)
            dk_c = jnp.zeros((HEAD_DIM, WINDOW), F32)
            dv_c = jnp.zeros((HEAD_DIM, WINDOW), F32)
            dk_p = jnp.zeros((HEAD_DIM, WINDOW), F32)
            dv_p = jnp.zeros((HEAD_DIM, WINDOW), F32)
            for h0 in range(j * GROUP, (j + 1) * GROUP, hp):
                q = _rope(_heads(xt_ref, h0, hp), cos_g, sin_g).astype(BF16)
                do_g = _heads(dtc_ref, h0, hp)
                do_b = do_g.astype(BF16)
                lse_g = _head_rows(lc_ref, h0, hp)
                delta = jnp.sum(_heads(otc_ref, h0, hp) * do_g, axis=0, keepdims=True)
                p_c = jnp.exp(jnp.where(valid_c, _tn(kc, q) * scale, NEG) - lse_g)
                p_p = jnp.exp(jnp.where(valid_p, _tn(kp, q) * scale, NEG) - lse_g)
                ds_c = (p_c * (_tn(vc, do_b) - delta)).astype(BF16)
                ds_p = (p_p * (_tn(vp, do_b) - delta)).astype(BF16)
                dq = _unrope((_nn(kc, ds_c) + _nn(kp, ds_p)) * scale, cos_g, sin_g)
                dsk = jnp.where(live, -jnp.exp(_head_sinks(sink_ref, h0, hp) - lse_g) * delta, 0.0)
                for g in range(hp):
                    rows = slice((h0 + g) * HEAD_DIM, (h0 + g + 1) * HEAD_DIM)
                    gt_ref[rows, :] = carry_ref[rows, :]
                    carry_ref[rows, :] = dq[:, g * WINDOW:(g + 1) * WINDOW]
                    dsk_ref[h0 + g:h0 + g + 1, :] += dsk[:, g * WINDOW:(g + 1) * WINDOW]
                dv_c += _nt(do_b, p_c.astype(BF16))
                dk_c += _nt(q, ds_c)
                dv_p += _nt(do_b, p_p.astype(BF16))
                dk_p += _nt(q, ds_p)
            gt_ref[k_rows, :] = _unrope((carry_ref[k_rows, :] + dk_p) * scale, cos_p, sin_p)
            gt_ref[v_rows, :] = carry_ref[v_rows, :] + dv_p
            carry_ref[k_rows, :] = dk_c
            carry_ref[v_rows, :] = dv_c
        dx = gt_ref[...].T
        dx_ref[...] = dx.astype(BF16)
        db_ref[...] += _fold8(dx)

    cur = lambda b, n: (b * nb + jnp.minimum(n, nb - 1), 0)
    prev = lambda b, n: (b * nb + jnp.maximum(jnp.minimum(n, nb - 1) - 1, 0), 0)
    done = lambda b, n: (b * nb + jnp.maximum(n - 1, 0), 0)
    stat_c = pl.BlockSpec((n_heads, WINDOW), lambda b, n: (0, b * nb + jnp.minimum(n, nb - 1)))
    tab_c = pl.BlockSpec((HEAD_DIM, WINDOW), lambda b, n: (0, jnp.minimum(n, nb - 1)))
    tab_p = pl.BlockSpec((HEAD_DIM, WINDOW), lambda b, n: (0, jnp.maximum(n - 1, 0)))
    return pl.pallas_call(
        body,
        name=name,
        grid=(bsz, nb + 1),
        in_specs=[SMEM, pl.BlockSpec((WINDOW, qw), cur), pl.BlockSpec((WINDOW, qw), prev),
                  pl.BlockSpec((WINDOW, d), cur), pl.BlockSpec((WINDOW, d), cur), stat_c, tab_c, tab_c, tab_p, tab_p],
        out_specs=[pl.BlockSpec((WINDOW, qw), done), pl.BlockSpec((8, qw), lambda b, n: (0, 0)),
                   pl.BlockSpec((n_heads, WINDOW), lambda b, n: (0, 0))],
        out_shape=[jax.ShapeDtypeStruct((t, qw), BF16), jax.ShapeDtypeStruct((8, qw), F32),
                   jax.ShapeDtypeStruct((n_heads, WINDOW), F32)],
        scratch_shapes=[pltpu.VMEM((qw, WINDOW), F32), pltpu.VMEM((2 * kvw, WINDOW), F32), pltpu.VMEM((d, WINDOW), F32),
                        pltpu.VMEM((d, WINDOW), F32), pltpu.VMEM((qw, WINDOW), F32), pltpu.VMEM((qw, WINDOW), F32)],
        compiler_params=_params(("arbitrary", "arbitrary")),
    )(sinks, qkv, qkv, o, do, lse, cos_t, sin_t, cos_t, sin_t)


def _place():
    return lax.axis_index("x"), lax.axis_index("y"), lax.axis_index("c")


def _other_chips(x, y):
    return [(1 - x, y), (x, 1 - y), (1 - x, 1 - y)]


def _place_shard(w, axis, q, name):
    ly, k, n = w.shape
    tr = _pick(k, (256, 128, 64, 32, 16, 8))
    steps = k // tr
    shape = (ly, k * N_CHIPS, n) if axis == 1 else (ly, k, n * N_CHIPS)
    if axis == 1:
        out_spec = pl.BlockSpec((None, tr, n), lambda l, i, q_ref: (l, q_ref[0] * steps + i, 0))
    else:
        out_spec = pl.BlockSpec((None, tr, n), lambda l, i, q_ref: (l, i, q_ref[0]))

    def body(q_ref, w_ref, o_ref):
        del q_ref
        o_ref[...] = w_ref[...].astype(BF16)

    return pl.pallas_call(
        body,
        name=name,
        grid_spec=pltpu.PrefetchScalarGridSpec(
            num_scalar_prefetch=1, grid=(ly, steps),
            in_specs=[pl.BlockSpec((None, tr, n), lambda l, i, q_ref: (l, i, 0))], out_specs=out_spec),
        out_shape=jax.ShapeDtypeStruct(shape, BF16),
        compiler_params=_params(("parallel", "parallel")),
    )(q, w)


def _half_block(ref, axis, layer, px, py, pc):
    blk = 2 * px + py
    if axis == 1:
        rows = ref.shape[1] // (2 * N_CHIPS)
        return ref.at[layer, pl.ds(pl.multiple_of((2 * blk + pc) * rows, 8), rows), :]
    rows, width = ref.shape[1] // 2, ref.shape[2] // N_CHIPS
    return ref.at[layer, pl.ds(pl.multiple_of(pc * rows, 8), rows), pl.ds(pl.multiple_of(blk * width, 128), width)]


def _gather_copy(refs, axes, pieces, send_sems, recv_sems, p, k, stage, whose):
    x, y, c = _place()
    chip = _other_chips(x, y)[k]
    i, layer = pieces[p]
    if stage == 0:
        origin = (x, y, c) if whose == "mine" else (*chip, c)
        to = (*chip, c)
    else:
        origin = (*chip, c) if whose == "mine" else (*chip, 1 - c)
        to = (x, y, 1 - c)
    blk = _half_block(refs[i], axes[i], layer, *origin)
    return pltpu.make_async_remote_copy(src_ref=blk, dst_ref=blk, send_sem=send_sems.at[p * 3 + k],
                                        recv_sem=recv_sems.at[p * 3 + k], device_id=to, device_id_type=MESH)


def _gather_weights(fulls, axes, pieces, name):
    n, m = len(fulls), 3 * len(pieces)

    def body(*refs):
        dst = refs[n:2 * n]
        sems = refs[2 * n:]
        todo = [(p, k) for p in range(len(pieces)) for k in range(3)]
        sends = [_gather_copy(dst, axes, pieces, sems[0], sems[1], p, k, 0, "mine") for p, k in todo]
        for cp in sends:
            cp.start()
        for p, k in todo:
            _gather_copy(dst, axes, pieces, sems[0], sems[1], p, k, 0, "theirs").wait_recv()
            sends.append(_gather_copy(dst, axes, pieces, sems[2], sems[3], p, k, 1, "mine"))
            sends[-1].start()
        for p, k in todo:
            _gather_copy(dst, axes, pieces, sems[2], sems[3], p, k, 1, "theirs").wait_recv()
        for cp in sends:
            cp.wait_send()

    return pl.pallas_call(
        body,
        name=name,
        in_specs=[ANY] * n,
        out_specs=[ANY] * n,
        out_shape=[jax.ShapeDtypeStruct(f.shape, f.dtype) for f in fulls],
        input_output_aliases={i: i for i in range(n)},
        scratch_shapes=[pltpu.SemaphoreType.DMA((m,))] * 4,
    )(*fulls)


HBM_SPEC = pl.BlockSpec(memory_space=pltpu.HBM)
SEM_SPEC = pl.BlockSpec(memory_space=pltpu.SEMAPHORE)


def _gather_start(fulls, axes, pieces, stage, name):
    n, m = len(fulls), 3 * len(pieces)

    def body(*refs):
        src = refs[:n]
        send_sems, recv_sems = refs[2 * n], refs[2 * n + 1]
        for p in range(len(pieces)):
            for k in range(3):
                _gather_copy(src, axes, pieces, send_sems, recv_sems, p, k, stage, "mine").start()

    out = pl.pallas_call(
        body,
        name=name,
        in_specs=[HBM_SPEC] * n,
        out_specs=[HBM_SPEC] * n + [SEM_SPEC, SEM_SPEC],
        out_shape=[pltpu.HBM(f.shape, f.dtype) for f in fulls] + [pltpu.SemaphoreType.DMA((m,)), pltpu.SemaphoreType.DMA((m,))],
        input_output_aliases={i: i for i in range(n)},
        compiler_params=pltpu.CompilerParams(has_side_effects=pltpu.SideEffectType.DATAFLOW_SIDE_EFFECTING),
    )(*[pltpu.with_memory_space_constraint(f, pltpu.HBM) for f in fulls])
    return list(out[:n]), out[n], out[n + 1]


def _gather_wait(fulls, send_sems, recv_sems, after, axes, pieces, stage, name):
    n = len(fulls)

    def body(*refs):
        src = refs[:n]
        s_sems, r_sems = refs[n], refs[n + 1]
        for p in range(len(pieces)):
            for k in range(3):
                _gather_copy(src, axes, pieces, s_sems, r_sems, p, k, stage, "mine").wait_send()
                _gather_copy(src, axes, pieces, s_sems, r_sems, p, k, stage, "theirs").wait_recv()

    out = pl.pallas_call(
        body,
        name=name,
        in_specs=[HBM_SPEC] * n + [SEM_SPEC, SEM_SPEC, ANY],
        out_specs=[HBM_SPEC] * n,
        out_shape=[pltpu.HBM(f.shape, f.dtype) for f in fulls],
        input_output_aliases={i: i for i in range(n)},
        compiler_params=pltpu.CompilerParams(has_side_effects=pltpu.SideEffectType.DATAFLOW_SIDE_EFFECTING),
    )(*fulls, send_sems, recv_sems, after)
    return list(out)


def _half_shape(kind, shape):
    if kind == "col":
        return (shape[0] // 2, shape[1])
    return (N_CHIPS, shape[1] // 2, shape[2])


def _half_of(kind, ref, h):
    if kind == "col":
        r = ref.shape[0] // 2
        return ref.at[pl.ds(pl.multiple_of(h * r, 8), r), :]
    r = ref.shape[1] // 2
    return ref.at[:, pl.ds(pl.multiple_of(h * r, 8), r), :]


def _pair_copy(src, dst, kinds, send_sems, recv_sems, i):
    x, y, c = _place()
    return pltpu.make_async_remote_copy(src_ref=_half_of(kinds[i], src[i], 1 - c), dst_ref=dst[i], send_sem=send_sems.at[i],
                                        recv_sem=recv_sems.at[i], device_id=(x, y, 1 - c), device_id_type=MESH)


def _pair_start(grads, kinds, token, name):
    n = len(grads)
    lands = [pltpu.HBM(_half_shape(kd, g.shape), g.dtype) for g, kd in zip(grads, kinds)]

    def body(*refs):
        src, dst = refs[:n], refs[2 * n + 2:3 * n + 2]
        send_sems, recv_sems = refs[3 * n + 2], refs[3 * n + 3]
        for i in range(n):
            _pair_copy(src, dst, kinds, send_sems, recv_sems, i).start()

    arrays = list(grads) + [token]
    out = pl.pallas_call(
        body,
        name=name,
        in_specs=[HBM_SPEC] * (n + 1),
        out_specs=[HBM_SPEC] * (2 * n + 1) + [SEM_SPEC, SEM_SPEC],
        out_shape=[pltpu.HBM(a.shape, a.dtype) for a in arrays] + lands + [pltpu.SemaphoreType.DMA((n,)), pltpu.SemaphoreType.DMA((n,))],
        input_output_aliases={i: i for i in range(n + 1)},
        compiler_params=pltpu.CompilerParams(has_side_effects=pltpu.SideEffectType.DATAFLOW_SIDE_EFFECTING),
    )(*[pltpu.with_memory_space_constraint(a, pltpu.HBM) for a in arrays])
    return list(out[:n]), out[n], list(out[n + 1:2 * n + 1]), out[2 * n + 1], out[2 * n + 2]


def _pair_wait(grads, kinds, token, lands, send_sems, recv_sems, name):
    n = len(grads)

    def body(*refs):
        src, dst = refs[:n], refs[n + 1:2 * n + 1]
        s_sems, r_sems = refs[2 * n + 1], refs[2 * n + 2]
        for i in range(n):
            cp = _pair_copy(src, dst, kinds, s_sems, r_sems, i)
            cp.wait_send()
            cp.wait_recv()

    arrays = list(grads) + [token] + list(lands)
    out = pl.pallas_call(
        body,
        name=name,
        in_specs=[HBM_SPEC] * (2 * n + 1) + [SEM_SPEC, SEM_SPEC],
        out_specs=[HBM_SPEC] * (2 * n + 1),
        out_shape=[pltpu.HBM(a.shape, a.dtype) for a in arrays],
        input_output_aliases={i: i for i in range(2 * n + 1)},
        compiler_params=pltpu.CompilerParams(has_side_effects=pltpu.SideEffectType.DATAFLOW_SIDE_EFFECTING),
    )(*arrays, send_sems, recv_sems)
    return list(out[:n]), out[n], list(out[n + 1:])


def _pair_sum(grad, recv, kind, c, own, layer, name):
    r, cols = own.shape[2:]
    if kind == "col":
        tr = _pick(r, (256, 128, 64, 32, 16, 8))
        steps = r // tr
        grid = (N_CHIPS, steps)
        g_spec = pl.BlockSpec((tr, cols), lambda s, i, c_ref: (c_ref[0] * steps + i, s))
        r_spec = pl.BlockSpec((tr, cols), lambda s, i, c_ref: (i, s))
        o_spec = pl.BlockSpec((None, None, tr, cols), lambda s, i, c_ref: (layer, s, i, 0))
        g_in = grad
    else:
        grid = (N_CHIPS, 1)
        g_spec = pl.BlockSpec((None, None, r, cols), lambda s, i, c_ref: (s, c_ref[0], 0, 0))
        r_spec = pl.BlockSpec((None, r, cols), lambda s, i, c_ref: (s, 0, 0))
        o_spec = pl.BlockSpec((None, None, r, cols), lambda s, i, c_ref: (layer, s, 0, 0))
        g_in = grad.reshape(N_CHIPS, 2, r, cols)

    def body(c_ref, g_ref, r_ref, own_ref, o_ref):
        del c_ref, own_ref
        o_ref[...] = (g_ref[...].astype(F32) + r_ref[...].astype(F32)).astype(o_ref.dtype)

    return pl.pallas_call(
        body,
        name=name,
        grid_spec=pltpu.PrefetchScalarGridSpec(num_scalar_prefetch=1, grid=grid, in_specs=[g_spec, r_spec, ANY], out_specs=o_spec),
        out_shape=jax.ShapeDtypeStruct(own.shape, own.dtype),
        input_output_aliases={3: 0},
        compiler_params=_params(("parallel", "parallel")),
    )(c, g_in, recv, own)


def _scatter_copy(own, mine, sib, pieces, send_sems, recv_sems, p, k, stage, whose):
    x, y, c = _place()
    q = 2 * x + y
    i, layer = pieces[p]
    per = 4 if stage == 0 else 3
    if k == 3:
        src, dst, to = own[i].at[layer, q], sib[i].at[layer, q], (x, y, 1 - c)
    else:
        chip = _other_chips(x, y)[k]
        slot = 2 * chip[0] + chip[1]
        if stage == 0:
            to = (*chip, c)
            src, dst = (own[i].at[layer, slot], mine[i].at[layer, q]) if whose == "mine" else (own[i].at[layer, q], mine[i].at[layer, slot])
        else:
            to = (x, y, 1 - c)
            src, dst = mine[i].at[layer, slot], sib[i].at[layer, slot]
    return pltpu.make_async_remote_copy(src_ref=src, dst_ref=dst, send_sem=send_sems.at[p * per + k],
                                        recv_sem=recv_sems.at[p * per + k], device_id=to, device_id_type=MESH)


def _scatter_start(own, mine, sib, token, pieces, stage, name):
    n = len(own)
    per = 4 if stage == 0 else 3
    m = per * len(pieces)
    n_arr = 3 * n + 1

    def body(*refs):
        o, mi, si = refs[:n], refs[n:2 * n], refs[2 * n:3 * n]
        send_sems, recv_sems = refs[2 * n_arr], refs[2 * n_arr + 1]
        for p in range(len(pieces)):
            for k in range(per):
                _scatter_copy(o, mi, si, pieces, send_sems, recv_sems, p, k, stage, "mine").start()

    arrays = list(own) + list(mine) + list(sib) + [token]
    out = pl.pallas_call(
        body,
        name=name,
        in_specs=[HBM_SPEC] * n_arr,
        out_specs=[HBM_SPEC] * n_arr + [SEM_SPEC, SEM_SPEC],
        out_shape=[pltpu.HBM(a.shape, a.dtype) for a in arrays] + [pltpu.SemaphoreType.DMA((m,)), pltpu.SemaphoreType.DMA((m,))],
        input_output_aliases={i: i for i in range(n_arr)},
        compiler_params=pltpu.CompilerParams(has_side_effects=pltpu.SideEffectType.DATAFLOW_SIDE_EFFECTING),
    )(*[pltpu.with_memory_space_constraint(a, pltpu.HBM) for a in arrays])
    return list(out[:n]), list(out[n:2 * n]), list(out[2 * n:3 * n]), out[3 * n], out[n_arr], out[n_arr + 1]


def _scatter_wait(own, mine, sib, token, send_sems, recv_sems, pieces, stage, name):
    n = len(own)
    per = 4 if stage == 0 else 3
    n_arr = 3 * n + 1

    def body(*refs):
        o, mi, si = refs[:n], refs[n:2 * n], refs[2 * n:3 * n]
        s_sems, r_sems = refs[n_arr], refs[n_arr + 1]
        for p in range(len(pieces)):
            for k in range(per):
                _scatter_copy(o, mi, si, pieces, s_sems, r_sems, p, k, stage, "mine").wait_send()
                _scatter_copy(o, mi, si, pieces, s_sems, r_sems, p, k, stage, "theirs").wait_recv()

    arrays = list(own) + list(mine) + list(sib) + [token]
    out = pl.pallas_call(
        body,
        name=name,
        in_specs=[HBM_SPEC] * n_arr + [SEM_SPEC, SEM_SPEC],
        out_specs=[HBM_SPEC] * n_arr,
        out_shape=[pltpu.HBM(a.shape, a.dtype) for a in arrays],
        input_output_aliases={i: i for i in range(n_arr)},
        compiler_params=pltpu.CompilerParams(has_side_effects=pltpu.SideEffectType.DATAFLOW_SIDE_EFFECTING),
    )(*arrays, send_sems, recv_sems)
    return list(out[:n]), list(out[n:2 * n]), list(out[2 * n:3 * n]), out[3 * n]


def _reduce_adamw(own, mine, sib, w, m, v, qc, name):
    ly, _, r, cols = mine.shape
    tr = _pick(r, (128, 64, 32, 16, 8))
    steps = r // tr
    c1 = 1.0 - ADAM_B1 ** ADAM_STEP
    c2 = 1.0 - ADAM_B2 ** ADAM_STEP

    def body(qc_ref, own_ref, mine_ref, sib_ref, w_ref, m_ref, v_ref, g_ref, d_ref, nm_ref, nv_ref):
        q = qc_ref[0]
        mine_sum = sib_sum = None
        for s in range(N_CHIPS):
            a = jnp.where(q == s, own_ref[...], mine_ref[s]).astype(F32)
            b = sib_ref[s].astype(F32)
            mine_sum = a if s == 0 else mine_sum + a
            sib_sum = b if s == 0 else sib_sum + b
        gv = jnp.where(pl.program_id(1) == qc_ref[1], mine_sum, sib_sum)
        nm = ADAM_B1 * m_ref[...] + (1.0 - ADAM_B1) * gv
        nv = ADAM_B2 * v_ref[...] + (1.0 - ADAM_B2) * (gv * gv)
        g_ref[...] = gv
        d_ref[...] = -ADAM_LR * ((nm / c1) / (jnp.sqrt(nv / c2) + ADAM_EPS) + ADAM_WD * w_ref[...])
        nm_ref[...] = nm
        nv_ref[...] = nv

    def mine_rows(h, i, qc_ref):
        return jnp.where(h == qc_ref[1], i, 0)

    def sib_rows(h, i, qc_ref):
        return jnp.where(h == qc_ref[1], 0, i)

    own_spec = pl.BlockSpec((None, None, tr, cols), lambda l, h, i, qc_ref: (l, qc_ref[0], mine_rows(h, i, qc_ref), 0))
    mine_spec = pl.BlockSpec((None, N_CHIPS, tr, cols), lambda l, h, i, qc_ref: (l, 0, mine_rows(h, i, qc_ref), 0))
    sib_spec = pl.BlockSpec((None, N_CHIPS, tr, cols), lambda l, h, i, qc_ref: (l, 0, sib_rows(h, i, qc_ref), 0))
    spec = pl.BlockSpec((None, tr, cols), lambda l, h, i, qc_ref: (l, h * steps + i, 0))
    shp = jax.ShapeDtypeStruct(w.shape, F32)
    return pl.pallas_call(
        body,
        name=name,
        grid_spec=pltpu.PrefetchScalarGridSpec(
            num_scalar_prefetch=1, grid=(ly, N_CORES, steps),
            in_specs=[own_spec, mine_spec, sib_spec, spec, spec, spec], out_specs=[spec] * 4),
        out_shape=[shp] * 4,
        compiler_params=_params(("parallel", "parallel", "parallel")),
    )(qc, own, mine, sib, w, m, v)


def _allreduce_small(v, name):
    r, w = v.shape

    def body(v_ref, o_ref, buf_ref, send_sems, recv_sems):
        x, y, c = _place()
        me = 4 * x + 2 * y + c

        def peer(k):
            return x ^ (k >> 2), y ^ ((k >> 1) & 1), c ^ (k & 1)

        def remote(k, slot):
            return pltpu.make_async_remote_copy(
                src_ref=v_ref, dst_ref=buf_ref.at[slot], send_sem=send_sems.at[k - 1], recv_sem=recv_sems.at[k - 1],
                device_id=peer(k), device_id_type=MESH)

        sends = [remote(k, me) for k in range(1, N_DEV)]
        for cp in sends:
            cp.start()
        buf_ref[me] = v_ref[...]
        for k in range(1, N_DEV):
            px, py, pc = peer(k)
            remote(k, 4 * px + 2 * py + pc).wait_recv()
        for cp in sends:
            cp.wait_send()
        acc = buf_ref[0]
        for dev in range(1, N_DEV):
            acc = acc + buf_ref[dev]
        o_ref[...] = acc

    vm = pl.BlockSpec(memory_space=pltpu.VMEM)
    return pl.pallas_call(
        body,
        name=name,
        in_specs=[vm],
        out_specs=vm,
        out_shape=jax.ShapeDtypeStruct((r, w), F32),
        scratch_shapes=[pltpu.VMEM((N_DEV, r, w), F32), pltpu.SemaphoreType.DMA((N_DEV - 1,)), pltpu.SemaphoreType.DMA((N_DEV - 1,))],
        compiler_params=pltpu.CompilerParams(vmem_limit_bytes=VMEM_LIMIT_BYTES),
    )(v)


def _adamw(w, g, m, v, name):
    ly, r, c = w.shape
    tr = _pick(r, (256, 128, 64, 32, 16, 8))
    c1 = 1.0 - ADAM_B1 ** ADAM_STEP
    c2 = 1.0 - ADAM_B2 ** ADAM_STEP

    def body(w_ref, g_ref, m_ref, v_ref, d_ref, nm_ref, nv_ref):
        gv = g_ref[...]
        nm = ADAM_B1 * m_ref[...] + (1.0 - ADAM_B1) * gv
        nv = ADAM_B2 * v_ref[...] + (1.0 - ADAM_B2) * (gv * gv)
        d_ref[...] = -ADAM_LR * ((nm / c1) / (jnp.sqrt(nv / c2) + ADAM_EPS) + ADAM_WD * w_ref[...])
        nm_ref[...] = nm
        nv_ref[...] = nv

    spec = pl.BlockSpec((None, tr, c), lambda l, i: (l, i, 0))
    shp = jax.ShapeDtypeStruct((ly, r, c), F32)
    return pl.pallas_call(
        body,
        name=name,
        grid=(ly, r // tr),
        in_specs=[spec] * 4,
        out_specs=[spec] * 3,
        out_shape=[shp] * 3,
        compiler_params=_params(("parallel", "parallel")),
    )(w, g, m, v)


def _rope_tables(seq):
    pos = jnp.arange(seq, dtype=F32)
    inv_freq = 1.0 / (ROPE_THETA ** (jnp.arange(0, HEAD_DIM, 2, dtype=F32) / HEAD_DIM))
    ang = (pos[:, None] * inv_freq[None, :]).T
    cos, sin = jnp.cos(ang), jnp.sin(ang)
    return jnp.concatenate([cos, cos], axis=0), jnp.concatenate([-sin, sin], axis=0)


def _pack(vs, fill=0.0):
    p = jnp.concatenate([v.reshape(-1) for v in vs])
    size = -(-p.shape[0] // 8192) * 8192
    return jnp.pad(p, (0, size - p.shape[0]), constant_values=fill).reshape(-1, 1024)


def _unpack(p, like):
    p = p.reshape(-1)
    out, o = [], 0
    for v in like:
        n = int(math.prod(v.shape))
        out.append(p[o:o + n].reshape(v.shape))
        o += n
    return out


def kernel(x, norm_mix, norm_ffn, norm_final, conv_w_in, conv_w_conv, conv_w_out, attn_w_qkv, attn_b_qkv, attn_sinks, attn_w_o, attn_b_o, ffn_w_in, ffn_w_conv, ffn_w_down, loss_target, m_norm_mix, m_norm_ffn, m_norm_final, m_conv_w_in, m_conv_w_conv, m_conv_w_out, m_attn_w_qkv, m_attn_b_qkv, m_attn_sinks, m_attn_w_o, m_attn_b_o, m_ffn_w_in, m_ffn_w_conv, m_ffn_w_down, v_norm_mix, v_norm_ffn, v_norm_final, v_conv_w_in, v_conv_w_conv, v_conv_w_out, v_attn_w_qkv, v_attn_b_qkv, v_attn_sinks, v_attn_w_o, v_attn_b_o, v_ffn_w_in, v_ffn_w_conv, v_ffn_w_down):
    bsz, seq, d = x.shape
    t = bsz * seq
    depth = norm_mix.shape[0]
    n_conv, n_attn = conv_w_in.shape[0], attn_w_qkv.shape[0]
    xq, yq, cq = _place()
    q = 2 * xq + yq

    big = [conv_w_in, conv_w_out, attn_w_qkv, attn_w_o, ffn_w_in, ffn_w_down]
    axes = [2, 1, 2, 1, 2, 1]
    q_arr = q.astype(jnp.int32).reshape(1)
    c_arr = cq.astype(jnp.int32).reshape(1)
    weights = [_place_shard(w, ax, q_arr, f"place_shard{n}") for n, (w, ax) in enumerate(zip(big, axes))]

    def pieces_of(i):
        return [(0, i // 2), (1, i // 2), (4, i), (5, i)] if i % 2 == 0 else [(2, i // 2), (3, i // 2), (4, i), (5, i)]

    weights = _gather_weights(weights, axes, pieces_of(0)[:2], "gather_mixer0")

    small_cols = [conv_w_conv, attn_b_qkv, attn_b_o, ffn_w_conv]

    def placed(v):
        width = v.shape[-1]
        full = jnp.zeros(v.shape[:-1] + (N_CHIPS * width,), F32)
        return lax.dynamic_update_slice_in_dim(full, v * (1.0 / N_CORES), q * width, axis=v.ndim - 1)

    full_cols = [placed(v) for v in small_cols]
    wc_conv, b_qkv, b_o, wf_conv = _unpack(_allreduce_small(_pack(full_cols), "gather_small"), full_cols)
    cos_t, sin_t = _rope_tables(seq)

    xs = x.reshape(t, d)
    saved = []
    sems = {}

    def fetch(pieces, stage, tag):
        nonlocal weights
        weights, sems[tag, 0], sems[tag, 1] = _gather_start(weights, axes, pieces, stage, f"gather_{'ici' if stage == 0 else 'pass'}_start{tag}")

    def settle(pieces, stage, tag, after):
        nonlocal weights
        weights = _gather_wait(weights, sems[tag, 0], sems[tag, 1], after, axes, pieces, stage,
                               f"gather_{'ici' if stage == 0 else 'pass'}_wait{tag}")

    for i in range(depth):
        j = i // 2
        ahead = pieces_of(i + 1) if i + 1 < depth else None
        if i == 0:
            fetch(pieces_of(0)[2:], 0, "0")
        elif ahead:
            fetch(ahead, 0, str(i + 1))
        w_cin, w_cout, w_qkv, w_o, w_fin, w_fdown = weights
        if i == 0:
            h = _rms_fwd(xs, norm_mix[0:1], "norm_mix_fwd0")
        g_ffn = norm_ffn[i:i + 1]
        if i % 2 == 0:
            pre = _mm(h, w_cin, "nn", BF16, layer=j, tm=1024, tn=768, tk=4096, name=f"conv_in_fwd{i}")
            mixed = _convgate_fwd(pre, wc_conv[j], seq, f"conv_gate_fwd{i}")
            if i == 0:
                settle(pieces_of(0)[2:], 0, "0", mixed)
                fetch(pieces_of(0)[2:], 1, "0")
                w_cin, w_cout, w_qkv, w_o, w_fin, w_fdown = weights
            x_mid, h2 = _mm(mixed, w_cout, "nn", F32, layer=j, residual=xs, norm_g=g_ffn, tm=512, tn=1024, tk=4096,
                            name=f"conv_out_fwd{i}")
            lse = None
        else:
            pre = _mm(h, w_qkv, "nn", F32, layer=j, bias=b_qkv[j:j + 1], tm=1024, tn=768, tk=4096, name=f"qkv_fwd{i}")
            mixed, lse = _attn_fwd(pre, attn_sinks[j], cos_t, sin_t, bsz, seq, f"attn_fwd{i}", hp=1)
            x_mid, h2 = _mm(mixed, w_o, "nn", F32, layer=j, bias=b_o[j:j + 1], residual=xs, norm_g=g_ffn, tm=512, tn=1024,
                            tk=4096, name=f"attn_out_fwd{i}")
        if i == 0:
            settle(pieces_of(0)[2:], 1, "0", h2)
            fetch(ahead, 0, "1")
        elif ahead:
            settle(ahead, 0, str(i + 1), x_mid)
            fetch(ahead, 1, str(i + 1))
        w_cin, w_cout, w_qkv, w_o, w_fin, w_fdown = weights
        gu = _mm(h2, w_fin, "nn", BF16, layer=i, n_outer=True, tm=1024, tn=2816, tk=4096, name=f"ffn_in_fwd{i}")
        if i == 0:
            settle(ahead, 0, "1", gu)
            fetch(ahead, 1, "1")
            w_cin, w_cout, w_qkv, w_o, w_fin, w_fdown = weights
        g_next = norm_mix[i + 1:i + 2] if i + 1 < depth else norm_final.reshape(1, d)
        x_next, act, h_next = _ffn_gate_down_fwd(gu, wf_conv[i], w_fdown, i, x_mid, g_next, seq, f"ffn_gate_down_fwd{i}")
        if ahead:
            settle(ahead, 1, str(i + 1), x_next)
        saved.append((xs, h, pre, mixed, lse, x_mid, h2, gu, act))
        xs, h = x_next, h_next
    w_cin, w_cout, w_qkv, w_o, w_fin, w_fdown = weights

    dx, dxb, sq, dg_final = _loss_head(xs, loss_target.reshape(t, d), norm_final.reshape(1, d), "loss_head")
    loss = lax.psum(0.5 * jnp.sum(sq) / d, ("x", "y", "c"))

    g_norm_mix, g_norm_ffn = [None] * depth, [None] * depth
    g_cin, g_cconv, g_cout = [None] * n_conv, [None] * n_conv, [None] * n_conv
    g_qkv, g_bqkv, g_sinks, g_o, g_bo = ([None] * n_attn for _ in range(5))
    g_fin, g_fconv, g_fdown = [None] * depth, [None] * depth, [None] * depth

    kinds6 = ["col", "row", "col", "row", "col", "row"]
    layers6 = [n_conv, n_conv, n_attn, n_attn, depth, depth]
    big_w = [conv_w_in, conv_w_out, attn_w_qkv, attn_w_o, ffn_w_in, ffn_w_down]

    def slot_stack(n):
        k, cols = big_w[n].shape[1], big_w[n].shape[2]
        r = k // 2
        return lax.empty((layers6[n], N_CHIPS, r, cols), BF16)

    own = [slot_stack(n) for n in range(6)]
    mine = [slot_stack(n) for n in range(6)]
    sib = [slot_stack(n) for n in range(6)]
    flight = {}

    def group(i, part):
        return f"{part}{i}", (pieces_of(i)[:2] if part == "m" else pieces_of(i)[2:])

    def scatter(grp, stage, action, token):
        tag, pieces = grp
        ts = [ti for ti, _ in pieces]
        local = [(n, l) for n, (_, l) in enumerate(pieces)]
        sub = ([own[ti] for ti in ts], [mine[ti] for ti in ts], [sib[ti] for ti in ts], token)
        label = f"grad_{'ici' if stage == 0 else 'pass'}_{action}_{tag}"
        if action == "start":
            o, mi, si, token, s_sems, r_sems = _scatter_start(*sub, local, stage, label)
            flight[tag] = (s_sems, r_sems)
        else:
            o, mi, si, token = _scatter_wait(*sub, *flight[tag], local, stage, label)
        for n, ti in enumerate(ts):
            own[ti], mine[ti], sib[ti] = o[n], mi[n], si[n]
        return token

    def pair_begin(grp, token):
        tag, pieces = grp
        grads = {0: g_cin, 1: g_cout, 2: g_qkv, 3: g_o, 4: g_fin, 5: g_fdown}
        parts, kinds = [], []
        for ti, l in pieces:
            g = grads[ti][l]
            parts.append(g if kinds6[ti] == "col" else g.reshape(N_CHIPS, g.shape[0] // N_CHIPS, g.shape[1]))
            kinds.append(kinds6[ti])
        parts, token, lands, s_sems, r_sems = _pair_start(parts, kinds, token, f"grad_pair_start_{tag}")
        flight["pair" + tag] = (parts, kinds, lands, s_sems, r_sems)
        return token

    def pair_finish(grp, token):
        tag, pieces = grp
        parts, kinds, lands, s_sems, r_sems = flight["pair" + tag]
        parts, token, recv = _pair_wait(parts, kinds, token, lands, s_sems, r_sems, f"grad_pair_wait_{tag}")
        for (ti, l), g, r in zip(pieces, parts, recv):
            own[ti] = _pair_sum(g, r, kinds6[ti], c_arr, own[ti], l, f"grad_pair_sum_{tag}_{ti}")
        return scatter(grp, 0, "start", token)

    for i in reversed(range(depth)):
        j = i // 2
        x_in, h, pre, mixed, lse, x_mid, h2, gu, act = saved[i]
        da = _mm(dxb, w_fdown, "nt", BF16, layer=i, n_outer=True, tm=1024, tn=2816, tk=4096, name=f"ffn_down_dx{i}")
        g_fdown[i] = _mm(act, dxb, "tn", BF16, tm=1408, tn=1024, tk=2048, name=f"ffn_down_dw{i}")
        dgu, dwc = _ffngate_bwd(gu, da, wf_conv[i], seq, f"ffn_gate_bwd{i}")
        g_fconv[i] = dwc[:3]
        g_fin[i] = _mm(h2, dgu, "tn", BF16, tm=1024, tn=1408, tk=2048, name=f"ffn_in_dw{i}")
        dgu = pair_begin(group(i, "f"), dgu)
        dx, dxb, dg, colsum = _rms_bwd(x_mid, dgu, w_fin, i, norm_ffn[i:i + 1], dx, f"ffn_in_dx_norm_bwd{i}")
        g_norm_ffn[i] = jnp.sum(dg, axis=0)
        if i + 1 < depth:
            dxb = scatter(group(i + 1, "m"), 1, "start", scatter(group(i + 1, "m"), 0, "wait", dxb))
            dxb = scatter(group(i + 1, "f"), 1, "wait", dxb)
        dxb = pair_finish(group(i, "f"), dxb)
        if i % 2 == 0:
            dmix = _mm(dxb, w_cout, "nt", BF16, layer=j, tm=512, tn=1024, tk=4096, name=f"conv_out_dx{i}")
            g_cout[j] = _mm(mixed, dxb, "tn", BF16, tm=1024, tn=1024, tk=2048, name=f"conv_out_dw{i}")
            dpre, dwc = _convgate_bwd(pre, dmix, wc_conv[j], seq, f"conv_gate_bwd{i}")
            g_cconv[j] = dwc[:3]
            g_cin[j] = _mm(h, dpre, "tn", BF16, tm=1024, tn=1536, tk=2048, name=f"conv_in_dw{i}")
            w_pre = w_cin
        else:
            g_bo[j] = jnp.sum(colsum, axis=0)
            dmix = _mm(dxb, w_o, "nt", F32, layer=j, tm=512, tn=1024, tk=4096, name=f"attn_out_dx{i}")
            g_o[j] = _mm(mixed, dxb, "tn", BF16, tm=1024, tn=1024, tk=2048, name=f"attn_out_dw{i}")
            dpre, dbias, dsk = _attn_bwd(pre, mixed, lse, dmix, attn_sinks[j], cos_t, sin_t, bsz, seq, f"attn_bwd{i}",
                                         hp=GROUP)
            g_bqkv[j] = jnp.sum(dbias, axis=0)
            g_sinks[j] = jnp.sum(dsk, axis=1)
            g_qkv[j] = _mm(h, dpre, "tn", BF16, tm=1024, tn=1536, tk=2048, name=f"qkv_dw{i}")
            w_pre = w_qkv
        dpre = pair_begin(group(i, "m"), dpre)
        dx, dxb, dg, _ = _rms_bwd(x_in, dpre, w_pre, j, norm_mix[i:i + 1], dx, f"mixer_in_dx_norm_bwd{i}")
        g_norm_mix[i] = jnp.sum(dg, axis=0)
        dxb = scatter(group(i, "f"), 1, "start", scatter(group(i, "f"), 0, "wait", dxb))
        if i + 1 < depth:
            dxb = scatter(group(i + 1, "m"), 1, "wait", dxb)
        dxb = pair_finish(group(i, "m"), dxb)
    grad_x = dx.reshape(bsz, seq, d)

    scatter(group(0, "f"), 1, "wait", dxb)
    big_m = [m_conv_w_in, m_conv_w_out, m_attn_w_qkv, m_attn_w_o, m_ffn_w_in, m_ffn_w_down]
    big_v = [v_conv_w_in, v_conv_w_out, v_attn_w_qkv, v_attn_w_o, v_ffn_w_in, v_ffn_w_down]
    big_names = ["conv_w_in", "conv_w_out", "attn_w_qkv", "attn_w_o", "ffn_w_in", "ffn_w_down"]
    qc_arr = jnp.stack([q, cq]).astype(jnp.int32)

    def adamw_of(n):
        return list(_reduce_adamw(own[n], mine[n], sib[n], big_w[n], big_m[n], big_v[n], qc_arr, f"adamw_{big_names[n]}"))

    big_upd = [None] * 6
    for n in (2, 3, 4, 5):
        big_upd[n] = adamw_of(n)
    token = scatter(group(0, "m"), 1, "start", scatter(group(0, "m"), 0, "wait", big_upd[5][1]))
    big_upd[5][1] = scatter(group(0, "m"), 1, "wait", token)
    for n in (0, 1):
        big_upd[n] = adamw_of(n)

    small = [jnp.stack(g_norm_mix), jnp.stack(g_norm_ffn), jnp.sum(dg_final, axis=0), jnp.stack(g_cconv),
             jnp.stack(g_bqkv), jnp.stack(g_sinks), jnp.stack(g_bo), jnp.stack(g_fconv)]
    sg = _unpack(_allreduce_small(_pack(small), "grad_small_allreduce"), small)

    def my_cols(v, like):
        width = like.shape[-1]
        return lax.dynamic_slice_in_dim(v, q * width, width, axis=v.ndim - 1)

    small_w = [norm_mix, norm_ffn, norm_final, conv_w_conv, attn_b_qkv, attn_sinks, attn_b_o, ffn_w_conv]
    small_m = [m_norm_mix, m_norm_ffn, m_norm_final, m_conv_w_conv, m_attn_b_qkv, m_attn_sinks, m_attn_b_o, m_ffn_w_conv]
    small_v = [v_norm_mix, v_norm_ffn, v_norm_final, v_conv_w_conv, v_attn_b_qkv, v_attn_sinks, v_attn_b_o, v_ffn_w_conv]
    small_g = [sg[0], sg[1], sg[2], my_cols(sg[3], conv_w_conv), my_cols(sg[4], attn_b_qkv), sg[5],
               my_cols(sg[6], attn_b_o), my_cols(sg[7], ffn_w_conv)]

    upd = {nm: tuple(u[1:]) for nm, u in zip(big_names, big_upd)}
    sd, sm, sv = _adamw(_pack(small_w)[None], _pack(small_g)[None], _pack(small_m)[None], _pack(small_v, 1.0)[None],
                        "adamw_small")
    sd, sm, sv = _unpack(sd, small_w), _unpack(sm, small_w), _unpack(sv, small_w)
    names = ["norm_mix", "norm_ffn", "norm_final", "conv_w_in", "conv_w_conv", "conv_w_out", "attn_w_qkv", "attn_b_qkv",
             "attn_sinks", "attn_w_o", "attn_b_o", "ffn_w_in", "ffn_w_conv", "ffn_w_down"]
    small_names = ["norm_mix", "norm_ffn", "norm_final", "conv_w_conv", "attn_b_qkv", "attn_sinks", "attn_b_o", "ffn_w_conv"]
    grads = dict(zip(small_names, small_g))
    grads.update({nm: u[0] for nm, u in zip(big_names, big_upd)})
    for n, nm in enumerate(small_names):
        upd[nm] = (sd[n], sm[n], sv[n])
    return (loss, grad_x, *[grads[nm] for nm in names], *[upd[nm][0] for nm in names],
            *[upd[nm][1] for nm in names], *[upd[nm][2] for nm in names])
```

```python
import math

import jax
import jax.numpy as jnp
from jax import lax
from jax.experimental import pallas as pl
from jax.experimental.pallas import tpu as pltpu

F32 = jnp.float32
BF16 = jnp.bfloat16

HEAD_DIM = 64
GROUP = 4
WINDOW = 128
EPS = 1e-5
ROPE_THETA = 10000.0
ADAM_LR, ADAM_B1, ADAM_B2, ADAM_EPS, ADAM_WD, ADAM_STEP = 0.001, 0.9, 0.999, 1e-08, 0.01, 10

N_CHIPS = 4
N_CORES = 2
N_DEV = 8
HALO = 16
VMEM_LIMIT_BYTES = 56 * 1024 * 1024
MESH = pl.DeviceIdType.MESH
ANY = pl.BlockSpec(memory_space=pl.ANY)
SMEM = pl.BlockSpec(memory_space=pltpu.SMEM)
NEG = float(jnp.finfo(jnp.float32).min)
ROW_TILES = (512, 256, 128, 64, 32, 16, 8)


def _pick(dim, cands):
    for c in cands:
        if dim % c == 0:
            return c
    return dim


def _params(sem):
    return pltpu.CompilerParams(dimension_semantics=sem, vmem_limit_bytes=VMEM_LIMIT_BYTES)


_DIMS = {"nn": (((1,), (0,)), ((), ())), "nt": (((1,), (1,)), ((), ())), "tn": (((0,), (0,)), ((), ()))}


def _mm(a, b, mode, out_dtype, *, layer=None, bias=None, residual=None, norm_g=None, n_outer=False, tm, tn, tk, name):
    b2 = b.shape[1:] if layer is not None else b.shape
    if mode == "nn":
        (m, k), n = a.shape, b2[1]
    elif mode == "nt":
        (m, k), n = a.shape, b2[0]
    else:
        (k, m), n = a.shape, b2[1]
    tm, tn, tk = min(tm, m), min(tn, n), min(tk, k)
    assert m % tm == 0 and n % tn == 0 and k % tk == 0, (name, a.shape, b.shape, tm, tn, tk)
    nk = k // tk

    def at(f):
        return (lambda p0, p1, p2: f(p1, p0, p2)) if n_outer else f

    a_spec = pl.BlockSpec((tk, tm), at(lambda i, j, l: (l, i))) if mode == "tn" else pl.BlockSpec((tm, tk), at(lambda i, j, l: (i, l)))
    if layer is None:
        b_spec = (pl.BlockSpec((tn, tk), at(lambda i, j, l: (j, l))) if mode == "nt"
                  else pl.BlockSpec((tk, tn), at(lambda i, j, l: (l, j))))
    elif mode == "nt":
        b_spec = pl.BlockSpec((None, tn, tk), at(lambda i, j, l: (layer, j, l)))
    else:
        b_spec = pl.BlockSpec((None, tk, tn), at(lambda i, j, l: (layer, l, j)))
    in_specs, args = [a_spec, b_spec], [a, b]
    if bias is not None:
        in_specs.append(pl.BlockSpec((1, tn), at(lambda i, j, l: (0, j))))
        args.append(bias)
    if residual is not None:
        in_specs.append(pl.BlockSpec((tm, tn), at(lambda i, j, l: (i, j))))
        args.append(residual)
    if norm_g is not None:
        assert tn == n, (name, "the RMSNorm of the result needs whole rows in a tile")
        in_specs.append(pl.BlockSpec((1, tn), at(lambda i, j, l: (0, j))))
        args.append(norm_g)
    has_bias, has_res, has_norm = bias is not None, residual is not None, norm_g is not None

    def body(*refs):
        a_ref, b_ref = refs[0], refs[1]
        pos = 2
        bias_ref = res_ref = g_ref = h_ref = None
        if has_bias:
            bias_ref, pos = refs[pos], pos + 1
        if has_res:
            res_ref, pos = refs[pos], pos + 1
        if has_norm:
            g_ref, pos = refs[pos], pos + 1
        o_ref, pos = refs[pos], pos + 1
        if has_norm:
            h_ref, pos = refs[pos], pos + 1
        acc_ref = refs[pos] if nk > 1 else None

        def finish(acc):
            if has_bias:
                acc = acc + bias_ref[...]
            if has_res:
                acc = acc + res_ref[...]
            o_ref[...] = acc.astype(o_ref.dtype)
            if has_norm:
                h_ref[...] = _rms(acc, g_ref[...]).astype(BF16)

        if nk == 1:
            finish(lax.dot_general(a_ref[...], b_ref[...], _DIMS[mode], preferred_element_type=F32))
            return
        l = pl.program_id(2)
        part = lax.dot_general(a_ref[...], b_ref[...], _DIMS[mode], preferred_element_type=F32)

        @pl.when(l == 0)
        def _():
            acc_ref[...] = part

        @pl.when(l > 0)
        def _():
            acc_ref[...] += part

        @pl.when(l == nk - 1)
        def _():
            finish(acc_ref[...])

    o_spec = pl.BlockSpec((tm, tn), at(lambda i, j, l: (i, j)))
    o_shape = jax.ShapeDtypeStruct((m, n), out_dtype)
    return pl.pallas_call(
        body,
        name=name,
        grid=(n // tn, m // tm, nk) if n_outer else (m // tm, n // tn, nk),
        in_specs=in_specs,
        out_specs=[o_spec, o_spec] if has_norm else o_spec,
        out_shape=[o_shape, jax.ShapeDtypeStruct((m, n), BF16)] if has_norm else o_shape,
        scratch_shapes=[pltpu.VMEM((tm, tn), F32)] if nk > 1 else [],
        compiler_params=_params(("parallel", "parallel", "arbitrary")),
    )(*args)


def _rms(x, g):
    return x * lax.rsqrt(jnp.mean(x * x, axis=-1, keepdims=True) + EPS) * g


def _fold8(v):
    r, d = v.shape
    return jnp.sum(v.reshape(r // 8, 8, d), axis=0)


def _rms_fwd(x, g, name):
    t, d = x.shape
    tm = _pick(t, ROW_TILES)

    def body(x_ref, g_ref, h_ref):
        xv = x_ref[...]
        r = lax.rsqrt(jnp.mean(xv * xv, axis=-1, keepdims=True) + EPS)
        h_ref[...] = (xv * r * g_ref[...]).astype(BF16)

    return pl.pallas_call(
        body,
        name=name,
        grid=(t // tm,),
        in_specs=[pl.BlockSpec((tm, d), lambda i: (i, 0)), pl.BlockSpec((1, d), lambda i: (0, 0))],
        out_specs=pl.BlockSpec((tm, d), lambda i: (i, 0)),
        out_shape=jax.ShapeDtypeStruct((t, d), BF16),
        compiler_params=_params(("parallel",)),
    )(x, g)


def _rms_bwd(x, dpre, w, layer, g, dx_in, name):
    t, d = x.shape
    k = dpre.shape[1]
    tm = _pick(t, ROW_TILES)

    sub = _pick(tm, (256, 128, 64, 32, 16, 8))

    def body(x_ref, dp_ref, w_ref, g_ref, dxi_ref, dx_ref, dxb_ref, dg_ref, cs_ref):
        i = pl.program_id(0)

        @pl.when(i == 0)
        def _():
            dg_ref[...] = jnp.zeros_like(dg_ref)
            cs_ref[...] = jnp.zeros_like(cs_ref)

        dg = jnp.zeros((8, d), F32)
        cs = jnp.zeros((8, d), F32)
        for r0 in range(0, tm, sub):
            rows = slice(r0, r0 + sub)
            xv = x_ref[rows, :]
            r = lax.rsqrt(jnp.mean(xv * xv, axis=-1, keepdims=True) + EPS)
            xhat = xv * r
            dy = lax.dot_general(dp_ref[rows, :], w_ref[...], _DIMS["nt"], preferred_element_type=F32)
            gdy = dy * g_ref[...]
            dx = dxi_ref[rows, :] + r * (gdy - xhat * jnp.mean(gdy * xhat, axis=-1, keepdims=True))
            dx_ref[rows, :] = dx
            dxb_ref[rows, :] = dx.astype(BF16)
            dg += _fold8(dy * xhat)
            cs += _fold8(dx)
        dg_ref[...] += dg
        cs_ref[...] += cs

    row = pl.BlockSpec((tm, d), lambda i: (i, 0))
    acc = pl.BlockSpec((8, d), lambda i: (0, 0))
    w_spec = pl.BlockSpec((None, d, k), lambda i: (layer, 0, 0), pipeline_mode=pl.Buffered(1))
    return pl.pallas_call(
        body,
        name=name,
        grid=(t // tm,),
        in_specs=[row, pl.BlockSpec((tm, k), lambda i: (i, 0)), w_spec, pl.BlockSpec((1, d), lambda i: (0, 0)), row],
        out_specs=[row, row, acc, acc],
        out_shape=[jax.ShapeDtypeStruct((t, d), F32), jax.ShapeDtypeStruct((t, d), BF16),
                   jax.ShapeDtypeStruct((8, d), F32), jax.ShapeDtypeStruct((8, d), F32)],
        compiler_params=_params(("arbitrary",)),
    )(x, dpre, w, g, dx_in)


def _loss_head(x, target, g, name):
    t, d = x.shape
    tm = _pick(t, ROW_TILES)
    inv_d = 1.0 / d

    def body(x_ref, t_ref, g_ref, dx_ref, dxb_ref, sq_ref, dg_ref):
        i = pl.program_id(0)
        xv = x_ref[...]
        gv = g_ref[...]
        r = lax.rsqrt(jnp.mean(xv * xv, axis=-1, keepdims=True) + EPS)
        xhat = xv * r
        err = xhat * gv - t_ref[...]
        dy = err * inv_d
        gdy = dy * gv
        dx = r * (gdy - xhat * jnp.mean(gdy * xhat, axis=-1, keepdims=True))
        dx_ref[...] = dx
        dxb_ref[...] = dx.astype(BF16)

        @pl.when(i == 0)
        def _():
            sq_ref[...] = jnp.zeros_like(sq_ref)
            dg_ref[...] = jnp.zeros_like(dg_ref)

        sq_ref[...] += _fold8(err * err)
        dg_ref[...] += _fold8(dy * xhat)

    row = pl.BlockSpec((tm, d), lambda i: (i, 0))
    acc = pl.BlockSpec((8, d), lambda i: (0, 0))
    return pl.pallas_call(
        body,
        name=name,
        grid=(t // tm,),
        in_specs=[row, row, pl.BlockSpec((1, d), lambda i: (0, 0))],
        out_specs=[row, row, acc, acc],
        out_shape=[jax.ShapeDtypeStruct((t, d), F32), jax.ShapeDtypeStruct((t, d), BF16),
                   jax.ShapeDtypeStruct((8, d), F32), jax.ShapeDtypeStruct((8, d), F32)],
        compiler_params=_params(("arbitrary",)),
    )(x, target, g)


def _rows(tm):
    return lax.broadcasted_iota(jnp.int32, (tm, 1), 0)


def _shift_down(u, before2):
    r8 = _rows(8)
    s1, s2 = pltpu.roll(u, 1, 0), pltpu.roll(u, 2, 0)
    top1 = jnp.where(r8 == 0, before2[1:2], s1[:8])
    top2 = jnp.where(r8 == 0, before2[0:1], jnp.where(r8 == 1, before2[1:2], s2[:8]))
    return jnp.concatenate([top1, s1[8:]], axis=0), jnp.concatenate([top2, s2[8:]], axis=0)


def _shift_up(u, after2):
    tm = u.shape[0]
    r8 = _rows(8)
    s1, s2 = pltpu.roll(u, tm - 1, 0), pltpu.roll(u, tm - 2, 0)
    bot1 = jnp.where(r8 == 7, after2[0:1], s1[tm - 8:])
    bot2 = jnp.where(r8 == 6, after2[0:1], jnp.where(r8 == 7, after2[1:2], s2[tm - 8:]))
    return jnp.concatenate([s1[:tm - 8], bot1], axis=0), jnp.concatenate([s2[:tm - 8], bot2], axis=0)


def _shift_matrix(tm, up):
    r = lax.broadcasted_iota(jnp.int32, (2 * tm, tm), 0)
    c = lax.broadcasted_iota(jnp.int32, (2 * tm, tm), 1)
    t = jnp.where(r >= tm, r - tm, r)
    k = jnp.where(r >= tm, 2, 1)
    return (c == (t + k if up else t - k)).astype(BF16)


def _shift_down_mxu(u, before2):
    tm = u.shape[0]
    moved = jnp.dot(_shift_matrix(tm, False), u.astype(BF16), preferred_element_type=F32)
    r8 = _rows(8)
    s1, s2 = moved[:tm], moved[tm:]
    top1 = s1[:8] + jnp.where(r8 == 0, before2[1:2], 0.0)
    top2 = s2[:8] + jnp.where(r8 == 0, before2[0:1], jnp.where(r8 == 1, before2[1:2], 0.0))
    return jnp.concatenate([top1, s1[8:]], axis=0), jnp.concatenate([top2, s2[8:]], axis=0)


def _shift_up_mxu(u, after2):
    tm = u.shape[0]
    moved = jnp.dot(_shift_matrix(tm, True), u.astype(BF16), preferred_element_type=F32)
    r8 = _rows(8)
    s1, s2 = moved[:tm], moved[tm:]
    bot1 = s1[tm - 8:] + jnp.where(r8 == 7, after2[0:1], 0.0)
    bot2 = s2[tm - 8:] + jnp.where(r8 == 6, after2[0:1], jnp.where(r8 == 7, after2[1:2], 0.0))
    return jnp.concatenate([s1[:tm - 8], bot1], axis=0), jnp.concatenate([s2[:tm - 8], bot2], axis=0)


def _conv_tile(seq):
    return _pick(seq, (256, 128, 64, 32, 16, 8))


def _halo_specs(tm, width, n_tiles):
    per = tm // HALO
    before = pl.BlockSpec((HALO, width), lambda i: (jnp.maximum(i * per - 1, 0), 0))
    after = pl.BlockSpec((HALO, width), lambda i: (jnp.minimum((i + 1) * per, n_tiles * per - 1), 0))
    return before, after


def _convgate_fwd(bcv, w, seq, name):
    t, d3 = bcv.shape
    d = d3 // 3
    tm = _conv_tile(seq)
    tps = seq // tm
    before, _ = _halo_specs(tm, d3, t // tm)

    def body(x_ref, xb_ref, w_ref, y_ref):
        i = pl.program_id(0)
        inner = (i % tps != 0).astype(F32)
        u = x_ref[:, d:2 * d].astype(F32) * x_ref[:, 2 * d:].astype(F32)
        xb = xb_ref[:, d:].astype(F32)[HALO - 2:]
        s1, s2 = _shift_down(u, xb[:, :d] * xb[:, d:] * inner)
        z = w_ref[2:3] * u + w_ref[1:2] * s1 + w_ref[0:1] * s2
        y_ref[...] = (x_ref[:, :d].astype(F32) * z).astype(BF16)

    return pl.pallas_call(
        body,
        name=name,
        grid=(t // tm,),
        in_specs=[pl.BlockSpec((tm, d3), lambda i: (i, 0)), before, pl.BlockSpec((3, d), lambda i: (0, 0))],
        out_specs=pl.BlockSpec((tm, d), lambda i: (i, 0)),
        out_shape=jax.ShapeDtypeStruct((t, d), BF16),
        compiler_params=_params(("parallel",)),
    )(bcv, bcv, w)


def _convgate_bwd(bcv, dy, w, seq, name):
    t, d3 = bcv.shape
    d = d3 // 3
    tm = _conv_tile(seq)
    tps = seq // tm
    before, after = _halo_specs(tm, d3, t // tm)
    _, after_dy = _halo_specs(tm, d, t // tm)

    def body(x_ref, xb_ref, xa_ref, dy_ref, dya_ref, w_ref, dx_ref, dw_ref):
        i = pl.program_id(0)
        inner_lo = (i % tps != 0).astype(F32)
        inner_hi = (i % tps != tps - 1).astype(F32)
        w0, w1, w2 = w_ref[0:1], w_ref[1:2], w_ref[2:3]
        b, c, v = x_ref[:, :d].astype(F32), x_ref[:, d:2 * d].astype(F32), x_ref[:, 2 * d:].astype(F32)
        u = c * v
        xb = xb_ref[:, d:].astype(F32)[HALO - 2:]
        s1, s2 = _shift_down(u, xb[:, :d] * xb[:, d:] * inner_lo)
        z = w2 * u + w1 * s1 + w0 * s2
        dyv = dy_ref[...].astype(F32)
        dz = dyv * b
        dza = dya_ref[...].astype(F32)[0:2] * xa_ref[:, :d].astype(F32)[0:2] * inner_hi
        n1, n2 = _shift_up(dz, dza)
        du = w2 * dz + w1 * n1 + w0 * n2
        dx_ref[:, :d] = (dyv * z).astype(BF16)
        dx_ref[:, d:2 * d] = (du * v).astype(BF16)
        dx_ref[:, 2 * d:] = (du * c).astype(BF16)

        @pl.when(i == 0)
        def _():
            dw_ref[...] = jnp.zeros_like(dw_ref)

        dw_ref[0:1] += jnp.sum(dz * s2, axis=0, keepdims=True)
        dw_ref[1:2] += jnp.sum(dz * s1, axis=0, keepdims=True)
        dw_ref[2:3] += jnp.sum(dz * u, axis=0, keepdims=True)

    return pl.pallas_call(
        body,
        name=name,
        grid=(t // tm,),
        in_specs=[pl.BlockSpec((tm, d3), lambda i: (i, 0)), before, after,
                  pl.BlockSpec((tm, d), lambda i: (i, 0)), after_dy, pl.BlockSpec((3, d), lambda i: (0, 0))],
        out_specs=[pl.BlockSpec((tm, d3), lambda i: (i, 0)), pl.BlockSpec((8, d), lambda i: (0, 0))],
        out_shape=[jax.ShapeDtypeStruct((t, d3), BF16), jax.ShapeDtypeStruct((8, d), F32)],
        compiler_params=_params(("arbitrary",)),
    )(bcv, bcv, bcv, dy, dy, w)


def _sigmoid(x):
    return 1.0 / (1.0 + jnp.exp(-x))


def _ffn_gate_down_fwd(gu, w, w_down, layer, resid, norm_g, seq, name):
    t, f2 = gu.shape
    f = f2 // 2
    d = w_down.shape[2]
    sub = _conv_tile(seq)
    tm = _pick(seq, (2 * sub, sub))
    tps = seq // tm
    before, _ = _halo_specs(tm, f2, t // tm)

    def body(x_ref, xb_ref, w_ref, wd_ref, res_ref, g_ref, o_ref, a_ref, h_ref):
        i = pl.program_id(0)
        inner = (i % tps != 0).astype(F32)
        for r0 in range(0, tm, sub):
            rows = slice(r0, r0 + sub)
            if r0 == 0:
                halo = xb_ref[:, :f].astype(F32)[HALO - 2:] * inner
            else:
                halo = x_ref[r0 - HALO:r0, :f].astype(F32)[HALO - 2:]
            s1, s2 = _shift_down_mxu(x_ref[rows, :f], halo)
            gc = w_ref[2:3] * x_ref[rows, :f].astype(F32) + w_ref[1:2] * s1 + w_ref[0:1] * s2
            act = (gc * _sigmoid(gc) * x_ref[rows, f:].astype(F32)).astype(BF16)
            a_ref[rows, :] = act
            out = jnp.dot(act, wd_ref[...], preferred_element_type=F32) + res_ref[rows, :]
            o_ref[rows, :] = out
            h_ref[rows, :] = _rms(out, g_ref[...]).astype(BF16)

    row_d = pl.BlockSpec((tm, d), lambda i: (i, 0))
    return pl.pallas_call(
        body,
        name=name,
        grid=(t // tm,),
        in_specs=[pl.BlockSpec((tm, f2), lambda i: (i, 0)), before, pl.BlockSpec((3, f), lambda i: (0, 0)),
                  pl.BlockSpec((None, f, d), lambda i: (layer, 0, 0), pipeline_mode=pl.Buffered(1)),
                  row_d, pl.BlockSpec((1, d), lambda i: (0, 0))],
        out_specs=[row_d, pl.BlockSpec((tm, f), lambda i: (i, 0)), row_d],
        out_shape=[jax.ShapeDtypeStruct((t, d), F32), jax.ShapeDtypeStruct((t, f), BF16), jax.ShapeDtypeStruct((t, d), BF16)],
        compiler_params=_params(("parallel",)),
    )(gu, gu, w, w_down, resid, norm_g)


def _ffngate_bwd(gu, da, w, seq, name):
    t, f2 = gu.shape
    f = f2 // 2
    tm = _conv_tile(seq)
    tps = seq // tm
    before, after = _halo_specs(tm, f2, t // tm)
    _, after_da = _halo_specs(tm, f, t // tm)

    def body(x_ref, xb_ref, xa_ref, da_ref, daa_ref, w_ref, dx_ref, dw_ref):
        i = pl.program_id(0)
        inner_lo = (i % tps != 0).astype(F32)
        inner_hi = (i % tps != tps - 1).astype(F32)
        w0, w1, w2 = w_ref[0:1], w_ref[1:2], w_ref[2:3]

        def dgate(gc, uv, dav):
            sg = _sigmoid(gc)
            return dav * uv * (sg * (1.0 + gc * (1.0 - sg))), dav * (gc * sg)

        g, u = x_ref[:, :f].astype(F32), x_ref[:, f:].astype(F32)
        s1, s2 = _shift_down_mxu(x_ref[:, :f], xb_ref[:, :f].astype(F32)[HALO - 2:] * inner_lo)
        gc = w2 * g + w1 * s1 + w0 * s2
        dgc, du = dgate(gc, u, da_ref[...].astype(F32))
        ga = xa_ref[:, :f].astype(F32)
        a1, a2 = _shift_down(ga, x_ref[tm - HALO:, :f].astype(F32)[HALO - 2:])
        gca = w2 * ga + w1 * a1 + w0 * a2
        dgca, _ = dgate(gca, xa_ref[:, f:].astype(F32), daa_ref[...].astype(F32))
        n1, n2 = _shift_up_mxu(dgc, dgca[0:2] * inner_hi)
        dx_ref[:, :f] = (w2 * dgc + w1 * n1 + w0 * n2).astype(BF16)
        dx_ref[:, f:] = du.astype(BF16)

        @pl.when(i == 0)
        def _():
            dw_ref[...] = jnp.zeros_like(dw_ref)

        dw_ref[0:1] += jnp.sum(dgc * s2, axis=0, keepdims=True)
        dw_ref[1:2] += jnp.sum(dgc * s1, axis=0, keepdims=True)
        dw_ref[2:3] += jnp.sum(dgc * g, axis=0, keepdims=True)

    return pl.pallas_call(
        body,
        name=name,
        grid=(t // tm,),
        in_specs=[pl.BlockSpec((tm, f2), lambda i: (i, 0)), before, after,
                  pl.BlockSpec((tm, f), lambda i: (i, 0)), after_da, pl.BlockSpec((3, f), lambda i: (0, 0))],
        out_specs=[pl.BlockSpec((tm, f2), lambda i: (i, 0)), pl.BlockSpec((8, f), lambda i: (0, 0))],
        out_shape=[jax.ShapeDtypeStruct((t, f2), BF16), jax.ShapeDtypeStruct((8, f), F32)],
        compiler_params=_params(("arbitrary",)),
    )(gu, gu, gu, da, da, w)


def _swap_halves(xt):
    half = HEAD_DIM // 2
    return jnp.concatenate([xt[half:], xt[:half]], axis=0)


def _rope(xt, cos, sin):
    return xt * cos + _swap_halves(xt) * sin


def _unrope(dxt, cos, sin):
    return dxt * cos - _swap_halves(dxt) * sin


def _key_query(count):
    kj = lax.broadcasted_iota(jnp.int32, (WINDOW, count * WINDOW), 0)
    qi = lax.broadcasted_iota(jnp.int32, (WINDOW, count * WINDOW), 1) & (WINDOW - 1)
    return kj, qi


def _band_masks(n, count):
    kj, qi = _key_query(count)
    return kj <= qi, jnp.logical_and(kj > qi, n > 0)


def _lanes(v, count):
    return jnp.concatenate([v] * count, axis=1) if count > 1 else v


def _heads(ref, h0, count):
    parts = [ref[(h0 + g) * HEAD_DIM:(h0 + g + 1) * HEAD_DIM, :] for g in range(count)]
    return jnp.concatenate(parts, axis=1) if count > 1 else parts[0]


def _head_rows(ref, h0, count):
    parts = [ref[h0 + g:h0 + g + 1, :] for g in range(count)]
    return jnp.concatenate(parts, axis=1) if count > 1 else parts[0]


def _head_sinks(sink_ref, h0, count):
    parts = [jnp.full((1, WINDOW), sink_ref[h0 + g], F32) for g in range(count)]
    return jnp.concatenate(parts, axis=1) if count > 1 else parts[0]


def _tn(a, b):
    return lax.dot_general(a, b, _DIMS["tn"], preferred_element_type=F32)


def _nt(a, b):
    return lax.dot_general(a, b, _DIMS["nt"], preferred_element_type=F32)


def _nn(a, b):
    return jnp.dot(a, b, preferred_element_type=F32)


def _attn_fwd(qkv, sinks, cos_t, sin_t, bsz, seq, name, hp):
    t, qw = qkv.shape
    d = qw * 2 // 3
    kvw = d // GROUP
    n_heads, n_kv = d // HEAD_DIM, kvw // HEAD_DIM
    nb = seq // WINDOW
    scale = HEAD_DIM ** -0.5

    def body(sink_ref, xc_ref, xp_ref, cc_ref, sc_ref, cp_ref, sp_ref, o_ref, lse_ref, xt_ref, pt_ref, ot_ref):
        n = pl.program_id(1)
        xt_ref[...] = xc_ref[...].T
        pt_ref[...] = xp_ref[:, d:].T
        cos_c, sin_c, cos_p, sin_p = cc_ref[...], sc_ref[...], cp_ref[...], sp_ref[...]
        cos_g, sin_g = _lanes(cos_c, hp), _lanes(sin_c, hp)
        valid_c, valid_p = _band_masks(n, hp)
        for j in range(n_kv):
            ko = j * HEAD_DIM
            kc = _rope(xt_ref[d + ko:d + ko + HEAD_DIM, :], cos_c, sin_c).astype(BF16)
            kp = _rope(pt_ref[ko:ko + HEAD_DIM, :], cos_p, sin_p).astype(BF16)
            vc = xt_ref[d + kvw + ko:d + kvw + ko + HEAD_DIM, :].astype(BF16)
            vp = pt_ref[kvw + ko:kvw + ko + HEAD_DIM, :].astype(BF16)
            for h0 in range(j * GROUP, (j + 1) * GROUP, hp):
                q = _rope(_heads(xt_ref, h0, hp), cos_g, sin_g).astype(BF16)
                sink = _head_sinks(sink_ref, h0, hp)
                s_c = jnp.where(valid_c, _tn(kc, q) * scale, NEG)
                s_p = jnp.where(valid_p, _tn(kp, q) * scale, NEG)
                m = jnp.maximum(jnp.maximum(jnp.max(s_c, axis=0, keepdims=True), jnp.max(s_p, axis=0, keepdims=True)), sink)
                p_c = jnp.exp(s_c - m)
                p_p = jnp.exp(s_p - m)
                den = jnp.sum(p_c, axis=0, keepdims=True) + jnp.sum(p_p, axis=0, keepdims=True) + jnp.exp(sink - m)
                inv = 1.0 / den
                o_g = _nn(vc, (p_c * inv).astype(BF16)) + _nn(vp, (p_p * inv).astype(BF16))
                lse_g = m + jnp.log(den)
                for g in range(hp):
                    h = h0 + g
                    ot_ref[h * HEAD_DIM:(h + 1) * HEAD_DIM, :] = o_g[:, g * WINDOW:(g + 1) * WINDOW]
                    lse_ref[h:h + 1, :] = lse_g[:, g * WINDOW:(g + 1) * WINDOW]
        o_ref[...] = ot_ref[...].T.astype(BF16)

    cur = lambda b, n: (b * nb + n, 0)
    prev = lambda b, n: (b * nb + jnp.maximum(n - 1, 0), 0)
    tab_c = pl.BlockSpec((HEAD_DIM, WINDOW), lambda b, n: (0, n))
    tab_p = pl.BlockSpec((HEAD_DIM, WINDOW), lambda b, n: (0, jnp.maximum(n - 1, 0)))
    return pl.pallas_call(
        body,
        name=name,
        grid=(bsz, nb),
        in_specs=[SMEM, pl.BlockSpec((WINDOW, qw), cur), pl.BlockSpec((WINDOW, qw), prev), tab_c, tab_c, tab_p, tab_p],
        out_specs=[pl.BlockSpec((WINDOW, d), cur), pl.BlockSpec((n_heads, WINDOW), lambda b, n: (0, b * nb + n))],
        out_shape=[jax.ShapeDtypeStruct((t, d), BF16), jax.ShapeDtypeStruct((n_heads, t), F32)],
        scratch_shapes=[pltpu.VMEM((qw, WINDOW), F32), pltpu.VMEM((2 * kvw, WINDOW), F32), pltpu.VMEM((d, WINDOW), F32)],
        compiler_params=_params(("parallel", "arbitrary")),
    )(sinks, qkv, qkv, cos_t, sin_t, cos_t, sin_t)


def _attn_bwd(qkv, o, lse, do, sinks, cos_t, sin_t, bsz, seq, name, hp):
    t, qw = qkv.shape
    d = qw * 2 // 3
    kvw = d // GROUP
    n_heads, n_kv = d // HEAD_DIM, kvw // HEAD_DIM
    nb = seq // WINDOW
    scale = HEAD_DIM ** -0.5

    def body(sink_ref, xc_ref, xp_ref, oc_ref, doc_ref, lc_ref, cc_ref, sc_ref, cp_ref, sp_ref,
             dx_ref, db_ref, dsk_ref, xt_ref, pt_ref, otc_ref, dtc_ref, gt_ref, carry_ref):
        b, n = pl.program_id(0), pl.program_id(1)
        live = n < nb
        xt_ref[...] = xc_ref[...].T
        pt_ref[...] = xp_ref[:, d:].T
        otc_ref[...] = oc_ref[...].astype(F32).T
        dtc_ref[...] = doc_ref[...].T
        cos_c, sin_c, cos_p, sin_p = cc_ref[...], sc_ref[...], cp_ref[...], sp_ref[...]
        cos_g, sin_g = _lanes(cos_c, hp), _lanes(sin_c, hp)
        kj, qi = _key_query(hp)
        valid_c = jnp.logical_and(kj <= qi, live)
        valid_p = jnp.logical_and(kj > qi, jnp.logical_and(n > 0, live))

        @pl.when(jnp.logical_and(b == 0, n == 0))
        def _():
            db_ref[...] = jnp.zeros_like(db_ref)
            dsk_ref[...] = jnp.zeros_like(dsk_ref)

        @pl.when(n == 0)
        def _():
            carry_ref[...] = jnp.zeros_like(carry_ref)

        for j in range(n_kv):
            ko = j * HEAD_DIM
            k_rows = slice(d + ko, d + ko + HEAD_DIM)
            v_rows = slice(d + kvw + ko, d + kvw + ko + HEAD_DIM)
            kc = _rope(xt_ref[k_rows, :], cos_c, sin_c).astype(BF16)
            kp = _rope(pt_ref[ko:ko + HEAD_DIM, :], cos_p, sin_p).astype(BF16)
            vc = xt_ref[v_rows, :].astype(BF16)
            vp = pt_ref[kvw + ko:kvw + ko + HEAD_DIM, :].astype(BF16)
            dk_c = jnp.zeros((HEAD_DIM, WINDOW), F32)
            dv_c = jnp.zeros((HEAD_DIM, WINDOW), F32)
            dk_p = jnp.zeros((HEAD_DIM, WINDOW), F32)
            dv_p = jnp.zeros((HEAD_DIM, WINDOW), F32)
            for h0 in range(j * GROUP, (j + 1) * GROUP, hp):
                q = _rope(_heads(xt_ref, h0, hp), cos_g, sin_g).astype(BF16)
                do_g = _heads(dtc_ref, h0, hp)
                do_b = do_g.astype(BF16)
                lse_g = _head_rows(lc_ref, h0, hp)
                delta = jnp.sum(_heads(otc_ref, h0, hp) * do_g, axis=0, keepdims=True)
                p_c = jnp.exp(jnp.where(valid_c, _tn(kc, q) * scale, NEG) - lse_g)
                p_p = jnp.exp(jnp.where(valid_p, _tn(kp, q) * scale, NEG) - lse_g)
                ds_c = (p_c * (_tn(vc, do_b) - delta)).astype(BF16)
                ds_p = (p_p * (_tn(vp, do_b) - delta)).astype(BF16)
                dq = _unrope((_nn(kc, ds_c) + _nn(kp, ds_p)) * scale, cos_g, sin_g)
                dsk = jnp.where(live, -jnp.exp(_head_sinks(sink_ref, h0, hp) - lse_g) * delta, 0.0)
                for g in range(hp):
                    rows = slice((h0 + g) * HEAD_DIM, (h0 + g + 1) * HEAD_DIM)
                    gt_ref[rows, :] = carry_ref[rows, :]
                    carry_ref[rows, :] = dq[:, g * WINDOW:(g + 1) * WINDOW]
                    dsk_ref[h0 + g:h0 + g + 1, :] += dsk[:, g * WINDOW:(g + 1) * WINDOW]
                dv_c += _nt(do_b, p_c.astype(BF16))
                dk_c += _nt(q, ds_c)
                dv_p += _nt(do_b, p_p.astype(BF16))
                dk_p += _nt(q, ds_p)
            gt_ref[k_rows, :] = _unrope((carry_ref[k_rows, :] + dk_p) * scale, cos_p, sin_p)
            gt_ref[v_rows, :] = carry_ref[v_rows, :] + dv_p
            carry_ref[k_rows, :] = dk_c
            carry_ref[v_rows, :] = dv_c
        dx = gt_ref[...].T
        dx_ref[...] = dx.astype(BF16)
        db_ref[...] += _fold8(dx)

    cur = lambda b, n: (b * nb + jnp.minimum(n, nb - 1), 0)
    prev = lambda b, n: (b * nb + jnp.maximum(jnp.minimum(n, nb - 1) - 1, 0), 0)
    done = lambda b, n: (b * nb + jnp.maximum(n - 1, 0), 0)
    stat_c = pl.BlockSpec((n_heads, WINDOW), lambda b, n: (0, b * nb + jnp.minimum(n, nb - 1)))
    tab_c = pl.BlockSpec((HEAD_DIM, WINDOW), lambda b, n: (0, jnp.minimum(n, nb - 1)))
    tab_p = pl.BlockSpec((HEAD_DIM, WINDOW), lambda b, n: (0, jnp.maximum(n - 1, 0)))
    return pl.pallas_call(
        body,
        name=name,
        grid=(bsz, nb + 1),
        in_specs=[SMEM, pl.BlockSpec((WINDOW, qw), cur), pl.BlockSpec((WINDOW, qw), prev),
                  pl.BlockSpec((WINDOW, d), cur), pl.BlockSpec((WINDOW, d), cur), stat_c, tab_c, tab_c, tab_p, tab_p],
        out_specs=[pl.BlockSpec((WINDOW, qw), done), pl.BlockSpec((8, qw), lambda b, n: (0, 0)),
                   pl.BlockSpec((n_heads, WINDOW), lambda b, n: (0, 0))],
        out_shape=[jax.ShapeDtypeStruct((t, qw), BF16), jax.ShapeDtypeStruct((8, qw), F32),
                   jax.ShapeDtypeStruct((n_heads, WINDOW), F32)],
        scratch_shapes=[pltpu.VMEM((qw, WINDOW), F32), pltpu.VMEM((2 * kvw, WINDOW), F32), pltpu.VMEM((d, WINDOW), F32),
                        pltpu.VMEM((d, WINDOW), F32), pltpu.VMEM((qw, WINDOW), F32), pltpu.VMEM((qw, WINDOW), F32)],
        compiler_params=_params(("arbitrary", "arbitrary")),
    )(sinks, qkv, qkv, o, do, lse, cos_t, sin_t, cos_t, sin_t)


def _place():
    return lax.axis_index("x"), lax.axis_index("y"), lax.axis_index("c")


def _other_chips(x, y):
    return [(1 - x, y), (x, 1 - y), (1 - x, 1 - y)]


def _place_shard(w, axis, q, name):
    ly, k, n = w.shape
    tr = _pick(k, (256, 128, 64, 32, 16, 8))
    steps = k // tr
    shape = (ly, k * N_CHIPS, n) if axis == 1 else (ly, k, n * N_CHIPS)
    if axis == 1:
        out_spec = pl.BlockSpec((None, tr, n), lambda l, i, q_ref: (l, q_ref[0] * steps + i, 0))
    else:
        out_spec = pl.BlockSpec((None, tr, n), lambda l, i, q_ref: (l, i, q_ref[0]))

    def body(q_ref, w_ref, o_ref):
        del q_ref
        o_ref[...] = w_ref[...].astype(BF16)

    return pl.pallas_call(
        body,
        name=name,
        grid_spec=pltpu.PrefetchScalarGridSpec(
            num_scalar_prefetch=1, grid=(ly, steps),
            in_specs=[pl.BlockSpec((None, tr, n), lambda l, i, q_ref: (l, i, 0))], out_specs=out_spec),
        out_shape=jax.ShapeDtypeStruct(shape, BF16),
        compiler_params=_params(("parallel", "parallel")),
    )(q, w)


def _half_block(ref, axis, layer, px, py, pc):
    blk = 2 * px + py
    if axis == 1:
        rows = ref.shape[1] // (2 * N_CHIPS)
        return ref.at[layer, pl.ds(pl.multiple_of((2 * blk + pc) * rows, 8), rows), :]
    rows, width = ref.shape[1] // 2, ref.shape[2] // N_CHIPS
    return ref.at[layer, pl.ds(pl.multiple_of(pc * rows, 8), rows), pl.ds(pl.multiple_of(blk * width, 128), width)]


def _gather_copy(refs, axes, pieces, send_sems, recv_sems, p, k, stage, whose):
    x, y, c = _place()
    chip = _other_chips(x, y)[k]
    i, layer = pieces[p]
    if stage == 0:
        origin = (x, y, c) if whose == "mine" else (*chip, c)
        to = (*chip, c)
    else:
        origin = (*chip, c) if whose == "mine" else (*chip, 1 - c)
        to = (x, y, 1 - c)
    blk = _half_block(refs[i], axes[i], layer, *origin)
    return pltpu.make_async_remote_copy(src_ref=blk, dst_ref=blk, send_sem=send_sems.at[p * 3 + k],
                                        recv_sem=recv_sems.at[p * 3 + k], device_id=to, device_id_type=MESH)


def _gather_weights(fulls, axes, pieces, name):
    n, m = len(fulls), 3 * len(pieces)

    def body(*refs):
        dst = refs[n:2 * n]
        sems = refs[2 * n:]
        todo = [(p, k) for p in range(len(pieces)) for k in range(3)]
        sends = [_gather_copy(dst, axes, pieces, sems[0], sems[1], p, k, 0, "mine") for p, k in todo]
        for cp in sends:
            cp.start()
        for p, k in todo:
            _gather_copy(dst, axes, pieces, sems[0], sems[1], p, k, 0, "theirs").wait_recv()
            sends.append(_gather_copy(dst, axes, pieces, sems[2], sems[3], p, k, 1, "mine"))
            sends[-1].start()
        for p, k in todo:
            _gather_copy(dst, axes, pieces, sems[2], sems[3], p, k, 1, "theirs").wait_recv()
        for cp in sends:
            cp.wait_send()

    return pl.pallas_call(
        body,
        name=name,
        in_specs=[ANY] * n,
        out_specs=[ANY] * n,
        out_shape=[jax.ShapeDtypeStruct(f.shape, f.dtype) for f in fulls],
        input_output_aliases={i: i for i in range(n)},
        scratch_shapes=[pltpu.SemaphoreType.DMA((m,))] * 4,
    )(*fulls)


HBM_SPEC = pl.BlockSpec(memory_space=pltpu.HBM)
SEM_SPEC = pl.BlockSpec(memory_space=pltpu.SEMAPHORE)


def _gather_start(fulls, axes, pieces, stage, name):
    n, m = len(fulls), 3 * len(pieces)

    def body(*refs):
        src = refs[:n]
        send_sems, recv_sems = refs[2 * n], refs[2 * n + 1]
        for p in range(len(pieces)):
            for k in range(3):
                _gather_copy(src, axes, pieces, send_sems, recv_sems, p, k, stage, "mine").start()

    out = pl.pallas_call(
        body,
        name=name,
        in_specs=[HBM_SPEC] * n,
        out_specs=[HBM_SPEC] * n + [SEM_SPEC, SEM_SPEC],
        out_shape=[pltpu.HBM(f.shape, f.dtype) for f in fulls] + [pltpu.SemaphoreType.DMA((m,)), pltpu.SemaphoreType.DMA((m,))],
        input_output_aliases={i: i for i in range(n)},
        compiler_params=pltpu.CompilerParams(has_side_effects=pltpu.SideEffectType.DATAFLOW_SIDE_EFFECTING),
    )(*[pltpu.with_memory_space_constraint(f, pltpu.HBM) for f in fulls])
    return list(out[:n]), out[n], out[n + 1]


def _gather_wait(fulls, send_sems, recv_sems, after, axes, pieces, stage, name):
    n = len(fulls)

    def body(*refs):
        src = refs[:n]
        s_sems, r_sems = refs[n], refs[n + 1]
        for p in range(len(pieces)):
            for k in range(3):
                _gather_copy(src, axes, pieces, s_sems, r_sems, p, k, stage, "mine").wait_send()
                _gather_copy(src, axes, pieces, s_sems, r_sems, p, k, stage, "theirs").wait_recv()

    out = pl.pallas_call(
        body,
        name=name,
        in_specs=[HBM_SPEC] * n + [SEM_SPEC, SEM_SPEC, ANY],
        out_specs=[HBM_SPEC] * n,
        out_shape=[pltpu.HBM(f.shape, f.dtype) for f in fulls],
        input_output_aliases={i: i for i in range(n)},
        compiler_params=pltpu.CompilerParams(has_side_effects=pltpu.SideEffectType.DATAFLOW_SIDE_EFFECTING),
    )(*fulls, send_sems, recv_sems, after)
    return list(out)


def _half_shape(kind, shape):
    if kind == "col":
        return (shape[0] // 2, shape[1])
    return (N_CHIPS, shape[1] // 2, shape[2])


def _half_of(kind, ref, h):
    if kind == "col":
        r = ref.shape[0] // 2
        return ref.at[pl.ds(pl.multiple_of(h * r, 8), r), :]
    r = ref.shape[1] // 2
    return ref.at[:, pl.ds(pl.multiple_of(h * r, 8), r), :]


def _pair_copy(src, dst, kinds, send_sems, recv_sems, i):
    x, y, c = _place()
    return pltpu.make_async_remote_copy(src_ref=_half_of(kinds[i], src[i], 1 - c), dst_ref=dst[i], send_sem=send_sems.at[i],
                                        recv_sem=recv_sems.at[i], device_id=(x, y, 1 - c), device_id_type=MESH)


def _pair_start(grads, kinds, token, name):
    n = len(grads)
    lands = [pltpu.HBM(_half_shape(kd, g.shape), g.dtype) for g, kd in zip(grads, kinds)]

    def body(*refs):
        src, dst = refs[:n], refs[2 * n + 2:3 * n + 2]
        send_sems, recv_sems = refs[3 * n + 2], refs[3 * n + 3]
        for i in range(n):
            _pair_copy(src, dst, kinds, send_sems, recv_sems, i).start()

    arrays = list(grads) + [token]
    out = pl.pallas_call(
        body,
        name=name,
        in_specs=[HBM_SPEC] * (n + 1),
        out_specs=[HBM_SPEC] * (2 * n + 1) + [SEM_SPEC, SEM_SPEC],
        out_shape=[pltpu.HBM(a.shape, a.dtype) for a in arrays] + lands + [pltpu.SemaphoreType.DMA((n,)), pltpu.SemaphoreType.DMA((n,))],
        input_output_aliases={i: i for i in range(n + 1)},
        compiler_params=pltpu.CompilerParams(has_side_effects=pltpu.SideEffectType.DATAFLOW_SIDE_EFFECTING),
    )(*[pltpu.with_memory_space_constraint(a, pltpu.HBM) for a in arrays])
    return list(out[:n]), out[n], list(out[n + 1:2 * n + 1]), out[2 * n + 1], out[2 * n + 2]


def _pair_wait(grads, kinds, token, lands, send_sems, recv_sems, name):
    n = len(grads)

    def body(*refs):
        src, dst = refs[:n], refs[n + 1:2 * n + 1]
        s_sems, r_sems = refs[2 * n + 1], refs[2 * n + 2]
        for i in range(n):
            cp = _pair_copy(src, dst, kinds, s_sems, r_sems, i)
            cp.wait_send()
            cp.wait_recv()

    arrays = list(grads) + [token] + list(lands)
    out = pl.pallas_call(
        body,
        name=name,
        in_specs=[HBM_SPEC] * (2 * n + 1) + [SEM_SPEC, SEM_SPEC],
        out_specs=[HBM_SPEC] * (2 * n + 1),
        out_shape=[pltpu.HBM(a.shape, a.dtype) for a in arrays],
        input_output_aliases={i: i for i in range(2 * n + 1)},
        compiler_params=pltpu.CompilerParams(has_side_effects=pltpu.SideEffectType.DATAFLOW_SIDE_EFFECTING),
    )(*arrays, send_sems, recv_sems)
    return list(out[:n]), out[n], list(out[n + 1:])


def _pair_sum(grad, recv, kind, c, own, layer, name):
    r, cols = own.shape[2:]
    if kind == "col":
        tr = _pick(r, (256, 128, 64, 32, 16, 8))
        steps = r // tr
        grid = (N_CHIPS, steps)
        g_spec = pl.BlockSpec((tr, cols), lambda s, i, c_ref: (c_ref[0] * steps + i, s))
        r_spec = pl.BlockSpec((tr, cols), lambda s, i, c_ref: (i, s))
        o_spec = pl.BlockSpec((None, None, tr, cols), lambda s, i, c_ref: (layer, s, i, 0))
        g_in = grad
    else:
        grid = (N_CHIPS, 1)
        g_spec = pl.BlockSpec((None, None, r, cols), lambda s, i, c_ref: (s, c_ref[0], 0, 0))
        r_spec = pl.BlockSpec((None, r, cols), lambda s, i, c_ref: (s, 0, 0))
        o_spec = pl.BlockSpec((None, None, r, cols), lambda s, i, c_ref: (layer, s, 0, 0))
        g_in = grad.reshape(N_CHIPS, 2, r, cols)

    def body(c_ref, g_ref, r_ref, own_ref, o_ref):
        del c_ref, own_ref
        o_ref[...] = (g_ref[...].astype(F32) + r_ref[...].astype(F32)).astype(o_ref.dtype)

    return pl.pallas_call(
        body,
        name=name,
        grid_spec=pltpu.PrefetchScalarGridSpec(num_scalar_prefetch=1, grid=grid, in_specs=[g_spec, r_spec, ANY], out_specs=o_spec),
        out_shape=jax.ShapeDtypeStruct(own.shape, own.dtype),
        input_output_aliases={3: 0},
        compiler_params=_params(("parallel", "parallel")),
    )(c, g_in, recv, own)


def _scatter_copy(own, mine, sib, pieces, send_sems, recv_sems, p, k, stage, whose):
    x, y, c = _place()
    q = 2 * x + y
    i, layer = pieces[p]
    per = 4 if stage == 0 else 3
    if k == 3:
        src, dst, to = own[i].at[layer, q], sib[i].at[layer, q], (x, y, 1 - c)
    else:
        chip = _other_chips(x, y)[k]
        slot = 2 * chip[0] + chip[1]
        if stage == 0:
            to = (*chip, c)
            src, dst = (own[i].at[layer, slot], mine[i].at[layer, q]) if whose == "mine" else (own[i].at[layer, q], mine[i].at[layer, slot])
        else:
            to = (x, y, 1 - c)
            src, dst = mine[i].at[layer, slot], sib[i].at[layer, slot]
    return pltpu.make_async_remote_copy(src_ref=src, dst_ref=dst, send_sem=send_sems.at[p * per + k],
                                        recv_sem=recv_sems.at[p * per + k], device_id=to, device_id_type=MESH)


def _scatter_start(own, mine, sib, token, pieces, stage, name):
    n = len(own)
    per = 4 if stage == 0 else 3
    m = per * len(pieces)
    n_arr = 3 * n + 1

    def body(*refs):
        o, mi, si = refs[:n], refs[n:2 * n], refs[2 * n:3 * n]
        send_sems, recv_sems = refs[2 * n_arr], refs[2 * n_arr + 1]
        for p in range(len(pieces)):
            for k in range(per):
                _scatter_copy(o, mi, si, pieces, send_sems, recv_sems, p, k, stage, "mine").start()

    arrays = list(own) + list(mine) + list(sib) + [token]
    out = pl.pallas_call(
        body,
        name=name,
        in_specs=[HBM_SPEC] * n_arr,
        out_specs=[HBM_SPEC] * n_arr + [SEM_SPEC, SEM_SPEC],
        out_shape=[pltpu.HBM(a.shape, a.dtype) for a in arrays] + [pltpu.SemaphoreType.DMA((m,)), pltpu.SemaphoreType.DMA((m,))],
        input_output_aliases={i: i for i in range(n_arr)},
        compiler_params=pltpu.CompilerParams(has_side_effects=pltpu.SideEffectType.DATAFLOW_SIDE_EFFECTING),
    )(*[pltpu.with_memory_space_constraint(a, pltpu.HBM) for a in arrays])
    return list(out[:n]), list(out[n:2 * n]), list(out[2 * n:3 * n]), out[3 * n], out[n_arr], out[n_arr + 1]


def _scatter_wait(own, mine, sib, token, send_sems, recv_sems, pieces, stage, name):
    n = len(own)
    per = 4 if stage == 0 else 3
    n_arr = 3 * n + 1

    def body(*refs):
        o, mi, si = refs[:n], refs[n:2 * n], refs[2 * n:3 * n]
        s_sems, r_sems = refs[n_arr], refs[n_arr + 1]
        for p in range(len(pieces)):
            for k in range(per):
                _scatter_copy(o, mi, si, pieces, s_sems, r_sems, p, k, stage, "mine").wait_send()
                _scatter_copy(o, mi, si, pieces, s_sems, r_sems, p, k, stage, "theirs").wait_recv()

    arrays = list(own) + list(mine) + list(sib) + [token]
    out = pl.pallas_call(
        body,
        name=name,
        in_specs=[HBM_SPEC] * n_arr + [SEM_SPEC, SEM_SPEC],
        out_specs=[HBM_SPEC] * n_arr,
        out_shape=[pltpu.HBM(a.shape, a.dtype) for a in arrays],
        input_output_aliases={i: i for i in range(n_arr)},
        compiler_params=pltpu.CompilerParams(has_side_effects=pltpu.SideEffectType.DATAFLOW_SIDE_EFFECTING),
    )(*arrays, send_sems, recv_sems)
    return list(out[:n]), list(out[n:2 * n]), list(out[2 * n:3 * n]), out[3 * n]


def _reduce_adamw(own, mine, sib, w, m, v, qc, name):
    ly, _, r, cols = mine.shape
    tr = _pick(r, (128, 64, 32, 16, 8))
    steps = r // tr
    c1 = 1.0 - ADAM_B1 ** ADAM_STEP
    c2 = 1.0 - ADAM_B2 ** ADAM_STEP

    def body(qc_ref, own_ref, mine_ref, sib_ref, w_ref, m_ref, v_ref, g_ref, d_ref, nm_ref, nv_ref):
        q = qc_ref[0]
        mine_sum = sib_sum = None
        for s in range(N_CHIPS):
            a = jnp.where(q == s, own_ref[...], mine_ref[s]).astype(F32)
            b = sib_ref[s].astype(F32)
            mine_sum = a if s == 0 else mine_sum + a
            sib_sum = b if s == 0 else sib_sum + b
        gv = jnp.where(pl.program_id(1) == qc_ref[1], mine_sum, sib_sum)
        nm = ADAM_B1 * m_ref[...] + (1.0 - ADAM_B1) * gv
        nv = ADAM_B2 * v_ref[...] + (1.0 - ADAM_B2) * (gv * gv)
        g_ref[...] = gv
        d_ref[...] = -ADAM_LR * ((nm / c1) / (jnp.sqrt(nv / c2) + ADAM_EPS) + ADAM_WD * w_ref[...])
        nm_ref[...] = nm
        nv_ref[...] = nv

    def mine_rows(h, i, qc_ref):
        return jnp.where(h == qc_ref[1], i, 0)

    def sib_rows(h, i, qc_ref):
        return jnp.where(h == qc_ref[1], 0, i)

    own_spec = pl.BlockSpec((None, None, tr, cols), lambda l, h, i, qc_ref: (l, qc_ref[0], mine_rows(h, i, qc_ref), 0))
    mine_spec = pl.BlockSpec((None, N_CHIPS, tr, cols), lambda l, h, i, qc_ref: (l, 0, mine_rows(h, i, qc_ref), 0))
    sib_spec = pl.BlockSpec((None, N_CHIPS, tr, cols), lambda l, h, i, qc_ref: (l, 0, sib_rows(h, i, qc_ref), 0))
    spec = pl.BlockSpec((None, tr, cols), lambda l, h, i, qc_ref: (l, h * steps + i, 0))
    shp = jax.ShapeDtypeStruct(w.shape, F32)
    return pl.pallas_call(
        body,
        name=name,
        grid_spec=pltpu.PrefetchScalarGridSpec(
            num_scalar_prefetch=1, grid=(ly, N_CORES, steps),
            in_specs=[own_spec, mine_spec, sib_spec, spec, spec, spec], out_specs=[spec] * 4),
        out_shape=[shp] * 4,
        compiler_params=_params(("parallel", "parallel", "parallel")),
    )(qc, own, mine, sib, w, m, v)


def _allreduce_small(v, name):
    r, w = v.shape

    def body(v_ref, o_ref, buf_ref, send_sems, recv_sems):
        x, y, c = _place()
        me = 4 * x + 2 * y + c

        def peer(k):
            return x ^ (k >> 2), y ^ ((k >> 1) & 1), c ^ (k & 1)

        def remote(k, slot):
            return pltpu.make_async_remote_copy(
                src_ref=v_ref, dst_ref=buf_ref.at[slot], send_sem=send_sems.at[k - 1], recv_sem=recv_sems.at[k - 1],
                device_id=peer(k), device_id_type=MESH)

        sends = [remote(k, me) for k in range(1, N_DEV)]
        for cp in sends:
            cp.start()
        buf_ref[me] = v_ref[...]
        for k in range(1, N_DEV):
            px, py, pc = peer(k)
            remote(k, 4 * px + 2 * py + pc).wait_recv()
        for cp in sends:
            cp.wait_send()
        acc = buf_ref[0]
        for dev in range(1, N_DEV):
            acc = acc + buf_ref[dev]
        o_ref[...] = acc

    vm = pl.BlockSpec(memory_space=pltpu.VMEM)
    return pl.pallas_call(
        body,
        name=name,
        in_specs=[vm],
        out_specs=vm,
        out_shape=jax.ShapeDtypeStruct((r, w), F32),
        scratch_shapes=[pltpu.VMEM((N_DEV, r, w), F32), pltpu.SemaphoreType.DMA((N_DEV - 1,)), pltpu.SemaphoreType.DMA((N_DEV - 1,))],
        compiler_params=pltpu.CompilerParams(vmem_limit_bytes=VMEM_LIMIT_BYTES),
    )(v)


def _adamw(w, g, m, v, name):
    ly, r, c = w.shape
    tr = _pick(r, (256, 128, 64, 32, 16, 8))
    c1 = 1.0 - ADAM_B1 ** ADAM_STEP
    c2 = 1.0 - ADAM_B2 ** ADAM_STEP

    def body(w_ref, g_ref, m_ref, v_ref, d_ref, nm_ref, nv_ref):
        gv = g_ref[...]
        nm = ADAM_B1 * m_ref[...] + (1.0 - ADAM_B1) * gv
        nv = ADAM_B2 * v_ref[...] + (1.0 - ADAM_B2) * (gv * gv)
        d_ref[...] = -ADAM_LR * ((nm / c1) / (jnp.sqrt(nv / c2) + ADAM_EPS) + ADAM_WD * w_ref[...])
        nm_ref[...] = nm
        nv_ref[...] = nv

    spec = pl.BlockSpec((None, tr, c), lambda l, i: (l, i, 0))
    shp = jax.ShapeDtypeStruct((ly, r, c), F32)
    return pl.pallas_call(
        body,
        name=name,
        grid=(ly, r // tr),
        in_specs=[spec] * 4,
        out_specs=[spec] * 3,
        out_shape=[shp] * 3,
        compiler_params=_params(("parallel", "parallel")),
    )(w, g, m, v)


def _rope_tables(seq):
    pos = jnp.arange(seq, dtype=F32)
    inv_freq = 1.0 / (ROPE_THETA ** (jnp.arange(0, HEAD_DIM, 2, dtype=F32) / HEAD_DIM))
    ang = (pos[:, None] * inv_freq[None, :]).T
    cos, sin = jnp.cos(ang), jnp.sin(ang)
    return jnp.concatenate([cos, cos], axis=0), jnp.concatenate([-sin, sin], axis=0)


def _pack(vs, fill=0.0):
    p = jnp.concatenate([v.reshape(-1) for v in vs])
    size = -(-p.shape[0] // 8192) * 8192
    return jnp.pad(p, (0, size - p.shape[0]), constant_values=fill).reshape(-1, 1024)


def _unpack(p, like):
    p = p.reshape(-1)
    out, o = [], 0
    for v in like:
        n = int(math.prod(v.shape))
        out.append(p[o:o + n].reshape(v.shape))
        o += n
    return out


def kernel(x, norm_mix, norm_ffn, norm_final, conv_w_in, conv_w_conv, conv_w_out, attn_w_qkv, attn_b_qkv, attn_sinks, attn_w_o, attn_b_o, ffn_w_in, ffn_w_conv, ffn_w_down, loss_target, m_norm_mix, m_norm_ffn, m_norm_final, m_conv_w_in, m_conv_w_conv, m_conv_w_out, m_attn_w_qkv, m_attn_b_qkv, m_attn_sinks, m_attn_w_o, m_attn_b_o, m_ffn_w_in, m_ffn_w_conv, m_ffn_w_down, v_norm_mix, v_norm_ffn, v_norm_final, v_conv_w_in, v_conv_w_conv, v_conv_w_out, v_attn_w_qkv, v_attn_b_qkv, v_attn_sinks, v_attn_w_o, v_attn_b_o, v_ffn_w_in, v_ffn_w_conv, v_ffn_w_down):
    bsz, seq, d = x.shape
    t = bsz * seq
    depth = norm_mix.shape[0]
    n_conv, n_attn = conv_w_in.shape[0], attn_w_qkv.shape[0]
    xq, yq, cq = _place()
    q = 2 * xq + yq

    big = [conv_w_in, conv_w_out, attn_w_qkv, attn_w_o, ffn_w_in, ffn_w_down]
    axes = [2, 1, 2, 1, 2, 1]
    q_arr = q.astype(jnp.int32).reshape(1)
    c_arr = cq.astype(jnp.int32).reshape(1)
    weights = [_place_shard(w, ax, q_arr, f"place_shard{n}") for n, (w, ax) in enumerate(zip(big, axes))]

    def pieces_of(i):
        return [(0, i // 2), (1, i // 2), (4, i), (5, i)] if i % 2 == 0 else [(2, i // 2), (3, i // 2), (4, i), (5, i)]

    weights = _gather_weights(weights, axes, pieces_of(0)[:2], "gather_mixer0")

    small_cols = [conv_w_conv, attn_b_qkv, attn_b_o, ffn_w_conv]

    def placed(v):
        width = v.shape[-1]
        full = jnp.zeros(v.shape[:-1] + (N_CHIPS * width,), F32)
        return lax.dynamic_update_slice_in_dim(full, v * (1.0 / N_CORES), q * width, axis=v.ndim - 1)

    full_cols = [placed(v) for v in small_cols]
    wc_conv, b_qkv, b_o, wf_conv = _unpack(_allreduce_small(_pack(full_cols), "gather_small"), full_cols)
    cos_t, sin_t = _rope_tables(seq)

    xs = x.reshape(t, d)
    saved = []
    sems = {}

    def fetch(pieces, stage, tag):
        nonlocal weights
        weights, sems[tag, 0], sems[tag, 1] = _gather_start(weights, axes, pieces, stage, f"gather_{'ici' if stage == 0 else 'pass'}_start{tag}")

    def settle(pieces, stage, tag, after):
        nonlocal weights
        weights = _gather_wait(weights, sems[tag, 0], sems[tag, 1], after, axes, pieces, stage,
                               f"gather_{'ici' if stage == 0 else 'pass'}_wait{tag}")

    for i in range(depth):
        j = i // 2
        ahead = pieces_of(i + 1) if i + 1 < depth else None
        if i == 0:
            fetch(pieces_of(0)[2:], 0, "0")
        elif ahead:
            fetch(ahead, 0, str(i + 1))
        w_cin, w_cout, w_qkv, w_o, w_fin, w_fdown = weights
        if i == 0:
            h = _rms_fwd(xs, norm_mix[0:1], "norm_mix_fwd0")
        g_ffn = norm_ffn[i:i + 1]
        if i % 2 == 0:
            pre = _mm(h, w_cin, "nn", BF16, layer=j, tm=1024, tn=768, tk=4096, name=f"conv_in_fwd{i}")
            mixed = _convgate_fwd(pre, wc_conv[j], seq, f"conv_gate_fwd{i}")
            if i == 0:
                settle(pieces_of(0)[2:], 0, "0", mixed)
                fetch(pieces_of(0)[2:], 1, "0")
                w_cin, w_cout, w_qkv, w_o, w_fin, w_fdown = weights
            x_mid, h2 = _mm(mixed, w_cout, "nn", F32, layer=j, residual=xs, norm_g=g_ffn, tm=512, tn=1024, tk=4096,
                            name=f"conv_out_fwd{i}")
            lse = None
        else:
            pre = _mm(h, w_qkv, "nn", F32, layer=j, bias=b_qkv[j:j + 1], tm=1024, tn=768, tk=4096, name=f"qkv_fwd{i}")
            mixed, lse = _attn_fwd(pre, attn_sinks[j], cos_t, sin_t, bsz, seq, f"attn_fwd{i}", hp=1)
            x_mid, h2 = _mm(mixed, w_o, "nn", F32, layer=j, bias=b_o[j:j + 1], residual=xs, norm_g=g_ffn, tm=512, tn=1024,
                            tk=4096, name=f"attn_out_fwd{i}")
        if i == 0:
            settle(pieces_of(0)[2:], 1, "0", h2)
            fetch(ahead, 0, "1")
        elif ahead:
            settle(ahead, 0, str(i + 1), x_mid)
            fetch(ahead, 1, str(i + 1))
        w_cin, w_cout, w_qkv, w_o, w_fin, w_fdown = weights
        gu = _mm(h2, w_fin, "nn", BF16, layer=i, n_outer=True, tm=1024, tn=2816, tk=4096, name=f"ffn_in_fwd{i}")
        if i == 0:
            settle(ahead, 0, "1", gu)
            fetch(ahead, 1, "1")
            w_cin, w_cout, w_qkv, w_o, w_fin, w_fdown = weights
        g_next = norm_mix[i + 1:i + 2] if i + 1 < depth else norm_final.reshape(1, d)
        x_next, act, h_next = _ffn_gate_down_fwd(gu, wf_conv[i], w_fdown, i, x_mid, g_next, seq, f"ffn_gate_down_fwd{i}")
        if ahead:
            settle(ahead, 1, str(i + 1), x_next)
        saved.append((xs, h, pre, mixed, lse, x_mid, h2, gu, act))
        xs, h = x_next, h_next
    w_cin, w_cout, w_qkv, w_o, w_fin, w_fdown = weights

    dx, dxb, sq, dg_final = _loss_head(xs, loss_target.reshape(t, d), norm_final.reshape(1, d), "loss_head")
    loss = lax.psum(0.5 * jnp.sum(sq) / d, ("x", "y", "c"))

    g_norm_mix, g_norm_ffn = [None] * depth, [None] * depth
    g_cin, g_cconv, g_cout = [None] * n_conv, [None] * n_conv, [None] * n_conv
    g_qkv, g_bqkv, g_sinks, g_o, g_bo = ([None] * n_attn for _ in range(5))
    g_fin, g_fconv, g_fdown = [None] * depth, [None] * depth, [None] * depth

    kinds6 = ["col", "row", "col", "row", "col", "row"]
    layers6 = [n_conv, n_conv, n_attn, n_attn, depth, depth]
    big_w = [conv_w_in, conv_w_out, attn_w_qkv, attn_w_o, ffn_w_in, ffn_w_down]

    def slot_stack(n):
        k, cols = big_w[n].shape[1], big_w[n].shape[2]
        r = k // 2
        return lax.empty((layers6[n], N_CHIPS, r, cols), BF16)

    own = [slot_stack(n) for n in range(6)]
    mine = [slot_stack(n) for n in range(6)]
    sib = [slot_stack(n) for n in range(6)]
    flight = {}

    def group(i, part):
        return f"{part}{i}", (pieces_of(i)[:2] if part == "m" else pieces_of(i)[2:])

    def scatter(grp, stage, action, token):
        tag, pieces = grp
        ts = [ti for ti, _ in pieces]
        local = [(n, l) for n, (_, l) in enumerate(pieces)]
        sub = ([own[ti] for ti in ts], [mine[ti] for ti in ts], [sib[ti] for ti in ts], token)
        label = f"grad_{'ici' if stage == 0 else 'pass'}_{action}_{tag}"
        if action == "start":
            o, mi, si, token, s_sems, r_sems = _scatter_start(*sub, local, stage, label)
            flight[tag] = (s_sems, r_sems)
        else:
            o, mi, si, token = _scatter_wait(*sub, *flight[tag], local, stage, label)
        for n, ti in enumerate(ts):
            own[ti], mine[ti], sib[ti] = o[n], mi[n], si[n]
        return token

    def pair_begin(grp, token):
        tag, pieces = grp
        grads = {0: g_cin, 1: g_cout, 2: g_qkv, 3: g_o, 4: g_fin, 5: g_fdown}
        parts, kinds = [], []
        for ti, l in pieces:
            g = grads[ti][l]
            parts.append(g if kinds6[ti] == "col" else g.reshape(N_CHIPS, g.shape[0] // N_CHIPS, g.shape[1]))
            kinds.append(kinds6[ti])
        parts, token, lands, s_sems, r_sems = _pair_start(parts, kinds, token, f"grad_pair_start_{tag}")
        flight["pair" + tag] = (parts, kinds, lands, s_sems, r_sems)
        return token

    def pair_finish(grp, token):
        tag, pieces = grp
        parts, kinds, lands, s_sems, r_sems = flight["pair" + tag]
        parts, token, recv = _pair_wait(parts, kinds, token, lands, s_sems, r_sems, f"grad_pair_wait_{tag}")
        for (ti, l), g, r in zip(pieces, parts, recv):
            own[ti] = _pair_sum(g, r, kinds6[ti], c_arr, own[ti], l, f"grad_pair_sum_{tag}_{ti}")
        return scatter(grp, 0, "start", token)

    for i in reversed(range(depth)):
        j = i // 2
        x_in, h, pre, mixed, lse, x_mid, h2, gu, act = saved[i]
        da = _mm(dxb, w_fdown, "nt", BF16, layer=i, n_outer=True, tm=1024, tn=2816, tk=4096, name=f"ffn_down_dx{i}")
        g_fdown[i] = _mm(act, dxb, "tn", BF16, tm=1408, tn=1024, tk=2048, name=f"ffn_down_dw{i}")
        dgu, dwc = _ffngate_bwd(gu, da, wf_conv[i], seq, f"ffn_gate_bwd{i}")
        g_fconv[i] = dwc[:3]
        g_fin[i] = _mm(h2, dgu, "tn", BF16, tm=1024, tn=1408, tk=2048, name=f"ffn_in_dw{i}")
        dgu = pair_begin(group(i, "f"), dgu)
        dx, dxb, dg, colsum = _rms_bwd(x_mid, dgu, w_fin, i, norm_ffn[i:i + 1], dx, f"ffn_in_dx_norm_bwd{i}")
        g_norm_ffn[i] = jnp.sum(dg, axis=0)
        if i + 1 < depth:
            dxb = scatter(group(i + 1, "m"), 1, "start", scatter(group(i + 1, "m"), 0, "wait", dxb))
            dxb = scatter(group(i + 1, "f"), 1, "wait", dxb)
        dxb = pair_finish(group(i, "f"), dxb)
        if i % 2 == 0:
            dmix = _mm(dxb, w_cout, "nt", BF16, layer=j, tm=512, tn=1024, tk=4096, name=f"conv_out_dx{i}")
            g_cout[j] = _mm(mixed, dxb, "tn", BF16, tm=1024, tn=1024, tk=2048, name=f"conv_out_dw{i}")
            dpre, dwc = _convgate_bwd(pre, dmix, wc_conv[j], seq, f"conv_gate_bwd{i}")
            g_cconv[j] = dwc[:3]
            g_cin[j] = _mm(h, dpre, "tn", BF16, tm=1024, tn=1536, tk=2048, name=f"conv_in_dw{i}")
            w_pre = w_cin
        else:
            g_bo[j] = jnp.sum(colsum, axis=0)
            dmix = _mm(dxb, w_o, "nt", F32, layer=j, tm=512, tn=1024, tk=4096, name=f"attn_out_dx{i}")
            g_o[j] = _mm(mixed, dxb, "tn", BF16, tm=1024, tn=1024, tk=2048, name=f"attn_out_dw{i}")
            dpre, dbias, dsk = _attn_bwd(pre, mixed, lse, dmix, attn_sinks[j], cos_t, sin_t, bsz, seq, f"attn_bwd{i}",
                                         hp=GROUP)
            g_bqkv[j] = jnp.sum(dbias, axis=0)
            g_sinks[j] = jnp.sum(dsk, axis=1)
            g_qkv[j] = _mm(h, dpre, "tn", BF16, tm=1024, tn=1536, tk=2048, name=f"qkv_dw{i}")
            w_pre = w_qkv
        dpre = pair_begin(group(i, "m"), dpre)
        dx, dxb, dg, _ = _rms_bwd(x_in, dpre, w_pre, j, norm_mix[i:i + 1], dx, f"mixer_in_dx_norm_bwd{i}")
        g_norm_mix[i] = jnp.sum(dg, axis=0)
        dxb = scatter(group(i, "f"), 1, "start", scatter(group(i, "f"), 0, "wait", dxb))
        if i + 1 < depth:
            dxb = scatter(group(i + 1, "m"), 1, "wait", dxb)
        dxb = pair_finish(group(i, "m"), dxb)
    grad_x = dx.reshape(bsz, seq, d)

    scatter(group(0, "f"), 1, "wait", dxb)
    big_m = [m_conv_w_in, m_conv_w_out, m_attn_w_qkv, m_attn_w_o, m_ffn_w_in, m_ffn_w_down]
    big_v = [v_conv_w_in, v_conv_w_out, v_attn_w_qkv, v_attn_w_o, v_ffn_w_in, v_ffn_w_down]
    big_names = ["conv_w_in", "conv_w_out", "attn_w_qkv", "attn_w_o", "ffn_w_in", "ffn_w_down"]
    qc_arr = jnp.stack([q, cq]).astype(jnp.int32)

    def adamw_of(n):
        return list(_reduce_adamw(own[n], mine[n], sib[n], big_w[n], big_m[n], big_v[n], qc_arr, f"adamw_{big_names[n]}"))

    big_upd = [None] * 6
    for n in (2, 3, 4, 5):
        big_upd[n] = adamw_of(n)
    token = scatter(group(0, "m"), 1, "start", scatter(group(0, "m"), 0, "wait", big_upd[5][1]))
    big_upd[5][1] = scatter(group(0, "m"), 1, "wait", token)
    for n in (0, 1):
        big_upd[n] = adamw_of(n)

    small = [jnp.stack(g_norm_mix), jnp.stack(g_norm_ffn), jnp.sum(dg_final, axis=0), jnp.stack(g_cconv),
             jnp.stack(g_bqkv), jnp.stack(g_sinks), jnp.stack(g_bo), jnp.stack(g_fconv)]
    sg = _unpack(_allreduce_small(_pack(small), "grad_small_allreduce"), small)

    def my_cols(v, like):
        width = like.shape[-1]
        return lax.dynamic_slice_in_dim(v, q * width, width, axis=v.ndim - 1)

    small_w = [norm_mix, norm_ffn, norm_final, conv_w_conv, attn_b_qkv, attn_sinks, attn_b_o, ffn_w_conv]
    small_m = [m_norm_mix, m_norm_ffn, m_norm_final, m_conv_w_conv, m_attn_b_qkv, m_attn_sinks, m_attn_b_o, m_ffn_w_conv]
    small_v = [v_norm_mix, v_norm_ffn, v_norm_final, v_conv_w_conv, v_attn_b_qkv, v_attn_sinks, v_attn_b_o, v_ffn_w_conv]
    small_g = [sg[0], sg[1], sg[2], my_cols(sg[3], conv_w_conv), my_cols(sg[4], attn_b_qkv), sg[5],
               my_cols(sg[6], attn_b_o), my_cols(sg[7], ffn_w_conv)]

    upd = {nm: tuple(u[1:]) for nm, u in zip(big_names, big_upd)}
    sd, sm, sv = _adamw(_pack(small_w)[None], _pack(small_g)[None], _pack(small_m)[None], _pack(small_v, 1.0)[None],
                        "adamw_small")
    sd, sm, sv = _unpack(sd, small_w), _unpack(sm, small_w), _unpack(sv, small_w)
    names = ["norm_mix", "norm_ffn", "norm_final", "conv_w_in", "conv_w_conv", "conv_w_out", "attn_w_qkv", "attn_b_qkv",
             "attn_sinks", "attn_w_o", "attn_b_o", "ffn_w_in", "ffn_w_conv", "ffn_w_down"]
    small_names = ["norm_mix", "norm_ffn", "norm_final", "conv_w_conv", "attn_b_qkv", "attn_sinks", "attn_b_o", "ffn_w_conv"]
    grads = dict(zip(small_names, small_g))
    grads.update({nm: u[0] for nm, u in zip(big_names, big_upd)})
    for n, nm in enumerate(small_names):
        upd[nm] = (sd[n], sm[n], sv[n])
    return (loss, grad_x, *[grads[nm] for nm in names], *[upd[nm][0] for nm in names],
            *[upd[nm][1] for nm in names], *[upd[nm][2] for nm in names])
```

```python
import math

import jax
import jax.numpy as jnp
from jax import lax
from jax.experimental import pallas as pl
from jax.experimental.pallas import tpu as pltpu

F32 = jnp.float32
BF16 = jnp.bfloat16

HEAD_DIM = 64
GROUP = 4
WINDOW = 128
EPS = 1e-5
ROPE_THETA = 10000.0
ADAM_LR, ADAM_B1, ADAM_B2, ADAM_EPS, ADAM_WD, ADAM_STEP = 0.001, 0.9, 0.999, 1e-08, 0.01, 10

N_CHIPS = 4
N_CORES = 2
N_DEV = 8
HALO = 16
VMEM_LIMIT_BYTES = 56 * 1024 * 1024
MESH = pl.DeviceIdType.MESH
ANY = pl.BlockSpec(memory_space=pl.ANY)
SMEM = pl.BlockSpec(memory_space=pltpu.SMEM)
NEG = float(jnp.finfo(jnp.float32).min)
ROW_TILES = (512, 256, 128, 64, 32, 16, 8)


def _pick(dim, cands):
    for c in cands:
        if dim % c == 0:
            return c
    return dim


def _params(sem):
    return pltpu.CompilerParams(dimension_semantics=sem, vmem_limit_bytes=VMEM_LIMIT_BYTES)


_DIMS = {"nn": (((1,), (0,)), ((), ())), "nt": (((1,), (1,)), ((), ())), "tn": (((0,), (0,)), ((), ()))}


def _mm(a, b, mode, out_dtype, *, layer=None, bias=None, residual=None, norm_g=None, n_outer=False, tm, tn, tk, name):
    b2 = b.shape[1:] if layer is not None else b.shape
    if mode == "nn":
        (m, k), n = a.shape, b2[1]
    elif mode == "nt":
        (m, k), n = a.shape, b2[0]
    else:
        (k, m), n = a.shape, b2[1]
    tm, tn, tk = min(tm, m), min(tn, n), min(tk, k)
    assert m % tm == 0 and n % tn == 0 and k % tk == 0, (name, a.shape, b.shape, tm, tn, tk)
    nk = k // tk

    def at(f):
        return (lambda p0, p1, p2: f(p1, p0, p2)) if n_outer else f

    a_spec = pl.BlockSpec((tk, tm), at(lambda i, j, l: (l, i))) if mode == "tn" else pl.BlockSpec((tm, tk), at(lambda i, j, l: (i, l)))
    if layer is None:
        b_spec = (pl.BlockSpec((tn, tk), at(lambda i, j, l: (j, l))) if mode == "nt"
                  else pl.BlockSpec((tk, tn), at(lambda i, j, l: (l, j))))
    elif mode == "nt":
        b_spec = pl.BlockSpec((None, tn, tk), at(lambda i, j, l: (layer, j, l)))
    else:
        b_spec = pl.BlockSpec((None, tk, tn), at(lambda i, j, l: (layer, l, j)))
    in_specs, args = [a_spec, b_spec], [a, b]
    if bias is not None:
        in_specs.append(pl.BlockSpec((1, tn), at(lambda i, j, l: (0, j))))
        args.append(bias)
    if residual is not None:
        in_specs.append(pl.BlockSpec((tm, tn), at(lambda i, j, l: (i, j))))
        args.append(residual)
    if norm_g is not None:
        assert tn == n, (name, "the RMSNorm of the result needs whole rows in a tile")
        in_specs.append(pl.BlockSpec((1, tn), at(lambda i, j, l: (0, j))))
        args.append(norm_g)
    has_bias, has_res, has_norm = bias is not None, residual is not None, norm_g is not None

    def body(*refs):
        a_ref, b_ref = refs[0], refs[1]
        pos = 2
        bias_ref = res_ref = g_ref = h_ref = None
        if has_bias:
            bias_ref, pos = refs[pos], pos + 1
        if has_res:
            res_ref, pos = refs[pos], pos + 1
        if has_norm:
            g_ref, pos = refs[pos], pos + 1
        o_ref, pos = refs[pos], pos + 1
        if has_norm:
            h_ref, pos = refs[pos], pos + 1
        acc_ref = refs[pos] if nk > 1 else None

        def finish(acc):
            if has_bias:
                acc = acc + bias_ref[...]
            if has_res:
                acc = acc + res_ref[...]
            o_ref[...] = acc.astype(o_ref.dtype)
            if has_norm:
                h_ref[...] = _rms(acc, g_ref[...]).astype(BF16)

        if nk == 1:
            finish(lax.dot_general(a_ref[...], b_ref[...], _DIMS[mode], preferred_element_type=F32))
            return
        l = pl.program_id(2)
        part = lax.dot_general(a_ref[...], b_ref[...], _DIMS[mode], preferred_element_type=F32)

        @pl.when(l == 0)
        def _():
            acc_ref[...] = part

        @pl.when(l > 0)
        def _():
            acc_ref[...] += part

        @pl.when(l == nk - 1)
        def _():
            finish(acc_ref[...])

    o_spec = pl.BlockSpec((tm, tn), at(lambda i, j, l: (i, j)))
    o_shape = jax.ShapeDtypeStruct((m, n), out_dtype)
    return pl.pallas_call(
        body,
        name=name,
        grid=(n // tn, m // tm, nk) if n_outer else (m // tm, n // tn, nk),
        in_specs=in_specs,
        out_specs=[o_spec, o_spec] if has_norm else o_spec,
        out_shape=[o_shape, jax.ShapeDtypeStruct((m, n), BF16)] if has_norm else o_shape,
        scratch_shapes=[pltpu.VMEM((tm, tn), F32)] if nk > 1 else [],
        compiler_params=_params(("parallel", "parallel", "arbitrary")),
    )(*args)


def _rms(x, g):
    return x * lax.rsqrt(jnp.mean(x * x, axis=-1, keepdims=True) + EPS) * g


def _fold8(v):
    r, d = v.shape
    return jnp.sum(v.reshape(r // 8, 8, d), axis=0)


def _rms_fwd(x, g, name):
    t, d = x.shape
    tm = _pick(t, ROW_TILES)

    def body(x_ref, g_ref, h_ref):
        xv = x_ref[...]
        r = lax.rsqrt(jnp.mean(xv * xv, axis=-1, keepdims=True) + EPS)
        h_ref[...] = (xv * r * g_ref[...]).astype(BF16)

    return pl.pallas_call(
        body,
        name=name,
        grid=(t // tm,),
        in_specs=[pl.BlockSpec((tm, d), lambda i: (i, 0)), pl.BlockSpec((1, d), lambda i: (0, 0))],
        out_specs=pl.BlockSpec((tm, d), lambda i: (i, 0)),
        out_shape=jax.ShapeDtypeStruct((t, d), BF16),
        compiler_params=_params(("parallel",)),
    )(x, g)


def _rms_bwd(x, dpre, w, layer, g, dx_in, name):
    t, d = x.shape
    k = dpre.shape[1]
    tm = _pick(t, ROW_TILES)

    sub = _pick(tm, (256, 128, 64, 32, 16, 8))

    def body(x_ref, dp_ref, w_ref, g_ref, dxi_ref, dx_ref, dxb_ref, dg_ref, cs_ref):
        i = pl.program_id(0)

        @pl.when(i == 0)
        def _():
            dg_ref[...] = jnp.zeros_like(dg_ref)
            cs_ref[...] = jnp.zeros_like(cs_ref)

        dg = jnp.zeros((8, d), F32)
        cs = jnp.zeros((8, d), F32)
        for r0 in range(0, tm, sub):
            rows = slice(r0, r0 + sub)
            xv = x_ref[rows, :]
            r = lax.rsqrt(jnp.mean(xv * xv, axis=-1, keepdims=True) + EPS)
            xhat = xv * r
            dy = lax.dot_general(dp_ref[rows, :], w_ref[...], _DIMS["nt"], preferred_element_type=F32)
            gdy = dy * g_ref[...]
            dx = dxi_ref[rows, :] + r * (gdy - xhat * jnp.mean(gdy * xhat, axis=-1, keepdims=True))
            dx_ref[rows, :] = dx
            dxb_ref[rows, :] = dx.astype(BF16)
            dg += _fold8(dy * xhat)
            cs += _fold8(dx)
        dg_ref[...] += dg
        cs_ref[...] += cs

    row = pl.BlockSpec((tm, d), lambda i: (i, 0))
    acc = pl.BlockSpec((8, d), lambda i: (0, 0))
    w_spec = pl.BlockSpec((None, d, k), lambda i: (layer, 0, 0), pipeline_mode=pl.Buffered(1))
    return pl.pallas_call(
        body,
        name=name,
        grid=(t // tm,),
        in_specs=[row, pl.BlockSpec((tm, k), lambda i: (i, 0)), w_spec, pl.BlockSpec((1, d), lambda i: (0, 0)), row],
        out_specs=[row, row, acc, acc],
        out_shape=[jax.ShapeDtypeStruct((t, d), F32), jax.ShapeDtypeStruct((t, d), BF16),
                   jax.ShapeDtypeStruct((8, d), F32), jax.ShapeDtypeStruct((8, d), F32)],
        compiler_params=_params(("arbitrary",)),
    )(x, dpre, w, g, dx_in)


def _loss_head(x, target, g, name):
    t, d = x.shape
    tm = _pick(t, ROW_TILES)
    inv_d = 1.0 / d

    def body(x_ref, t_ref, g_ref, dx_ref, dxb_ref, sq_ref, dg_ref):
        i = pl.program_id(0)
        xv = x_ref[...]
        gv = g_ref[...]
        r = lax.rsqrt(jnp.mean(xv * xv, axis=-1, keepdims=True) + EPS)
        xhat = xv * r
        err = xhat * gv - t_ref[...]
        dy = err * inv_d
        gdy = dy * gv
        dx = r * (gdy - xhat * jnp.mean(gdy * xhat, axis=-1, keepdims=True))
        dx_ref[...] = dx
        dxb_ref[...] = dx.astype(BF16)

        @pl.when(i == 0)
        def _():
            sq_ref[...] = jnp.zeros_like(sq_ref)
            dg_ref[...] = jnp.zeros_like(dg_ref)

        sq_ref[...] += _fold8(err * err)
        dg_ref[...] += _fold8(dy * xhat)

    row = pl.BlockSpec((tm, d), lambda i: (i, 0))
    acc = pl.BlockSpec((8, d), lambda i: (0, 0))
    return pl.pallas_call(
        body,
        name=name,
        grid=(t // tm,),
        in_specs=[row, row, pl.BlockSpec((1, d), lambda i: (0, 0))],
        out_specs=[row, row, acc, acc],
        out_shape=[jax.ShapeDtypeStruct((t, d), F32), jax.ShapeDtypeStruct((t, d), BF16),
                   jax.ShapeDtypeStruct((8, d), F32), jax.ShapeDtypeStruct((8, d), F32)],
        compiler_params=_params(("arbitrary",)),
    )(x, target, g)


def _rows(tm):
    return lax.broadcasted_iota(jnp.int32, (tm, 1), 0)


def _shift_down(u, before2):
    r8 = _rows(8)
    s1, s2 = pltpu.roll(u, 1, 0), pltpu.roll(u, 2, 0)
    top1 = jnp.where(r8 == 0, before2[1:2], s1[:8])
    top2 = jnp.where(r8 == 0, before2[0:1], jnp.where(r8 == 1, before2[1:2], s2[:8]))
    return jnp.concatenate([top1, s1[8:]], axis=0), jnp.concatenate([top2, s2[8:]], axis=0)


def _shift_up(u, after2):
    tm = u.shape[0]
    r8 = _rows(8)
    s1, s2 = pltpu.roll(u, tm - 1, 0), pltpu.roll(u, tm - 2, 0)
    bot1 = jnp.where(r8 == 7, after2[0:1], s1[tm - 8:])
    bot2 = jnp.where(r8 == 6, after2[0:1], jnp.where(r8 == 7, after2[1:2], s2[tm - 8:]))
    return jnp.concatenate([s1[:tm - 8], bot1], axis=0), jnp.concatenate([s2[:tm - 8], bot2], axis=0)


def _shift_matrix(tm, up):
    r = lax.broadcasted_iota(jnp.int32, (2 * tm, tm), 0)
    c = lax.broadcasted_iota(jnp.int32, (2 * tm, tm), 1)
    t = jnp.where(r >= tm, r - tm, r)
    k = jnp.where(r >= tm, 2, 1)
    return (c == (t + k if up else t - k)).astype(BF16)


def _shift_down_mxu(u, before2):
    tm = u.shape[0]
    moved = jnp.dot(_shift_matrix(tm, False), u.astype(BF16), preferred_element_type=F32)
    r8 = _rows(8)
    s1, s2 = moved[:tm], moved[tm:]
    top1 = s1[:8] + jnp.where(r8 == 0, before2[1:2], 0.0)
    top2 = s2[:8] + jnp.where(r8 == 0, before2[0:1], jnp.where(r8 == 1, before2[1:2], 0.0))
    return jnp.concatenate([top1, s1[8:]], axis=0), jnp.concatenate([top2, s2[8:]], axis=0)


def _shift_up_mxu(u, after2):
    tm = u.shape[0]
    moved = jnp.dot(_shift_matrix(tm, True), u.astype(BF16), preferred_element_type=F32)
    r8 = _rows(8)
    s1, s2 = moved[:tm], moved[tm:]
    bot1 = s1[tm - 8:] + jnp.where(r8 == 7, after2[0:1], 0.0)
    bot2 = s2[tm - 8:] + jnp.where(r8 == 6, after2[0:1], jnp.where(r8 == 7, after2[1:2], 0.0))
    return jnp.concatenate([s1[:tm - 8], bot1], axis=0), jnp.concatenate([s2[:tm - 8], bot2], axis=0)


def _conv_tile(seq):
    return _pick(seq, (256, 128, 64, 32, 16, 8))


def _halo_specs(tm, width, n_tiles):
    per = tm // HALO
    before = pl.BlockSpec((HALO, width), lambda i: (jnp.maximum(i * per - 1, 0), 0))
    after = pl.BlockSpec((HALO, width), lambda i: (jnp.minimum((i + 1) * per, n_tiles * per - 1), 0))
    return before, after


def _convgate_fwd(bcv, w, seq, name):
    t, d3 = bcv.shape
    d = d3 // 3
    tm = _conv_tile(seq)
    tps = seq // tm
    before, _ = _halo_specs(tm, d3, t // tm)

    def body(x_ref, xb_ref, w_ref, y_ref):
        i = pl.program_id(0)
        inner = (i % tps != 0).astype(F32)
        u = x_ref[:, d:2 * d].astype(F32) * x_ref[:, 2 * d:].astype(F32)
        xb = xb_ref[:, d:].astype(F32)[HALO - 2:]
        s1, s2 = _shift_down(u, xb[:, :d] * xb[:, d:] * inner)
        z = w_ref[2:3] * u + w_ref[1:2] * s1 + w_ref[0:1] * s2
        y_ref[...] = (x_ref[:, :d].astype(F32) * z).astype(BF16)

    return pl.pallas_call(
        body,
        name=name,
        grid=(t // tm,),
        in_specs=[pl.BlockSpec((tm, d3), lambda i: (i, 0)), before, pl.BlockSpec((3, d), lambda i: (0, 0))],
        out_specs=pl.BlockSpec((tm, d), lambda i: (i, 0)),
        out_shape=jax.ShapeDtypeStruct((t, d), BF16),
        compiler_params=_params(("parallel",)),
    )(bcv, bcv, w)


def _convgate_bwd(bcv, dy, w, seq, name):
    t, d3 = bcv.shape
    d = d3 // 3
    tm = _conv_tile(seq)
    tps = seq // tm
    before, after = _halo_specs(tm, d3, t // tm)
    _, after_dy = _halo_specs(tm, d, t // tm)

    def body(x_ref, xb_ref, xa_ref, dy_ref, dya_ref, w_ref, dx_ref, dw_ref):
        i = pl.program_id(0)
        inner_lo = (i % tps != 0).astype(F32)
        inner_hi = (i % tps != tps - 1).astype(F32)
        w0, w1, w2 = w_ref[0:1], w_ref[1:2], w_ref[2:3]
        b, c, v = x_ref[:, :d].astype(F32), x_ref[:, d:2 * d].astype(F32), x_ref[:, 2 * d:].astype(F32)
        u = c * v
        xb = xb_ref[:, d:].astype(F32)[HALO - 2:]
        s1, s2 = _shift_down(u, xb[:, :d] * xb[:, d:] * inner_lo)
        z = w2 * u + w1 * s1 + w0 * s2
        dyv = dy_ref[...].astype(F32)
        dz = dyv * b
        dza = dya_ref[...].astype(F32)[0:2] * xa_ref[:, :d].astype(F32)[0:2] * inner_hi
        n1, n2 = _shift_up(dz, dza)
        du = w2 * dz + w1 * n1 + w0 * n2
        dx_ref[:, :d] = (dyv * z).astype(BF16)
        dx_ref[:, d:2 * d] = (du * v).astype(BF16)
        dx_ref[:, 2 * d:] = (du * c).astype(BF16)

        @pl.when(i == 0)
        def _():
            dw_ref[...] = jnp.zeros_like(dw_ref)

        dw_ref[0:1] += jnp.sum(dz * s2, axis=0, keepdims=True)
        dw_ref[1:2] += jnp.sum(dz * s1, axis=0, keepdims=True)
        dw_ref[2:3] += jnp.sum(dz * u, axis=0, keepdims=True)

    return pl.pallas_call(
        body,
        name=name,
        grid=(t // tm,),
        in_specs=[pl.BlockSpec((tm, d3), lambda i: (i, 0)), before, after,
                  pl.BlockSpec((tm, d), lambda i: (i, 0)), after_dy, pl.BlockSpec((3, d), lambda i: (0, 0))],
        out_specs=[pl.BlockSpec((tm, d3), lambda i: (i, 0)), pl.BlockSpec((8, d), lambda i: (0, 0))],
        out_shape=[jax.ShapeDtypeStruct((t, d3), BF16), jax.ShapeDtypeStruct((8, d), F32)],
        compiler_params=_params(("arbitrary",)),
    )(bcv, bcv, bcv, dy, dy, w)


def _sigmoid(x):
    return 1.0 / (1.0 + jnp.exp(-x))


def _ffn_gate_down_fwd(gu, w, w_down, layer, resid, norm_g, seq, name):
    t, f2 = gu.shape
    f = f2 // 2
    d = w_down.shape[2]
    sub = _conv_tile(seq)
    tm = _pick(seq, (2 * sub, sub))
    tps = seq // tm
    before, _ = _halo_specs(tm, f2, t // tm)

    def body(x_ref, xb_ref, w_ref, wd_ref, res_ref, g_ref, o_ref, a_ref, h_ref):
        i = pl.program_id(0)
        inner = (i % tps != 0).astype(F32)
        for r0 in range(0, tm, sub):
            rows = slice(r0, r0 + sub)
            if r0 == 0:
                halo = xb_ref[:, :f].astype(F32)[HALO - 2:] * inner
            else:
                halo = x_ref[r0 - HALO:r0, :f].astype(F32)[HALO - 2:]
            s1, s2 = _shift_down_mxu(x_ref[rows, :f], halo)
            gc = w_ref[2:3] * x_ref[rows, :f].astype(F32) + w_ref[1:2] * s1 + w_ref[0:1] * s2
            act = (gc * _sigmoid(gc) * x_ref[rows, f:].astype(F32)).astype(BF16)
            a_ref[rows, :] = act
            out = jnp.dot(act, wd_ref[...], preferred_element_type=F32) + res_ref[rows, :]
            o_ref[rows, :] = out
            h_ref[rows, :] = _rms(out, g_ref[...]).astype(BF16)

    row_d = pl.BlockSpec((tm, d), lambda i: (i, 0))
    return pl.pallas_call(
        body,
        name=name,
        grid=(t // tm,),
        in_specs=[pl.BlockSpec((tm, f2), lambda i: (i, 0)), before, pl.BlockSpec((3, f), lambda i: (0, 0)),
                  pl.BlockSpec((None, f, d), lambda i: (layer, 0, 0), pipeline_mode=pl.Buffered(1)),
                  row_d, pl.BlockSpec((1, d), lambda i: (0, 0))],
        out_specs=[row_d, pl.BlockSpec((tm, f), lambda i: (i, 0)), row_d],
        out_shape=[jax.ShapeDtypeStruct((t, d), F32), jax.ShapeDtypeStruct((t, f), BF16), jax.ShapeDtypeStruct((t, d), BF16)],
        compiler_params=_params(("parallel",)),
    )(gu, gu, w, w_down, resid, norm_g)


def _ffngate_bwd(gu, da, w, seq, name):
    t, f2 = gu.shape
    f = f2 // 2
    tm = _conv_tile(seq)
    tps = seq // tm
    before, after = _halo_specs(tm, f2, t // tm)
    _, after_da = _halo_specs(tm, f, t // tm)

    def body(x_ref, xb_ref, xa_ref, da_ref, daa_ref, w_ref, dx_ref, dw_ref):
        i = pl.program_id(0)
        inner_lo = (i % tps != 0).astype(F32)
        inner_hi = (i % tps != tps - 1).astype(F32)
        w0, w1, w2 = w_ref[0:1], w_ref[1:2], w_ref[2:3]

        def dgate(gc, uv, dav):
            sg = _sigmoid(gc)
            return dav * uv * (sg * (1.0 + gc * (1.0 - sg))), dav * (gc * sg)

        g, u = x_ref[:, :f].astype(F32), x_ref[:, f:].astype(F32)
        s1, s2 = _shift_down_mxu(x_ref[:, :f], xb_ref[:, :f].astype(F32)[HALO - 2:] * inner_lo)
        gc = w2 * g + w1 * s1 + w0 * s2
        dgc, du = dgate(gc, u, da_ref[...].astype(F32))
        ga = xa_ref[:, :f].astype(F32)
        a1, a2 = _shift_down(ga, x_ref[tm - HALO:, :f].astype(F32)[HALO - 2:])
        gca = w2 * ga + w1 * a1 + w0 * a2
        dgca, _ = dgate(gca, xa_ref[:, f:].astype(F32), daa_ref[...].astype(F32))
        n1, n2 = _shift_up_mxu(dgc, dgca[0:2] * inner_hi)
        dx_ref[:, :f] = (w2 * dgc + w1 * n1 + w0 * n2).astype(BF16)
        dx_ref[:, f:] = du.astype(BF16)

        @pl.when(i == 0)
        def _():
            dw_ref[...] = jnp.zeros_like(dw_ref)

        dw_ref[0:1] += jnp.sum(dgc * s2, axis=0, keepdims=True)
        dw_ref[1:2] += jnp.sum(dgc * s1, axis=0, keepdims=True)
        dw_ref[2:3] += jnp.sum(dgc * g, axis=0, keepdims=True)

    return pl.pallas_call(
        body,
        name=name,
        grid=(t // tm,),
        in_specs=[pl.BlockSpec((tm, f2), lambda i: (i, 0)), before, after,
                  pl.BlockSpec((tm, f), lambda i: (i, 0)), after_da, pl.BlockSpec((3, f), lambda i: (0, 0))],
        out_specs=[pl.BlockSpec((tm, f2), lambda i: (i, 0)), pl.BlockSpec((8, f), lambda i: (0, 0))],
        out_shape=[jax.ShapeDtypeStruct((t, f2), BF16), jax.ShapeDtypeStruct((8, f), F32)],
        compiler_params=_params(("arbitrary",)),
    )(gu, gu, gu, da, da, w)


def _swap_halves(xt):
    half = HEAD_DIM // 2
    return jnp.concatenate([xt[half:], xt[:half]], axis=0)


def _rope(xt, cos, sin):
    return xt * cos + _swap_halves(xt) * sin


def _unrope(dxt, cos, sin):
    return dxt * cos - _swap_halves(dxt) * sin


def _key_query(count):
    kj = lax.broadcasted_iota(jnp.int32, (WINDOW, count * WINDOW), 0)
    qi = lax.broadcasted_iota(jnp.int32, (WINDOW, count * WINDOW), 1) & (WINDOW - 1)
    return kj, qi


def _band_masks(n, count):
    kj, qi = _key_query(count)
    return kj <= qi, jnp.logical_and(kj > qi, n > 0)


def _lanes(v, count):
    return jnp.concatenate([v] * count, axis=1) if count > 1 else v


def _heads(ref, h0, count):
    parts = [ref[(h0 + g) * HEAD_DIM:(h0 + g + 1) * HEAD_DIM, :] for g in range(count)]
    return jnp.concatenate(parts, axis=1) if count > 1 else parts[0]


def _head_rows(ref, h0, count):
    parts = [ref[h0 + g:h0 + g + 1, :] for g in range(count)]
    return jnp.concatenate(parts, axis=1) if count > 1 else parts[0]


def _head_sinks(sink_ref, h0, count):
    parts = [jnp.full((1, WINDOW), sink_ref[h0 + g], F32) for g in range(count)]
    return jnp.concatenate(parts, axis=1) if count > 1 else parts[0]


def _tn(a, b):
    return lax.dot_general(a, b, _DIMS["tn"], preferred_element_type=F32)


def _nt(a, b):
    return lax.dot_general(a, b, _DIMS["nt"], preferred_element_type=F32)


def _nn(a, b):
    return jnp.dot(a, b, preferred_element_type=F32)


def _attn_fwd(qkv, sinks, cos_t, sin_t, bsz, seq, name, hp):
    t, qw = qkv.shape
    d = qw * 2 // 3
    kvw = d // GROUP
    n_heads, n_kv = d // HEAD_DIM, kvw // HEAD_DIM
    nb = seq // WINDOW
    scale = HEAD_DIM ** -0.5

    def body(sink_ref, xc_ref, xp_ref, cc_ref, sc_ref, cp_ref, sp_ref, o_ref, lse_ref, xt_ref, pt_ref, ot_ref):
        n = pl.program_id(1)
        xt_ref[...] = xc_ref[...].T
        pt_ref[...] = xp_ref[:, d:].T
        cos_c, sin_c, cos_p, sin_p = cc_ref[...], sc_ref[...], cp_ref[...], sp_ref[...]
        cos_g, sin_g = _lanes(cos_c, hp), _lanes(sin_c, hp)
        valid_c, valid_p = _band_masks(n, hp)
        for j in range(n_kv):
            ko = j * HEAD_DIM
            kc = _rope(xt_ref[d + ko:d + ko + HEAD_DIM, :], cos_c, sin_c).astype(BF16)
            kp = _rope(pt_ref[ko:ko + HEAD_DIM, :], cos_p, sin_p).astype(BF16)
            vc = xt_ref[d + kvw + ko:d + kvw + ko + HEAD_DIM, :].astype(BF16)
            vp = pt_ref[kvw + ko:kvw + ko + HEAD_DIM, :].astype(BF16)
            for h0 in range(j * GROUP, (j + 1) * GROUP, hp):
                q = _rope(_heads(xt_ref, h0, hp), cos_g, sin_g).astype(BF16)
                sink = _head_sinks(sink_ref, h0, hp)
                s_c = jnp.where(valid_c, _tn(kc, q) * scale, NEG)
                s_p = jnp.where(valid_p, _tn(kp, q) * scale, NEG)
                m = jnp.maximum(jnp.maximum(jnp.max(s_c, axis=0, keepdims=True), jnp.max(s_p, axis=0, keepdims=True)), sink)
                p_c = jnp.exp(s_c - m)
                p_p = jnp.exp(s_p - m)
                den = jnp.sum(p_c, axis=0, keepdims=True) + jnp.sum(p_p, axis=0, keepdims=True) + jnp.exp(sink - m)
                inv = 1.0 / den
                o_g = _nn(vc, (p_c * inv).astype(BF16)) + _nn(vp, (p_p * inv).astype(BF16))
                lse_g = m + jnp.log(den)
                for g in range(hp):
                    h = h0 + g
                    ot_ref[h * HEAD_DIM:(h + 1) * HEAD_DIM, :] = o_g[:, g * WINDOW:(g + 1) * WINDOW]
                    lse_ref[h:h + 1, :] = lse_g[:, g * WINDOW:(g + 1) * WINDOW]
        o_ref[...] = ot_ref[...].T.astype(BF16)

    cur = lambda b, n: (b * nb + n, 0)
    prev = lambda b, n: (b * nb + jnp.maximum(n - 1, 0), 0)
    tab_c = pl.BlockSpec((HEAD_DIM, WINDOW), lambda b, n: (0, n))
    tab_p = pl.BlockSpec((HEAD_DIM, WINDOW), lambda b, n: (0, jnp.maximum(n - 1, 0)))
    return pl.pallas_call(
        body,
        name=name,
        grid=(bsz, nb),
        in_specs=[SMEM, pl.BlockSpec((WINDOW, qw), cur), pl.BlockSpec((WINDOW, qw), prev), tab_c, tab_c, tab_p, tab_p],
        out_specs=[pl.BlockSpec((WINDOW, d), cur), pl.BlockSpec((n_heads, WINDOW), lambda b, n: (0, b * nb + n))],
        out_shape=[jax.ShapeDtypeStruct((t, d), BF16), jax.ShapeDtypeStruct((n_heads, t), F32)],
        scratch_shapes=[pltpu.VMEM((qw, WINDOW), F32), pltpu.VMEM((2 * kvw, WINDOW), F32), pltpu.VMEM((d, WINDOW), F32)],
        compiler_params=_params(("parallel", "arbitrary")),
    )(sinks, qkv, qkv, cos_t, sin_t, cos_t, sin_t)


def _attn_bwd(qkv, o, lse, do, sinks, cos_t, sin_t, bsz, seq, name, hp):
    t, qw = qkv.shape
    d = qw * 2 // 3
    kvw = d // GROUP
    n_heads, n_kv = d // HEAD_DIM, kvw // HEAD_DIM
    nb = seq // WINDOW
    scale = HEAD_DIM ** -0.5

    def body(sink_ref, xc_ref, xp_ref, oc_ref, doc_ref, lc_ref, cc_ref, sc_ref, cp_ref, sp_ref,
             dx_ref, db_ref, dsk_ref, xt_ref, pt_ref, otc_ref, dtc_ref, gt_ref, carry_ref):
        b, n = pl.program_id(0), pl.program_id(1)
        live = n < nb
        xt_ref[...] = xc_ref[...].T
        pt_ref[...] = xp_ref[:, d:].T
        otc_ref[...] = oc_ref[...].astype(F32).T
        dtc_ref[...] = doc_ref[...].T
        cos_c, sin_c, cos_p, sin_p = cc_ref[...], sc_ref[...], cp_ref[...], sp_ref[...]
        cos_g, sin_g = _lanes(cos_c, hp), _lanes(sin_c, hp)
        kj, qi = _key_query(hp)
        valid_c = jnp.logical_and(kj <= qi, live)
        valid_p = jnp.logical_and(kj > qi, jnp.logical_and(n > 0, live))

        @pl.when(jnp.logical_and(b == 0, n == 0))
        def _():
            db_ref[...] = jnp.zeros_like(db_ref)
            dsk_ref[...] = jnp.zeros_like(dsk_ref)

        @pl.when(n == 0)
        def _():
            carry_ref[...] = jnp.zeros_like(carry_ref)

        for j in range(n_kv):
            ko = j * HEAD_DIM
            k_rows = slice(d + ko, d + ko + HEAD_DIM)
            v_rows = slice(d + kvw + ko, d + kvw + ko + HEAD_DIM)
            kc = _rope(xt_ref[k_rows, :], cos_c, sin_c).astype(BF16)
            kp = _rope(pt_ref[ko:ko + HEAD_DIM, :], cos_p, sin_p).astype(BF16)
            vc = xt_ref[v_rows, :].astype(BF16)
            vp = pt_ref[kvw + ko:kvw + ko + HEAD_DIM, :].astype(BF16)
            dk_c = jnp.zeros((HEAD_DIM, WINDOW), F32)
            dv_c = jnp.zeros((HEAD_DIM, WINDOW), F32)
            dk_p = jnp.zeros((HEAD_DIM, WINDOW), F32)
            dv_p = jnp.zeros((HEAD_DIM, WINDOW), F32)
            for h0 in range(j * GROUP, (j + 1) * GROUP, hp):
                q = _rope(_heads(xt_ref, h0, hp), cos_g, sin_g).astype(BF16)
                do_g = _heads(dtc_ref, h0, hp)
                do_b = do_g.astype(BF16)
                lse_g = _head_rows(lc_ref, h0, hp)
                delta = jnp.sum(_heads(otc_ref, h0, hp) * do_g, axis=0, keepdims=True)
                p_c = jnp.exp(jnp.where(valid_c, _tn(kc, q) * scale, NEG) - lse_g)
                p_p = jnp.exp(jnp.where(valid_p, _tn(kp, q) * scale, NEG) - lse_g)
                ds_c = (p_c * (_tn(vc, do_b) - delta)).astype(BF16)
                ds_p = (p_p * (_tn(vp, do_b) - delta)).astype(BF16)
                dq = _unrope((_nn(kc, ds_c) + _nn(kp, ds_p)) * scale, cos_g, sin_g)
                dsk = jnp.where(live, -jnp.exp(_head_sinks(sink_ref, h0, hp) - lse_g) * delta, 0.0)
                for g in range(hp):
                    rows = slice((h0 + g) * HEAD_DIM, (h0 + g + 1) * HEAD_DIM)
                    gt_ref[rows, :] = carry_ref[rows, :]
                    carry_ref[rows, :] = dq[:, g * WINDOW:(g + 1) * WINDOW]
                    dsk_ref[h0 + g:h0 + g + 1, :] += dsk[:, g * WINDOW:(g + 1) * WINDOW]
                dv_c += _nt(do_b, p_c.astype(BF16))
                dk_c += _nt(q, ds_c)
                dv_p += _nt(do_b, p_p.astype(BF16))
                dk_p += _nt(q, ds_p)
            gt_ref[k_rows, :] = _unrope((carry_ref[k_rows, :] + dk_p) * scale, cos_p, sin_p)
            gt_ref[v_rows, :] = carry_ref[v_rows, :] + dv_p
            carry_ref[k_rows, :] = dk_c
            carry_ref[v_rows, :] = dv_c
        dx = gt_ref[...].T
        dx_ref[...] = dx.astype(BF16)
        db_ref[...] += _fold8(dx)

    cur = lambda b, n: (b * nb + jnp.minimum(n, nb - 1), 0)
    prev = lambda b, n: (b * nb + jnp.maximum(jnp.minimum(n, nb - 1) - 1, 0), 0)
    done = lambda b, n: (b * nb + jnp.maximum(n - 1, 0), 0)
    stat_c = pl.BlockSpec((n_heads, WINDOW), lambda b, n: (0, b * nb + jnp.minimum(n, nb - 1)))
    tab_c = pl.BlockSpec((HEAD_DIM, WINDOW), lambda b, n: (0, jnp.minimum(n, nb - 1)))
    tab_p = pl.BlockSpec((HEAD_DIM, WINDOW), lambda b, n: (0, jnp.maximum(n - 1, 0)))
    return pl.pallas_call(
        body,
        name=name,
        grid=(bsz, nb + 1),
        in_specs=[SMEM, pl.BlockSpec((WINDOW, qw), cur), pl.BlockSpec((WINDOW, qw), prev),
                  pl.BlockSpec((WINDOW, d), cur), pl.BlockSpec((WINDOW, d), cur), stat_c, tab_c, tab_c, tab_p, tab_p],
        out_specs=[pl.BlockSpec((WINDOW, qw), done), pl.BlockSpec((8, qw), lambda b, n: (0, 0)),
                   pl.BlockSpec((n_heads, WINDOW), lambda b, n: (0, 0))],
        out_shape=[jax.ShapeDtypeStruct((t, qw), BF16), jax.ShapeDtypeStruct((8, qw), F32),
                   jax.ShapeDtypeStruct((n_heads, WINDOW), F32)],
        scratch_shapes=[pltpu.VMEM((qw, WINDOW), F32), pltpu.VMEM((2 * kvw, WINDOW), F32), pltpu.VMEM((d, WINDOW), F32),
                        pltpu.VMEM((d, WINDOW), F32), pltpu.VMEM((qw, WINDOW), F32), pltpu.VMEM((qw, WINDOW), F32)],
        compiler_params=_params(("arbitrary", "arbitrary")),
    )(sinks, qkv, qkv, o, do, lse, cos_t, sin_t, cos_t, sin_t)


def _place():
    return lax.axis_index("x"), lax.axis_index("y"), lax.axis_index("c")


def _other_chips(x, y):
    return [(1 - x, y), (x, 1 - y), (1 - x, 1 - y)]


def _place_shard(w, axis, q, name):
    ly, k, n = w.shape
    tr = _pick(k, (256, 128, 64, 32, 16, 8))
    steps = k // tr
    shape = (ly, k * N_CHIPS, n) if axis == 1 else (ly, k, n * N_CHIPS)
    if axis == 1:
        out_spec = pl.BlockSpec((None, tr, n), lambda l, i, q_ref: (l, q_ref[0] * steps + i, 0))
    else:
        out_spec = pl.BlockSpec((None, tr, n), lambda l, i, q_ref: (l, i, q_ref[0]))

    def body(q_ref, w_ref, o_ref):
        del q_ref
        o_ref[...] = w_ref[...].astype(BF16)

    return pl.pallas_call(
        body,
        name=name,
        grid_spec=pltpu.PrefetchScalarGridSpec(
            num_scalar_prefetch=1, grid=(ly, steps),
            in_specs=[pl.BlockSpec((None, tr, n), lambda l, i, q_ref: (l, i, 0))], out_specs=out_spec),
        out_shape=jax.ShapeDtypeStruct(shape, BF16),
        compiler_params=_params(("parallel", "parallel")),
    )(q, w)


def _half_block(ref, axis, layer, px, py, pc):
    blk = 2 * px + py
    if axis == 1:
        rows = ref.shape[1] // (2 * N_CHIPS)
        return ref.at[layer, pl.ds(pl.multiple_of((2 * blk + pc) * rows, 8), rows), :]
    rows, width = ref.shape[1] // 2, ref.shape[2] // N_CHIPS
    return ref.at[layer, pl.ds(pl.multiple_of(pc * rows, 8), rows), pl.ds(pl.multiple_of(blk * width, 128), width)]


def _gather_copy(refs, axes, pieces, send_sems, recv_sems, p, k, stage, whose):
    x, y, c = _place()
    chip = _other_chips(x, y)[k]
    i, layer = pieces[p]
    if stage == 0:
        origin = (x, y, c) if whose == "mine" else (*chip, c)
        to = (*chip, c)
    else:
        origin = (*chip, c) if whose == "mine" else (*chip, 1 - c)
        to = (x, y, 1 - c)
    blk = _half_block(refs[i], axes[i], layer, *origin)
    return pltpu.make_async_remote_copy(src_ref=blk, dst_ref=blk, send_sem=send_sems.at[p * 3 + k],
                                        recv_sem=recv_sems.at[p * 3 + k], device_id=to, device_id_type=MESH)


HBM_SPEC = pl.BlockSpec(memory_space=pltpu.HBM)
SEM_SPEC = pl.BlockSpec(memory_space=pltpu.SEMAPHORE)


def _gather_start(fulls, axes, pieces, stage, name):
    n, m = len(fulls), 3 * len(pieces)

    def body(*refs):
        src = refs[:n]
        send_sems, recv_sems = refs[2 * n], refs[2 * n + 1]
        for p in range(len(pieces)):
            for k in range(3):
                _gather_copy(src, axes, pieces, send_sems, recv_sems, p, k, stage, "mine").start()

    out = pl.pallas_call(
        body,
        name=name,
        in_specs=[HBM_SPEC] * n,
        out_specs=[HBM_SPEC] * n + [SEM_SPEC, SEM_SPEC],
        out_shape=[pltpu.HBM(f.shape, f.dtype) for f in fulls] + [pltpu.SemaphoreType.DMA((m,)), pltpu.SemaphoreType.DMA((m,))],
        input_output_aliases={i: i for i in range(n)},
        compiler_params=pltpu.CompilerParams(has_side_effects=pltpu.SideEffectType.DATAFLOW_SIDE_EFFECTING),
    )(*[pltpu.with_memory_space_constraint(f, pltpu.HBM) for f in fulls])
    return list(out[:n]), out[n], out[n + 1]


def _gather_wait(fulls, send_sems, recv_sems, after, axes, pieces, stage, name):
    n = len(fulls)

    def body(*refs):
        src = refs[:n]
        s_sems, r_sems = refs[n], refs[n + 1]
        for p in range(len(pieces)):
            for k in range(3):
                _gather_copy(src, axes, pieces, s_sems, r_sems, p, k, stage, "mine").wait_send()
                _gather_copy(src, axes, pieces, s_sems, r_sems, p, k, stage, "theirs").wait_recv()

    out = pl.pallas_call(
        body,
        name=name,
        in_specs=[HBM_SPEC] * n + [SEM_SPEC, SEM_SPEC, ANY],
        out_specs=[HBM_SPEC] * n,
        out_shape=[pltpu.HBM(f.shape, f.dtype) for f in fulls],
        input_output_aliases={i: i for i in range(n)},
        compiler_params=pltpu.CompilerParams(has_side_effects=pltpu.SideEffectType.DATAFLOW_SIDE_EFFECTING),
    )(*fulls, send_sems, recv_sems, after)
    return list(out)


def _half_shape(kind, shape):
    if kind == "col":
        return (shape[0] // 2, shape[1])
    return (N_CHIPS, shape[1] // 2, shape[2])


def _half_of(kind, ref, h):
    if kind == "col":
        r = ref.shape[0] // 2
        return ref.at[pl.ds(pl.multiple_of(h * r, 8), r), :]
    r = ref.shape[1] // 2
    return ref.at[:, pl.ds(pl.multiple_of(h * r, 8), r), :]


def _pair_copy(src, dst, kinds, send_sems, recv_sems, i):
    x, y, c = _place()
    return pltpu.make_async_remote_copy(src_ref=_half_of(kinds[i], src[i], 1 - c), dst_ref=dst[i], send_sem=send_sems.at[i],
                                        recv_sem=recv_sems.at[i], device_id=(x, y, 1 - c), device_id_type=MESH)


def _pair_start(grads, kinds, token, name):
    n = len(grads)
    lands = [pltpu.HBM(_half_shape(kd, g.shape), g.dtype) for g, kd in zip(grads, kinds)]

    def body(*refs):
        src, dst = refs[:n], refs[2 * n + 2:3 * n + 2]
        send_sems, recv_sems = refs[3 * n + 2], refs[3 * n + 3]
        for i in range(n):
            _pair_copy(src, dst, kinds, send_sems, recv_sems, i).start()

    arrays = list(grads) + [token]
    out = pl.pallas_call(
        body,
        name=name,
        in_specs=[HBM_SPEC] * (n + 1),
        out_specs=[HBM_SPEC] * (2 * n + 1) + [SEM_SPEC, SEM_SPEC],
        out_shape=[pltpu.HBM(a.shape, a.dtype) for a in arrays] + lands + [pltpu.SemaphoreType.DMA((n,)), pltpu.SemaphoreType.DMA((n,))],
        input_output_aliases={i: i for i in range(n + 1)},
        compiler_params=pltpu.CompilerParams(has_side_effects=pltpu.SideEffectType.DATAFLOW_SIDE_EFFECTING),
    )(*[pltpu.with_memory_space_constraint(a, pltpu.HBM) for a in arrays])
    return list(out[:n]), out[n], list(out[n + 1:2 * n + 1]), out[2 * n + 1], out[2 * n + 2]


def _pair_wait(grads, kinds, token, lands, send_sems, recv_sems, name):
    n = len(grads)

    def body(*refs):
        src, dst = refs[:n], refs[n + 1:2 * n + 1]
        s_sems, r_sems = refs[2 * n + 1], refs[2 * n + 2]
        for i in range(n):
            cp = _pair_copy(src, dst, kinds, s_sems, r_sems, i)
            cp.wait_send()
            cp.wait_recv()

    arrays = list(grads) + [token] + list(lands)
    out = pl.pallas_call(
        body,
        name=name,
        in_specs=[HBM_SPEC] * (2 * n + 1) + [SEM_SPEC, SEM_SPEC],
        out_specs=[HBM_SPEC] * (2 * n + 1),
        out_shape=[pltpu.HBM(a.shape, a.dtype) for a in arrays],
        input_output_aliases={i: i for i in range(2 * n + 1)},
        compiler_params=pltpu.CompilerParams(has_side_effects=pltpu.SideEffectType.DATAFLOW_SIDE_EFFECTING),
    )(*arrays, send_sems, recv_sems)
    return list(out[:n]), out[n], list(out[n + 1:])


def _pair_sum(grad, recv, kind, c, own, layer, name):
    r, cols = own.shape[2:]
    if kind == "col":
        tr = _pick(r, (256, 128, 64, 32, 16, 8))
        steps = r // tr
        grid = (N_CHIPS, steps)
        g_spec = pl.BlockSpec((tr, cols), lambda s, i, c_ref: (c_ref[0] * steps + i, s))
        r_spec = pl.BlockSpec((tr, cols), lambda s, i, c_ref: (i, s))
        o_spec = pl.BlockSpec((None, None, tr, cols), lambda s, i, c_ref: (layer, s, i, 0))
        g_in = grad
    else:
        grid = (N_CHIPS, 1)
        g_spec = pl.BlockSpec((None, None, r, cols), lambda s, i, c_ref: (s, c_ref[0], 0, 0))
        r_spec = pl.BlockSpec((None, r, cols), lambda s, i, c_ref: (s, 0, 0))
        o_spec = pl.BlockSpec((None, None, r, cols), lambda s, i, c_ref: (layer, s, 0, 0))
        g_in = grad.reshape(N_CHIPS, 2, r, cols)

    def body(c_ref, g_ref, r_ref, own_ref, o_ref):
        del c_ref, own_ref
        o_ref[...] = (g_ref[...].astype(F32) + r_ref[...].astype(F32)).astype(o_ref.dtype)

    return pl.pallas_call(
        body,
        name=name,
        grid_spec=pltpu.PrefetchScalarGridSpec(num_scalar_prefetch=1, grid=grid, in_specs=[g_spec, r_spec, ANY], out_specs=o_spec),
        out_shape=jax.ShapeDtypeStruct(own.shape, own.dtype),
        input_output_aliases={3: 0},
        compiler_params=_params(("parallel", "parallel")),
    )(c, g_in, recv, own)


def _scatter_copy(own, mine, sib, pieces, send_sems, recv_sems, p, k, stage, whose):
    x, y, c = _place()
    q = 2 * x + y
    i, layer = pieces[p]
    per = 4 if stage == 0 else 3
    if k == 3:
        src, dst, to = own[i].at[layer, q], sib[i].at[layer, q], (x, y, 1 - c)
    else:
        chip = _other_chips(x, y)[k]
        slot = 2 * chip[0] + chip[1]
        if stage == 0:
            to = (*chip, c)
            src, dst = (own[i].at[layer, slot], mine[i].at[layer, q]) if whose == "mine" else (own[i].at[layer, q], mine[i].at[layer, slot])
        else:
            to = (x, y, 1 - c)
            src, dst = mine[i].at[layer, slot], sib[i].at[layer, slot]
    return pltpu.make_async_remote_copy(src_ref=src, dst_ref=dst, send_sem=send_sems.at[p * per + k],
                                        recv_sem=recv_sems.at[p * per + k], device_id=to, device_id_type=MESH)


def _scatter_start(own, mine, sib, token, pieces, stage, name):
    n = len(own)
    per = 4 if stage == 0 else 3
    m = per * len(pieces)
    n_arr = 3 * n + 1

    def body(*refs):
        o, mi, si = refs[:n], refs[n:2 * n], refs[2 * n:3 * n]
        send_sems, recv_sems = refs[2 * n_arr], refs[2 * n_arr + 1]
        for p in range(len(pieces)):
            for k in range(per):
                _scatter_copy(o, mi, si, pieces, send_sems, recv_sems, p, k, stage, "mine").start()

    arrays = list(own) + list(mine) + list(sib) + [token]
    out = pl.pallas_call(
        body,
        name=name,
        in_specs=[HBM_SPEC] * n_arr,
        out_specs=[HBM_SPEC] * n_arr + [SEM_SPEC, SEM_SPEC],
        out_shape=[pltpu.HBM(a.shape, a.dtype) for a in arrays] + [pltpu.SemaphoreType.DMA((m,)), pltpu.SemaphoreType.DMA((m,))],
        input_output_aliases={i: i for i in range(n_arr)},
        compiler_params=pltpu.CompilerParams(has_side_effects=pltpu.SideEffectType.DATAFLOW_SIDE_EFFECTING),
    )(*[pltpu.with_memory_space_constraint(a, pltpu.HBM) for a in arrays])
    return list(out[:n]), list(out[n:2 * n]), list(out[2 * n:3 * n]), out[3 * n], out[n_arr], out[n_arr + 1]


def _scatter_wait(own, mine, sib, token, send_sems, recv_sems, pieces, stage, name):
    n = len(own)
    per = 4 if stage == 0 else 3
    n_arr = 3 * n + 1

    def body(*refs):
        o, mi, si = refs[:n], refs[n:2 * n], refs[2 * n:3 * n]
        s_sems, r_sems = refs[n_arr], refs[n_arr + 1]
        for p in range(len(pieces)):
            for k in range(per):
                _scatter_copy(o, mi, si, pieces, s_sems, r_sems, p, k, stage, "mine").wait_send()
                _scatter_copy(o, mi, si, pieces, s_sems, r_sems, p, k, stage, "theirs").wait_recv()

    arrays = list(own) + list(mine) + list(sib) + [token]
    out = pl.pallas_call(
        body,
        name=name,
        in_specs=[HBM_SPEC] * n_arr + [SEM_SPEC, SEM_SPEC],
        out_specs=[HBM_SPEC] * n_arr,
        out_shape=[pltpu.HBM(a.shape, a.dtype) for a in arrays],
        input_output_aliases={i: i for i in range(n_arr)},
        compiler_params=pltpu.CompilerParams(has_side_effects=pltpu.SideEffectType.DATAFLOW_SIDE_EFFECTING),
    )(*arrays, send_sems, recv_sems)
    return list(out[:n]), list(out[n:2 * n]), list(out[2 * n:3 * n]), out[3 * n]


def _reduce_adamw(own, mine, sib, w, m, v, qc, name):
    ly, _, r, cols = mine.shape
    tr = _pick(r, (128, 64, 32, 16, 8))
    steps = r // tr
    c1 = 1.0 - ADAM_B1 ** ADAM_STEP
    c2 = 1.0 - ADAM_B2 ** ADAM_STEP

    def body(qc_ref, own_ref, mine_ref, sib_ref, w_ref, m_ref, v_ref, g_ref, d_ref, nm_ref, nv_ref):
        q = qc_ref[0]
        mine_sum = sib_sum = None
        for s in range(N_CHIPS):
            a = jnp.where(q == s, own_ref[...], mine_ref[s]).astype(F32)
            b = sib_ref[s].astype(F32)
            mine_sum = a if s == 0 else mine_sum + a
            sib_sum = b if s == 0 else sib_sum + b
        gv = jnp.where(pl.program_id(1) == qc_ref[1], mine_sum, sib_sum)
        nm = ADAM_B1 * m_ref[...] + (1.0 - ADAM_B1) * gv
        nv = ADAM_B2 * v_ref[...] + (1.0 - ADAM_B2) * (gv * gv)
        g_ref[...] = gv
        d_ref[...] = -ADAM_LR * ((nm / c1) / (jnp.sqrt(nv / c2) + ADAM_EPS) + ADAM_WD * w_ref[...])
        nm_ref[...] = nm
        nv_ref[...] = nv

    def mine_rows(h, i, qc_ref):
        return jnp.where(h == qc_ref[1], i, 0)

    def sib_rows(h, i, qc_ref):
        return jnp.where(h == qc_ref[1], 0, i)

    own_spec = pl.BlockSpec((None, None, tr, cols), lambda l, h, i, qc_ref: (l, qc_ref[0], mine_rows(h, i, qc_ref), 0))
    mine_spec = pl.BlockSpec((None, N_CHIPS, tr, cols), lambda l, h, i, qc_ref: (l, 0, mine_rows(h, i, qc_ref), 0))
    sib_spec = pl.BlockSpec((None, N_CHIPS, tr, cols), lambda l, h, i, qc_ref: (l, 0, sib_rows(h, i, qc_ref), 0))
    spec = pl.BlockSpec((None, tr, cols), lambda l, h, i, qc_ref: (l, h * steps + i, 0))
    shp = jax.ShapeDtypeStruct(w.shape, F32)
    return pl.pallas_call(
        body,
        name=name,
        grid_spec=pltpu.PrefetchScalarGridSpec(
            num_scalar_prefetch=1, grid=(ly, N_CORES, steps),
            in_specs=[own_spec, mine_spec, sib_spec, spec, spec, spec], out_specs=[spec] * 4),
        out_shape=[shp] * 4,
        compiler_params=_params(("parallel", "parallel", "parallel")),
    )(qc, own, mine, sib, w, m, v)


def _allreduce_small(v, name):
    r, w = v.shape

    def body(v_ref, o_ref, buf_ref, send_sems, recv_sems):
        x, y, c = _place()
        me = 4 * x + 2 * y + c

        def peer(k):
            return x ^ (k >> 2), y ^ ((k >> 1) & 1), c ^ (k & 1)

        def remote(k, slot):
            return pltpu.make_async_remote_copy(
                src_ref=v_ref, dst_ref=buf_ref.at[slot], send_sem=send_sems.at[k - 1], recv_sem=recv_sems.at[k - 1],
                device_id=peer(k), device_id_type=MESH)

        sends = [remote(k, me) for k in range(1, N_DEV)]
        for cp in sends:
            cp.start()
        buf_ref[me] = v_ref[...]
        for k in range(1, N_DEV):
            px, py, pc = peer(k)
            remote(k, 4 * px + 2 * py + pc).wait_recv()
        for cp in sends:
            cp.wait_send()
        acc = buf_ref[0]
        for dev in range(1, N_DEV):
            acc = acc + buf_ref[dev]
        o_ref[...] = acc

    vm = pl.BlockSpec(memory_space=pltpu.VMEM)
    return pl.pallas_call(
        body,
        name=name,
        in_specs=[vm],
        out_specs=vm,
        out_shape=jax.ShapeDtypeStruct((r, w), F32),
        scratch_shapes=[pltpu.VMEM((N_DEV, r, w), F32), pltpu.SemaphoreType.DMA((N_DEV - 1,)), pltpu.SemaphoreType.DMA((N_DEV - 1,))],
        compiler_params=pltpu.CompilerParams(vmem_limit_bytes=VMEM_LIMIT_BYTES),
    )(v)


def _adamw(w, g, m, v, name):
    ly, r, c = w.shape
    tr = _pick(r, (256, 128, 64, 32, 16, 8))
    c1 = 1.0 - ADAM_B1 ** ADAM_STEP
    c2 = 1.0 - ADAM_B2 ** ADAM_STEP

    def body(w_ref, g_ref, m_ref, v_ref, d_ref, nm_ref, nv_ref):
        gv = g_ref[...]
        nm = ADAM_B1 * m_ref[...] + (1.0 - ADAM_B1) * gv
        nv = ADAM_B2 * v_ref[...] + (1.0 - ADAM_B2) * (gv * gv)
        d_ref[...] = -ADAM_LR * ((nm / c1) / (jnp.sqrt(nv / c2) + ADAM_EPS) + ADAM_WD * w_ref[...])
        nm_ref[...] = nm
        nv_ref[...] = nv

    spec = pl.BlockSpec((None, tr, c), lambda l, i: (l, i, 0))
    shp = jax.ShapeDtypeStruct((ly, r, c), F32)
    return pl.pallas_call(
        body,
        name=name,
        grid=(ly, r // tr),
        in_specs=[spec] * 4,
        out_specs=[spec] * 3,
        out_shape=[shp] * 3,
        compiler_params=_params(("parallel", "parallel")),
    )(w, g, m, v)


def _rope_tables(seq):
    pos = jnp.arange(seq, dtype=F32)
    inv_freq = 1.0 / (ROPE_THETA ** (jnp.arange(0, HEAD_DIM, 2, dtype=F32) / HEAD_DIM))
    ang = (pos[:, None] * inv_freq[None, :]).T
    cos, sin = jnp.cos(ang), jnp.sin(ang)
    return jnp.concatenate([cos, cos], axis=0), jnp.concatenate([-sin, sin], axis=0)


def _pack(vs, fill=0.0):
    p = jnp.concatenate([v.reshape(-1) for v in vs])
    size = -(-p.shape[0] // 8192) * 8192
    return jnp.pad(p, (0, size - p.shape[0]), constant_values=fill).reshape(-1, 1024)


def _unpack(p, like):
    p = p.reshape(-1)
    out, o = [], 0
    for v in like:
        n = int(math.prod(v.shape))
        out.append(p[o:o + n].reshape(v.shape))
        o += n
    return out


def kernel(x, norm_mix, norm_ffn, norm_final, conv_w_in, conv_w_conv, conv_w_out, attn_w_qkv, attn_b_qkv, attn_sinks, attn_w_o, attn_b_o, ffn_w_in, ffn_w_conv, ffn_w_down, loss_target, m_norm_mix, m_norm_ffn, m_norm_final, m_conv_w_in, m_conv_w_conv, m_conv_w_out, m_attn_w_qkv, m_attn_b_qkv, m_attn_sinks, m_attn_w_o, m_attn_b_o, m_ffn_w_in, m_ffn_w_conv, m_ffn_w_down, v_norm_mix, v_norm_ffn, v_norm_final, v_conv_w_in, v_conv_w_conv, v_conv_w_out, v_attn_w_qkv, v_attn_b_qkv, v_attn_sinks, v_attn_w_o, v_attn_b_o, v_ffn_w_in, v_ffn_w_conv, v_ffn_w_down):
    bsz, seq, d = x.shape
    t = bsz * seq
    depth = norm_mix.shape[0]
    n_conv, n_attn = conv_w_in.shape[0], attn_w_qkv.shape[0]
    xq, yq, cq = _place()
    q = 2 * xq + yq

    big = [conv_w_in, conv_w_out, attn_w_qkv, attn_w_o, ffn_w_in, ffn_w_down]
    axes = [2, 1, 2, 1, 2, 1]
    q_arr = q.astype(jnp.int32).reshape(1)
    c_arr = cq.astype(jnp.int32).reshape(1)
    weights = [None] * 6
    sems = {}

    def place(n):
        weights[n] = _place_shard(big[n], axes[n], q_arr, f"place_shard{n}")

    def pieces_of(i):
        return [(0, i // 2), (1, i // 2), (4, i), (5, i)] if i % 2 == 0 else [(2, i // 2), (3, i // 2), (4, i), (5, i)]

    def subset(pieces):
        ts = sorted({ti for ti, _ in pieces})
        return ts, [(ts.index(ti), l) for ti, l in pieces]

    def fetch(pieces, stage, tag):
        ts, local = subset(pieces)
        out, sems[tag, 0], sems[tag, 1] = _gather_start([weights[ti] for ti in ts], [axes[ti] for ti in ts], local, stage,
                                                        f"gather_{'ici' if stage == 0 else 'pass'}_start{tag}")
        for n, ti in enumerate(ts):
            weights[ti] = out[n]

    def settle(pieces, stage, tag, after):
        ts, local = subset(pieces)
        out = _gather_wait([weights[ti] for ti in ts], sems[tag, 0], sems[tag, 1], after, [axes[ti] for ti in ts], local, stage,
                           f"gather_{'ici' if stage == 0 else 'pass'}_wait{tag}")
        for n, ti in enumerate(ts):
            weights[ti] = out[n]

    xs = x.reshape(t, d)
    place(0)
    place(1)
    fetch(pieces_of(0)[:2], 0, "m0")
    for n in (2, 3, 4, 5):
        place(n)
    h = _rms_fwd(xs, norm_mix[0:1], "norm_mix_fwd0")
    settle(pieces_of(0)[:2], 0, "m0", h)
    fetch(pieces_of(0)[:2], 1, "m0")

    small_cols = [conv_w_conv, attn_b_qkv, attn_b_o, ffn_w_conv]

    def placed(v):
        width = v.shape[-1]
        full = jnp.zeros(v.shape[:-1] + (N_CHIPS * width,), F32)
        return lax.dynamic_update_slice_in_dim(full, v * (1.0 / N_CORES), q * width, axis=v.ndim - 1)

    full_cols = [placed(v) for v in small_cols]
    small_full = _allreduce_small(_pack(full_cols), "gather_small")
    wc_conv, b_qkv, b_o, wf_conv = _unpack(small_full, full_cols)
    settle(pieces_of(0)[:2], 1, "m0", small_full)
    cos_t, sin_t = _rope_tables(seq)

    saved = []
    for i in range(depth):
        j = i // 2
        ahead = pieces_of(i + 1) if i + 1 < depth else None
        if i == 0:
            fetch(pieces_of(0)[2:], 0, "0")
        elif ahead:
            fetch(ahead, 0, str(i + 1))
        w_cin, w_cout, w_qkv, w_o, w_fin, w_fdown = weights
        g_ffn = norm_ffn[i:i + 1]
        if i % 2 == 0:
            pre = _mm(h, w_cin, "nn", BF16, layer=j, tm=1024, tn=768, tk=4096, name=f"conv_in_fwd{i}")
            mixed = _convgate_fwd(pre, wc_conv[j], seq, f"conv_gate_fwd{i}")
            if i == 0:
                settle(pieces_of(0)[2:], 0, "0", mixed)
                fetch(pieces_of(0)[2:], 1, "0")
                w_cin, w_cout, w_qkv, w_o, w_fin, w_fdown = weights
            x_mid, h2 = _mm(mixed, w_cout, "nn", F32, layer=j, residual=xs, norm_g=g_ffn, tm=512, tn=1024, tk=4096,
                            name=f"conv_out_fwd{i}")
            lse = None
        else:
            pre = _mm(h, w_qkv, "nn", F32, layer=j, bias=b_qkv[j:j + 1], tm=1024, tn=768, tk=4096, name=f"qkv_fwd{i}")
            mixed, lse = _attn_fwd(pre, attn_sinks[j], cos_t, sin_t, bsz, seq, f"attn_fwd{i}", hp=1)
            x_mid, h2 = _mm(mixed, w_o, "nn", F32, layer=j, bias=b_o[j:j + 1], residual=xs, norm_g=g_ffn, tm=512, tn=1024,
                            tk=4096, name=f"attn_out_fwd{i}")
        if i == 0:
            settle(pieces_of(0)[2:], 1, "0", h2)
            fetch(ahead, 0, "1")
        elif ahead:
            settle(ahead, 0, str(i + 1), x_mid)
            fetch(ahead, 1, str(i + 1))
        w_cin, w_cout, w_qkv, w_o, w_fin, w_fdown = weights
        gu = _mm(h2, w_fin, "nn", BF16, layer=i, n_outer=True, tm=1024, tn=2816, tk=4096, name=f"ffn_in_fwd{i}")
        if i == 0:
            settle(ahead, 0, "1", gu)
            fetch(ahead, 1, "1")
            w_cin, w_cout, w_qkv, w_o, w_fin, w_fdown = weights
        g_next = norm_mix[i + 1:i + 2] if i + 1 < depth else norm_final.reshape(1, d)
        x_next, act, h_next = _ffn_gate_down_fwd(gu, wf_conv[i], w_fdown, i, x_mid, g_next, seq, f"ffn_gate_down_fwd{i}")
        if ahead:
            settle(ahead, 1, str(i + 1), x_next)
        saved.append((xs, h, pre, mixed, lse, x_mid, h2, gu, act))
        xs, h = x_next, h_next
    w_cin, w_cout, w_qkv, w_o, w_fin, w_fdown = weights

    dx, dxb, sq, dg_final = _loss_head(xs, loss_target.reshape(t, d), norm_final.reshape(1, d), "loss_head")
    loss = lax.psum(0.5 * jnp.sum(sq) / d, ("x", "y", "c"))

    g_norm_mix, g_norm_ffn = [None] * depth, [None] * depth
    g_cin, g_cconv, g_cout = [None] * n_conv, [None] * n_conv, [None] * n_conv
    g_qkv, g_bqkv, g_sinks, g_o, g_bo = ([None] * n_attn for _ in range(5))
    g_fin, g_fconv, g_fdown = [None] * depth, [None] * depth, [None] * depth

    kinds6 = ["col", "row", "col", "row", "col", "row"]
    layers6 = [n_conv, n_conv, n_attn, n_attn, depth, depth]
    big_w = [conv_w_in, conv_w_out, attn_w_qkv, attn_w_o, ffn_w_in, ffn_w_down]

    def slot_stack(n):
        k, cols = big_w[n].shape[1], big_w[n].shape[2]
        r = k // 2
        return lax.empty((layers6[n], N_CHIPS, r, cols), BF16)

    own = [slot_stack(n) for n in range(6)]
    mine = [slot_stack(n) for n in range(6)]
    sib = [slot_stack(n) for n in range(6)]
    flight = {}

    def group(i, part):
        return f"{part}{i}", (pieces_of(i)[:2] if part == "m" else pieces_of(i)[2:])

    def scatter(grp, stage, action, token):
        tag, pieces = grp
        ts = [ti for ti, _ in pieces]
        local = [(n, l) for n, (_, l) in enumerate(pieces)]
        sub = ([own[ti] for ti in ts], [mine[ti] for ti in ts], [sib[ti] for ti in ts], token)
        label = f"grad_{'ici' if stage == 0 else 'pass'}_{action}_{tag}"
        if action == "start":
            o, mi, si, token, s_sems, r_sems = _scatter_start(*sub, local, stage, label)
            flight[tag] = (s_sems, r_sems)
        else:
            o, mi, si, token = _scatter_wait(*sub, *flight[tag], local, stage, label)
        for n, ti in enumerate(ts):
            own[ti], mine[ti], sib[ti] = o[n], mi[n], si[n]
        return token

    def pair_begin(grp, token):
        tag, pieces = grp
        grads = {0: g_cin, 1: g_cout, 2: g_qkv, 3: g_o, 4: g_fin, 5: g_fdown}
        parts, kinds = [], []
        for ti, l in pieces:
            g = grads[ti][l]
            parts.append(g if kinds6[ti] == "col" else g.reshape(N_CHIPS, g.shape[0] // N_CHIPS, g.shape[1]))
            kinds.append(kinds6[ti])
        parts, token, lands, s_sems, r_sems = _pair_start(parts, kinds, token, f"grad_pair_start_{tag}")
        flight["pair" + tag] = (parts, kinds, lands, s_sems, r_sems)
        return token

    def pair_finish(grp, token):
        tag, pieces = grp
        parts, kinds, lands, s_sems, r_sems = flight["pair" + tag]
        parts, token, recv = _pair_wait(parts, kinds, token, lands, s_sems, r_sems, f"grad_pair_wait_{tag}")
        for (ti, l), g, r in zip(pieces, parts, recv):
            own[ti] = _pair_sum(g, r, kinds6[ti], c_arr, own[ti], l, f"grad_pair_sum_{tag}_{ti}")
        return scatter(grp, 0, "start", token)

    for i in reversed(range(depth)):
        j = i // 2
        x_in, h, pre, mixed, lse, x_mid, h2, gu, act = saved[i]
        da = _mm(dxb, w_fdown, "nt", BF16, layer=i, n_outer=True, tm=1024, tn=2816, tk=4096, name=f"ffn_down_dx{i}")
        g_fdown[i] = _mm(act, dxb, "tn", BF16, tm=1408, tn=1024, tk=2048, name=f"ffn_down_dw{i}")
        dgu, dwc = _ffngate_bwd(gu, da, wf_conv[i], seq, f"ffn_gate_bwd{i}")
        g_fconv[i] = dwc[:3]
        g_fin[i] = _mm(h2, dgu, "tn", BF16, tm=1024, tn=1408, tk=2048, name=f"ffn_in_dw{i}")
        dgu = pair_begin(group(i, "f"), dgu)
        dx, dxb, dg, colsum = _rms_bwd(x_mid, dgu, w_fin, i, norm_ffn[i:i + 1], dx, f"ffn_in_dx_norm_bwd{i}")
        g_norm_ffn[i] = jnp.sum(dg, axis=0)
        if i + 1 < depth:
            dxb = scatter(group(i + 1, "m"), 1, "start", scatter(group(i + 1, "m"), 0, "wait", dxb))
            dxb = scatter(group(i + 1, "f"), 1, "wait", dxb)
        dxb = pair_finish(group(i, "f"), dxb)
        if i % 2 == 0:
            dmix = _mm(dxb, w_cout, "nt", BF16, layer=j, tm=512, tn=1024, tk=4096, name=f"conv_out_dx{i}")
            g_cout[j] = _mm(mixed, dxb, "tn", BF16, tm=1024, tn=1024, tk=2048, name=f"conv_out_dw{i}")
            dpre, dwc = _convgate_bwd(pre, dmix, wc_conv[j], seq, f"conv_gate_bwd{i}")
            g_cconv[j] = dwc[:3]
            g_cin[j] = _mm(h, dpre, "tn", BF16, tm=1024, tn=1536, tk=2048, name=f"conv_in_dw{i}")
            w_pre = w_cin
        else:
            g_bo[j] = jnp.sum(colsum, axis=0)
            dmix = _mm(dxb, w_o, "nt", F32, layer=j, tm=512, tn=1024, tk=4096, name=f"attn_out_dx{i}")
            g_o[j] = _mm(mixed, dxb, "tn", BF16, tm=1024, tn=1024, tk=2048, name=f"attn_out_dw{i}")
            dpre, dbias, dsk = _attn_bwd(pre, mixed, lse, dmix, attn_sinks[j], cos_t, sin_t, bsz, seq, f"attn_bwd{i}",
                                         hp=GROUP)
            g_bqkv[j] = jnp.sum(dbias, axis=0)
            g_sinks[j] = jnp.sum(dsk, axis=1)
            g_qkv[j] = _mm(h, dpre, "tn", BF16, tm=1024, tn=1536, tk=2048, name=f"qkv_dw{i}")
            w_pre = w_qkv
        dpre = pair_begin(group(i, "m"), dpre)
        dx, dxb, dg, _ = _rms_bwd(x_in, dpre, w_pre, j, norm_mix[i:i + 1], dx, f"mixer_in_dx_norm_bwd{i}")
        g_norm_mix[i] = jnp.sum(dg, axis=0)
        dxb = scatter(group(i, "f"), 1, "start", scatter(group(i, "f"), 0, "wait", dxb))
        if i + 1 < depth:
            dxb = scatter(group(i + 1, "m"), 1, "wait", dxb)
        dxb = pair_finish(group(i, "m"), dxb)
    grad_x = dx.reshape(bsz, seq, d)

    scatter(group(0, "f"), 1, "wait", dxb)
    big_m = [m_conv_w_in, m_conv_w_out, m_attn_w_qkv, m_attn_w_o, m_ffn_w_in, m_ffn_w_down]
    big_v = [v_conv_w_in, v_conv_w_out, v_attn_w_qkv, v_attn_w_o, v_ffn_w_in, v_ffn_w_down]
    big_names = ["conv_w_in", "conv_w_out", "attn_w_qkv", "attn_w_o", "ffn_w_in", "ffn_w_down"]
    qc_arr = jnp.stack([q, cq]).astype(jnp.int32)

    def adamw_of(n):
        return list(_reduce_adamw(own[n], mine[n], sib[n], big_w[n], big_m[n], big_v[n], qc_arr, f"adamw_{big_names[n]}"))

    big_upd = [None] * 6
    for n in (2, 3, 4, 5):
        big_upd[n] = adamw_of(n)
    token = scatter(group(0, "m"), 1, "start", scatter(group(0, "m"), 0, "wait", big_upd[5][1]))
    big_upd[5][1] = scatter(group(0, "m"), 1, "wait", token)
    for n in (0, 1):
        big_upd[n] = adamw_of(n)

    small = [jnp.stack(g_norm_mix), jnp.stack(g_norm_ffn), jnp.sum(dg_final, axis=0), jnp.stack(g_cconv),
             jnp.stack(g_bqkv), jnp.stack(g_sinks), jnp.stack(g_bo), jnp.stack(g_fconv)]
    sg = _unpack(_allreduce_small(_pack(small), "grad_small_allreduce"), small)

    def my_cols(v, like):
        width = like.shape[-1]
        return lax.dynamic_slice_in_dim(v, q * width, width, axis=v.ndim - 1)

    small_w = [norm_mix, norm_ffn, norm_final, conv_w_conv, attn_b_qkv, attn_sinks, attn_b_o, ffn_w_conv]
    small_m = [m_norm_mix, m_norm_ffn, m_norm_final, m_conv_w_conv, m_attn_b_qkv, m_attn_sinks, m_attn_b_o, m_ffn_w_conv]
    small_v = [v_norm_mix, v_norm_ffn, v_norm_final, v_conv_w_conv, v_attn_b_qkv, v_attn_sinks, v_attn_b_o, v_ffn_w_conv]
    small_g = [sg[0], sg[1], sg[2], my_cols(sg[3], conv_w_conv), my_cols(sg[4], attn_b_qkv), sg[5],
               my_cols(sg[6], attn_b_o), my_cols(sg[7], ffn_w_conv)]

    upd = {nm: tuple(u[1:]) for nm, u in zip(big_names, big_upd)}
    sd, sm, sv = _adamw(_pack(small_w)[None], _pack(small_g)[None], _pack(small_m)[None], _pack(small_v, 1.0)[None],
                        "adamw_small")
    sd, sm, sv = _unpack(sd, small_w), _unpack(sm, small_w), _unpack(sv, small_w)
    names = ["norm_mix", "norm_ffn", "norm_final", "conv_w_in", "conv_w_conv", "conv_w_out", "attn_w_qkv", "attn_b_qkv",
             "attn_sinks", "attn_w_o", "attn_b_o", "ffn_w_in", "ffn_w_conv", "ffn_w_down"]
    small_names = ["norm_mix", "norm_ffn", "norm_final", "conv_w_conv", "attn_b_qkv", "attn_sinks", "attn_b_o", "ffn_w_conv"]
    grads = dict(zip(small_names, small_g))
    grads.update({nm: u[0] for nm, u in zip(big_names, big_upd)})
    for n, nm in enumerate(small_names):
        upd[nm] = (sd[n], sm[n], sv[n])
    return (loss, grad_x, *[grads[nm] for nm in names], *[upd[nm][0] for nm in names],
            *[upd[nm][1] for nm in names], *[upd[nm][2] for nm in names])
```

```python
import math

import jax
import jax.numpy as jnp
from jax import lax
from jax.experimental import pallas as pl
from jax.experimental.pallas import tpu as pltpu

F32 = jnp.float32
BF16 = jnp.bfloat16

HEAD_DIM = 64
GROUP = 4
WINDOW = 128
EPS = 1e-5
ROPE_THETA = 10000.0
ADAM_LR, ADAM_B1, ADAM_B2, ADAM_EPS, ADAM_WD, ADAM_STEP = 0.001, 0.9, 0.999, 1e-08, 0.01, 10

N_CHIPS = 4
N_CORES = 2
N_DEV = 8
HALO = 16
VMEM_LIMIT_BYTES = 56 * 1024 * 1024
MESH = pl.DeviceIdType.MESH
ANY = pl.BlockSpec(memory_space=pl.ANY)
SMEM = pl.BlockSpec(memory_space=pltpu.SMEM)
NEG = float(jnp.finfo(jnp.float32).min)
ROW_TILES = (512, 256, 128, 64, 32, 16, 8)


def _pick(dim, cands):
    for c in cands:
        if dim % c == 0:
            return c
    return dim


def _params(sem):
    return pltpu.CompilerParams(dimension_semantics=sem, vmem_limit_bytes=VMEM_LIMIT_BYTES)


_DIMS = {"nn": (((1,), (0,)), ((), ())), "nt": (((1,), (1,)), ((), ())), "tn": (((0,), (0,)), ((), ()))}


def _mm(a, b, mode, out_dtype, *, layer=None, bias=None, residual=None, norm_g=None, n_outer=False, tm, tn, tk, name):
    b2 = b.shape[1:] if layer is not None else b.shape
    if mode == "nn":
        (m, k), n = a.shape, b2[1]
    elif mode == "nt":
        (m, k), n = a.shape, b2[0]
    else:
        (k, m), n = a.shape, b2[1]
    tm, tn, tk = min(tm, m), min(tn, n), min(tk, k)
    assert m % tm == 0 and n % tn == 0 and k % tk == 0, (name, a.shape, b.shape, tm, tn, tk)
    nk = k // tk

    def at(f):
        return (lambda p0, p1, p2: f(p1, p0, p2)) if n_outer else f

    a_spec = pl.BlockSpec((tk, tm), at(lambda i, j, l: (l, i))) if mode == "tn" else pl.BlockSpec((tm, tk), at(lambda i, j, l: (i, l)))
    if layer is None:
        b_spec = (pl.BlockSpec((tn, tk), at(lambda i, j, l: (j, l))) if mode == "nt"
                  else pl.BlockSpec((tk, tn), at(lambda i, j, l: (l, j))))
    elif mode == "nt":
        b_spec = pl.BlockSpec((None, tn, tk), at(lambda i, j, l: (layer, j, l)))
    else:
        b_spec = pl.BlockSpec((None, tk, tn), at(lambda i, j, l: (layer, l, j)))
    in_specs, args = [a_spec, b_spec], [a, b]
    if bias is not None:
        in_specs.append(pl.BlockSpec((1, tn), at(lambda i, j, l: (0, j))))
        args.append(bias)
    if residual is not None:
        in_specs.append(pl.BlockSpec((tm, tn), at(lambda i, j, l: (i, j))))
        args.append(residual)
    if norm_g is not None:
        assert tn == n, (name, "the RMSNorm of the result needs whole rows in a tile")
        in_specs.append(pl.BlockSpec((1, tn), at(lambda i, j, l: (0, j))))
        args.append(norm_g)
    has_bias, has_res, has_norm = bias is not None, residual is not None, norm_g is not None

    def body(*refs):
        a_ref, b_ref = refs[0], refs[1]
        pos = 2
        bias_ref = res_ref = g_ref = h_ref = None
        if has_bias:
            bias_ref, pos = refs[pos], pos + 1
        if has_res:
            res_ref, pos = refs[pos], pos + 1
        if has_norm:
            g_ref, pos = refs[pos], pos + 1
        o_ref, pos = refs[pos], pos + 1
        if has_norm:
            h_ref, pos = refs[pos], pos + 1
        acc_ref = refs[pos] if nk > 1 else None

        def finish(acc):
            if has_bias:
                acc = acc + bias_ref[...]
            if has_res:
                acc = acc + res_ref[...]
            o_ref[...] = acc.astype(o_ref.dtype)
            if has_norm:
                h_ref[...] = _rms(acc, g_ref[...]).astype(BF16)

        if nk == 1:
            finish(lax.dot_general(a_ref[...], b_ref[...], _DIMS[mode], preferred_element_type=F32))
            return
        l = pl.program_id(2)
        part = lax.dot_general(a_ref[...], b_ref[...], _DIMS[mode], preferred_element_type=F32)

        @pl.when(l == 0)
        def _():
            acc_ref[...] = part

        @pl.when(l > 0)
        def _():
            acc_ref[...] += part

        @pl.when(l == nk - 1)
        def _():
            finish(acc_ref[...])

    o_spec = pl.BlockSpec((tm, tn), at(lambda i, j, l: (i, j)))
    o_shape = jax.ShapeDtypeStruct((m, n), out_dtype)
    return pl.pallas_call(
        body,
        name=name,
        grid=(n // tn, m // tm, nk) if n_outer else (m // tm, n // tn, nk),
        in_specs=in_specs,
        out_specs=[o_spec, o_spec] if has_norm else o_spec,
        out_shape=[o_shape, jax.ShapeDtypeStruct((m, n), BF16)] if has_norm else o_shape,
        scratch_shapes=[pltpu.VMEM((tm, tn), F32)] if nk > 1 else [],
        compiler_params=_params(("parallel", "parallel", "arbitrary")),
    )(*args)


def _rms(x, g):
    return x * lax.rsqrt(jnp.mean(x * x, axis=-1, keepdims=True) + EPS) * g


def _fold8(v):
    r, d = v.shape
    return jnp.sum(v.reshape(r // 8, 8, d), axis=0)


def _rms_fwd(x, g, name):
    t, d = x.shape
    tm = _pick(t, ROW_TILES)

    def body(x_ref, g_ref, h_ref):
        xv = x_ref[...]
        r = lax.rsqrt(jnp.mean(xv * xv, axis=-1, keepdims=True) + EPS)
        h_ref[...] = (xv * r * g_ref[...]).astype(BF16)

    return pl.pallas_call(
        body,
        name=name,
        grid=(t // tm,),
        in_specs=[pl.BlockSpec((tm, d), lambda i: (i, 0)), pl.BlockSpec((1, d), lambda i: (0, 0))],
        out_specs=pl.BlockSpec((tm, d), lambda i: (i, 0)),
        out_shape=jax.ShapeDtypeStruct((t, d), BF16),
        compiler_params=_params(("parallel",)),
    )(x, g)


def _rms_bwd(x, dpre, w, layer, g, dx_in, name):
    t, d = x.shape
    k = dpre.shape[1]
    tm = _pick(t, ROW_TILES)

    sub = _pick(tm, (256, 128, 64, 32, 16, 8))

    def body(x_ref, dp_ref, w_ref, g_ref, dxi_ref, dx_ref, dxb_ref, dg_ref, cs_ref):
        i = pl.program_id(0)

        @pl.when(i == 0)
        def _():
            dg_ref[...] = jnp.zeros_like(dg_ref)
            cs_ref[...] = jnp.zeros_like(cs_ref)

        dg = jnp.zeros((8, d), F32)
        cs = jnp.zeros((8, d), F32)
        for r0 in range(0, tm, sub):
            rows = slice(r0, r0 + sub)
            xv = x_ref[rows, :]
            r = lax.rsqrt(jnp.mean(xv * xv, axis=-1, keepdims=True) + EPS)
            xhat = xv * r
            dy = lax.dot_general(dp_ref[rows, :], w_ref[...], _DIMS["nt"], preferred_element_type=F32)
            gdy = dy * g_ref[...]
            dx = dxi_ref[rows, :] + r * (gdy - xhat * jnp.mean(gdy * xhat, axis=-1, keepdims=True))
            dx_ref[rows, :] = dx
            dxb_ref[rows, :] = dx.astype(BF16)
            dg += _fold8(dy * xhat)
            cs += _fold8(dx)
        dg_ref[...] += dg
        cs_ref[...] += cs

    row = pl.BlockSpec((tm, d), lambda i: (i, 0))
    acc = pl.BlockSpec((8, d), lambda i: (0, 0))
    w_spec = pl.BlockSpec((None, d, k), lambda i: (layer, 0, 0), pipeline_mode=pl.Buffered(1))
    return pl.pallas_call(
        body,
        name=name,
        grid=(t // tm,),
        in_specs=[row, pl.BlockSpec((tm, k), lambda i: (i, 0)), w_spec, pl.BlockSpec((1, d), lambda i: (0, 0)), row],
        out_specs=[row, row, acc, acc],
        out_shape=[jax.ShapeDtypeStruct((t, d), F32), jax.ShapeDtypeStruct((t, d), BF16),
                   jax.ShapeDtypeStruct((8, d), F32), jax.ShapeDtypeStruct((8, d), F32)],
        compiler_params=_params(("arbitrary",)),
    )(x, dpre, w, g, dx_in)


def _loss_head(x, target, g, name):
    t, d = x.shape
    tm = _pick(t, ROW_TILES)
    inv_d = 1.0 / d

    def body(x_ref, t_ref, g_ref, dx_ref, dxb_ref, sq_ref, dg_ref):
        i = pl.program_id(0)
        xv = x_ref[...]
        gv = g_ref[...]
        r = lax.rsqrt(jnp.mean(xv * xv, axis=-1, keepdims=True) + EPS)
        xhat = xv * r
        err = xhat * gv - t_ref[...]
        dy = err * inv_d
        gdy = dy * gv
        dx = r * (gdy - xhat * jnp.mean(gdy * xhat, axis=-1, keepdims=True))
        dx_ref[...] = dx
        dxb_ref[...] = dx.astype(BF16)

        @pl.when(i == 0)
        def _():
            sq_ref[...] = jnp.zeros_like(sq_ref)
            dg_ref[...] = jnp.zeros_like(dg_ref)

        sq_ref[...] += _fold8(err * err)
        dg_ref[...] += _fold8(dy * xhat)

    row = pl.BlockSpec((tm, d), lambda i: (i, 0))
    acc = pl.BlockSpec((8, d), lambda i: (0, 0))
    return pl.pallas_call(
        body,
        name=name,
        grid=(t // tm,),
        in_specs=[row, row, pl.BlockSpec((1, d), lambda i: (0, 0))],
        out_specs=[row, row, acc, acc],
        out_shape=[jax.ShapeDtypeStruct((t, d), F32), jax.ShapeDtypeStruct((t, d), BF16),
                   jax.ShapeDtypeStruct((8, d), F32), jax.ShapeDtypeStruct((8, d), F32)],
        compiler_params=_params(("arbitrary",)),
    )(x, target, g)


def _rows(tm):
    return lax.broadcasted_iota(jnp.int32, (tm, 1), 0)


def _shift_down(u, before2):
    r8 = _rows(8)
    s1, s2 = pltpu.roll(u, 1, 0), pltpu.roll(u, 2, 0)
    top1 = jnp.where(r8 == 0, before2[1:2], s1[:8])
    top2 = jnp.where(r8 == 0, before2[0:1], jnp.where(r8 == 1, before2[1:2], s2[:8]))
    return jnp.concatenate([top1, s1[8:]], axis=0), jnp.concatenate([top2, s2[8:]], axis=0)


def _shift_up(u, after2):
    tm = u.shape[0]
    r8 = _rows(8)
    s1, s2 = pltpu.roll(u, tm - 1, 0), pltpu.roll(u, tm - 2, 0)
    bot1 = jnp.where(r8 == 7, after2[0:1], s1[tm - 8:])
    bot2 = jnp.where(r8 == 6, after2[0:1], jnp.where(r8 == 7, after2[1:2], s2[tm - 8:]))
    return jnp.concatenate([s1[:tm - 8], bot1], axis=0), jnp.concatenate([s2[:tm - 8], bot2], axis=0)


def _shift_matrix(tm, up):
    r = lax.broadcasted_iota(jnp.int32, (2 * tm, tm), 0)
    c = lax.broadcasted_iota(jnp.int32, (2 * tm, tm), 1)
    t = jnp.where(r >= tm, r - tm, r)
    k = jnp.where(r >= tm, 2, 1)
    return (c == (t + k if up else t - k)).astype(BF16)


def _shift_down_mxu(u, before2):
    tm = u.shape[0]
    moved = jnp.dot(_shift_matrix(tm, False), u.astype(BF16), preferred_element_type=F32)
    r8 = _rows(8)
    s1, s2 = moved[:tm], moved[tm:]
    top1 = s1[:8] + jnp.where(r8 == 0, before2[1:2], 0.0)
    top2 = s2[:8] + jnp.where(r8 == 0, before2[0:1], jnp.where(r8 == 1, before2[1:2], 0.0))
    return jnp.concatenate([top1, s1[8:]], axis=0), jnp.concatenate([top2, s2[8:]], axis=0)


def _shift_up_mxu(u, after2):
    tm = u.shape[0]
    moved = jnp.dot(_shift_matrix(tm, True), u.astype(BF16), preferred_element_type=F32)
    r8 = _rows(8)
    s1, s2 = moved[:tm], moved[tm:]
    bot1 = s1[tm - 8:] + jnp.where(r8 == 7, after2[0:1], 0.0)
    bot2 = s2[tm - 8:] + jnp.where(r8 == 6, after2[0:1], jnp.where(r8 == 7, after2[1:2], 0.0))
    return jnp.concatenate([s1[:tm - 8], bot1], axis=0), jnp.concatenate([s2[:tm - 8], bot2], axis=0)


def _conv_tile(seq):
    return _pick(seq, (256, 128, 64, 32, 16, 8))


def _halo_specs(tm, width, n_tiles):
    per = tm // HALO
    before = pl.BlockSpec((HALO, width), lambda i: (jnp.maximum(i * per - 1, 0), 0))
    after = pl.BlockSpec((HALO, width), lambda i: (jnp.minimum((i + 1) * per, n_tiles * per - 1), 0))
    return before, after


def _convgate_fwd(bcv, w, seq, name):
    t, d3 = bcv.shape
    d = d3 // 3
    tm = _conv_tile(seq)
    tps = seq // tm
    before, _ = _halo_specs(tm, d3, t // tm)

    def body(x_ref, xb_ref, w_ref, y_ref):
        i = pl.program_id(0)
        inner = (i % tps != 0).astype(F32)
        u = x_ref[:, d:2 * d].astype(F32) * x_ref[:, 2 * d:].astype(F32)
        xb = xb_ref[:, d:].astype(F32)[HALO - 2:]
        s1, s2 = _shift_down(u, xb[:, :d] * xb[:, d:] * inner)
        z = w_ref[2:3] * u + w_ref[1:2] * s1 + w_ref[0:1] * s2
        y_ref[...] = (x_ref[:, :d].astype(F32) * z).astype(BF16)

    return pl.pallas_call(
        body,
        name=name,
        grid=(t // tm,),
        in_specs=[pl.BlockSpec((tm, d3), lambda i: (i, 0)), before, pl.BlockSpec((3, d), lambda i: (0, 0))],
        out_specs=pl.BlockSpec((tm, d), lambda i: (i, 0)),
        out_shape=jax.ShapeDtypeStruct((t, d), BF16),
        compiler_params=_params(("parallel",)),
    )(bcv, bcv, w)


def _convgate_bwd(bcv, dy, w, seq, name):
    t, d3 = bcv.shape
    d = d3 // 3
    tm = _conv_tile(seq)
    tps = seq // tm
    before, after = _halo_specs(tm, d3, t // tm)
    _, after_dy = _halo_specs(tm, d, t // tm)

    def body(x_ref, xb_ref, xa_ref, dy_ref, dya_ref, w_ref, dx_ref, dw_ref):
        i = pl.program_id(0)
        inner_lo = (i % tps != 0).astype(F32)
        inner_hi = (i % tps != tps - 1).astype(F32)
        w0, w1, w2 = w_ref[0:1], w_ref[1:2], w_ref[2:3]
        b, c, v = x_ref[:, :d].astype(F32), x_ref[:, d:2 * d].astype(F32), x_ref[:, 2 * d:].astype(F32)
        u = c * v
        xb = xb_ref[:, d:].astype(F32)[HALO - 2:]
        s1, s2 = _shift_down(u, xb[:, :d] * xb[:, d:] * inner_lo)
        z = w2 * u + w1 * s1 + w0 * s2
        dyv = dy_ref[...].astype(F32)
        dz = dyv * b
        dza = dya_ref[...].astype(F32)[0:2] * xa_ref[:, :d].astype(F32)[0:2] * inner_hi
        n1, n2 = _shift_up(dz, dza)
        du = w2 * dz + w1 * n1 + w0 * n2
        dx_ref[:, :d] = (dyv * z).astype(BF16)
        dx_ref[:, d:2 * d] = (du * v).astype(BF16)
        dx_ref[:, 2 * d:] = (du * c).astype(BF16)

        @pl.when(i == 0)
        def _():
            dw_ref[...] = jnp.zeros_like(dw_ref)

        dw_ref[0:1] += jnp.sum(dz * s2, axis=0, keepdims=True)
        dw_ref[1:2] += jnp.sum(dz * s1, axis=0, keepdims=True)
        dw_ref[2:3] += jnp.sum(dz * u, axis=0, keepdims=True)

    return pl.pallas_call(
        body,
        name=name,
        grid=(t // tm,),
        in_specs=[pl.BlockSpec((tm, d3), lambda i: (i, 0)), before, after,
                  pl.BlockSpec((tm, d), lambda i: (i, 0)), after_dy, pl.BlockSpec((3, d), lambda i: (0, 0))],
        out_specs=[pl.BlockSpec((tm, d3), lambda i: (i, 0)), pl.BlockSpec((8, d), lambda i: (0, 0))],
        out_shape=[jax.ShapeDtypeStruct((t, d3), BF16), jax.ShapeDtypeStruct((8, d), F32)],
        compiler_params=_params(("arbitrary",)),
    )(bcv, bcv, bcv, dy, dy, w)


def _sigmoid(x):
    return 1.0 / (1.0 + jnp.exp(-x))


def _ffn_gate_down_fwd(gu, w, w_down, layer, resid, norm_g, seq, name):
    t, f2 = gu.shape
    f = f2 // 2
    d = w_down.shape[2]
    sub = _conv_tile(seq)
    tm = _pick(seq, (2 * sub, sub))
    tps = seq // tm
    before, _ = _halo_specs(tm, f2, t // tm)

    def body(x_ref, xb_ref, w_ref, wd_ref, res_ref, g_ref, o_ref, a_ref, h_ref):
        i = pl.program_id(0)
        inner = (i % tps != 0).astype(F32)
        for r0 in range(0, tm, sub):
            rows = slice(r0, r0 + sub)
            if r0 == 0:
                halo = xb_ref[:, :f].astype(F32)[HALO - 2:] * inner
            else:
                halo = x_ref[r0 - HALO:r0, :f].astype(F32)[HALO - 2:]
            s1, s2 = _shift_down_mxu(x_ref[rows, :f], halo)
            gc = w_ref[2:3] * x_ref[rows, :f].astype(F32) + w_ref[1:2] * s1 + w_ref[0:1] * s2
            act = (gc * _sigmoid(gc) * x_ref[rows, f:].astype(F32)).astype(BF16)
            a_ref[rows, :] = act
            out = jnp.dot(act, wd_ref[...], preferred_element_type=F32) + res_ref[rows, :]
            o_ref[rows, :] = out
            h_ref[rows, :] = _rms(out, g_ref[...]).astype(BF16)

    row_d = pl.BlockSpec((tm, d), lambda i: (i, 0))
    return pl.pallas_call(
        body,
        name=name,
        grid=(t // tm,),
        in_specs=[pl.BlockSpec((tm, f2), lambda i: (i, 0)), before, pl.BlockSpec((3, f), lambda i: (0, 0)),
                  pl.BlockSpec((None, f, d), lambda i: (layer, 0, 0), pipeline_mode=pl.Buffered(1)),
                  row_d, pl.BlockSpec((1, d), lambda i: (0, 0))],
        out_specs=[row_d, pl.BlockSpec((tm, f), lambda i: (i, 0)), row_d],
        out_shape=[jax.ShapeDtypeStruct((t, d), F32), jax.ShapeDtypeStruct((t, f), BF16), jax.ShapeDtypeStruct((t, d), BF16)],
        compiler_params=_params(("parallel",)),
    )(gu, gu, w, w_down, resid, norm_g)


def _ffngate_bwd(gu, da, w, seq, name):
    t, f2 = gu.shape
    f = f2 // 2
    tm = _conv_tile(seq)
    tps = seq // tm
    before, after = _halo_specs(tm, f2, t // tm)
    _, after_da = _halo_specs(tm, f, t // tm)

    def body(x_ref, xb_ref, xa_ref, da_ref, daa_ref, w_ref, dx_ref, dw_ref):
        i = pl.program_id(0)
        inner_lo = (i % tps != 0).astype(F32)
        inner_hi = (i % tps != tps - 1).astype(F32)
        w0, w1, w2 = w_ref[0:1], w_ref[1:2], w_ref[2:3]

        def dgate(gc, uv, dav):
            sg = _sigmoid(gc)
            return dav * uv * (sg * (1.0 + gc * (1.0 - sg))), dav * (gc * sg)

        g, u = x_ref[:, :f].astype(F32), x_ref[:, f:].astype(F32)
        s1, s2 = _shift_down_mxu(x_ref[:, :f], xb_ref[:, :f].astype(F32)[HALO - 2:] * inner_lo)
        gc = w2 * g + w1 * s1 + w0 * s2
        dgc, du = dgate(gc, u, da_ref[...].astype(F32))
        ga = xa_ref[:, :f].astype(F32)
        a1, a2 = _shift_down(ga, x_ref[tm - HALO:, :f].astype(F32)[HALO - 2:])
        gca = w2 * ga + w1 * a1 + w0 * a2
        dgca, _ = dgate(gca, xa_ref[:, f:].astype(F32), daa_ref[...].astype(F32))
        n1, n2 = _shift_up_mxu(dgc, dgca[0:2] * inner_hi)
        dx_ref[:, :f] = (w2 * dgc + w1 * n1 + w0 * n2).astype(BF16)
        dx_ref[:, f:] = du.astype(BF16)

        @pl.when(i == 0)
        def _():
            dw_ref[...] = jnp.zeros_like(dw_ref)

        dw_ref[0:1] += jnp.sum(dgc * s2, axis=0, keepdims=True)
        dw_ref[1:2] += jnp.sum(dgc * s1, axis=0, keepdims=True)
        dw_ref[2:3] += jnp.sum(dgc * g, axis=0, keepdims=True)

    return pl.pallas_call(
        body,
        name=name,
        grid=(t // tm,),
        in_specs=[pl.BlockSpec((tm, f2), lambda i: (i, 0)), before, after,
                  pl.BlockSpec((tm, f), lambda i: (i, 0)), after_da, pl.BlockSpec((3, f), lambda i: (0, 0))],
        out_specs=[pl.BlockSpec((tm, f2), lambda i: (i, 0)), pl.BlockSpec((8, f), lambda i: (0, 0))],
        out_shape=[jax.ShapeDtypeStruct((t, f2), BF16), jax.ShapeDtypeStruct((8, f), F32)],
        compiler_params=_params(("arbitrary",)),
    )(gu, gu, gu, da, da, w)


def _swap_halves(xt):
    half = HEAD_DIM // 2
    return jnp.concatenate([xt[half:], xt[:half]], axis=0)


def _rope(xt, cos, sin):
    return xt * cos + _swap_halves(xt) * sin


def _unrope(dxt, cos, sin):
    return dxt * cos - _swap_halves(dxt) * sin


def _key_query(count):
    kj = lax.broadcasted_iota(jnp.int32, (WINDOW, count * WINDOW), 0)
    qi = lax.broadcasted_iota(jnp.int32, (WINDOW, count * WINDOW), 1) & (WINDOW - 1)
    return kj, qi


def _band_masks(n, count):
    kj, qi = _key_query(count)
    return kj <= qi, jnp.logical_and(kj > qi, n > 0)


def _lanes(v, count):
    return jnp.concatenate([v] * count, axis=1) if count > 1 else v


def _heads(ref, h0, count):
    parts = [ref[(h0 + g) * HEAD_DIM:(h0 + g + 1) * HEAD_DIM, :] for g in range(count)]
    return jnp.concatenate(parts, axis=1) if count > 1 else parts[0]


def _head_rows(ref, h0, count):
    parts = [ref[h0 + g:h0 + g + 1, :] for g in range(count)]
    return jnp.concatenate(parts, axis=1) if count > 1 else parts[0]


def _head_sinks(sink_ref, h0, count):
    parts = [jnp.full((1, WINDOW), sink_ref[h0 + g], F32) for g in range(count)]
    return jnp.concatenate(parts, axis=1) if count > 1 else parts[0]


def _tn(a, b):
    return lax.dot_general(a, b, _DIMS["tn"], preferred_element_type=F32)


def _nt(a, b):
    return lax.dot_general(a, b, _DIMS["nt"], preferred_element_type=F32)


def _nn(a, b):
    return jnp.dot(a, b, preferred_element_type=F32)


def _attn_fwd(qkv, sinks, cos_t, sin_t, bsz, seq, name, hp):
    t, qw = qkv.shape
    d = qw * 2 // 3
    kvw = d // GROUP
    n_heads, n_kv = d // HEAD_DIM, kvw // HEAD_DIM
    nb = seq // WINDOW
    scale = HEAD_DIM ** -0.5

    def body(sink_ref, xc_ref, xp_ref, cc_ref, sc_ref, cp_ref, sp_ref, o_ref, lse_ref, xt_ref, pt_ref, ot_ref):
        n = pl.program_id(1)
        xt_ref[...] = xc_ref[...].T
        pt_ref[...] = xp_ref[:, d:].T
        cos_c, sin_c, cos_p, sin_p = cc_ref[...], sc_ref[...], cp_ref[...], sp_ref[...]
        cos_g, sin_g = _lanes(cos_c, hp), _lanes(sin_c, hp)
        valid_c, valid_p = _band_masks(n, hp)
        for j in range(n_kv):
            ko = j * HEAD_DIM
            kc = _rope(xt_ref[d + ko:d + ko + HEAD_DIM, :], cos_c, sin_c).astype(BF16)
            kp = _rope(pt_ref[ko:ko + HEAD_DIM, :], cos_p, sin_p).astype(BF16)
            vc = xt_ref[d + kvw + ko:d + kvw + ko + HEAD_DIM, :].astype(BF16)
            vp = pt_ref[kvw + ko:kvw + ko + HEAD_DIM, :].astype(BF16)
            for h0 in range(j * GROUP, (j + 1) * GROUP, hp):
                q = _rope(_heads(xt_ref, h0, hp), cos_g, sin_g).astype(BF16)
                sink = _head_sinks(sink_ref, h0, hp)
                s_c = jnp.where(valid_c, _tn(kc, q) * scale, NEG)
                s_p = jnp.where(valid_p, _tn(kp, q) * scale, NEG)
                m = jnp.maximum(jnp.maximum(jnp.max(s_c, axis=0, keepdims=True), jnp.max(s_p, axis=0, keepdims=True)), sink)
                p_c = jnp.exp(s_c - m)
                p_p = jnp.exp(s_p - m)
                den = jnp.sum(p_c, axis=0, keepdims=True) + jnp.sum(p_p, axis=0, keepdims=True) + jnp.exp(sink - m)
                inv = 1.0 / den
                o_g = _nn(vc, (p_c * inv).astype(BF16)) + _nn(vp, (p_p * inv).astype(BF16))
                lse_g = m + jnp.log(den)
                for g in range(hp):
                    h = h0 + g
                    ot_ref[h * HEAD_DIM:(h + 1) * HEAD_DIM, :] = o_g[:, g * WINDOW:(g + 1) * WINDOW]
                    lse_ref[h:h + 1, :] = lse_g[:, g * WINDOW:(g + 1) * WINDOW]
        o_ref[...] = ot_ref[...].T.astype(BF16)

    cur = lambda b, n: (b * nb + n, 0)
    prev = lambda b, n: (b * nb + jnp.maximum(n - 1, 0), 0)
    tab_c = pl.BlockSpec((HEAD_DIM, WINDOW), lambda b, n: (0, n))
    tab_p = pl.BlockSpec((HEAD_DIM, WINDOW), lambda b, n: (0, jnp.maximum(n - 1, 0)))
    return pl.pallas_call(
        body,
        name=name,
        grid=(bsz, nb),
        in_specs=[SMEM, pl.BlockSpec((WINDOW, qw), cur), pl.BlockSpec((WINDOW, qw), prev), tab_c, tab_c, tab_p, tab_p],
        out_specs=[pl.BlockSpec((WINDOW, d), cur), pl.BlockSpec((n_heads, WINDOW), lambda b, n: (0, b * nb + n))],
        out_shape=[jax.ShapeDtypeStruct((t, d), BF16), jax.ShapeDtypeStruct((n_heads, t), F32)],
        scratch_shapes=[pltpu.VMEM((qw, WINDOW), F32), pltpu.VMEM((2 * kvw, WINDOW), F32), pltpu.VMEM((d, WINDOW), F32)],
        compiler_params=_params(("parallel", "arbitrary")),
    )(sinks, qkv, qkv, cos_t, sin_t, cos_t, sin_t)


def _attn_bwd(qkv, o, lse, do, sinks, cos_t, sin_t, bsz, seq, name, hp):
    t, qw = qkv.shape
    d = qw * 2 // 3
    kvw = d // GROUP
    n_heads, n_kv = d // HEAD_DIM, kvw // HEAD_DIM
    nb = seq // WINDOW
    scale = HEAD_DIM ** -0.5

    def body(sink_ref, xc_ref, xp_ref, oc_ref, doc_ref, lc_ref, cc_ref, sc_ref, cp_ref, sp_ref,
             dx_ref, db_ref, dsk_ref, xt_ref, pt_ref, otc_ref, dtc_ref, gt_ref, carry_ref):
        b, n = pl.program_id(0), pl.program_id(1)
        live = n < nb
        xt_ref[...] = xc_ref[...].T
        pt_ref[...] = xp_ref[:, d:].T
        otc_ref[...] = oc_ref[...].astype(F32).T
        dtc_ref[...] = doc_ref[...].T
        cos_c, sin_c, cos_p, sin_p = cc_ref[...], sc_ref[...], cp_ref[...], sp_ref[...]
        cos_g, sin_g = _lanes(cos_c, hp), _lanes(sin_c, hp)
        kj, qi = _key_query(hp)
        valid_c = jnp.logical_and(kj <= qi, live)
        valid_p = jnp.logical_and(kj > qi, jnp.logical_and(n > 0, live))

        @pl.when(jnp.logical_and(b == 0, n == 0))
        def _():
            db_ref[...] = jnp.zeros_like(db_ref)
            dsk_ref[...] = jnp.zeros_like(dsk_ref)

        @pl.when(n == 0)
        def _():
            carry_ref[...] = jnp.zeros_like(carry_ref)

        for j in range(n_kv):
            ko = j * HEAD_DIM
            k_rows = slice(d + ko, d + ko + HEAD_DIM)
            v_rows = slice(d + kvw + ko, d + kvw + ko + HEAD_DIM)
            kc = _rope(xt_ref[k_rows, :], cos_c, sin_c).astype(BF16)
            kp = _rope(pt_ref[ko:ko + HEAD_DIM, :], cos_p, sin_p).astype(BF16)
            vc = xt_ref[v_rows, :].astype(BF16)
            vp = pt_ref[kvw + ko:kvw + ko + HEAD_DIM, :].astype(BF16)
            dk_c = jnp.zeros((HEAD_DIM, WINDOW), F32)
            dv_c = jnp.zeros((HEAD_DIM, WINDOW), F32)
            dk_p = jnp.zeros((HEAD_DIM, WINDOW), F32)
            dv_p = jnp.zeros((HEAD_DIM, WINDOW), F32)
            for h0 in range(j * GROUP, (j + 1) * GROUP, hp):
                q = _rope(_heads(xt_ref, h0, hp), cos_g, sin_g).astype(BF16)
                do_g = _heads(dtc_ref, h0, hp)
                do_b = do_g.astype(BF16)
                lse_g = _head_rows(lc_ref, h0, hp)
                delta = jnp.sum(_heads(otc_ref, h0, hp) * do_g, axis=0, keepdims=True)
                p_c = jnp.exp(jnp.where(valid_c, _tn(kc, q) * scale, NEG) - lse_g)
                p_p = jnp.exp(jnp.where(valid_p, _tn(kp, q) * scale, NEG) - lse_g)
                ds_c = (p_c * (_tn(vc, do_b) - delta)).astype(BF16)
                ds_p = (p_p * (_tn(vp, do_b) - delta)).astype(BF16)
                dq = _unrope((_nn(kc, ds_c) + _nn(kp, ds_p)) * scale, cos_g, sin_g)
                dsk = jnp.where(live, -jnp.exp(_head_sinks(sink_ref, h0, hp) - lse_g) * delta, 0.0)
                for g in range(hp):
                    rows = slice((h0 + g) * HEAD_DIM, (h0 + g + 1) * HEAD_DIM)
                    gt_ref[rows, :] = carry_ref[rows, :]
                    carry_ref[rows, :] = dq[:, g * WINDOW:(g + 1) * WINDOW]
                    dsk_ref[h0 + g:h0 + g + 1, :] += dsk[:, g * WINDOW:(g + 1) * WINDOW]
                dv_c += _nt(do_b, p_c.astype(BF16))
                dk_c += _nt(q, ds_c)
                dv_p += _nt(do_b, p_p.astype(BF16))
                dk_p += _nt(q, ds_p)
            gt_ref[k_rows, :] = _unrope((carry_ref[k_rows, :] + dk_p) * scale, cos_p, sin_p)
            gt_ref[v_rows, :] = carry_ref[v_rows, :] + dv_p
            carry_ref[k_rows, :] = dk_c
            carry_ref[v_rows, :] = dv_c
        dx = gt_ref[...].T
        dx_ref[...] = dx.astype(BF16)
        db_ref[...] += _fold8(dx)

    cur = lambda b, n: (b * nb + jnp.minimum(n, nb - 1), 0)
    prev = lambda b, n: (b * nb + jnp.maximum(jnp.minimum(n, nb - 1) - 1, 0), 0)
    done = lambda b, n: (b * nb + jnp.maximum(n - 1, 0), 0)
    stat_c = pl.BlockSpec((n_heads, WINDOW), lambda b, n: (0, b * nb + jnp.minimum(n, nb - 1)))
    tab_c = pl.BlockSpec((HEAD_DIM, WINDOW), lambda b, n: (0, jnp.minimum(n, nb - 1)))
    tab_p = pl.BlockSpec((HEAD_DIM, WINDOW), lambda b, n: (0, jnp.maximum(n - 1, 0)))
    return pl.pallas_call(
        body,
        name=name,
        grid=(bsz, nb + 1),
        in_specs=[SMEM, pl.BlockSpec((WINDOW, qw), cur), pl.BlockSpec((WINDOW, qw), prev),
                  pl.BlockSpec((WINDOW, d), cur), pl.BlockSpec((WINDOW, d), cur), stat_c, tab_c, tab_c, tab_p, tab_p],
        out_specs=[pl.BlockSpec((WINDOW, qw), done), pl.BlockSpec((8, qw), lambda b, n: (0, 0)),
                   pl.BlockSpec((n_heads, WINDOW), lambda b, n: (0, 0))],
        out_shape=[jax.ShapeDtypeStruct((t, qw), BF16), jax.ShapeDtypeStruct((8, qw), F32),
                   jax.ShapeDtypeStruct((n_heads, WINDOW), F32)],
        scratch_shapes=[pltpu.VMEM((qw, WINDOW), F32), pltpu.VMEM((2 * kvw, WINDOW), F32), pltpu.VMEM((d, WINDOW), F32),
                        pltpu.VMEM((d, WINDOW), F32), pltpu.VMEM((qw, WINDOW), F32), pltpu.VMEM((qw, WINDOW), F32)],
        compiler_params=_params(("arbitrary", "arbitrary")),
    )(sinks, qkv, qkv, o, do, lse, cos_t, sin_t, cos_t, sin_t)


def _place():
    return lax.axis_index("x"), lax.axis_index("y"), lax.axis_index("c")


def _other_chips(x, y):
    return [(1 - x, y), (x, 1 - y), (1 - x, 1 - y)]


def _place_shard(w, axis, q, name):
    ly, k, n = w.shape
    tr = _pick(k, (256, 128, 64, 32, 16, 8))
    steps = k // tr
    shape = (ly, k * N_CHIPS, n) if axis == 1 else (ly, k, n * N_CHIPS)
    if axis == 1:
        out_spec = pl.BlockSpec((None, tr, n), lambda l, i, q_ref: (l, q_ref[0] * steps + i, 0))
    else:
        out_spec = pl.BlockSpec((None, tr, n), lambda l, i, q_ref: (l, i, q_ref[0]))

    def body(q_ref, w_ref, o_ref):
        del q_ref
        o_ref[...] = w_ref[...].astype(BF16)

    return pl.pallas_call(
        body,
        name=name,
        grid_spec=pltpu.PrefetchScalarGridSpec(
            num_scalar_prefetch=1, grid=(ly, steps),
            in_specs=[pl.BlockSpec((None, tr, n), lambda l, i, q_ref: (l, i, 0))], out_specs=out_spec),
        out_shape=jax.ShapeDtypeStruct(shape, BF16),
        compiler_params=_params(("parallel", "parallel")),
    )(q, w)


def _half_block(ref, axis, layer, px, py, pc):
    blk = 2 * px + py
    if axis == 1:
        rows = ref.shape[1] // (2 * N_CHIPS)
        return ref.at[layer, pl.ds(pl.multiple_of((2 * blk + pc) * rows, 8), rows), :]
    rows, width = ref.shape[1] // 2, ref.shape[2] // N_CHIPS
    return ref.at[layer, pl.ds(pl.multiple_of(pc * rows, 8), rows), pl.ds(pl.multiple_of(blk * width, 128), width)]


def _gather_copy(refs, axes, pieces, send_sems, recv_sems, p, k, stage, whose):
    x, y, c = _place()
    chip = _other_chips(x, y)[k]
    i, layer = pieces[p]
    if stage == 0:
        origin = (x, y, c) if whose == "mine" else (*chip, c)
        to = (*chip, c)
    else:
        origin = (*chip, c) if whose == "mine" else (*chip, 1 - c)
        to = (x, y, 1 - c)
    blk = _half_block(refs[i], axes[i], layer, *origin)
    return pltpu.make_async_remote_copy(src_ref=blk, dst_ref=blk, send_sem=send_sems.at[p * 3 + k],
                                        recv_sem=recv_sems.at[p * 3 + k], device_id=to, device_id_type=MESH)


HBM_SPEC = pl.BlockSpec(memory_space=pltpu.HBM)
SEM_SPEC = pl.BlockSpec(memory_space=pltpu.SEMAPHORE)


def _gather_start(fulls, axes, pieces, stage, name):
    n, m = len(fulls), 3 * len(pieces)

    def body(*refs):
        src = refs[:n]
        send_sems, recv_sems = refs[2 * n], refs[2 * n + 1]
        for p in range(len(pieces)):
            for k in range(3):
                _gather_copy(src, axes, pieces, send_sems, recv_sems, p, k, stage, "mine").start()

    out = pl.pallas_call(
        body,
        name=name,
        in_specs=[HBM_SPEC] * n,
        out_specs=[HBM_SPEC] * n + [SEM_SPEC, SEM_SPEC],
        out_shape=[pltpu.HBM(f.shape, f.dtype) for f in fulls] + [pltpu.SemaphoreType.DMA((m,)), pltpu.SemaphoreType.DMA((m,))],
        input_output_aliases={i: i for i in range(n)},
        compiler_params=pltpu.CompilerParams(has_side_effects=pltpu.SideEffectType.DATAFLOW_SIDE_EFFECTING),
    )(*[pltpu.with_memory_space_constraint(f, pltpu.HBM) for f in fulls])
    return list(out[:n]), out[n], out[n + 1]


def _gather_wait(fulls, send_sems, recv_sems, after, axes, pieces, stage, name):
    n = len(fulls)

    def body(*refs):
        src = refs[:n]
        s_sems, r_sems = refs[n], refs[n + 1]
        for p in range(len(pieces)):
            for k in range(3):
                _gather_copy(src, axes, pieces, s_sems, r_sems, p, k, stage, "mine").wait_send()
                _gather_copy(src, axes, pieces, s_sems, r_sems, p, k, stage, "theirs").wait_recv()

    out = pl.pallas_call(
        body,
        name=name,
        in_specs=[HBM_SPEC] * n + [SEM_SPEC, SEM_SPEC, ANY],
        out_specs=[HBM_SPEC] * n,
        out_shape=[pltpu.HBM(f.shape, f.dtype) for f in fulls],
        input_output_aliases={i: i for i in range(n)},
        compiler_params=pltpu.CompilerParams(has_side_effects=pltpu.SideEffectType.DATAFLOW_SIDE_EFFECTING),
    )(*fulls, send_sems, recv_sems, after)
    return list(out)


def _half_shape(kind, shape):
    if kind == "col":
        return (shape[0] // 2, shape[1])
    return (N_CHIPS, shape[1] // 2, shape[2])


def _half_of(kind, ref, h):
    if kind == "col":
        r = ref.shape[0] // 2
        return ref.at[pl.ds(pl.multiple_of(h * r, 8), r), :]
    r = ref.shape[1] // 2
    return ref.at[:, pl.ds(pl.multiple_of(h * r, 8), r), :]


def _pair_copy(src, dst, kinds, send_sems, recv_sems, i):
    x, y, c = _place()
    return pltpu.make_async_remote_copy(src_ref=_half_of(kinds[i], src[i], 1 - c), dst_ref=dst[i], send_sem=send_sems.at[i],
                                        recv_sem=recv_sems.at[i], device_id=(x, y, 1 - c), device_id_type=MESH)


def _pair_start(grads, kinds, token, name):
    n = len(grads)
    lands = [pltpu.HBM(_half_shape(kd, g.shape), g.dtype) for g, kd in zip(grads, kinds)]

    def body(*refs):
        src, dst = refs[:n], refs[2 * n + 2:3 * n + 2]
        send_sems, recv_sems = refs[3 * n + 2], refs[3 * n + 3]
        for i in range(n):
            _pair_copy(src, dst, kinds, send_sems, recv_sems, i).start()

    arrays = list(grads) + [token]
    out = pl.pallas_call(
        body,
        name=name,
        in_specs=[HBM_SPEC] * (n + 1),
        out_specs=[HBM_SPEC] * (2 * n + 1) + [SEM_SPEC, SEM_SPEC],
        out_shape=[pltpu.HBM(a.shape, a.dtype) for a in arrays] + lands + [pltpu.SemaphoreType.DMA((n,)), pltpu.SemaphoreType.DMA((n,))],
        input_output_aliases={i: i for i in range(n + 1)},
        compiler_params=pltpu.CompilerParams(has_side_effects=pltpu.SideEffectType.DATAFLOW_SIDE_EFFECTING),
    )(*[pltpu.with_memory_space_constraint(a, pltpu.HBM) for a in arrays])
    return list(out[:n]), out[n], list(out[n + 1:2 * n + 1]), out[2 * n + 1], out[2 * n + 2]


def _pair_wait(grads, kinds, token, lands, send_sems, recv_sems, name):
    n = len(grads)

    def body(*refs):
        src, dst = refs[:n], refs[n + 1:2 * n + 1]
        s_sems, r_sems = refs[2 * n + 1], refs[2 * n + 2]
        for i in range(n):
            cp = _pair_copy(src, dst, kinds, s_sems, r_sems, i)
            cp.wait_send()
            cp.wait_recv()

    arrays = list(grads) + [token] + list(lands)
    out = pl.pallas_call(
        body,
        name=name,
        in_specs=[HBM_SPEC] * (2 * n + 1) + [SEM_SPEC, SEM_SPEC],
        out_specs=[HBM_SPEC] * (2 * n + 1),
        out_shape=[pltpu.HBM(a.shape, a.dtype) for a in arrays],
        input_output_aliases={i: i for i in range(2 * n + 1)},
        compiler_params=pltpu.CompilerParams(has_side_effects=pltpu.SideEffectType.DATAFLOW_SIDE_EFFECTING),
    )(*arrays, send_sems, recv_sems)
    return list(out[:n]), out[n], list(out[n + 1:])


def _pair_sum(grad, recv, kind, c, own, layer, name):
    r, cols = own.shape[2:]
    if kind == "col":
        tr = _pick(r, (256, 128, 64, 32, 16, 8))
        steps = r // tr
        grid = (N_CHIPS, steps)
        g_spec = pl.BlockSpec((tr, cols), lambda s, i, c_ref: (c_ref[0] * steps + i, s))
        r_spec = pl.BlockSpec((tr, cols), lambda s, i, c_ref: (i, s))
        o_spec = pl.BlockSpec((None, None, tr, cols), lambda s, i, c_ref: (layer, s, i, 0))
        g_in = grad
    else:
        grid = (N_CHIPS, 1)
        g_spec = pl.BlockSpec((None, None, r, cols), lambda s, i, c_ref: (s, c_ref[0], 0, 0))
        r_spec = pl.BlockSpec((None, r, cols), lambda s, i, c_ref: (s, 0, 0))
        o_spec = pl.BlockSpec((None, None, r, cols), lambda s, i, c_ref: (layer, s, 0, 0))
        g_in = grad.reshape(N_CHIPS, 2, r, cols)

    def body(c_ref, g_ref, r_ref, own_ref, o_ref):
        del c_ref, own_ref
        o_ref[...] = (g_ref[...].astype(F32) + r_ref[...].astype(F32)).astype(o_ref.dtype)

    return pl.pallas_call(
        body,
        name=name,
        grid_spec=pltpu.PrefetchScalarGridSpec(num_scalar_prefetch=1, grid=grid, in_specs=[g_spec, r_spec, ANY], out_specs=o_spec),
        out_shape=jax.ShapeDtypeStruct(own.shape, own.dtype),
        input_output_aliases={3: 0},
        compiler_params=_params(("parallel", "parallel")),
    )(c, g_in, recv, own)


def _scatter_copy(own, mine, sib, pieces, send_sems, recv_sems, p, k, stage, whose):
    x, y, c = _place()
    q = 2 * x + y
    i, layer = pieces[p]
    per = 4 if stage == 0 else 3
    if k == 3:
        src, dst, to = own[i].at[layer, q], sib[i].at[layer, q], (x, y, 1 - c)
    else:
        chip = _other_chips(x, y)[k]
        slot = 2 * chip[0] + chip[1]
        if stage == 0:
            to = (*chip, c)
            src, dst = (own[i].at[layer, slot], mine[i].at[layer, q]) if whose == "mine" else (own[i].at[layer, q], mine[i].at[layer, slot])
        else:
            to = (x, y, 1 - c)
            src, dst = mine[i].at[layer, slot], sib[i].at[layer, slot]
    return pltpu.make_async_remote_copy(src_ref=src, dst_ref=dst, send_sem=send_sems.at[p * per + k],
                                        recv_sem=recv_sems.at[p * per + k], device_id=to, device_id_type=MESH)


def _scatter_start(own, mine, sib, token, pieces, stage, name):
    n = len(own)
    per = 4 if stage == 0 else 3
    m = per * len(pieces)
    n_arr = 3 * n + 1

    def body(*refs):
        o, mi, si = refs[:n], refs[n:2 * n], refs[2 * n:3 * n]
        send_sems, recv_sems = refs[2 * n_arr], refs[2 * n_arr + 1]
        for p in range(len(pieces)):
            for k in range(per):
                _scatter_copy(o, mi, si, pieces, send_sems, recv_sems, p, k, stage, "mine").start()

    arrays = list(own) + list(mine) + list(sib) + [token]
    out = pl.pallas_call(
        body,
        name=name,
        in_specs=[HBM_SPEC] * n_arr,
        out_specs=[HBM_SPEC] * n_arr + [SEM_SPEC, SEM_SPEC],
        out_shape=[pltpu.HBM(a.shape, a.dtype) for a in arrays] + [pltpu.SemaphoreType.DMA((m,)), pltpu.SemaphoreType.DMA((m,))],
        input_output_aliases={i: i for i in range(n_arr)},
        compiler_params=pltpu.CompilerParams(has_side_effects=pltpu.SideEffectType.DATAFLOW_SIDE_EFFECTING),
    )(*[pltpu.with_memory_space_constraint(a, pltpu.HBM) for a in arrays])
    return list(out[:n]), list(out[n:2 * n]), list(out[2 * n:3 * n]), out[3 * n], out[n_arr], out[n_arr + 1]


def _scatter_wait(own, mine, sib, token, send_sems, recv_sems, pieces, stage, name):
    n = len(own)
    per = 4 if stage == 0 else 3
    n_arr = 3 * n + 1

    def body(*refs):
        o, mi, si = refs[:n], refs[n:2 * n], refs[2 * n:3 * n]
        s_sems, r_sems = refs[n_arr], refs[n_arr + 1]
        for p in range(len(pieces)):
            for k in range(per):
                _scatter_copy(o, mi, si, pieces, s_sems, r_sems, p, k, stage, "mine").wait_send()
                _scatter_copy(o, mi, si, pieces, s_sems, r_sems, p, k, stage, "theirs").wait_recv()

    arrays = list(own) + list(mine) + list(sib) + [token]
    out = pl.pallas_call(
        body,
        name=name,
        in_specs=[HBM_SPEC] * n_arr + [SEM_SPEC, SEM_SPEC],
        out_specs=[HBM_SPEC] * n_arr,
        out_shape=[pltpu.HBM(a.shape, a.dtype) for a in arrays],
        input_output_aliases={i: i for i in range(n_arr)},
        compiler_params=pltpu.CompilerParams(has_side_effects=pltpu.SideEffectType.DATAFLOW_SIDE_EFFECTING),
    )(*arrays, send_sems, recv_sems)
    return list(out[:n]), list(out[n:2 * n]), list(out[2 * n:3 * n]), out[3 * n]


def _reduce_adamw(own, mine, sib, w, m, v, qc, name):
    ly, _, r, cols = mine.shape
    tr = _pick(r, (128, 64, 32, 16, 8))
    steps = r // tr
    c1 = 1.0 - ADAM_B1 ** ADAM_STEP
    c2 = 1.0 - ADAM_B2 ** ADAM_STEP

    def body(qc_ref, own_ref, mine_ref, sib_ref, w_ref, m_ref, v_ref, g_ref, d_ref, nm_ref, nv_ref):
        q = qc_ref[0]
        mine_sum = sib_sum = None
        for s in range(N_CHIPS):
            a = jnp.where(q == s, own_ref[...], mine_ref[s]).astype(F32)
            b = sib_ref[s].astype(F32)
            mine_sum = a if s == 0 else mine_sum + a
            sib_sum = b if s == 0 else sib_sum + b
        gv = jnp.where(pl.program_id(1) == qc_ref[1], mine_sum, sib_sum)
        nm = ADAM_B1 * m_ref[...] + (1.0 - ADAM_B1) * gv
        nv = ADAM_B2 * v_ref[...] + (1.0 - ADAM_B2) * (gv * gv)
        g_ref[...] = gv
        d_ref[...] = -ADAM_LR * ((nm / c1) / (jnp.sqrt(nv / c2) + ADAM_EPS) + ADAM_WD * w_ref[...])
        nm_ref[...] = nm
        nv_ref[...] = nv

    def mine_rows(h, i, qc_ref):
        return jnp.where(h == qc_ref[1], i, 0)

    def sib_rows(h, i, qc_ref):
        return jnp.where(h == qc_ref[1], 0, i)

    own_spec = pl.BlockSpec((None, None, tr, cols), lambda l, h, i, qc_ref: (l, qc_ref[0], mine_rows(h, i, qc_ref), 0))
    mine_spec = pl.BlockSpec((None, N_CHIPS, tr, cols), lambda l, h, i, qc_ref: (l, 0, mine_rows(h, i, qc_ref), 0))
    sib_spec = pl.BlockSpec((None, N_CHIPS, tr, cols), lambda l, h, i, qc_ref: (l, 0, sib_rows(h, i, qc_ref), 0))
    spec = pl.BlockSpec((None, tr, cols), lambda l, h, i, qc_ref: (l, h * steps + i, 0))
    shp = jax.ShapeDtypeStruct(w.shape, F32)
    return pl.pallas_call(
        body,
        name=name,
        grid_spec=pltpu.PrefetchScalarGridSpec(
            num_scalar_prefetch=1, grid=(ly, N_CORES, steps),
            in_specs=[own_spec, mine_spec, sib_spec, spec, spec, spec], out_specs=[spec] * 4),
        out_shape=[shp] * 4,
        compiler_params=_params(("parallel", "parallel", "parallel")),
    )(qc, own, mine, sib, w, m, v)


def _allreduce_small(v, name):
    r, w = v.shape

    def body(v_ref, o_ref, buf_ref, send_sems, recv_sems):
        x, y, c = _place()
        me = 4 * x + 2 * y + c

        def peer(k):
            return x ^ (k >> 2), y ^ ((k >> 1) & 1), c ^ (k & 1)

        def remote(k, slot):
            return pltpu.make_async_remote_copy(
                src_ref=v_ref, dst_ref=buf_ref.at[slot], send_sem=send_sems.at[k - 1], recv_sem=recv_sems.at[k - 1],
                device_id=peer(k), device_id_type=MESH)

        sends = [remote(k, me) for k in range(1, N_DEV)]
        for cp in sends:
            cp.start()
        buf_ref[me] = v_ref[...]
        for k in range(1, N_DEV):
            px, py, pc = peer(k)
            remote(k, 4 * px + 2 * py + pc).wait_recv()
        for cp in sends:
            cp.wait_send()
        acc = buf_ref[0]
        for dev in range(1, N_DEV):
            acc = acc + buf_ref[dev]
        o_ref[...] = acc

    vm = pl.BlockSpec(memory_space=pltpu.VMEM)
    return pl.pallas_call(
        body,
        name=name,
        in_specs=[vm],
        out_specs=vm,
        out_shape=jax.ShapeDtypeStruct((r, w), F32),
        scratch_shapes=[pltpu.VMEM((N_DEV, r, w), F32), pltpu.SemaphoreType.DMA((N_DEV - 1,)), pltpu.SemaphoreType.DMA((N_DEV - 1,))],
        compiler_params=pltpu.CompilerParams(vmem_limit_bytes=VMEM_LIMIT_BYTES),
    )(v)


def _adamw(w, g, m, v, name):
    ly, r, c = w.shape
    tr = _pick(r, (256, 128, 64, 32, 16, 8))
    c1 = 1.0 - ADAM_B1 ** ADAM_STEP
    c2 = 1.0 - ADAM_B2 ** ADAM_STEP

    def body(w_ref, g_ref, m_ref, v_ref, d_ref, nm_ref, nv_ref):
        gv = g_ref[...]
        nm = ADAM_B1 * m_ref[...] + (1.0 - ADAM_B1) * gv
        nv = ADAM_B2 * v_ref[...] + (1.0 - ADAM_B2) * (gv * gv)
        d_ref[...] = -ADAM_LR * ((nm / c1) / (jnp.sqrt(nv / c2) + ADAM_EPS) + ADAM_WD * w_ref[...])
        nm_ref[...] = nm
        nv_ref[...] = nv

    spec = pl.BlockSpec((None, tr, c), lambda l, i: (l, i, 0))
    shp = jax.ShapeDtypeStruct((ly, r, c), F32)
    return pl.pallas_call(
        body,
        name=name,
        grid=(ly, r // tr),
        in_specs=[spec] * 4,
        out_specs=[spec] * 3,
        out_shape=[shp] * 3,
        compiler_params=_params(("parallel", "parallel")),
    )(w, g, m, v)


def _rope_tables(seq):
    pos = jnp.arange(seq, dtype=F32)
    inv_freq = 1.0 / (ROPE_THETA ** (jnp.arange(0, HEAD_DIM, 2, dtype=F32) / HEAD_DIM))
    ang = (pos[:, None] * inv_freq[None, :]).T
    cos, sin = jnp.cos(ang), jnp.sin(ang)
    return jnp.concatenate([cos, cos], axis=0), jnp.concatenate([-sin, sin], axis=0)


def _pack(vs, fill=0.0):
    p = jnp.concatenate([v.reshape(-1) for v in vs])
    size = -(-p.shape[0] // 8192) * 8192
    return jnp.pad(p, (0, size - p.shape[0]), constant_values=fill).reshape(-1, 1024)


def _unpack(p, like):
    p = p.reshape(-1)
    out, o = [], 0
    for v in like:
        n = int(math.prod(v.shape))
        out.append(p[o:o + n].reshape(v.shape))
        o += n
    return out


def kernel(x, norm_mix, norm_ffn, norm_final, conv_w_in, conv_w_conv, conv_w_out, attn_w_qkv, attn_b_qkv, attn_sinks, attn_w_o, attn_b_o, ffn_w_in, ffn_w_conv, ffn_w_down, loss_target, m_norm_mix, m_norm_ffn, m_norm_final, m_conv_w_in, m_conv_w_conv, m_conv_w_out, m_attn_w_qkv, m_attn_b_qkv, m_attn_sinks, m_attn_w_o, m_attn_b_o, m_ffn_w_in, m_ffn_w_conv, m_ffn_w_down, v_norm_mix, v_norm_ffn, v_norm_final, v_conv_w_in, v_conv_w_conv, v_conv_w_out, v_attn_w_qkv, v_attn_b_qkv, v_attn_sinks, v_attn_w_o, v_attn_b_o, v_ffn_w_in, v_ffn_w_conv, v_ffn_w_down):
    bsz, seq, d = x.shape
    t = bsz * seq
    depth = norm_mix.shape[0]
    n_conv, n_attn = conv_w_in.shape[0], attn_w_qkv.shape[0]
    xq, yq, cq = _place()
    q = 2 * xq + yq

    big = [conv_w_in, conv_w_out, attn_w_qkv, attn_w_o, ffn_w_in, ffn_w_down]
    axes = [2, 1, 2, 1, 2, 1]
    q_arr = q.astype(jnp.int32).reshape(1)
    c_arr = cq.astype(jnp.int32).reshape(1)
    weights = [None] * 6
    sems = {}

    def place(n):
        weights[n] = _place_shard(big[n], axes[n], q_arr, f"place_shard{n}")

    def pieces_of(i):
        return [(0, i // 2), (1, i // 2), (4, i), (5, i)] if i % 2 == 0 else [(2, i // 2), (3, i // 2), (4, i), (5, i)]

    def subset(pieces):
        ts = [ti for ti in range(6) if weights[ti] is not None]
        return ts, [(ts.index(ti), l) for ti, l in pieces]

    def fetch(pieces, stage, tag):
        ts, local = subset(pieces)
        out, sems[tag, 0], sems[tag, 1] = _gather_start([weights[ti] for ti in ts], [axes[ti] for ti in ts], local, stage,
                                                        f"gather_{'ici' if stage == 0 else 'pass'}_start{tag}")
        for n, ti in enumerate(ts):
            weights[ti] = out[n]

    def settle(pieces, stage, tag, after):
        ts, local = subset(pieces)
        out = _gather_wait([weights[ti] for ti in ts], sems[tag, 0], sems[tag, 1], after, [axes[ti] for ti in ts], local, stage,
                           f"gather_{'ici' if stage == 0 else 'pass'}_wait{tag}")
        for n, ti in enumerate(ts):
            weights[ti] = out[n]

    xs = x.reshape(t, d)
    place(0)
    place(1)
    fetch(pieces_of(0)[:2], 0, "m0")
    for n in (2, 3, 4, 5):
        place(n)
    h = _rms_fwd(xs, norm_mix[0:1], "norm_mix_fwd0")
    settle(pieces_of(0)[:2], 0, "m0", h)
    fetch(pieces_of(0)[:2], 1, "m0")

    small_cols = [conv_w_conv, attn_b_qkv, attn_b_o, ffn_w_conv]

    def placed(v):
        width = v.shape[-1]
        full = jnp.zeros(v.shape[:-1] + (N_CHIPS * width,), F32)
        return lax.dynamic_update_slice_in_dim(full, v * (1.0 / N_CORES), q * width, axis=v.ndim - 1)

    full_cols = [placed(v) for v in small_cols]
    small_full = _allreduce_small(_pack(full_cols), "gather_small")
    wc_conv, b_qkv, b_o, wf_conv = _unpack(small_full, full_cols)
    settle(pieces_of(0)[:2], 1, "m0", small_full)
    cos_t, sin_t = _rope_tables(seq)

    saved = []
    for i in range(depth):
        j = i // 2
        ahead = pieces_of(i + 1) if i + 1 < depth else None
        if i == 0:
            fetch(pieces_of(0)[2:], 0, "0")
        elif ahead:
            fetch(ahead, 0, str(i + 1))
        w_cin, w_cout, w_qkv, w_o, w_fin, w_fdown = weights
        g_ffn = norm_ffn[i:i + 1]
        if i % 2 == 0:
            pre = _mm(h, w_cin, "nn", BF16, layer=j, tm=1024, tn=768, tk=4096, name=f"conv_in_fwd{i}")
            mixed = _convgate_fwd(pre, wc_conv[j], seq, f"conv_gate_fwd{i}")
            if i == 0:
                settle(pieces_of(0)[2:], 0, "0", mixed)
                fetch(pieces_of(0)[2:], 1, "0")
                w_cin, w_cout, w_qkv, w_o, w_fin, w_fdown = weights
            x_mid, h2 = _mm(mixed, w_cout, "nn", F32, layer=j, residual=xs, norm_g=g_ffn, tm=512, tn=1024, tk=4096,
                            name=f"conv_out_fwd{i}")
            lse = None
        else:
            pre = _mm(h, w_qkv, "nn", F32, layer=j, bias=b_qkv[j:j + 1], tm=1024, tn=768, tk=4096, name=f"qkv_fwd{i}")
            mixed, lse = _attn_fwd(pre, attn_sinks[j], cos_t, sin_t, bsz, seq, f"attn_fwd{i}", hp=1)
            x_mid, h2 = _mm(mixed, w_o, "nn", F32, layer=j, bias=b_o[j:j + 1], residual=xs, norm_g=g_ffn, tm=512, tn=1024,
                            tk=4096, name=f"attn_out_fwd{i}")
        if i == 0:
            settle(pieces_of(0)[2:], 1, "0", h2)
            fetch(ahead, 0, "1")
        elif ahead:
            settle(ahead, 0, str(i + 1), x_mid)
            fetch(ahead, 1, str(i + 1))
        w_cin, w_cout, w_qkv, w_o, w_fin, w_fdown = weights
        gu = _mm(h2, w_fin, "nn", BF16, layer=i, n_outer=True, tm=1024, tn=2816, tk=4096, name=f"ffn_in_fwd{i}")
        if i == 0:
            settle(ahead, 0, "1", gu)
            fetch(ahead, 1, "1")
            w_cin, w_cout, w_qkv, w_o, w_fin, w_fdown = weights
        g_next = norm_mix[i + 1:i + 2] if i + 1 < depth else norm_final.reshape(1, d)
        x_next, act, h_next = _ffn_gate_down_fwd(gu, wf_conv[i], w_fdown, i, x_mid, g_next, seq, f"ffn_gate_down_fwd{i}")
        if ahead:
            settle(ahead, 1, str(i + 1), x_next)
        saved.append((xs, h, pre, mixed, lse, x_mid, h2, gu, act))
        xs, h = x_next, h_next
    w_cin, w_cout, w_qkv, w_o, w_fin, w_fdown = weights

    dx, dxb, sq, dg_final = _loss_head(xs, loss_target.reshape(t, d), norm_final.reshape(1, d), "loss_head")
    loss = lax.psum(0.5 * jnp.sum(sq) / d, ("x", "y", "c"))

    g_norm_mix, g_norm_ffn = [None] * depth, [None] * depth
    g_cin, g_cconv, g_cout = [None] * n_conv, [None] * n_conv, [None] * n_conv
    g_qkv, g_bqkv, g_sinks, g_o, g_bo = ([None] * n_attn for _ in range(5))
    g_fin, g_fconv, g_fdown = [None] * depth, [None] * depth, [None] * depth

    kinds6 = ["col", "row", "col", "row", "col", "row"]
    layers6 = [n_conv, n_conv, n_attn, n_attn, depth, depth]
    big_w = [conv_w_in, conv_w_out, attn_w_qkv, attn_w_o, ffn_w_in, ffn_w_down]

    def slot_stack(n):
        k, cols = big_w[n].shape[1], big_w[n].shape[2]
        r = k // 2
        return lax.empty((layers6[n], N_CHIPS, r, cols), BF16)

    own = [slot_stack(n) for n in range(6)]
    mine = [slot_stack(n) for n in range(6)]
    sib = [slot_stack(n) for n in range(6)]
    flight = {}

    def group(i, part):
        return f"{part}{i}", (pieces_of(i)[:2] if part == "m" else pieces_of(i)[2:])

    def scatter(grp, stage, action, token):
        tag, pieces = grp
        ts = [ti for ti, _ in pieces]
        local = [(n, l) for n, (_, l) in enumerate(pieces)]
        sub = ([own[ti] for ti in ts], [mine[ti] for ti in ts], [sib[ti] for ti in ts], token)
        label = f"grad_{'ici' if stage == 0 else 'pass'}_{action}_{tag}"
        if action == "start":
            o, mi, si, token, s_sems, r_sems = _scatter_start(*sub, local, stage, label)
            flight[tag] = (s_sems, r_sems)
        else:
            o, mi, si, token = _scatter_wait(*sub, *flight[tag], local, stage, label)
        for n, ti in enumerate(ts):
            own[ti], mine[ti], sib[ti] = o[n], mi[n], si[n]
        return token

    def pair_begin(grp, token):
        tag, pieces = grp
        grads = {0: g_cin, 1: g_cout, 2: g_qkv, 3: g_o, 4: g_fin, 5: g_fdown}
        parts, kinds = [], []
        for ti, l in pieces:
            g = grads[ti][l]
            parts.append(g if kinds6[ti] == "col" else g.reshape(N_CHIPS, g.shape[0] // N_CHIPS, g.shape[1]))
            kinds.append(kinds6[ti])
        parts, token, lands, s_sems, r_sems = _pair_start(parts, kinds, token, f"grad_pair_start_{tag}")
        flight["pair" + tag] = (parts, kinds, lands, s_sems, r_sems)
        return token

    def pair_finish(grp, token):
        tag, pieces = grp
        parts, kinds, lands, s_sems, r_sems = flight["pair" + tag]
        parts, token, recv = _pair_wait(parts, kinds, token, lands, s_sems, r_sems, f"grad_pair_wait_{tag}")
        for (ti, l), g, r in zip(pieces, parts, recv):
            own[ti] = _pair_sum(g, r, kinds6[ti], c_arr, own[ti], l, f"grad_pair_sum_{tag}_{ti}")
        return scatter(grp, 0, "start", token)

    for i in reversed(range(depth)):
        j = i // 2
        x_in, h, pre, mixed, lse, x_mid, h2, gu, act = saved[i]
        da = _mm(dxb, w_fdown, "nt", BF16, layer=i, n_outer=True, tm=1024, tn=2816, tk=4096, name=f"ffn_down_dx{i}")
        g_fdown[i] = _mm(act, dxb, "tn", BF16, tm=1408, tn=1024, tk=2048, name=f"ffn_down_dw{i}")
        dgu, dwc = _ffngate_bwd(gu, da, wf_conv[i], seq, f"ffn_gate_bwd{i}")
        g_fconv[i] = dwc[:3]
        g_fin[i] = _mm(h2, dgu, "tn", BF16, tm=1024, tn=1408, tk=2048, name=f"ffn_in_dw{i}")
        dgu = pair_begin(group(i, "f"), dgu)
        dx, dxb, dg, colsum = _rms_bwd(x_mid, dgu, w_fin, i, norm_ffn[i:i + 1], dx, f"ffn_in_dx_norm_bwd{i}")
        g_norm_ffn[i] = jnp.sum(dg, axis=0)
        if i + 1 < depth:
            dxb = scatter(group(i + 1, "m"), 1, "start", scatter(group(i + 1, "m"), 0, "wait", dxb))
            dxb = scatter(group(i + 1, "f"), 1, "wait", dxb)
        dxb = pair_finish(group(i, "f"), dxb)
        if i % 2 == 0:
            dmix = _mm(dxb, w_cout, "nt", BF16, layer=j, tm=512, tn=1024, tk=4096, name=f"conv_out_dx{i}")
            g_cout[j] = _mm(mixed, dxb, "tn", BF16, tm=1024, tn=1024, tk=2048, name=f"conv_out_dw{i}")
            dpre, dwc = _convgate_bwd(pre, dmix, wc_conv[j], seq, f"conv_gate_bwd{i}")
            g_cconv[j] = dwc[:3]
            g_cin[j] = _mm(h, dpre, "tn", BF16, tm=1024, tn=1536, tk=2048, name=f"conv_in_dw{i}")
            w_pre = w_cin
        else:
            g_bo[j] = jnp.sum(colsum, axis=0)
            dmix = _mm(dxb, w_o, "nt", F32, layer=j, tm=512, tn=1024, tk=4096, name=f"attn_out_dx{i}")
            g_o[j] = _mm(mixed, dxb, "tn", BF16, tm=1024, tn=1024, tk=2048, name=f"attn_out_dw{i}")
            dpre, dbias, dsk = _attn_bwd(pre, mixed, lse, dmix, attn_sinks[j], cos_t, sin_t, bsz, seq, f"attn_bwd{i}",
                                         hp=GROUP)
            g_bqkv[j] = jnp.sum(dbias, axis=0)
            g_sinks[j] = jnp.sum(dsk, axis=1)
            g_qkv[j] = _mm(h, dpre, "tn", BF16, tm=1024, tn=1536, tk=2048, name=f"qkv_dw{i}")
            w_pre = w_qkv
        dpre = pair_begin(group(i, "m"), dpre)
        dx, dxb, dg, _ = _rms_bwd(x_in, dpre, w_pre, j, norm_mix[i:i + 1], dx, f"mixer_in_dx_norm_bwd{i}")
        g_norm_mix[i] = jnp.sum(dg, axis=0)
        dxb = scatter(group(i, "f"), 1, "start", scatter(group(i, "f"), 0, "wait", dxb))
        if i + 1 < depth:
            dxb = scatter(group(i + 1, "m"), 1, "wait", dxb)
        dxb = pair_finish(group(i, "m"), dxb)
    grad_x = dx.reshape(bsz, seq, d)

    scatter(group(0, "f"), 1, "wait", dxb)
    big_m = [m_conv_w_in, m_conv_w_out, m_attn_w_qkv, m_attn_w_o, m_ffn_w_in, m_ffn_w_down]
    big_v = [v_conv_w_in, v_conv_w_out, v_attn_w_qkv, v_attn_w_o, v_ffn_w_in, v_ffn_w_down]
    big_names = ["conv_w_in", "conv_w_out", "attn_w_qkv", "attn_w_o", "ffn_w_in", "ffn_w_down"]
    qc_arr = jnp.stack([q, cq]).astype(jnp.int32)

    def adamw_of(n):
        return list(_reduce_adamw(own[n], mine[n], sib[n], big_w[n], big_m[n], big_v[n], qc_arr, f"adamw_{big_names[n]}"))

    big_upd = [None] * 6
    for n in (2, 3, 4, 5):
        big_upd[n] = adamw_of(n)
    token = scatter(group(0, "m"), 1, "start", scatter(group(0, "m"), 0, "wait", big_upd[5][1]))
    big_upd[5][1] = scatter(group(0, "m"), 1, "wait", token)
    for n in (0, 1):
        big_upd[n] = adamw_of(n)

    small = [jnp.stack(g_norm_mix), jnp.stack(g_norm_ffn), jnp.sum(dg_final, axis=0), jnp.stack(g_cconv),
             jnp.stack(g_bqkv), jnp.stack(g_sinks), jnp.stack(g_bo), jnp.stack(g_fconv)]
    sg = _unpack(_allreduce_small(_pack(small), "grad_small_allreduce"), small)

    def my_cols(v, like):
        width = like.shape[-1]
        return lax.dynamic_slice_in_dim(v, q * width, width, axis=v.ndim - 1)

    small_w = [norm_mix, norm_ffn, norm_final, conv_w_conv, attn_b_qkv, attn_sinks, attn_b_o, ffn_w_conv]
    small_m = [m_norm_mix, m_norm_ffn, m_norm_final, m_conv_w_conv, m_attn_b_qkv, m_attn_sinks, m_attn_b_o, m_ffn_w_conv]
    small_v = [v_norm_mix, v_norm_ffn, v_norm_final, v_conv_w_conv, v_attn_b_qkv, v_attn_sinks, v_attn_b_o, v_ffn_w_conv]
    small_g = [sg[0], sg[1], sg[2], my_cols(sg[3], conv_w_conv), my_cols(sg[4], attn_b_qkv), sg[5],
               my_cols(sg[6], attn_b_o), my_cols(sg[7], ffn_w_conv)]

    upd = {nm: tuple(u[1:]) for nm, u in zip(big_names, big_upd)}
    sd, sm, sv = _adamw(_pack(small_w)[None], _pack(small_g)[None], _pack(small_m)[None], _pack(small_v, 1.0)[None],
                        "adamw_small")
    sd, sm, sv = _unpack(sd, small_w), _unpack(sm, small_w), _unpack(sv, small_w)
    names = ["norm_mix", "norm_ffn", "norm_final", "conv_w_in", "conv_w_conv", "conv_w_out", "attn_w_qkv", "attn_b_qkv",
             "attn_sinks", "attn_w_o", "attn_b_o", "ffn_w_in", "ffn_w_conv", "ffn_w_down"]
    small_names = ["norm_mix", "norm_ffn", "norm_final", "conv_w_conv", "attn_b_qkv", "attn_sinks", "attn_b_o", "ffn_w_conv"]
    grads = dict(zip(small_names, small_g))
    grads.update({nm: u[0] for nm, u in zip(big_names, big_upd)})
    for n, nm in enumerate(small_names):
        upd[nm] = (sd[n], sm[n], sv[n])
    return (loss, grad_x, *[grads[nm] for nm in names], *[upd[nm][0] for nm in names],
            *[upd[nm][1] for nm in names], *[upd[nm][2] for nm in names])
```

```python
import math

import jax
import jax.numpy as jnp
from jax import lax
from jax.experimental import pallas as pl
from jax.experimental.pallas import tpu as pltpu

F32 = jnp.float32
BF16 = jnp.bfloat16

HEAD_DIM = 64
GROUP = 4
WINDOW = 128
EPS = 1e-5
ROPE_THETA = 10000.0
ADAM_LR, ADAM_B1, ADAM_B2, ADAM_EPS, ADAM_WD, ADAM_STEP = 0.001, 0.9, 0.999, 1e-08, 0.01, 10

N_CHIPS = 4
N_CORES = 2
N_DEV = 8
HALO = 16
VMEM_LIMIT_BYTES = 56 * 1024 * 1024
MESH = pl.DeviceIdType.MESH
ANY = pl.BlockSpec(memory_space=pl.ANY)
SMEM = pl.BlockSpec(memory_space=pltpu.SMEM)
NEG = float(jnp.finfo(jnp.float32).min)
ROW_TILES = (512, 256, 128, 64, 32, 16, 8)


def _pick(dim, cands):
    for c in cands:
        if dim % c == 0:
            return c
    return dim


def _params(sem):
    return pltpu.CompilerParams(dimension_semantics=sem, vmem_limit_bytes=VMEM_LIMIT_BYTES)


_DIMS = {"nn": (((1,), (0,)), ((), ())), "nt": (((1,), (1,)), ((), ())), "tn": (((0,), (0,)), ((), ()))}


def _mm(a, b, mode, out_dtype, *, layer=None, bias=None, residual=None, norm_g=None, n_outer=False, tm, tn, tk, name):
    b2 = b.shape[1:] if layer is not None else b.shape
    if mode == "nn":
        (m, k), n = a.shape, b2[1]
    elif mode == "nt":
        (m, k), n = a.shape, b2[0]
    else:
        (k, m), n = a.shape, b2[1]
    tm, tn, tk = min(tm, m), min(tn, n), min(tk, k)
    assert m % tm == 0 and n % tn == 0 and k % tk == 0, (name, a.shape, b.shape, tm, tn, tk)
    nk = k // tk

    def at(f):
        return (lambda p0, p1, p2: f(p1, p0, p2)) if n_outer else f

    a_spec = pl.BlockSpec((tk, tm), at(lambda i, j, l: (l, i))) if mode == "tn" else pl.BlockSpec((tm, tk), at(lambda i, j, l: (i, l)))
    if layer is None:
        b_spec = (pl.BlockSpec((tn, tk), at(lambda i, j, l: (j, l))) if mode == "nt"
                  else pl.BlockSpec((tk, tn), at(lambda i, j, l: (l, j))))
    elif mode == "nt":
        b_spec = pl.BlockSpec((None, tn, tk), at(lambda i, j, l: (layer, j, l)))
    else:
        b_spec = pl.BlockSpec((None, tk, tn), at(lambda i, j, l: (layer, l, j)))
    in_specs, args = [a_spec, b_spec], [a, b]
    if bias is not None:
        in_specs.append(pl.BlockSpec((1, tn), at(lambda i, j, l: (0, j))))
        args.append(bias)
    if residual is not None:
        in_specs.append(pl.BlockSpec((tm, tn), at(lambda i, j, l: (i, j))))
        args.append(residual)
    if norm_g is not None:
        assert tn == n, (name, "the RMSNorm of the result needs whole rows in a tile")
        in_specs.append(pl.BlockSpec((1, tn), at(lambda i, j, l: (0, j))))
        args.append(norm_g)
    has_bias, has_res, has_norm = bias is not None, residual is not None, norm_g is not None

    def body(*refs):
        a_ref, b_ref = refs[0], refs[1]
        pos = 2
        bias_ref = res_ref = g_ref = h_ref = None
        if has_bias:
            bias_ref, pos = refs[pos], pos + 1
        if has_res:
            res_ref, pos = refs[pos], pos + 1
        if has_norm:
            g_ref, pos = refs[pos], pos + 1
        o_ref, pos = refs[pos], pos + 1
        if has_norm:
            h_ref, pos = refs[pos], pos + 1
        acc_ref = refs[pos] if nk > 1 else None

        def finish(acc):
            if has_bias:
                acc = acc + bias_ref[...]
            if has_res:
                acc = acc + res_ref[...]
            o_ref[...] = acc.astype(o_ref.dtype)
            if has_norm:
                h_ref[...] = _rms(acc, g_ref[...]).astype(BF16)

        if nk == 1:
            finish(lax.dot_general(a_ref[...], b_ref[...], _DIMS[mode], preferred_element_type=F32))
            return
        l = pl.program_id(2)
        part = lax.dot_general(a_ref[...], b_ref[...], _DIMS[mode], preferred_element_type=F32)

        @pl.when(l == 0)
        def _():
            acc_ref[...] = part

        @pl.when(l > 0)
        def _():
            acc_ref[...] += part

        @pl.when(l == nk - 1)
        def _():
            finish(acc_ref[...])

    o_spec = pl.BlockSpec((tm, tn), at(lambda i, j, l: (i, j)))
    o_shape = jax.ShapeDtypeStruct((m, n), out_dtype)
    return pl.pallas_call(
        body,
        name=name,
        grid=(n // tn, m // tm, nk) if n_outer else (m // tm, n // tn, nk),
        in_specs=in_specs,
        out_specs=[o_spec, o_spec] if has_norm else o_spec,
        out_shape=[o_shape, jax.ShapeDtypeStruct((m, n), BF16)] if has_norm else o_shape,
        scratch_shapes=[pltpu.VMEM((tm, tn), F32)] if nk > 1 else [],
        compiler_params=_params(("parallel", "parallel", "arbitrary")),
    )(*args)


def _rms(x, g):
    return x * lax.rsqrt(jnp.mean(x * x, axis=-1, keepdims=True) + EPS) * g


def _fold8(v):
    r, d = v.shape
    return jnp.sum(v.reshape(r // 8, 8, d), axis=0)


def _rms_fwd(x, g, name):
    t, d = x.shape
    tm = _pick(t, ROW_TILES)

    def body(x_ref, g_ref, h_ref):
        xv = x_ref[...]
        r = lax.rsqrt(jnp.mean(xv * xv, axis=-1, keepdims=True) + EPS)
        h_ref[...] = (xv * r * g_ref[...]).astype(BF16)

    return pl.pallas_call(
        body,
        name=name,
        grid=(t // tm,),
        in_specs=[pl.BlockSpec((tm, d), lambda i: (i, 0)), pl.BlockSpec((1, d), lambda i: (0, 0))],
        out_specs=pl.BlockSpec((tm, d), lambda i: (i, 0)),
        out_shape=jax.ShapeDtypeStruct((t, d), BF16),
        compiler_params=_params(("parallel",)),
    )(x, g)


def _rms_bwd(x, dpre, w, layer, g, dx_in, name):
    t, d = x.shape
    k = dpre.shape[1]
    tm = _pick(t, ROW_TILES)

    sub = _pick(tm, (256, 128, 64, 32, 16, 8))

    def body(x_ref, dp_ref, w_ref, g_ref, dxi_ref, dx_ref, dxb_ref, dg_ref, cs_ref):
        i = pl.program_id(0)

        @pl.when(i == 0)
        def _():
            dg_ref[...] = jnp.zeros_like(dg_ref)
            cs_ref[...] = jnp.zeros_like(cs_ref)

        dg = jnp.zeros((8, d), F32)
        cs = jnp.zeros((8, d), F32)
        for r0 in range(0, tm, sub):
            rows = slice(r0, r0 + sub)
            xv = x_ref[rows, :]
            r = lax.rsqrt(jnp.mean(xv * xv, axis=-1, keepdims=True) + EPS)
            xhat = xv * r
            dy = lax.dot_general(dp_ref[rows, :], w_ref[...], _DIMS["nt"], preferred_element_type=F32)
            gdy = dy * g_ref[...]
            dx = dxi_ref[rows, :] + r * (gdy - xhat * jnp.mean(gdy * xhat, axis=-1, keepdims=True))
            dx_ref[rows, :] = dx
            dxb_ref[rows, :] = dx.astype(BF16)
            dg += _fold8(dy * xhat)
            cs += _fold8(dx)
        dg_ref[...] += dg
        cs_ref[...] += cs

    row = pl.BlockSpec((tm, d), lambda i: (i, 0))
    acc = pl.BlockSpec((8, d), lambda i: (0, 0))
    w_spec = pl.BlockSpec((None, d, k), lambda i: (layer, 0, 0), pipeline_mode=pl.Buffered(1))
    return pl.pallas_call(
        body,
        name=name,
        grid=(t // tm,),
        in_specs=[row, pl.BlockSpec((tm, k), lambda i: (i, 0)), w_spec, pl.BlockSpec((1, d), lambda i: (0, 0)), row],
        out_specs=[row, row, acc, acc],
        out_shape=[jax.ShapeDtypeStruct((t, d), F32), jax.ShapeDtypeStruct((t, d), BF16),
                   jax.ShapeDtypeStruct((8, d), F32), jax.ShapeDtypeStruct((8, d), F32)],
        compiler_params=_params(("arbitrary",)),
    )(x, dpre, w, g, dx_in)


def _loss_head(x, target, g, name):
    t, d = x.shape
    tm = _pick(t, ROW_TILES)
    inv_d = 1.0 / d

    def body(x_ref, t_ref, g_ref, dx_ref, dxb_ref, sq_ref, dg_ref):
        i = pl.program_id(0)
        xv = x_ref[...]
        gv = g_ref[...]
        r = lax.rsqrt(jnp.mean(xv * xv, axis=-1, keepdims=True) + EPS)
        xhat = xv * r
        err = xhat * gv - t_ref[...]
        dy = err * inv_d
        gdy = dy * gv
        dx = r * (gdy - xhat * jnp.mean(gdy * xhat, axis=-1, keepdims=True))
        dx_ref[...] = dx
        dxb_ref[...] = dx.astype(BF16)

        @pl.when(i == 0)
        def _():
            sq_ref[...] = jnp.zeros_like(sq_ref)
            dg_ref[...] = jnp.zeros_like(dg_ref)

        sq_ref[...] += _fold8(err * err)
        dg_ref[...] += _fold8(dy * xhat)

    row = pl.BlockSpec((tm, d), lambda i: (i, 0))
    acc = pl.BlockSpec((8, d), lambda i: (0, 0))
    return pl.pallas_call(
        body,
        name=name,
        grid=(t // tm,),
        in_specs=[row, row, pl.BlockSpec((1, d), lambda i: (0, 0))],
        out_specs=[row, row, acc, acc],
        out_shape=[jax.ShapeDtypeStruct((t, d), F32), jax.ShapeDtypeStruct((t, d), BF16),
                   jax.ShapeDtypeStruct((8, d), F32), jax.ShapeDtypeStruct((8, d), F32)],
        compiler_params=_params(("arbitrary",)),
    )(x, target, g)


def _rows(tm):
    return lax.broadcasted_iota(jnp.int32, (tm, 1), 0)


def _shift_down(u, before2):
    r8 = _rows(8)
    s1, s2 = pltpu.roll(u, 1, 0), pltpu.roll(u, 2, 0)
    top1 = jnp.where(r8 == 0, before2[1:2], s1[:8])
    top2 = jnp.where(r8 == 0, before2[0:1], jnp.where(r8 == 1, before2[1:2], s2[:8]))
    return jnp.concatenate([top1, s1[8:]], axis=0), jnp.concatenate([top2, s2[8:]], axis=0)


def _shift_up(u, after2):
    tm = u.shape[0]
    r8 = _rows(8)
    s1, s2 = pltpu.roll(u, tm - 1, 0), pltpu.roll(u, tm - 2, 0)
    bot1 = jnp.where(r8 == 7, after2[0:1], s1[tm - 8:])
    bot2 = jnp.where(r8 == 6, after2[0:1], jnp.where(r8 == 7, after2[1:2], s2[tm - 8:]))
    return jnp.concatenate([s1[:tm - 8], bot1], axis=0), jnp.concatenate([s2[:tm - 8], bot2], axis=0)


def _shift_matrix(tm, up):
    r = lax.broadcasted_iota(jnp.int32, (2 * tm, tm), 0)
    c = lax.broadcasted_iota(jnp.int32, (2 * tm, tm), 1)
    t = jnp.where(r >= tm, r - tm, r)
    k = jnp.where(r >= tm, 2, 1)
    return (c == (t + k if up else t - k)).astype(BF16)


def _shift_down_mxu(u, before2):
    tm = u.shape[0]
    moved = jnp.dot(_shift_matrix(tm, False), u.astype(BF16), preferred_element_type=F32)
    r8 = _rows(8)
    s1, s2 = moved[:tm], moved[tm:]
    top1 = s1[:8] + jnp.where(r8 == 0, before2[1:2], 0.0)
    top2 = s2[:8] + jnp.where(r8 == 0, before2[0:1], jnp.where(r8 == 1, before2[1:2], 0.0))
    return jnp.concatenate([top1, s1[8:]], axis=0), jnp.concatenate([top2, s2[8:]], axis=0)


def _shift_up_mxu(u, after2):
    tm = u.shape[0]
    moved = jnp.dot(_shift_matrix(tm, True), u.astype(BF16), preferred_element_type=F32)
    r8 = _rows(8)
    s1, s2 = moved[:tm], moved[tm:]
    bot1 = s1[tm - 8:] + jnp.where(r8 == 7, after2[0:1], 0.0)
    bot2 = s2[tm - 8:] + jnp.where(r8 == 6, after2[0:1], jnp.where(r8 == 7, after2[1:2], 0.0))
    return jnp.concatenate([s1[:tm - 8], bot1], axis=0), jnp.concatenate([s2[:tm - 8], bot2], axis=0)


def _conv_tile(seq):
    return _pick(seq, (256, 128, 64, 32, 16, 8))


def _halo_specs(tm, width, n_tiles):
    per = tm // HALO
    before = pl.BlockSpec((HALO, width), lambda i: (jnp.maximum(i * per - 1, 0), 0))
    after = pl.BlockSpec((HALO, width), lambda i: (jnp.minimum((i + 1) * per, n_tiles * per - 1), 0))
    return before, after


def _convgate_fwd(bcv, w, seq, name):
    t, d3 = bcv.shape
    d = d3 // 3
    tm = _conv_tile(seq)
    tps = seq // tm
    before, _ = _halo_specs(tm, d3, t // tm)

    def body(x_ref, xb_ref, w_ref, y_ref):
        i = pl.program_id(0)
        inner = (i % tps != 0).astype(F32)
        u = x_ref[:, d:2 * d].astype(F32) * x_ref[:, 2 * d:].astype(F32)
        m1, m2 = _shift_down_mxu(x_ref[:, d:], xb_ref[:, d:].astype(F32)[HALO - 2:] * inner)
        s1, s2 = m1[:, :d] * m1[:, d:], m2[:, :d] * m2[:, d:]
        z = w_ref[2:3] * u + w_ref[1:2] * s1 + w_ref[0:1] * s2
        y_ref[...] = (x_ref[:, :d].astype(F32) * z).astype(BF16)

    return pl.pallas_call(
        body,
        name=name,
        grid=(t // tm,),
        in_specs=[pl.BlockSpec((tm, d3), lambda i: (i, 0)), before, pl.BlockSpec((3, d), lambda i: (0, 0))],
        out_specs=pl.BlockSpec((tm, d), lambda i: (i, 0)),
        out_shape=jax.ShapeDtypeStruct((t, d), BF16),
        compiler_params=_params(("parallel",)),
    )(bcv, bcv, w)


def _convgate_bwd(bcv, dy, w, seq, name):
    t, d3 = bcv.shape
    d = d3 // 3
    tm = _conv_tile(seq)
    tps = seq // tm
    before, after = _halo_specs(tm, d3, t // tm)
    _, after_dy = _halo_specs(tm, d, t // tm)

    def body(x_ref, xb_ref, xa_ref, dy_ref, dya_ref, w_ref, dx_ref, dw_ref):
        i = pl.program_id(0)
        inner_lo = (i % tps != 0).astype(F32)
        inner_hi = (i % tps != tps - 1).astype(F32)
        w0, w1, w2 = w_ref[0:1], w_ref[1:2], w_ref[2:3]
        b, c, v = x_ref[:, :d].astype(F32), x_ref[:, d:2 * d].astype(F32), x_ref[:, 2 * d:].astype(F32)
        u = c * v
        m1, m2 = _shift_down_mxu(x_ref[:, d:], xb_ref[:, d:].astype(F32)[HALO - 2:] * inner_lo)
        s1, s2 = m1[:, :d] * m1[:, d:], m2[:, :d] * m2[:, d:]
        z = w2 * u + w1 * s1 + w0 * s2
        dyv = dy_ref[...].astype(F32)
        dz = dyv * b
        dza = dya_ref[...].astype(F32)[0:2] * xa_ref[:, :d].astype(F32)[0:2] * inner_hi
        n1, n2 = _shift_up(dz, dza)
        du = w2 * dz + w1 * n1 + w0 * n2
        dx_ref[:, :d] = (dyv * z).astype(BF16)
        dx_ref[:, d:2 * d] = (du * v).astype(BF16)
        dx_ref[:, 2 * d:] = (du * c).astype(BF16)

        @pl.when(i == 0)
        def _():
            dw_ref[...] = jnp.zeros_like(dw_ref)

        dw_ref[0:1] += jnp.sum(dz * s2, axis=0, keepdims=True)
        dw_ref[1:2] += jnp.sum(dz * s1, axis=0, keepdims=True)
        dw_ref[2:3] += jnp.sum(dz * u, axis=0, keepdims=True)

    return pl.pallas_call(
        body,
        name=name,
        grid=(t // tm,),
        in_specs=[pl.BlockSpec((tm, d3), lambda i: (i, 0)), before, after,
                  pl.BlockSpec((tm, d), lambda i: (i, 0)), after_dy, pl.BlockSpec((3, d), lambda i: (0, 0))],
        out_specs=[pl.BlockSpec((tm, d3), lambda i: (i, 0)), pl.BlockSpec((8, d), lambda i: (0, 0))],
        out_shape=[jax.ShapeDtypeStruct((t, d3), BF16), jax.ShapeDtypeStruct((8, d), F32)],
        compiler_params=_params(("arbitrary",)),
    )(bcv, bcv, bcv, dy, dy, w)


def _sigmoid(x):
    return 1.0 / (1.0 + jnp.exp(-x))


def _ffn_gate_down_fwd(gu, w, w_down, layer, resid, norm_g, seq, name):
    t, f2 = gu.shape
    f = f2 // 2
    d = w_down.shape[2]
    sub = _conv_tile(seq)
    tm = _pick(seq, (2 * sub, sub))
    tps = seq // tm
    before, _ = _halo_specs(tm, f2, t // tm)

    def body(x_ref, xb_ref, w_ref, wd_ref, res_ref, g_ref, o_ref, a_ref, h_ref):
        i = pl.program_id(0)
        inner = (i % tps != 0).astype(F32)
        for r0 in range(0, tm, sub):
            rows = slice(r0, r0 + sub)
            if r0 == 0:
                halo = xb_ref[:, :f].astype(F32)[HALO - 2:] * inner
            else:
                halo = x_ref[r0 - HALO:r0, :f].astype(F32)[HALO - 2:]
            s1, s2 = _shift_down_mxu(x_ref[rows, :f], halo)
            gc = w_ref[2:3] * x_ref[rows, :f].astype(F32) + w_ref[1:2] * s1 + w_ref[0:1] * s2
            act = (gc * _sigmoid(gc) * x_ref[rows, f:].astype(F32)).astype(BF16)
            a_ref[rows, :] = act
            out = jnp.dot(act, wd_ref[...], preferred_element_type=F32) + res_ref[rows, :]
            o_ref[rows, :] = out
            h_ref[rows, :] = _rms(out, g_ref[...]).astype(BF16)

    row_d = pl.BlockSpec((tm, d), lambda i: (i, 0))
    return pl.pallas_call(
        body,
        name=name,
        grid=(t // tm,),
        in_specs=[pl.BlockSpec((tm, f2), lambda i: (i, 0)), before, pl.BlockSpec((3, f), lambda i: (0, 0)),
                  pl.BlockSpec((None, f, d), lambda i: (layer, 0, 0), pipeline_mode=pl.Buffered(1)),
                  row_d, pl.BlockSpec((1, d), lambda i: (0, 0))],
        out_specs=[row_d, pl.BlockSpec((tm, f), lambda i: (i, 0)), row_d],
        out_shape=[jax.ShapeDtypeStruct((t, d), F32), jax.ShapeDtypeStruct((t, f), BF16), jax.ShapeDtypeStruct((t, d), BF16)],
        compiler_params=_params(("parallel",)),
    )(gu, gu, w, w_down, resid, norm_g)


def _ffngate_bwd(gu, da, w, seq, name):
    t, f2 = gu.shape
    f = f2 // 2
    tm = _conv_tile(seq)
    tps = seq // tm
    before, after = _halo_specs(tm, f2, t // tm)
    _, after_da = _halo_specs(tm, f, t // tm)

    def body(x_ref, xb_ref, xa_ref, da_ref, daa_ref, w_ref, dx_ref, dw_ref):
        i = pl.program_id(0)
        inner_lo = (i % tps != 0).astype(F32)
        inner_hi = (i % tps != tps - 1).astype(F32)
        w0, w1, w2 = w_ref[0:1], w_ref[1:2], w_ref[2:3]

        def dgate(gc, uv, dav):
            sg = _sigmoid(gc)
            return dav * uv * (sg * (1.0 + gc * (1.0 - sg))), dav * (gc * sg)

        g, u = x_ref[:, :f].astype(F32), x_ref[:, f:].astype(F32)
        s1, s2 = _shift_down_mxu(x_ref[:, :f], xb_ref[:, :f].astype(F32)[HALO - 2:] * inner_lo)
        gc = w2 * g + w1 * s1 + w0 * s2
        dgc, du = dgate(gc, u, da_ref[...].astype(F32))
        ga = xa_ref[:, :f].astype(F32)
        a1, a2 = _shift_down(ga, x_ref[tm - HALO:, :f].astype(F32)[HALO - 2:])
        gca = w2 * ga + w1 * a1 + w0 * a2
        dgca, _ = dgate(gca, xa_ref[:, f:].astype(F32), daa_ref[...].astype(F32))
        n1, n2 = _shift_up_mxu(dgc, dgca[0:2] * inner_hi)
        dx_ref[:, :f] = (w2 * dgc + w1 * n1 + w0 * n2).astype(BF16)
        dx_ref[:, f:] = du.astype(BF16)

        @pl.when(i == 0)
        def _():
            dw_ref[...] = jnp.zeros_like(dw_ref)

        dw_ref[0:1] += jnp.sum(dgc * s2, axis=0, keepdims=True)
        dw_ref[1:2] += jnp.sum(dgc * s1, axis=0, keepdims=True)
        dw_ref[2:3] += jnp.sum(dgc * g, axis=0, keepdims=True)

    return pl.pallas_call(
        body,
        name=name,
        grid=(t // tm,),
        in_specs=[pl.BlockSpec((tm, f2), lambda i: (i, 0)), before, after,
                  pl.BlockSpec((tm, f), lambda i: (i, 0)), after_da, pl.BlockSpec((3, f), lambda i: (0, 0))],
        out_specs=[pl.BlockSpec((tm, f2), lambda i: (i, 0)), pl.BlockSpec((8, f), lambda i: (0, 0))],
        out_shape=[jax.ShapeDtypeStruct((t, f2), BF16), jax.ShapeDtypeStruct((8, f), F32)],
        compiler_params=_params(("arbitrary",)),
    )(gu, gu, gu, da, da, w)


def _swap_halves(xt):
    half = HEAD_DIM // 2
    return jnp.concatenate([xt[half:], xt[:half]], axis=0)


def _rope(xt, cos, sin):
    return xt * cos + _swap_halves(xt) * sin


def _unrope(dxt, cos, sin):
    return dxt * cos - _swap_halves(dxt) * sin


def _key_query(count):
    kj = lax.broadcasted_iota(jnp.int32, (WINDOW, count * WINDOW), 0)
    qi = lax.broadcasted_iota(jnp.int32, (WINDOW, count * WINDOW), 1) & (WINDOW - 1)
    return kj, qi


def _band_masks(n, count):
    kj, qi = _key_query(count)
    return kj <= qi, jnp.logical_and(kj > qi, n > 0)


def _lanes(v, count):
    return jnp.concatenate([v] * count, axis=1) if count > 1 else v


def _heads(ref, h0, count):
    parts = [ref[(h0 + g) * HEAD_DIM:(h0 + g + 1) * HEAD_DIM, :] for g in range(count)]
    return jnp.concatenate(parts, axis=1) if count > 1 else parts[0]


def _head_rows(ref, h0, count):
    parts = [ref[h0 + g:h0 + g + 1, :] for g in range(count)]
    return jnp.concatenate(parts, axis=1) if count > 1 else parts[0]


def _head_sinks(sink_ref, h0, count):
    parts = [jnp.full((1, WINDOW), sink_ref[h0 + g], F32) for g in range(count)]
    return jnp.concatenate(parts, axis=1) if count > 1 else parts[0]


def _tn(a, b):
    return lax.dot_general(a, b, _DIMS["tn"], preferred_element_type=F32)


def _nt(a, b):
    return lax.dot_general(a, b, _DIMS["nt"], preferred_element_type=F32)


def _nn(a, b):
    return jnp.dot(a, b, preferred_element_type=F32)


def _attn_fwd(qkv, sinks, cos_t, sin_t, bsz, seq, name, hp):
    t, qw = qkv.shape
    d = qw * 2 // 3
    kvw = d // GROUP
    n_heads, n_kv = d // HEAD_DIM, kvw // HEAD_DIM
    nb = seq // WINDOW
    scale = HEAD_DIM ** -0.5

    def body(sink_ref, xc_ref, xp_ref, cc_ref, sc_ref, cp_ref, sp_ref, o_ref, lse_ref, xt_ref, pt_ref, ot_ref):
        n = pl.program_id(1)
        xt_ref[...] = xc_ref[...].T
        pt_ref[...] = xp_ref[:, d:].T
        cos_c, sin_c, cos_p, sin_p = cc_ref[...], sc_ref[...], cp_ref[...], sp_ref[...]
        cos_g, sin_g = _lanes(cos_c, hp), _lanes(sin_c, hp)
        valid_c, valid_p = _band_masks(n, hp)
        for j in range(n_kv):
            ko = j * HEAD_DIM
            kc = _rope(xt_ref[d + ko:d + ko + HEAD_DIM, :], cos_c, sin_c).astype(BF16)
            kp = _rope(pt_ref[ko:ko + HEAD_DIM, :], cos_p, sin_p).astype(BF16)
            vc = xt_ref[d + kvw + ko:d + kvw + ko + HEAD_DIM, :].astype(BF16)
            vp = pt_ref[kvw + ko:kvw + ko + HEAD_DIM, :].astype(BF16)
            for h0 in range(j * GROUP, (j + 1) * GROUP, hp):
                q = _rope(_heads(xt_ref, h0, hp), cos_g, sin_g).astype(BF16)
                sink = _head_sinks(sink_ref, h0, hp)
                s_c = jnp.where(valid_c, _tn(kc, q) * scale, NEG)
                s_p = jnp.where(valid_p, _tn(kp, q) * scale, NEG)
                m = jnp.maximum(jnp.maximum(jnp.max(s_c, axis=0, keepdims=True), jnp.max(s_p, axis=0, keepdims=True)), sink)
                p_c = jnp.exp(s_c - m)
                p_p = jnp.exp(s_p - m)
                den = jnp.sum(p_c, axis=0, keepdims=True) + jnp.sum(p_p, axis=0, keepdims=True) + jnp.exp(sink - m)
                inv = 1.0 / den
                o_g = _nn(vc, (p_c * inv).astype(BF16)) + _nn(vp, (p_p * inv).astype(BF16))
                lse_g = m + jnp.log(den)
                for g in range(hp):
                    h = h0 + g
                    ot_ref[h * HEAD_DIM:(h + 1) * HEAD_DIM, :] = o_g[:, g * WINDOW:(g + 1) * WINDOW]
                    lse_ref[h:h + 1, :] = lse_g[:, g * WINDOW:(g + 1) * WINDOW]
        o_ref[...] = ot_ref[...].T.astype(BF16)

    cur = lambda b, n: (b * nb + n, 0)
    prev = lambda b, n: (b * nb + jnp.maximum(n - 1, 0), 0)
    tab_c = pl.BlockSpec((HEAD_DIM, WINDOW), lambda b, n: (0, n))
    tab_p = pl.BlockSpec((HEAD_DIM, WINDOW), lambda b, n: (0, jnp.maximum(n - 1, 0)))
    return pl.pallas_call(
        body,
        name=name,
        grid=(bsz, nb),
        in_specs=[SMEM, pl.BlockSpec((WINDOW, qw), cur), pl.BlockSpec((WINDOW, qw), prev), tab_c, tab_c, tab_p, tab_p],
        out_specs=[pl.BlockSpec((WINDOW, d), cur), pl.BlockSpec((n_heads, WINDOW), lambda b, n: (0, b * nb + n))],
        out_shape=[jax.ShapeDtypeStruct((t, d), BF16), jax.ShapeDtypeStruct((n_heads, t), F32)],
        scratch_shapes=[pltpu.VMEM((qw, WINDOW), F32), pltpu.VMEM((2 * kvw, WINDOW), F32), pltpu.VMEM((d, WINDOW), F32)],
        compiler_params=_params(("parallel", "arbitrary")),
    )(sinks, qkv, qkv, cos_t, sin_t, cos_t, sin_t)


def _attn_bwd(qkv, o, lse, do, sinks, cos_t, sin_t, bsz, seq, name, hp):
    t, qw = qkv.shape
    d = qw * 2 // 3
    kvw = d // GROUP
    n_heads, n_kv = d // HEAD_DIM, kvw // HEAD_DIM
    nb = seq // WINDOW
    scale = HEAD_DIM ** -0.5

    def body(sink_ref, xc_ref, xp_ref, oc_ref, doc_ref, lc_ref, cc_ref, sc_ref, cp_ref, sp_ref,
             dx_ref, db_ref, dsk_ref, xt_ref, pt_ref, otc_ref, dtc_ref, gt_ref, carry_ref):
        b, n = pl.program_id(0), pl.program_id(1)
        live = n < nb
        xt_ref[...] = xc_ref[...].T
        pt_ref[...] = xp_ref[:, d:].T
        otc_ref[...] = oc_ref[...].astype(F32).T
        dtc_ref[...] = doc_ref[...].T
        cos_c, sin_c, cos_p, sin_p = cc_ref[...], sc_ref[...], cp_ref[...], sp_ref[...]
        cos_g, sin_g = _lanes(cos_c, hp), _lanes(sin_c, hp)
        kj, qi = _key_query(hp)
        valid_c = jnp.logical_and(kj <= qi, live)
        valid_p = jnp.logical_and(kj > qi, jnp.logical_and(n > 0, live))

        @pl.when(jnp.logical_and(b == 0, n == 0))
        def _():
            db_ref[...] = jnp.zeros_like(db_ref)
            dsk_ref[...] = jnp.zeros_like(dsk_ref)

        @pl.when(n == 0)
        def _():
            carry_ref[...] = jnp.zeros_like(carry_ref)

        for j in range(n_kv):
            ko = j * HEAD_DIM
            k_rows = slice(d + ko, d + ko + HEAD_DIM)
            v_rows = slice(d + kvw + ko, d + kvw + ko + HEAD_DIM)
            kc = _rope(xt_ref[k_rows, :], cos_c, sin_c).astype(BF16)
            kp = _rope(pt_ref[ko:ko + HEAD_DIM, :], cos_p, sin_p).astype(BF16)
            vc = xt_ref[v_rows, :].astype(BF16)
            vp = pt_ref[kvw + ko:kvw + ko + HEAD_DIM, :].astype(BF16)
            dk_c = jnp.zeros((HEAD_DIM, WINDOW), F32)
            dv_c = jnp.zeros((HEAD_DIM, WINDOW), F32)
            dk_p = jnp.zeros((HEAD_DIM, WINDOW), F32)
            dv_p = jnp.zeros((HEAD_DIM, WINDOW), F32)
            for h0 in range(j * GROUP, (j + 1) * GROUP, hp):
                q = _rope(_heads(xt_ref, h0, hp), cos_g, sin_g).astype(BF16)
                do_g = _heads(dtc_ref, h0, hp)
                do_b = do_g.astype(BF16)
                lse_g = _head_rows(lc_ref, h0, hp)
                delta = jnp.sum(_heads(otc_ref, h0, hp) * do_g, axis=0, keepdims=True)
                p_c = jnp.exp(jnp.where(valid_c, _tn(kc, q) * scale, NEG) - lse_g)
                p_p = jnp.exp(jnp.where(valid_p, _tn(kp, q) * scale, NEG) - lse_g)
                ds_c = (p_c * (_tn(vc, do_b) - delta)).astype(BF16)
                ds_p = (p_p * (_tn(vp, do_b) - delta)).astype(BF16)
                dq = _unrope((_nn(kc, ds_c) + _nn(kp, ds_p)) * scale, cos_g, sin_g)
                dsk = jnp.where(live, -jnp.exp(_head_sinks(sink_ref, h0, hp) - lse_g) * delta, 0.0)
                for g in range(hp):
                    rows = slice((h0 + g) * HEAD_DIM, (h0 + g + 1) * HEAD_DIM)
                    gt_ref[rows, :] = carry_ref[rows, :]
                    carry_ref[rows, :] = dq[:, g * WINDOW:(g + 1) * WINDOW]
                    dsk_ref[h0 + g:h0 + g + 1, :] += dsk[:, g * WINDOW:(g + 1) * WINDOW]
                dv_c += _nt(do_b, p_c.astype(BF16))
                dk_c += _nt(q, ds_c)
                dv_p += _nt(do_b, p_p.astype(BF16))
                dk_p += _nt(q, ds_p)
            gt_ref[k_rows, :] = _unrope((carry_ref[k_rows, :] + dk_p) * scale, cos_p, sin_p)
            gt_ref[v_rows, :] = carry_ref[v_rows, :] + dv_p
            carry_ref[k_rows, :] = dk_c
            carry_ref[v_rows, :] = dv_c
        dx = gt_ref[...].T
        dx_ref[...] = dx.astype(BF16)
        db_ref[...] += _fold8(dx)

    cur = lambda b, n: (b * nb + jnp.minimum(n, nb - 1), 0)
    prev = lambda b, n: (b * nb + jnp.maximum(jnp.minimum(n, nb - 1) - 1, 0), 0)
    done = lambda b, n: (b * nb + jnp.maximum(n - 1, 0), 0)
    stat_c = pl.BlockSpec((n_heads, WINDOW), lambda b, n: (0, b * nb + jnp.minimum(n, nb - 1)))
    tab_c = pl.BlockSpec((HEAD_DIM, WINDOW), lambda b, n: (0, jnp.minimum(n, nb - 1)))
    tab_p = pl.BlockSpec((HEAD_DIM, WINDOW), lambda b, n: (0, jnp.maximum(n - 1, 0)))
    return pl.pallas_call(
        body,
        name=name,
        grid=(bsz, nb + 1),
        in_specs=[SMEM, pl.BlockSpec((WINDOW, qw), cur), pl.BlockSpec((WINDOW, qw), prev),
                  pl.BlockSpec((WINDOW, d), cur), pl.BlockSpec((WINDOW, d), cur), stat_c, tab_c, tab_c, tab_p, tab_p],
        out_specs=[pl.BlockSpec((WINDOW, qw), done), pl.BlockSpec((8, qw), lambda b, n: (0, 0)),
                   pl.BlockSpec((n_heads, WINDOW), lambda b, n: (0, 0))],
        out_shape=[jax.ShapeDtypeStruct((t, qw), BF16), jax.ShapeDtypeStruct((8, qw), F32),
                   jax.ShapeDtypeStruct((n_heads, WINDOW), F32)],
        scratch_shapes=[pltpu.VMEM((qw, WINDOW), F32), pltpu.VMEM((2 * kvw, WINDOW), F32), pltpu.VMEM((d, WINDOW), F32),
                        pltpu.VMEM((d, WINDOW), F32), pltpu.VMEM((qw, WINDOW), F32), pltpu.VMEM((qw, WINDOW), F32)],
        compiler_params=_params(("arbitrary", "arbitrary")),
    )(sinks, qkv, qkv, o, do, lse, cos_t, sin_t, cos_t, sin_t)


def _place():
    return lax.axis_index("x"), lax.axis_index("y"), lax.axis_index("c")


def _other_chips(x, y):
    return [(1 - x, y), (x, 1 - y), (1 - x, 1 - y)]


def _place_shard(w, axis, q, name):
    ly, k, n = w.shape
    tr = _pick(k, (256, 128, 64, 32, 16, 8))
    steps = k // tr
    shape = (ly, k * N_CHIPS, n) if axis == 1 else (ly, k, n * N_CHIPS)
    if axis == 1:
        out_spec = pl.BlockSpec((None, tr, n), lambda l, i, q_ref: (l, q_ref[0] * steps + i, 0))
    else:
        out_spec = pl.BlockSpec((None, tr, n), lambda l, i, q_ref: (l, i, q_ref[0]))

    def body(q_ref, w_ref, o_ref):
        del q_ref
        o_ref[...] = w_ref[...].astype(BF16)

    return pl.pallas_call(
        body,
        name=name,
        grid_spec=pltpu.PrefetchScalarGridSpec(
            num_scalar_prefetch=1, grid=(ly, steps),
            in_specs=[pl.BlockSpec((None, tr, n), lambda l, i, q_ref: (l, i, 0))], out_specs=out_spec),
        out_shape=jax.ShapeDtypeStruct(shape, BF16),
        compiler_params=_params(("parallel", "parallel")),
    )(q, w)


def _half_block(ref, axis, layer, px, py, pc):
    blk = 2 * px + py
    if axis == 1:
        rows = ref.shape[1] // (2 * N_CHIPS)
        return ref.at[layer, pl.ds(pl.multiple_of((2 * blk + pc) * rows, 8), rows), :]
    rows, width = ref.shape[1] // 2, ref.shape[2] // N_CHIPS
    return ref.at[layer, pl.ds(pl.multiple_of(pc * rows, 8), rows), pl.ds(pl.multiple_of(blk * width, 128), width)]


def _gather_copy(refs, axes, pieces, send_sems, recv_sems, p, k, stage, whose):
    x, y, c = _place()
    chip = _other_chips(x, y)[k]
    i, layer = pieces[p]
    if stage == 0:
        origin = (x, y, c) if whose == "mine" else (*chip, c)
        to = (*chip, c)
    else:
        origin = (*chip, c) if whose == "mine" else (*chip, 1 - c)
        to = (x, y, 1 - c)
    blk = _half_block(refs[i], axes[i], layer, *origin)
    return pltpu.make_async_remote_copy(src_ref=blk, dst_ref=blk, send_sem=send_sems.at[p * 3 + k],
                                        recv_sem=recv_sems.at[p * 3 + k], device_id=to, device_id_type=MESH)


HBM_SPEC = pl.BlockSpec(memory_space=pltpu.HBM)
SEM_SPEC = pl.BlockSpec(memory_space=pltpu.SEMAPHORE)


def _gather_start(fulls, axes, pieces, stage, name):
    n, m = len(fulls), 3 * len(pieces)

    def body(*refs):
        src = refs[:n]
        send_sems, recv_sems = refs[2 * n], refs[2 * n + 1]
        for p in range(len(pieces)):
            for k in range(3):
                _gather_copy(src, axes, pieces, send_sems, recv_sems, p, k, stage, "mine").start()

    out = pl.pallas_call(
        body,
        name=name,
        in_specs=[HBM_SPEC] * n,
        out_specs=[HBM_SPEC] * n + [SEM_SPEC, SEM_SPEC],
        out_shape=[pltpu.HBM(f.shape, f.dtype) for f in fulls] + [pltpu.SemaphoreType.DMA((m,)), pltpu.SemaphoreType.DMA((m,))],
        input_output_aliases={i: i for i in range(n)},
        compiler_params=pltpu.CompilerParams(has_side_effects=pltpu.SideEffectType.DATAFLOW_SIDE_EFFECTING),
    )(*[pltpu.with_memory_space_constraint(f, pltpu.HBM) for f in fulls])
    return list(out[:n]), out[n], out[n + 1]


def _gather_wait(fulls, send_sems, recv_sems, after, axes, pieces, stage, name):
    n = len(fulls)

    def body(*refs):
        src = refs[:n]
        s_sems, r_sems = refs[n], refs[n + 1]
        for p in range(len(pieces)):
            for k in range(3):
                _gather_copy(src, axes, pieces, s_sems, r_sems, p, k, stage, "mine").wait_send()
                _gather_copy(src, axes, pieces, s_sems, r_sems, p, k, stage, "theirs").wait_recv()

    out = pl.pallas_call(
        body,
        name=name,
        in_specs=[HBM_SPEC] * n + [SEM_SPEC, SEM_SPEC, ANY],
        out_specs=[HBM_SPEC] * n,
        out_shape=[pltpu.HBM(f.shape, f.dtype) for f in fulls],
        input_output_aliases={i: i for i in range(n)},
        compiler_params=pltpu.CompilerParams(has_side_effects=pltpu.SideEffectType.DATAFLOW_SIDE_EFFECTING),
    )(*fulls, send_sems, recv_sems, after)
    return list(out)


def _half_shape(kind, shape):
    if kind == "col":
        return (shape[0] // 2, shape[1])
    return (N_CHIPS, shape[1] // 2, shape[2])


def _half_of(kind, ref, h):
    if kind == "col":
        r = ref.shape[0] // 2
        return ref.at[pl.ds(pl.multiple_of(h * r, 8), r), :]
    r = ref.shape[1] // 2
    return ref.at[:, pl.ds(pl.multiple_of(h * r, 8), r), :]


def _pair_copy(src, dst, kinds, send_sems, recv_sems, i):
    x, y, c = _place()
    return pltpu.make_async_remote_copy(src_ref=_half_of(kinds[i], src[i], 1 - c), dst_ref=dst[i], send_sem=send_sems.at[i],
                                        recv_sem=recv_sems.at[i], device_id=(x, y, 1 - c), device_id_type=MESH)


def _pair_start(grads, kinds, token, name):
    n = len(grads)
    lands = [pltpu.HBM(_half_shape(kd, g.shape), g.dtype) for g, kd in zip(grads, kinds)]

    def body(*refs):
        src, dst = refs[:n], refs[2 * n + 2:3 * n + 2]
        send_sems, recv_sems = refs[3 * n + 2], refs[3 * n + 3]
        for i in range(n):
            _pair_copy(src, dst, kinds, send_sems, recv_sems, i).start()

    arrays = list(grads) + [token]
    out = pl.pallas_call(
        body,
        name=name,
        in_specs=[HBM_SPEC] * (n + 1),
        out_specs=[HBM_SPEC] * (2 * n + 1) + [SEM_SPEC, SEM_SPEC],
        out_shape=[pltpu.HBM(a.shape, a.dtype) for a in arrays] + lands + [pltpu.SemaphoreType.DMA((n,)), pltpu.SemaphoreType.DMA((n,))],
        input_output_aliases={i: i for i in range(n + 1)},
        compiler_params=pltpu.CompilerParams(has_side_effects=pltpu.SideEffectType.DATAFLOW_SIDE_EFFECTING),
    )(*[pltpu.with_memory_space_constraint(a, pltpu.HBM) for a in arrays])
    return list(out[:n]), out[n], list(out[n + 1:2 * n + 1]), out[2 * n + 1], out[2 * n + 2]


def _pair_wait(grads, kinds, token, lands, send_sems, recv_sems, name):
    n = len(grads)

    def body(*refs):
        src, dst = refs[:n], refs[n + 1:2 * n + 1]
        s_sems, r_sems = refs[2 * n + 1], refs[2 * n + 2]
        for i in range(n):
            cp = _pair_copy(src, dst, kinds, s_sems, r_sems, i)
            cp.wait_send()
            cp.wait_recv()

    arrays = list(grads) + [token] + list(lands)
    out = pl.pallas_call(
        body,
        name=name,
        in_specs=[HBM_SPEC] * (2 * n + 1) + [SEM_SPEC, SEM_SPEC],
        out_specs=[HBM_SPEC] * (2 * n + 1),
        out_shape=[pltpu.HBM(a.shape, a.dtype) for a in arrays],
        input_output_aliases={i: i for i in range(2 * n + 1)},
        compiler_params=pltpu.CompilerParams(has_side_effects=pltpu.SideEffectType.DATAFLOW_SIDE_EFFECTING),
    )(*arrays, send_sems, recv_sems)
    return list(out[:n]), out[n], list(out[n + 1:])


def _pair_sum(grad, recv, kind, c, own, layer, name):
    r, cols = own.shape[2:]
    if kind == "col":
        tr = _pick(r, (256, 128, 64, 32, 16, 8))
        steps = r // tr
        grid = (N_CHIPS, steps)
        g_spec = pl.BlockSpec((tr, cols), lambda s, i, c_ref: (c_ref[0] * steps + i, s))
        r_spec = pl.BlockSpec((tr, cols), lambda s, i, c_ref: (i, s))
        o_spec = pl.BlockSpec((None, None, tr, cols), lambda s, i, c_ref: (layer, s, i, 0))
        g_in = grad
    else:
        grid = (N_CHIPS, 1)
        g_spec = pl.BlockSpec((None, None, r, cols), lambda s, i, c_ref: (s, c_ref[0], 0, 0))
        r_spec = pl.BlockSpec((None, r, cols), lambda s, i, c_ref: (s, 0, 0))
        o_spec = pl.BlockSpec((None, None, r, cols), lambda s, i, c_ref: (layer, s, 0, 0))
        g_in = grad.reshape(N_CHIPS, 2, r, cols)

    def body(c_ref, g_ref, r_ref, own_ref, o_ref):
        del c_ref, own_ref
        o_ref[...] = (g_ref[...].astype(F32) + r_ref[...].astype(F32)).astype(o_ref.dtype)

    return pl.pallas_call(
        body,
        name=name,
        grid_spec=pltpu.PrefetchScalarGridSpec(num_scalar_prefetch=1, grid=grid, in_specs=[g_spec, r_spec, ANY], out_specs=o_spec),
        out_shape=jax.ShapeDtypeStruct(own.shape, own.dtype),
        input_output_aliases={3: 0},
        compiler_params=_params(("parallel", "parallel")),
    )(c, g_in, recv, own)


def _scatter_copy(own, mine, sib, pieces, send_sems, recv_sems, p, k, stage, whose):
    x, y, c = _place()
    q = 2 * x + y
    i, layer = pieces[p]
    per = 4 if stage == 0 else 3
    if k == 3:
        src, dst, to = own[i].at[layer, q], sib[i].at[layer, q], (x, y, 1 - c)
    else:
        chip = _other_chips(x, y)[k]
        slot = 2 * chip[0] + chip[1]
        if stage == 0:
            to = (*chip, c)
            src, dst = (own[i].at[layer, slot], mine[i].at[layer, q]) if whose == "mine" else (own[i].at[layer, q], mine[i].at[layer, slot])
        else:
            to = (x, y, 1 - c)
            src, dst = mine[i].at[layer, slot], sib[i].at[layer, slot]
    return pltpu.make_async_remote_copy(src_ref=src, dst_ref=dst, send_sem=send_sems.at[p * per + k],
                                        recv_sem=recv_sems.at[p * per + k], device_id=to, device_id_type=MESH)


def _scatter_start(own, mine, sib, token, pieces, stage, name):
    n = len(own)
    per = 4 if stage == 0 else 3
    m = per * len(pieces)
    n_arr = 3 * n + 1

    def body(*refs):
        o, mi, si = refs[:n], refs[n:2 * n], refs[2 * n:3 * n]
        send_sems, recv_sems = refs[2 * n_arr], refs[2 * n_arr + 1]
        for p in range(len(pieces)):
            for k in range(per):
                _scatter_copy(o, mi, si, pieces, send_sems, recv_sems, p, k, stage, "mine").start()

    arrays = list(own) + list(mine) + list(sib) + [token]
    out = pl.pallas_call(
        body,
        name=name,
        in_specs=[HBM_SPEC] * n_arr,
        out_specs=[HBM_SPEC] * n_arr + [SEM_SPEC, SEM_SPEC],
        out_shape=[pltpu.HBM(a.shape, a.dtype) for a in arrays] + [pltpu.SemaphoreType.DMA((m,)), pltpu.SemaphoreType.DMA((m,))],
        input_output_aliases={i: i for i in range(n_arr)},
        compiler_params=pltpu.CompilerParams(has_side_effects=pltpu.SideEffectType.DATAFLOW_SIDE_EFFECTING),
    )(*[pltpu.with_memory_space_constraint(a, pltpu.HBM) for a in arrays])
    return list(out[:n]), list(out[n:2 * n]), list(out[2 * n:3 * n]), out[3 * n], out[n_arr], out[n_arr + 1]


def _scatter_wait(own, mine, sib, token, send_sems, recv_sems, pieces, stage, name):
    n = len(own)
    per = 4 if stage == 0 else 3
    n_arr = 3 * n + 1

    def body(*refs):
        o, mi, si = refs[:n], refs[n:2 * n], refs[2 * n:3 * n]
        s_sems, r_sems = refs[n_arr], refs[n_arr + 1]
        for p in range(len(pieces)):
            for k in range(per):
                _scatter_copy(o, mi, si, pieces, s_sems, r_sems, p, k, stage, "mine").wait_send()
                _scatter_copy(o, mi, si, pieces, s_sems, r_sems, p, k, stage, "theirs").wait_recv()

    arrays = list(own) + list(mine) + list(sib) + [token]
    out = pl.pallas_call(
        body,
        name=name,
        in_specs=[HBM_SPEC] * n_arr + [SEM_SPEC, SEM_SPEC],
        out_specs=[HBM_SPEC] * n_arr,
        out_shape=[pltpu.HBM(a.shape, a.dtype) for a in arrays],
        input_output_aliases={i: i for i in range(n_arr)},
        compiler_params=pltpu.CompilerParams(has_side_effects=pltpu.SideEffectType.DATAFLOW_SIDE_EFFECTING),
    )(*arrays, send_sems, recv_sems)
    return list(out[:n]), list(out[n:2 * n]), list(out[2 * n:3 * n]), out[3 * n]


def _reduce_adamw(own, mine, sib, w, m, v, qc, name):
    ly, _, r, cols = mine.shape
    tr = _pick(r, (128, 64, 32, 16, 8))
    steps = r // tr
    c1 = 1.0 - ADAM_B1 ** ADAM_STEP
    c2 = 1.0 - ADAM_B2 ** ADAM_STEP

    def body(qc_ref, own_ref, mine_ref, sib_ref, w_ref, m_ref, v_ref, g_ref, d_ref, nm_ref, nv_ref):
        q = qc_ref[0]
        mine_sum = sib_sum = None
        for s in range(N_CHIPS):
            a = jnp.where(q == s, own_ref[...], mine_ref[s]).astype(F32)
            b = sib_ref[s].astype(F32)
            mine_sum = a if s == 0 else mine_sum + a
            sib_sum = b if s == 0 else sib_sum + b
        gv = jnp.where(pl.program_id(1) == qc_ref[1], mine_sum, sib_sum)
        nm = ADAM_B1 * m_ref[...] + (1.0 - ADAM_B1) * gv
        nv = ADAM_B2 * v_ref[...] + (1.0 - ADAM_B2) * (gv * gv)
        g_ref[...] = gv
        d_ref[...] = -ADAM_LR * ((nm / c1) / (jnp.sqrt(nv / c2) + ADAM_EPS) + ADAM_WD * w_ref[...])
        nm_ref[...] = nm
        nv_ref[...] = nv

    def mine_rows(h, i, qc_ref):
        return jnp.where(h == qc_ref[1], i, 0)

    def sib_rows(h, i, qc_ref):
        return jnp.where(h == qc_ref[1], 0, i)

    own_spec = pl.BlockSpec((None, None, tr, cols), lambda l, h, i, qc_ref: (l, qc_ref[0], mine_rows(h, i, qc_ref), 0))
    mine_spec = pl.BlockSpec((None, N_CHIPS, tr, cols), lambda l, h, i, qc_ref: (l, 0, mine_rows(h, i, qc_ref), 0))
    sib_spec = pl.BlockSpec((None, N_CHIPS, tr, cols), lambda l, h, i, qc_ref: (l, 0, sib_rows(h, i, qc_ref), 0))
    spec = pl.BlockSpec((None, tr, cols), lambda l, h, i, qc_ref: (l, h * steps + i, 0))
    shp = jax.ShapeDtypeStruct(w.shape, F32)
    return pl.pallas_call(
        body,
        name=name,
        grid_spec=pltpu.PrefetchScalarGridSpec(
            num_scalar_prefetch=1, grid=(ly, N_CORES, steps),
            in_specs=[own_spec, mine_spec, sib_spec, spec, spec, spec], out_specs=[spec] * 4),
        out_shape=[shp] * 4,
        compiler_params=_params(("parallel", "parallel", "parallel")),
    )(qc, own, mine, sib, w, m, v)


def _allreduce_small(v, name):
    r, w = v.shape

    def body(v_ref, o_ref, buf_ref, send_sems, recv_sems):
        x, y, c = _place()
        me = 4 * x + 2 * y + c

        def peer(k):
            return x ^ (k >> 2), y ^ ((k >> 1) & 1), c ^ (k & 1)

        def remote(k, slot):
            return pltpu.make_async_remote_copy(
                src_ref=v_ref, dst_ref=buf_ref.at[slot], send_sem=send_sems.at[k - 1], recv_sem=recv_sems.at[k - 1],
                device_id=peer(k), device_id_type=MESH)

        sends = [remote(k, me) for k in range(1, N_DEV)]
        for cp in sends:
            cp.start()
        buf_ref[me] = v_ref[...]
        for k in range(1, N_DEV):
            px, py, pc = peer(k)
            remote(k, 4 * px + 2 * py + pc).wait_recv()
        for cp in sends:
            cp.wait_send()
        acc = buf_ref[0]
        for dev in range(1, N_DEV):
            acc = acc + buf_ref[dev]
        o_ref[...] = acc

    vm = pl.BlockSpec(memory_space=pltpu.VMEM)
    return pl.pallas_call(
        body,
        name=name,
        in_specs=[vm],
        out_specs=vm,
        out_shape=jax.ShapeDtypeStruct((r, w), F32),
        scratch_shapes=[pltpu.VMEM((N_DEV, r, w), F32), pltpu.SemaphoreType.DMA((N_DEV - 1,)), pltpu.SemaphoreType.DMA((N_DEV - 1,))],
        compiler_params=pltpu.CompilerParams(vmem_limit_bytes=VMEM_LIMIT_BYTES),
    )(v)


def _adamw(w, g, m, v, name):
    ly, r, c = w.shape
    tr = _pick(r, (256, 128, 64, 32, 16, 8))
    c1 = 1.0 - ADAM_B1 ** ADAM_STEP
    c2 = 1.0 - ADAM_B2 ** ADAM_STEP

    def body(w_ref, g_ref, m_ref, v_ref, d_ref, nm_ref, nv_ref):
        gv = g_ref[...]
        nm = ADAM_B1 * m_ref[...] + (1.0 - ADAM_B1) * gv
        nv = ADAM_B2 * v_ref[...] + (1.0 - ADAM_B2) * (gv * gv)
        d_ref[...] = -ADAM_LR * ((nm / c1) / (jnp.sqrt(nv / c2) + ADAM_EPS) + ADAM_WD * w_ref[...])
        nm_ref[...] = nm
        nv_ref[...] = nv

    spec = pl.BlockSpec((None, tr, c), lambda l, i: (l, i, 0))
    shp = jax.ShapeDtypeStruct((ly, r, c), F32)
    return pl.pallas_call(
        body,
        name=name,
        grid=(ly, r // tr),
        in_specs=[spec] * 4,
        out_specs=[spec] * 3,
        out_shape=[shp] * 3,
        compiler_params=_params(("parallel", "parallel")),
    )(w, g, m, v)


def _rope_tables(seq):
    pos = jnp.arange(seq, dtype=F32)
    inv_freq = 1.0 / (ROPE_THETA ** (jnp.arange(0, HEAD_DIM, 2, dtype=F32) / HEAD_DIM))
    ang = (pos[:, None] * inv_freq[None, :]).T
    cos, sin = jnp.cos(ang), jnp.sin(ang)
    return jnp.concatenate([cos, cos], axis=0), jnp.concatenate([-sin, sin], axis=0)


def _pack(vs, fill=0.0):
    p = jnp.concatenate([v.reshape(-1) for v in vs])
    size = -(-p.shape[0] // 8192) * 8192
    return jnp.pad(p, (0, size - p.shape[0]), constant_values=fill).reshape(-1, 1024)


def _unpack(p, like):
    p = p.reshape(-1)
    out, o = [], 0
    for v in like:
        n = int(math.prod(v.shape))
        out.append(p[o:o + n].reshape(v.shape))
        o += n
    return out


def kernel(x, norm_mix, norm_ffn, norm_final, conv_w_in, conv_w_conv, conv_w_out, attn_w_qkv, attn_b_qkv, attn_sinks, attn_w_o, attn_b_o, ffn_w_in, ffn_w_conv, ffn_w_down, loss_target, m_norm_mix, m_norm_ffn, m_norm_final, m_conv_w_in, m_conv_w_conv, m_conv_w_out, m_attn_w_qkv, m_attn_b_qkv, m_attn_sinks, m_attn_w_o, m_attn_b_o, m_ffn_w_in, m_ffn_w_conv, m_ffn_w_down, v_norm_mix, v_norm_ffn, v_norm_final, v_conv_w_in, v_conv_w_conv, v_conv_w_out, v_attn_w_qkv, v_attn_b_qkv, v_attn_sinks, v_attn_w_o, v_attn_b_o, v_ffn_w_in, v_ffn_w_conv, v_ffn_w_down):
    bsz, seq, d = x.shape
    t = bsz * seq
    depth = norm_mix.shape[0]
    n_conv, n_attn = conv_w_in.shape[0], attn_w_qkv.shape[0]
    xq, yq, cq = _place()
    q = 2 * xq + yq

    big = [conv_w_in, conv_w_out, attn_w_qkv, attn_w_o, ffn_w_in, ffn_w_down]
    axes = [2, 1, 2, 1, 2, 1]
    q_arr = q.astype(jnp.int32).reshape(1)
    c_arr = cq.astype(jnp.int32).reshape(1)
    weights = [None] * 6
    sems = {}

    def place(n):
        weights[n] = _place_shard(big[n], axes[n], q_arr, f"place_shard{n}")

    def pieces_of(i):
        return [(0, i // 2), (1, i // 2), (4, i), (5, i)] if i % 2 == 0 else [(2, i // 2), (3, i // 2), (4, i), (5, i)]

    def subset(pieces):
        ts = [ti for ti in range(6) if weights[ti] is not None]
        return ts, [(ts.index(ti), l) for ti, l in pieces]

    def fetch(pieces, stage, tag):
        ts, local = subset(pieces)
        out, sems[tag, 0], sems[tag, 1] = _gather_start([weights[ti] for ti in ts], [axes[ti] for ti in ts], local, stage,
                                                        f"gather_{'ici' if stage == 0 else 'pass'}_start{tag}")
        for n, ti in enumerate(ts):
            weights[ti] = out[n]

    def settle(pieces, stage, tag, after):
        ts, local = subset(pieces)
        out = _gather_wait([weights[ti] for ti in ts], sems[tag, 0], sems[tag, 1], after, [axes[ti] for ti in ts], local, stage,
                           f"gather_{'ici' if stage == 0 else 'pass'}_wait{tag}")
        for n, ti in enumerate(ts):
            weights[ti] = out[n]

    xs = x.reshape(t, d)
    place(0)
    place(1)
    fetch(pieces_of(0)[:2], 0, "m0")
    for n in (2, 3, 4, 5):
        place(n)
    h = _rms_fwd(xs, norm_mix[0:1], "norm_mix_fwd0")
    settle(pieces_of(0)[:2], 0, "m0", h)
    fetch(pieces_of(0)[:2], 1, "m0")

    small_cols = [conv_w_conv, attn_b_qkv, attn_b_o, ffn_w_conv]

    def placed(v):
        width = v.shape[-1]
        full = jnp.zeros(v.shape[:-1] + (N_CHIPS * width,), F32)
        return lax.dynamic_update_slice_in_dim(full, v * (1.0 / N_CORES), q * width, axis=v.ndim - 1)

    full_cols = [placed(v) for v in small_cols]
    small_full = _allreduce_small(_pack(full_cols), "gather_small")
    wc_conv, b_qkv, b_o, wf_conv = _unpack(small_full, full_cols)
    settle(pieces_of(0)[:2], 1, "m0", small_full)
    cos_t, sin_t = _rope_tables(seq)

    saved = []
    for i in range(depth):
        j = i // 2
        ahead = pieces_of(i + 1) if i + 1 < depth else None
        if i == 0:
            fetch(pieces_of(0)[2:], 0, "0")
        elif ahead:
            fetch(ahead, 0, str(i + 1))
        w_cin, w_cout, w_qkv, w_o, w_fin, w_fdown = weights
        g_ffn = norm_ffn[i:i + 1]
        if i % 2 == 0:
            pre = _mm(h, w_cin, "nn", BF16, layer=j, tm=1024, tn=768, tk=4096, name=f"conv_in_fwd{i}")
            mixed = _convgate_fwd(pre, wc_conv[j], seq, f"conv_gate_fwd{i}")
            if i == 0:
                settle(pieces_of(0)[2:], 0, "0", mixed)
                fetch(pieces_of(0)[2:], 1, "0")
                w_cin, w_cout, w_qkv, w_o, w_fin, w_fdown = weights
            x_mid, h2 = _mm(mixed, w_cout, "nn", F32, layer=j, residual=xs, norm_g=g_ffn, tm=512, tn=1024, tk=4096,
                            name=f"conv_out_fwd{i}")
            lse = None
        else:
            pre = _mm(h, w_qkv, "nn", F32, layer=j, bias=b_qkv[j:j + 1], tm=1024, tn=768, tk=4096, name=f"qkv_fwd{i}")
            mixed, lse = _attn_fwd(pre, attn_sinks[j], cos_t, sin_t, bsz, seq, f"attn_fwd{i}", hp=1)
            x_mid, h2 = _mm(mixed, w_o, "nn", F32, layer=j, bias=b_o[j:j + 1], residual=xs, norm_g=g_ffn, tm=512, tn=1024,
                            tk=4096, name=f"attn_out_fwd{i}")
        if i == 0:
            settle(pieces_of(0)[2:], 1, "0", h2)
            fetch(ahead, 0, "1")
        elif ahead:
            settle(ahead, 0, str(i + 1), x_mid)
            fetch(ahead, 1, str(i + 1))
        w_cin, w_cout, w_qkv, w_o, w_fin, w_fdown = weights
        gu = _mm(h2, w_fin, "nn", BF16, layer=i, n_outer=True, tm=1024, tn=2816, tk=4096, name=f"ffn_in_fwd{i}")
        if i == 0:
            settle(ahead, 0, "1", gu)
            fetch(ahead, 1, "1")
            w_cin, w_cout, w_qkv, w_o, w_fin, w_fdown = weights
        g_next = norm_mix[i + 1:i + 2] if i + 1 < depth else norm_final.reshape(1, d)
        x_next, act, h_next = _ffn_gate_down_fwd(gu, wf_conv[i], w_fdown, i, x_mid, g_next, seq, f"ffn_gate_down_fwd{i}")
        if ahead:
            settle(ahead, 1, str(i + 1), x_next)
        saved.append((xs, h, pre, mixed, lse, x_mid, h2, gu, act))
        xs, h = x_next, h_next
    w_cin, w_cout, w_qkv, w_o, w_fin, w_fdown = weights

    dx, dxb, sq, dg_final = _loss_head(xs, loss_target.reshape(t, d), norm_final.reshape(1, d), "loss_head")
    loss = lax.psum(0.5 * jnp.sum(sq) / d, ("x", "y", "c"))

    g_norm_mix, g_norm_ffn = [None] * depth, [None] * depth
    g_cin, g_cconv, g_cout = [None] * n_conv, [None] * n_conv, [None] * n_conv
    g_qkv, g_bqkv, g_sinks, g_o, g_bo = ([None] * n_attn for _ in range(5))
    g_fin, g_fconv, g_fdown = [None] * depth, [None] * depth, [None] * depth

    kinds6 = ["col", "row", "col", "row", "col", "row"]
    layers6 = [n_conv, n_conv, n_attn, n_attn, depth, depth]
    big_w = [conv_w_in, conv_w_out, attn_w_qkv, attn_w_o, ffn_w_in, ffn_w_down]

    def slot_stack(n):
        k, cols = big_w[n].shape[1], big_w[n].shape[2]
        r = k // 2
        return lax.empty((layers6[n], N_CHIPS, r, cols), BF16)

    own = [slot_stack(n) for n in range(6)]
    mine = [slot_stack(n) for n in range(6)]
    sib = [slot_stack(n) for n in range(6)]
    flight = {}

    def group(i, part):
        return f"{part}{i}", (pieces_of(i)[:2] if part == "m" else pieces_of(i)[2:])

    def scatter(grp, stage, action, token):
        tag, pieces = grp
        ts = [ti for ti, _ in pieces]
        local = [(n, l) for n, (_, l) in enumerate(pieces)]
        sub = ([own[ti] for ti in ts], [mine[ti] for ti in ts], [sib[ti] for ti in ts], token)
        label = f"grad_{'ici' if stage == 0 else 'pass'}_{action}_{tag}"
        if action == "start":
            o, mi, si, token, s_sems, r_sems = _scatter_start(*sub, local, stage, label)
            flight[tag] = (s_sems, r_sems)
        else:
            o, mi, si, token = _scatter_wait(*sub, *flight[tag], local, stage, label)
        for n, ti in enumerate(ts):
            own[ti], mine[ti], sib[ti] = o[n], mi[n], si[n]
        return token

    def pair_begin(grp, token):
        tag, pieces = grp
        grads = {0: g_cin, 1: g_cout, 2: g_qkv, 3: g_o, 4: g_fin, 5: g_fdown}
        parts, kinds = [], []
        for ti, l in pieces:
            g = grads[ti][l]
            parts.append(g if kinds6[ti] == "col" else g.reshape(N_CHIPS, g.shape[0] // N_CHIPS, g.shape[1]))
            kinds.append(kinds6[ti])
        parts, token, lands, s_sems, r_sems = _pair_start(parts, kinds, token, f"grad_pair_start_{tag}")
        flight["pair" + tag] = (parts, kinds, lands, s_sems, r_sems)
        return token

    def pair_finish(grp, token):
        tag, pieces = grp
        parts, kinds, lands, s_sems, r_sems = flight["pair" + tag]
        parts, token, recv = _pair_wait(parts, kinds, token, lands, s_sems, r_sems, f"grad_pair_wait_{tag}")
        for (ti, l), g, r in zip(pieces, parts, recv):
            own[ti] = _pair_sum(g, r, kinds6[ti], c_arr, own[ti], l, f"grad_pair_sum_{tag}_{ti}")
        return scatter(grp, 0, "start", token)

    for i in reversed(range(depth)):
        j = i // 2
        x_in, h, pre, mixed, lse, x_mid, h2, gu, act = saved[i]
        da = _mm(dxb, w_fdown, "nt", BF16, layer=i, n_outer=True, tm=1024, tn=2816, tk=4096, name=f"ffn_down_dx{i}")
        g_fdown[i] = _mm(act, dxb, "tn", BF16, tm=1408, tn=1024, tk=2048, name=f"ffn_down_dw{i}")
        dgu, dwc = _ffngate_bwd(gu, da, wf_conv[i], seq, f"ffn_gate_bwd{i}")
        g_fconv[i] = dwc[:3]
        g_fin[i] = _mm(h2, dgu, "tn", BF16, tm=1024, tn=1408, tk=2048, name=f"ffn_in_dw{i}")
        dgu = pair_begin(group(i, "f"), dgu)
        dx, dxb, dg, colsum = _rms_bwd(x_mid, dgu, w_fin, i, norm_ffn[i:i + 1], dx, f"ffn_in_dx_norm_bwd{i}")
        g_norm_ffn[i] = jnp.sum(dg, axis=0)
        if i + 1 < depth:
            dxb = scatter(group(i + 1, "m"), 1, "start", scatter(group(i + 1, "m"), 0, "wait", dxb))
            dxb = scatter(group(i + 1, "f"), 1, "wait", dxb)
        dxb = pair_finish(group(i, "f"), dxb)
        if i % 2 == 0:
            dmix = _mm(dxb, w_cout, "nt", BF16, layer=j, tm=512, tn=1024, tk=4096, name=f"conv_out_dx{i}")
            g_cout[j] = _mm(mixed, dxb, "tn", BF16, tm=1024, tn=1024, tk=2048, name=f"conv_out_dw{i}")
            dpre, dwc = _convgate_bwd(pre, dmix, wc_conv[j], seq, f"conv_gate_bwd{i}")
            g_cconv[j] = dwc[:3]
            g_cin[j] = _mm(h, dpre, "tn", BF16, tm=1024, tn=1536, tk=2048, name=f"conv_in_dw{i}")
            w_pre = w_cin
        else:
            g_bo[j] = jnp.sum(colsum, axis=0)
            dmix = _mm(dxb, w_o, "nt", F32, layer=j, tm=512, tn=1024, tk=4096, name=f"attn_out_dx{i}")
            g_o[j] = _mm(mixed, dxb, "tn", BF16, tm=1024, tn=1024, tk=2048, name=f"attn_out_dw{i}")
            dpre, dbias, dsk = _attn_bwd(pre, mixed, lse, dmix, attn_sinks[j], cos_t, sin_t, bsz, seq, f"attn_bwd{i}",
                                         hp=GROUP)
            g_bqkv[j] = jnp.sum(dbias, axis=0)
            g_sinks[j] = jnp.sum(dsk, axis=1)
            g_qkv[j] = _mm(h, dpre, "tn", BF16, tm=1024, tn=1536, tk=2048, name=f"qkv_dw{i}")
            w_pre = w_qkv
        dpre = pair_begin(group(i, "m"), dpre)
        dx, dxb, dg, _ = _rms_bwd(x_in, dpre, w_pre, j, norm_mix[i:i + 1], dx, f"mixer_in_dx_norm_bwd{i}")
        g_norm_mix[i] = jnp.sum(dg, axis=0)
        dxb = scatter(group(i, "f"), 1, "start", scatter(group(i, "f"), 0, "wait", dxb))
        if i + 1 < depth:
            dxb = scatter(group(i + 1, "m"), 1, "wait", dxb)
        dxb = pair_finish(group(i, "m"), dxb)
    grad_x = dx.reshape(bsz, seq, d)

    scatter(group(0, "f"), 1, "wait", dxb)
    big_m = [m_conv_w_in, m_conv_w_out, m_attn_w_qkv, m_attn_w_o, m_ffn_w_in, m_ffn_w_down]
    big_v = [v_conv_w_in, v_conv_w_out, v_attn_w_qkv, v_attn_w_o, v_ffn_w_in, v_ffn_w_down]
    big_names = ["conv_w_in", "conv_w_out", "attn_w_qkv", "attn_w_o", "ffn_w_in", "ffn_w_down"]
    qc_arr = jnp.stack([q, cq]).astype(jnp.int32)

    def adamw_of(n):
        return list(_reduce_adamw(own[n], mine[n], sib[n], big_w[n], big_m[n], big_v[n], qc_arr, f"adamw_{big_names[n]}"))

    big_upd = [None] * 6
    for n in (2, 3, 4, 5):
        big_upd[n] = adamw_of(n)
    token = scatter(group(0, "m"), 1, "start", scatter(group(0, "m"), 0, "wait", big_upd[5][1]))
    big_upd[5][1] = scatter(group(0, "m"), 1, "wait", token)
    for n in (0, 1):
        big_upd[n] = adamw_of(n)

    small = [jnp.stack(g_norm_mix), jnp.stack(g_norm_ffn), jnp.sum(dg_final, axis=0), jnp.stack(g_cconv),
             jnp.stack(g_bqkv), jnp.stack(g_sinks), jnp.stack(g_bo), jnp.stack(g_fconv)]
    sg = _unpack(_allreduce_small(_pack(small), "grad_small_allreduce"), small)

    def my_cols(v, like):
        width = like.shape[-1]
        return lax.dynamic_slice_in_dim(v, q * width, width, axis=v.ndim - 1)

    small_w = [norm_mix, norm_ffn, norm_final, conv_w_conv, attn_b_qkv, attn_sinks, attn_b_o, ffn_w_conv]
    small_m = [m_norm_mix, m_norm_ffn, m_norm_final, m_conv_w_conv, m_attn_b_qkv, m_attn_sinks, m_attn_b_o, m_ffn_w_conv]
    small_v = [v_norm_mix, v_norm_ffn, v_norm_final, v_conv_w_conv, v_attn_b_qkv, v_attn_sinks, v_attn_b_o, v_ffn_w_conv]
    small_g = [sg[0], sg[1], sg[2], my_cols(sg[3], conv_w_conv), my_cols(sg[4], attn_b_qkv), sg[5],
               my_cols(sg[6], attn_b_o), my_cols(sg[7], ffn_w_conv)]

    upd = {nm: tuple(u[1:]) for nm, u in zip(big_names, big_upd)}
    sd, sm, sv = _adamw(_pack(small_w)[None], _pack(small_g)[None], _pack(small_m)[None], _pack(small_v, 1.0)[None],
                        "adamw_small")
    sd, sm, sv = _unpack(sd, small_w), _unpack(sm, small_w), _unpack(sv, small_w)
    names = ["norm_mix", "norm_ffn", "norm_final", "conv_w_in", "conv_w_conv", "conv_w_out", "attn_w_qkv", "attn_b_qkv",
             "attn_sinks", "attn_w_o", "attn_b_o", "ffn_w_in", "ffn_w_conv", "ffn_w_down"]
    small_names = ["norm_mix", "norm_ffn", "norm_final", "conv_w_conv", "attn_b_qkv", "attn_sinks", "attn_b_o", "ffn_w_conv"]
    grads = dict(zip(small_names, small_g))
    grads.update({nm: u[0] for nm, u in zip(big_names, big_upd)})
    for n, nm in enumerate(small_names):
        upd[nm] = (sd[n], sm[n], sv[n])
    return (loss, grad_x, *[grads[nm] for nm in names], *[upd[nm][0] for nm in names],
            *[upd[nm][1] for nm in names], *[upd[nm][2] for nm in names])
```

```python
import math

import jax
import jax.numpy as jnp
from jax import lax
from jax.experimental import pallas as pl
from jax.experimental.pallas import tpu as pltpu

F32 = jnp.float32
BF16 = jnp.bfloat16

HEAD_DIM = 64
GROUP = 4
WINDOW = 128
EPS = 1e-5
ROPE_THETA = 10000.0
ADAM_LR, ADAM_B1, ADAM_B2, ADAM_EPS, ADAM_WD, ADAM_STEP = 0.001, 0.9, 0.999, 1e-08, 0.01, 10

N_CHIPS = 4
N_CORES = 2
N_DEV = 8
HALO = 16
VMEM_LIMIT_BYTES = 56 * 1024 * 1024
MESH = pl.DeviceIdType.MESH
ANY = pl.BlockSpec(memory_space=pl.ANY)
SMEM = pl.BlockSpec(memory_space=pltpu.SMEM)
NEG = float(jnp.finfo(jnp.float32).min)
ROW_TILES = (512, 256, 128, 64, 32, 16, 8)


def _pick(dim, cands):
    for c in cands:
        if dim % c == 0:
            return c
    return dim


def _params(sem):
    return pltpu.CompilerParams(dimension_semantics=sem, vmem_limit_bytes=VMEM_LIMIT_BYTES)


_DIMS = {"nn": (((1,), (0,)), ((), ())), "nt": (((1,), (1,)), ((), ())), "tn": (((0,), (0,)), ((), ()))}


def _mm(a, b, mode, out_dtype, *, layer=None, bias=None, residual=None, norm_g=None, n_outer=False, tm, tn, tk, name):
    b2 = b.shape[1:] if layer is not None else b.shape
    if mode == "nn":
        (m, k), n = a.shape, b2[1]
    elif mode == "nt":
        (m, k), n = a.shape, b2[0]
    else:
        (k, m), n = a.shape, b2[1]
    tm, tn, tk = min(tm, m), min(tn, n), min(tk, k)
    assert m % tm == 0 and n % tn == 0 and k % tk == 0, (name, a.shape, b.shape, tm, tn, tk)
    nk = k // tk

    def at(f):
        return (lambda p0, p1, p2: f(p1, p0, p2)) if n_outer else f

    a_spec = pl.BlockSpec((tk, tm), at(lambda i, j, l: (l, i))) if mode == "tn" else pl.BlockSpec((tm, tk), at(lambda i, j, l: (i, l)))
    if layer is None:
        b_spec = (pl.BlockSpec((tn, tk), at(lambda i, j, l: (j, l))) if mode == "nt"
                  else pl.BlockSpec((tk, tn), at(lambda i, j, l: (l, j))))
    elif mode == "nt":
        b_spec = pl.BlockSpec((None, tn, tk), at(lambda i, j, l: (layer, j, l)))
    else:
        b_spec = pl.BlockSpec((None, tk, tn), at(lambda i, j, l: (layer, l, j)))
    in_specs, args = [a_spec, b_spec], [a, b]
    if bias is not None:
        in_specs.append(pl.BlockSpec((1, tn), at(lambda i, j, l: (0, j))))
        args.append(bias)
    if residual is not None:
        in_specs.append(pl.BlockSpec((tm, tn), at(lambda i, j, l: (i, j))))
        args.append(residual)
    if norm_g is not None:
        assert tn == n, (name, "the RMSNorm of the result needs whole rows in a tile")
        in_specs.append(pl.BlockSpec((1, tn), at(lambda i, j, l: (0, j))))
        args.append(norm_g)
    has_bias, has_res, has_norm = bias is not None, residual is not None, norm_g is not None

    def body(*refs):
        a_ref, b_ref = refs[0], refs[1]
        pos = 2
        bias_ref = res_ref = g_ref = h_ref = None
        if has_bias:
            bias_ref, pos = refs[pos], pos + 1
        if has_res:
            res_ref, pos = refs[pos], pos + 1
        if has_norm:
            g_ref, pos = refs[pos], pos + 1
        o_ref, pos = refs[pos], pos + 1
        if has_norm:
            h_ref, pos = refs[pos], pos + 1
        acc_ref = refs[pos] if nk > 1 else None

        def finish(acc):
            if has_bias:
                acc = acc + bias_ref[...]
            if has_res:
                acc = acc + res_ref[...]
            o_ref[...] = acc.astype(o_ref.dtype)
            if has_norm:
                h_ref[...] = _rms(acc, g_ref[...]).astype(BF16)

        if nk == 1:
            finish(lax.dot_general(a_ref[...], b_ref[...], _DIMS[mode], preferred_element_type=F32))
            return
        l = pl.program_id(2)
        part = lax.dot_general(a_ref[...], b_ref[...], _DIMS[mode], preferred_element_type=F32)

        @pl.when(l == 0)
        def _():
            acc_ref[...] = part

        @pl.when(l > 0)
        def _():
            acc_ref[...] += part

        @pl.when(l == nk - 1)
        def _():
            finish(acc_ref[...])

    o_spec = pl.BlockSpec((tm, tn), at(lambda i, j, l: (i, j)))
    o_shape = jax.ShapeDtypeStruct((m, n), out_dtype)
    return pl.pallas_call(
        body,
        name=name,
        grid=(n // tn, m // tm, nk) if n_outer else (m // tm, n // tn, nk),
        in_specs=in_specs,
        out_specs=[o_spec, o_spec] if has_norm else o_spec,
        out_shape=[o_shape, jax.ShapeDtypeStruct((m, n), BF16)] if has_norm else o_shape,
        scratch_shapes=[pltpu.VMEM((tm, tn), F32)] if nk > 1 else [],
        compiler_params=_params(("parallel", "parallel", "arbitrary")),
    )(*args)


def _rms(x, g):
    return x * lax.rsqrt(jnp.mean(x * x, axis=-1, keepdims=True) + EPS) * g


def _fold8(v):
    r, d = v.shape
    return jnp.sum(v.reshape(r // 8, 8, d), axis=0)


def _rms_fwd(x, g, name):
    t, d = x.shape
    tm = _pick(t, ROW_TILES)

    def body(x_ref, g_ref, h_ref):
        xv = x_ref[...]
        r = lax.rsqrt(jnp.mean(xv * xv, axis=-1, keepdims=True) + EPS)
        h_ref[...] = (xv * r * g_ref[...]).astype(BF16)

    return pl.pallas_call(
        body,
        name=name,
        grid=(t // tm,),
        in_specs=[pl.BlockSpec((tm, d), lambda i: (i, 0)), pl.BlockSpec((1, d), lambda i: (0, 0))],
        out_specs=pl.BlockSpec((tm, d), lambda i: (i, 0)),
        out_shape=jax.ShapeDtypeStruct((t, d), BF16),
        compiler_params=_params(("parallel",)),
    )(x, g)


def _rms_bwd(x, dpre, w, layer, g, dx_in, name):
    t, d = x.shape
    k = dpre.shape[1]
    tm = _pick(t, ROW_TILES)

    sub = _pick(tm, (256, 128, 64, 32, 16, 8))

    def body(x_ref, dp_ref, w_ref, g_ref, dxi_ref, dx_ref, dxb_ref, dg_ref, cs_ref):
        i = pl.program_id(0)

        @pl.when(i == 0)
        def _():
            dg_ref[...] = jnp.zeros_like(dg_ref)
            cs_ref[...] = jnp.zeros_like(cs_ref)

        dg = jnp.zeros((8, d), F32)
        cs = jnp.zeros((8, d), F32)
        for r0 in range(0, tm, sub):
            rows = slice(r0, r0 + sub)
            xv = x_ref[rows, :]
            r = lax.rsqrt(jnp.mean(xv * xv, axis=-1, keepdims=True) + EPS)
            xhat = xv * r
            dy = lax.dot_general(dp_ref[rows, :], w_ref[...], _DIMS["nt"], preferred_element_type=F32)
            gdy = dy * g_ref[...]
            dx = dxi_ref[rows, :] + r * (gdy - xhat * jnp.mean(gdy * xhat, axis=-1, keepdims=True))
            dx_ref[rows, :] = dx
            dxb_ref[rows, :] = dx.astype(BF16)
            dg += _fold8(dy * xhat)
            cs += _fold8(dx)
        dg_ref[...] += dg
        cs_ref[...] += cs

    row = pl.BlockSpec((tm, d), lambda i: (i, 0))
    acc = pl.BlockSpec((8, d), lambda i: (0, 0))
    w_spec = pl.BlockSpec((None, d, k), lambda i: (layer, 0, 0), pipeline_mode=pl.Buffered(1))
    return pl.pallas_call(
        body,
        name=name,
        grid=(t // tm,),
        in_specs=[row, pl.BlockSpec((tm, k), lambda i: (i, 0)), w_spec, pl.BlockSpec((1, d), lambda i: (0, 0)), row],
        out_specs=[row, row, acc, acc],
        out_shape=[jax.ShapeDtypeStruct((t, d), F32), jax.ShapeDtypeStruct((t, d), BF16),
                   jax.ShapeDtypeStruct((8, d), F32), jax.ShapeDtypeStruct((8, d), F32)],
        compiler_params=_params(("arbitrary",)),
    )(x, dpre, w, g, dx_in)


def _loss_head(x, target, g, name):
    t, d = x.shape
    tm = _pick(t, ROW_TILES)
    inv_d = 1.0 / d

    def body(x_ref, t_ref, g_ref, dx_ref, dxb_ref, sq_ref, dg_ref):
        i = pl.program_id(0)
        xv = x_ref[...]
        gv = g_ref[...]
        r = lax.rsqrt(jnp.mean(xv * xv, axis=-1, keepdims=True) + EPS)
        xhat = xv * r
        err = xhat * gv - t_ref[...]
        dy = err * inv_d
        gdy = dy * gv
        dx = r * (gdy - xhat * jnp.mean(gdy * xhat, axis=-1, keepdims=True))
        dx_ref[...] = dx
        dxb_ref[...] = dx.astype(BF16)

        @pl.when(i == 0)
        def _():
            sq_ref[...] = jnp.zeros_like(sq_ref)
            dg_ref[...] = jnp.zeros_like(dg_ref)

        sq_ref[...] += _fold8(err * err)
        dg_ref[...] += _fold8(dy * xhat)

    row = pl.BlockSpec((tm, d), lambda i: (i, 0))
    acc = pl.BlockSpec((8, d), lambda i: (0, 0))
    return pl.pallas_call(
        body,
        name=name,
        grid=(t // tm,),
        in_specs=[row, row, pl.BlockSpec((1, d), lambda i: (0, 0))],
        out_specs=[row, row, acc, acc],
        out_shape=[jax.ShapeDtypeStruct((t, d), F32), jax.ShapeDtypeStruct((t, d), BF16),
                   jax.ShapeDtypeStruct((8, d), F32), jax.ShapeDtypeStruct((8, d), F32)],
        compiler_params=_params(("arbitrary",)),
    )(x, target, g)


def _rows(tm):
    return lax.broadcasted_iota(jnp.int32, (tm, 1), 0)


def _shift_down(u, before2):
    r8 = _rows(8)
    s1, s2 = pltpu.roll(u, 1, 0), pltpu.roll(u, 2, 0)
    top1 = jnp.where(r8 == 0, before2[1:2], s1[:8])
    top2 = jnp.where(r8 == 0, before2[0:1], jnp.where(r8 == 1, before2[1:2], s2[:8]))
    return jnp.concatenate([top1, s1[8:]], axis=0), jnp.concatenate([top2, s2[8:]], axis=0)


def _shift_up(u, after2):
    tm = u.shape[0]
    r8 = _rows(8)
    s1, s2 = pltpu.roll(u, tm - 1, 0), pltpu.roll(u, tm - 2, 0)
    bot1 = jnp.where(r8 == 7, after2[0:1], s1[tm - 8:])
    bot2 = jnp.where(r8 == 6, after2[0:1], jnp.where(r8 == 7, after2[1:2], s2[tm - 8:]))
    return jnp.concatenate([s1[:tm - 8], bot1], axis=0), jnp.concatenate([s2[:tm - 8], bot2], axis=0)


def _shift_matrix(tm, up):
    r = lax.broadcasted_iota(jnp.int32, (2 * tm, tm), 0)
    c = lax.broadcasted_iota(jnp.int32, (2 * tm, tm), 1)
    t = jnp.where(r >= tm, r - tm, r)
    k = jnp.where(r >= tm, 2, 1)
    return (c == (t + k if up else t - k)).astype(BF16)


def _shift_down_mxu(u, before2, mat=None):
    tm = u.shape[0]
    moved = jnp.dot(_shift_matrix(tm, False) if mat is None else mat, u.astype(BF16), preferred_element_type=F32)
    r8 = _rows(8)
    s1, s2 = moved[:tm], moved[tm:]
    top1 = s1[:8] + jnp.where(r8 == 0, before2[1:2], 0.0)
    top2 = s2[:8] + jnp.where(r8 == 0, before2[0:1], jnp.where(r8 == 1, before2[1:2], 0.0))
    return jnp.concatenate([top1, s1[8:]], axis=0), jnp.concatenate([top2, s2[8:]], axis=0)


def _shift_up_mxu(u, after2, mat=None):
    tm = u.shape[0]
    moved = jnp.dot(_shift_matrix(tm, True) if mat is None else mat, u.astype(BF16), preferred_element_type=F32)
    r8 = _rows(8)
    s1, s2 = moved[:tm], moved[tm:]
    bot1 = s1[tm - 8:] + jnp.where(r8 == 7, after2[0:1], 0.0)
    bot2 = s2[tm - 8:] + jnp.where(r8 == 6, after2[0:1], jnp.where(r8 == 7, after2[1:2], 0.0))
    return jnp.concatenate([s1[:tm - 8], bot1], axis=0), jnp.concatenate([s2[:tm - 8], bot2], axis=0)


def _conv_tile(seq):
    return _pick(seq, (256, 128, 64, 32, 16, 8))


def _halo_specs(tm, width, n_tiles):
    per = tm // HALO
    before = pl.BlockSpec((HALO, width), lambda i: (jnp.maximum(i * per - 1, 0), 0))
    after = pl.BlockSpec((HALO, width), lambda i: (jnp.minimum((i + 1) * per, n_tiles * per - 1), 0))
    return before, after


def _convgate_fwd(bcv, w, seq, name):
    t, d3 = bcv.shape
    d = d3 // 3
    tm = _conv_tile(seq)
    tps = seq // tm
    before, _ = _halo_specs(tm, d3, t // tm)

    def body(x_ref, xb_ref, w_ref, y_ref):
        i = pl.program_id(0)
        inner = (i % tps != 0).astype(F32)
        u = x_ref[:, d:2 * d].astype(F32) * x_ref[:, 2 * d:].astype(F32)
        xb = xb_ref[:, d:].astype(F32)[HALO - 2:]
        s1, s2 = _shift_down(u, xb[:, :d] * xb[:, d:] * inner)
        z = w_ref[2:3] * u + w_ref[1:2] * s1 + w_ref[0:1] * s2
        y_ref[...] = (x_ref[:, :d].astype(F32) * z).astype(BF16)

    return pl.pallas_call(
        body,
        name=name,
        grid=(t // tm,),
        in_specs=[pl.BlockSpec((tm, d3), lambda i: (i, 0)), before, pl.BlockSpec((3, d), lambda i: (0, 0))],
        out_specs=pl.BlockSpec((tm, d), lambda i: (i, 0)),
        out_shape=jax.ShapeDtypeStruct((t, d), BF16),
        compiler_params=_params(("parallel",)),
    )(bcv, bcv, w)


def _convgate_bwd(bcv, dy, w, seq, name):
    t, d3 = bcv.shape
    d = d3 // 3
    tm = _conv_tile(seq)
    tps = seq // tm
    before, after = _halo_specs(tm, d3, t // tm)
    _, after_dy = _halo_specs(tm, d, t // tm)

    def body(x_ref, xb_ref, xa_ref, dy_ref, dya_ref, w_ref, dx_ref, dw_ref):
        i = pl.program_id(0)
        inner_lo = (i % tps != 0).astype(F32)
        inner_hi = (i % tps != tps - 1).astype(F32)
        w0, w1, w2 = w_ref[0:1], w_ref[1:2], w_ref[2:3]
        b, c, v = x_ref[:, :d].astype(F32), x_ref[:, d:2 * d].astype(F32), x_ref[:, 2 * d:].astype(F32)
        u = c * v
        xb = xb_ref[:, d:].astype(F32)[HALO - 2:]
        s1, s2 = _shift_down(u, xb[:, :d] * xb[:, d:] * inner_lo)
        z = w2 * u + w1 * s1 + w0 * s2
        dyv = dy_ref[...].astype(F32)
        dz = dyv * b
        dza = dya_ref[...].astype(F32)[0:2] * xa_ref[:, :d].astype(F32)[0:2] * inner_hi
        n1, n2 = _shift_up(dz, dza)
        du = w2 * dz + w1 * n1 + w0 * n2
        dx_ref[:, :d] = (dyv * z).astype(BF16)
        dx_ref[:, d:2 * d] = (du * v).astype(BF16)
        dx_ref[:, 2 * d:] = (du * c).astype(BF16)

        @pl.when(i == 0)
        def _():
            dw_ref[...] = jnp.zeros_like(dw_ref)

        dw_ref[0:1] += jnp.sum(dz * s2, axis=0, keepdims=True)
        dw_ref[1:2] += jnp.sum(dz * s1, axis=0, keepdims=True)
        dw_ref[2:3] += jnp.sum(dz * u, axis=0, keepdims=True)

    return pl.pallas_call(
        body,
        name=name,
        grid=(t // tm,),
        in_specs=[pl.BlockSpec((tm, d3), lambda i: (i, 0)), before, after,
                  pl.BlockSpec((tm, d), lambda i: (i, 0)), after_dy, pl.BlockSpec((3, d), lambda i: (0, 0))],
        out_specs=[pl.BlockSpec((tm, d3), lambda i: (i, 0)), pl.BlockSpec((8, d), lambda i: (0, 0))],
        out_shape=[jax.ShapeDtypeStruct((t, d3), BF16), jax.ShapeDtypeStruct((8, d), F32)],
        compiler_params=_params(("arbitrary",)),
    )(bcv, bcv, bcv, dy, dy, w)


def _sigmoid(x):
    return 1.0 / (1.0 + jnp.exp(-x))


def _ffn_gate_down_fwd(gu, w, w_down, layer, resid, norm_g, seq, name):
    t, f2 = gu.shape
    f = f2 // 2
    d = w_down.shape[2]
    sub = _conv_tile(seq)
    tm = _pick(seq, (2 * sub, sub))
    tps = seq // tm
    before, _ = _halo_specs(tm, f2, t // tm)

    def body(x_ref, xb_ref, w_ref, wd_ref, res_ref, g_ref, o_ref, a_ref, h_ref):
        i = pl.program_id(0)
        inner = (i % tps != 0).astype(F32)
        for r0 in range(0, tm, sub):
            rows = slice(r0, r0 + sub)
            if r0 == 0:
                halo = xb_ref[:, :f].astype(F32)[HALO - 2:] * inner
            else:
                halo = x_ref[r0 - HALO:r0, :f].astype(F32)[HALO - 2:]
            s1, s2 = _shift_down_mxu(x_ref[rows, :f], halo)
            gc = w_ref[2:3] * x_ref[rows, :f].astype(F32) + w_ref[1:2] * s1 + w_ref[0:1] * s2
            act = (gc * _sigmoid(gc) * x_ref[rows, f:].astype(F32)).astype(BF16)
            a_ref[rows, :] = act
            out = jnp.dot(act, wd_ref[...], preferred_element_type=F32) + res_ref[rows, :]
            o_ref[rows, :] = out
            h_ref[rows, :] = _rms(out, g_ref[...]).astype(BF16)

    row_d = pl.BlockSpec((tm, d), lambda i: (i, 0))
    return pl.pallas_call(
        body,
        name=name,
        grid=(t // tm,),
        in_specs=[pl.BlockSpec((tm, f2), lambda i: (i, 0)), before, pl.BlockSpec((3, f), lambda i: (0, 0)),
                  pl.BlockSpec((None, f, d), lambda i: (layer, 0, 0), pipeline_mode=pl.Buffered(1)),
                  row_d, pl.BlockSpec((1, d), lambda i: (0, 0))],
        out_specs=[row_d, pl.BlockSpec((tm, f), lambda i: (i, 0)), row_d],
        out_shape=[jax.ShapeDtypeStruct((t, d), F32), jax.ShapeDtypeStruct((t, f), BF16), jax.ShapeDtypeStruct((t, d), BF16)],
        compiler_params=_params(("parallel",)),
    )(gu, gu, w, w_down, resid, norm_g)


def _ffngate_bwd(gu, da, w, seq, name):
    t, f2 = gu.shape
    f = f2 // 2
    tm = _conv_tile(seq)
    tps = seq // tm
    before, after = _halo_specs(tm, f2, t // tm)
    _, after_da = _halo_specs(tm, f, t // tm)
    fc = _pick(f, (256, 128))

    def body(x_ref, xb_ref, xa_ref, da_ref, daa_ref, w_ref, dx_ref, dw_ref):
        i = pl.program_id(0)
        inner_lo = (i % tps != 0).astype(F32)
        inner_hi = (i % tps != tps - 1).astype(F32)
        down, up = _shift_matrix(tm, False), _shift_matrix(tm, True)

        def dgate(gc, uv, dav):
            sg = _sigmoid(gc)
            return dav * uv * (sg * (1.0 + gc * (1.0 - sg))), dav * (gc * sg)

        @pl.when(i == 0)
        def _():
            dw_ref[...] = jnp.zeros_like(dw_ref)

        for c0 in range(0, f, fc):
            cg, cu = slice(c0, c0 + fc), slice(f + c0, f + c0 + fc)
            w0, w1, w2 = w_ref[0:1, cg], w_ref[1:2, cg], w_ref[2:3, cg]
            g, u = x_ref[:, cg].astype(F32), x_ref[:, cu].astype(F32)
            s1, s2 = _shift_down_mxu(x_ref[:, cg], xb_ref[:, cg].astype(F32)[HALO - 2:] * inner_lo, down)
            gc = w2 * g + w1 * s1 + w0 * s2
            dgc, du = dgate(gc, u, da_ref[:, cg].astype(F32))
            ga = xa_ref[:, cg].astype(F32)
            a1, a2 = _shift_down(ga, x_ref[tm - HALO:, cg].astype(F32)[HALO - 2:])
            gca = w2 * ga + w1 * a1 + w0 * a2
            dgca, _ = dgate(gca, xa_ref[:, cu].astype(F32), daa_ref[:, cg].astype(F32))
            n1, n2 = _shift_up_mxu(dgc, dgca[0:2] * inner_hi, up)
            dx_ref[:, cg] = (w2 * dgc + w1 * n1 + w0 * n2).astype(BF16)
            dx_ref[:, cu] = du.astype(BF16)
            dw_ref[0:1, cg] += jnp.sum(dgc * s2, axis=0, keepdims=True)
            dw_ref[1:2, cg] += jnp.sum(dgc * s1, axis=0, keepdims=True)
            dw_ref[2:3, cg] += jnp.sum(dgc * g, axis=0, keepdims=True)

    return pl.pallas_call(
        body,
        name=name,
        grid=(t // tm,),
        in_specs=[pl.BlockSpec((tm, f2), lambda i: (i, 0)), before, after,
                  pl.BlockSpec((tm, f), lambda i: (i, 0)), after_da, pl.BlockSpec((3, f), lambda i: (0, 0))],
        out_specs=[pl.BlockSpec((tm, f2), lambda i: (i, 0)), pl.BlockSpec((8, f), lambda i: (0, 0))],
        out_shape=[jax.ShapeDtypeStruct((t, f2), BF16), jax.ShapeDtypeStruct((8, f), F32)],
        compiler_params=_params(("arbitrary",)),
    )(gu, gu, gu, da, da, w)


def _swap_halves(xt):
    half = HEAD_DIM // 2
    return jnp.concatenate([xt[half:], xt[:half]], axis=0)


def _rope(xt, cos, sin):
    return xt * cos + _swap_halves(xt) * sin


def _unrope(dxt, cos, sin):
    return dxt * cos - _swap_halves(dxt) * sin


def _key_query(count):
    kj = lax.broadcasted_iota(jnp.int32, (WINDOW, count * WINDOW), 0)
    qi = lax.broadcasted_iota(jnp.int32, (WINDOW, count * WINDOW), 1) & (WINDOW - 1)
    return kj, qi


def _band_masks(n, count):
    kj, qi = _key_query(count)
    return kj <= qi, jnp.logical_and(kj > qi, n > 0)


def _lanes(v, count):
    return jnp.concatenate([v] * count, axis=1) if count > 1 else v


def _heads(ref, h0, count):
    parts = [ref[(h0 + g) * HEAD_DIM:(h0 + g + 1) * HEAD_DIM, :] for g in range(count)]
    return jnp.concatenate(parts, axis=1) if count > 1 else parts[0]


def _head_rows(ref, h0, count):
    parts = [ref[h0 + g:h0 + g + 1, :] for g in range(count)]
    return jnp.concatenate(parts, axis=1) if count > 1 else parts[0]


def _head_sinks(sink_ref, h0, count):
    parts = [jnp.full((1, WINDOW), sink_ref[h0 + g], F32) for g in range(count)]
    return jnp.concatenate(parts, axis=1) if count > 1 else parts[0]


def _tn(a, b):
    return lax.dot_general(a, b, _DIMS["tn"], preferred_element_type=F32)


def _nt(a, b):
    return lax.dot_general(a, b, _DIMS["nt"], preferred_element_type=F32)


def _nn(a, b):
    return jnp.dot(a, b, preferred_element_type=F32)


def _attn_fwd(qkv, sinks, cos_t, sin_t, bsz, seq, name, hp):
    t, qw = qkv.shape
    d = qw * 2 // 3
    kvw = d // GROUP
    n_heads, n_kv = d // HEAD_DIM, kvw // HEAD_DIM
    nb = seq // WINDOW
    scale = HEAD_DIM ** -0.5

    def body(sink_ref, xc_ref, xp_ref, cc_ref, sc_ref, cp_ref, sp_ref, o_ref, lse_ref, xt_ref, pt_ref, ot_ref):
        n = pl.program_id(1)
        xt_ref[...] = xc_ref[...].T
        pt_ref[...] = xp_ref[:, d:].T
        cos_c, sin_c, cos_p, sin_p = cc_ref[...], sc_ref[...], cp_ref[...], sp_ref[...]
        cos_g, sin_g = _lanes(cos_c, hp), _lanes(sin_c, hp)
        valid_c, valid_p = _band_masks(n, hp)
        for j in range(n_kv):
            ko = j * HEAD_DIM
            kc = _rope(xt_ref[d + ko:d + ko + HEAD_DIM, :], cos_c, sin_c).astype(BF16)
            kp = _rope(pt_ref[ko:ko + HEAD_DIM, :], cos_p, sin_p).astype(BF16)
            vc = xt_ref[d + kvw + ko:d + kvw + ko + HEAD_DIM, :].astype(BF16)
            vp = pt_ref[kvw + ko:kvw + ko + HEAD_DIM, :].astype(BF16)
            for h0 in range(j * GROUP, (j + 1) * GROUP, hp):
                q = _rope(_heads(xt_ref, h0, hp), cos_g, sin_g).astype(BF16)
                sink = _head_sinks(sink_ref, h0, hp)
                s_c = jnp.where(valid_c, _tn(kc, q) * scale, NEG)
                s_p = jnp.where(valid_p, _tn(kp, q) * scale, NEG)
                m = jnp.maximum(jnp.maximum(jnp.max(s_c, axis=0, keepdims=True), jnp.max(s_p, axis=0, keepdims=True)), sink)
                p_c = jnp.exp(s_c - m)
                p_p = jnp.exp(s_p - m)
                den = jnp.sum(p_c, axis=0, keepdims=True) + jnp.sum(p_p, axis=0, keepdims=True) + jnp.exp(sink - m)
                inv = 1.0 / den
                o_g = _nn(vc, (p_c * inv).astype(BF16)) + _nn(vp, (p_p * inv).astype(BF16))
                lse_g = m + jnp.log(den)
                for g in range(hp):
                    h = h0 + g
                    ot_ref[h * HEAD_DIM:(h + 1) * HEAD_DIM, :] = o_g[:, g * WINDOW:(g + 1) * WINDOW]
                    lse_ref[h:h + 1, :] = lse_g[:, g * WINDOW:(g + 1) * WINDOW]
        o_ref[...] = ot_ref[...].T.astype(BF16)

    cur = lambda b, n: (b * nb + n, 0)
    prev = lambda b, n: (b * nb + jnp.maximum(n - 1, 0), 0)
    tab_c = pl.BlockSpec((HEAD_DIM, WINDOW), lambda b, n: (0, n))
    tab_p = pl.BlockSpec((HEAD_DIM, WINDOW), lambda b, n: (0, jnp.maximum(n - 1, 0)))
    return pl.pallas_call(
        body,
        name=name,
        grid=(bsz, nb),
        in_specs=[SMEM, pl.BlockSpec((WINDOW, qw), cur), pl.BlockSpec((WINDOW, qw), prev), tab_c, tab_c, tab_p, tab_p],
        out_specs=[pl.BlockSpec((WINDOW, d), cur), pl.BlockSpec((n_heads, WINDOW), lambda b, n: (0, b * nb + n))],
        out_shape=[jax.ShapeDtypeStruct((t, d), BF16), jax.ShapeDtypeStruct((n_heads, t), F32)],
        scratch_shapes=[pltpu.VMEM((qw, WINDOW), F32), pltpu.VMEM((2 * kvw, WINDOW), F32), pltpu.VMEM((d, WINDOW), F32)],
        compiler_params=_params(("parallel", "arbitrary")),
    )(sinks, qkv, qkv, cos_t, sin_t, cos_t, sin_t)


def _attn_bwd(qkv, o, lse, do, sinks, cos_t, sin_t, bsz, seq, name, hp):
    t, qw = qkv.shape
    d = qw * 2 // 3
    kvw = d // GROUP
    n_heads, n_kv = d // HEAD_DIM, kvw // HEAD_DIM
    nb = seq // WINDOW
    scale = HEAD_DIM ** -0.5

    def body(sink_ref, xc_ref, xp_ref, oc_ref, doc_ref, lc_ref, cc_ref, sc_ref, cp_ref, sp_ref,
             dx_ref, db_ref, dsk_ref, xt_ref, pt_ref, otc_ref, dtc_ref, gt_ref, carry_ref):
        b, n = pl.program_id(0), pl.program_id(1)
        live = n < nb
        xt_ref[...] = xc_ref[...].T
        pt_ref[...] = xp_ref[:, d:].T
        otc_ref[...] = oc_ref[...].astype(F32).T
        dtc_ref[...] = doc_ref[...].T
        cos_c, sin_c, cos_p, sin_p = cc_ref[...], sc_ref[...], cp_ref[...], sp_ref[...]
        cos_g, sin_g = _lanes(cos_c, hp), _lanes(sin_c, hp)
        kj, qi = _key_query(hp)
        valid_c = jnp.logical_and(kj <= qi, live)
        valid_p = jnp.logical_and(kj > qi, jnp.logical_and(n > 0, live))

        @pl.when(jnp.logical_and(b == 0, n == 0))
        def _():
            db_ref[...] = jnp.zeros_like(db_ref)
            dsk_ref[...] = jnp.zeros_like(dsk_ref)

        @pl.when(n == 0)
        def _():
            carry_ref[...] = jnp.zeros_like(carry_ref)

        for j in range(n_kv):
            ko = j * HEAD_DIM
            k_rows = slice(d + ko, d + ko + HEAD_DIM)
            v_rows = slice(d + kvw + ko, d + kvw + ko + HEAD_DIM)
            kc = _rope(xt_ref[k_rows, :], cos_c, sin_c).astype(BF16)
            kp = _rope(pt_ref[ko:ko + HEAD_DIM, :], cos_p, sin_p).astype(BF16)
            vc = xt_ref[v_rows, :].astype(BF16)
            vp = pt_ref[kvw + ko:kvw + ko + HEAD_DIM, :].astype(BF16)
            dk_c = jnp.zeros((HEAD_DIM, WINDOW), F32)
            dv_c = jnp.zeros((HEAD_DIM, WINDOW), F32)
            dk_p = jnp.zeros((HEAD_DIM, WINDOW), F32)
            dv_p = jnp.zeros((HEAD_DIM, WINDOW), F32)
            for h0 in range(j * GROUP, (j + 1) * GROUP, hp):
                q = _rope(_heads(xt_ref, h0, hp), cos_g, sin_g).astype(BF16)
                do_g = _heads(dtc_ref, h0, hp)
                do_b = do_g.astype(BF16)
                lse_g = _head_rows(lc_ref, h0, hp)
                delta = jnp.sum(_heads(otc_ref, h0, hp) * do_g, axis=0, keepdims=True)
                p_c = jnp.exp(jnp.where(valid_c, _tn(kc, q) * scale, NEG) - lse_g)
                p_p = jnp.exp(jnp.where(valid_p, _tn(kp, q) * scale, NEG) - lse_g)
                ds_c = (p_c * (_tn(vc, do_b) - delta)).astype(BF16)
                ds_p = (p_p * (_tn(vp, do_b) - delta)).astype(BF16)
                dq = _unrope((_nn(kc, ds_c) + _nn(kp, ds_p)) * scale, cos_g, sin_g)
                dsk = jnp.where(live, -jnp.exp(_head_sinks(sink_ref, h0, hp) - lse_g) * delta, 0.0)
                for g in range(hp):
                    rows = slice((h0 + g) * HEAD_DIM, (h0 + g + 1) * HEAD_DIM)
                    gt_ref[rows, :] = carry_ref[rows, :]
                    carry_ref[rows, :] = dq[:, g * WINDOW:(g + 1) * WINDOW]
                    dsk_ref[h0 + g:h0 + g + 1, :] += dsk[:, g * WINDOW:(g + 1) * WINDOW]
                dv_c += _nt(do_b, p_c.astype(BF16))
                dk_c += _nt(q, ds_c)
                dv_p += _nt(do_b, p_p.astype(BF16))
                dk_p += _nt(q, ds_p)
            gt_ref[k_rows, :] = _unrope((carry_ref[k_rows, :] + dk_p) * scale, cos_p, sin_p)
            gt_ref[v_rows, :] = carry_ref[v_rows, :] + dv_p
            carry_ref[k_rows, :] = dk_c
            carry_ref[v_rows, :] = dv_c
        dx = gt_ref[...].T
        dx_ref[...] = dx.astype(BF16)
        db_ref[...] += _fold8(dx)

    cur = lambda b, n: (b * nb + jnp.minimum(n, nb - 1), 0)
    prev = lambda b, n: (b * nb + jnp.maximum(jnp.minimum(n, nb - 1) - 1, 0), 0)
    done = lambda b, n: (b * nb + jnp.maximum(n - 1, 0), 0)
    stat_c = pl.BlockSpec((n_heads, WINDOW), lambda b, n: (0, b * nb + jnp.minimum(n, nb - 1)))
    tab_c = pl.BlockSpec((HEAD_DIM, WINDOW), lambda b, n: (0, jnp.minimum(n, nb - 1)))
    tab_p = pl.BlockSpec((HEAD_DIM, WINDOW), lambda b, n: (0, jnp.maximum(n - 1, 0)))
    return pl.pallas_call(
        body,
        name=name,
        grid=(bsz, nb + 1),
        in_specs=[SMEM, pl.BlockSpec((WINDOW, qw), cur), pl.BlockSpec((WINDOW, qw), prev),
                  pl.BlockSpec((WINDOW, d), cur), pl.BlockSpec((WINDOW, d), cur), stat_c, tab_c, tab_c, tab_p, tab_p],
        out_specs=[pl.BlockSpec((WINDOW, qw), done), pl.BlockSpec((8, qw), lambda b, n: (0, 0)),
                   pl.BlockSpec((n_heads, WINDOW), lambda b, n: (0, 0))],
        out_shape=[jax.ShapeDtypeStruct((t, qw), BF16), jax.ShapeDtypeStruct((8, qw), F32),
                   jax.ShapeDtypeStruct((n_heads, WINDOW), F32)],
        scratch_shapes=[pltpu.VMEM((qw, WINDOW), F32), pltpu.VMEM((2 * kvw, WINDOW), F32), pltpu.VMEM((d, WINDOW), F32),
                        pltpu.VMEM((d, WINDOW), F32), pltpu.VMEM((qw, WINDOW), F32), pltpu.VMEM((qw, WINDOW), F32)],
        compiler_params=_params(("arbitrary", "arbitrary")),
    )(sinks, qkv, qkv, o, do, lse, cos_t, sin_t, cos_t, sin_t)


def _place():
    return lax.axis_index("x"), lax.axis_index("y"), lax.axis_index("c")


def _other_chips(x, y):
    return [(1 - x, y), (x, 1 - y), (1 - x, 1 - y)]


def _place_shard(w, axis, q, name):
    ly, k, n = w.shape
    tr = _pick(k, (256, 128, 64, 32, 16, 8))
    steps = k // tr
    shape = (ly, k * N_CHIPS, n) if axis == 1 else (ly, k, n * N_CHIPS)
    if axis == 1:
        out_spec = pl.BlockSpec((None, tr, n), lambda l, i, q_ref: (l, q_ref[0] * steps + i, 0))
    else:
        out_spec = pl.BlockSpec((None, tr, n), lambda l, i, q_ref: (l, i, q_ref[0]))

    def body(q_ref, w_ref, o_ref):
        del q_ref
        o_ref[...] = w_ref[...].astype(BF16)

    return pl.pallas_call(
        body,
        name=name,
        grid_spec=pltpu.PrefetchScalarGridSpec(
            num_scalar_prefetch=1, grid=(ly, steps),
            in_specs=[pl.BlockSpec((None, tr, n), lambda l, i, q_ref: (l, i, 0))], out_specs=out_spec),
        out_shape=jax.ShapeDtypeStruct(shape, BF16),
        compiler_params=_params(("parallel", "parallel")),
    )(q, w)


def _half_block(ref, axis, layer, px, py, pc):
    blk = 2 * px + py
    if axis == 1:
        rows = ref.shape[1] // (2 * N_CHIPS)
        return ref.at[layer, pl.ds(pl.multiple_of((2 * blk + pc) * rows, 8), rows), :]
    rows, width = ref.shape[1] // 2, ref.shape[2] // N_CHIPS
    return ref.at[layer, pl.ds(pl.multiple_of(pc * rows, 8), rows), pl.ds(pl.multiple_of(blk * width, 128), width)]


def _gather_copy(refs, axes, pieces, send_sems, recv_sems, p, k, stage, whose):
    x, y, c = _place()
    chip = _other_chips(x, y)[k]
    i, layer = pieces[p]
    if stage == 0:
        origin = (x, y, c) if whose == "mine" else (*chip, c)
        to = (*chip, c)
    else:
        origin = (*chip, c) if whose == "mine" else (*chip, 1 - c)
        to = (x, y, 1 - c)
    blk = _half_block(refs[i], axes[i], layer, *origin)
    return pltpu.make_async_remote_copy(src_ref=blk, dst_ref=blk, send_sem=send_sems.at[p * 3 + k],
                                        recv_sem=recv_sems.at[p * 3 + k], device_id=to, device_id_type=MESH)


HBM_SPEC = pl.BlockSpec(memory_space=pltpu.HBM)
SEM_SPEC = pl.BlockSpec(memory_space=pltpu.SEMAPHORE)


def _gather_start(fulls, axes, pieces, stage, name):
    n, m = len(fulls), 3 * len(pieces)

    def body(*refs):
        src = refs[:n]
        send_sems, recv_sems = refs[2 * n], refs[2 * n + 1]
        for p in range(len(pieces)):
            for k in range(3):
                _gather_copy(src, axes, pieces, send_sems, recv_sems, p, k, stage, "mine").start()

    out = pl.pallas_call(
        body,
        name=name,
        in_specs=[HBM_SPEC] * n,
        out_specs=[HBM_SPEC] * n + [SEM_SPEC, SEM_SPEC],
        out_shape=[pltpu.HBM(f.shape, f.dtype) for f in fulls] + [pltpu.SemaphoreType.DMA((m,)), pltpu.SemaphoreType.DMA((m,))],
        input_output_aliases={i: i for i in range(n)},
        compiler_params=pltpu.CompilerParams(has_side_effects=pltpu.SideEffectType.DATAFLOW_SIDE_EFFECTING),
    )(*[pltpu.with_memory_space_constraint(f, pltpu.HBM) for f in fulls])
    return list(out[:n]), out[n], out[n + 1]


def _gather_wait(fulls, send_sems, recv_sems, after, axes, pieces, stage, name):
    n = len(fulls)

    def body(*refs):
        src = refs[:n]
        s_sems, r_sems = refs[n], refs[n + 1]
        for p in range(len(pieces)):
            for k in range(3):
                _gather_copy(src, axes, pieces, s_sems, r_sems, p, k, stage, "mine").wait_send()
                _gather_copy(src, axes, pieces, s_sems, r_sems, p, k, stage, "theirs").wait_recv()

    out = pl.pallas_call(
        body,
        name=name,
        in_specs=[HBM_SPEC] * n + [SEM_SPEC, SEM_SPEC, ANY],
        out_specs=[HBM_SPEC] * n,
        out_shape=[pltpu.HBM(f.shape, f.dtype) for f in fulls],
        input_output_aliases={i: i for i in range(n)},
        compiler_params=pltpu.CompilerParams(has_side_effects=pltpu.SideEffectType.DATAFLOW_SIDE_EFFECTING),
    )(*fulls, send_sems, recv_sems, after)
    return list(out)


def _half_shape(kind, shape):
    if kind == "col":
        return (shape[0] // 2, shape[1])
    return (N_CHIPS, shape[1] // 2, shape[2])


def _half_of(kind, ref, h):
    if kind == "col":
        r = ref.shape[0] // 2
        return ref.at[pl.ds(pl.multiple_of(h * r, 8), r), :]
    r = ref.shape[1] // 2
    return ref.at[:, pl.ds(pl.multiple_of(h * r, 8), r), :]


def _pair_copy(src, dst, kinds, send_sems, recv_sems, i):
    x, y, c = _place()
    return pltpu.make_async_remote_copy(src_ref=_half_of(kinds[i], src[i], 1 - c), dst_ref=dst[i], send_sem=send_sems.at[i],
                                        recv_sem=recv_sems.at[i], device_id=(x, y, 1 - c), device_id_type=MESH)


def _pair_start(grads, kinds, token, name):
    n = len(grads)
    lands = [pltpu.HBM(_half_shape(kd, g.shape), g.dtype) for g, kd in zip(grads, kinds)]

    def body(*refs):
        src, dst = refs[:n], refs[2 * n + 2:3 * n + 2]
        send_sems, recv_sems = refs[3 * n + 2], refs[3 * n + 3]
        for i in range(n):
            _pair_copy(src, dst, kinds, send_sems, recv_sems, i).start()

    arrays = list(grads) + [token]
    out = pl.pallas_call(
        body,
        name=name,
        in_specs=[HBM_SPEC] * (n + 1),
        out_specs=[HBM_SPEC] * (2 * n + 1) + [SEM_SPEC, SEM_SPEC],
        out_shape=[pltpu.HBM(a.shape, a.dtype) for a in arrays] + lands + [pltpu.SemaphoreType.DMA((n,)), pltpu.SemaphoreType.DMA((n,))],
        input_output_aliases={i: i for i in range(n + 1)},
        compiler_params=pltpu.CompilerParams(has_side_effects=pltpu.SideEffectType.DATAFLOW_SIDE_EFFECTING),
    )(*[pltpu.with_memory_space_constraint(a, pltpu.HBM) for a in arrays])
    return list(out[:n]), out[n], list(out[n + 1:2 * n + 1]), out[2 * n + 1], out[2 * n + 2]


def _pair_wait(grads, kinds, token, lands, send_sems, recv_sems, name):
    n = len(grads)

    def body(*refs):
        src, dst = refs[:n], refs[n + 1:2 * n + 1]
        s_sems, r_sems = refs[2 * n + 1], refs[2 * n + 2]
        for i in range(n):
            cp = _pair_copy(src, dst, kinds, s_sems, r_sems, i)
            cp.wait_send()
            cp.wait_recv()

    arrays = list(grads) + [token] + list(lands)
    out = pl.pallas_call(
        body,
        name=name,
        in_specs=[HBM_SPEC] * (2 * n + 1) + [SEM_SPEC, SEM_SPEC],
        out_specs=[HBM_SPEC] * (2 * n + 1),
        out_shape=[pltpu.HBM(a.shape, a.dtype) for a in arrays],
        input_output_aliases={i: i for i in range(2 * n + 1)},
        compiler_params=pltpu.CompilerParams(has_side_effects=pltpu.SideEffectType.DATAFLOW_SIDE_EFFECTING),
    )(*arrays, send_sems, recv_sems)
    return list(out[:n]), out[n], list(out[n + 1:])


def _pair_sum(grad, recv, kind, c, own, layer, name):
    r, cols = own.shape[2:]
    if kind == "col":
        tr = _pick(r, (256, 128, 64, 32, 16, 8))
        steps = r // tr
        grid = (N_CHIPS, steps)
        g_spec = pl.BlockSpec((tr, cols), lambda s, i, c_ref: (c_ref[0] * steps + i, s))
        r_spec = pl.BlockSpec((tr, cols), lambda s, i, c_ref: (i, s))
        o_spec = pl.BlockSpec((None, None, tr, cols), lambda s, i, c_ref: (layer, s, i, 0))
        g_in = grad
    else:
        grid = (N_CHIPS, 1)
        g_spec = pl.BlockSpec((None, None, r, cols), lambda s, i, c_ref: (s, c_ref[0], 0, 0))
        r_spec = pl.BlockSpec((None, r, cols), lambda s, i, c_ref: (s, 0, 0))
        o_spec = pl.BlockSpec((None, None, r, cols), lambda s, i, c_ref: (layer, s, 0, 0))
        g_in = grad.reshape(N_CHIPS, 2, r, cols)

    def body(c_ref, g_ref, r_ref, own_ref, o_ref):
        del c_ref, own_ref
        o_ref[...] = (g_ref[...].astype(F32) + r_ref[...].astype(F32)).astype(o_ref.dtype)

    return pl.pallas_call(
        body,
        name=name,
        grid_spec=pltpu.PrefetchScalarGridSpec(num_scalar_prefetch=1, grid=grid, in_specs=[g_spec, r_spec, ANY], out_specs=o_spec),
        out_shape=jax.ShapeDtypeStruct(own.shape, own.dtype),
        input_output_aliases={3: 0},
        compiler_params=_params(("parallel", "parallel")),
    )(c, g_in, recv, own)


def _scatter_copy(own, mine, sib, pieces, send_sems, recv_sems, p, k, stage, whose):
    x, y, c = _place()
    q = 2 * x + y
    i, layer = pieces[p]
    per = 4 if stage == 0 else 3
    if k == 3:
        src, dst, to = own[i].at[layer, q], sib[i].at[layer, q], (x, y, 1 - c)
    else:
        chip = _other_chips(x, y)[k]
        slot = 2 * chip[0] + chip[1]
        if stage == 0:
            to = (*chip, c)
            src, dst = (own[i].at[layer, slot], mine[i].at[layer, q]) if whose == "mine" else (own[i].at[layer, q], mine[i].at[layer, slot])
        else:
            to = (x, y, 1 - c)
            src, dst = mine[i].at[layer, slot], sib[i].at[layer, slot]
    return pltpu.make_async_remote_copy(src_ref=src, dst_ref=dst, send_sem=send_sems.at[p * per + k],
                                        recv_sem=recv_sems.at[p * per + k], device_id=to, device_id_type=MESH)


def _scatter_start(own, mine, sib, token, pieces, stage, name):
    n = len(own)
    per = 4 if stage == 0 else 3
    m = per * len(pieces)
    n_arr = 3 * n + 1

    def body(*refs):
        o, mi, si = refs[:n], refs[n:2 * n], refs[2 * n:3 * n]
        send_sems, recv_sems = refs[2 * n_arr], refs[2 * n_arr + 1]
        for p in range(len(pieces)):
            for k in range(per):
                _scatter_copy(o, mi, si, pieces, send_sems, recv_sems, p, k, stage, "mine").start()

    arrays = list(own) + list(mine) + list(sib) + [token]
    out = pl.pallas_call(
        body,
        name=name,
        in_specs=[HBM_SPEC] * n_arr,
        out_specs=[HBM_SPEC] * n_arr + [SEM_SPEC, SEM_SPEC],
        out_shape=[pltpu.HBM(a.shape, a.dtype) for a in arrays] + [pltpu.SemaphoreType.DMA((m,)), pltpu.SemaphoreType.DMA((m,))],
        input_output_aliases={i: i for i in range(n_arr)},
        compiler_params=pltpu.CompilerParams(has_side_effects=pltpu.SideEffectType.DATAFLOW_SIDE_EFFECTING),
    )(*[pltpu.with_memory_space_constraint(a, pltpu.HBM) for a in arrays])
    return list(out[:n]), list(out[n:2 * n]), list(out[2 * n:3 * n]), out[3 * n], out[n_arr], out[n_arr + 1]


def _scatter_wait(own, mine, sib, token, send_sems, recv_sems, pieces, stage, name):
    n = len(own)
    per = 4 if stage == 0 else 3
    n_arr = 3 * n + 1

    def body(*refs):
        o, mi, si = refs[:n], refs[n:2 * n], refs[2 * n:3 * n]
        s_sems, r_sems = refs[n_arr], refs[n_arr + 1]
        for p in range(len(pieces)):
            for k in range(per):
                _scatter_copy(o, mi, si, pieces, s_sems, r_sems, p, k, stage, "mine").wait_send()
                _scatter_copy(o, mi, si, pieces, s_sems, r_sems, p, k, stage, "theirs").wait_recv()

    arrays = list(own) + list(mine) + list(sib) + [token]
    out = pl.pallas_call(
        body,
        name=name,
        in_specs=[HBM_SPEC] * n_arr + [SEM_SPEC, SEM_SPEC],
        out_specs=[HBM_SPEC] * n_arr,
        out_shape=[pltpu.HBM(a.shape, a.dtype) for a in arrays],
        input_output_aliases={i: i for i in range(n_arr)},
        compiler_params=pltpu.CompilerParams(has_side_effects=pltpu.SideEffectType.DATAFLOW_SIDE_EFFECTING),
    )(*arrays, send_sems, recv_sems)
    return list(out[:n]), list(out[n:2 * n]), list(out[2 * n:3 * n]), out[3 * n]


def _reduce_adamw(own, mine, sib, w, m, v, qc, name):
    ly, _, r, cols = mine.shape
    tr = _pick(r, (128, 64, 32, 16, 8))
    steps = r // tr
    c1 = 1.0 - ADAM_B1 ** ADAM_STEP
    c2 = 1.0 - ADAM_B2 ** ADAM_STEP

    def body(qc_ref, own_ref, mine_ref, sib_ref, w_ref, m_ref, v_ref, g_ref, d_ref, nm_ref, nv_ref):
        q = qc_ref[0]
        mine_sum = sib_sum = None
        for s in range(N_CHIPS):
            a = jnp.where(q == s, own_ref[...], mine_ref[s]).astype(F32)
            b = sib_ref[s].astype(F32)
            mine_sum = a if s == 0 else mine_sum + a
            sib_sum = b if s == 0 else sib_sum + b
        gv = jnp.where(pl.program_id(1) == qc_ref[1], mine_sum, sib_sum)
        nm = ADAM_B1 * m_ref[...] + (1.0 - ADAM_B1) * gv
        nv = ADAM_B2 * v_ref[...] + (1.0 - ADAM_B2) * (gv * gv)
        g_ref[...] = gv
        d_ref[...] = -ADAM_LR * ((nm / c1) / (jnp.sqrt(nv / c2) + ADAM_EPS) + ADAM_WD * w_ref[...])
        nm_ref[...] = nm
        nv_ref[...] = nv

    def mine_rows(h, i, qc_ref):
        return jnp.where(h == qc_ref[1], i, 0)

    def sib_rows(h, i, qc_ref):
        return jnp.where(h == qc_ref[1], 0, i)

    own_spec = pl.BlockSpec((None, None, tr, cols), lambda l, h, i, qc_ref: (l, qc_ref[0], mine_rows(h, i, qc_ref), 0))
    mine_spec = pl.BlockSpec((None, N_CHIPS, tr, cols), lambda l, h, i, qc_ref: (l, 0, mine_rows(h, i, qc_ref), 0))
    sib_spec = pl.BlockSpec((None, N_CHIPS, tr, cols), lambda l, h, i, qc_ref: (l, 0, sib_rows(h, i, qc_ref), 0))
    spec = pl.BlockSpec((None, tr, cols), lambda l, h, i, qc_ref: (l, h * steps + i, 0))
    shp = jax.ShapeDtypeStruct(w.shape, F32)
    return pl.pallas_call(
        body,
        name=name,
        grid_spec=pltpu.PrefetchScalarGridSpec(
            num_scalar_prefetch=1, grid=(ly, N_CORES, steps),
            in_specs=[own_spec, mine_spec, sib_spec, spec, spec, spec], out_specs=[spec] * 4),
        out_shape=[shp] * 4,
        compiler_params=_params(("parallel", "parallel", "parallel")),
    )(qc, own, mine, sib, w, m, v)


def _allreduce_small(v, name):
    r, w = v.shape

    def body(v_ref, o_ref, buf_ref, send_sems, recv_sems):
        x, y, c = _place()
        me = 4 * x + 2 * y + c

        def peer(k):
            return x ^ (k >> 2), y ^ ((k >> 1) & 1), c ^ (k & 1)

        def remote(k, slot):
            return pltpu.make_async_remote_copy(
                src_ref=v_ref, dst_ref=buf_ref.at[slot], send_sem=send_sems.at[k - 1], recv_sem=recv_sems.at[k - 1],
                device_id=peer(k), device_id_type=MESH)

        sends = [remote(k, me) for k in range(1, N_DEV)]
        for cp in sends:
            cp.start()
        buf_ref[me] = v_ref[...]
        for k in range(1, N_DEV):
            px, py, pc = peer(k)
            remote(k, 4 * px + 2 * py + pc).wait_recv()
        for cp in sends:
            cp.wait_send()
        acc = buf_ref[0]
        for dev in range(1, N_DEV):
            acc = acc + buf_ref[dev]
        o_ref[...] = acc

    vm = pl.BlockSpec(memory_space=pltpu.VMEM)
    return pl.pallas_call(
        body,
        name=name,
        in_specs=[vm],
        out_specs=vm,
        out_shape=jax.ShapeDtypeStruct((r, w), F32),
        scratch_shapes=[pltpu.VMEM((N_DEV, r, w), F32), pltpu.SemaphoreType.DMA((N_DEV - 1,)), pltpu.SemaphoreType.DMA((N_DEV - 1,))],
        compiler_params=pltpu.CompilerParams(vmem_limit_bytes=VMEM_LIMIT_BYTES),
    )(v)


def _adamw(w, g, m, v, name):
    ly, r, c = w.shape
    tr = _pick(r, (256, 128, 64, 32, 16, 8))
    c1 = 1.0 - ADAM_B1 ** ADAM_STEP
    c2 = 1.0 - ADAM_B2 ** ADAM_STEP

    def body(w_ref, g_ref, m_ref, v_ref, d_ref, nm_ref, nv_ref):
        gv = g_ref[...]
        nm = ADAM_B1 * m_ref[...] + (1.0 - ADAM_B1) * gv
        nv = ADAM_B2 * v_ref[...] + (1.0 - ADAM_B2) * (gv * gv)
        d_ref[...] = -ADAM_LR * ((nm / c1) / (jnp.sqrt(nv / c2) + ADAM_EPS) + ADAM_WD * w_ref[...])
        nm_ref[...] = nm
        nv_ref[...] = nv

    spec = pl.BlockSpec((None, tr, c), lambda l, i: (l, i, 0))
    shp = jax.ShapeDtypeStruct((ly, r, c), F32)
    return pl.pallas_call(
        body,
        name=name,
        grid=(ly, r // tr),
        in_specs=[spec] * 4,
        out_specs=[spec] * 3,
        out_shape=[shp] * 3,
        compiler_params=_params(("parallel", "parallel")),
    )(w, g, m, v)


def _rope_tables(seq):
    pos = jnp.arange(seq, dtype=F32)
    inv_freq = 1.0 / (ROPE_THETA ** (jnp.arange(0, HEAD_DIM, 2, dtype=F32) / HEAD_DIM))
    ang = (pos[:, None] * inv_freq[None, :]).T
    cos, sin = jnp.cos(ang), jnp.sin(ang)
    return jnp.concatenate([cos, cos], axis=0), jnp.concatenate([-sin, sin], axis=0)


def _pack(vs, fill=0.0):
    p = jnp.concatenate([v.reshape(-1) for v in vs])
    size = -(-p.shape[0] // 8192) * 8192
    return jnp.pad(p, (0, size - p.shape[0]), constant_values=fill).reshape(-1, 1024)


def _unpack(p, like):
    p = p.reshape(-1)
    out, o = [], 0
    for v in like:
        n = int(math.prod(v.shape))
        out.append(p[o:o + n].reshape(v.shape))
        o += n
    return out


def kernel(x, norm_mix, norm_ffn, norm_final, conv_w_in, conv_w_conv, conv_w_out, attn_w_qkv, attn_b_qkv, attn_sinks, attn_w_o, attn_b_o, ffn_w_in, ffn_w_conv, ffn_w_down, loss_target, m_norm_mix, m_norm_ffn, m_norm_final, m_conv_w_in, m_conv_w_conv, m_conv_w_out, m_attn_w_qkv, m_attn_b_qkv, m_attn_sinks, m_attn_w_o, m_attn_b_o, m_ffn_w_in, m_ffn_w_conv, m_ffn_w_down, v_norm_mix, v_norm_ffn, v_norm_final, v_conv_w_in, v_conv_w_conv, v_conv_w_out, v_attn_w_qkv, v_attn_b_qkv, v_attn_sinks, v_attn_w_o, v_attn_b_o, v_ffn_w_in, v_ffn_w_conv, v_ffn_w_down):
    bsz, seq, d = x.shape
    t = bsz * seq
    depth = norm_mix.shape[0]
    n_conv, n_attn = conv_w_in.shape[0], attn_w_qkv.shape[0]
    xq, yq, cq = _place()
    q = 2 * xq + yq

    big = [conv_w_in, conv_w_out, attn_w_qkv, attn_w_o, ffn_w_in, ffn_w_down]
    axes = [2, 1, 2, 1, 2, 1]
    q_arr = q.astype(jnp.int32).reshape(1)
    c_arr = cq.astype(jnp.int32).reshape(1)
    weights = [None] * 6
    sems = {}

    def place(n):
        weights[n] = _place_shard(big[n], axes[n], q_arr, f"place_shard{n}")

    def pieces_of(i):
        return [(0, i // 2), (1, i // 2), (4, i), (5, i)] if i % 2 == 0 else [(2, i // 2), (3, i // 2), (4, i), (5, i)]

    def subset(pieces):
        ts = [ti for ti in range(6) if weights[ti] is not None]
        return ts, [(ts.index(ti), l) for ti, l in pieces]

    def fetch(pieces, stage, tag):
        ts, local = subset(pieces)
        out, sems[tag, 0], sems[tag, 1] = _gather_start([weights[ti] for ti in ts], [axes[ti] for ti in ts], local, stage,
                                                        f"gather_{'ici' if stage == 0 else 'pass'}_start{tag}")
        for n, ti in enumerate(ts):
            weights[ti] = out[n]

    def settle(pieces, stage, tag, after):
        ts, local = subset(pieces)
        out = _gather_wait([weights[ti] for ti in ts], sems[tag, 0], sems[tag, 1], after, [axes[ti] for ti in ts], local, stage,
                           f"gather_{'ici' if stage == 0 else 'pass'}_wait{tag}")
        for n, ti in enumerate(ts):
            weights[ti] = out[n]

    xs = x.reshape(t, d)
    place(0)
    place(1)
    fetch(pieces_of(0)[:2], 0, "m0")
    for n in (2, 3, 4, 5):
        place(n)
    h = _rms_fwd(xs, norm_mix[0:1], "norm_mix_fwd0")
    settle(pieces_of(0)[:2], 0, "m0", h)
    fetch(pieces_of(0)[:2], 1, "m0")

    small_cols = [conv_w_conv, attn_b_qkv, attn_b_o, ffn_w_conv]

    def placed(v):
        width = v.shape[-1]
        full = jnp.zeros(v.shape[:-1] + (N_CHIPS * width,), F32)
        return lax.dynamic_update_slice_in_dim(full, v * (1.0 / N_CORES), q * width, axis=v.ndim - 1)

    full_cols = [placed(v) for v in small_cols]
    small_full = _allreduce_small(_pack(full_cols), "gather_small")
    wc_conv, b_qkv, b_o, wf_conv = _unpack(small_full, full_cols)
    settle(pieces_of(0)[:2], 1, "m0", small_full)
    cos_t, sin_t = _rope_tables(seq)

    saved = []
    for i in range(depth):
        j = i // 2
        ahead = pieces_of(i + 1) if i + 1 < depth else None
        if i == 0:
            fetch(pieces_of(0)[2:], 0, "0")
        elif ahead:
            fetch(ahead, 0, str(i + 1))
        w_cin, w_cout, w_qkv, w_o, w_fin, w_fdown = weights
        g_ffn = norm_ffn[i:i + 1]
        if i % 2 == 0:
            pre = _mm(h, w_cin, "nn", BF16, layer=j, tm=1024, tn=768, tk=4096, name=f"conv_in_fwd{i}")
            mixed = _convgate_fwd(pre, wc_conv[j], seq, f"conv_gate_fwd{i}")
            if i == 0:
                settle(pieces_of(0)[2:], 0, "0", mixed)
                fetch(pieces_of(0)[2:], 1, "0")
                w_cin, w_cout, w_qkv, w_o, w_fin, w_fdown = weights
            x_mid, h2 = _mm(mixed, w_cout, "nn", F32, layer=j, residual=xs, norm_g=g_ffn, tm=512, tn=1024, tk=4096,
                            name=f"conv_out_fwd{i}")
            lse = None
        else:
            pre = _mm(h, w_qkv, "nn", F32, layer=j, bias=b_qkv[j:j + 1], tm=1024, tn=768, tk=4096, name=f"qkv_fwd{i}")
            mixed, lse = _attn_fwd(pre, attn_sinks[j], cos_t, sin_t, bsz, seq, f"attn_fwd{i}", hp=1)
            x_mid, h2 = _mm(mixed, w_o, "nn", F32, layer=j, bias=b_o[j:j + 1], residual=xs, norm_g=g_ffn, tm=512, tn=1024,
                            tk=4096, name=f"attn_out_fwd{i}")
        if i == 0:
            settle(pieces_of(0)[2:], 1, "0", h2)
            fetch(ahead, 0, "1")
        elif ahead:
            settle(ahead, 0, str(i + 1), x_mid)
            fetch(ahead, 1, str(i + 1))
        w_cin, w_cout, w_qkv, w_o, w_fin, w_fdown = weights
        gu = _mm(h2, w_fin, "nn", BF16, layer=i, n_outer=True, tm=1024, tn=2816, tk=4096, name=f"ffn_in_fwd{i}")
        if i == 0:
            settle(ahead, 0, "1", gu)
            fetch(ahead, 1, "1")
            w_cin, w_cout, w_qkv, w_o, w_fin, w_fdown = weights
        g_next = norm_mix[i + 1:i + 2] if i + 1 < depth else norm_final.reshape(1, d)
        x_next, act, h_next = _ffn_gate_down_fwd(gu, wf_conv[i], w_fdown, i, x_mid, g_next, seq, f"ffn_gate_down_fwd{i}")
        if ahead:
            settle(ahead, 1, str(i + 1), x_next)
        saved.append((xs, h, pre, mixed, lse, x_mid, h2, gu, act))
        xs, h = x_next, h_next
    w_cin, w_cout, w_qkv, w_o, w_fin, w_fdown = weights

    dx, dxb, sq, dg_final = _loss_head(xs, loss_target.reshape(t, d), norm_final.reshape(1, d), "loss_head")
    loss = lax.psum(0.5 * jnp.sum(sq) / d, ("x", "y", "c"))

    g_norm_mix, g_norm_ffn = [None] * depth, [None] * depth
    g_cin, g_cconv, g_cout = [None] * n_conv, [None] * n_conv, [None] * n_conv
    g_qkv, g_bqkv, g_sinks, g_o, g_bo = ([None] * n_attn for _ in range(5))
    g_fin, g_fconv, g_fdown = [None] * depth, [None] * depth, [None] * depth

    kinds6 = ["col", "row", "col", "row", "col", "row"]
    layers6 = [n_conv, n_conv, n_attn, n_attn, depth, depth]
    big_w = [conv_w_in, conv_w_out, attn_w_qkv, attn_w_o, ffn_w_in, ffn_w_down]

    def slot_stack(n):
        k, cols = big_w[n].shape[1], big_w[n].shape[2]
        r = k // 2
        return lax.empty((layers6[n], N_CHIPS, r, cols), BF16)

    own = [slot_stack(n) for n in range(6)]
    mine = [slot_stack(n) for n in range(6)]
    sib = [slot_stack(n) for n in range(6)]
    flight = {}

    def group(i, part):
        return f"{part}{i}", (pieces_of(i)[:2] if part == "m" else pieces_of(i)[2:])

    def scatter(grp, stage, action, token):
        tag, pieces = grp
        ts = [ti for ti, _ in pieces]
        local = [(n, l) for n, (_, l) in enumerate(pieces)]
        sub = ([own[ti] for ti in ts], [mine[ti] for ti in ts], [sib[ti] for ti in ts], token)
        label = f"grad_{'ici' if stage == 0 else 'pass'}_{action}_{tag}"
        if action == "start":
            o, mi, si, token, s_sems, r_sems = _scatter_start(*sub, local, stage, label)
            flight[tag] = (s_sems, r_sems)
        else:
            o, mi, si, token = _scatter_wait(*sub, *flight[tag], local, stage, label)
        for n, ti in enumerate(ts):
            own[ti], mine[ti], sib[ti] = o[n], mi[n], si[n]
        return token

    def pair_begin(grp, token):
        tag, pieces = grp
        grads = {0: g_cin, 1: g_cout, 2: g_qkv, 3: g_o, 4: g_fin, 5: g_fdown}
        parts, kinds = [], []
        for ti, l in pieces:
            g = grads[ti][l]
            parts.append(g if kinds6[ti] == "col" else g.reshape(N_CHIPS, g.shape[0] // N_CHIPS, g.shape[1]))
            kinds.append(kinds6[ti])
        parts, token, lands, s_sems, r_sems = _pair_start(parts, kinds, token, f"grad_pair_start_{tag}")
        flight["pair" + tag] = (parts, kinds, lands, s_sems, r_sems)
        return token

    def pair_finish(grp, token):
        tag, pieces = grp
        parts, kinds, lands, s_sems, r_sems = flight["pair" + tag]
        parts, token, recv = _pair_wait(parts, kinds, token, lands, s_sems, r_sems, f"grad_pair_wait_{tag}")
        for (ti, l), g, r in zip(pieces, parts, recv):
            own[ti] = _pair_sum(g, r, kinds6[ti], c_arr, own[ti], l, f"grad_pair_sum_{tag}_{ti}")
        return scatter(grp, 0, "start", token)

    for i in reversed(range(depth)):
        j = i // 2
        x_in, h, pre, mixed, lse, x_mid, h2, gu, act = saved[i]
        da = _mm(dxb, w_fdown, "nt", BF16, layer=i, n_outer=True, tm=1024, tn=2816, tk=4096, name=f"ffn_down_dx{i}")
        g_fdown[i] = _mm(act, dxb, "tn", BF16, tm=1408, tn=1024, tk=2048, name=f"ffn_down_dw{i}")
        dgu, dwc = _ffngate_bwd(gu, da, wf_conv[i], seq, f"ffn_gate_bwd{i}")
        g_fconv[i] = dwc[:3]
        g_fin[i] = _mm(h2, dgu, "tn", BF16, tm=1024, tn=1408, tk=2048, name=f"ffn_in_dw{i}")
        dgu = pair_begin(group(i, "f"), dgu)
        dx, dxb, dg, colsum = _rms_bwd(x_mid, dgu, w_fin, i, norm_ffn[i:i + 1], dx, f"ffn_in_dx_norm_bwd{i}")
        g_norm_ffn[i] = jnp.sum(dg, axis=0)
        if i + 1 < depth:
            dxb = scatter(group(i + 1, "m"), 1, "start", scatter(group(i + 1, "m"), 0, "wait", dxb))
            dxb = scatter(group(i + 1, "f"), 1, "wait", dxb)
        dxb = pair_finish(group(i, "f"), dxb)
        if i % 2 == 0:
            dmix = _mm(dxb, w_cout, "nt", BF16, layer=j, tm=512, tn=1024, tk=4096, name=f"conv_out_dx{i}")
            g_cout[j] = _mm(mixed, dxb, "tn", BF16, tm=1024, tn=1024, tk=2048, name=f"conv_out_dw{i}")
            dpre, dwc = _convgate_bwd(pre, dmix, wc_conv[j], seq, f"conv_gate_bwd{i}")
            g_cconv[j] = dwc[:3]
            g_cin[j] = _mm(h, dpre, "tn", BF16, tm=1024, tn=1536, tk=2048, name=f"conv_in_dw{i}")
            w_pre = w_cin
        else:
            g_bo[j] = jnp.sum(colsum, axis=0)
            dmix = _mm(dxb, w_o, "nt", F32, layer=j, tm=512, tn=1024, tk=4096, name=f"attn_out_dx{i}")
            g_o[j] = _mm(mixed, dxb, "tn", BF16, tm=1024, tn=1024, tk=2048, name=f"attn_out_dw{i}")
            dpre, dbias, dsk = _attn_bwd(pre, mixed, lse, dmix, attn_sinks[j], cos_t, sin_t, bsz, seq, f"attn_bwd{i}",
                                         hp=GROUP)
            g_bqkv[j] = jnp.sum(dbias, axis=0)
            g_sinks[j] = jnp.sum(dsk, axis=1)
            g_qkv[j] = _mm(h, dpre, "tn", BF16, tm=1024, tn=1536, tk=2048, name=f"qkv_dw{i}")
            w_pre = w_qkv
        dpre = pair_begin(group(i, "m"), dpre)
        dx, dxb, dg, _ = _rms_bwd(x_in, dpre, w_pre, j, norm_mix[i:i + 1], dx, f"mixer_in_dx_norm_bwd{i}")
        g_norm_mix[i] = jnp.sum(dg, axis=0)
        dxb = scatter(group(i, "f"), 1, "start", scatter(group(i, "f"), 0, "wait", dxb))
        if i + 1 < depth:
            dxb = scatter(group(i + 1, "m"), 1, "wait", dxb)
        dxb = pair_finish(group(i, "m"), dxb)
    grad_x = dx.reshape(bsz, seq, d)

    scatter(group(0, "f"), 1, "wait", dxb)
    big_m = [m_conv_w_in, m_conv_w_out, m_attn_w_qkv, m_attn_w_o, m_ffn_w_in, m_ffn_w_down]
    big_v = [v_conv_w_in, v_conv_w_out, v_attn_w_qkv, v_attn_w_o, v_ffn_w_in, v_ffn_w_down]
    big_names = ["conv_w_in", "conv_w_out", "attn_w_qkv", "attn_w_o", "ffn_w_in", "ffn_w_down"]
    qc_arr = jnp.stack([q, cq]).astype(jnp.int32)

    def adamw_of(n):
        return list(_reduce_adamw(own[n], mine[n], sib[n], big_w[n], big_m[n], big_v[n], qc_arr, f"adamw_{big_names[n]}"))

    big_upd = [None] * 6
    for n in (2, 3, 4, 5):
        big_upd[n] = adamw_of(n)
    token = scatter(group(0, "m"), 1, "start", scatter(group(0, "m"), 0, "wait", big_upd[5][1]))
    big_upd[5][1] = scatter(group(0, "m"), 1, "wait", token)
    for n in (0, 1):
        big_upd[n] = adamw_of(n)

    small = [jnp.stack(g_norm_mix), jnp.stack(g_norm_ffn), jnp.sum(dg_final, axis=0), jnp.stack(g_cconv),
             jnp.stack(g_bqkv), jnp.stack(g_sinks), jnp.stack(g_bo), jnp.stack(g_fconv)]
    sg = _unpack(_allreduce_small(_pack(small), "grad_small_allreduce"), small)

    def my_cols(v, like):
        width = like.shape[-1]
        return lax.dynamic_slice_in_dim(v, q * width, width, axis=v.ndim - 1)

    small_w = [norm_mix, norm_ffn, norm_final, conv_w_conv, attn_b_qkv, attn_sinks, attn_b_o, ffn_w_conv]
    small_m = [m_norm_mix, m_norm_ffn, m_norm_final, m_conv_w_conv, m_attn_b_qkv, m_attn_sinks, m_attn_b_o, m_ffn_w_conv]
    small_v = [v_norm_mix, v_norm_ffn, v_norm_final, v_conv_w_conv, v_attn_b_qkv, v_attn_sinks, v_attn_b_o, v_ffn_w_conv]
    small_g = [sg[0], sg[1], sg[2], my_cols(sg[3], conv_w_conv), my_cols(sg[4], attn_b_qkv), sg[5],
               my_cols(sg[6], attn_b_o), my_cols(sg[7], ffn_w_conv)]

    upd = {nm: tuple(u[1:]) for nm, u in zip(big_names, big_upd)}
    sd, sm, sv = _adamw(_pack(small_w)[None], _pack(small_g)[None], _pack(small_m)[None], _pack(small_v, 1.0)[None],
                        "adamw_small")
    sd, sm, sv = _unpack(sd, small_w), _unpack(sm, small_w), _unpack(sv, small_w)
    names = ["norm_mix", "norm_ffn", "norm_final", "conv_w_in", "conv_w_conv", "conv_w_out", "attn_w_qkv", "attn_b_qkv",
             "attn_sinks", "attn_w_o", "attn_b_o", "ffn_w_in", "ffn_w_conv", "ffn_w_down"]
    small_names = ["norm_mix", "norm_ffn", "norm_final", "conv_w_conv", "attn_b_qkv", "attn_sinks", "attn_b_o", "ffn_w_conv"]
    grads = dict(zip(small_names, small_g))
    grads.update({nm: u[0] for nm, u in zip(big_names, big_upd)})
    for n, nm in enumerate(small_names):
        upd[nm] = (sd[n], sm[n], sv[n])
    return (loss, grad_x, *[grads[nm] for nm in names], *[upd[nm][0] for nm in names],
            *[upd[nm][1] for nm in names], *[upd[nm][2] for nm in names])
```
